```python
import math
import jax, jax.numpy as jnp
from jax import lax
import numpy as np

D_MODEL = 1024
BATCH = 8
SEQ = 4096
DEPTH = 2

HEAD_DIM = 64
NUM_BUCKETS = 32
MAX_DISTANCE = 128
N_BIAS_HEADS = 16
A_HEADS = 8
A_KV = 2
A_WINDOW = 128
B_HEADS = 8
B_KV = 2
CMP_BLOCK = 32
CMP_STRIDE = 16
CMP_HIDDEN = 256
SLC_BLOCK = 64
SLC_TOPN = 16
B_WINDOW = 512
C_HEADS = 16
IDX_HEADS = 8
IDX_DIM = 32
C_TOPK_MAX = 256
D_FF = 2816
N_EXPERTS = 8
TOP_K = 2
Q_BLOCK = 128
SLC_Q_CHUNK = 64
ALPHA = (2.0 * DEPTH) ** 0.25
BETA = (8.0 * DEPTH) ** -0.25
LN_EPS = 1e-5
NEG = -1e30
POS_BIG = 1e30

A_Q = A_HEADS * HEAD_DIM
A_KVW = A_KV * HEAD_DIM
B_Q = B_HEADS * HEAD_DIM
B_KVW = B_KV * HEAD_DIM
W_IN0_SIZES = [A_Q, A_KVW, A_KVW, B_Q, 6 * B_KVW, 3 * B_HEADS]
W_IN0_COLS = sum(W_IN0_SIZES)
W_IN1_SIZES = [C_HEADS * HEAD_DIM, HEAD_DIM, HEAD_DIM, IDX_HEADS * IDX_DIM, IDX_DIM, IDX_HEADS]
W_IN1_COLS = sum(W_IN1_SIZES)
CMP_IN = CMP_BLOCK * HEAD_DIM

kernel_name = 'hybrid_swa_nsa_dsa_deepnorm'

f32 = jnp.float32


def _split_cols(x, sizes):
    return jnp.split(x, np.cumsum(sizes)[:-1].tolist(), axis=-1)


def layer_norm(x, g, b):
    xf = x.astype(f32)
    mu = xf.mean(-1, keepdims=True)
    var = jnp.square(xf - mu).mean(-1, keepdims=True)
    return ((xf - mu) * lax.rsqrt(var + LN_EPS) * g + b).astype(x.dtype)


def t5_bucket(dist):
    n = jnp.maximum(dist, 0)
    max_exact = NUM_BUCKETS // 2
    nf = jnp.maximum(n, 1).astype(f32)
    large = max_exact + (jnp.log(nf / max_exact) / math.log(MAX_DISTANCE / max_exact)
                         * (NUM_BUCKETS - max_exact)).astype(jnp.int32)
    large = jnp.minimum(large, NUM_BUCKETS - 1)
    return jnp.where(n < max_exact, n, large)


def swiglu(x, w_gate, w_up, w_down):
    return (jax.nn.silu(x @ w_gate) * (x @ w_up)) @ w_down


def banded_gqa(q, k, v, table, window, sinks):
    Bn, S, G, R, dh = q.shape
    pad = -(-(window - 1) // Q_BLOCK) * Q_BLOCK
    kw = pad + Q_BLOCK
    kp = jnp.pad(k, ((0, 0), (pad, 0), (0, 0), (0, 0)))
    vp = jnp.pad(v, ((0, 0), (pad, 0), (0, 0), (0, 0)))
    ki = jnp.arange(kw)[None, :]
    dist = jnp.arange(Q_BLOCK)[:, None] + pad - ki
    band = (dist >= 0) & (dist < window)
    bias = jnp.transpose(table[t5_bucket(dist)], (2, 3, 0, 1))
    scale = dh ** -0.5

    def one_block(i):
        start = i * Q_BLOCK
        qb = lax.dynamic_slice_in_dim(q, start, Q_BLOCK, axis=1)
        kb = lax.dynamic_slice_in_dim(kp, start, kw, axis=1)
        vb = lax.dynamic_slice_in_dim(vp, start, kw, axis=1)
        s = jnp.einsum('bqgrd,bkgd->bgrqk', qb, kb).astype(f32) * scale + bias
        valid = band & (start - pad + ki >= 0)
        s = jnp.where(valid, s, NEG)
        if sinks is None:
            p = jax.nn.softmax(s, axis=-1)
        else:
            sk = sinks.astype(f32)[None, :, :, None, None]
            m = jnp.maximum(s.max(-1, keepdims=True), sk)
            e = jnp.exp(s - m)
            p = e / (e.sum(-1, keepdims=True) + jnp.exp(sk - m))
        return jnp.einsum('bgrqk,bkgd->bqgrd', p.astype(vb.dtype), vb)

    out = lax.map(one_block, jnp.arange(S // Q_BLOCK))
    return jnp.moveaxis(out, 0, 1).reshape(Bn, S, G, R, dh)


def nsa_compress(kv, pos, w1, w2):
    Bn, S, G, dh = kv.shape
    n_cmp = (S - CMP_BLOCK) // CMP_STRIDE + 1
    idx = jnp.arange(n_cmp)[:, None] * CMP_STRIDE + jnp.arange(CMP_BLOCK)[None, :]
    blocks = kv[:, idx] + pos[:, None, :]
    blocks = jnp.moveaxis(blocks, 3, 2).reshape(Bn, n_cmp, G, CMP_BLOCK * dh)
    return jax.nn.gelu(blocks @ w1) @ w2


def nsa_selected(q, k, v, sel, table):
    Bn, S, G, R, dh = q.shape
    n_keys = sel.shape[-1] * SLC_BLOCK
    kb = k.reshape(Bn, S // SLC_BLOCK, SLC_BLOCK, G, dh).transpose(0, 3, 1, 2, 4)
    vb = v.reshape(Bn, S // SLC_BLOCK, SLC_BLOCK, G, dh).transpose(0, 3, 1, 2, 4)
    tbg = table.transpose(1, 0, 2)
    bi = jnp.arange(Bn)[:, None, None, None]
    gi = jnp.arange(G)[None, :, None, None]
    offs = jnp.arange(SLC_BLOCK)
    scale = dh ** -0.5

    def one_chunk(i):
        start = i * SLC_Q_CHUNK
        qc = lax.dynamic_slice_in_dim(q, start, SLC_Q_CHUNK, axis=1)
        sc = lax.dynamic_slice_in_dim(sel, start, SLC_Q_CHUNK, axis=2)
        kg = kb[bi, gi, sc].reshape(Bn, G, SLC_Q_CHUNK, n_keys, dh)
        vg = vb[bi, gi, sc].reshape(Bn, G, SLC_Q_CHUNK, n_keys, dh)
        kpos = (sc[..., None] * SLC_BLOCK + offs).reshape(Bn, G, SLC_Q_CHUNK, n_keys)
        dist = (start + jnp.arange(SLC_Q_CHUNK))[:, None] - kpos
        valid = (dist >= 0)[:, :, :, None, :]
        bias = jnp.moveaxis(tbg[gi, t5_bucket(dist)], 4, 3)
        s = jnp.einsum('bqgrd,bgqkd->bgqrk', qc, kg).astype(f32) * scale + bias
        p = jax.nn.softmax(jnp.where(valid, s, NEG), axis=-1)
        return jnp.einsum('bgqrk,bgqkd->bqgrd', p.astype(vg.dtype), vg)

    out = lax.map(one_chunk, jnp.arange(S // SLC_Q_CHUNK))
    return jnp.moveaxis(out, 0, 1).reshape(Bn, S, G, R, dh)


def nsa_attention(q, k_cmp, v_cmp, k_slc, v_slc, k_win, v_win, gates, table,
                  cmpk_pos, cmpk_w1, cmpk_w2, cmpv_pos, cmpv_w1, cmpv_w2):
    Bn, S, G, R, dh = q.shape
    scale = dh ** -0.5
    tpos = jnp.arange(S)
    kc = nsa_compress(k_cmp, cmpk_pos, cmpk_w1, cmpk_w2)
    vc = nsa_compress(v_cmp, cmpv_pos, cmpv_w1, cmpv_w2)
    n_cmp = kc.shape[1]
    cmp_end = jnp.arange(n_cmp) * CMP_STRIDE + (CMP_BLOCK - 1)
    dist_c = tpos[:, None] - cmp_end[None, :]
    valid_c = dist_c >= 0
    bias_c = jnp.transpose(table[t5_bucket(dist_c)], (2, 3, 0, 1))
    s_c = jnp.einsum('bsgrd,bngd->bgrsn', q, kc).astype(f32) * scale + bias_c
    p_c = jax.nn.softmax(jnp.where(valid_c, s_c, NEG), axis=-1) * valid_c
    o_cmp = jnp.einsum('bgrsn,bngd->bsgrd', p_c.astype(vc.dtype), vc)
    n_slc = S // SLC_BLOCK
    n_sel = min(SLC_TOPN, n_slc)
    ci = np.arange(n_cmp)[:, None] * CMP_STRIDE
    sj = np.arange(n_slc)[None, :] * SLC_BLOCK
    overlap = np.clip(np.minimum(ci + CMP_BLOCK, sj + SLC_BLOCK) - np.maximum(ci, sj), 0, None) / CMP_BLOCK
    imp = jnp.einsum('bgrsn,nj->bgsj', p_c, jnp.asarray(overlap, dtype=f32))
    blk = jnp.arange(n_slc)
    cur = (tpos // SLC_BLOCK)[:, None]
    forced = (blk == 0) | (blk == cur) | (blk == cur - 1)
    adm = blk * SLC_BLOCK <= tpos[:, None]
    score = jnp.where(forced, POS_BIG, jnp.where(adm, imp, NEG))
    _, sel = lax.top_k(score, n_sel)
    o_slc = nsa_selected(q, k_slc, v_slc, sel, table)
    o_win = banded_gqa(q, k_win, v_win, table, B_WINDOW, None)
    g = jax.nn.sigmoid(gates.astype(f32))[..., None].astype(q.dtype)
    return g[:, :, 0] * o_cmp + g[:, :, 1] * o_slc + g[:, :, 2] * o_win


def dsa_attention(q, k, v, q_idx, k_idx, w_idx, table):
    Bn, S, H, dh = q.shape
    topk = min(C_TOPK_MAX, S // 4)
    scale = dh ** -0.5
    kpos = jnp.arange(S)
    bi = jnp.arange(Bn)[:, None, None]
    w_idx = w_idx.astype(f32)

    def one_block(i):
        start = i * Q_BLOCK
        qb = lax.dynamic_slice_in_dim(q, start, Q_BLOCK, axis=1)
        qib = lax.dynamic_slice_in_dim(q_idx, start, Q_BLOCK, axis=1)
        wib = lax.dynamic_slice_in_dim(w_idx, start, Q_BLOCK, axis=1)
        qpos = start + jnp.arange(Q_BLOCK)
        rel = jax.nn.relu(jnp.einsum('bqhd,bsd->bqhs', qib, k_idx).astype(f32))
        index_score = jnp.einsum('bqh,bqhs->bqs', wib, rel)
        index_score = jnp.where(kpos[None, :] <= qpos[:, None], index_score, NEG)
        _, sel = lax.top_k(index_score, topk)
        kg = k[bi, sel]
        vg = v[bi, sel]
        dist = qpos[:, None] - sel
        valid = (dist >= 0)[:, :, None, :]
        bias = jnp.moveaxis(table[t5_bucket(dist)], 3, 2)
        s = jnp.einsum('bqhd,bqkd->bqhk', qb, kg).astype(f32) * scale + bias
        p = jax.nn.softmax(jnp.where(valid, s, NEG), axis=-1)
        return jnp.einsum('bqhk,bqkd->bqhd', p.astype(vg.dtype), vg)

    out = lax.map(one_block, jnp.arange(S // Q_BLOCK))
    return jnp.moveaxis(out, 0, 1).reshape(Bn, S, H * dh)


def moe_swiglu(x, router, w_gate, w_up, w_down):
    xt = x.reshape(-1, x.shape[-1])
    logits = (xt @ router).astype(f32)
    top_v, top_i = lax.top_k(logits, TOP_K)
    gates = jax.nn.softmax(top_v, axis=-1)
    combine = jnp.sum(jax.nn.one_hot(top_i, N_EXPERTS, dtype=f32) * gates[..., None], axis=1)
    y = jnp.zeros_like(xt)
    for e in range(N_EXPERTS):
        y = y + combine[:, e:e + 1].astype(xt.dtype) * swiglu(xt, w_gate[e], w_up[e], w_down[e])
    return y.reshape(x.shape)


def even_layer(x, rel_bias, w_in, sinks, cmpk_pos, cmpk_w1, cmpk_w2, cmpv_pos, cmpv_w1, cmpv_w2,
               w_out, ln1_g, ln1_b, ffn_gate, ffn_up, ffn_down, ln2_g, ln2_b):
    Bn, S, _ = x.shape
    ra, rb = A_HEADS // A_KV, B_HEADS // B_KV
    qa, ka, va, qb, kvb, gb = _split_cols(x @ w_in, W_IN0_SIZES)
    table_a = rel_bias[:, :A_HEADS].reshape(NUM_BUCKETS, A_KV, ra)
    out_a = banded_gqa(qa.reshape(Bn, S, A_KV, ra, HEAD_DIM), ka.reshape(Bn, S, A_KV, HEAD_DIM),
                       va.reshape(Bn, S, A_KV, HEAD_DIM), table_a, A_WINDOW, sinks.reshape(A_KV, ra))
    kvb = kvb.reshape(Bn, S, 6, B_KV, HEAD_DIM)
    table_b = rel_bias[:, A_HEADS:A_HEADS + B_HEADS].reshape(NUM_BUCKETS, B_KV, rb)
    out_b = nsa_attention(qb.reshape(Bn, S, B_KV, rb, HEAD_DIM),
                          kvb[:, :, 0], kvb[:, :, 1], kvb[:, :, 2], kvb[:, :, 3], kvb[:, :, 4], kvb[:, :, 5],
                          gb.reshape(Bn, S, 3, B_KV, rb), table_b,
                          cmpk_pos, cmpk_w1, cmpk_w2, cmpv_pos, cmpv_w1, cmpv_w2)
    mix = jnp.concatenate([out_a.reshape(Bn, S, A_Q), out_b.reshape(Bn, S, B_Q)], axis=-1)
    x = layer_norm(ALPHA * x + mix @ w_out, ln1_g, ln1_b)
    return layer_norm(ALPHA * x + swiglu(x, ffn_gate, ffn_up, ffn_down), ln2_g, ln2_b)


def odd_layer(x, rel_bias, w_in, w_out, ln1_g, ln1_b, router, exp_gate, exp_up, exp_down, ln2_g, ln2_b):
    Bn, S, _ = x.shape
    q, k, v, qi, ki, wi = _split_cols(x @ w_in, W_IN1_SIZES)
    mix = dsa_attention(q.reshape(Bn, S, C_HEADS, HEAD_DIM), k, v,
                        qi.reshape(Bn, S, IDX_HEADS, IDX_DIM), ki, wi, rel_bias[:, :C_HEADS])
    x = layer_norm(ALPHA * x + mix @ w_out, ln1_g, ln1_b)
    return layer_norm(ALPHA * x + moe_swiglu(x, router, exp_gate, exp_up, exp_down), ln2_g, ln2_b)


def setup_inputs(seed: int = 0) -> dict:
    key = jax.random.key(seed)
    ks = jax.random.split(key, 28)

    def nrm(i, shape, scale):
        return jax.random.normal(ks[i], shape, dtype=jnp.float32) * scale

    d = D_MODEL
    return {
        'x': nrm(0, (BATCH, SEQ, d), 1.0),
        'rel_bias': nrm(1, (NUM_BUCKETS, N_BIAS_HEADS), 0.3),
        'l0_w_in': nrm(2, (d, W_IN0_COLS), d ** -0.5),
        'l0_sinks': nrm(3, (A_HEADS,), 0.5),
        'l0_cmpk_pos': nrm(4, (CMP_BLOCK, HEAD_DIM), 0.1),
        'l0_cmpk_w1': nrm(5, (CMP_IN, CMP_HIDDEN), CMP_IN ** -0.5),
        'l0_cmpk_w2': nrm(6, (CMP_HIDDEN, HEAD_DIM), CMP_HIDDEN ** -0.5),
        'l0_cmpv_pos': nrm(7, (CMP_BLOCK, HEAD_DIM), 0.1),
        'l0_cmpv_w1': nrm(8, (CMP_IN, CMP_HIDDEN), CMP_IN ** -0.5),
        'l0_cmpv_w2': nrm(9, (CMP_HIDDEN, HEAD_DIM), CMP_HIDDEN ** -0.5),
        'l0_w_out': nrm(10, (A_Q + B_Q, d), BETA * (A_Q + B_Q) ** -0.5),
        'l0_ln1_g': 1.0 + nrm(11, (d,), 0.01),
        'l0_ln1_b': nrm(12, (d,), 0.01),
        'l0_ffn_gate': nrm(13, (d, D_FF), d ** -0.5),
        'l0_ffn_up': nrm(14, (d, D_FF), d ** -0.5),
        'l0_ffn_down': nrm(15, (D_FF, d), BETA * D_FF ** -0.5),
        'l0_ln2_g': 1.0 + nrm(16, (d,), 0.01),
        'l0_ln2_b': nrm(17, (d,), 0.01),
        'l1_w_in': nrm(18, (d, W_IN1_COLS), d ** -0.5),
        'l1_w_out': nrm(19, (C_HEADS * HEAD_DIM, d), BETA * (C_HEADS * HEAD_DIM) ** -0.5),
        'l1_ln1_g': 1.0 + nrm(20, (d,), 0.01),
        'l1_ln1_b': nrm(21, (d,), 0.01),
        'l1_router': nrm(22, (d, N_EXPERTS), d ** -0.5),
        'l1_exp_gate': nrm(23, (N_EXPERTS, d, D_FF), d ** -0.5),
        'l1_exp_up': nrm(24, (N_EXPERTS, d, D_FF), d ** -0.5),
        'l1_exp_down': nrm(25, (N_EXPERTS, D_FF, d), BETA * D_FF ** -0.5),
        'l1_ln2_g': 1.0 + nrm(26, (d,), 0.01),
        'l1_ln2_b': nrm(27, (d,), 0.01),
    }


def reference(x, rel_bias, l0_w_in, l0_sinks, l0_cmpk_pos, l0_cmpk_w1, l0_cmpk_w2, l0_cmpv_pos,
              l0_cmpv_w1, l0_cmpv_w2, l0_w_out, l0_ln1_g, l0_ln1_b, l0_ffn_gate, l0_ffn_up, l0_ffn_down,
              l0_ln2_g, l0_ln2_b, l1_w_in, l1_w_out, l1_ln1_g, l1_ln1_b, l1_router, l1_exp_gate,
              l1_exp_up, l1_exp_down, l1_ln2_g, l1_ln2_b):
    layer_params = (
        (l0_w_in, l0_sinks, l0_cmpk_pos, l0_cmpk_w1, l0_cmpk_w2, l0_cmpv_pos, l0_cmpv_w1, l0_cmpv_w2,
         l0_w_out, l0_ln1_g, l0_ln1_b, l0_ffn_gate, l0_ffn_up, l0_ffn_down, l0_ln2_g, l0_ln2_b),
        (l1_w_in, l1_w_out, l1_ln1_g, l1_ln1_b, l1_router, l1_exp_gate, l1_exp_up, l1_exp_down,
         l1_ln2_g, l1_ln2_b),
    )
    for i in range(DEPTH):
        if i % 2 == 0:
            x = even_layer(x, rel_bias, *layer_params[i])
        else:
            x = odd_layer(x, rel_bias, *layer_params[i])
    return x
```

```python
import functools
import math

import numpy as np
import jax
import jax.numpy as jnp
from jax import lax
from jax.experimental import pallas as pl
from jax.experimental.pallas import tpu as pltpu

f32 = jnp.float32
bf16 = jnp.bfloat16
i32 = jnp.int32

D_MODEL = 1024
HEAD_DIM = 64
NUM_BUCKETS = 32
MAX_DISTANCE = 128
A_HEADS, A_KV, A_WINDOW = 8, 2, 128
B_HEADS, B_KV, B_WINDOW = 8, 2, 512
CMP_BLOCK, CMP_STRIDE = 32, 16
SLC_BLOCK, SLC_TOPN = 64, 16
C_HEADS, IDX_HEADS, IDX_DIM, C_TOPK_MAX = 16, 8, 32, 256
D_FF, N_EXPERTS = 2816, 8
DEPTH = 2
ALPHA = (2.0 * DEPTH) ** 0.25
LN_EPS = 1e-5
NEG = -1e30
POS_BIG = 1e30
INT_MIN = -(2 ** 31)

TQ = 128
FF_CHUNK = 256
VMEM_LIMIT = 56 * 1024 * 1024


def _cparams(sem):
    return pltpu.CompilerParams(dimension_semantics=sem, vmem_limit_bytes=VMEM_LIMIT)


def _bucket_table(max_dist):
    n = np.arange(max_dist + 1)
    max_exact = NUM_BUCKETS // 2
    nf = np.maximum(n, 1).astype(np.float64)
    large = max_exact + (np.log(nf / max_exact) / math.log(MAX_DISTANCE / max_exact)
                         * (NUM_BUCKETS - max_exact)).astype(np.int64)
    large = np.minimum(large, NUM_BUCKETS - 1)
    return np.where(n < max_exact, n, large).astype(np.int32)


def _bucket_of(dist):
    dist = np.maximum(dist, 0)
    return _bucket_table(int(dist.max()))[dist]


def _mm_kernel(x_ref, w_ref, o_ref):
    o_ref[...] = jnp.dot(x_ref[...].astype(bf16), w_ref[...], preferred_element_type=f32).astype(o_ref.dtype)


def _matmul(x, w, out_dtype, tm=512, tn=None):
    m, k = x.shape
    n = w.shape[1]
    tn = n if tn is None else tn
    return pl.pallas_call(
        _mm_kernel,
        grid=(m // tm, n // tn),
        in_specs=[pl.BlockSpec((tm, k), lambda i, j: (i, 0)),
                  pl.BlockSpec((k, tn), lambda i, j: (0, j))],
        out_specs=pl.BlockSpec((tm, tn), lambda i, j: (i, j)),
        out_shape=jax.ShapeDtypeStruct((m, n), out_dtype),
        compiler_params=_cparams(("parallel", "arbitrary")),
        name="matmul",
    )(x, w)


def _bias_expand_kernel(tab_ref, idx_ref, o_ref):
    h = pl.program_id(0)
    idx = idx_ref[...]
    acc = jnp.zeros(idx.shape, f32)
    for b in range(NUM_BUCKETS):
        acc = jnp.where(idx == b, tab_ref[h, b], acc)
    o_ref[0] = acc


def _bias_expand(table_t, idx, tr):
    nh = table_t.shape[0]
    r, c = idx.shape
    return pl.pallas_call(
        _bias_expand_kernel,
        grid=(nh, r // tr),
        in_specs=[pl.BlockSpec(memory_space=pltpu.SMEM),
                  pl.BlockSpec((tr, c), lambda h, i: (i, 0))],
        out_specs=pl.BlockSpec((1, tr, c), lambda h, i: (h, i, 0)),
        out_shape=jax.ShapeDtypeStruct((nh, r, c), f32),
        compiler_params=_cparams(("parallel", "arbitrary")),
        name="bias_expand",
    )(table_t, jnp.asarray(idx, dtype=i32))


def _layer_norm(z, g, b):
    mu = jnp.mean(z, axis=-1, keepdims=True)
    zc = z - mu
    var = jnp.mean(zc * zc, axis=-1, keepdims=True)
    return zc * lax.rsqrt(var + LN_EPS) * g + b


def _proj_ln_kernel(mix_ref, w_ref, x_ref, g_ref, b_ref, o_ref):
    y = jnp.dot(mix_ref[...], w_ref[...], preferred_element_type=f32)
    o_ref[...] = _layer_norm(ALPHA * x_ref[...] + y, g_ref[...], b_ref[...])


def _proj_res_ln(mix, w, x, g, b, tm=512):
    m, k = mix.shape
    d = w.shape[1]
    return pl.pallas_call(
        _proj_ln_kernel,
        grid=(m // tm,),
        in_specs=[pl.BlockSpec((tm, k), lambda i: (i, 0)),
                  pl.BlockSpec((k, d), lambda i: (0, 0)),
                  pl.BlockSpec((tm, d), lambda i: (i, 0)),
                  pl.BlockSpec((1, d), lambda i: (0, 0)),
                  pl.BlockSpec((1, d), lambda i: (0, 0))],
        out_specs=pl.BlockSpec((tm, d), lambda i: (i, 0)),
        out_shape=jax.ShapeDtypeStruct((m, d), f32),
        compiler_params=_cparams(("parallel",)),
        name="proj_res_ln",
    )(mix, w, x, g.reshape(1, d), b.reshape(1, d))


def _swiglu_accumulate(xb, wg_ref, wu_ref, wd_ref, acc_ref, row_scale, widx):
    d_ff = wg_ref.shape[-1]
    for c in range(d_ff // FF_CHUNK):
        cols = slice(c * FF_CHUNK, (c + 1) * FF_CHUNK)
        gate = jnp.dot(xb, wg_ref[widx + (slice(None), cols)], preferred_element_type=f32)
        up = jnp.dot(xb, wu_ref[widx + (slice(None), cols)], preferred_element_type=f32)
        h = jax.nn.silu(gate) * up
        if row_scale is not None:
            h = h * row_scale
        acc_ref[...] += jnp.dot(h.astype(bf16), wd_ref[widx + (cols, slice(None))], preferred_element_type=f32)


def _ffn_kernel(x_ref, wg_ref, wu_ref, wd_ref, g_ref, b_ref, o_ref, acc_ref):
    x = x_ref[...]
    acc_ref[...] = jnp.zeros_like(acc_ref)
    _swiglu_accumulate(x.astype(bf16), wg_ref, wu_ref, wd_ref, acc_ref, None, ())
    o_ref[...] = _layer_norm(ALPHA * x + acc_ref[...], g_ref[...], b_ref[...])


def _ffn_res_ln(x, wg, wu, wd, g, b, tm=512):
    m, d = x.shape
    d_ff = wg.shape[1]
    return pl.pallas_call(
        _ffn_kernel,
        grid=(m // tm,),
        in_specs=[pl.BlockSpec((tm, d), lambda i: (i, 0)),
                  pl.BlockSpec((d, d_ff), lambda i: (0, 0)),
                  pl.BlockSpec((d, d_ff), lambda i: (0, 0)),
                  pl.BlockSpec((d_ff, d), lambda i: (0, 0)),
                  pl.BlockSpec((1, d), lambda i: (0, 0)),
                  pl.BlockSpec((1, d), lambda i: (0, 0))],
        out_specs=pl.BlockSpec((tm, d), lambda i: (i, 0)),
        out_shape=jax.ShapeDtypeStruct((m, d), f32),
        scratch_shapes=[pltpu.VMEM((tm, d), f32)],
        compiler_params=_cparams(("parallel",)),
        name="ffn_res_ln",
    )(x, wg, wu, wd, g.reshape(1, d), b.reshape(1, d))


def _moe_kernel(x_ref, r_ref, wg_ref, wu_ref, wd_ref, g_ref, b_ref, o_ref, acc_ref, comb_ref, xb_ref):
    e = pl.program_id(1)
    lane = lax.broadcasted_iota(i32, comb_ref.shape, 1)

    @pl.when(e == 0)
    def _():
        x = x_ref[...]
        xb_ref[...] = x.astype(bf16)
        acc_ref[...] = jnp.zeros_like(acc_ref)
        logits = jnp.dot(x, r_ref[...], preferred_element_type=f32)
        logits = jnp.where(lane < N_EXPERTS, logits, -jnp.inf)
        m1 = jnp.max(logits, axis=-1, keepdims=True)
        i1 = jnp.min(jnp.where(logits == m1, lane, 128), axis=-1, keepdims=True)
        rest = jnp.where(lane == i1, -jnp.inf, logits)
        m2 = jnp.max(rest, axis=-1, keepdims=True)
        i2 = jnp.min(jnp.where(rest == m2, lane, 128), axis=-1, keepdims=True)
        e2 = jnp.exp(m2 - m1)
        g1 = 1.0 / (1.0 + e2)
        comb_ref[...] = jnp.where(lane == i1, g1, 0.0) + jnp.where(lane == i2, e2 * g1, 0.0)

    c = jnp.sum(jnp.where(lane == e, comb_ref[...], 0.0), axis=-1, keepdims=True)
    _swiglu_accumulate(xb_ref[...], wg_ref, wu_ref, wd_ref, acc_ref, c, (0,))

    @pl.when(e == N_EXPERTS - 1)
    def _():
        o_ref[...] = _layer_norm(ALPHA * x_ref[...] + acc_ref[...], g_ref[...], b_ref[...])


def _moe_res_ln(x, router, wg, wu, wd, g, b, tm=512):
    m, d = x.shape
    ne, _, d_ff = wg.shape
    router_p = jnp.pad(router, ((0, 0), (0, 128 - ne)))
    return pl.pallas_call(
        _moe_kernel,
        grid=(m // tm, ne),
        in_specs=[pl.BlockSpec((tm, d), lambda i, e: (i, 0)),
                  pl.BlockSpec((d, 128), lambda i, e: (0, 0)),
                  pl.BlockSpec((1, d, d_ff), lambda i, e: (e, 0, 0)),
                  pl.BlockSpec((1, d, d_ff), lambda i, e: (e, 0, 0)),
                  pl.BlockSpec((1, d_ff, d), lambda i, e: (e, 0, 0)),
                  pl.BlockSpec((1, d), lambda i, e: (0, 0)),
                  pl.BlockSpec((1, d), lambda i, e: (0, 0))],
        out_specs=pl.BlockSpec((tm, d), lambda i, e: (i, 0)),
        out_shape=jax.ShapeDtypeStruct((m, d), f32),
        scratch_shapes=[pltpu.VMEM((tm, d), f32), pltpu.VMEM((tm, 128), f32), pltpu.VMEM((tm, d), bf16)],
        compiler_params=_cparams(("parallel", "arbitrary")),
        name="moe_res_ln",
    )(x, router_p, wg, wu, wd, g.reshape(1, d), b.reshape(1, d))


def _dot_nt(a, b):
    return lax.dot_general(a, b, (((1,), (1,)), ((), ())), preferred_element_type=f32)


def _banded_kernel(sinks_ref, q_ref, k_ref, v_ref, bias_ref, o_ref, *, rep, kw, pad, window, use_sinks):
    g = pl.program_id(1)
    i = pl.program_id(2)
    start = pl.multiple_of(i * TQ, TQ)
    k = k_ref[0, 0, pl.ds(start, kw), :]
    v = v_ref[0, 0, pl.ds(start, kw), :]
    qi = lax.broadcasted_iota(i32, (TQ, kw), 0)
    ki = lax.broadcasted_iota(i32, (TQ, kw), 1)
    dist = qi + pad - ki
    valid = (dist >= 0) & (dist < window) & (start - pad + ki >= 0)
    scale = HEAD_DIM ** -0.5
    outs = []
    for r in range(rep):
        q = q_ref[0, :, r * HEAD_DIM:(r + 1) * HEAD_DIM]
        s = _dot_nt(q, k) * scale + bias_ref[r]
        s = jnp.where(valid, s, NEG)
        m = jnp.max(s, axis=-1, keepdims=True)
        if use_sinks:
            sk = sinks_ref[g * rep + r]
            m = jnp.maximum(m, sk)
        e = jnp.exp(s - m)
        l = jnp.sum(e, axis=-1, keepdims=True)
        if use_sinks:
            l = l + jnp.exp(sk - m)
        outs.append(jnp.dot(e.astype(bf16), v, preferred_element_type=f32) / l)
    o_ref[0] = jnp.concatenate(outs, axis=-1).astype(o_ref.dtype)


def _banded_gqa(q, k, v, bias, sinks, window, out_dtype):
    bn, s, hq = q.shape
    g = k.shape[1]
    rep = hq // (g * HEAD_DIM)
    pad = -(-(window - 1) // TQ) * TQ
    kw = pad + TQ
    kp = jnp.pad(k, ((0, 0), (0, 0), (pad, 0), (0, 0)))
    vp = jnp.pad(v, ((0, 0), (0, 0), (pad, 0), (0, 0)))
    use_sinks = sinks is not None
    sinks_arr = sinks.astype(f32) if use_sinks else jnp.zeros((g * rep,), f32)
    kern = functools.partial(_banded_kernel, rep=rep, kw=kw, pad=pad, window=window, use_sinks=use_sinks)
    return pl.pallas_call(
        kern,
        grid=(bn, g, s // TQ),
        in_specs=[pl.BlockSpec(memory_space=pltpu.SMEM),
                  pl.BlockSpec((1, TQ, rep * HEAD_DIM), lambda b, gg, i: (b, i, gg)),
                  pl.BlockSpec((1, 1, pad + s, HEAD_DIM), lambda b, gg, i: (b, gg, 0, 0)),
                  pl.BlockSpec((1, 1, pad + s, HEAD_DIM), lambda b, gg, i: (b, gg, 0, 0)),
                  pl.BlockSpec((rep, TQ, kw), lambda b, gg, i: (gg, 0, 0))],
        out_specs=pl.BlockSpec((1, TQ, rep * HEAD_DIM), lambda b, gg, i: (b, i, gg)),
        out_shape=jax.ShapeDtypeStruct((bn, s, hq), out_dtype),
        compiler_params=_cparams(("parallel", "parallel", "arbitrary")),
        name="banded_gqa_w%d" % window,
    )(sinks_arr, q, kp, vp, bias)


def _compress_kernel(blk_ref, pos_ref, w1_ref, w2_ref, o_ref):
    xb = (blk_ref[0].astype(f32) + pos_ref[...]).astype(bf16)
    h = jax.nn.gelu(jnp.dot(xb, w1_ref[...], preferred_element_type=f32))
    o_ref[0] = jnp.dot(h.astype(bf16), w2_ref[...], preferred_element_type=f32).astype(o_ref.dtype)


def _nsa_compress(kv, pos, w1, w2):
    bn, g, s, dh = kv.shape
    nc = s // CMP_STRIDE
    chunks = kv.reshape(bn, g, nc, CMP_STRIDE * dh)
    nxt = jnp.pad(chunks[:, :, 1:], ((0, 0), (0, 0), (0, 1), (0, 0)))
    blocks = jnp.concatenate([chunks, nxt], axis=-1).reshape(bn * g, nc, CMP_BLOCK * dh)
    cin = CMP_BLOCK * dh
    hid = w1.shape[1]
    out = pl.pallas_call(
        _compress_kernel,
        grid=(bn * g,),
        in_specs=[pl.BlockSpec((1, nc, cin), lambda i: (i, 0, 0)),
                  pl.BlockSpec((1, cin), lambda i: (0, 0)),
                  pl.BlockSpec((cin, hid), lambda i: (0, 0)),
                  pl.BlockSpec((hid, dh), lambda i: (0, 0))],
        out_specs=pl.BlockSpec((1, nc, dh), lambda i: (i, 0, 0)),
        out_shape=jax.ShapeDtypeStruct((bn * g, nc, dh), bf16),
        compiler_params=_cparams(("parallel",)),
        name="nsa_compress",
    )(blocks, pos.reshape(1, cin), w1.astype(bf16), w2.astype(bf16))
    return out.reshape(bn, g, nc, dh)


def _flash_step(q_rows, ks, vs, biases, mask, m_ref, l_ref, acc_ref, row0):
    nh = len(biases)
    s = _dot_nt(q_rows, ks) * (HEAD_DIM ** -0.5)
    ps = []
    for r in range(nh):
        rows = pl.ds(row0 + r * TQ, TQ)
        sr = s[r * TQ:(r + 1) * TQ] + biases[r]
        sr = jnp.where(mask, sr, NEG)
        m_old = m_ref[rows, :]
        m_new = jnp.maximum(m_old, jnp.max(sr, axis=-1, keepdims=True))
        p = jnp.where(mask, jnp.exp(sr - m_new), 0.0)
        alpha = jnp.exp(m_old - m_new)
        l_ref[rows, :] = alpha * l_ref[rows, :] + jnp.sum(p, axis=-1, keepdims=True)
        acc_ref[rows, :] = alpha * acc_ref[rows, :]
        m_ref[rows, :] = m_new
        ps.append(p.astype(bf16))
    rows = pl.ds(row0, nh * TQ)
    acc_ref[rows, :] += jnp.dot(jnp.concatenate(ps, axis=0), vs, preferred_element_type=f32)


def _nsa_kernel(far_ref, q_ref, kc_ref, vc_ref, biasc_ref, ov_ref, exp_ref, ks_ref, vs_ref, near_ref,
                gates_ref, owin_ref, o_ref, m_ref, l_ref, acc_ref, *, rep, n_sel):
    g = pl.program_id(1)
    i = pl.program_id(2)
    t0 = i * TQ
    nc = kc_ref.shape[2]
    nblk = ov_ref.shape[1]
    scale = HEAD_DIM ** -0.5
    qall = jnp.concatenate([q_ref[0, :, r * HEAD_DIM:(r + 1) * HEAD_DIM] for r in range(rep)], axis=0)

    tpos = t0 + lax.broadcasted_iota(i32, (TQ, nc), 0)
    cmp_end = lax.broadcasted_iota(i32, (TQ, nc), 1) * CMP_STRIDE + (CMP_BLOCK - 1)
    valid_c = tpos >= cmp_end
    s_c = _dot_nt(qall, kc_ref[0, 0]) * scale
    p_rows = []
    p_sum = jnp.zeros((TQ, nc), f32)
    for r in range(rep):
        sr = jnp.where(valid_c, s_c[r * TQ:(r + 1) * TQ] + biasc_ref[r], NEG)
        m = jnp.max(sr, axis=-1, keepdims=True)
        e = jnp.where(valid_c, jnp.exp(sr - m), 0.0)
        l = jnp.sum(e, axis=-1, keepdims=True)
        p = e * (1.0 / jnp.maximum(l, 1e-30))
        p_rows.append(p.astype(bf16))
        p_sum = p_sum + p
    o_cmp = jnp.dot(jnp.concatenate(p_rows, axis=0), vc_ref[0, 0], preferred_element_type=f32)

    imp = jnp.dot(p_sum, ov_ref[...], preferred_element_type=f32)
    blk = lax.broadcasted_iota(i32, (TQ, nblk), 1)
    tq = t0 + lax.broadcasted_iota(i32, (TQ, nblk), 0)
    cur = tq // SLC_BLOCK
    forced = (blk == 0) | (blk == cur) | (blk == cur - 1)
    score = jnp.where(forced, POS_BIG, jnp.where(blk * SLC_BLOCK <= tq, imp, NEG))
    rank = jnp.zeros((TQ, nblk), i32)
    for kk in range(nblk):
        ck = score[:, kk:kk + 1]
        later = jnp.where(blk > kk, 1, 0)
        rank = rank + jnp.where(ck > score, 1, jnp.where(ck == score, later, 0))
    sel = jnp.where(rank < n_sel, 1.0, 0.0).astype(bf16)

    m_ref[...] = jnp.full(m_ref.shape, NEG, f32)
    l_ref[...] = jnp.zeros(l_ref.shape, f32)
    acc_ref[...] = jnp.zeros(acc_ref.shape, f32)
    fars = [far_ref[g * rep + r] for r in range(rep)]
    row = lax.broadcasted_iota(i32, (TQ, TQ), 0)
    col = lax.broadcasted_iota(i32, (TQ, TQ), 1)

    def tile_mask(j):
        return jnp.dot(sel, exp_ref[j], preferred_element_type=f32) > 0.5

    def kv_tile(j):
        rows = pl.ds(pl.multiple_of(j * TQ, TQ), TQ)
        return ks_ref[0, 0, rows, :], vs_ref[0, 0, rows, :]

    def far_body(j, carry):
        ks, vs = kv_tile(j)
        _flash_step(qall, ks, vs, fars, tile_mask(j), m_ref, l_ref, acc_ref, 0)
        return carry

    lax.fori_loop(0, jnp.maximum(i - 1, 0), far_body, 0)

    @pl.when(i >= 1)
    def _():
        ks, vs = kv_tile(i - 1)
        _flash_step(qall, ks, vs, [near_ref[r, :, 0:TQ] for r in range(rep)], tile_mask(i - 1),
                    m_ref, l_ref, acc_ref, 0)

    ks, vs = kv_tile(i)
    diag_mask = (row >= col) & tile_mask(i)
    _flash_step(qall, ks, vs, [near_ref[r, :, TQ:2 * TQ] for r in range(rep)], diag_mask, m_ref, l_ref, acc_ref, 0)
    o_slc = acc_ref[...] / l_ref[...]

    sig = jax.nn.sigmoid(gates_ref[0, 0])
    outs = []
    for r in range(rep):
        rows = slice(r * TQ, (r + 1) * TQ)
        o_win = owin_ref[0, :, r * HEAD_DIM:(r + 1) * HEAD_DIM]
        outs.append(sig[:, r:r + 1] * o_cmp[rows] + sig[:, rep + r:rep + r + 1] * o_slc[rows]
                    + sig[:, 2 * rep + r:2 * rep + r + 1] * o_win)
    o_ref[0] = jnp.concatenate(outs, axis=-1).astype(o_ref.dtype)


def _nsa_mix(q, kc, vc, bias_c, ks, vs, near, far, gates, o_win):
    bn, s, hq = q.shape
    g = kc.shape[1]
    nc = kc.shape[2]
    rep = hq // (g * HEAD_DIM)
    nblk = s // SLC_BLOCK
    n_sel = min(SLC_TOPN, nblk)
    nt = s // TQ
    ci = np.arange(nc)[:, None] * CMP_STRIDE
    sj = np.arange(nblk)[None, :] * SLC_BLOCK
    overlap = np.clip(np.minimum(ci + CMP_BLOCK, sj + SLC_BLOCK) - np.maximum(ci, sj), 0, None) / CMP_BLOCK
    overlap[nc - 1] = 0.0
    expand = (np.arange(nblk)[None, :, None] == (np.arange(nt)[:, None, None] * TQ
                                                 + np.arange(TQ)[None, None, :]) // SLC_BLOCK)
    kern = functools.partial(_nsa_kernel, rep=rep, n_sel=n_sel)
    rw = rep * HEAD_DIM
    return pl.pallas_call(
        kern,
        grid=(bn, g, nt),
        in_specs=[pl.BlockSpec(memory_space=pltpu.SMEM),
                  pl.BlockSpec((1, TQ, rw), lambda b, gg, i: (b, i, gg)),
                  pl.BlockSpec((1, 1, nc, HEAD_DIM), lambda b, gg, i: (b, gg, 0, 0)),
                  pl.BlockSpec((1, 1, nc, HEAD_DIM), lambda b, gg, i: (b, gg, 0, 0)),
                  pl.BlockSpec((rep, TQ, nc), lambda b, gg, i: (gg, i, 0)),
                  pl.BlockSpec((nc, nblk), lambda b, gg, i: (0, 0)),
                  pl.BlockSpec((nt, nblk, TQ), lambda b, gg, i: (0, 0, 0)),
                  pl.BlockSpec((1, 1, s, HEAD_DIM), lambda b, gg, i: (b, gg, 0, 0)),
                  pl.BlockSpec((1, 1, s, HEAD_DIM), lambda b, gg, i: (b, gg, 0, 0)),
                  pl.BlockSpec((rep, TQ, 2 * TQ), lambda b, gg, i: (gg, 0, 0)),
                  pl.BlockSpec((1, 1, TQ, 3 * rep), lambda b, gg, i: (b, gg, i, 0)),
                  pl.BlockSpec((1, TQ, rw), lambda b, gg, i: (b, i, gg))],
        out_specs=pl.BlockSpec((1, TQ, rw), lambda b, gg, i: (b, i, gg)),
        out_shape=jax.ShapeDtypeStruct((bn, s, hq), bf16),
        scratch_shapes=[pltpu.VMEM((rep * TQ, 1), f32), pltpu.VMEM((rep * TQ, 1), f32),
                        pltpu.VMEM((rep * TQ, HEAD_DIM), f32)],
        compiler_params=_cparams(("parallel", "parallel", "arbitrary")),
        name="nsa_mix",
    )(far, q, kc, vc, bias_c, jnp.asarray(overlap, f32), jnp.asarray(expand, bf16), ks, vs, near, gates, o_win)


def _dsa_kernel(far_ref, q_ref, k_ref, v_ref, qi_ref, ki_ref, wi_ref, near_ref, tri_ref, o_ref,
                keys_ref, mask_ref, m_ref, l_ref, acc_ref, *, topk, head_chunk):
    i = pl.program_id(1)
    t0 = i * TQ
    nh = q_ref.shape[2] // HEAD_DIM
    nih = qi_ref.shape[2] // IDX_DIM
    row = lax.broadcasted_iota(i32, (TQ, TQ), 0)
    col = lax.broadcasted_iota(i32, (TQ, TQ), 1)
    n_sel_tiles = jnp.maximum(i + 1, -(-topk // TQ))

    qi_all = jnp.concatenate([qi_ref[0, :, h * IDX_DIM:(h + 1) * IDX_DIM] for h in range(nih)], axis=0)
    wi = wi_ref[0]

    def score_body(j, carry):
        kt = ki_ref[0, pl.ds(pl.multiple_of(j * TQ, TQ), TQ), :]
        rel = jnp.maximum(_dot_nt(qi_all, kt), 0.0)
        sc = jnp.zeros((TQ, TQ), f32)
        for h in range(nih):
            sc = sc + wi[:, h:h + 1] * rel[h * TQ:(h + 1) * TQ]
        sc = jnp.where(j * TQ + col <= t0 + row, sc, NEG)
        sc = jnp.where(sc == 0.0, 0.0, sc)
        bits = pltpu.bitcast(sc, i32)
        keys_ref[j] = jnp.where(bits < 0, bits ^ 0x7FFFFFFF, bits)
        return carry

    lax.fori_loop(0, n_sel_tiles, score_body, 0)

    def count(pred_fn):
        def body(j, cnt):
            return cnt + jnp.where(pred_fn(keys_ref[j]), 1.0, 0.0)
        cnt = lax.fori_loop(0, n_sel_tiles, body, jnp.zeros((TQ, TQ), f32))
        return jnp.broadcast_to(jnp.sum(cnt, axis=-1, keepdims=True), (TQ, TQ))

    def bit_body(t, prefix):
        cand = prefix ^ jnp.left_shift(jnp.int32(1), 31 - t)
        return jnp.where(count(lambda kj: kj >= cand) >= topk, cand, prefix)

    thr = lax.fori_loop(0, 32, bit_body, jnp.full((TQ, TQ), INT_MIN, i32))
    need = topk - count(lambda kj: kj > thr)

    def mask_body(j, running):
        kj = keys_ref[j]
        tie = jnp.where(kj == thr, 1.0, 0.0)
        before = jnp.dot(tie.astype(bf16), tri_ref[...], preferred_element_type=f32) + running
        take = jnp.where(kj > thr, 1.0, jnp.where(before < need, tie, 0.0))
        mask_ref[j] = jnp.where(j * TQ + col <= t0 + row, take, 0.0)
        return running + jnp.broadcast_to(jnp.sum(tie, axis=-1, keepdims=True), (TQ, TQ))

    lax.fori_loop(0, i + 1, mask_body, jnp.zeros((TQ, TQ), f32))

    m_ref[...] = jnp.full(m_ref.shape, NEG, f32)
    l_ref[...] = jnp.zeros(l_ref.shape, f32)
    acc_ref[...] = jnp.zeros(acc_ref.shape, f32)
    qall = jnp.concatenate([q_ref[0, :, h * HEAD_DIM:(h + 1) * HEAD_DIM] for h in range(nh)], axis=0)
    hc = head_chunk

    def attend(j, bias_fn):
        rows = pl.ds(pl.multiple_of(j * TQ, TQ), TQ)
        ks = k_ref[0, rows, :]
        vs = v_ref[0, rows, :]
        mask = mask_ref[j] > 0.5
        for c in range(nh // hc):
            _flash_step(qall[c * hc * TQ:(c + 1) * hc * TQ], ks, vs, [bias_fn(c * hc + r) for r in range(hc)],
                        mask, m_ref, l_ref, acc_ref, c * hc * TQ)

    def far_body(j, carry):
        attend(j, lambda h: far_ref[h])
        return carry

    lax.fori_loop(0, jnp.maximum(i - 1, 0), far_body, 0)

    @pl.when(i >= 1)
    def _():
        attend(i - 1, lambda h: near_ref[h, :, 0:TQ])

    attend(i, lambda h: near_ref[h, :, TQ:2 * TQ])
    o = acc_ref[...] / l_ref[...]
    o_ref[0] = jnp.concatenate([o[h * TQ:(h + 1) * TQ] for h in range(nh)], axis=-1).astype(o_ref.dtype)


def _dsa_attention(q, k, v, qi, ki, wi, near, far):
    bn, s, hq = q.shape
    nh = hq // HEAD_DIM
    nt = s // TQ
    topk = min(C_TOPK_MAX, s // 4)
    tri = np.triu(np.ones((TQ, TQ), np.float32), 1)
    kern = functools.partial(_dsa_kernel, topk=topk, head_chunk=4)
    nt_scr = max(nt, -(-topk // TQ))
    return pl.pallas_call(
        kern,
        grid=(bn, nt),
        in_specs=[pl.BlockSpec(memory_space=pltpu.SMEM),
                  pl.BlockSpec((1, TQ, hq), lambda b, i: (b, i, 0)),
                  pl.BlockSpec((1, s, HEAD_DIM), lambda b, i: (b, 0, 0)),
                  pl.BlockSpec((1, s, HEAD_DIM), lambda b, i: (b, 0, 0)),
                  pl.BlockSpec((1, TQ, qi.shape[2]), lambda b, i: (b, i, 0)),
                  pl.BlockSpec((1, s, IDX_DIM), lambda b, i: (b, 0, 0)),
                  pl.BlockSpec((1, TQ, wi.shape[2]), lambda b, i: (b, i, 0)),
                  pl.BlockSpec((nh, TQ, 2 * TQ), lambda b, i: (0, 0, 0)),
                  pl.BlockSpec((TQ, TQ), lambda b, i: (0, 0))],
        out_specs=pl.BlockSpec((1, TQ, hq), lambda b, i: (b, i, 0)),
        out_shape=jax.ShapeDtypeStruct((bn, s, hq), bf16),
        scratch_shapes=[pltpu.VMEM((nt_scr, TQ, TQ), i32), pltpu.VMEM((nt, TQ, TQ), f32),
                        pltpu.VMEM((nh * TQ, 1), f32), pltpu.VMEM((nh * TQ, 1), f32),
                        pltpu.VMEM((nh * TQ, HEAD_DIM), f32)],
        compiler_params=_cparams(("parallel", "arbitrary")),
        name="dsa_attention",
    )(far, q, k, v, qi, ki, wi, near, jnp.asarray(tri, bf16))


def _heads_first(a, g):
    bn, s, _ = a.shape
    return a.reshape(bn, s, g, HEAD_DIM).transpose(0, 2, 1, 3)


def _even_layer(x, table_t, w_in, sinks, cmpk_pos, cmpk_w1, cmpk_w2, cmpv_pos, cmpv_w1, cmpv_w2,
                w_out, ln1_g, ln1_b, ffn_gate, ffn_up, ffn_down, ln2_g, ln2_b):
    bn, s, d = x.shape
    xf = x.reshape(bn * s, d)
    a_q, a_kv, b_q, b_kv = A_HEADS * HEAD_DIM, A_KV * HEAD_DIM, B_HEADS * HEAD_DIM, B_KV * HEAD_DIM
    n_main = a_q + 2 * a_kv + b_q + 6 * b_kv
    rb = B_HEADS // B_KV
    main = _matmul(xf, w_in[:, :n_main].astype(bf16), bf16, tn=512).reshape(bn, s, n_main)
    w_tail = jnp.pad(w_in[:, n_main:], ((0, 0), (0, 128 - 3 * B_HEADS))).astype(bf16)
    gates = _matmul(xf, w_tail, f32)[:, :3 * B_HEADS]
    gates = gates.reshape(bn, s, 3, B_KV, rb).transpose(0, 3, 1, 2, 4).reshape(bn, B_KV, s, 3 * rb)
    cols = np.cumsum([0, a_q, a_kv, a_kv, b_q] + [b_kv] * 6)
    qa, ka, va, qb, kc_in, vc_in, ksl, vsl, kwn, vwn = [main[:, :, cols[t]:cols[t + 1]] for t in range(10)]

    near_idx = _bucket_of(np.arange(TQ)[:, None] + TQ - np.arange(2 * TQ)[None, :])
    near = _bias_expand(table_t, near_idx, TQ)
    pad_b = -(-(B_WINDOW - 1) // TQ) * TQ
    win_idx = _bucket_of(np.arange(TQ)[:, None] + pad_b - np.arange(pad_b + TQ)[None, :])
    bias_win = _bias_expand(table_t[A_HEADS:A_HEADS + B_HEADS], win_idx, TQ)
    nc = s // CMP_STRIDE
    cmp_idx = _bucket_of(np.arange(s)[:, None] - (np.arange(nc)[None, :] * CMP_STRIDE + CMP_BLOCK - 1))
    bias_c = _bias_expand(table_t[A_HEADS:A_HEADS + B_HEADS], cmp_idx, min(s, 512))
    far = table_t[:, NUM_BUCKETS - 1]

    out_a = _banded_gqa(qa, _heads_first(ka, A_KV), _heads_first(va, A_KV), near[:A_HEADS], sinks, A_WINDOW, bf16)
    o_win = _banded_gqa(qb, _heads_first(kwn, B_KV), _heads_first(vwn, B_KV), bias_win, None, B_WINDOW, f32)
    kc = _nsa_compress(_heads_first(kc_in, B_KV), cmpk_pos, cmpk_w1, cmpk_w2)
    vc = _nsa_compress(_heads_first(vc_in, B_KV), cmpv_pos, cmpv_w1, cmpv_w2)
    out_b = _nsa_mix(qb, kc, vc, bias_c, _heads_first(ksl, B_KV), _heads_first(vsl, B_KV),
                     near[A_HEADS:A_HEADS + B_HEADS], far[A_HEADS:A_HEADS + B_HEADS], gates, o_win)
    mix = jnp.concatenate([out_a, out_b], axis=-1).reshape(bn * s, a_q + b_q)
    x1 = _proj_res_ln(mix, w_out.astype(bf16), xf, ln1_g, ln1_b)
    x2 = _ffn_res_ln(x1, ffn_gate.astype(bf16), ffn_up.astype(bf16), ffn_down.astype(bf16), ln2_g, ln2_b)
    return x2.reshape(bn, s, d), near, far


def _odd_layer(x, near, far, w_in, w_out, ln1_g, ln1_b, router, exp_gate, exp_up, exp_down, ln2_g, ln2_b):
    bn, s, d = x.shape
    xf = x.reshape(bn * s, d)
    c_q = C_HEADS * HEAD_DIM
    n_main = c_q + 2 * HEAD_DIM + IDX_HEADS * IDX_DIM
    main = _matmul(xf, w_in[:, :n_main].astype(bf16), bf16, tn=n_main // 2 if n_main % 256 == 0 else n_main)
    main = main.reshape(bn, s, n_main)
    w_tail = jnp.pad(w_in[:, n_main:], ((0, 0), (0, 128 - IDX_DIM - IDX_HEADS))).astype(bf16)
    tail = _matmul(xf, w_tail, f32).reshape(bn, s, 128)
    q = main[:, :, :c_q]
    k = main[:, :, c_q:c_q + HEAD_DIM]
    v = main[:, :, c_q + HEAD_DIM:c_q + 2 * HEAD_DIM]
    qi = main[:, :, c_q + 2 * HEAD_DIM:]
    ki = tail[:, :, :IDX_DIM].astype(bf16)
    wi = tail[:, :, IDX_DIM:IDX_DIM + IDX_HEADS]
    mix = _dsa_attention(q, k, v, qi, ki, wi, near[:C_HEADS], far[:C_HEADS]).reshape(bn * s, c_q)
    x1 = _proj_res_ln(mix, w_out.astype(bf16), xf, ln1_g, ln1_b)
    x2 = _moe_res_ln(x1, router, exp_gate.astype(bf16), exp_up.astype(bf16), exp_down.astype(bf16), ln2_g, ln2_b)
    return x2.reshape(bn, s, d)


def kernel(x, rel_bias, l0_w_in, l0_sinks, l0_cmpk_pos, l0_cmpk_w1, l0_cmpk_w2, l0_cmpv_pos, l0_cmpv_w1,
           l0_cmpv_w2, l0_w_out, l0_ln1_g, l0_ln1_b, l0_ffn_gate, l0_ffn_up, l0_ffn_down, l0_ln2_g, l0_ln2_b,
           l1_w_in, l1_w_out, l1_ln1_g, l1_ln1_b, l1_router, l1_exp_gate, l1_exp_up, l1_exp_down, l1_ln2_g,
           l1_ln2_b):
    table_t = rel_bias.T.astype(f32)
    x, near, far = _even_layer(x, table_t, l0_w_in, l0_sinks, l0_cmpk_pos, l0_cmpk_w1, l0_cmpk_w2, l0_cmpv_pos,
                               l0_cmpv_w1, l0_cmpv_w2, l0_w_out, l0_ln1_g, l0_ln1_b, l0_ffn_gate, l0_ffn_up,
                               l0_ffn_down, l0_ln2_g, l0_ln2_b)
    return _odd_layer(x, near, far, l1_w_in, l1_w_out, l1_ln1_g, l1_ln1_b, l1_router, l1_exp_gate, l1_exp_up,
                      l1_exp_down, l1_ln2_g, l1_ln2_b)
```

```python
import functools
import math

import numpy as np
import jax
import jax.numpy as jnp
from jax import lax
from jax.experimental import pallas as pl
from jax.experimental.pallas import tpu as pltpu

f32 = jnp.float32
bf16 = jnp.bfloat16
i32 = jnp.int32

D_MODEL = 1024
HEAD_DIM = 64
NUM_BUCKETS = 32
MAX_DISTANCE = 128
A_HEADS, A_KV, A_WINDOW = 8, 2, 128
B_HEADS, B_KV, B_WINDOW = 8, 2, 512
CMP_BLOCK, CMP_STRIDE = 32, 16
SLC_BLOCK, SLC_TOPN = 64, 16
C_HEADS, IDX_HEADS, IDX_DIM, C_TOPK_MAX = 16, 8, 32, 256
D_FF, N_EXPERTS = 2816, 8
DEPTH = 2
ALPHA = (2.0 * DEPTH) ** 0.25
LN_EPS = 1e-5
NEG = -1e30
POS_BIG = 1e30
INT_MIN = -(2 ** 31)

TQ = 128
FF_CHUNK = 256
VMEM_LIMIT = 56 * 1024 * 1024


def _cparams(sem):
    return pltpu.CompilerParams(dimension_semantics=sem, vmem_limit_bytes=VMEM_LIMIT)


def _bucket_table(max_dist):
    n = np.arange(max_dist + 1)
    max_exact = NUM_BUCKETS // 2
    nf = np.maximum(n, 1).astype(np.float64)
    large = max_exact + (np.log(nf / max_exact) / math.log(MAX_DISTANCE / max_exact)
                         * (NUM_BUCKETS - max_exact)).astype(np.int64)
    large = np.minimum(large, NUM_BUCKETS - 1)
    return np.where(n < max_exact, n, large).astype(np.int32)


def _bucket_of(dist):
    dist = np.maximum(dist, 0)
    return _bucket_table(int(dist.max()))[dist]


def _mm_kernel(x_ref, w_ref, o_ref):
    o_ref[...] = jnp.dot(x_ref[...].astype(bf16), w_ref[...], preferred_element_type=f32).astype(o_ref.dtype)


def _matmul(x, w, out_dtype, tm=512, tn=None):
    m, k = x.shape
    n = w.shape[1]
    tn = n if tn is None else tn
    return pl.pallas_call(
        _mm_kernel,
        grid=(m // tm, n // tn),
        in_specs=[pl.BlockSpec((tm, k), lambda i, j: (i, 0)),
                  pl.BlockSpec((k, tn), lambda i, j: (0, j))],
        out_specs=pl.BlockSpec((tm, tn), lambda i, j: (i, j)),
        out_shape=jax.ShapeDtypeStruct((m, n), out_dtype),
        compiler_params=_cparams(("parallel", "arbitrary")),
        name="matmul",
    )(x, w)


def _bias_expand_kernel(tab_ref, idx_ref, o_ref):
    h = pl.program_id(0)
    idx = idx_ref[...]
    acc = jnp.zeros(idx.shape, f32)
    for b in range(NUM_BUCKETS):
        acc = jnp.where(idx == b, tab_ref[h, b], acc)
    o_ref[0] = acc


def _bias_expand(table_t, idx, tr):
    nh = table_t.shape[0]
    r, c = idx.shape
    return pl.pallas_call(
        _bias_expand_kernel,
        grid=(nh, r // tr),
        in_specs=[pl.BlockSpec(memory_space=pltpu.SMEM),
                  pl.BlockSpec((tr, c), lambda h, i: (i, 0))],
        out_specs=pl.BlockSpec((1, tr, c), lambda h, i: (h, i, 0)),
        out_shape=jax.ShapeDtypeStruct((nh, r, c), f32),
        compiler_params=_cparams(("parallel", "arbitrary")),
        name="bias_expand",
    )(table_t, jnp.asarray(idx, dtype=i32))


def _layer_norm(z, g, b):
    mu = jnp.mean(z, axis=-1, keepdims=True)
    zc = z - mu
    var = jnp.mean(zc * zc, axis=-1, keepdims=True)
    return zc * lax.rsqrt(var + LN_EPS) * g + b


def _proj_ln_kernel(mix_ref, w_ref, x_ref, g_ref, b_ref, o_ref):
    y = jnp.dot(mix_ref[...], w_ref[...], preferred_element_type=f32)
    o_ref[...] = _layer_norm(ALPHA * x_ref[...] + y, g_ref[...], b_ref[...])


def _proj_res_ln(mix, w, x, g, b, tm=512):
    m, k = mix.shape
    d = w.shape[1]
    return pl.pallas_call(
        _proj_ln_kernel,
        grid=(m // tm,),
        in_specs=[pl.BlockSpec((tm, k), lambda i: (i, 0)),
                  pl.BlockSpec((k, d), lambda i: (0, 0)),
                  pl.BlockSpec((tm, d), lambda i: (i, 0)),
                  pl.BlockSpec((1, d), lambda i: (0, 0)),
                  pl.BlockSpec((1, d), lambda i: (0, 0))],
        out_specs=pl.BlockSpec((tm, d), lambda i: (i, 0)),
        out_shape=jax.ShapeDtypeStruct((m, d), f32),
        compiler_params=_cparams(("parallel",)),
        name="proj_res_ln",
    )(mix, w, x, g.reshape(1, d), b.reshape(1, d))


def _swiglu_accumulate(xb, wg_ref, wu_ref, wd_ref, acc_ref, row_scale, widx):
    d_ff = wg_ref.shape[-1]
    for c in range(d_ff // FF_CHUNK):
        cols = slice(c * FF_CHUNK, (c + 1) * FF_CHUNK)
        gate = jnp.dot(xb, wg_ref[widx + (slice(None), cols)], preferred_element_type=f32)
        up = jnp.dot(xb, wu_ref[widx + (slice(None), cols)], preferred_element_type=f32)
        h = jax.nn.silu(gate) * up
        if row_scale is not None:
            h = h * row_scale
        acc_ref[...] += jnp.dot(h.astype(bf16), wd_ref[widx + (cols, slice(None))], preferred_element_type=f32)


def _ffn_kernel(x_ref, wg_ref, wu_ref, wd_ref, g_ref, b_ref, o_ref, acc_ref):
    x = x_ref[...]
    acc_ref[...] = jnp.zeros_like(acc_ref)
    _swiglu_accumulate(x.astype(bf16), wg_ref, wu_ref, wd_ref, acc_ref, None, ())
    o_ref[...] = _layer_norm(ALPHA * x + acc_ref[...], g_ref[...], b_ref[...])


def _ffn_res_ln(x, wg, wu, wd, g, b, tm=512):
    m, d = x.shape
    d_ff = wg.shape[1]
    return pl.pallas_call(
        _ffn_kernel,
        grid=(m // tm,),
        in_specs=[pl.BlockSpec((tm, d), lambda i: (i, 0)),
                  pl.BlockSpec((d, d_ff), lambda i: (0, 0)),
                  pl.BlockSpec((d, d_ff), lambda i: (0, 0)),
                  pl.BlockSpec((d_ff, d), lambda i: (0, 0)),
                  pl.BlockSpec((1, d), lambda i: (0, 0)),
                  pl.BlockSpec((1, d), lambda i: (0, 0))],
        out_specs=pl.BlockSpec((tm, d), lambda i: (i, 0)),
        out_shape=jax.ShapeDtypeStruct((m, d), f32),
        scratch_shapes=[pltpu.VMEM((tm, d), f32)],
        compiler_params=_cparams(("parallel",)),
        name="ffn_res_ln",
    )(x, wg, wu, wd, g.reshape(1, d), b.reshape(1, d))


def _moe_kernel(x_ref, r_ref, wg_ref, wu_ref, wd_ref, g_ref, b_ref, o_ref, acc_ref, comb_ref, xb_ref):
    e = pl.program_id(1)
    lane = lax.broadcasted_iota(i32, comb_ref.shape, 1)

    @pl.when(e == 0)
    def _():
        x = x_ref[...]
        xb_ref[...] = x.astype(bf16)
        acc_ref[...] = jnp.zeros_like(acc_ref)
        logits = jnp.dot(x, r_ref[...], preferred_element_type=f32)
        logits = jnp.where(lane < N_EXPERTS, logits, -jnp.inf)
        m1 = jnp.max(logits, axis=-1, keepdims=True)
        i1 = jnp.min(jnp.where(logits == m1, lane, 128), axis=-1, keepdims=True)
        rest = jnp.where(lane == i1, -jnp.inf, logits)
        m2 = jnp.max(rest, axis=-1, keepdims=True)
        i2 = jnp.min(jnp.where(rest == m2, lane, 128), axis=-1, keepdims=True)
        e2 = jnp.exp(m2 - m1)
        g1 = 1.0 / (1.0 + e2)
        comb_ref[...] = jnp.where(lane == i1, g1, 0.0) + jnp.where(lane == i2, e2 * g1, 0.0)

    c = jnp.sum(jnp.where(lane == e, comb_ref[...], 0.0), axis=-1, keepdims=True)
    _swiglu_accumulate(xb_ref[...], wg_ref, wu_ref, wd_ref, acc_ref, c, (0,))

    @pl.when(e == N_EXPERTS - 1)
    def _():
        o_ref[...] = _layer_norm(ALPHA * x_ref[...] + acc_ref[...], g_ref[...], b_ref[...])


def _moe_res_ln(x, router, wg, wu, wd, g, b, tm=512):
    m, d = x.shape
    ne, _, d_ff = wg.shape
    router_p = jnp.pad(router, ((0, 0), (0, 128 - ne)))
    return pl.pallas_call(
        _moe_kernel,
        grid=(m // tm, ne),
        in_specs=[pl.BlockSpec((tm, d), lambda i, e: (i, 0)),
                  pl.BlockSpec((d, 128), lambda i, e: (0, 0)),
                  pl.BlockSpec((1, d, d_ff), lambda i, e: (e, 0, 0)),
                  pl.BlockSpec((1, d, d_ff), lambda i, e: (e, 0, 0)),
                  pl.BlockSpec((1, d_ff, d), lambda i, e: (e, 0, 0)),
                  pl.BlockSpec((1, d), lambda i, e: (0, 0)),
                  pl.BlockSpec((1, d), lambda i, e: (0, 0))],
        out_specs=pl.BlockSpec((tm, d), lambda i, e: (i, 0)),
        out_shape=jax.ShapeDtypeStruct((m, d), f32),
        scratch_shapes=[pltpu.VMEM((tm, d), f32), pltpu.VMEM((tm, 128), f32), pltpu.VMEM((tm, d), bf16)],
        compiler_params=_cparams(("parallel", "arbitrary")),
        name="moe_res_ln",
    )(x, router_p, wg, wu, wd, g.reshape(1, d), b.reshape(1, d))


def _dot_nt(a, b):
    return lax.dot_general(a, b, (((1,), (1,)), ((), ())), preferred_element_type=f32)


def _banded_kernel(sinks_ref, q_ref, k_ref, v_ref, bias_ref, o_ref, *, rep, kw, pad, window, use_sinks):
    g = pl.program_id(1)
    i = pl.program_id(2)
    start = pl.multiple_of(i * TQ, TQ)
    k = k_ref[0, 0, pl.ds(start, kw), :]
    v = v_ref[0, 0, pl.ds(start, kw), :]
    qi = lax.broadcasted_iota(i32, (TQ, kw), 0)
    ki = lax.broadcasted_iota(i32, (TQ, kw), 1)
    dist = qi + pad - ki
    valid = (dist >= 0) & (dist < window) & (start - pad + ki >= 0)
    scale = HEAD_DIM ** -0.5
    outs = []
    for r in range(rep):
        q = q_ref[0, :, r * HEAD_DIM:(r + 1) * HEAD_DIM]
        s = _dot_nt(q, k) * scale + bias_ref[r]
        s = jnp.where(valid, s, NEG)
        m = jnp.max(s, axis=-1, keepdims=True)
        if use_sinks:
            sk = sinks_ref[g * rep + r]
            m = jnp.maximum(m, sk)
        e = jnp.exp(s - m)
        l = jnp.sum(e, axis=-1, keepdims=True)
        if use_sinks:
            l = l + jnp.exp(sk - m)
        outs.append(jnp.dot(e.astype(bf16), v, preferred_element_type=f32) / l)
    o_ref[0] = jnp.concatenate(outs, axis=-1).astype(o_ref.dtype)


def _banded_gqa(q, k, v, bias, sinks, window, out_dtype):
    bn, s, hq = q.shape
    g = k.shape[1]
    rep = hq // (g * HEAD_DIM)
    pad = -(-(window - 1) // TQ) * TQ
    kw = pad + TQ
    kp = jnp.pad(k, ((0, 0), (0, 0), (pad, 0), (0, 0)))
    vp = jnp.pad(v, ((0, 0), (0, 0), (pad, 0), (0, 0)))
    use_sinks = sinks is not None
    sinks_arr = sinks.astype(f32) if use_sinks else jnp.zeros((g * rep,), f32)
    kern = functools.partial(_banded_kernel, rep=rep, kw=kw, pad=pad, window=window, use_sinks=use_sinks)
    return pl.pallas_call(
        kern,
        grid=(bn, g, s // TQ),
        in_specs=[pl.BlockSpec(memory_space=pltpu.SMEM),
                  pl.BlockSpec((1, TQ, rep * HEAD_DIM), lambda b, gg, i: (b, i, gg)),
                  pl.BlockSpec((1, 1, pad + s, HEAD_DIM), lambda b, gg, i: (b, gg, 0, 0)),
                  pl.BlockSpec((1, 1, pad + s, HEAD_DIM), lambda b, gg, i: (b, gg, 0, 0)),
                  pl.BlockSpec((rep, TQ, kw), lambda b, gg, i: (gg, 0, 0))],
        out_specs=pl.BlockSpec((1, TQ, rep * HEAD_DIM), lambda b, gg, i: (b, i, gg)),
        out_shape=jax.ShapeDtypeStruct((bn, s, hq), out_dtype),
        compiler_params=_cparams(("parallel", "parallel", "arbitrary")),
        name="banded_gqa_w%d" % window,
    )(sinks_arr, q, kp, vp, bias)


def _compress_kernel(blk_ref, pos_ref, w1_ref, w2_ref, o_ref):
    xb = (blk_ref[0].astype(f32) + pos_ref[...]).astype(bf16)
    h = jax.nn.gelu(jnp.dot(xb, w1_ref[...], preferred_element_type=f32))
    o_ref[0] = jnp.dot(h.astype(bf16), w2_ref[...], preferred_element_type=f32).astype(o_ref.dtype)


def _nsa_compress(kv, pos, w1, w2):
    bn, g, s, dh = kv.shape
    nc = s // CMP_STRIDE
    chunks = kv.reshape(bn, g, nc, CMP_STRIDE * dh)
    nxt = jnp.pad(chunks[:, :, 1:], ((0, 0), (0, 0), (0, 1), (0, 0)))
    blocks = jnp.concatenate([chunks, nxt], axis=-1).reshape(bn * g, nc, CMP_BLOCK * dh)
    cin = CMP_BLOCK * dh
    hid = w1.shape[1]
    out = pl.pallas_call(
        _compress_kernel,
        grid=(bn * g,),
        in_specs=[pl.BlockSpec((1, nc, cin), lambda i: (i, 0, 0)),
                  pl.BlockSpec((1, cin), lambda i: (0, 0)),
                  pl.BlockSpec((cin, hid), lambda i: (0, 0)),
                  pl.BlockSpec((hid, dh), lambda i: (0, 0))],
        out_specs=pl.BlockSpec((1, nc, dh), lambda i: (i, 0, 0)),
        out_shape=jax.ShapeDtypeStruct((bn * g, nc, dh), bf16),
        compiler_params=_cparams(("parallel",)),
        name="nsa_compress",
    )(blocks, pos.reshape(1, cin), w1.astype(bf16), w2.astype(bf16))
    return out.reshape(bn, g, nc, dh)


def _flash_step(q_rows, ks, vs, biases, mask, m_ref, l_ref, acc_ref, row0):
    nh = len(biases)
    s = _dot_nt(q_rows, ks) * (HEAD_DIM ** -0.5)
    ps = []
    for r in range(nh):
        rows = pl.ds(row0 + r * TQ, TQ)
        sr = s[r * TQ:(r + 1) * TQ] + biases[r]
        sr = jnp.where(mask, sr, NEG)
        m_old = m_ref[rows, :]
        m_new = jnp.maximum(m_old, jnp.max(sr, axis=-1, keepdims=True))
        p = jnp.where(mask, jnp.exp(sr - m_new), 0.0)
        alpha = jnp.exp(m_old - m_new)
        l_ref[rows, :] = alpha * l_ref[rows, :] + jnp.sum(p, axis=-1, keepdims=True)
        acc_ref[rows, :] = alpha * acc_ref[rows, :]
        m_ref[rows, :] = m_new
        ps.append(p.astype(bf16))
    rows = pl.ds(row0, nh * TQ)
    acc_ref[rows, :] += jnp.dot(jnp.concatenate(ps, axis=0), vs, preferred_element_type=f32)


def _nsa_kernel(far_ref, q_ref, kc_ref, vc_ref, biasc_ref, ov_ref, exp_ref, ks_ref, vs_ref, near_ref,
                gates_ref, owin_ref, o_ref, m_ref, l_ref, acc_ref, *, rep, n_sel):
    g = pl.program_id(1)
    i = pl.program_id(2)
    t0 = i * TQ
    nc = kc_ref.shape[2]
    nblk = ov_ref.shape[1]
    scale = HEAD_DIM ** -0.5
    qall = jnp.concatenate([q_ref[0, :, r * HEAD_DIM:(r + 1) * HEAD_DIM] for r in range(rep)], axis=0)

    tpos = t0 + lax.broadcasted_iota(i32, (TQ, nc), 0)
    cmp_end = lax.broadcasted_iota(i32, (TQ, nc), 1) * CMP_STRIDE + (CMP_BLOCK - 1)
    valid_c = tpos >= cmp_end
    s_c = _dot_nt(qall, kc_ref[0, 0]) * scale
    p_rows = []
    p_sum = jnp.zeros((TQ, nc), f32)
    for r in range(rep):
        sr = jnp.where(valid_c, s_c[r * TQ:(r + 1) * TQ] + biasc_ref[r], NEG)
        m = jnp.max(sr, axis=-1, keepdims=True)
        e = jnp.where(valid_c, jnp.exp(sr - m), 0.0)
        l = jnp.sum(e, axis=-1, keepdims=True)
        p = e * (1.0 / jnp.maximum(l, 1e-30))
        p_rows.append(p.astype(bf16))
        p_sum = p_sum + p
    o_cmp = jnp.dot(jnp.concatenate(p_rows, axis=0), vc_ref[0, 0], preferred_element_type=f32)

    imp = jnp.dot(p_sum, ov_ref[...], preferred_element_type=f32)
    blk = lax.broadcasted_iota(i32, (TQ, nblk), 1)
    tq = t0 + lax.broadcasted_iota(i32, (TQ, nblk), 0)
    cur = tq // SLC_BLOCK
    forced = (blk == 0) | (blk == cur) | (blk == cur - 1)
    score = jnp.where(forced, POS_BIG, jnp.where(blk * SLC_BLOCK <= tq, imp, NEG))
    rank = jnp.zeros((TQ, nblk), i32)
    for kk in range(nblk):
        ck = score[:, kk:kk + 1]
        later = jnp.where(blk > kk, 1, 0)
        rank = rank + jnp.where(ck > score, 1, jnp.where(ck == score, later, 0))
    sel = jnp.where(rank < n_sel, 1.0, 0.0).astype(bf16)

    m_ref[...] = jnp.full(m_ref.shape, NEG, f32)
    l_ref[...] = jnp.zeros(l_ref.shape, f32)
    acc_ref[...] = jnp.zeros(acc_ref.shape, f32)
    fars = [far_ref[g * rep + r] for r in range(rep)]
    row = lax.broadcasted_iota(i32, (TQ, TQ), 0)
    col = lax.broadcasted_iota(i32, (TQ, TQ), 1)

    def tile_mask(j):
        return jnp.dot(sel, exp_ref[j], preferred_element_type=f32) > 0.5

    def kv_tile(j):
        rows = pl.ds(pl.multiple_of(j * TQ, TQ), TQ)
        return ks_ref[0, 0, rows, :], vs_ref[0, 0, rows, :]

    def far_body(j, carry):
        ks, vs = kv_tile(j)
        _flash_step(qall, ks, vs, fars, tile_mask(j), m_ref, l_ref, acc_ref, 0)
        return carry

    lax.fori_loop(0, jnp.maximum(i - 1, 0), far_body, 0)

    @pl.when(i >= 1)
    def _():
        ks, vs = kv_tile(i - 1)
        _flash_step(qall, ks, vs, [near_ref[r, :, 0:TQ] for r in range(rep)], tile_mask(i - 1),
                    m_ref, l_ref, acc_ref, 0)

    ks, vs = kv_tile(i)
    diag_mask = (row >= col) & tile_mask(i)
    _flash_step(qall, ks, vs, [near_ref[r, :, TQ:2 * TQ] for r in range(rep)], diag_mask, m_ref, l_ref, acc_ref, 0)
    o_slc = acc_ref[...] / l_ref[...]

    sig = jax.nn.sigmoid(gates_ref[0, 0])
    outs = []
    for r in range(rep):
        rows = slice(r * TQ, (r + 1) * TQ)
        o_win = owin_ref[0, :, r * HEAD_DIM:(r + 1) * HEAD_DIM]
        outs.append(sig[:, r:r + 1] * o_cmp[rows] + sig[:, rep + r:rep + r + 1] * o_slc[rows]
                    + sig[:, 2 * rep + r:2 * rep + r + 1] * o_win)
    o_ref[0] = jnp.concatenate(outs, axis=-1).astype(o_ref.dtype)


def _nsa_mix(q, kc, vc, bias_c, ks, vs, near, far, gates, o_win):
    bn, s, hq = q.shape
    g = kc.shape[1]
    nc = kc.shape[2]
    rep = hq // (g * HEAD_DIM)
    nblk = s // SLC_BLOCK
    n_sel = min(SLC_TOPN, nblk)
    nt = s // TQ
    ci = np.arange(nc)[:, None] * CMP_STRIDE
    sj = np.arange(nblk)[None, :] * SLC_BLOCK
    overlap = np.clip(np.minimum(ci + CMP_BLOCK, sj + SLC_BLOCK) - np.maximum(ci, sj), 0, None) / CMP_BLOCK
    overlap[nc - 1] = 0.0
    expand = (np.arange(nblk)[None, :, None] == (np.arange(nt)[:, None, None] * TQ
                                                 + np.arange(TQ)[None, None, :]) // SLC_BLOCK)
    kern = functools.partial(_nsa_kernel, rep=rep, n_sel=n_sel)
    rw = rep * HEAD_DIM
    return pl.pallas_call(
        kern,
        grid=(bn, g, nt),
        in_specs=[pl.BlockSpec(memory_space=pltpu.SMEM),
                  pl.BlockSpec((1, TQ, rw), lambda b, gg, i: (b, i, gg)),
                  pl.BlockSpec((1, 1, nc, HEAD_DIM), lambda b, gg, i: (b, gg, 0, 0)),
                  pl.BlockSpec((1, 1, nc, HEAD_DIM), lambda b, gg, i: (b, gg, 0, 0)),
                  pl.BlockSpec((rep, TQ, nc), lambda b, gg, i: (gg, i, 0)),
                  pl.BlockSpec((nc, nblk), lambda b, gg, i: (0, 0)),
                  pl.BlockSpec((nt, nblk, TQ), lambda b, gg, i: (0, 0, 0)),
                  pl.BlockSpec((1, 1, s, HEAD_DIM), lambda b, gg, i: (b, gg, 0, 0)),
                  pl.BlockSpec((1, 1, s, HEAD_DIM), lambda b, gg, i: (b, gg, 0, 0)),
                  pl.BlockSpec((rep, TQ, 2 * TQ), lambda b, gg, i: (gg, 0, 0)),
                  pl.BlockSpec((1, 1, TQ, 3 * rep), lambda b, gg, i: (b, gg, i, 0)),
                  pl.BlockSpec((1, TQ, rw), lambda b, gg, i: (b, i, gg))],
        out_specs=pl.BlockSpec((1, TQ, rw), lambda b, gg, i: (b, i, gg)),
        out_shape=jax.ShapeDtypeStruct((bn, s, hq), bf16),
        scratch_shapes=[pltpu.VMEM((rep * TQ, 1), f32), pltpu.VMEM((rep * TQ, 1), f32),
                        pltpu.VMEM((rep * TQ, HEAD_DIM), f32)],
        compiler_params=_cparams(("parallel", "parallel", "arbitrary")),
        name="nsa_mix",
    )(far, q, kc, vc, bias_c, jnp.asarray(overlap, f32), jnp.asarray(expand, bf16), ks, vs, near, gates, o_win)


def _dsa_kernel_rowmajor(far_ref, q_ref, k_ref, v_ref, qi_ref, ki_ref, wi_ref, near_ref, tri_ref, o_ref,
                         keys_ref, mask_ref, m_ref, l_ref, acc_ref, *, topk, head_chunk):
    i = pl.program_id(1)
    t0 = i * TQ
    nh = q_ref.shape[2] // HEAD_DIM
    nih = qi_ref.shape[2] // IDX_DIM
    row = lax.broadcasted_iota(i32, (TQ, TQ), 0)
    col = lax.broadcasted_iota(i32, (TQ, TQ), 1)
    n_sel_tiles = jnp.maximum(i + 1, -(-topk // TQ))

    qi_all = jnp.concatenate([qi_ref[0, :, h * IDX_DIM:(h + 1) * IDX_DIM] for h in range(nih)], axis=0)
    wi = wi_ref[0]

    def score_body(j, carry):
        kt = ki_ref[0, pl.ds(pl.multiple_of(j * TQ, TQ), TQ), :]
        rel = jnp.maximum(_dot_nt(qi_all, kt), 0.0)
        sc = jnp.zeros((TQ, TQ), f32)
        for h in range(nih):
            sc = sc + wi[:, h:h + 1] * rel[h * TQ:(h + 1) * TQ]
        sc = jnp.where(j * TQ + col <= t0 + row, sc, NEG)
        sc = jnp.where(sc == 0.0, 0.0, sc)
        bits = pltpu.bitcast(sc, i32)
        keys_ref[j] = jnp.where(bits < 0, bits ^ 0x7FFFFFFF, bits)
        return carry

    lax.fori_loop(0, n_sel_tiles, score_body, 0)

    def count(pred_fn):
        def body(j, cnt):
            return cnt + jnp.where(pred_fn(keys_ref[j]), 1.0, 0.0)
        cnt = lax.fori_loop(0, n_sel_tiles, body, jnp.zeros((TQ, TQ), f32))
        return jnp.broadcast_to(jnp.sum(cnt, axis=-1, keepdims=True), (TQ, TQ))

    def bit_body(t, prefix):
        cand = prefix ^ jnp.left_shift(jnp.int32(1), 31 - t)
        return jnp.where(count(lambda kj: kj >= cand) >= topk, cand, prefix)

    thr = lax.fori_loop(0, 32, bit_body, jnp.full((TQ, TQ), INT_MIN, i32))
    need = topk - count(lambda kj: kj > thr)

    def mask_body(j, running):
        kj = keys_ref[j]
        tie = jnp.where(kj == thr, 1.0, 0.0)
        before = jnp.dot(tie.astype(bf16), tri_ref[...], preferred_element_type=f32) + running
        take = jnp.where(kj > thr, 1.0, jnp.where(before < need, tie, 0.0))
        mask_ref[j] = jnp.where(j * TQ + col <= t0 + row, take, 0.0)
        return running + jnp.broadcast_to(jnp.sum(tie, axis=-1, keepdims=True), (TQ, TQ))

    lax.fori_loop(0, i + 1, mask_body, jnp.zeros((TQ, TQ), f32))

    m_ref[...] = jnp.full(m_ref.shape, NEG, f32)
    l_ref[...] = jnp.zeros(l_ref.shape, f32)
    acc_ref[...] = jnp.zeros(acc_ref.shape, f32)
    qall = jnp.concatenate([q_ref[0, :, h * HEAD_DIM:(h + 1) * HEAD_DIM] for h in range(nh)], axis=0)
    hc = head_chunk

    def attend(j, bias_fn):
        rows = pl.ds(pl.multiple_of(j * TQ, TQ), TQ)
        ks = k_ref[0, rows, :]
        vs = v_ref[0, rows, :]
        mask = mask_ref[j] > 0.5
        for c in range(nh // hc):
            _flash_step(qall[c * hc * TQ:(c + 1) * hc * TQ], ks, vs, [bias_fn(c * hc + r) for r in range(hc)],
                        mask, m_ref, l_ref, acc_ref, c * hc * TQ)

    def far_body(j, carry):
        attend(j, lambda h: far_ref[h])
        return carry

    lax.fori_loop(0, jnp.maximum(i - 1, 0), far_body, 0)

    @pl.when(i >= 1)
    def _():
        attend(i - 1, lambda h: near_ref[h, :, 0:TQ])

    attend(i, lambda h: near_ref[h, :, TQ:2 * TQ])
    o = acc_ref[...] / l_ref[...]
    o_ref[0] = jnp.concatenate([o[h * TQ:(h + 1) * TQ] for h in range(nh)], axis=-1).astype(o_ref.dtype)


def _dsa_attention_rowmajor(q, k, v, qi, ki, wi, near, far):
    bn, s, hq = q.shape
    nh = hq // HEAD_DIM
    nt = s // TQ
    topk = min(C_TOPK_MAX, s // 4)
    tri = np.triu(np.ones((TQ, TQ), np.float32), 1)
    kern = functools.partial(_dsa_kernel_rowmajor, topk=topk, head_chunk=4)
    nt_scr = max(nt, -(-topk // TQ))
    return pl.pallas_call(
        kern,
        grid=(bn, nt),
        in_specs=[pl.BlockSpec(memory_space=pltpu.SMEM),
                  pl.BlockSpec((1, TQ, hq), lambda b, i: (b, i, 0)),
                  pl.BlockSpec((1, s, HEAD_DIM), lambda b, i: (b, 0, 0)),
                  pl.BlockSpec((1, s, HEAD_DIM), lambda b, i: (b, 0, 0)),
                  pl.BlockSpec((1, TQ, qi.shape[2]), lambda b, i: (b, i, 0)),
                  pl.BlockSpec((1, s, IDX_DIM), lambda b, i: (b, 0, 0)),
                  pl.BlockSpec((1, TQ, wi.shape[2]), lambda b, i: (b, i, 0)),
                  pl.BlockSpec((nh, TQ, 2 * TQ), lambda b, i: (0, 0, 0)),
                  pl.BlockSpec((TQ, TQ), lambda b, i: (0, 0))],
        out_specs=pl.BlockSpec((1, TQ, hq), lambda b, i: (b, i, 0)),
        out_shape=jax.ShapeDtypeStruct((bn, s, hq), bf16),
        scratch_shapes=[pltpu.VMEM((nt_scr, TQ, TQ), i32), pltpu.VMEM((nt, TQ, TQ), f32),
                        pltpu.VMEM((nh * TQ, 1), f32), pltpu.VMEM((nh * TQ, 1), f32),
                        pltpu.VMEM((nh * TQ, HEAD_DIM), f32)],
        compiler_params=_cparams(("parallel", "arbitrary")),
        name="dsa_attention",
    )(far, q, k, v, qi, ki, wi, near, jnp.asarray(tri, bf16))


V_ROWS = HEAD_DIM + 16


def _dsa_kernel(qt_ref, k_ref, vt_ref, qit_ref, ki_ref, wit_ref, bias_ref, tri_ref, o_ref,
                keys_ref, qs_ref, m_ref, acc_ref, s_ref, mt_ref, al_ref, *, topk, chunk):
    i = pl.program_id(1)
    t0 = i * TQ
    width = qt_ref.shape[3]
    nh = width // TQ
    nih = qit_ref.shape[3] // TQ
    krow = lax.broadcasted_iota(i32, (TQ, TQ), 0)
    qcol = lax.broadcasted_iota(i32, (TQ, TQ), 1)
    n_pairs = jnp.maximum(i // 2 + 1, -(-topk // (2 * TQ)))

    qit = qit_ref[0, 0]
    wit = wit_ref[0, 0]

    def score_tile(j):
        kt = ki_ref[0, pl.ds(pl.multiple_of(j * TQ, TQ), TQ), :]
        rel = jnp.maximum(jnp.dot(kt, qit, preferred_element_type=f32), 0.0)
        sc = jnp.zeros((TQ, TQ), f32)
        for h in range(nih):
            sc = sc + wit[h:h + 1, :] * rel[:, h * TQ:(h + 1) * TQ]
        sc = jnp.where(j * TQ + krow <= t0 + qcol, sc, NEG)
        sc = jnp.where(sc == 0.0, 0.0, sc)
        bits = pltpu.bitcast(sc, i32)
        keys_ref[j] = jnp.where(bits < 0, bits ^ 0x7FFFFFFF, bits)

    def score_body(jj, carry):
        score_tile(2 * jj)
        score_tile(2 * jj + 1)
        return carry

    lax.fori_loop(0, n_pairs, score_body, 0)

    def count(pred_fn):
        def body(jj, cnt):
            return (cnt + jnp.where(pred_fn(keys_ref[2 * jj]), 1.0, 0.0)
                    + jnp.where(pred_fn(keys_ref[2 * jj + 1]), 1.0, 0.0))
        cnt = lax.fori_loop(0, n_pairs, body, jnp.zeros((TQ, TQ), f32))
        return jnp.sum(cnt, axis=0, keepdims=True)

    def bit_body(t, prefix):
        cand = prefix ^ jnp.left_shift(jnp.int32(1), 31 - t)
        return jnp.where(count(lambda kj: kj >= cand) >= topk, cand, prefix)

    thr = lax.fori_loop(0, 32, bit_body, jnp.full((1, TQ), INT_MIN, i32))
    n_ge = count(lambda kj: kj >= thr)

    @pl.when(jnp.max(n_ge) > topk)
    def _():
        need = topk - count(lambda kj: kj > thr)

        def fix_body(j, running):
            kj = keys_ref[j]
            tie = jnp.where(kj == thr, 1.0, 0.0)
            before = jnp.dot(tri_ref[...], tie.astype(bf16), preferred_element_type=f32) + running
            keys_ref[j] = jnp.where(before >= need, jnp.where(kj == thr, kj - 1, kj), kj)
            return running + jnp.sum(tie, axis=0, keepdims=True)

        lax.fori_loop(0, 2 * n_pairs, fix_body, jnp.zeros((1, TQ), f32))

    m_ref[...] = jnp.full(m_ref.shape, NEG, f32)
    acc_ref[...] = jnp.zeros(acc_ref.shape, f32)
    qs_ref[...] = (qt_ref[0, 0].astype(f32) * (HEAD_DIM ** -0.5)).astype(bf16)

    def scores(j):
        slot = j % 2
        kt = k_ref[0, pl.ds(pl.multiple_of(j * TQ, TQ), TQ), :]
        kind = jnp.clip(j - (i - 2), 0, 2)
        maskb = jnp.where(keys_ref[j] >= thr, 0.0, NEG)
        maskb = jnp.concatenate([maskb] * (chunk // TQ), axis=1)
        for c in range(width // chunk):
            cols = slice(c * chunk, (c + 1) * chunk)
            s = jnp.dot(kt, qs_ref[:, cols], preferred_element_type=f32) + bias_ref[kind, :, cols] + maskb
            s_ref[slot, :, cols] = s
            m_old = m_ref[:, cols]
            m_new = jnp.maximum(m_old, jnp.max(s, axis=0, keepdims=True))
            m_ref[:, cols] = m_new
            mt_ref[slot, :, cols] = m_new
            al_ref[slot, :, cols] = jnp.exp(m_old - m_new)

    def values(j):
        slot = j % 2
        vt = vt_ref[0, j]
        for c in range(width // chunk):
            cols = slice(c * chunk, (c + 1) * chunk)
            p = jnp.exp(s_ref[slot, :, cols] - mt_ref[slot, :, cols])
            acc_ref[:, cols] = (al_ref[slot, :, cols] * acc_ref[:, cols]
                                + jnp.dot(vt, p.astype(bf16), preferred_element_type=f32))

    scores(0)

    def pipe_body(j, carry):
        scores(j)
        values(j - 1)
        return carry

    lax.fori_loop(1, i + 1, pipe_body, 0)
    values(i)
    acc = acc_ref[...]
    ot = acc[0:HEAD_DIM] / acc[HEAD_DIM:HEAD_DIM + 1]
    o_ref[0] = jnp.concatenate([ot[:, h * TQ:(h + 1) * TQ].T for h in range(nh)], axis=1).astype(o_ref.dtype)


def _dsa_attention(q, k, v, qi, ki, wi, near, far):
    bn, s, hq = q.shape
    nh = hq // HEAD_DIM
    nih = qi.shape[2] // IDX_DIM
    nt = s // TQ
    topk = min(C_TOPK_MAX, s // 4)
    width = nh * TQ
    qt = q.reshape(bn, nt, TQ, nh, HEAD_DIM).transpose(0, 1, 4, 3, 2).reshape(bn, nt, HEAD_DIM, width)
    qit = qi.reshape(bn, nt, TQ, nih, IDX_DIM).transpose(0, 1, 4, 3, 2).reshape(bn, nt, IDX_DIM, nih * TQ)
    wit = wi.reshape(bn, nt, TQ, nih).transpose(0, 1, 3, 2)
    vt = v.reshape(bn, nt, TQ, HEAD_DIM).transpose(0, 1, 3, 2)
    vt = jnp.concatenate([vt, jnp.ones((bn, nt, 1, TQ), vt.dtype),
                          jnp.zeros((bn, nt, V_ROWS - HEAD_DIM - 1, TQ), vt.dtype)], axis=2)
    neart = near.reshape(nh, TQ, 2, TQ).transpose(2, 3, 0, 1).reshape(2, TQ, width)
    causal = np.where(np.arange(TQ)[:, None] <= np.arange(TQ)[None, :], 0.0, NEG).astype(np.float32)
    bias3 = jnp.stack([jnp.broadcast_to(jnp.repeat(far, TQ)[None, :], (TQ, width)), neart[0],
                       neart[1] + jnp.asarray(np.tile(causal, (1, nh)))])
    tri = np.tril(np.ones((TQ, TQ), np.float32), -1)
    kern = functools.partial(_dsa_kernel, topk=topk, chunk=512)
    nt_scr = max(nt, -(-topk // TQ))
    return pl.pallas_call(
        kern,
        grid=(bn, nt),
        in_specs=[pl.BlockSpec((1, 1, HEAD_DIM, width), lambda b, i: (b, i, 0, 0)),
                  pl.BlockSpec((1, s, HEAD_DIM), lambda b, i: (b, 0, 0)),
                  pl.BlockSpec((1, nt, V_ROWS, TQ), lambda b, i: (b, 0, 0, 0)),
                  pl.BlockSpec((1, 1, IDX_DIM, nih * TQ), lambda b, i: (b, i, 0, 0)),
                  pl.BlockSpec((1, s, IDX_DIM), lambda b, i: (b, 0, 0)),
                  pl.BlockSpec((1, 1, nih, TQ), lambda b, i: (b, i, 0, 0)),
                  pl.BlockSpec((3, TQ, width), lambda b, i: (0, 0, 0)),
                  pl.BlockSpec((TQ, TQ), lambda b, i: (0, 0))],
        out_specs=pl.BlockSpec((1, TQ, hq), lambda b, i: (b, i, 0)),
        out_shape=jax.ShapeDtypeStruct((bn, s, hq), bf16),
        scratch_shapes=[pltpu.VMEM((nt_scr, TQ, TQ), i32), pltpu.VMEM((HEAD_DIM, width), bf16),
                        pltpu.VMEM((1, width), f32), pltpu.VMEM((V_ROWS, width), f32),
                        pltpu.VMEM((2, TQ, width), f32), pltpu.VMEM((2, 1, width), f32),
                        pltpu.VMEM((2, 1, width), f32)],
        compiler_params=_cparams(("parallel", "arbitrary")),
        name="dsa_attention",
    )(qt, k, vt, qit, ki, wit, bias3, jnp.asarray(tri, bf16))


def _heads_first(a, g):
    bn, s, _ = a.shape
    return a.reshape(bn, s, g, HEAD_DIM).transpose(0, 2, 1, 3)


def _even_layer(x, table_t, w_in, sinks, cmpk_pos, cmpk_w1, cmpk_w2, cmpv_pos, cmpv_w1, cmpv_w2,
                w_out, ln1_g, ln1_b, ffn_gate, ffn_up, ffn_down, ln2_g, ln2_b):
    bn, s, d = x.shape
    xf = x.reshape(bn * s, d)
    a_q, a_kv, b_q, b_kv = A_HEADS * HEAD_DIM, A_KV * HEAD_DIM, B_HEADS * HEAD_DIM, B_KV * HEAD_DIM
    n_main = a_q + 2 * a_kv + b_q + 6 * b_kv
    rb = B_HEADS // B_KV
    main = _matmul(xf, w_in[:, :n_main].astype(bf16), bf16, tn=512).reshape(bn, s, n_main)
    w_tail = jnp.pad(w_in[:, n_main:], ((0, 0), (0, 128 - 3 * B_HEADS))).astype(bf16)
    gates = _matmul(xf, w_tail, f32)[:, :3 * B_HEADS]
    gates = gates.reshape(bn, s, 3, B_KV, rb).transpose(0, 3, 1, 2, 4).reshape(bn, B_KV, s, 3 * rb)
    cols = np.cumsum([0, a_q, a_kv, a_kv, b_q] + [b_kv] * 6)
    qa, ka, va, qb, kc_in, vc_in, ksl, vsl, kwn, vwn = [main[:, :, cols[t]:cols[t + 1]] for t in range(10)]

    near_idx = _bucket_of(np.arange(TQ)[:, None] + TQ - np.arange(2 * TQ)[None, :])
    near = _bias_expand(table_t, near_idx, TQ)
    pad_b = -(-(B_WINDOW - 1) // TQ) * TQ
    win_idx = _bucket_of(np.arange(TQ)[:, None] + pad_b - np.arange(pad_b + TQ)[None, :])
    bias_win = _bias_expand(table_t[A_HEADS:A_HEADS + B_HEADS], win_idx, TQ)
    nc = s // CMP_STRIDE
    cmp_idx = _bucket_of(np.arange(s)[:, None] - (np.arange(nc)[None, :] * CMP_STRIDE + CMP_BLOCK - 1))
    bias_c = _bias_expand(table_t[A_HEADS:A_HEADS + B_HEADS], cmp_idx, min(s, 512))
    far = table_t[:, NUM_BUCKETS - 1]

    out_a = _banded_gqa(qa, _heads_first(ka, A_KV), _heads_first(va, A_KV), near[:A_HEADS], sinks, A_WINDOW, bf16)
    o_win = _banded_gqa(qb, _heads_first(kwn, B_KV), _heads_first(vwn, B_KV), bias_win, None, B_WINDOW, f32)
    kc = _nsa_compress(_heads_first(kc_in, B_KV), cmpk_pos, cmpk_w1, cmpk_w2)
    vc = _nsa_compress(_heads_first(vc_in, B_KV), cmpv_pos, cmpv_w1, cmpv_w2)
    out_b = _nsa_mix(qb, kc, vc, bias_c, _heads_first(ksl, B_KV), _heads_first(vsl, B_KV),
                     near[A_HEADS:A_HEADS + B_HEADS], far[A_HEADS:A_HEADS + B_HEADS], gates, o_win)
    mix = jnp.concatenate([out_a, out_b], axis=-1).reshape(bn * s, a_q + b_q)
    x1 = _proj_res_ln(mix, w_out.astype(bf16), xf, ln1_g, ln1_b)
    x2 = _ffn_res_ln(x1, ffn_gate.astype(bf16), ffn_up.astype(bf16), ffn_down.astype(bf16), ln2_g, ln2_b)
    return x2.reshape(bn, s, d), near, far


def _odd_layer(x, near, far, w_in, w_out, ln1_g, ln1_b, router, exp_gate, exp_up, exp_down, ln2_g, ln2_b):
    bn, s, d = x.shape
    xf = x.reshape(bn * s, d)
    c_q = C_HEADS * HEAD_DIM
    n_main = c_q + 2 * HEAD_DIM + IDX_HEADS * IDX_DIM
    main = _matmul(xf, w_in[:, :n_main].astype(bf16), bf16, tn=n_main // 2 if n_main % 256 == 0 else n_main)
    main = main.reshape(bn, s, n_main)
    w_tail = jnp.pad(w_in[:, n_main:], ((0, 0), (0, 128 - IDX_DIM - IDX_HEADS))).astype(bf16)
    tail = _matmul(xf, w_tail, f32).reshape(bn, s, 128)
    q = main[:, :, :c_q]
    k = main[:, :, c_q:c_q + HEAD_DIM]
    v = main[:, :, c_q + HEAD_DIM:c_q + 2 * HEAD_DIM]
    qi = main[:, :, c_q + 2 * HEAD_DIM:]
    ki = tail[:, :, :IDX_DIM].astype(bf16)
    wi = tail[:, :, IDX_DIM:IDX_DIM + IDX_HEADS]
    mix = _dsa_attention(q, k, v, qi, ki, wi, near[:C_HEADS], far[:C_HEADS]).reshape(bn * s, c_q)
    x1 = _proj_res_ln(mix, w_out.astype(bf16), xf, ln1_g, ln1_b)
    x2 = _moe_res_ln(x1, router, exp_gate.astype(bf16), exp_up.astype(bf16), exp_down.astype(bf16), ln2_g, ln2_b)
    return x2.reshape(bn, s, d)


def kernel(x, rel_bias, l0_w_in, l0_sinks, l0_cmpk_pos, l0_cmpk_w1, l0_cmpk_w2, l0_cmpv_pos, l0_cmpv_w1,
           l0_cmpv_w2, l0_w_out, l0_ln1_g, l0_ln1_b, l0_ffn_gate, l0_ffn_up, l0_ffn_down, l0_ln2_g, l0_ln2_b,
           l1_w_in, l1_w_out, l1_ln1_g, l1_ln1_b, l1_router, l1_exp_gate, l1_exp_up, l1_exp_down, l1_ln2_g,
           l1_ln2_b):
    table_t = rel_bias.T.astype(f32)
    x, near, far = _even_layer(x, table_t, l0_w_in, l0_sinks, l0_cmpk_pos, l0_cmpk_w1, l0_cmpk_w2, l0_cmpv_pos,
                               l0_cmpv_w1, l0_cmpv_w2, l0_w_out, l0_ln1_g, l0_ln1_b, l0_ffn_gate, l0_ffn_up,
                               l0_ffn_down, l0_ln2_g, l0_ln2_b)
    return _odd_layer(x, near, far, l1_w_in, l1_w_out, l1_ln1_g, l1_ln1_b, l1_router, l1_exp_gate, l1_exp_up,
                      l1_exp_down, l1_ln2_g, l1_ln2_b)
```

```python
import functools
import math

import numpy as np
import jax
import jax.numpy as jnp
from jax import lax
from jax.experimental import pallas as pl
from jax.experimental.pallas import tpu as pltpu

f32 = jnp.float32
bf16 = jnp.bfloat16
i32 = jnp.int32

D_MODEL = 1024
HEAD_DIM = 64
NUM_BUCKETS = 32
MAX_DISTANCE = 128
A_HEADS, A_KV, A_WINDOW = 8, 2, 128
B_HEADS, B_KV, B_WINDOW = 8, 2, 512
CMP_BLOCK, CMP_STRIDE = 32, 16
SLC_BLOCK, SLC_TOPN = 64, 16
C_HEADS, IDX_HEADS, IDX_DIM, C_TOPK_MAX = 16, 8, 32, 256
D_FF, N_EXPERTS = 2816, 8
DEPTH = 2
ALPHA = (2.0 * DEPTH) ** 0.25
LN_EPS = 1e-5
NEG = -1e30
POS_BIG = 1e30
INT_MIN = -(2 ** 31)

TQ = 128
FF_CHUNK = 256
VMEM_LIMIT = 56 * 1024 * 1024


def _cparams(sem):
    return pltpu.CompilerParams(dimension_semantics=sem, vmem_limit_bytes=VMEM_LIMIT)


def _bucket_table(max_dist):
    n = np.arange(max_dist + 1)
    max_exact = NUM_BUCKETS // 2
    nf = np.maximum(n, 1).astype(np.float64)
    large = max_exact + (np.log(nf / max_exact) / math.log(MAX_DISTANCE / max_exact)
                         * (NUM_BUCKETS - max_exact)).astype(np.int64)
    large = np.minimum(large, NUM_BUCKETS - 1)
    return np.where(n < max_exact, n, large).astype(np.int32)


def _bucket_of(dist):
    dist = np.maximum(dist, 0)
    return _bucket_table(int(dist.max()))[dist]


def _mm_kernel(x_ref, w_ref, o_ref):
    o_ref[...] = jnp.dot(x_ref[...].astype(bf16), w_ref[...], preferred_element_type=f32).astype(o_ref.dtype)


def _matmul(x, w, out_dtype, tm=512, tn=None):
    m, k = x.shape
    n = w.shape[1]
    tn = n if tn is None else tn
    return pl.pallas_call(
        _mm_kernel,
        grid=(m // tm, n // tn),
        in_specs=[pl.BlockSpec((tm, k), lambda i, j: (i, 0)),
                  pl.BlockSpec((k, tn), lambda i, j: (0, j))],
        out_specs=pl.BlockSpec((tm, tn), lambda i, j: (i, j)),
        out_shape=jax.ShapeDtypeStruct((m, n), out_dtype),
        compiler_params=_cparams(("parallel", "arbitrary")),
        name="matmul",
    )(x, w)


def _bias_expand_kernel(tab_ref, idx_ref, o_ref):
    h = pl.program_id(0)
    idx = idx_ref[...]
    acc = jnp.zeros(idx.shape, f32)
    for b in range(NUM_BUCKETS):
        acc = jnp.where(idx == b, tab_ref[h, b], acc)
    o_ref[0] = acc


def _bias_expand(table_t, idx, tr):
    nh = table_t.shape[0]
    r, c = idx.shape
    return pl.pallas_call(
        _bias_expand_kernel,
        grid=(nh, r // tr),
        in_specs=[pl.BlockSpec(memory_space=pltpu.SMEM),
                  pl.BlockSpec((tr, c), lambda h, i: (i, 0))],
        out_specs=pl.BlockSpec((1, tr, c), lambda h, i: (h, i, 0)),
        out_shape=jax.ShapeDtypeStruct((nh, r, c), f32),
        compiler_params=_cparams(("parallel", "arbitrary")),
        name="bias_expand",
    )(table_t, jnp.asarray(idx, dtype=i32))


def _layer_norm(z, g, b):
    mu = jnp.mean(z, axis=-1, keepdims=True)
    zc = z - mu
    var = jnp.mean(zc * zc, axis=-1, keepdims=True)
    return zc * lax.rsqrt(var + LN_EPS) * g + b


def _proj_ln_kernel(mix_ref, w_ref, x_ref, g_ref, b_ref, o_ref):
    y = jnp.dot(mix_ref[...], w_ref[...], preferred_element_type=f32)
    o_ref[...] = _layer_norm(ALPHA * x_ref[...] + y, g_ref[...], b_ref[...])


def _proj_res_ln(mix, w, x, g, b, tm=512):
    m, k = mix.shape
    d = w.shape[1]
    return pl.pallas_call(
        _proj_ln_kernel,
        grid=(m // tm,),
        in_specs=[pl.BlockSpec((tm, k), lambda i: (i, 0)),
                  pl.BlockSpec((k, d), lambda i: (0, 0)),
                  pl.BlockSpec((tm, d), lambda i: (i, 0)),
                  pl.BlockSpec((1, d), lambda i: (0, 0)),
                  pl.BlockSpec((1, d), lambda i: (0, 0))],
        out_specs=pl.BlockSpec((tm, d), lambda i: (i, 0)),
        out_shape=jax.ShapeDtypeStruct((m, d), f32),
        compiler_params=_cparams(("parallel",)),
        name="proj_res_ln",
    )(mix, w, x, g.reshape(1, d), b.reshape(1, d))


def _swiglu_accumulate(xb, wg_ref, wu_ref, wd_ref, acc_ref, row_scale, widx):
    d_ff = wg_ref.shape[-1]
    for c in range(d_ff // FF_CHUNK):
        cols = slice(c * FF_CHUNK, (c + 1) * FF_CHUNK)
        gate = jnp.dot(xb, wg_ref[widx + (slice(None), cols)], preferred_element_type=f32)
        up = jnp.dot(xb, wu_ref[widx + (slice(None), cols)], preferred_element_type=f32)
        h = jax.nn.silu(gate) * up
        if row_scale is not None:
            h = h * row_scale
        acc_ref[...] += jnp.dot(h.astype(bf16), wd_ref[widx + (cols, slice(None))], preferred_element_type=f32)


def _ffn_kernel(x_ref, wg_ref, wu_ref, wd_ref, g_ref, b_ref, o_ref, acc_ref):
    x = x_ref[...]
    acc_ref[...] = jnp.zeros_like(acc_ref)
    _swiglu_accumulate(x.astype(bf16), wg_ref, wu_ref, wd_ref, acc_ref, None, ())
    o_ref[...] = _layer_norm(ALPHA * x + acc_ref[...], g_ref[...], b_ref[...])


def _ffn_res_ln(x, wg, wu, wd, g, b, tm=512):
    m, d = x.shape
    d_ff = wg.shape[1]
    return pl.pallas_call(
        _ffn_kernel,
        grid=(m // tm,),
        in_specs=[pl.BlockSpec((tm, d), lambda i: (i, 0)),
                  pl.BlockSpec((d, d_ff), lambda i: (0, 0)),
                  pl.BlockSpec((d, d_ff), lambda i: (0, 0)),
                  pl.BlockSpec((d_ff, d), lambda i: (0, 0)),
                  pl.BlockSpec((1, d), lambda i: (0, 0)),
                  pl.BlockSpec((1, d), lambda i: (0, 0))],
        out_specs=pl.BlockSpec((tm, d), lambda i: (i, 0)),
        out_shape=jax.ShapeDtypeStruct((m, d), f32),
        scratch_shapes=[pltpu.VMEM((tm, d), f32)],
        compiler_params=_cparams(("parallel",)),
        name="ffn_res_ln",
    )(x, wg, wu, wd, g.reshape(1, d), b.reshape(1, d))


def _moe_kernel(x_ref, r_ref, wg_ref, wu_ref, wd_ref, g_ref, b_ref, o_ref, acc_ref, comb_ref, xb_ref):
    e = pl.program_id(1)
    lane = lax.broadcasted_iota(i32, comb_ref.shape, 1)

    @pl.when(e == 0)
    def _():
        x = x_ref[...]
        xb_ref[...] = x.astype(bf16)
        acc_ref[...] = jnp.zeros_like(acc_ref)
        logits = jnp.dot(x, r_ref[...], preferred_element_type=f32)
        logits = jnp.where(lane < N_EXPERTS, logits, -jnp.inf)
        m1 = jnp.max(logits, axis=-1, keepdims=True)
        i1 = jnp.min(jnp.where(logits == m1, lane, 128), axis=-1, keepdims=True)
        rest = jnp.where(lane == i1, -jnp.inf, logits)
        m2 = jnp.max(rest, axis=-1, keepdims=True)
        i2 = jnp.min(jnp.where(rest == m2, lane, 128), axis=-1, keepdims=True)
        e2 = jnp.exp(m2 - m1)
        g1 = 1.0 / (1.0 + e2)
        comb_ref[...] = jnp.where(lane == i1, g1, 0.0) + jnp.where(lane == i2, e2 * g1, 0.0)

    c = jnp.sum(jnp.where(lane == e, comb_ref[...], 0.0), axis=-1, keepdims=True)
    _swiglu_accumulate(xb_ref[...], wg_ref, wu_ref, wd_ref, acc_ref, c, (0,))

    @pl.when(e == N_EXPERTS - 1)
    def _():
        o_ref[...] = _layer_norm(ALPHA * x_ref[...] + acc_ref[...], g_ref[...], b_ref[...])


def _moe_res_ln(x, router, wg, wu, wd, g, b, tm=512):
    m, d = x.shape
    ne, _, d_ff = wg.shape
    router_p = jnp.pad(router, ((0, 0), (0, 128 - ne)))
    return pl.pallas_call(
        _moe_kernel,
        grid=(m // tm, ne),
        in_specs=[pl.BlockSpec((tm, d), lambda i, e: (i, 0)),
                  pl.BlockSpec((d, 128), lambda i, e: (0, 0)),
                  pl.BlockSpec((1, d, d_ff), lambda i, e: (e, 0, 0)),
                  pl.BlockSpec((1, d, d_ff), lambda i, e: (e, 0, 0)),
                  pl.BlockSpec((1, d_ff, d), lambda i, e: (e, 0, 0)),
                  pl.BlockSpec((1, d), lambda i, e: (0, 0)),
                  pl.BlockSpec((1, d), lambda i, e: (0, 0))],
        out_specs=pl.BlockSpec((tm, d), lambda i, e: (i, 0)),
        out_shape=jax.ShapeDtypeStruct((m, d), f32),
        scratch_shapes=[pltpu.VMEM((tm, d), f32), pltpu.VMEM((tm, 128), f32), pltpu.VMEM((tm, d), bf16)],
        compiler_params=_cparams(("parallel", "arbitrary")),
        name="moe_res_ln",
    )(x, router_p, wg, wu, wd, g.reshape(1, d), b.reshape(1, d))


def _dot_nt(a, b):
    return lax.dot_general(a, b, (((1,), (1,)), ((), ())), preferred_element_type=f32)


def _banded_kernel(sinks_ref, q_ref, k_ref, v_ref, bias_ref, o_ref, *, rep, kw, pad, window, use_sinks):
    g = pl.program_id(1)
    i = pl.program_id(2)
    start = pl.multiple_of(i * TQ, TQ)
    k = k_ref[0, 0, pl.ds(start, kw), :]
    v = v_ref[0, 0, pl.ds(start, kw), :]
    qi = lax.broadcasted_iota(i32, (TQ, kw), 0)
    ki = lax.broadcasted_iota(i32, (TQ, kw), 1)
    dist = qi + pad - ki
    valid = (dist >= 0) & (dist < window) & (start - pad + ki >= 0)
    scale = HEAD_DIM ** -0.5
    outs = []
    for r in range(rep):
        q = q_ref[0, :, r * HEAD_DIM:(r + 1) * HEAD_DIM]
        s = _dot_nt(q, k) * scale + bias_ref[r]
        s = jnp.where(valid, s, NEG)
        m = jnp.max(s, axis=-1, keepdims=True)
        if use_sinks:
            sk = sinks_ref[g * rep + r]
            m = jnp.maximum(m, sk)
        e = jnp.exp(s - m)
        l = jnp.sum(e, axis=-1, keepdims=True)
        if use_sinks:
            l = l + jnp.exp(sk - m)
        outs.append(jnp.dot(e.astype(bf16), v, preferred_element_type=f32) / l)
    o_ref[0] = jnp.concatenate(outs, axis=-1).astype(o_ref.dtype)


def _banded_gqa(q, k, v, bias, sinks, window, out_dtype):
    bn, s, hq = q.shape
    g = k.shape[1]
    rep = hq // (g * HEAD_DIM)
    pad = -(-(window - 1) // TQ) * TQ
    kw = pad + TQ
    kp = jnp.pad(k, ((0, 0), (0, 0), (pad, 0), (0, 0)))
    vp = jnp.pad(v, ((0, 0), (0, 0), (pad, 0), (0, 0)))
    use_sinks = sinks is not None
    sinks_arr = sinks.astype(f32) if use_sinks else jnp.zeros((g * rep,), f32)
    kern = functools.partial(_banded_kernel, rep=rep, kw=kw, pad=pad, window=window, use_sinks=use_sinks)
    return pl.pallas_call(
        kern,
        grid=(bn, g, s // TQ),
        in_specs=[pl.BlockSpec(memory_space=pltpu.SMEM),
                  pl.BlockSpec((1, TQ, rep * HEAD_DIM), lambda b, gg, i: (b, i, gg)),
                  pl.BlockSpec((1, 1, pad + s, HEAD_DIM), lambda b, gg, i: (b, gg, 0, 0)),
                  pl.BlockSpec((1, 1, pad + s, HEAD_DIM), lambda b, gg, i: (b, gg, 0, 0)),
                  pl.BlockSpec((rep, TQ, kw), lambda b, gg, i: (gg, 0, 0))],
        out_specs=pl.BlockSpec((1, TQ, rep * HEAD_DIM), lambda b, gg, i: (b, i, gg)),
        out_shape=jax.ShapeDtypeStruct((bn, s, hq), out_dtype),
        compiler_params=_cparams(("parallel", "parallel", "arbitrary")),
        name="banded_gqa_w%d" % window,
    )(sinks_arr, q, kp, vp, bias)


def _compress_kernel(blk_ref, pos_ref, w1_ref, w2_ref, o_ref):
    xb = (blk_ref[0].astype(f32) + pos_ref[...]).astype(bf16)
    h = jax.nn.gelu(jnp.dot(xb, w1_ref[...], preferred_element_type=f32))
    o_ref[0] = jnp.dot(h.astype(bf16), w2_ref[...], preferred_element_type=f32).astype(o_ref.dtype)


def _nsa_compress(kv, pos, w1, w2):
    bn, g, s, dh = kv.shape
    nc = s // CMP_STRIDE
    chunks = kv.reshape(bn, g, nc, CMP_STRIDE * dh)
    nxt = jnp.pad(chunks[:, :, 1:], ((0, 0), (0, 0), (0, 1), (0, 0)))
    blocks = jnp.concatenate([chunks, nxt], axis=-1).reshape(bn * g, nc, CMP_BLOCK * dh)
    cin = CMP_BLOCK * dh
    hid = w1.shape[1]
    out = pl.pallas_call(
        _compress_kernel,
        grid=(bn * g,),
        in_specs=[pl.BlockSpec((1, nc, cin), lambda i: (i, 0, 0)),
                  pl.BlockSpec((1, cin), lambda i: (0, 0)),
                  pl.BlockSpec((cin, hid), lambda i: (0, 0)),
                  pl.BlockSpec((hid, dh), lambda i: (0, 0))],
        out_specs=pl.BlockSpec((1, nc, dh), lambda i: (i, 0, 0)),
        out_shape=jax.ShapeDtypeStruct((bn * g, nc, dh), bf16),
        compiler_params=_cparams(("parallel",)),
        name="nsa_compress",
    )(blocks, pos.reshape(1, cin), w1.astype(bf16), w2.astype(bf16))
    return out.reshape(bn, g, nc, dh)


def _flash_step(q_rows, ks, vs, biases, mask, m_ref, l_ref, acc_ref, row0):
    nh = len(biases)
    s = _dot_nt(q_rows, ks) * (HEAD_DIM ** -0.5)
    ps = []
    for r in range(nh):
        rows = pl.ds(row0 + r * TQ, TQ)
        sr = s[r * TQ:(r + 1) * TQ] + biases[r]
        sr = jnp.where(mask, sr, NEG)
        m_old = m_ref[rows, :]
        m_new = jnp.maximum(m_old, jnp.max(sr, axis=-1, keepdims=True))
        p = jnp.where(mask, jnp.exp(sr - m_new), 0.0)
        alpha = jnp.exp(m_old - m_new)
        l_ref[rows, :] = alpha * l_ref[rows, :] + jnp.sum(p, axis=-1, keepdims=True)
        acc_ref[rows, :] = alpha * acc_ref[rows, :]
        m_ref[rows, :] = m_new
        ps.append(p.astype(bf16))
    rows = pl.ds(row0, nh * TQ)
    acc_ref[rows, :] += jnp.dot(jnp.concatenate(ps, axis=0), vs, preferred_element_type=f32)


def _nsa_kernel_rowmajor(far_ref, q_ref, kc_ref, vc_ref, biasc_ref, ov_ref, exp_ref, ks_ref, vs_ref, near_ref,
                gates_ref, owin_ref, o_ref, m_ref, l_ref, acc_ref, *, rep, n_sel):
    g = pl.program_id(1)
    i = pl.program_id(2)
    t0 = i * TQ
    nc = kc_ref.shape[2]
    nblk = ov_ref.shape[1]
    scale = HEAD_DIM ** -0.5
    qall = jnp.concatenate([q_ref[0, :, r * HEAD_DIM:(r + 1) * HEAD_DIM] for r in range(rep)], axis=0)

    tpos = t0 + lax.broadcasted_iota(i32, (TQ, nc), 0)
    cmp_end = lax.broadcasted_iota(i32, (TQ, nc), 1) * CMP_STRIDE + (CMP_BLOCK - 1)
    valid_c = tpos >= cmp_end
    s_c = _dot_nt(qall, kc_ref[0, 0]) * scale
    p_rows = []
    p_sum = jnp.zeros((TQ, nc), f32)
    for r in range(rep):
        sr = jnp.where(valid_c, s_c[r * TQ:(r + 1) * TQ] + biasc_ref[r], NEG)
        m = jnp.max(sr, axis=-1, keepdims=True)
        e = jnp.where(valid_c, jnp.exp(sr - m), 0.0)
        l = jnp.sum(e, axis=-1, keepdims=True)
        p = e * (1.0 / jnp.maximum(l, 1e-30))
        p_rows.append(p.astype(bf16))
        p_sum = p_sum + p
    o_cmp = jnp.dot(jnp.concatenate(p_rows, axis=0), vc_ref[0, 0], preferred_element_type=f32)

    imp = jnp.dot(p_sum, ov_ref[...], preferred_element_type=f32)
    blk = lax.broadcasted_iota(i32, (TQ, nblk), 1)
    tq = t0 + lax.broadcasted_iota(i32, (TQ, nblk), 0)
    cur = tq // SLC_BLOCK
    forced = (blk == 0) | (blk == cur) | (blk == cur - 1)
    score = jnp.where(forced, POS_BIG, jnp.where(blk * SLC_BLOCK <= tq, imp, NEG))
    rank = jnp.zeros((TQ, nblk), i32)
    for kk in range(nblk):
        ck = score[:, kk:kk + 1]
        later = jnp.where(blk > kk, 1, 0)
        rank = rank + jnp.where(ck > score, 1, jnp.where(ck == score, later, 0))
    sel = jnp.where(rank < n_sel, 1.0, 0.0).astype(bf16)

    m_ref[...] = jnp.full(m_ref.shape, NEG, f32)
    l_ref[...] = jnp.zeros(l_ref.shape, f32)
    acc_ref[...] = jnp.zeros(acc_ref.shape, f32)
    fars = [far_ref[g * rep + r] for r in range(rep)]
    row = lax.broadcasted_iota(i32, (TQ, TQ), 0)
    col = lax.broadcasted_iota(i32, (TQ, TQ), 1)

    def tile_mask(j):
        return jnp.dot(sel, exp_ref[j], preferred_element_type=f32) > 0.5

    def kv_tile(j):
        rows = pl.ds(pl.multiple_of(j * TQ, TQ), TQ)
        return ks_ref[0, 0, rows, :], vs_ref[0, 0, rows, :]

    def far_body(j, carry):
        ks, vs = kv_tile(j)
        _flash_step(qall, ks, vs, fars, tile_mask(j), m_ref, l_ref, acc_ref, 0)
        return carry

    lax.fori_loop(0, jnp.maximum(i - 1, 0), far_body, 0)

    @pl.when(i >= 1)
    def _():
        ks, vs = kv_tile(i - 1)
        _flash_step(qall, ks, vs, [near_ref[r, :, 0:TQ] for r in range(rep)], tile_mask(i - 1),
                    m_ref, l_ref, acc_ref, 0)

    ks, vs = kv_tile(i)
    diag_mask = (row >= col) & tile_mask(i)
    _flash_step(qall, ks, vs, [near_ref[r, :, TQ:2 * TQ] for r in range(rep)], diag_mask, m_ref, l_ref, acc_ref, 0)
    o_slc = acc_ref[...] / l_ref[...]

    sig = jax.nn.sigmoid(gates_ref[0, 0])
    outs = []
    for r in range(rep):
        rows = slice(r * TQ, (r + 1) * TQ)
        o_win = owin_ref[0, :, r * HEAD_DIM:(r + 1) * HEAD_DIM]
        outs.append(sig[:, r:r + 1] * o_cmp[rows] + sig[:, rep + r:rep + r + 1] * o_slc[rows]
                    + sig[:, 2 * rep + r:2 * rep + r + 1] * o_win)
    o_ref[0] = jnp.concatenate(outs, axis=-1).astype(o_ref.dtype)


def _nsa_mix_rowmajor(q, kc, vc, bias_c, ks, vs, near, far, gates, o_win):
    bn, s, hq = q.shape
    g = kc.shape[1]
    nc = kc.shape[2]
    rep = hq // (g * HEAD_DIM)
    nblk = s // SLC_BLOCK
    n_sel = min(SLC_TOPN, nblk)
    nt = s // TQ
    ci = np.arange(nc)[:, None] * CMP_STRIDE
    sj = np.arange(nblk)[None, :] * SLC_BLOCK
    overlap = np.clip(np.minimum(ci + CMP_BLOCK, sj + SLC_BLOCK) - np.maximum(ci, sj), 0, None) / CMP_BLOCK
    overlap[nc - 1] = 0.0
    expand = (np.arange(nblk)[None, :, None] == (np.arange(nt)[:, None, None] * TQ
                                                 + np.arange(TQ)[None, None, :]) // SLC_BLOCK)
    kern = functools.partial(_nsa_kernel_rowmajor, rep=rep, n_sel=n_sel)
    rw = rep * HEAD_DIM
    return pl.pallas_call(
        kern,
        grid=(bn, g, nt),
        in_specs=[pl.BlockSpec(memory_space=pltpu.SMEM),
                  pl.BlockSpec((1, TQ, rw), lambda b, gg, i: (b, i, gg)),
                  pl.BlockSpec((1, 1, nc, HEAD_DIM), lambda b, gg, i: (b, gg, 0, 0)),
                  pl.BlockSpec((1, 1, nc, HEAD_DIM), lambda b, gg, i: (b, gg, 0, 0)),
                  pl.BlockSpec((rep, TQ, nc), lambda b, gg, i: (gg, i, 0)),
                  pl.BlockSpec((nc, nblk), lambda b, gg, i: (0, 0)),
                  pl.BlockSpec((nt, nblk, TQ), lambda b, gg, i: (0, 0, 0)),
                  pl.BlockSpec((1, 1, s, HEAD_DIM), lambda b, gg, i: (b, gg, 0, 0)),
                  pl.BlockSpec((1, 1, s, HEAD_DIM), lambda b, gg, i: (b, gg, 0, 0)),
                  pl.BlockSpec((rep, TQ, 2 * TQ), lambda b, gg, i: (gg, 0, 0)),
                  pl.BlockSpec((1, 1, TQ, 3 * rep), lambda b, gg, i: (b, gg, i, 0)),
                  pl.BlockSpec((1, TQ, rw), lambda b, gg, i: (b, i, gg))],
        out_specs=pl.BlockSpec((1, TQ, rw), lambda b, gg, i: (b, i, gg)),
        out_shape=jax.ShapeDtypeStruct((bn, s, hq), bf16),
        scratch_shapes=[pltpu.VMEM((rep * TQ, 1), f32), pltpu.VMEM((rep * TQ, 1), f32),
                        pltpu.VMEM((rep * TQ, HEAD_DIM), f32)],
        compiler_params=_cparams(("parallel", "parallel", "arbitrary")),
        name="nsa_mix",
    )(far, q, kc, vc, bias_c, jnp.asarray(overlap, f32), jnp.asarray(expand, bf16), ks, vs, near, gates, o_win)


def _dsa_kernel_rowmajor(far_ref, q_ref, k_ref, v_ref, qi_ref, ki_ref, wi_ref, near_ref, tri_ref, o_ref,
                         keys_ref, mask_ref, m_ref, l_ref, acc_ref, *, topk, head_chunk):
    i = pl.program_id(1)
    t0 = i * TQ
    nh = q_ref.shape[2] // HEAD_DIM
    nih = qi_ref.shape[2] // IDX_DIM
    row = lax.broadcasted_iota(i32, (TQ, TQ), 0)
    col = lax.broadcasted_iota(i32, (TQ, TQ), 1)
    n_sel_tiles = jnp.maximum(i + 1, -(-topk // TQ))

    qi_all = jnp.concatenate([qi_ref[0, :, h * IDX_DIM:(h + 1) * IDX_DIM] for h in range(nih)], axis=0)
    wi = wi_ref[0]

    def score_body(j, carry):
        kt = ki_ref[0, pl.ds(pl.multiple_of(j * TQ, TQ), TQ), :]
        rel = jnp.maximum(_dot_nt(qi_all, kt), 0.0)
        sc = jnp.zeros((TQ, TQ), f32)
        for h in range(nih):
            sc = sc + wi[:, h:h + 1] * rel[h * TQ:(h + 1) * TQ]
        sc = jnp.where(j * TQ + col <= t0 + row, sc, NEG)
        sc = jnp.where(sc == 0.0, 0.0, sc)
        bits = pltpu.bitcast(sc, i32)
        keys_ref[j] = jnp.where(bits < 0, bits ^ 0x7FFFFFFF, bits)
        return carry

    lax.fori_loop(0, n_sel_tiles, score_body, 0)

    def count(pred_fn):
        def body(j, cnt):
            return cnt + jnp.where(pred_fn(keys_ref[j]), 1.0, 0.0)
        cnt = lax.fori_loop(0, n_sel_tiles, body, jnp.zeros((TQ, TQ), f32))
        return jnp.broadcast_to(jnp.sum(cnt, axis=-1, keepdims=True), (TQ, TQ))

    def bit_body(t, prefix):
        cand = prefix ^ jnp.left_shift(jnp.int32(1), 31 - t)
        return jnp.where(count(lambda kj: kj >= cand) >= topk, cand, prefix)

    thr = lax.fori_loop(0, 32, bit_body, jnp.full((TQ, TQ), INT_MIN, i32))
    need = topk - count(lambda kj: kj > thr)

    def mask_body(j, running):
        kj = keys_ref[j]
        tie = jnp.where(kj == thr, 1.0, 0.0)
        before = jnp.dot(tie.astype(bf16), tri_ref[...], preferred_element_type=f32) + running
        take = jnp.where(kj > thr, 1.0, jnp.where(before < need, tie, 0.0))
        mask_ref[j] = jnp.where(j * TQ + col <= t0 + row, take, 0.0)
        return running + jnp.broadcast_to(jnp.sum(tie, axis=-1, keepdims=True), (TQ, TQ))

    lax.fori_loop(0, i + 1, mask_body, jnp.zeros((TQ, TQ), f32))

    m_ref[...] = jnp.full(m_ref.shape, NEG, f32)
    l_ref[...] = jnp.zeros(l_ref.shape, f32)
    acc_ref[...] = jnp.zeros(acc_ref.shape, f32)
    qall = jnp.concatenate([q_ref[0, :, h * HEAD_DIM:(h + 1) * HEAD_DIM] for h in range(nh)], axis=0)
    hc = head_chunk

    def attend(j, bias_fn):
        rows = pl.ds(pl.multiple_of(j * TQ, TQ), TQ)
        ks = k_ref[0, rows, :]
        vs = v_ref[0, rows, :]
        mask = mask_ref[j] > 0.5
        for c in range(nh // hc):
            _flash_step(qall[c * hc * TQ:(c + 1) * hc * TQ], ks, vs, [bias_fn(c * hc + r) for r in range(hc)],
                        mask, m_ref, l_ref, acc_ref, c * hc * TQ)

    def far_body(j, carry):
        attend(j, lambda h: far_ref[h])
        return carry

    lax.fori_loop(0, jnp.maximum(i - 1, 0), far_body, 0)

    @pl.when(i >= 1)
    def _():
        attend(i - 1, lambda h: near_ref[h, :, 0:TQ])

    attend(i, lambda h: near_ref[h, :, TQ:2 * TQ])
    o = acc_ref[...] / l_ref[...]
    o_ref[0] = jnp.concatenate([o[h * TQ:(h + 1) * TQ] for h in range(nh)], axis=-1).astype(o_ref.dtype)


def _dsa_attention_rowmajor(q, k, v, qi, ki, wi, near, far):
    bn, s, hq = q.shape
    nh = hq // HEAD_DIM
    nt = s // TQ
    topk = min(C_TOPK_MAX, s // 4)
    tri = np.triu(np.ones((TQ, TQ), np.float32), 1)
    kern = functools.partial(_dsa_kernel_rowmajor, topk=topk, head_chunk=4)
    nt_scr = max(nt, -(-topk // TQ))
    return pl.pallas_call(
        kern,
        grid=(bn, nt),
        in_specs=[pl.BlockSpec(memory_space=pltpu.SMEM),
                  pl.BlockSpec((1, TQ, hq), lambda b, i: (b, i, 0)),
                  pl.BlockSpec((1, s, HEAD_DIM), lambda b, i: (b, 0, 0)),
                  pl.BlockSpec((1, s, HEAD_DIM), lambda b, i: (b, 0, 0)),
                  pl.BlockSpec((1, TQ, qi.shape[2]), lambda b, i: (b, i, 0)),
                  pl.BlockSpec((1, s, IDX_DIM), lambda b, i: (b, 0, 0)),
                  pl.BlockSpec((1, TQ, wi.shape[2]), lambda b, i: (b, i, 0)),
                  pl.BlockSpec((nh, TQ, 2 * TQ), lambda b, i: (0, 0, 0)),
                  pl.BlockSpec((TQ, TQ), lambda b, i: (0, 0))],
        out_specs=pl.BlockSpec((1, TQ, hq), lambda b, i: (b, i, 0)),
        out_shape=jax.ShapeDtypeStruct((bn, s, hq), bf16),
        scratch_shapes=[pltpu.VMEM((nt_scr, TQ, TQ), i32), pltpu.VMEM((nt, TQ, TQ), f32),
                        pltpu.VMEM((nh * TQ, 1), f32), pltpu.VMEM((nh * TQ, 1), f32),
                        pltpu.VMEM((nh * TQ, HEAD_DIM), f32)],
        compiler_params=_cparams(("parallel", "arbitrary")),
        name="dsa_attention",
    )(far, q, k, v, qi, ki, wi, near, jnp.asarray(tri, bf16))


V_ROWS = HEAD_DIM + 16


def _masked_flash(i, qs_ref, key_tile, value_tile, mask_bias, bias_ref, m_ref, acc_ref, s_ref, mt_ref, al_ref, chunk):
    width = qs_ref.shape[1]
    m_ref[...] = jnp.full(m_ref.shape, NEG, f32)
    acc_ref[...] = jnp.zeros(acc_ref.shape, f32)

    def scores(j):
        slot = j % 2
        kt = key_tile(j)
        kind = jnp.clip(j - (i - 2), 0, 2)
        maskb = jnp.concatenate([mask_bias(j)] * (chunk // TQ), axis=1)
        for c in range(width // chunk):
            cols = slice(c * chunk, (c + 1) * chunk)
            s = jnp.dot(kt, qs_ref[:, cols], preferred_element_type=f32) + bias_ref[kind, :, cols] + maskb
            s_ref[slot, :, cols] = s
            m_old = m_ref[:, cols]
            m_new = jnp.maximum(m_old, jnp.max(s, axis=0, keepdims=True))
            m_ref[:, cols] = m_new
            mt_ref[slot, :, cols] = m_new
            al_ref[slot, :, cols] = jnp.exp(m_old - m_new)

    def values(j):
        slot = j % 2
        vt = value_tile(j)
        for c in range(width // chunk):
            cols = slice(c * chunk, (c + 1) * chunk)
            p = jnp.exp(s_ref[slot, :, cols] - mt_ref[slot, :, cols])
            acc_ref[:, cols] = (al_ref[slot, :, cols] * acc_ref[:, cols]
                                + jnp.dot(vt, p.astype(bf16), preferred_element_type=f32))

    scores(0)

    def pipe_body(j, carry):
        values(j - 1)
        scores(j)
        return carry

    lax.fori_loop(1, i + 1, pipe_body, 0)
    values(i)


def _flash_bias_tiles(near, far, groups):
    nh = near.shape[0]
    per = nh // groups
    width = per * TQ
    neart = near.reshape(groups, per, TQ, 2, TQ).transpose(0, 3, 4, 1, 2).reshape(groups, 2, TQ, width)
    causal = np.where(np.arange(TQ)[:, None] <= np.arange(TQ)[None, :], 0.0, NEG).astype(np.float32)
    far_tile = jnp.broadcast_to(jnp.repeat(far, TQ).reshape(groups, 1, width), (groups, TQ, width))
    return jnp.stack([far_tile, neart[:, 0], neart[:, 1] + jnp.asarray(np.tile(causal, (1, per)))], axis=1)


def _value_tiles(v):
    lead = v.shape[:-2]
    nt = v.shape[-2] // TQ
    vt = jnp.swapaxes(v.reshape(lead + (nt, TQ, HEAD_DIM)), -1, -2)
    return jnp.concatenate([vt, jnp.ones(lead + (nt, 1, TQ), vt.dtype),
                            jnp.zeros(lead + (nt, V_ROWS - HEAD_DIM - 1, TQ), vt.dtype)], axis=-2)


def _nsa_kernel(qt_ref, kc_ref, vct_ref, biasc_ref, ovt_ref, ks_ref, vst_ref, bias_ref, gt_ref, gn_ref, owin_ref,
                o_ref, qs_ref, selb_ref, m_ref, acc_ref, s_ref, mt_ref, al_ref, *, rep, n_sel):
    i = pl.program_id(2)
    t0 = i * TQ
    nc = kc_ref.shape[2]
    nblk = ovt_ref.shape[0]
    qs_ref[...] = (qt_ref[0, 0, 0].astype(f32) * (HEAD_DIM ** -0.5)).astype(bf16)

    s_c = jnp.dot(kc_ref[0, 0], qs_ref[...], preferred_element_type=f32)
    cmp_end = lax.broadcasted_iota(i32, (nc, TQ), 0) * CMP_STRIDE + (CMP_BLOCK - 1)
    valid_c = t0 + lax.broadcasted_iota(i32, (nc, TQ), 1) >= cmp_end
    p_cols = []
    p_sum = jnp.zeros((nc, TQ), f32)
    for r in range(rep):
        sr = jnp.where(valid_c, s_c[:, r * TQ:(r + 1) * TQ] + biasc_ref[r, 0], NEG)
        m = jnp.max(sr, axis=0, keepdims=True)
        e = jnp.where(valid_c, jnp.exp(sr - m), 0.0)
        l = jnp.sum(e, axis=0, keepdims=True)
        p = e * (1.0 / jnp.maximum(l, 1e-30))
        p_cols.append(p.astype(bf16))
        p_sum = p_sum + p
    o_cmp = jnp.dot(vct_ref[0, 0], jnp.concatenate(p_cols, axis=1), preferred_element_type=f32)

    imp = jnp.dot(ovt_ref[...], p_sum, preferred_element_type=f32)
    blk = lax.broadcasted_iota(i32, (nblk, TQ), 0)
    tq = t0 + lax.broadcasted_iota(i32, (nblk, TQ), 1)
    cur = tq // SLC_BLOCK
    forced = (blk == 0) | (blk == cur) | (blk == cur - 1)
    score = jnp.where(forced, POS_BIG, jnp.where(blk * SLC_BLOCK <= tq, imp, NEG))
    rank = jnp.zeros((nblk, TQ), i32)
    for kk in range(nblk):
        ck = score[kk:kk + 1, :]
        rank = rank + jnp.where(ck > score, 1, jnp.where(ck == score, jnp.where(blk > kk, 1, 0), 0))
    selb_ref[...] = jnp.where(rank < n_sel, 0.0, NEG)

    per_tile = TQ // SLC_BLOCK

    def mask_bias(j):
        rows = [jnp.broadcast_to(selb_ref[pl.ds(per_tile * j + t, 1), :], (SLC_BLOCK, TQ)) for t in range(per_tile)]
        return jnp.concatenate(rows, axis=0)

    _masked_flash(i, qs_ref,
                  lambda j: ks_ref[0, 0, pl.ds(pl.multiple_of(j * TQ, TQ), TQ), :],
                  lambda j: vst_ref[0, 0, j],
                  mask_bias, bias_ref.at[0], m_ref, acc_ref, s_ref, mt_ref, al_ref, rep * TQ)
    acc = acc_ref[...]
    o_slc = acc[0:HEAD_DIM] / acc[HEAD_DIM:HEAD_DIM + 1]

    sig_t = jax.nn.sigmoid(gt_ref[0, 0, 0])
    sig_n = jax.nn.sigmoid(gn_ref[0, 0])
    outs = []
    for r in range(rep):
        cols = slice(r * TQ, (r + 1) * TQ)
        mixed = sig_t[r:r + 1] * o_cmp[:, cols] + sig_t[rep + r:rep + r + 1] * o_slc[:, cols]
        o_win = owin_ref[0, :, r * HEAD_DIM:(r + 1) * HEAD_DIM]
        outs.append(mixed.T + sig_n[:, 2 * rep + r:2 * rep + r + 1] * o_win)
    o_ref[0] = jnp.concatenate(outs, axis=1).astype(o_ref.dtype)


def _nsa_mix(q, kc, vc, bias_c, ks, vs, near, far, gates, o_win):
    bn, s, hq = q.shape
    g = kc.shape[1]
    nc = kc.shape[2]
    rep = hq // (g * HEAD_DIM)
    nblk = s // SLC_BLOCK
    n_sel = min(SLC_TOPN, nblk)
    nt = s // TQ
    width = rep * TQ
    ci = np.arange(nc)[:, None] * CMP_STRIDE
    sj = np.arange(nblk)[None, :] * SLC_BLOCK
    overlap = np.clip(np.minimum(ci + CMP_BLOCK, sj + SLC_BLOCK) - np.maximum(ci, sj), 0, None) / CMP_BLOCK
    overlap[nc - 1] = 0.0
    qt = q.reshape(bn, nt, TQ, g, rep, HEAD_DIM).transpose(0, 3, 1, 5, 4, 2).reshape(bn, g, nt, HEAD_DIM, width)
    gates_t = gates.reshape(bn, g, nt, TQ, 3 * rep).transpose(0, 1, 2, 4, 3)
    kern = functools.partial(_nsa_kernel, rep=rep, n_sel=n_sel)
    rw = rep * HEAD_DIM
    return pl.pallas_call(
        kern,
        grid=(bn, g, nt),
        in_specs=[pl.BlockSpec((1, 1, 1, HEAD_DIM, width), lambda b, gg, i: (b, gg, i, 0, 0)),
                  pl.BlockSpec((1, 1, nc, HEAD_DIM), lambda b, gg, i: (b, gg, 0, 0)),
                  pl.BlockSpec((1, 1, HEAD_DIM, nc), lambda b, gg, i: (b, gg, 0, 0)),
                  pl.BlockSpec((rep, 1, nc, TQ), lambda b, gg, i: (gg, i, 0, 0)),
                  pl.BlockSpec((nblk, nc), lambda b, gg, i: (0, 0)),
                  pl.BlockSpec((1, 1, s, HEAD_DIM), lambda b, gg, i: (b, gg, 0, 0)),
                  pl.BlockSpec((1, 1, nt, V_ROWS, TQ), lambda b, gg, i: (b, gg, 0, 0, 0)),
                  pl.BlockSpec((1, 3, TQ, width), lambda b, gg, i: (gg, 0, 0, 0)),
                  pl.BlockSpec((1, 1, 1, 3 * rep, TQ), lambda b, gg, i: (b, gg, i, 0, 0)),
                  pl.BlockSpec((1, 1, TQ, 3 * rep), lambda b, gg, i: (b, gg, i, 0)),
                  pl.BlockSpec((1, TQ, rw), lambda b, gg, i: (b, i, gg))],
        out_specs=pl.BlockSpec((1, TQ, rw), lambda b, gg, i: (b, i, gg)),
        out_shape=jax.ShapeDtypeStruct((bn, s, hq), bf16),
        scratch_shapes=[pltpu.VMEM((HEAD_DIM, width), bf16), pltpu.VMEM((nblk, TQ), f32),
                        pltpu.VMEM((1, width), f32), pltpu.VMEM((V_ROWS, width), f32),
                        pltpu.VMEM((2, TQ, width), f32), pltpu.VMEM((2, 1, width), f32),
                        pltpu.VMEM((2, 1, width), f32)],
        compiler_params=_cparams(("parallel", "parallel", "arbitrary")),
        name="nsa_mix",
    )(qt, kc, jnp.swapaxes(vc, 2, 3), bias_c, jnp.asarray(overlap.T, f32), ks, _value_tiles(vs),
      _flash_bias_tiles(near, far, g), gates_t, gates, o_win)


def _dsa_kernel(qt_ref, k_ref, vt_ref, qit_ref, ki_ref, wit_ref, bias_ref, tri_ref, o_ref,
                keys_ref, qs_ref, m_ref, acc_ref, s_ref, mt_ref, al_ref, *, topk, chunk):
    i = pl.program_id(1)
    t0 = i * TQ
    width = qt_ref.shape[3]
    nh = width // TQ
    nih = qit_ref.shape[3] // TQ
    krow = lax.broadcasted_iota(i32, (TQ, TQ), 0)
    qcol = lax.broadcasted_iota(i32, (TQ, TQ), 1)
    n_pairs = jnp.maximum(i // 2 + 1, -(-topk // (2 * TQ)))

    qit = qit_ref[0, 0]
    wit = wit_ref[0, 0]

    def score_tile(j):
        kt = ki_ref[0, pl.ds(pl.multiple_of(j * TQ, TQ), TQ), :]
        rel = jnp.maximum(jnp.dot(kt, qit, preferred_element_type=f32), 0.0)
        sc = jnp.zeros((TQ, TQ), f32)
        for h in range(nih):
            sc = sc + wit[h:h + 1, :] * rel[:, h * TQ:(h + 1) * TQ]
        sc = jnp.where(j * TQ + krow <= t0 + qcol, sc, NEG)
        sc = jnp.where(sc == 0.0, 0.0, sc)
        bits = pltpu.bitcast(sc, i32)
        keys_ref[j] = jnp.where(bits < 0, bits ^ 0x7FFFFFFF, bits)

    def score_body(jj, carry):
        score_tile(2 * jj)
        score_tile(2 * jj + 1)
        return carry

    lax.fori_loop(0, n_pairs, score_body, 0)

    def count(pred_fn):
        def body(jj, cnt):
            return (cnt + jnp.where(pred_fn(keys_ref[2 * jj]), 1.0, 0.0)
                    + jnp.where(pred_fn(keys_ref[2 * jj + 1]), 1.0, 0.0))
        cnt = lax.fori_loop(0, n_pairs, body, jnp.zeros((TQ, TQ), f32))
        return jnp.sum(cnt, axis=0, keepdims=True)

    def bit_body(t, prefix):
        cand = prefix ^ jnp.left_shift(jnp.int32(1), 31 - t)
        return jnp.where(count(lambda kj: kj >= cand) >= topk, cand, prefix)

    thr = lax.fori_loop(0, 32, bit_body, jnp.full((1, TQ), INT_MIN, i32))
    n_ge = count(lambda kj: kj >= thr)

    @pl.when(jnp.max(n_ge) > topk)
    def _():
        need = topk - count(lambda kj: kj > thr)

        def fix_body(j, running):
            kj = keys_ref[j]
            tie = jnp.where(kj == thr, 1.0, 0.0)
            before = jnp.dot(tri_ref[...], tie.astype(bf16), preferred_element_type=f32) + running
            keys_ref[j] = jnp.where(before >= need, jnp.where(kj == thr, kj - 1, kj), kj)
            return running + jnp.sum(tie, axis=0, keepdims=True)

        lax.fori_loop(0, 2 * n_pairs, fix_body, jnp.zeros((1, TQ), f32))

    qs_ref[...] = (qt_ref[0, 0].astype(f32) * (HEAD_DIM ** -0.5)).astype(bf16)
    _masked_flash(i, qs_ref,
                  lambda j: k_ref[0, pl.ds(pl.multiple_of(j * TQ, TQ), TQ), :],
                  lambda j: vt_ref[0, j],
                  lambda j: jnp.where(keys_ref[j] >= thr, 0.0, NEG),
                  bias_ref, m_ref, acc_ref, s_ref, mt_ref, al_ref, chunk)
    acc = acc_ref[...]
    ot = acc[0:HEAD_DIM] / acc[HEAD_DIM:HEAD_DIM + 1]
    o_ref[0] = jnp.concatenate([ot[:, h * TQ:(h + 1) * TQ].T for h in range(nh)], axis=1).astype(o_ref.dtype)


def _dsa_attention(q, k, v, qi, ki, wi, near, far):
    bn, s, hq = q.shape
    nh = hq // HEAD_DIM
    nih = qi.shape[2] // IDX_DIM
    nt = s // TQ
    topk = min(C_TOPK_MAX, s // 4)
    width = nh * TQ
    qt = q.reshape(bn, nt, TQ, nh, HEAD_DIM).transpose(0, 1, 4, 3, 2).reshape(bn, nt, HEAD_DIM, width)
    qit = qi.reshape(bn, nt, TQ, nih, IDX_DIM).transpose(0, 1, 4, 3, 2).reshape(bn, nt, IDX_DIM, nih * TQ)
    wit = wi.reshape(bn, nt, TQ, nih).transpose(0, 1, 3, 2)
    vt = _value_tiles(v)
    bias3 = _flash_bias_tiles(near, far, 1)[0]
    tri = np.tril(np.ones((TQ, TQ), np.float32), -1)
    kern = functools.partial(_dsa_kernel, topk=topk, chunk=512)
    nt_scr = max(nt, -(-topk // TQ))
    return pl.pallas_call(
        kern,
        grid=(bn, nt),
        in_specs=[pl.BlockSpec((1, 1, HEAD_DIM, width), lambda b, i: (b, i, 0, 0)),
                  pl.BlockSpec((1, s, HEAD_DIM), lambda b, i: (b, 0, 0)),
                  pl.BlockSpec((1, nt, V_ROWS, TQ), lambda b, i: (b, 0, 0, 0)),
                  pl.BlockSpec((1, 1, IDX_DIM, nih * TQ), lambda b, i: (b, i, 0, 0)),
                  pl.BlockSpec((1, s, IDX_DIM), lambda b, i: (b, 0, 0)),
                  pl.BlockSpec((1, 1, nih, TQ), lambda b, i: (b, i, 0, 0)),
                  pl.BlockSpec((3, TQ, width), lambda b, i: (0, 0, 0)),
                  pl.BlockSpec((TQ, TQ), lambda b, i: (0, 0))],
        out_specs=pl.BlockSpec((1, TQ, hq), lambda b, i: (b, i, 0)),
        out_shape=jax.ShapeDtypeStruct((bn, s, hq), bf16),
        scratch_shapes=[pltpu.VMEM((nt_scr, TQ, TQ), i32), pltpu.VMEM((HEAD_DIM, width), bf16),
                        pltpu.VMEM((1, width), f32), pltpu.VMEM((V_ROWS, width), f32),
                        pltpu.VMEM((2, TQ, width), f32), pltpu.VMEM((2, 1, width), f32),
                        pltpu.VMEM((2, 1, width), f32)],
        compiler_params=_cparams(("parallel", "arbitrary")),
        name="dsa_attention",
    )(qt, k, vt, qit, ki, wit, bias3, jnp.asarray(tri, bf16))


def _heads_first(a, g):
    bn, s, _ = a.shape
    return a.reshape(bn, s, g, HEAD_DIM).transpose(0, 2, 1, 3)


def _even_layer(x, table_t, w_in, sinks, cmpk_pos, cmpk_w1, cmpk_w2, cmpv_pos, cmpv_w1, cmpv_w2,
                w_out, ln1_g, ln1_b, ffn_gate, ffn_up, ffn_down, ln2_g, ln2_b):
    bn, s, d = x.shape
    xf = x.reshape(bn * s, d)
    a_q, a_kv, b_q, b_kv = A_HEADS * HEAD_DIM, A_KV * HEAD_DIM, B_HEADS * HEAD_DIM, B_KV * HEAD_DIM
    n_main = a_q + 2 * a_kv + b_q + 6 * b_kv
    rb = B_HEADS // B_KV
    main = _matmul(xf, w_in[:, :n_main].astype(bf16), bf16, tn=512).reshape(bn, s, n_main)
    w_tail = jnp.pad(w_in[:, n_main:], ((0, 0), (0, 128 - 3 * B_HEADS))).astype(bf16)
    gates = _matmul(xf, w_tail, f32)[:, :3 * B_HEADS]
    gates = gates.reshape(bn, s, 3, B_KV, rb).transpose(0, 3, 1, 2, 4).reshape(bn, B_KV, s, 3 * rb)
    cols = np.cumsum([0, a_q, a_kv, a_kv, b_q] + [b_kv] * 6)
    qa, ka, va, qb, kc_in, vc_in, ksl, vsl, kwn, vwn = [main[:, :, cols[t]:cols[t + 1]] for t in range(10)]

    near_idx = _bucket_of(np.arange(TQ)[:, None] + TQ - np.arange(2 * TQ)[None, :])
    near = _bias_expand(table_t, near_idx, TQ)
    pad_b = -(-(B_WINDOW - 1) // TQ) * TQ
    win_idx = _bucket_of(np.arange(TQ)[:, None] + pad_b - np.arange(pad_b + TQ)[None, :])
    bias_win = _bias_expand(table_t[A_HEADS:A_HEADS + B_HEADS], win_idx, TQ)
    nc = s // CMP_STRIDE
    nt = s // TQ
    cmp_idx = _bucket_of((np.arange(nt)[:, None, None] * TQ + np.arange(TQ)[None, None, :])
                         - (np.arange(nc)[None, :, None] * CMP_STRIDE + CMP_BLOCK - 1)).reshape(nt * nc, TQ)
    bias_c = _bias_expand(table_t[A_HEADS:A_HEADS + B_HEADS], cmp_idx, min(nt * nc, 512))
    bias_c = bias_c.reshape(B_HEADS, nt, nc, TQ)
    far = table_t[:, NUM_BUCKETS - 1]

    out_a = _banded_gqa(qa, _heads_first(ka, A_KV), _heads_first(va, A_KV), near[:A_HEADS], sinks, A_WINDOW, bf16)
    o_win = _banded_gqa(qb, _heads_first(kwn, B_KV), _heads_first(vwn, B_KV), bias_win, None, B_WINDOW, f32)
    kc = _nsa_compress(_heads_first(kc_in, B_KV), cmpk_pos, cmpk_w1, cmpk_w2)
    vc = _nsa_compress(_heads_first(vc_in, B_KV), cmpv_pos, cmpv_w1, cmpv_w2)
    out_b = _nsa_mix(qb, kc, vc, bias_c, _heads_first(ksl, B_KV), _heads_first(vsl, B_KV),
                     near[A_HEADS:A_HEADS + B_HEADS], far[A_HEADS:A_HEADS + B_HEADS], gates, o_win)
    mix = jnp.concatenate([out_a, out_b], axis=-1).reshape(bn * s, a_q + b_q)
    x1 = _proj_res_ln(mix, w_out.astype(bf16), xf, ln1_g, ln1_b)
    x2 = _ffn_res_ln(x1, ffn_gate.astype(bf16), ffn_up.astype(bf16), ffn_down.astype(bf16), ln2_g, ln2_b)
    return x2.reshape(bn, s, d), near, far


def _odd_layer(x, near, far, w_in, w_out, ln1_g, ln1_b, router, exp_gate, exp_up, exp_down, ln2_g, ln2_b):
    bn, s, d = x.shape
    xf = x.reshape(bn * s, d)
    c_q = C_HEADS * HEAD_DIM
    n_main = c_q + 2 * HEAD_DIM + IDX_HEADS * IDX_DIM
    main = _matmul(xf, w_in[:, :n_main].astype(bf16), bf16, tn=n_main // 2 if n_main % 256 == 0 else n_main)
    main = main.reshape(bn, s, n_main)
    w_tail = jnp.pad(w_in[:, n_main:], ((0, 0), (0, 128 - IDX_DIM - IDX_HEADS))).astype(bf16)
    tail = _matmul(xf, w_tail, f32).reshape(bn, s, 128)
    q = main[:, :, :c_q]
    k = main[:, :, c_q:c_q + HEAD_DIM]
    v = main[:, :, c_q + HEAD_DIM:c_q + 2 * HEAD_DIM]
    qi = main[:, :, c_q + 2 * HEAD_DIM:]
    ki = tail[:, :, :IDX_DIM].astype(bf16)
    wi = tail[:, :, IDX_DIM:IDX_DIM + IDX_HEADS]
    mix = _dsa_attention(q, k, v, qi, ki, wi, near[:C_HEADS], far[:C_HEADS]).reshape(bn * s, c_q)
    x1 = _proj_res_ln(mix, w_out.astype(bf16), xf, ln1_g, ln1_b)
    x2 = _moe_res_ln(x1, router, exp_gate.astype(bf16), exp_up.astype(bf16), exp_down.astype(bf16), ln2_g, ln2_b)
    return x2.reshape(bn, s, d)


def kernel(x, rel_bias, l0_w_in, l0_sinks, l0_cmpk_pos, l0_cmpk_w1, l0_cmpk_w2, l0_cmpv_pos, l0_cmpv_w1,
           l0_cmpv_w2, l0_w_out, l0_ln1_g, l0_ln1_b, l0_ffn_gate, l0_ffn_up, l0_ffn_down, l0_ln2_g, l0_ln2_b,
           l1_w_in, l1_w_out, l1_ln1_g, l1_ln1_b, l1_router, l1_exp_gate, l1_exp_up, l1_exp_down, l1_ln2_g,
           l1_ln2_b):
    table_t = rel_bias.T.astype(f32)
    x, near, far = _even_layer(x, table_t, l0_w_in, l0_sinks, l0_cmpk_pos, l0_cmpk_w1, l0_cmpk_w2, l0_cmpv_pos,
                               l0_cmpv_w1, l0_cmpv_w2, l0_w_out, l0_ln1_g, l0_ln1_b, l0_ffn_gate, l0_ffn_up,
                               l0_ffn_down, l0_ln2_g, l0_ln2_b)
    return _odd_layer(x, near, far, l1_w_in, l1_w_out, l1_ln1_g, l1_ln1_b, l1_router, l1_exp_gate, l1_exp_up,
                      l1_exp_down, l1_ln2_g, l1_ln2_b)
```

```python
import functools
import math

import numpy as np
import jax
import jax.numpy as jnp
from jax import lax
from jax.experimental import pallas as pl
from jax.experimental.pallas import tpu as pltpu

f32 = jnp.float32
bf16 = jnp.bfloat16
i32 = jnp.int32

D_MODEL = 1024
HEAD_DIM = 64
NUM_BUCKETS = 32
MAX_DISTANCE = 128
A_HEADS, A_KV, A_WINDOW = 8, 2, 128
B_HEADS, B_KV, B_WINDOW = 8, 2, 512
CMP_BLOCK, CMP_STRIDE = 32, 16
SLC_BLOCK, SLC_TOPN = 64, 16
C_HEADS, IDX_HEADS, IDX_DIM, C_TOPK_MAX = 16, 8, 32, 256
D_FF, N_EXPERTS = 2816, 8
DEPTH = 2
ALPHA = (2.0 * DEPTH) ** 0.25
LN_EPS = 1e-5
NEG = -1e30
POS_BIG = 1e30
INT_MIN = -(2 ** 31)

TQ = 128
FF_CHUNK = 256
VMEM_LIMIT = 56 * 1024 * 1024


def _cparams(sem):
    return pltpu.CompilerParams(dimension_semantics=sem, vmem_limit_bytes=VMEM_LIMIT)


def _bucket_table(max_dist):
    n = np.arange(max_dist + 1)
    max_exact = NUM_BUCKETS // 2
    nf = np.maximum(n, 1).astype(np.float64)
    large = max_exact + (np.log(nf / max_exact) / math.log(MAX_DISTANCE / max_exact)
                         * (NUM_BUCKETS - max_exact)).astype(np.int64)
    large = np.minimum(large, NUM_BUCKETS - 1)
    return np.where(n < max_exact, n, large).astype(np.int32)


def _bucket_of(dist):
    dist = np.maximum(dist, 0)
    return _bucket_table(int(dist.max()))[dist]


def _mm_kernel(x_ref, w_ref, o_ref):
    o_ref[...] = jnp.dot(x_ref[...].astype(bf16), w_ref[...], preferred_element_type=f32).astype(o_ref.dtype)


def _matmul(x, w, out_dtype, tm=512, tn=None):
    m, k = x.shape
    n = w.shape[1]
    tn = n if tn is None else tn
    return pl.pallas_call(
        _mm_kernel,
        grid=(m // tm, n // tn),
        in_specs=[pl.BlockSpec((tm, k), lambda i, j: (i, 0)),
                  pl.BlockSpec((k, tn), lambda i, j: (0, j))],
        out_specs=pl.BlockSpec((tm, tn), lambda i, j: (i, j)),
        out_shape=jax.ShapeDtypeStruct((m, n), out_dtype),
        compiler_params=_cparams(("parallel", "arbitrary")),
        name="matmul",
    )(x, w)


def _bias_expand_kernel(tab_ref, idx_ref, o_ref):
    h = pl.program_id(0)
    idx = idx_ref[...]
    acc = jnp.zeros(idx.shape, f32)
    for b in range(NUM_BUCKETS):
        acc = jnp.where(idx == b, tab_ref[h, b], acc)
    o_ref[0] = acc


def _bias_expand(table_t, idx, tr):
    nh = table_t.shape[0]
    r, c = idx.shape
    return pl.pallas_call(
        _bias_expand_kernel,
        grid=(nh, r // tr),
        in_specs=[pl.BlockSpec(memory_space=pltpu.SMEM),
                  pl.BlockSpec((tr, c), lambda h, i: (i, 0))],
        out_specs=pl.BlockSpec((1, tr, c), lambda h, i: (h, i, 0)),
        out_shape=jax.ShapeDtypeStruct((nh, r, c), f32),
        compiler_params=_cparams(("parallel", "arbitrary")),
        name="bias_expand",
    )(table_t, jnp.asarray(idx, dtype=i32))


def _layer_norm(z, g, b):
    mu = jnp.mean(z, axis=-1, keepdims=True)
    zc = z - mu
    var = jnp.mean(zc * zc, axis=-1, keepdims=True)
    return zc * lax.rsqrt(var + LN_EPS) * g + b


def _proj_ln_kernel(mix_ref, w_ref, x_ref, g_ref, b_ref, o_ref):
    y = jnp.dot(mix_ref[...], w_ref[...], preferred_element_type=f32)
    o_ref[...] = _layer_norm(ALPHA * x_ref[...] + y, g_ref[...], b_ref[...])


def _proj_res_ln(mix, w, x, g, b, tm=512):
    m, k = mix.shape
    d = w.shape[1]
    return pl.pallas_call(
        _proj_ln_kernel,
        grid=(m // tm,),
        in_specs=[pl.BlockSpec((tm, k), lambda i: (i, 0)),
                  pl.BlockSpec((k, d), lambda i: (0, 0)),
                  pl.BlockSpec((tm, d), lambda i: (i, 0)),
                  pl.BlockSpec((1, d), lambda i: (0, 0)),
                  pl.BlockSpec((1, d), lambda i: (0, 0))],
        out_specs=pl.BlockSpec((tm, d), lambda i: (i, 0)),
        out_shape=jax.ShapeDtypeStruct((m, d), f32),
        compiler_params=_cparams(("parallel",)),
        name="proj_res_ln",
    )(mix, w, x, g.reshape(1, d), b.reshape(1, d))


def _swiglu_accumulate(xb, wg_ref, wu_ref, wd_ref, acc_ref, row_scale, widx):
    d_ff = wg_ref.shape[-1]
    for c in range(d_ff // FF_CHUNK):
        cols = slice(c * FF_CHUNK, (c + 1) * FF_CHUNK)
        gate = jnp.dot(xb, wg_ref[widx + (slice(None), cols)], preferred_element_type=f32)
        up = jnp.dot(xb, wu_ref[widx + (slice(None), cols)], preferred_element_type=f32)
        h = jax.nn.silu(gate) * up
        if row_scale is not None:
            h = h * row_scale
        acc_ref[...] += jnp.dot(h.astype(bf16), wd_ref[widx + (cols, slice(None))], preferred_element_type=f32)


def _ffn_kernel(x_ref, wg_ref, wu_ref, wd_ref, g_ref, b_ref, o_ref, acc_ref):
    x = x_ref[...]
    acc_ref[...] = jnp.zeros_like(acc_ref)
    _swiglu_accumulate(x.astype(bf16), wg_ref, wu_ref, wd_ref, acc_ref, None, ())
    o_ref[...] = _layer_norm(ALPHA * x + acc_ref[...], g_ref[...], b_ref[...])


def _ffn_res_ln(x, wg, wu, wd, g, b, tm=512):
    m, d = x.shape
    d_ff = wg.shape[1]
    return pl.pallas_call(
        _ffn_kernel,
        grid=(m // tm,),
        in_specs=[pl.BlockSpec((tm, d), lambda i: (i, 0)),
                  pl.BlockSpec((d, d_ff), lambda i: (0, 0)),
                  pl.BlockSpec((d, d_ff), lambda i: (0, 0)),
                  pl.BlockSpec((d_ff, d), lambda i: (0, 0)),
                  pl.BlockSpec((1, d), lambda i: (0, 0)),
                  pl.BlockSpec((1, d), lambda i: (0, 0))],
        out_specs=pl.BlockSpec((tm, d), lambda i: (i, 0)),
        out_shape=jax.ShapeDtypeStruct((m, d), f32),
        scratch_shapes=[pltpu.VMEM((tm, d), f32)],
        compiler_params=_cparams(("parallel",)),
        name="ffn_res_ln",
    )(x, wg, wu, wd, g.reshape(1, d), b.reshape(1, d))


def _moe_kernel_dense(x_ref, r_ref, wg_ref, wu_ref, wd_ref, g_ref, b_ref, o_ref, acc_ref, comb_ref, xb_ref):
    e = pl.program_id(1)
    lane = lax.broadcasted_iota(i32, comb_ref.shape, 1)

    @pl.when(e == 0)
    def _():
        x = x_ref[...]
        xb_ref[...] = x.astype(bf16)
        acc_ref[...] = jnp.zeros_like(acc_ref)
        logits = jnp.dot(x, r_ref[...], preferred_element_type=f32)
        logits = jnp.where(lane < N_EXPERTS, logits, -jnp.inf)
        m1 = jnp.max(logits, axis=-1, keepdims=True)
        i1 = jnp.min(jnp.where(logits == m1, lane, 128), axis=-1, keepdims=True)
        rest = jnp.where(lane == i1, -jnp.inf, logits)
        m2 = jnp.max(rest, axis=-1, keepdims=True)
        i2 = jnp.min(jnp.where(rest == m2, lane, 128), axis=-1, keepdims=True)
        e2 = jnp.exp(m2 - m1)
        g1 = 1.0 / (1.0 + e2)
        comb_ref[...] = jnp.where(lane == i1, g1, 0.0) + jnp.where(lane == i2, e2 * g1, 0.0)

    c = jnp.sum(jnp.where(lane == e, comb_ref[...], 0.0), axis=-1, keepdims=True)
    _swiglu_accumulate(xb_ref[...], wg_ref, wu_ref, wd_ref, acc_ref, c, (0,))

    @pl.when(e == N_EXPERTS - 1)
    def _():
        o_ref[...] = _layer_norm(ALPHA * x_ref[...] + acc_ref[...], g_ref[...], b_ref[...])


def _router_gates(x, r_ref, lane):
    logits = jnp.dot(x, r_ref[...], preferred_element_type=f32)
    logits = jnp.where(lane < N_EXPERTS, logits, -jnp.inf)
    m1 = jnp.max(logits, axis=-1, keepdims=True)
    i1 = jnp.min(jnp.where(logits == m1, lane, 128), axis=-1, keepdims=True)
    rest = jnp.where(lane == i1, -jnp.inf, logits)
    m2 = jnp.max(rest, axis=-1, keepdims=True)
    i2 = jnp.min(jnp.where(rest == m2, lane, 128), axis=-1, keepdims=True)
    e2 = jnp.exp(m2 - m1)
    g1 = 1.0 / (1.0 + e2)
    return jnp.where(lane == i1, g1, 0.0) + jnp.where(lane == i2, e2 * g1, 0.0)


MOE_CAP = 384


def _moe_kernel(x_ref, r_ref, tri_ref, wg_ref, wu_ref, wd_ref, g_ref, b_ref, o_ref,
                comb_ref, pos_ref, combt_ref, post_ref, yc_ref):
    e = pl.program_id(1)
    tm = x_ref.shape[0]
    lane = lax.broadcasted_iota(i32, (tm, 128), 1)

    @pl.when(e == 0)
    def _():
        comb = _router_gates(x_ref[...], r_ref, lane)
        comb_ref[...] = comb
        chosen = jnp.where(comb > 0.0, 1.0, 0.0)
        pos = jnp.dot(tri_ref[...], chosen.astype(bf16), preferred_element_type=f32)
        pos = jnp.where(comb > 0.0, pos, -1.0)
        pos_ref[...] = pos
        for t in range(tm // 128):
            rows = slice(t * 128, (t + 1) * 128)
            combt_ref[:, rows] = comb[rows].T[0:N_EXPERTS]
            post_ref[:, rows] = pos[rows].T[0:N_EXPERTS]
        o_ref[...] = jnp.zeros_like(o_ref)

    gate_col = jnp.sum(jnp.where(lane == e, comb_ref[...], 0.0), axis=-1, keepdims=True)
    pos_col = jnp.sum(jnp.where(lane == e, pos_ref[...], 0.0), axis=-1, keepdims=True).astype(i32)
    pos_row = post_ref[pl.ds(e, 1), :].astype(i32)
    n_tok = jnp.sum(jnp.where(combt_ref[pl.ds(e, 1), :] > 0.0, 1.0, 0.0)).astype(i32)

    def pass_body(p, carry):
        base = p * MOE_CAP
        gather = jnp.where(pos_row == base + lax.broadcasted_iota(i32, (MOE_CAP, tm), 0), 1.0, 0.0).astype(bf16)
        xc = jnp.dot(gather, x_ref[...].astype(bf16), preferred_element_type=f32).astype(bf16)
        yc_ref[...] = jnp.zeros_like(yc_ref)
        _swiglu_accumulate(xc, wg_ref, wu_ref, wd_ref, yc_ref, None, (0,))
        scatter = jnp.where(pos_col == base + lax.broadcasted_iota(i32, (tm, MOE_CAP), 1), 1.0, 0.0).astype(bf16)
        o_ref[...] += gate_col * jnp.dot(scatter, yc_ref[...].astype(bf16), preferred_element_type=f32)
        return carry

    lax.fori_loop(0, (n_tok + MOE_CAP - 1) // MOE_CAP, pass_body, 0)

    @pl.when(e == N_EXPERTS - 1)
    def _():
        o_ref[...] = _layer_norm(ALPHA * x_ref[...] + o_ref[...], g_ref[...], b_ref[...])


def _moe_res_ln(x, router, wg, wu, wd, g, b, tm=1024):
    m, d = x.shape
    ne, _, d_ff = wg.shape
    tm = min(tm, m)
    router_p = jnp.pad(router, ((0, 0), (0, 128 - ne)))
    tri = jnp.asarray(np.tril(np.ones((tm, tm), np.float32), -1), bf16)
    single = pl.Buffered(1)
    return pl.pallas_call(
        _moe_kernel,
        grid=(m // tm, ne),
        in_specs=[pl.BlockSpec((tm, d), lambda i, e: (i, 0), pipeline_mode=single),
                  pl.BlockSpec((d, 128), lambda i, e: (0, 0), pipeline_mode=single),
                  pl.BlockSpec((tm, tm), lambda i, e: (0, 0), pipeline_mode=single),
                  pl.BlockSpec((1, d, d_ff), lambda i, e: (e, 0, 0)),
                  pl.BlockSpec((1, d, d_ff), lambda i, e: (e, 0, 0)),
                  pl.BlockSpec((1, d_ff, d), lambda i, e: (e, 0, 0)),
                  pl.BlockSpec((1, d), lambda i, e: (0, 0)),
                  pl.BlockSpec((1, d), lambda i, e: (0, 0))],
        out_specs=pl.BlockSpec((tm, d), lambda i, e: (i, 0)),
        out_shape=jax.ShapeDtypeStruct((m, d), f32),
        scratch_shapes=[pltpu.VMEM((tm, 128), f32), pltpu.VMEM((tm, 128), f32),
                        pltpu.VMEM((N_EXPERTS, tm), f32), pltpu.VMEM((N_EXPERTS, tm), f32),
                        pltpu.VMEM((MOE_CAP, d), f32)],
        compiler_params=_cparams(("parallel", "arbitrary")),
        name="moe_res_ln",
    )(x, router_p, tri, wg, wu, wd, g.reshape(1, d), b.reshape(1, d))


def _moe_res_ln_dense(x, router, wg, wu, wd, g, b, tm=512):
    m, d = x.shape
    ne, _, d_ff = wg.shape
    router_p = jnp.pad(router, ((0, 0), (0, 128 - ne)))
    return pl.pallas_call(
        _moe_kernel_dense,
        grid=(m // tm, ne),
        in_specs=[pl.BlockSpec((tm, d), lambda i, e: (i, 0)),
                  pl.BlockSpec((d, 128), lambda i, e: (0, 0)),
                  pl.BlockSpec((1, d, d_ff), lambda i, e: (e, 0, 0)),
                  pl.BlockSpec((1, d, d_ff), lambda i, e: (e, 0, 0)),
                  pl.BlockSpec((1, d_ff, d), lambda i, e: (e, 0, 0)),
                  pl.BlockSpec((1, d), lambda i, e: (0, 0)),
                  pl.BlockSpec((1, d), lambda i, e: (0, 0))],
        out_specs=pl.BlockSpec((tm, d), lambda i, e: (i, 0)),
        out_shape=jax.ShapeDtypeStruct((m, d), f32),
        scratch_shapes=[pltpu.VMEM((tm, d), f32), pltpu.VMEM((tm, 128), f32), pltpu.VMEM((tm, d), bf16)],
        compiler_params=_cparams(("parallel", "arbitrary")),
        name="moe_res_ln",
    )(x, router_p, wg, wu, wd, g.reshape(1, d), b.reshape(1, d))


def _dot_nt(a, b):
    return lax.dot_general(a, b, (((1,), (1,)), ((), ())), preferred_element_type=f32)


def _banded_kernel(sinks_ref, q_ref, k_ref, v_ref, bias_ref, o_ref, *, rep, kw, pad, window, use_sinks):
    g = pl.program_id(1)
    i = pl.program_id(2)
    start = pl.multiple_of(i * TQ, TQ)
    k = k_ref[0, 0, pl.ds(start, kw), :]
    v = v_ref[0, 0, pl.ds(start, kw), :]
    qi = lax.broadcasted_iota(i32, (TQ, kw), 0)
    ki = lax.broadcasted_iota(i32, (TQ, kw), 1)
    dist = qi + pad - ki
    valid = (dist >= 0) & (dist < window) & (start - pad + ki >= 0)
    scale = HEAD_DIM ** -0.5
    outs = []
    for r in range(rep):
        q = q_ref[0, :, r * HEAD_DIM:(r + 1) * HEAD_DIM]
        s = _dot_nt(q, k) * scale + bias_ref[r]
        s = jnp.where(valid, s, NEG)
        m = jnp.max(s, axis=-1, keepdims=True)
        if use_sinks:
            sk = sinks_ref[g * rep + r]
            m = jnp.maximum(m, sk)
        e = jnp.exp(s - m)
        l = jnp.sum(e, axis=-1, keepdims=True)
        if use_sinks:
            l = l + jnp.exp(sk - m)
        outs.append(jnp.dot(e.astype(bf16), v, preferred_element_type=f32) / l)
    o_ref[0] = jnp.concatenate(outs, axis=-1).astype(o_ref.dtype)


def _banded_gqa(q, k, v, bias, sinks, window, out_dtype):
    bn, s, hq = q.shape
    g = k.shape[1]
    rep = hq // (g * HEAD_DIM)
    pad = -(-(window - 1) // TQ) * TQ
    kw = pad + TQ
    kp = jnp.pad(k, ((0, 0), (0, 0), (pad, 0), (0, 0)))
    vp = jnp.pad(v, ((0, 0), (0, 0), (pad, 0), (0, 0)))
    use_sinks = sinks is not None
    sinks_arr = sinks.astype(f32) if use_sinks else jnp.zeros((g * rep,), f32)
    kern = functools.partial(_banded_kernel, rep=rep, kw=kw, pad=pad, window=window, use_sinks=use_sinks)
    return pl.pallas_call(
        kern,
        grid=(bn, g, s // TQ),
        in_specs=[pl.BlockSpec(memory_space=pltpu.SMEM),
                  pl.BlockSpec((1, TQ, rep * HEAD_DIM), lambda b, gg, i: (b, i, gg)),
                  pl.BlockSpec((1, 1, pad + s, HEAD_DIM), lambda b, gg, i: (b, gg, 0, 0)),
                  pl.BlockSpec((1, 1, pad + s, HEAD_DIM), lambda b, gg, i: (b, gg, 0, 0)),
                  pl.BlockSpec((rep, TQ, kw), lambda b, gg, i: (gg, 0, 0))],
        out_specs=pl.BlockSpec((1, TQ, rep * HEAD_DIM), lambda b, gg, i: (b, i, gg)),
        out_shape=jax.ShapeDtypeStruct((bn, s, hq), out_dtype),
        compiler_params=_cparams(("parallel", "parallel", "arbitrary")),
        name="banded_gqa_w%d" % window,
    )(sinks_arr, q, kp, vp, bias)


def _compress_kernel(blk_ref, pos_ref, w1_ref, w2_ref, o_ref):
    xb = (blk_ref[0].astype(f32) + pos_ref[...]).astype(bf16)
    h = jax.nn.gelu(jnp.dot(xb, w1_ref[...], preferred_element_type=f32))
    o_ref[0] = jnp.dot(h.astype(bf16), w2_ref[...], preferred_element_type=f32).astype(o_ref.dtype)


def _nsa_compress(kv, pos, w1, w2):
    bn, g, s, dh = kv.shape
    nc = s // CMP_STRIDE
    chunks = kv.reshape(bn, g, nc, CMP_STRIDE * dh)
    nxt = jnp.pad(chunks[:, :, 1:], ((0, 0), (0, 0), (0, 1), (0, 0)))
    blocks = jnp.concatenate([chunks, nxt], axis=-1).reshape(bn * g, nc, CMP_BLOCK * dh)
    cin = CMP_BLOCK * dh
    hid = w1.shape[1]
    out = pl.pallas_call(
        _compress_kernel,
        grid=(bn * g,),
        in_specs=[pl.BlockSpec((1, nc, cin), lambda i: (i, 0, 0)),
                  pl.BlockSpec((1, cin), lambda i: (0, 0)),
                  pl.BlockSpec((cin, hid), lambda i: (0, 0)),
                  pl.BlockSpec((hid, dh), lambda i: (0, 0))],
        out_specs=pl.BlockSpec((1, nc, dh), lambda i: (i, 0, 0)),
        out_shape=jax.ShapeDtypeStruct((bn * g, nc, dh), bf16),
        compiler_params=_cparams(("parallel",)),
        name="nsa_compress",
    )(blocks, pos.reshape(1, cin), w1.astype(bf16), w2.astype(bf16))
    return out.reshape(bn, g, nc, dh)


def _flash_step(q_rows, ks, vs, biases, mask, m_ref, l_ref, acc_ref, row0):
    nh = len(biases)
    s = _dot_nt(q_rows, ks) * (HEAD_DIM ** -0.5)
    ps = []
    for r in range(nh):
        rows = pl.ds(row0 + r * TQ, TQ)
        sr = s[r * TQ:(r + 1) * TQ] + biases[r]
        sr = jnp.where(mask, sr, NEG)
        m_old = m_ref[rows, :]
        m_new = jnp.maximum(m_old, jnp.max(sr, axis=-1, keepdims=True))
        p = jnp.where(mask, jnp.exp(sr - m_new), 0.0)
        alpha = jnp.exp(m_old - m_new)
        l_ref[rows, :] = alpha * l_ref[rows, :] + jnp.sum(p, axis=-1, keepdims=True)
        acc_ref[rows, :] = alpha * acc_ref[rows, :]
        m_ref[rows, :] = m_new
        ps.append(p.astype(bf16))
    rows = pl.ds(row0, nh * TQ)
    acc_ref[rows, :] += jnp.dot(jnp.concatenate(ps, axis=0), vs, preferred_element_type=f32)


def _nsa_kernel_rowmajor(far_ref, q_ref, kc_ref, vc_ref, biasc_ref, ov_ref, exp_ref, ks_ref, vs_ref, near_ref,
                gates_ref, owin_ref, o_ref, m_ref, l_ref, acc_ref, *, rep, n_sel):
    g = pl.program_id(1)
    i = pl.program_id(2)
    t0 = i * TQ
    nc = kc_ref.shape[2]
    nblk = ov_ref.shape[1]
    scale = HEAD_DIM ** -0.5
    qall = jnp.concatenate([q_ref[0, :, r * HEAD_DIM:(r + 1) * HEAD_DIM] for r in range(rep)], axis=0)

    tpos = t0 + lax.broadcasted_iota(i32, (TQ, nc), 0)
    cmp_end = lax.broadcasted_iota(i32, (TQ, nc), 1) * CMP_STRIDE + (CMP_BLOCK - 1)
    valid_c = tpos >= cmp_end
    s_c = _dot_nt(qall, kc_ref[0, 0]) * scale
    p_rows = []
    p_sum = jnp.zeros((TQ, nc), f32)
    for r in range(rep):
        sr = jnp.where(valid_c, s_c[r * TQ:(r + 1) * TQ] + biasc_ref[r], NEG)
        m = jnp.max(sr, axis=-1, keepdims=True)
        e = jnp.where(valid_c, jnp.exp(sr - m), 0.0)
        l = jnp.sum(e, axis=-1, keepdims=True)
        p = e * (1.0 / jnp.maximum(l, 1e-30))
        p_rows.append(p.astype(bf16))
        p_sum = p_sum + p
    o_cmp = jnp.dot(jnp.concatenate(p_rows, axis=0), vc_ref[0, 0], preferred_element_type=f32)

    imp = jnp.dot(p_sum, ov_ref[...], preferred_element_type=f32)
    blk = lax.broadcasted_iota(i32, (TQ, nblk), 1)
    tq = t0 + lax.broadcasted_iota(i32, (TQ, nblk), 0)
    cur = tq // SLC_BLOCK
    forced = (blk == 0) | (blk == cur) | (blk == cur - 1)
    score = jnp.where(forced, POS_BIG, jnp.where(blk * SLC_BLOCK <= tq, imp, NEG))
    rank = jnp.zeros((TQ, nblk), i32)
    for kk in range(nblk):
        ck = score[:, kk:kk + 1]
        later = jnp.where(blk > kk, 1, 0)
        rank = rank + jnp.where(ck > score, 1, jnp.where(ck == score, later, 0))
    sel = jnp.where(rank < n_sel, 1.0, 0.0).astype(bf16)

    m_ref[...] = jnp.full(m_ref.shape, NEG, f32)
    l_ref[...] = jnp.zeros(l_ref.shape, f32)
    acc_ref[...] = jnp.zeros(acc_ref.shape, f32)
    fars = [far_ref[g * rep + r] for r in range(rep)]
    row = lax.broadcasted_iota(i32, (TQ, TQ), 0)
    col = lax.broadcasted_iota(i32, (TQ, TQ), 1)

    def tile_mask(j):
        return jnp.dot(sel, exp_ref[j], preferred_element_type=f32) > 0.5

    def kv_tile(j):
        rows = pl.ds(pl.multiple_of(j * TQ, TQ), TQ)
        return ks_ref[0, 0, rows, :], vs_ref[0, 0, rows, :]

    def far_body(j, carry):
        ks, vs = kv_tile(j)
        _flash_step(qall, ks, vs, fars, tile_mask(j), m_ref, l_ref, acc_ref, 0)
        return carry

    lax.fori_loop(0, jnp.maximum(i - 1, 0), far_body, 0)

    @pl.when(i >= 1)
    def _():
        ks, vs = kv_tile(i - 1)
        _flash_step(qall, ks, vs, [near_ref[r, :, 0:TQ] for r in range(rep)], tile_mask(i - 1),
                    m_ref, l_ref, acc_ref, 0)

    ks, vs = kv_tile(i)
    diag_mask = (row >= col) & tile_mask(i)
    _flash_step(qall, ks, vs, [near_ref[r, :, TQ:2 * TQ] for r in range(rep)], diag_mask, m_ref, l_ref, acc_ref, 0)
    o_slc = acc_ref[...] / l_ref[...]

    sig = jax.nn.sigmoid(gates_ref[0, 0])
    outs = []
    for r in range(rep):
        rows = slice(r * TQ, (r + 1) * TQ)
        o_win = owin_ref[0, :, r * HEAD_DIM:(r + 1) * HEAD_DIM]
        outs.append(sig[:, r:r + 1] * o_cmp[rows] + sig[:, rep + r:rep + r + 1] * o_slc[rows]
                    + sig[:, 2 * rep + r:2 * rep + r + 1] * o_win)
    o_ref[0] = jnp.concatenate(outs, axis=-1).astype(o_ref.dtype)


def _nsa_mix_rowmajor(q, kc, vc, bias_c, ks, vs, near, far, gates, o_win):
    bn, s, hq = q.shape
    g = kc.shape[1]
    nc = kc.shape[2]
    rep = hq // (g * HEAD_DIM)
    nblk = s // SLC_BLOCK
    n_sel = min(SLC_TOPN, nblk)
    nt = s // TQ
    ci = np.arange(nc)[:, None] * CMP_STRIDE
    sj = np.arange(nblk)[None, :] * SLC_BLOCK
    overlap = np.clip(np.minimum(ci + CMP_BLOCK, sj + SLC_BLOCK) - np.maximum(ci, sj), 0, None) / CMP_BLOCK
    overlap[nc - 1] = 0.0
    expand = (np.arange(nblk)[None, :, None] == (np.arange(nt)[:, None, None] * TQ
                                                 + np.arange(TQ)[None, None, :]) // SLC_BLOCK)
    kern = functools.partial(_nsa_kernel_rowmajor, rep=rep, n_sel=n_sel)
    rw = rep * HEAD_DIM
    return pl.pallas_call(
        kern,
        grid=(bn, g, nt),
        in_specs=[pl.BlockSpec(memory_space=pltpu.SMEM),
                  pl.BlockSpec((1, TQ, rw), lambda b, gg, i: (b, i, gg)),
                  pl.BlockSpec((1, 1, nc, HEAD_DIM), lambda b, gg, i: (b, gg, 0, 0)),
                  pl.BlockSpec((1, 1, nc, HEAD_DIM), lambda b, gg, i: (b, gg, 0, 0)),
                  pl.BlockSpec((rep, TQ, nc), lambda b, gg, i: (gg, i, 0)),
                  pl.BlockSpec((nc, nblk), lambda b, gg, i: (0, 0)),
                  pl.BlockSpec((nt, nblk, TQ), lambda b, gg, i: (0, 0, 0)),
                  pl.BlockSpec((1, 1, s, HEAD_DIM), lambda b, gg, i: (b, gg, 0, 0)),
                  pl.BlockSpec((1, 1, s, HEAD_DIM), lambda b, gg, i: (b, gg, 0, 0)),
                  pl.BlockSpec((rep, TQ, 2 * TQ), lambda b, gg, i: (gg, 0, 0)),
                  pl.BlockSpec((1, 1, TQ, 3 * rep), lambda b, gg, i: (b, gg, i, 0)),
                  pl.BlockSpec((1, TQ, rw), lambda b, gg, i: (b, i, gg))],
        out_specs=pl.BlockSpec((1, TQ, rw), lambda b, gg, i: (b, i, gg)),
        out_shape=jax.ShapeDtypeStruct((bn, s, hq), bf16),
        scratch_shapes=[pltpu.VMEM((rep * TQ, 1), f32), pltpu.VMEM((rep * TQ, 1), f32),
                        pltpu.VMEM((rep * TQ, HEAD_DIM), f32)],
        compiler_params=_cparams(("parallel", "parallel", "arbitrary")),
        name="nsa_mix",
    )(far, q, kc, vc, bias_c, jnp.asarray(overlap, f32), jnp.asarray(expand, bf16), ks, vs, near, gates, o_win)


def _dsa_kernel_rowmajor(far_ref, q_ref, k_ref, v_ref, qi_ref, ki_ref, wi_ref, near_ref, tri_ref, o_ref,
                         keys_ref, mask_ref, m_ref, l_ref, acc_ref, *, topk, head_chunk):
    i = pl.program_id(1)
    t0 = i * TQ
    nh = q_ref.shape[2] // HEAD_DIM
    nih = qi_ref.shape[2] // IDX_DIM
    row = lax.broadcasted_iota(i32, (TQ, TQ), 0)
    col = lax.broadcasted_iota(i32, (TQ, TQ), 1)
    n_sel_tiles = jnp.maximum(i + 1, -(-topk // TQ))

    qi_all = jnp.concatenate([qi_ref[0, :, h * IDX_DIM:(h + 1) * IDX_DIM] for h in range(nih)], axis=0)
    wi = wi_ref[0]

    def score_body(j, carry):
        kt = ki_ref[0, pl.ds(pl.multiple_of(j * TQ, TQ), TQ), :]
        rel = jnp.maximum(_dot_nt(qi_all, kt), 0.0)
        sc = jnp.zeros((TQ, TQ), f32)
        for h in range(nih):
            sc = sc + wi[:, h:h + 1] * rel[h * TQ:(h + 1) * TQ]
        sc = jnp.where(j * TQ + col <= t0 + row, sc, NEG)
        sc = jnp.where(sc == 0.0, 0.0, sc)
        bits = pltpu.bitcast(sc, i32)
        keys_ref[j] = jnp.where(bits < 0, bits ^ 0x7FFFFFFF, bits)
        return carry

    lax.fori_loop(0, n_sel_tiles, score_body, 0)

    def count(pred_fn):
        def body(j, cnt):
            return cnt + jnp.where(pred_fn(keys_ref[j]), 1.0, 0.0)
        cnt = lax.fori_loop(0, n_sel_tiles, body, jnp.zeros((TQ, TQ), f32))
        return jnp.broadcast_to(jnp.sum(cnt, axis=-1, keepdims=True), (TQ, TQ))

    def bit_body(t, prefix):
        cand = prefix ^ jnp.left_shift(jnp.int32(1), 31 - t)
        return jnp.where(count(lambda kj: kj >= cand) >= topk, cand, prefix)

    thr = lax.fori_loop(0, 32, bit_body, jnp.full((TQ, TQ), INT_MIN, i32))
    need = topk - count(lambda kj: kj > thr)

    def mask_body(j, running):
        kj = keys_ref[j]
        tie = jnp.where(kj == thr, 1.0, 0.0)
        before = jnp.dot(tie.astype(bf16), tri_ref[...], preferred_element_type=f32) + running
        take = jnp.where(kj > thr, 1.0, jnp.where(before < need, tie, 0.0))
        mask_ref[j] = jnp.where(j * TQ + col <= t0 + row, take, 0.0)
        return running + jnp.broadcast_to(jnp.sum(tie, axis=-1, keepdims=True), (TQ, TQ))

    lax.fori_loop(0, i + 1, mask_body, jnp.zeros((TQ, TQ), f32))

    m_ref[...] = jnp.full(m_ref.shape, NEG, f32)
    l_ref[...] = jnp.zeros(l_ref.shape, f32)
    acc_ref[...] = jnp.zeros(acc_ref.shape, f32)
    qall = jnp.concatenate([q_ref[0, :, h * HEAD_DIM:(h + 1) * HEAD_DIM] for h in range(nh)], axis=0)
    hc = head_chunk

    def attend(j, bias_fn):
        rows = pl.ds(pl.multiple_of(j * TQ, TQ), TQ)
        ks = k_ref[0, rows, :]
        vs = v_ref[0, rows, :]
        mask = mask_ref[j] > 0.5
        for c in range(nh // hc):
            _flash_step(qall[c * hc * TQ:(c + 1) * hc * TQ], ks, vs, [bias_fn(c * hc + r) for r in range(hc)],
                        mask, m_ref, l_ref, acc_ref, c * hc * TQ)

    def far_body(j, carry):
        attend(j, lambda h: far_ref[h])
        return carry

    lax.fori_loop(0, jnp.maximum(i - 1, 0), far_body, 0)

    @pl.when(i >= 1)
    def _():
        attend(i - 1, lambda h: near_ref[h, :, 0:TQ])

    attend(i, lambda h: near_ref[h, :, TQ:2 * TQ])
    o = acc_ref[...] / l_ref[...]
    o_ref[0] = jnp.concatenate([o[h * TQ:(h + 1) * TQ] for h in range(nh)], axis=-1).astype(o_ref.dtype)


def _dsa_attention_rowmajor(q, k, v, qi, ki, wi, near, far):
    bn, s, hq = q.shape
    nh = hq // HEAD_DIM
    nt = s // TQ
    topk = min(C_TOPK_MAX, s // 4)
    tri = np.triu(np.ones((TQ, TQ), np.float32), 1)
    kern = functools.partial(_dsa_kernel_rowmajor, topk=topk, head_chunk=4)
    nt_scr = max(nt, -(-topk // TQ))
    return pl.pallas_call(
        kern,
        grid=(bn, nt),
        in_specs=[pl.BlockSpec(memory_space=pltpu.SMEM),
                  pl.BlockSpec((1, TQ, hq), lambda b, i: (b, i, 0)),
                  pl.BlockSpec((1, s, HEAD_DIM), lambda b, i: (b, 0, 0)),
                  pl.BlockSpec((1, s, HEAD_DIM), lambda b, i: (b, 0, 0)),
                  pl.BlockSpec((1, TQ, qi.shape[2]), lambda b, i: (b, i, 0)),
                  pl.BlockSpec((1, s, IDX_DIM), lambda b, i: (b, 0, 0)),
                  pl.BlockSpec((1, TQ, wi.shape[2]), lambda b, i: (b, i, 0)),
                  pl.BlockSpec((nh, TQ, 2 * TQ), lambda b, i: (0, 0, 0)),
                  pl.BlockSpec((TQ, TQ), lambda b, i: (0, 0))],
        out_specs=pl.BlockSpec((1, TQ, hq), lambda b, i: (b, i, 0)),
        out_shape=jax.ShapeDtypeStruct((bn, s, hq), bf16),
        scratch_shapes=[pltpu.VMEM((nt_scr, TQ, TQ), i32), pltpu.VMEM((nt, TQ, TQ), f32),
                        pltpu.VMEM((nh * TQ, 1), f32), pltpu.VMEM((nh * TQ, 1), f32),
                        pltpu.VMEM((nh * TQ, HEAD_DIM), f32)],
        compiler_params=_cparams(("parallel", "arbitrary")),
        name="dsa_attention",
    )(far, q, k, v, qi, ki, wi, near, jnp.asarray(tri, bf16))


V_ROWS = HEAD_DIM + 16


def _masked_flash(i, qs_ref, key_tile, value_tile, mask_bias, bias_ref, m_ref, acc_ref, s_ref, mt_ref, al_ref, chunk):
    width = qs_ref.shape[1]
    m_ref[...] = jnp.full(m_ref.shape, NEG, f32)
    acc_ref[...] = jnp.zeros(acc_ref.shape, f32)

    def scores(j):
        slot = j % 2
        kt = key_tile(j)
        kind = jnp.clip(j - (i - 2), 0, 2)
        maskb = jnp.concatenate([mask_bias(j)] * (chunk // TQ), axis=1)
        for c in range(width // chunk):
            cols = slice(c * chunk, (c + 1) * chunk)
            s = jnp.dot(kt, qs_ref[:, cols], preferred_element_type=f32) + bias_ref[kind, :, cols] + maskb
            s_ref[slot, :, cols] = s
            m_old = m_ref[:, cols]
            m_new = jnp.maximum(m_old, jnp.max(s, axis=0, keepdims=True))
            m_ref[:, cols] = m_new
            mt_ref[slot, :, cols] = m_new
            al_ref[slot, :, cols] = jnp.exp(m_old - m_new)

    def values(j):
        slot = j % 2
        vt = value_tile(j)
        for c in range(width // chunk):
            cols = slice(c * chunk, (c + 1) * chunk)
            p = jnp.exp(s_ref[slot, :, cols] - mt_ref[slot, :, cols])
            acc_ref[:, cols] = (al_ref[slot, :, cols] * acc_ref[:, cols]
                                + jnp.dot(vt, p.astype(bf16), preferred_element_type=f32))

    scores(0)

    def pipe_body(j, carry):
        values(j - 1)
        scores(j)
        return carry

    lax.fori_loop(1, i + 1, pipe_body, 0)
    values(i)


def _flash_bias_tiles(near, far, groups):
    nh = near.shape[0]
    per = nh // groups
    width = per * TQ
    neart = near.reshape(groups, per, TQ, 2, TQ).transpose(0, 3, 4, 1, 2).reshape(groups, 2, TQ, width)
    causal = np.where(np.arange(TQ)[:, None] <= np.arange(TQ)[None, :], 0.0, NEG).astype(np.float32)
    far_tile = jnp.broadcast_to(jnp.repeat(far, TQ).reshape(groups, 1, width), (groups, TQ, width))
    return jnp.stack([far_tile, neart[:, 0], neart[:, 1] + jnp.asarray(np.tile(causal, (1, per)))], axis=1)


def _value_tiles(v):
    lead = v.shape[:-2]
    nt = v.shape[-2] // TQ
    vt = jnp.swapaxes(v.reshape(lead + (nt, TQ, HEAD_DIM)), -1, -2)
    return jnp.concatenate([vt, jnp.ones(lead + (nt, 1, TQ), vt.dtype),
                            jnp.zeros(lead + (nt, V_ROWS - HEAD_DIM - 1, TQ), vt.dtype)], axis=-2)


def _nsa_kernel(qt_ref, kc_ref, vct_ref, biasc_ref, ovt_ref, ks_ref, vst_ref, bias_ref, gt_ref, gn_ref, owin_ref,
                o_ref, qs_ref, selb_ref, m_ref, acc_ref, s_ref, mt_ref, al_ref, *, rep, n_sel):
    i = pl.program_id(2)
    t0 = i * TQ
    nc = kc_ref.shape[2]
    nblk = ovt_ref.shape[0]
    qs_ref[...] = (qt_ref[0, 0, 0].astype(f32) * (HEAD_DIM ** -0.5)).astype(bf16)

    s_c = jnp.dot(kc_ref[0, 0], qs_ref[...], preferred_element_type=f32)
    cmp_end = lax.broadcasted_iota(i32, (nc, TQ), 0) * CMP_STRIDE + (CMP_BLOCK - 1)
    valid_c = t0 + lax.broadcasted_iota(i32, (nc, TQ), 1) >= cmp_end
    p_cols = []
    p_sum = jnp.zeros((nc, TQ), f32)
    for r in range(rep):
        sr = jnp.where(valid_c, s_c[:, r * TQ:(r + 1) * TQ] + biasc_ref[r, 0], NEG)
        m = jnp.max(sr, axis=0, keepdims=True)
        e = jnp.where(valid_c, jnp.exp(sr - m), 0.0)
        l = jnp.sum(e, axis=0, keepdims=True)
        p = e * (1.0 / jnp.maximum(l, 1e-30))
        p_cols.append(p.astype(bf16))
        p_sum = p_sum + p
    o_cmp = jnp.dot(vct_ref[0, 0], jnp.concatenate(p_cols, axis=1), preferred_element_type=f32)

    imp = jnp.dot(ovt_ref[...], p_sum, preferred_element_type=f32)
    blk = lax.broadcasted_iota(i32, (nblk, TQ), 0)
    tq = t0 + lax.broadcasted_iota(i32, (nblk, TQ), 1)
    cur = tq // SLC_BLOCK
    forced = (blk == 0) | (blk == cur) | (blk == cur - 1)
    score = jnp.where(forced, POS_BIG, jnp.where(blk * SLC_BLOCK <= tq, imp, NEG))
    rank = jnp.zeros((nblk, TQ), i32)
    for kk in range(nblk):
        ck = score[kk:kk + 1, :]
        rank = rank + jnp.where(ck > score, 1, jnp.where(ck == score, jnp.where(blk > kk, 1, 0), 0))
    selb_ref[...] = jnp.where(rank < n_sel, 0.0, NEG)

    per_tile = TQ // SLC_BLOCK

    def mask_bias(j):
        rows = [jnp.broadcast_to(selb_ref[pl.ds(per_tile * j + t, 1), :], (SLC_BLOCK, TQ)) for t in range(per_tile)]
        return jnp.concatenate(rows, axis=0)

    _masked_flash(i, qs_ref,
                  lambda j: ks_ref[0, 0, pl.ds(pl.multiple_of(j * TQ, TQ), TQ), :],
                  lambda j: vst_ref[0, 0, j],
                  mask_bias, bias_ref.at[0], m_ref, acc_ref, s_ref, mt_ref, al_ref, rep * TQ)
    acc = acc_ref[...]
    o_slc = acc[0:HEAD_DIM] / acc[HEAD_DIM:HEAD_DIM + 1]

    sig_t = jax.nn.sigmoid(gt_ref[0, 0, 0])
    sig_n = jax.nn.sigmoid(gn_ref[0, 0])
    outs = []
    for r in range(rep):
        cols = slice(r * TQ, (r + 1) * TQ)
        mixed = sig_t[r:r + 1] * o_cmp[:, cols] + sig_t[rep + r:rep + r + 1] * o_slc[:, cols]
        o_win = owin_ref[0, :, r * HEAD_DIM:(r + 1) * HEAD_DIM]
        outs.append(mixed.T + sig_n[:, 2 * rep + r:2 * rep + r + 1] * o_win)
    o_ref[0] = jnp.concatenate(outs, axis=1).astype(o_ref.dtype)


def _nsa_mix(q, kc, vc, bias_c, ks, vs, near, far, gates, o_win):
    bn, s, hq = q.shape
    g = kc.shape[1]
    nc = kc.shape[2]
    rep = hq // (g * HEAD_DIM)
    nblk = s // SLC_BLOCK
    n_sel = min(SLC_TOPN, nblk)
    nt = s // TQ
    width = rep * TQ
    ci = np.arange(nc)[:, None] * CMP_STRIDE
    sj = np.arange(nblk)[None, :] * SLC_BLOCK
    overlap = np.clip(np.minimum(ci + CMP_BLOCK, sj + SLC_BLOCK) - np.maximum(ci, sj), 0, None) / CMP_BLOCK
    overlap[nc - 1] = 0.0
    qt = q.reshape(bn, nt, TQ, g, rep, HEAD_DIM).transpose(0, 3, 1, 5, 4, 2).reshape(bn, g, nt, HEAD_DIM, width)
    gates_t = gates.reshape(bn, g, nt, TQ, 3 * rep).transpose(0, 1, 2, 4, 3)
    kern = functools.partial(_nsa_kernel, rep=rep, n_sel=n_sel)
    rw = rep * HEAD_DIM
    return pl.pallas_call(
        kern,
        grid=(bn, g, nt),
        in_specs=[pl.BlockSpec((1, 1, 1, HEAD_DIM, width), lambda b, gg, i: (b, gg, i, 0, 0)),
                  pl.BlockSpec((1, 1, nc, HEAD_DIM), lambda b, gg, i: (b, gg, 0, 0)),
                  pl.BlockSpec((1, 1, HEAD_DIM, nc), lambda b, gg, i: (b, gg, 0, 0)),
                  pl.BlockSpec((rep, 1, nc, TQ), lambda b, gg, i: (gg, i, 0, 0)),
                  pl.BlockSpec((nblk, nc), lambda b, gg, i: (0, 0)),
                  pl.BlockSpec((1, 1, s, HEAD_DIM), lambda b, gg, i: (b, gg, 0, 0)),
                  pl.BlockSpec((1, 1, nt, V_ROWS, TQ), lambda b, gg, i: (b, gg, 0, 0, 0)),
                  pl.BlockSpec((1, 3, TQ, width), lambda b, gg, i: (gg, 0, 0, 0)),
                  pl.BlockSpec((1, 1, 1, 3 * rep, TQ), lambda b, gg, i: (b, gg, i, 0, 0)),
                  pl.BlockSpec((1, 1, TQ, 3 * rep), lambda b, gg, i: (b, gg, i, 0)),
                  pl.BlockSpec((1, TQ, rw), lambda b, gg, i: (b, i, gg))],
        out_specs=pl.BlockSpec((1, TQ, rw), lambda b, gg, i: (b, i, gg)),
        out_shape=jax.ShapeDtypeStruct((bn, s, hq), bf16),
        scratch_shapes=[pltpu.VMEM((HEAD_DIM, width), bf16), pltpu.VMEM((nblk, TQ), f32),
                        pltpu.VMEM((1, width), f32), pltpu.VMEM((V_ROWS, width), f32),
                        pltpu.VMEM((2, TQ, width), f32), pltpu.VMEM((2, 1, width), f32),
                        pltpu.VMEM((2, 1, width), f32)],
        compiler_params=_cparams(("parallel", "parallel", "arbitrary")),
        name="nsa_mix",
    )(qt, kc, jnp.swapaxes(vc, 2, 3), bias_c, jnp.asarray(overlap.T, f32), ks, _value_tiles(vs),
      _flash_bias_tiles(near, far, g), gates_t, gates, o_win)


def _dsa_kernel(qt_ref, k_ref, vt_ref, qit_ref, ki_ref, wit_ref, bias_ref, tri_ref, o_ref,
                keys_ref, qs_ref, m_ref, acc_ref, s_ref, mt_ref, al_ref, *, topk, chunk):
    i = pl.program_id(1)
    t0 = i * TQ
    width = qt_ref.shape[3]
    nh = width // TQ
    nih = qit_ref.shape[3] // TQ
    krow = lax.broadcasted_iota(i32, (TQ, TQ), 0)
    qcol = lax.broadcasted_iota(i32, (TQ, TQ), 1)
    n_pairs = jnp.maximum(i // 2 + 1, -(-topk // (2 * TQ)))

    qit = qit_ref[0, 0]
    wit = wit_ref[0, 0]

    def score_tile(j):
        kt = ki_ref[0, pl.ds(pl.multiple_of(j * TQ, TQ), TQ), :]
        rel = jnp.maximum(jnp.dot(kt, qit, preferred_element_type=f32), 0.0)
        sc = jnp.zeros((TQ, TQ), f32)
        for h in range(nih):
            sc = sc + wit[h:h + 1, :] * rel[:, h * TQ:(h + 1) * TQ]
        sc = jnp.where(j * TQ + krow <= t0 + qcol, sc, NEG)
        sc = jnp.where(sc == 0.0, 0.0, sc)
        bits = pltpu.bitcast(sc, i32)
        keys_ref[j] = jnp.where(bits < 0, bits ^ 0x7FFFFFFF, bits)

    def score_body(jj, carry):
        score_tile(2 * jj)
        score_tile(2 * jj + 1)
        return carry

    lax.fori_loop(0, n_pairs, score_body, 0)

    def count(pred_fn):
        def body(jj, cnt):
            return (cnt + jnp.where(pred_fn(keys_ref[2 * jj]), 1.0, 0.0)
                    + jnp.where(pred_fn(keys_ref[2 * jj + 1]), 1.0, 0.0))
        cnt = lax.fori_loop(0, n_pairs, body, jnp.zeros((TQ, TQ), f32))
        return jnp.sum(cnt, axis=0, keepdims=True)

    def bit_body(t, prefix):
        cand = prefix ^ jnp.left_shift(jnp.int32(1), 31 - t)
        return jnp.where(count(lambda kj: kj >= cand) >= topk, cand, prefix)

    thr = lax.fori_loop(0, 32, bit_body, jnp.full((1, TQ), INT_MIN, i32))
    n_ge = count(lambda kj: kj >= thr)

    @pl.when(jnp.max(n_ge) > topk)
    def _():
        need = topk - count(lambda kj: kj > thr)

        def fix_body(j, running):
            kj = keys_ref[j]
            tie = jnp.where(kj == thr, 1.0, 0.0)
            before = jnp.dot(tri_ref[...], tie.astype(bf16), preferred_element_type=f32) + running
            keys_ref[j] = jnp.where(before >= need, jnp.where(kj == thr, kj - 1, kj), kj)
            return running + jnp.sum(tie, axis=0, keepdims=True)

        lax.fori_loop(0, 2 * n_pairs, fix_body, jnp.zeros((1, TQ), f32))

    qs_ref[...] = (qt_ref[0, 0].astype(f32) * (HEAD_DIM ** -0.5)).astype(bf16)
    _masked_flash(i, qs_ref,
                  lambda j: k_ref[0, pl.ds(pl.multiple_of(j * TQ, TQ), TQ), :],
                  lambda j: vt_ref[0, j],
                  lambda j: jnp.where(keys_ref[j] >= thr, 0.0, NEG),
                  bias_ref, m_ref, acc_ref, s_ref, mt_ref, al_ref, chunk)
    acc = acc_ref[...]
    ot = acc[0:HEAD_DIM] / acc[HEAD_DIM:HEAD_DIM + 1]
    o_ref[0] = jnp.concatenate([ot[:, h * TQ:(h + 1) * TQ].T for h in range(nh)], axis=1).astype(o_ref.dtype)


def _dsa_attention(q, k, v, qi, ki, wi, near, far):
    bn, s, hq = q.shape
    nh = hq // HEAD_DIM
    nih = qi.shape[2] // IDX_DIM
    nt = s // TQ
    topk = min(C_TOPK_MAX, s // 4)
    width = nh * TQ
    qt = q.reshape(bn, nt, TQ, nh, HEAD_DIM).transpose(0, 1, 4, 3, 2).reshape(bn, nt, HEAD_DIM, width)
    qit = qi.reshape(bn, nt, TQ, nih, IDX_DIM).transpose(0, 1, 4, 3, 2).reshape(bn, nt, IDX_DIM, nih * TQ)
    wit = wi.reshape(bn, nt, TQ, nih).transpose(0, 1, 3, 2)
    vt = _value_tiles(v)
    bias3 = _flash_bias_tiles(near, far, 1)[0]
    tri = np.tril(np.ones((TQ, TQ), np.float32), -1)
    kern = functools.partial(_dsa_kernel, topk=topk, chunk=512)
    nt_scr = max(nt, -(-topk // TQ))
    return pl.pallas_call(
        kern,
        grid=(bn, nt),
        in_specs=[pl.BlockSpec((1, 1, HEAD_DIM, width), lambda b, i: (b, i, 0, 0)),
                  pl.BlockSpec((1, s, HEAD_DIM), lambda b, i: (b, 0, 0)),
                  pl.BlockSpec((1, nt, V_ROWS, TQ), lambda b, i: (b, 0, 0, 0)),
                  pl.BlockSpec((1, 1, IDX_DIM, nih * TQ), lambda b, i: (b, i, 0, 0)),
                  pl.BlockSpec((1, s, IDX_DIM), lambda b, i: (b, 0, 0)),
                  pl.BlockSpec((1, 1, nih, TQ), lambda b, i: (b, i, 0, 0)),
                  pl.BlockSpec((3, TQ, width), lambda b, i: (0, 0, 0)),
                  pl.BlockSpec((TQ, TQ), lambda b, i: (0, 0))],
        out_specs=pl.BlockSpec((1, TQ, hq), lambda b, i: (b, i, 0)),
        out_shape=jax.ShapeDtypeStruct((bn, s, hq), bf16),
        scratch_shapes=[pltpu.VMEM((nt_scr, TQ, TQ), i32), pltpu.VMEM((HEAD_DIM, width), bf16),
                        pltpu.VMEM((1, width), f32), pltpu.VMEM((V_ROWS, width), f32),
                        pltpu.VMEM((2, TQ, width), f32), pltpu.VMEM((2, 1, width), f32),
                        pltpu.VMEM((2, 1, width), f32)],
        compiler_params=_cparams(("parallel", "arbitrary")),
        name="dsa_attention",
    )(qt, k, vt, qit, ki, wit, bias3, jnp.asarray(tri, bf16))


def _heads_first(a, g):
    bn, s, _ = a.shape
    return a.reshape(bn, s, g, HEAD_DIM).transpose(0, 2, 1, 3)


def _even_layer(x, table_t, w_in, sinks, cmpk_pos, cmpk_w1, cmpk_w2, cmpv_pos, cmpv_w1, cmpv_w2,
                w_out, ln1_g, ln1_b, ffn_gate, ffn_up, ffn_down, ln2_g, ln2_b):
    bn, s, d = x.shape
    xf = x.reshape(bn * s, d)
    a_q, a_kv, b_q, b_kv = A_HEADS * HEAD_DIM, A_KV * HEAD_DIM, B_HEADS * HEAD_DIM, B_KV * HEAD_DIM
    n_main = a_q + 2 * a_kv + b_q + 6 * b_kv
    rb = B_HEADS // B_KV
    main = _matmul(xf, w_in[:, :n_main].astype(bf16), bf16, tn=512).reshape(bn, s, n_main)
    w_tail = jnp.pad(w_in[:, n_main:], ((0, 0), (0, 128 - 3 * B_HEADS))).astype(bf16)
    gates = _matmul(xf, w_tail, f32)[:, :3 * B_HEADS]
    gates = gates.reshape(bn, s, 3, B_KV, rb).transpose(0, 3, 1, 2, 4).reshape(bn, B_KV, s, 3 * rb)
    cols = np.cumsum([0, a_q, a_kv, a_kv, b_q] + [b_kv] * 6)
    qa, ka, va, qb, kc_in, vc_in, ksl, vsl, kwn, vwn = [main[:, :, cols[t]:cols[t + 1]] for t in range(10)]

    near_idx = _bucket_of(np.arange(TQ)[:, None] + TQ - np.arange(2 * TQ)[None, :])
    near = _bias_expand(table_t, near_idx, TQ)
    pad_b = -(-(B_WINDOW - 1) // TQ) * TQ
    win_idx = _bucket_of(np.arange(TQ)[:, None] + pad_b - np.arange(pad_b + TQ)[None, :])
    bias_win = _bias_expand(table_t[A_HEADS:A_HEADS + B_HEADS], win_idx, TQ)
    nc = s // CMP_STRIDE
    nt = s // TQ
    cmp_idx = _bucket_of((np.arange(nt)[:, None, None] * TQ + np.arange(TQ)[None, None, :])
                         - (np.arange(nc)[None, :, None] * CMP_STRIDE + CMP_BLOCK - 1)).reshape(nt * nc, TQ)
    bias_c = _bias_expand(table_t[A_HEADS:A_HEADS + B_HEADS], cmp_idx, min(nt * nc, 512))
    bias_c = bias_c.reshape(B_HEADS, nt, nc, TQ)
    far = table_t[:, NUM_BUCKETS - 1]

    out_a = _banded_gqa(qa, _heads_first(ka, A_KV), _heads_first(va, A_KV), near[:A_HEADS], sinks, A_WINDOW, bf16)
    o_win = _banded_gqa(qb, _heads_first(kwn, B_KV), _heads_first(vwn, B_KV), bias_win, None, B_WINDOW, f32)
    kc = _nsa_compress(_heads_first(kc_in, B_KV), cmpk_pos, cmpk_w1, cmpk_w2)
    vc = _nsa_compress(_heads_first(vc_in, B_KV), cmpv_pos, cmpv_w1, cmpv_w2)
    out_b = _nsa_mix(qb, kc, vc, bias_c, _heads_first(ksl, B_KV), _heads_first(vsl, B_KV),
                     near[A_HEADS:A_HEADS + B_HEADS], far[A_HEADS:A_HEADS + B_HEADS], gates, o_win)
    mix = jnp.concatenate([out_a, out_b], axis=-1).reshape(bn * s, a_q + b_q)
    x1 = _proj_res_ln(mix, w_out.astype(bf16), xf, ln1_g, ln1_b)
    x2 = _ffn_res_ln(x1, ffn_gate.astype(bf16), ffn_up.astype(bf16), ffn_down.astype(bf16), ln2_g, ln2_b)
    return x2.reshape(bn, s, d), near, far


def _odd_layer(x, near, far, w_in, w_out, ln1_g, ln1_b, router, exp_gate, exp_up, exp_down, ln2_g, ln2_b):
    bn, s, d = x.shape
    xf = x.reshape(bn * s, d)
    c_q = C_HEADS * HEAD_DIM
    n_main = c_q + 2 * HEAD_DIM + IDX_HEADS * IDX_DIM
    main = _matmul(xf, w_in[:, :n_main].astype(bf16), bf16, tn=n_main // 2 if n_main % 256 == 0 else n_main)
    main = main.reshape(bn, s, n_main)
    w_tail = jnp.pad(w_in[:, n_main:], ((0, 0), (0, 128 - IDX_DIM - IDX_HEADS))).astype(bf16)
    tail = _matmul(xf, w_tail, f32).reshape(bn, s, 128)
    q = main[:, :, :c_q]
    k = main[:, :, c_q:c_q + HEAD_DIM]
    v = main[:, :, c_q + HEAD_DIM:c_q + 2 * HEAD_DIM]
    qi = main[:, :, c_q + 2 * HEAD_DIM:]
    ki = tail[:, :, :IDX_DIM].astype(bf16)
    wi = tail[:, :, IDX_DIM:IDX_DIM + IDX_HEADS]
    mix = _dsa_attention(q, k, v, qi, ki, wi, near[:C_HEADS], far[:C_HEADS]).reshape(bn * s, c_q)
    x1 = _proj_res_ln(mix, w_out.astype(bf16), xf, ln1_g, ln1_b)
    x2 = _moe_res_ln(x1, router, exp_gate.astype(bf16), exp_up.astype(bf16), exp_down.astype(bf16), ln2_g, ln2_b)
    return x2.reshape(bn, s, d)


def kernel(x, rel_bias, l0_w_in, l0_sinks, l0_cmpk_pos, l0_cmpk_w1, l0_cmpk_w2, l0_cmpv_pos, l0_cmpv_w1,
           l0_cmpv_w2, l0_w_out, l0_ln1_g, l0_ln1_b, l0_ffn_gate, l0_ffn_up, l0_ffn_down, l0_ln2_g, l0_ln2_b,
           l1_w_in, l1_w_out, l1_ln1_g, l1_ln1_b, l1_router, l1_exp_gate, l1_exp_up, l1_exp_down, l1_ln2_g,
           l1_ln2_b):
    table_t = rel_bias.T.astype(f32)
    x, near, far = _even_layer(x, table_t, l0_w_in, l0_sinks, l0_cmpk_pos, l0_cmpk_w1, l0_cmpk_w2, l0_cmpv_pos,
                               l0_cmpv_w1, l0_cmpv_w2, l0_w_out, l0_ln1_g, l0_ln1_b, l0_ffn_gate, l0_ffn_up,
                               l0_ffn_down, l0_ln2_g, l0_ln2_b)
    return _odd_layer(x, near, far, l1_w_in, l1_w_out, l1_ln1_g, l1_ln1_b, l1_router, l1_exp_gate, l1_exp_up,
                      l1_exp_down, l1_ln2_g, l1_ln2_b)
```

```python
import functools
import math

import numpy as np
import jax
import jax.numpy as jnp
from jax import lax
from jax.experimental import pallas as pl
from jax.experimental.pallas import tpu as pltpu

f32 = jnp.float32
bf16 = jnp.bfloat16
i32 = jnp.int32

D_MODEL = 1024
HEAD_DIM = 64
NUM_BUCKETS = 32
MAX_DISTANCE = 128
A_HEADS, A_KV, A_WINDOW = 8, 2, 128
B_HEADS, B_KV, B_WINDOW = 8, 2, 512
CMP_BLOCK, CMP_STRIDE = 32, 16
SLC_BLOCK, SLC_TOPN = 64, 16
C_HEADS, IDX_HEADS, IDX_DIM, C_TOPK_MAX = 16, 8, 32, 256
D_FF, N_EXPERTS = 2816, 8
DEPTH = 2
ALPHA = (2.0 * DEPTH) ** 0.25
LN_EPS = 1e-5
NEG = -1e30
POS_BIG = 1e30
INT_MIN = -(2 ** 31)

TQ = 128
FF_CHUNK = 256
VMEM_LIMIT = 56 * 1024 * 1024


def _cparams(sem):
    return pltpu.CompilerParams(dimension_semantics=sem, vmem_limit_bytes=VMEM_LIMIT)


def _bucket_table(max_dist):
    n = np.arange(max_dist + 1)
    max_exact = NUM_BUCKETS // 2
    nf = np.maximum(n, 1).astype(np.float64)
    large = max_exact + (np.log(nf / max_exact) / math.log(MAX_DISTANCE / max_exact)
                         * (NUM_BUCKETS - max_exact)).astype(np.int64)
    large = np.minimum(large, NUM_BUCKETS - 1)
    return np.where(n < max_exact, n, large).astype(np.int32)


def _bucket_of(dist):
    dist = np.maximum(dist, 0)
    return _bucket_table(int(dist.max()))[dist]


def _mm_kernel(x_ref, w_ref, o_ref):
    o_ref[...] = jnp.dot(x_ref[...].astype(bf16), w_ref[...], preferred_element_type=f32).astype(o_ref.dtype)


def _matmul(x, w, out_dtype, tm=512, tn=None):
    m, k = x.shape
    n = w.shape[1]
    tn = n if tn is None else tn
    return pl.pallas_call(
        _mm_kernel,
        grid=(m // tm, n // tn),
        in_specs=[pl.BlockSpec((tm, k), lambda i, j: (i, 0)),
                  pl.BlockSpec((k, tn), lambda i, j: (0, j))],
        out_specs=pl.BlockSpec((tm, tn), lambda i, j: (i, j)),
        out_shape=jax.ShapeDtypeStruct((m, n), out_dtype),
        compiler_params=_cparams(("parallel", "arbitrary")),
        name="matmul",
    )(x, w)


def _bias_expand_kernel(tab_ref, idx_ref, o_ref):
    h = pl.program_id(0)
    idx = idx_ref[...]
    acc = jnp.zeros(idx.shape, f32)
    for b in range(NUM_BUCKETS):
        acc = jnp.where(idx == b, tab_ref[h, b], acc)
    o_ref[0] = acc


def _bias_expand(table_t, idx, tr):
    nh = table_t.shape[0]
    r, c = idx.shape
    return pl.pallas_call(
        _bias_expand_kernel,
        grid=(nh, r // tr),
        in_specs=[pl.BlockSpec(memory_space=pltpu.SMEM),
                  pl.BlockSpec((tr, c), lambda h, i: (i, 0))],
        out_specs=pl.BlockSpec((1, tr, c), lambda h, i: (h, i, 0)),
        out_shape=jax.ShapeDtypeStruct((nh, r, c), f32),
        compiler_params=_cparams(("parallel", "arbitrary")),
        name="bias_expand",
    )(table_t, jnp.asarray(idx, dtype=i32))


def _layer_norm(z, g, b):
    mu = jnp.mean(z, axis=-1, keepdims=True)
    zc = z - mu
    var = jnp.mean(zc * zc, axis=-1, keepdims=True)
    return zc * lax.rsqrt(var + LN_EPS) * g + b


def _proj_ln_kernel(mix_ref, w_ref, x_ref, g_ref, b_ref, o_ref):
    y = jnp.dot(mix_ref[...], w_ref[...], preferred_element_type=f32)
    o_ref[...] = _layer_norm(ALPHA * x_ref[...] + y, g_ref[...], b_ref[...])


def _proj_res_ln(mix, w, x, g, b, tm=512):
    m, k = mix.shape
    d = w.shape[1]
    return pl.pallas_call(
        _proj_ln_kernel,
        grid=(m // tm,),
        in_specs=[pl.BlockSpec((tm, k), lambda i: (i, 0)),
                  pl.BlockSpec((k, d), lambda i: (0, 0)),
                  pl.BlockSpec((tm, d), lambda i: (i, 0)),
                  pl.BlockSpec((1, d), lambda i: (0, 0)),
                  pl.BlockSpec((1, d), lambda i: (0, 0))],
        out_specs=pl.BlockSpec((tm, d), lambda i: (i, 0)),
        out_shape=jax.ShapeDtypeStruct((m, d), f32),
        compiler_params=_cparams(("parallel",)),
        name="proj_res_ln",
    )(mix, w, x, g.reshape(1, d), b.reshape(1, d))


def _swiglu_accumulate(xb, wg_ref, wu_ref, wd_ref, acc_ref, row_scale, widx):
    d_ff = wg_ref.shape[-1]
    for c in range(d_ff // FF_CHUNK):
        cols = slice(c * FF_CHUNK, (c + 1) * FF_CHUNK)
        gate = jnp.dot(xb, wg_ref[widx + (slice(None), cols)], preferred_element_type=f32)
        up = jnp.dot(xb, wu_ref[widx + (slice(None), cols)], preferred_element_type=f32)
        h = jax.nn.silu(gate) * up
        if row_scale is not None:
            h = h * row_scale
        acc_ref[...] += jnp.dot(h.astype(bf16), wd_ref[widx + (cols, slice(None))], preferred_element_type=f32)


def _ffn_kernel(x_ref, wg_ref, wu_ref, wd_ref, g_ref, b_ref, o_ref, acc_ref):
    x = x_ref[...]
    acc_ref[...] = jnp.zeros_like(acc_ref)
    _swiglu_accumulate(x.astype(bf16), wg_ref, wu_ref, wd_ref, acc_ref, None, ())
    o_ref[...] = _layer_norm(ALPHA * x + acc_ref[...], g_ref[...], b_ref[...])


def _ffn_res_ln(x, wg, wu, wd, g, b, tm=512):
    m, d = x.shape
    d_ff = wg.shape[1]
    return pl.pallas_call(
        _ffn_kernel,
        grid=(m // tm,),
        in_specs=[pl.BlockSpec((tm, d), lambda i: (i, 0)),
                  pl.BlockSpec((d, d_ff), lambda i: (0, 0)),
                  pl.BlockSpec((d, d_ff), lambda i: (0, 0)),
                  pl.BlockSpec((d_ff, d), lambda i: (0, 0)),
                  pl.BlockSpec((1, d), lambda i: (0, 0)),
                  pl.BlockSpec((1, d), lambda i: (0, 0))],
        out_specs=pl.BlockSpec((tm, d), lambda i: (i, 0)),
        out_shape=jax.ShapeDtypeStruct((m, d), f32),
        scratch_shapes=[pltpu.VMEM((tm, d), f32)],
        compiler_params=_cparams(("parallel",)),
        name="ffn_res_ln",
    )(x, wg, wu, wd, g.reshape(1, d), b.reshape(1, d))


def _moe_kernel_dense(x_ref, r_ref, wg_ref, wu_ref, wd_ref, g_ref, b_ref, o_ref, acc_ref, comb_ref, xb_ref):
    e = pl.program_id(1)
    lane = lax.broadcasted_iota(i32, comb_ref.shape, 1)

    @pl.when(e == 0)
    def _():
        x = x_ref[...]
        xb_ref[...] = x.astype(bf16)
        acc_ref[...] = jnp.zeros_like(acc_ref)
        logits = jnp.dot(x, r_ref[...], preferred_element_type=f32)
        logits = jnp.where(lane < N_EXPERTS, logits, -jnp.inf)
        m1 = jnp.max(logits, axis=-1, keepdims=True)
        i1 = jnp.min(jnp.where(logits == m1, lane, 128), axis=-1, keepdims=True)
        rest = jnp.where(lane == i1, -jnp.inf, logits)
        m2 = jnp.max(rest, axis=-1, keepdims=True)
        i2 = jnp.min(jnp.where(rest == m2, lane, 128), axis=-1, keepdims=True)
        e2 = jnp.exp(m2 - m1)
        g1 = 1.0 / (1.0 + e2)
        comb_ref[...] = jnp.where(lane == i1, g1, 0.0) + jnp.where(lane == i2, e2 * g1, 0.0)

    c = jnp.sum(jnp.where(lane == e, comb_ref[...], 0.0), axis=-1, keepdims=True)
    _swiglu_accumulate(xb_ref[...], wg_ref, wu_ref, wd_ref, acc_ref, c, (0,))

    @pl.when(e == N_EXPERTS - 1)
    def _():
        o_ref[...] = _layer_norm(ALPHA * x_ref[...] + acc_ref[...], g_ref[...], b_ref[...])


def _router_gates(x, r_ref, lane):
    logits = jnp.dot(x, r_ref[...], preferred_element_type=f32)
    logits = jnp.where(lane < N_EXPERTS, logits, -jnp.inf)
    m1 = jnp.max(logits, axis=-1, keepdims=True)
    i1 = jnp.min(jnp.where(logits == m1, lane, 128), axis=-1, keepdims=True)
    rest = jnp.where(lane == i1, -jnp.inf, logits)
    m2 = jnp.max(rest, axis=-1, keepdims=True)
    i2 = jnp.min(jnp.where(rest == m2, lane, 128), axis=-1, keepdims=True)
    e2 = jnp.exp(m2 - m1)
    g1 = 1.0 / (1.0 + e2)
    return jnp.where(lane == i1, g1, 0.0) + jnp.where(lane == i2, e2 * g1, 0.0)


MOE_CAP = 384


def _moe_kernel(x_ref, r_ref, tri_ref, wg_ref, wu_ref, wd_ref, g_ref, b_ref, o_ref,
                comb_ref, pos_ref, combt_ref, post_ref, yc_ref):
    e = pl.program_id(1)
    tm = x_ref.shape[0]
    lane = lax.broadcasted_iota(i32, (tm, 128), 1)

    @pl.when(e == 0)
    def _():
        comb = _router_gates(x_ref[...], r_ref, lane)
        comb_ref[...] = comb
        chosen = jnp.where(comb > 0.0, 1.0, 0.0)
        pos = jnp.dot(tri_ref[...], chosen.astype(bf16), preferred_element_type=f32)
        pos = jnp.where(comb > 0.0, pos, -1.0)
        pos_ref[...] = pos
        for t in range(tm // 128):
            rows = slice(t * 128, (t + 1) * 128)
            combt_ref[:, rows] = comb[rows].T[0:N_EXPERTS]
            post_ref[:, rows] = pos[rows].T[0:N_EXPERTS]
        o_ref[...] = jnp.zeros_like(o_ref)

    gate_col = jnp.sum(jnp.where(lane == e, comb_ref[...], 0.0), axis=-1, keepdims=True)
    pos_col = jnp.sum(jnp.where(lane == e, pos_ref[...], 0.0), axis=-1, keepdims=True).astype(i32)
    pos_row = post_ref[pl.ds(e, 1), :].astype(i32)
    n_tok = jnp.sum(jnp.where(combt_ref[pl.ds(e, 1), :] > 0.0, 1.0, 0.0)).astype(i32)

    def pass_body(p, carry):
        base = p * MOE_CAP
        gather = jnp.where(pos_row == base + lax.broadcasted_iota(i32, (MOE_CAP, tm), 0), 1.0, 0.0).astype(bf16)
        xc = jnp.dot(gather, x_ref[...].astype(bf16), preferred_element_type=f32).astype(bf16)
        yc_ref[...] = jnp.zeros_like(yc_ref)
        _swiglu_accumulate(xc, wg_ref, wu_ref, wd_ref, yc_ref, None, (0,))
        scatter = jnp.where(pos_col == base + lax.broadcasted_iota(i32, (tm, MOE_CAP), 1), 1.0, 0.0).astype(bf16)
        o_ref[...] += gate_col * jnp.dot(scatter, yc_ref[...].astype(bf16), preferred_element_type=f32)
        return carry

    lax.fori_loop(0, (n_tok + MOE_CAP - 1) // MOE_CAP, pass_body, 0)

    @pl.when(e == N_EXPERTS - 1)
    def _():
        o_ref[...] = _layer_norm(ALPHA * x_ref[...] + o_ref[...], g_ref[...], b_ref[...])


def _moe_res_ln(x, router, wg, wu, wd, g, b, tm=1024):
    m, d = x.shape
    ne, _, d_ff = wg.shape
    tm = min(tm, m)
    router_p = jnp.pad(router, ((0, 0), (0, 128 - ne)))
    tri = jnp.asarray(np.tril(np.ones((tm, tm), np.float32), -1), bf16)
    single = pl.Buffered(1)
    return pl.pallas_call(
        _moe_kernel,
        grid=(m // tm, ne),
        in_specs=[pl.BlockSpec((tm, d), lambda i, e: (i, 0), pipeline_mode=single),
                  pl.BlockSpec((d, 128), lambda i, e: (0, 0), pipeline_mode=single),
                  pl.BlockSpec((tm, tm), lambda i, e: (0, 0), pipeline_mode=single),
                  pl.BlockSpec((1, d, d_ff), lambda i, e: (e, 0, 0)),
                  pl.BlockSpec((1, d, d_ff), lambda i, e: (e, 0, 0)),
                  pl.BlockSpec((1, d_ff, d), lambda i, e: (e, 0, 0)),
                  pl.BlockSpec((1, d), lambda i, e: (0, 0)),
                  pl.BlockSpec((1, d), lambda i, e: (0, 0))],
        out_specs=pl.BlockSpec((tm, d), lambda i, e: (i, 0)),
        out_shape=jax.ShapeDtypeStruct((m, d), f32),
        scratch_shapes=[pltpu.VMEM((tm, 128), f32), pltpu.VMEM((tm, 128), f32),
                        pltpu.VMEM((N_EXPERTS, tm), f32), pltpu.VMEM((N_EXPERTS, tm), f32),
                        pltpu.VMEM((MOE_CAP, d), f32)],
        compiler_params=_cparams(("parallel", "arbitrary")),
        name="moe_res_ln",
    )(x, router_p, tri, wg, wu, wd, g.reshape(1, d), b.reshape(1, d))


def _moe_res_ln_dense(x, router, wg, wu, wd, g, b, tm=512):
    m, d = x.shape
    ne, _, d_ff = wg.shape
    router_p = jnp.pad(router, ((0, 0), (0, 128 - ne)))
    return pl.pallas_call(
        _moe_kernel_dense,
        grid=(m // tm, ne),
        in_specs=[pl.BlockSpec((tm, d), lambda i, e: (i, 0)),
                  pl.BlockSpec((d, 128), lambda i, e: (0, 0)),
                  pl.BlockSpec((1, d, d_ff), lambda i, e: (e, 0, 0)),
                  pl.BlockSpec((1, d, d_ff), lambda i, e: (e, 0, 0)),
                  pl.BlockSpec((1, d_ff, d), lambda i, e: (e, 0, 0)),
                  pl.BlockSpec((1, d), lambda i, e: (0, 0)),
                  pl.BlockSpec((1, d), lambda i, e: (0, 0))],
        out_specs=pl.BlockSpec((tm, d), lambda i, e: (i, 0)),
        out_shape=jax.ShapeDtypeStruct((m, d), f32),
        scratch_shapes=[pltpu.VMEM((tm, d), f32), pltpu.VMEM((tm, 128), f32), pltpu.VMEM((tm, d), bf16)],
        compiler_params=_cparams(("parallel", "arbitrary")),
        name="moe_res_ln",
    )(x, router_p, wg, wu, wd, g.reshape(1, d), b.reshape(1, d))


def _dot_nt(a, b):
    return lax.dot_general(a, b, (((1,), (1,)), ((), ())), preferred_element_type=f32)


def _banded_kernel_rowmajor(sinks_ref, q_ref, k_ref, v_ref, bias_ref, o_ref, *, rep, kw, pad, window, use_sinks):
    g = pl.program_id(1)
    i = pl.program_id(2)
    start = pl.multiple_of(i * TQ, TQ)
    k = k_ref[0, 0, pl.ds(start, kw), :]
    v = v_ref[0, 0, pl.ds(start, kw), :]
    qi = lax.broadcasted_iota(i32, (TQ, kw), 0)
    ki = lax.broadcasted_iota(i32, (TQ, kw), 1)
    dist = qi + pad - ki
    valid = (dist >= 0) & (dist < window) & (start - pad + ki >= 0)
    scale = HEAD_DIM ** -0.5
    outs = []
    for r in range(rep):
        q = q_ref[0, :, r * HEAD_DIM:(r + 1) * HEAD_DIM]
        s = _dot_nt(q, k) * scale + bias_ref[r]
        s = jnp.where(valid, s, NEG)
        m = jnp.max(s, axis=-1, keepdims=True)
        if use_sinks:
            sk = sinks_ref[g * rep + r]
            m = jnp.maximum(m, sk)
        e = jnp.exp(s - m)
        l = jnp.sum(e, axis=-1, keepdims=True)
        if use_sinks:
            l = l + jnp.exp(sk - m)
        outs.append(jnp.dot(e.astype(bf16), v, preferred_element_type=f32) / l)
    o_ref[0] = jnp.concatenate(outs, axis=-1).astype(o_ref.dtype)


def _banded_gqa_rowmajor(q, k, v, bias, sinks, window, out_dtype):
    bn, s, hq = q.shape
    g = k.shape[1]
    rep = hq // (g * HEAD_DIM)
    pad = -(-(window - 1) // TQ) * TQ
    kw = pad + TQ
    kp = jnp.pad(k, ((0, 0), (0, 0), (pad, 0), (0, 0)))
    vp = jnp.pad(v, ((0, 0), (0, 0), (pad, 0), (0, 0)))
    use_sinks = sinks is not None
    sinks_arr = sinks.astype(f32) if use_sinks else jnp.zeros((g * rep,), f32)
    kern = functools.partial(_banded_kernel_rowmajor, rep=rep, kw=kw, pad=pad, window=window, use_sinks=use_sinks)
    return pl.pallas_call(
        kern,
        grid=(bn, g, s // TQ),
        in_specs=[pl.BlockSpec(memory_space=pltpu.SMEM),
                  pl.BlockSpec((1, TQ, rep * HEAD_DIM), lambda b, gg, i: (b, i, gg)),
                  pl.BlockSpec((1, 1, pad + s, HEAD_DIM), lambda b, gg, i: (b, gg, 0, 0)),
                  pl.BlockSpec((1, 1, pad + s, HEAD_DIM), lambda b, gg, i: (b, gg, 0, 0)),
                  pl.BlockSpec((rep, TQ, kw), lambda b, gg, i: (gg, 0, 0))],
        out_specs=pl.BlockSpec((1, TQ, rep * HEAD_DIM), lambda b, gg, i: (b, i, gg)),
        out_shape=jax.ShapeDtypeStruct((bn, s, hq), out_dtype),
        compiler_params=_cparams(("parallel", "parallel", "arbitrary")),
        name="banded_gqa_w%d" % window,
    )(sinks_arr, q, kp, vp, bias)


def _compress_kernel(blk_ref, pos_ref, w1_ref, w2_ref, o_ref):
    xb = (blk_ref[0].astype(f32) + pos_ref[...]).astype(bf16)
    h = jax.nn.gelu(jnp.dot(xb, w1_ref[...], preferred_element_type=f32))
    o_ref[0] = jnp.dot(h.astype(bf16), w2_ref[...], preferred_element_type=f32).astype(o_ref.dtype)


def _nsa_compress(kv, pos, w1, w2):
    bn, g, s, dh = kv.shape
    nc = s // CMP_STRIDE
    chunks = kv.reshape(bn, g, nc, CMP_STRIDE * dh)
    nxt = jnp.pad(chunks[:, :, 1:], ((0, 0), (0, 0), (0, 1), (0, 0)))
    blocks = jnp.concatenate([chunks, nxt], axis=-1).reshape(bn * g, nc, CMP_BLOCK * dh)
    cin = CMP_BLOCK * dh
    hid = w1.shape[1]
    out = pl.pallas_call(
        _compress_kernel,
        grid=(bn * g,),
        in_specs=[pl.BlockSpec((1, nc, cin), lambda i: (i, 0, 0)),
                  pl.BlockSpec((1, cin), lambda i: (0, 0)),
                  pl.BlockSpec((cin, hid), lambda i: (0, 0)),
                  pl.BlockSpec((hid, dh), lambda i: (0, 0))],
        out_specs=pl.BlockSpec((1, nc, dh), lambda i: (i, 0, 0)),
        out_shape=jax.ShapeDtypeStruct((bn * g, nc, dh), bf16),
        compiler_params=_cparams(("parallel",)),
        name="nsa_compress",
    )(blocks, pos.reshape(1, cin), w1.astype(bf16), w2.astype(bf16))
    return out.reshape(bn, g, nc, dh)


def _flash_step(q_rows, ks, vs, biases, mask, m_ref, l_ref, acc_ref, row0):
    nh = len(biases)
    s = _dot_nt(q_rows, ks) * (HEAD_DIM ** -0.5)
    ps = []
    for r in range(nh):
        rows = pl.ds(row0 + r * TQ, TQ)
        sr = s[r * TQ:(r + 1) * TQ] + biases[r]
        sr = jnp.where(mask, sr, NEG)
        m_old = m_ref[rows, :]
        m_new = jnp.maximum(m_old, jnp.max(sr, axis=-1, keepdims=True))
        p = jnp.where(mask, jnp.exp(sr - m_new), 0.0)
        alpha = jnp.exp(m_old - m_new)
        l_ref[rows, :] = alpha * l_ref[rows, :] + jnp.sum(p, axis=-1, keepdims=True)
        acc_ref[rows, :] = alpha * acc_ref[rows, :]
        m_ref[rows, :] = m_new
        ps.append(p.astype(bf16))
    rows = pl.ds(row0, nh * TQ)
    acc_ref[rows, :] += jnp.dot(jnp.concatenate(ps, axis=0), vs, preferred_element_type=f32)


def _nsa_kernel_rowmajor(far_ref, q_ref, kc_ref, vc_ref, biasc_ref, ov_ref, exp_ref, ks_ref, vs_ref, near_ref,
                gates_ref, owin_ref, o_ref, m_ref, l_ref, acc_ref, *, rep, n_sel):
    g = pl.program_id(1)
    i = pl.program_id(2)
    t0 = i * TQ
    nc = kc_ref.shape[2]
    nblk = ov_ref.shape[1]
    scale = HEAD_DIM ** -0.5
    qall = jnp.concatenate([q_ref[0, :, r * HEAD_DIM:(r + 1) * HEAD_DIM] for r in range(rep)], axis=0)

    tpos = t0 + lax.broadcasted_iota(i32, (TQ, nc), 0)
    cmp_end = lax.broadcasted_iota(i32, (TQ, nc), 1) * CMP_STRIDE + (CMP_BLOCK - 1)
    valid_c = tpos >= cmp_end
    s_c = _dot_nt(qall, kc_ref[0, 0]) * scale
    p_rows = []
    p_sum = jnp.zeros((TQ, nc), f32)
    for r in range(rep):
        sr = jnp.where(valid_c, s_c[r * TQ:(r + 1) * TQ] + biasc_ref[r], NEG)
        m = jnp.max(sr, axis=-1, keepdims=True)
        e = jnp.where(valid_c, jnp.exp(sr - m), 0.0)
        l = jnp.sum(e, axis=-1, keepdims=True)
        p = e * (1.0 / jnp.maximum(l, 1e-30))
        p_rows.append(p.astype(bf16))
        p_sum = p_sum + p
    o_cmp = jnp.dot(jnp.concatenate(p_rows, axis=0), vc_ref[0, 0], preferred_element_type=f32)

    imp = jnp.dot(p_sum, ov_ref[...], preferred_element_type=f32)
    blk = lax.broadcasted_iota(i32, (TQ, nblk), 1)
    tq = t0 + lax.broadcasted_iota(i32, (TQ, nblk), 0)
    cur = tq // SLC_BLOCK
    forced = (blk == 0) | (blk == cur) | (blk == cur - 1)
    score = jnp.where(forced, POS_BIG, jnp.where(blk * SLC_BLOCK <= tq, imp, NEG))
    rank = jnp.zeros((TQ, nblk), i32)
    for kk in range(nblk):
        ck = score[:, kk:kk + 1]
        later = jnp.where(blk > kk, 1, 0)
        rank = rank + jnp.where(ck > score, 1, jnp.where(ck == score, later, 0))
    sel = jnp.where(rank < n_sel, 1.0, 0.0).astype(bf16)

    m_ref[...] = jnp.full(m_ref.shape, NEG, f32)
    l_ref[...] = jnp.zeros(l_ref.shape, f32)
    acc_ref[...] = jnp.zeros(acc_ref.shape, f32)
    fars = [far_ref[g * rep + r] for r in range(rep)]
    row = lax.broadcasted_iota(i32, (TQ, TQ), 0)
    col = lax.broadcasted_iota(i32, (TQ, TQ), 1)

    def tile_mask(j):
        return jnp.dot(sel, exp_ref[j], preferred_element_type=f32) > 0.5

    def kv_tile(j):
        rows = pl.ds(pl.multiple_of(j * TQ, TQ), TQ)
        return ks_ref[0, 0, rows, :], vs_ref[0, 0, rows, :]

    def far_body(j, carry):
        ks, vs = kv_tile(j)
        _flash_step(qall, ks, vs, fars, tile_mask(j), m_ref, l_ref, acc_ref, 0)
        return carry

    lax.fori_loop(0, jnp.maximum(i - 1, 0), far_body, 0)

    @pl.when(i >= 1)
    def _():
        ks, vs = kv_tile(i - 1)
        _flash_step(qall, ks, vs, [near_ref[r, :, 0:TQ] for r in range(rep)], tile_mask(i - 1),
                    m_ref, l_ref, acc_ref, 0)

    ks, vs = kv_tile(i)
    diag_mask = (row >= col) & tile_mask(i)
    _flash_step(qall, ks, vs, [near_ref[r, :, TQ:2 * TQ] for r in range(rep)], diag_mask, m_ref, l_ref, acc_ref, 0)
    o_slc = acc_ref[...] / l_ref[...]

    sig = jax.nn.sigmoid(gates_ref[0, 0])
    outs = []
    for r in range(rep):
        rows = slice(r * TQ, (r + 1) * TQ)
        o_win = owin_ref[0, :, r * HEAD_DIM:(r + 1) * HEAD_DIM]
        outs.append(sig[:, r:r + 1] * o_cmp[rows] + sig[:, rep + r:rep + r + 1] * o_slc[rows]
                    + sig[:, 2 * rep + r:2 * rep + r + 1] * o_win)
    o_ref[0] = jnp.concatenate(outs, axis=-1).astype(o_ref.dtype)


def _nsa_mix_rowmajor(q, kc, vc, bias_c, ks, vs, near, far, gates, o_win):
    bn, s, hq = q.shape
    g = kc.shape[1]
    nc = kc.shape[2]
    rep = hq // (g * HEAD_DIM)
    nblk = s // SLC_BLOCK
    n_sel = min(SLC_TOPN, nblk)
    nt = s // TQ
    ci = np.arange(nc)[:, None] * CMP_STRIDE
    sj = np.arange(nblk)[None, :] * SLC_BLOCK
    overlap = np.clip(np.minimum(ci + CMP_BLOCK, sj + SLC_BLOCK) - np.maximum(ci, sj), 0, None) / CMP_BLOCK
    overlap[nc - 1] = 0.0
    expand = (np.arange(nblk)[None, :, None] == (np.arange(nt)[:, None, None] * TQ
                                                 + np.arange(TQ)[None, None, :]) // SLC_BLOCK)
    kern = functools.partial(_nsa_kernel_rowmajor, rep=rep, n_sel=n_sel)
    rw = rep * HEAD_DIM
    return pl.pallas_call(
        kern,
        grid=(bn, g, nt),
        in_specs=[pl.BlockSpec(memory_space=pltpu.SMEM),
                  pl.BlockSpec((1, TQ, rw), lambda b, gg, i: (b, i, gg)),
                  pl.BlockSpec((1, 1, nc, HEAD_DIM), lambda b, gg, i: (b, gg, 0, 0)),
                  pl.BlockSpec((1, 1, nc, HEAD_DIM), lambda b, gg, i: (b, gg, 0, 0)),
                  pl.BlockSpec((rep, TQ, nc), lambda b, gg, i: (gg, i, 0)),
                  pl.BlockSpec((nc, nblk), lambda b, gg, i: (0, 0)),
                  pl.BlockSpec((nt, nblk, TQ), lambda b, gg, i: (0, 0, 0)),
                  pl.BlockSpec((1, 1, s, HEAD_DIM), lambda b, gg, i: (b, gg, 0, 0)),
                  pl.BlockSpec((1, 1, s, HEAD_DIM), lambda b, gg, i: (b, gg, 0, 0)),
                  pl.BlockSpec((rep, TQ, 2 * TQ), lambda b, gg, i: (gg, 0, 0)),
                  pl.BlockSpec((1, 1, TQ, 3 * rep), lambda b, gg, i: (b, gg, i, 0)),
                  pl.BlockSpec((1, TQ, rw), lambda b, gg, i: (b, i, gg))],
        out_specs=pl.BlockSpec((1, TQ, rw), lambda b, gg, i: (b, i, gg)),
        out_shape=jax.ShapeDtypeStruct((bn, s, hq), bf16),
        scratch_shapes=[pltpu.VMEM((rep * TQ, 1), f32), pltpu.VMEM((rep * TQ, 1), f32),
                        pltpu.VMEM((rep * TQ, HEAD_DIM), f32)],
        compiler_params=_cparams(("parallel", "parallel", "arbitrary")),
        name="nsa_mix",
    )(far, q, kc, vc, bias_c, jnp.asarray(overlap, f32), jnp.asarray(expand, bf16), ks, vs, near, gates, o_win)


def _dsa_kernel_rowmajor(far_ref, q_ref, k_ref, v_ref, qi_ref, ki_ref, wi_ref, near_ref, tri_ref, o_ref,
                         keys_ref, mask_ref, m_ref, l_ref, acc_ref, *, topk, head_chunk):
    i = pl.program_id(1)
    t0 = i * TQ
    nh = q_ref.shape[2] // HEAD_DIM
    nih = qi_ref.shape[2] // IDX_DIM
    row = lax.broadcasted_iota(i32, (TQ, TQ), 0)
    col = lax.broadcasted_iota(i32, (TQ, TQ), 1)
    n_sel_tiles = jnp.maximum(i + 1, -(-topk // TQ))

    qi_all = jnp.concatenate([qi_ref[0, :, h * IDX_DIM:(h + 1) * IDX_DIM] for h in range(nih)], axis=0)
    wi = wi_ref[0]

    def score_body(j, carry):
        kt = ki_ref[0, pl.ds(pl.multiple_of(j * TQ, TQ), TQ), :]
        rel = jnp.maximum(_dot_nt(qi_all, kt), 0.0)
        sc = jnp.zeros((TQ, TQ), f32)
        for h in range(nih):
            sc = sc + wi[:, h:h + 1] * rel[h * TQ:(h + 1) * TQ]
        sc = jnp.where(j * TQ + col <= t0 + row, sc, NEG)
        sc = jnp.where(sc == 0.0, 0.0, sc)
        bits = pltpu.bitcast(sc, i32)
        keys_ref[j] = jnp.where(bits < 0, bits ^ 0x7FFFFFFF, bits)
        return carry

    lax.fori_loop(0, n_sel_tiles, score_body, 0)

    def count(pred_fn):
        def body(j, cnt):
            return cnt + jnp.where(pred_fn(keys_ref[j]), 1.0, 0.0)
        cnt = lax.fori_loop(0, n_sel_tiles, body, jnp.zeros((TQ, TQ), f32))
        return jnp.broadcast_to(jnp.sum(cnt, axis=-1, keepdims=True), (TQ, TQ))

    def bit_body(t, prefix):
        cand = prefix ^ jnp.left_shift(jnp.int32(1), 31 - t)
        return jnp.where(count(lambda kj: kj >= cand) >= topk, cand, prefix)

    thr = lax.fori_loop(0, 32, bit_body, jnp.full((TQ, TQ), INT_MIN, i32))
    need = topk - count(lambda kj: kj > thr)

    def mask_body(j, running):
        kj = keys_ref[j]
        tie = jnp.where(kj == thr, 1.0, 0.0)
        before = jnp.dot(tie.astype(bf16), tri_ref[...], preferred_element_type=f32) + running
        take = jnp.where(kj > thr, 1.0, jnp.where(before < need, tie, 0.0))
        mask_ref[j] = jnp.where(j * TQ + col <= t0 + row, take, 0.0)
        return running + jnp.broadcast_to(jnp.sum(tie, axis=-1, keepdims=True), (TQ, TQ))

    lax.fori_loop(0, i + 1, mask_body, jnp.zeros((TQ, TQ), f32))

    m_ref[...] = jnp.full(m_ref.shape, NEG, f32)
    l_ref[...] = jnp.zeros(l_ref.shape, f32)
    acc_ref[...] = jnp.zeros(acc_ref.shape, f32)
    qall = jnp.concatenate([q_ref[0, :, h * HEAD_DIM:(h + 1) * HEAD_DIM] for h in range(nh)], axis=0)
    hc = head_chunk

    def attend(j, bias_fn):
        rows = pl.ds(pl.multiple_of(j * TQ, TQ), TQ)
        ks = k_ref[0, rows, :]
        vs = v_ref[0, rows, :]
        mask = mask_ref[j] > 0.5
        for c in range(nh // hc):
            _flash_step(qall[c * hc * TQ:(c + 1) * hc * TQ], ks, vs, [bias_fn(c * hc + r) for r in range(hc)],
                        mask, m_ref, l_ref, acc_ref, c * hc * TQ)

    def far_body(j, carry):
        attend(j, lambda h: far_ref[h])
        return carry

    lax.fori_loop(0, jnp.maximum(i - 1, 0), far_body, 0)

    @pl.when(i >= 1)
    def _():
        attend(i - 1, lambda h: near_ref[h, :, 0:TQ])

    attend(i, lambda h: near_ref[h, :, TQ:2 * TQ])
    o = acc_ref[...] / l_ref[...]
    o_ref[0] = jnp.concatenate([o[h * TQ:(h + 1) * TQ] for h in range(nh)], axis=-1).astype(o_ref.dtype)


def _dsa_attention_rowmajor(q, k, v, qi, ki, wi, near, far):
    bn, s, hq = q.shape
    nh = hq // HEAD_DIM
    nt = s // TQ
    topk = min(C_TOPK_MAX, s // 4)
    tri = np.triu(np.ones((TQ, TQ), np.float32), 1)
    kern = functools.partial(_dsa_kernel_rowmajor, topk=topk, head_chunk=4)
    nt_scr = max(nt, -(-topk // TQ))
    return pl.pallas_call(
        kern,
        grid=(bn, nt),
        in_specs=[pl.BlockSpec(memory_space=pltpu.SMEM),
                  pl.BlockSpec((1, TQ, hq), lambda b, i: (b, i, 0)),
                  pl.BlockSpec((1, s, HEAD_DIM), lambda b, i: (b, 0, 0)),
                  pl.BlockSpec((1, s, HEAD_DIM), lambda b, i: (b, 0, 0)),
                  pl.BlockSpec((1, TQ, qi.shape[2]), lambda b, i: (b, i, 0)),
                  pl.BlockSpec((1, s, IDX_DIM), lambda b, i: (b, 0, 0)),
                  pl.BlockSpec((1, TQ, wi.shape[2]), lambda b, i: (b, i, 0)),
                  pl.BlockSpec((nh, TQ, 2 * TQ), lambda b, i: (0, 0, 0)),
                  pl.BlockSpec((TQ, TQ), lambda b, i: (0, 0))],
        out_specs=pl.BlockSpec((1, TQ, hq), lambda b, i: (b, i, 0)),
        out_shape=jax.ShapeDtypeStruct((bn, s, hq), bf16),
        scratch_shapes=[pltpu.VMEM((nt_scr, TQ, TQ), i32), pltpu.VMEM((nt, TQ, TQ), f32),
                        pltpu.VMEM((nh * TQ, 1), f32), pltpu.VMEM((nh * TQ, 1), f32),
                        pltpu.VMEM((nh * TQ, HEAD_DIM), f32)],
        compiler_params=_cparams(("parallel", "arbitrary")),
        name="dsa_attention",
    )(far, q, k, v, qi, ki, wi, near, jnp.asarray(tri, bf16))


V_ROWS = HEAD_DIM + 16


def _masked_flash(i, qs_ref, key_tile, value_tile, mask_bias, bias_ref, m_ref, acc_ref, s_ref, mt_ref, al_ref, chunk):
    width = qs_ref.shape[1]
    m_ref[...] = jnp.full(m_ref.shape, NEG, f32)
    acc_ref[...] = jnp.zeros(acc_ref.shape, f32)

    def scores(j):
        slot = j % 2
        kt = key_tile(j)
        kind = jnp.clip(j - (i - 2), 0, 2)
        maskb = jnp.concatenate([mask_bias(j)] * (chunk // TQ), axis=1)
        for c in range(width // chunk):
            cols = slice(c * chunk, (c + 1) * chunk)
            s = jnp.dot(kt, qs_ref[:, cols], preferred_element_type=f32) + bias_ref[kind, :, cols] + maskb
            s_ref[slot, :, cols] = s
            m_old = m_ref[:, cols]
            m_new = jnp.maximum(m_old, jnp.max(s, axis=0, keepdims=True))
            m_ref[:, cols] = m_new
            mt_ref[slot, :, cols] = m_new
            al_ref[slot, :, cols] = jnp.exp(m_old - m_new)

    def values(j):
        slot = j % 2
        vt = value_tile(j)
        for c in range(width // chunk):
            cols = slice(c * chunk, (c + 1) * chunk)
            p = jnp.exp(s_ref[slot, :, cols] - mt_ref[slot, :, cols])
            acc_ref[:, cols] = (al_ref[slot, :, cols] * acc_ref[:, cols]
                                + jnp.dot(vt, p.astype(bf16), preferred_element_type=f32))

    scores(0)

    def pipe_body(j, carry):
        values(j - 1)
        scores(j)
        return carry

    lax.fori_loop(1, i + 1, pipe_body, 0)
    values(i)


def _flash_bias_tiles(near, far, groups):
    nh = near.shape[0]
    per = nh // groups
    width = per * TQ
    neart = near.reshape(groups, per, TQ, 2, TQ).transpose(0, 3, 4, 1, 2).reshape(groups, 2, TQ, width)
    causal = np.where(np.arange(TQ)[:, None] <= np.arange(TQ)[None, :], 0.0, NEG).astype(np.float32)
    far_tile = jnp.broadcast_to(jnp.repeat(far, TQ).reshape(groups, 1, width), (groups, TQ, width))
    return jnp.stack([far_tile, neart[:, 0], neart[:, 1] + jnp.asarray(np.tile(causal, (1, per)))], axis=1)


def _value_tiles(v):
    lead = v.shape[:-2]
    nt = v.shape[-2] // TQ
    vt = jnp.swapaxes(v.reshape(lead + (nt, TQ, HEAD_DIM)), -1, -2)
    return jnp.concatenate([vt, jnp.ones(lead + (nt, 1, TQ), vt.dtype),
                            jnp.zeros(lead + (nt, V_ROWS - HEAD_DIM - 1, TQ), vt.dtype)], axis=-2)


def _queries_on_lanes(q, g):
    bn, s, hq = q.shape
    rep = hq // (g * HEAD_DIM)
    nt = s // TQ
    return q.reshape(bn, nt, TQ, g, rep, HEAD_DIM).transpose(0, 3, 1, 5, 4, 2).reshape(bn, g, nt, HEAD_DIM, rep * TQ)


def _banded_kernel(qt_ref, k_ref, vt_ref, bias_ref, neg_ref, sink_ref, o_ref, *, rep, kt_tiles, use_sinks):
    i = pl.program_id(1)
    kw = kt_tiles * TQ
    outs = []
    for g in range(qt_ref.shape[1]):
        qs = jnp.concatenate([(qt_ref[0, g, 0].astype(f32) * (HEAD_DIM ** -0.5)).astype(bf16), neg_ref[...]], axis=0)
        k = k_ref[0, g, pl.ds(pl.multiple_of(i * TQ, TQ), kw), :]
        s = jnp.dot(k, qs, preferred_element_type=f32) + bias_ref[g]
        m = jnp.max(s, axis=0, keepdims=True)
        if use_sinks:
            m = jnp.maximum(m, sink_ref[g])
        p = jnp.exp(s - m).astype(bf16)
        vt = jnp.concatenate([vt_ref[0, g, i + t] for t in range(kt_tiles)], axis=1)
        acc = jnp.dot(vt, p, preferred_element_type=f32)
        l = acc[HEAD_DIM:HEAD_DIM + 1]
        if use_sinks:
            l = l + jnp.exp(sink_ref[g] - m)
        ot = acc[0:HEAD_DIM] / l
        outs += [ot[:, r * TQ:(r + 1) * TQ].T for r in range(rep)]
    o_ref[0] = jnp.concatenate(outs, axis=1).astype(o_ref.dtype)


def _banded_gqa(qt, k, v, bias, sinks, window, out_dtype):
    bn, g, nt, _, width = qt.shape
    rep = width // TQ
    s = nt * TQ
    pad = -(-(window - 1) // TQ) * TQ
    kw = pad + TQ
    kd = 2 * HEAD_DIM
    flag = jnp.broadcast_to((jnp.arange(pad + s) < pad).astype(k.dtype)[:, None], (bn, g, pad + s, 1))
    kp = jnp.concatenate([jnp.pad(k, ((0, 0), (0, 0), (pad, 0), (0, 0))), flag,
                          jnp.zeros((bn, g, pad + s, kd - HEAD_DIM - 1), k.dtype)], axis=-1)
    vt = _value_tiles(jnp.pad(v, ((0, 0), (0, 0), (pad, 0), (0, 0))))
    neg = np.zeros((kd - HEAD_DIM, width), np.float32)
    neg[0] = NEG
    dist = np.arange(TQ)[:, None] + pad - np.arange(kw)[None, :]
    band = jnp.asarray((dist >= 0) & (dist < window))
    bias_t = jnp.where(band, bias, NEG).reshape(g, rep, TQ, kw).transpose(0, 3, 1, 2).reshape(g, kw, width)
    use_sinks = sinks is not None
    sink_rows = (jnp.repeat(sinks.astype(f32), TQ) if use_sinks else jnp.zeros((g * width,), f32)).reshape(g, 1, width)
    kern = functools.partial(_banded_kernel, rep=rep, kt_tiles=kw // TQ, use_sinks=use_sinks)
    rw = rep * HEAD_DIM
    return pl.pallas_call(
        kern,
        grid=(bn, nt),
        in_specs=[pl.BlockSpec((1, g, 1, HEAD_DIM, width), lambda b, i: (b, 0, i, 0, 0)),
                  pl.BlockSpec((1, g, pad + s, kd), lambda b, i: (b, 0, 0, 0)),
                  pl.BlockSpec((1, g, (pad + s) // TQ, V_ROWS, TQ), lambda b, i: (b, 0, 0, 0, 0)),
                  pl.BlockSpec((g, kw, width), lambda b, i: (0, 0, 0)),
                  pl.BlockSpec((kd - HEAD_DIM, width), lambda b, i: (0, 0)),
                  pl.BlockSpec((g, 1, width), lambda b, i: (0, 0, 0))],
        out_specs=pl.BlockSpec((1, TQ, g * rw), lambda b, i: (b, i, 0)),
        out_shape=jax.ShapeDtypeStruct((bn, s, g * rw), out_dtype),
        compiler_params=_cparams(("parallel", "arbitrary")),
        name="banded_gqa_w%d" % window,
    )(qt, kp, vt, bias_t, jnp.asarray(neg, bf16), sink_rows)


def _nsa_kernel(qt_ref, kc_ref, vct_ref, biasc_ref, ovt_ref, ks_ref, vst_ref, bias_ref, gt_ref, gn_ref, owin_ref,
                o_ref, qs_ref, selb_ref, m_ref, acc_ref, s_ref, mt_ref, al_ref, *, rep, n_sel):
    i = pl.program_id(2)
    t0 = i * TQ
    nc = kc_ref.shape[2]
    nblk = ovt_ref.shape[0]
    qs_ref[...] = (qt_ref[0, 0, 0].astype(f32) * (HEAD_DIM ** -0.5)).astype(bf16)

    s_c = jnp.dot(kc_ref[0, 0], qs_ref[...], preferred_element_type=f32)
    cmp_end = lax.broadcasted_iota(i32, (nc, TQ), 0) * CMP_STRIDE + (CMP_BLOCK - 1)
    valid_c = t0 + lax.broadcasted_iota(i32, (nc, TQ), 1) >= cmp_end
    p_cols = []
    p_sum = jnp.zeros((nc, TQ), f32)
    for r in range(rep):
        sr = jnp.where(valid_c, s_c[:, r * TQ:(r + 1) * TQ] + biasc_ref[r, 0], NEG)
        m = jnp.max(sr, axis=0, keepdims=True)
        e = jnp.where(valid_c, jnp.exp(sr - m), 0.0)
        l = jnp.sum(e, axis=0, keepdims=True)
        p = e * (1.0 / jnp.maximum(l, 1e-30))
        p_cols.append(p.astype(bf16))
        p_sum = p_sum + p
    o_cmp = jnp.dot(vct_ref[0, 0], jnp.concatenate(p_cols, axis=1), preferred_element_type=f32)

    imp = jnp.dot(ovt_ref[...], p_sum, preferred_element_type=f32)
    blk = lax.broadcasted_iota(i32, (nblk, TQ), 0)
    tq = t0 + lax.broadcasted_iota(i32, (nblk, TQ), 1)
    cur = tq // SLC_BLOCK
    forced = (blk == 0) | (blk == cur) | (blk == cur - 1)
    score = jnp.where(forced, POS_BIG, jnp.where(blk * SLC_BLOCK <= tq, imp, NEG))
    rank = jnp.zeros((nblk, TQ), i32)
    for kk in range(nblk):
        ck = score[kk:kk + 1, :]
        rank = rank + jnp.where(ck > score, 1, jnp.where(ck == score, jnp.where(blk > kk, 1, 0), 0))
    selb_ref[...] = jnp.where(rank < n_sel, 0.0, NEG)

    per_tile = TQ // SLC_BLOCK

    def mask_bias(j):
        rows = [jnp.broadcast_to(selb_ref[pl.ds(per_tile * j + t, 1), :], (SLC_BLOCK, TQ)) for t in range(per_tile)]
        return jnp.concatenate(rows, axis=0)

    _masked_flash(i, qs_ref,
                  lambda j: ks_ref[0, 0, pl.ds(pl.multiple_of(j * TQ, TQ), TQ), :],
                  lambda j: vst_ref[0, 0, j],
                  mask_bias, bias_ref.at[0], m_ref, acc_ref, s_ref, mt_ref, al_ref, rep * TQ)
    acc = acc_ref[...]
    o_slc = acc[0:HEAD_DIM] / acc[HEAD_DIM:HEAD_DIM + 1]

    sig_t = jax.nn.sigmoid(gt_ref[0, 0, 0])
    sig_n = jax.nn.sigmoid(gn_ref[0, 0])
    outs = []
    for r in range(rep):
        cols = slice(r * TQ, (r + 1) * TQ)
        mixed = sig_t[r:r + 1] * o_cmp[:, cols] + sig_t[rep + r:rep + r + 1] * o_slc[:, cols]
        o_win = owin_ref[0, :, r * HEAD_DIM:(r + 1) * HEAD_DIM]
        outs.append(mixed.T + sig_n[:, 2 * rep + r:2 * rep + r + 1] * o_win)
    o_ref[0] = jnp.concatenate(outs, axis=1).astype(o_ref.dtype)


def _nsa_mix(qt, kc, vc, bias_c, ks, vs, near, far, gates, o_win):
    bn, g, nt, _, width = qt.shape
    s = nt * TQ
    nc = kc.shape[2]
    rep = width // TQ
    hq = g * rep * HEAD_DIM
    nblk = s // SLC_BLOCK
    n_sel = min(SLC_TOPN, nblk)
    ci = np.arange(nc)[:, None] * CMP_STRIDE
    sj = np.arange(nblk)[None, :] * SLC_BLOCK
    overlap = np.clip(np.minimum(ci + CMP_BLOCK, sj + SLC_BLOCK) - np.maximum(ci, sj), 0, None) / CMP_BLOCK
    overlap[nc - 1] = 0.0
    gates_t = gates.reshape(bn, g, nt, TQ, 3 * rep).transpose(0, 1, 2, 4, 3)
    kern = functools.partial(_nsa_kernel, rep=rep, n_sel=n_sel)
    rw = rep * HEAD_DIM
    return pl.pallas_call(
        kern,
        grid=(bn, g, nt),
        in_specs=[pl.BlockSpec((1, 1, 1, HEAD_DIM, width), lambda b, gg, i: (b, gg, i, 0, 0)),
                  pl.BlockSpec((1, 1, nc, HEAD_DIM), lambda b, gg, i: (b, gg, 0, 0)),
                  pl.BlockSpec((1, 1, HEAD_DIM, nc), lambda b, gg, i: (b, gg, 0, 0)),
                  pl.BlockSpec((rep, 1, nc, TQ), lambda b, gg, i: (gg, i, 0, 0)),
                  pl.BlockSpec((nblk, nc), lambda b, gg, i: (0, 0)),
                  pl.BlockSpec((1, 1, s, HEAD_DIM), lambda b, gg, i: (b, gg, 0, 0)),
                  pl.BlockSpec((1, 1, nt, V_ROWS, TQ), lambda b, gg, i: (b, gg, 0, 0, 0)),
                  pl.BlockSpec((1, 3, TQ, width), lambda b, gg, i: (gg, 0, 0, 0)),
                  pl.BlockSpec((1, 1, 1, 3 * rep, TQ), lambda b, gg, i: (b, gg, i, 0, 0)),
                  pl.BlockSpec((1, 1, TQ, 3 * rep), lambda b, gg, i: (b, gg, i, 0)),
                  pl.BlockSpec((1, TQ, rw), lambda b, gg, i: (b, i, gg))],
        out_specs=pl.BlockSpec((1, TQ, rw), lambda b, gg, i: (b, i, gg)),
        out_shape=jax.ShapeDtypeStruct((bn, s, hq), bf16),
        scratch_shapes=[pltpu.VMEM((HEAD_DIM, width), bf16), pltpu.VMEM((nblk, TQ), f32),
                        pltpu.VMEM((1, width), f32), pltpu.VMEM((V_ROWS, width), f32),
                        pltpu.VMEM((2, TQ, width), f32), pltpu.VMEM((2, 1, width), f32),
                        pltpu.VMEM((2, 1, width), f32)],
        compiler_params=_cparams(("parallel", "parallel", "arbitrary")),
        name="nsa_mix",
    )(qt, kc, jnp.swapaxes(vc, 2, 3), bias_c, jnp.asarray(overlap.T, f32), ks, _value_tiles(vs),
      _flash_bias_tiles(near, far, g), gates_t, gates, o_win)


def _dsa_kernel(qt_ref, k_ref, vt_ref, qit_ref, ki_ref, wit_ref, bias_ref, tri_ref, o_ref,
                keys_ref, qs_ref, m_ref, acc_ref, s_ref, mt_ref, al_ref, *, topk, chunk):
    i = pl.program_id(1)
    t0 = i * TQ
    width = qt_ref.shape[3]
    nh = width // TQ
    nih = qit_ref.shape[3] // TQ
    krow = lax.broadcasted_iota(i32, (TQ, TQ), 0)
    qcol = lax.broadcasted_iota(i32, (TQ, TQ), 1)
    n_pairs = jnp.maximum(i // 2 + 1, -(-topk // (2 * TQ)))

    qit = qit_ref[0, 0]
    wit = wit_ref[0, 0]

    def score_tile(j):
        kt = ki_ref[0, pl.ds(pl.multiple_of(j * TQ, TQ), TQ), :]
        rel = jnp.maximum(jnp.dot(kt, qit, preferred_element_type=f32), 0.0)
        sc = jnp.zeros((TQ, TQ), f32)
        for h in range(nih):
            sc = sc + wit[h:h + 1, :] * rel[:, h * TQ:(h + 1) * TQ]
        sc = jnp.where(j * TQ + krow <= t0 + qcol, sc, NEG)
        sc = jnp.where(sc == 0.0, 0.0, sc)
        bits = pltpu.bitcast(sc, i32)
        keys_ref[j] = jnp.where(bits < 0, bits ^ 0x7FFFFFFF, bits)

    def score_body(jj, carry):
        score_tile(2 * jj)
        score_tile(2 * jj + 1)
        return carry

    lax.fori_loop(0, n_pairs, score_body, 0)

    def count(pred_fn):
        def body(jj, cnt):
            return (cnt + jnp.where(pred_fn(keys_ref[2 * jj]), 1.0, 0.0)
                    + jnp.where(pred_fn(keys_ref[2 * jj + 1]), 1.0, 0.0))
        cnt = lax.fori_loop(0, n_pairs, body, jnp.zeros((TQ, TQ), f32))
        return jnp.sum(cnt, axis=0, keepdims=True)

    def bit_body(t, prefix):
        cand = prefix ^ jnp.left_shift(jnp.int32(1), 31 - t)
        return jnp.where(count(lambda kj: kj >= cand) >= topk, cand, prefix)

    thr = lax.fori_loop(0, 32, bit_body, jnp.full((1, TQ), INT_MIN, i32))
    n_ge = count(lambda kj: kj >= thr)

    @pl.when(jnp.max(n_ge) > topk)
    def _():
        need = topk - count(lambda kj: kj > thr)

        def fix_body(j, running):
            kj = keys_ref[j]
            tie = jnp.where(kj == thr, 1.0, 0.0)
            before = jnp.dot(tri_ref[...], tie.astype(bf16), preferred_element_type=f32) + running
            keys_ref[j] = jnp.where(before >= need, jnp.where(kj == thr, kj - 1, kj), kj)
            return running + jnp.sum(tie, axis=0, keepdims=True)

        lax.fori_loop(0, 2 * n_pairs, fix_body, jnp.zeros((1, TQ), f32))

    qs_ref[...] = (qt_ref[0, 0].astype(f32) * (HEAD_DIM ** -0.5)).astype(bf16)
    _masked_flash(i, qs_ref,
                  lambda j: k_ref[0, pl.ds(pl.multiple_of(j * TQ, TQ), TQ), :],
                  lambda j: vt_ref[0, j],
                  lambda j: jnp.where(keys_ref[j] >= thr, 0.0, NEG),
                  bias_ref, m_ref, acc_ref, s_ref, mt_ref, al_ref, chunk)
    acc = acc_ref[...]
    ot = acc[0:HEAD_DIM] / acc[HEAD_DIM:HEAD_DIM + 1]
    o_ref[0] = jnp.concatenate([ot[:, h * TQ:(h + 1) * TQ].T for h in range(nh)], axis=1).astype(o_ref.dtype)


def _dsa_attention(q, k, v, qi, ki, wi, near, far):
    bn, s, hq = q.shape
    nh = hq // HEAD_DIM
    nih = qi.shape[2] // IDX_DIM
    nt = s // TQ
    topk = min(C_TOPK_MAX, s // 4)
    width = nh * TQ
    qt = q.reshape(bn, nt, TQ, nh, HEAD_DIM).transpose(0, 1, 4, 3, 2).reshape(bn, nt, HEAD_DIM, width)
    qit = qi.reshape(bn, nt, TQ, nih, IDX_DIM).transpose(0, 1, 4, 3, 2).reshape(bn, nt, IDX_DIM, nih * TQ)
    wit = wi.reshape(bn, nt, TQ, nih).transpose(0, 1, 3, 2)
    vt = _value_tiles(v)
    bias3 = _flash_bias_tiles(near, far, 1)[0]
    tri = np.tril(np.ones((TQ, TQ), np.float32), -1)
    kern = functools.partial(_dsa_kernel, topk=topk, chunk=512)
    nt_scr = max(nt, -(-topk // TQ))
    return pl.pallas_call(
        kern,
        grid=(bn, nt),
        in_specs=[pl.BlockSpec((1, 1, HEAD_DIM, width), lambda b, i: (b, i, 0, 0)),
                  pl.BlockSpec((1, s, HEAD_DIM), lambda b, i: (b, 0, 0)),
                  pl.BlockSpec((1, nt, V_ROWS, TQ), lambda b, i: (b, 0, 0, 0)),
                  pl.BlockSpec((1, 1, IDX_DIM, nih * TQ), lambda b, i: (b, i, 0, 0)),
                  pl.BlockSpec((1, s, IDX_DIM), lambda b, i: (b, 0, 0)),
                  pl.BlockSpec((1, 1, nih, TQ), lambda b, i: (b, i, 0, 0)),
                  pl.BlockSpec((3, TQ, width), lambda b, i: (0, 0, 0)),
                  pl.BlockSpec((TQ, TQ), lambda b, i: (0, 0))],
        out_specs=pl.BlockSpec((1, TQ, hq), lambda b, i: (b, i, 0)),
        out_shape=jax.ShapeDtypeStruct((bn, s, hq), bf16),
        scratch_shapes=[pltpu.VMEM((nt_scr, TQ, TQ), i32), pltpu.VMEM((HEAD_DIM, width), bf16),
                        pltpu.VMEM((1, width), f32), pltpu.VMEM((V_ROWS, width), f32),
                        pltpu.VMEM((2, TQ, width), f32), pltpu.VMEM((2, 1, width), f32),
                        pltpu.VMEM((2, 1, width), f32)],
        compiler_params=_cparams(("parallel", "arbitrary")),
        name="dsa_attention",
    )(qt, k, vt, qit, ki, wit, bias3, jnp.asarray(tri, bf16))


def _heads_first(a, g):
    bn, s, _ = a.shape
    return a.reshape(bn, s, g, HEAD_DIM).transpose(0, 2, 1, 3)


def _even_layer(x, table_t, w_in, sinks, cmpk_pos, cmpk_w1, cmpk_w2, cmpv_pos, cmpv_w1, cmpv_w2,
                w_out, ln1_g, ln1_b, ffn_gate, ffn_up, ffn_down, ln2_g, ln2_b):
    bn, s, d = x.shape
    xf = x.reshape(bn * s, d)
    a_q, a_kv, b_q, b_kv = A_HEADS * HEAD_DIM, A_KV * HEAD_DIM, B_HEADS * HEAD_DIM, B_KV * HEAD_DIM
    n_main = a_q + 2 * a_kv + b_q + 6 * b_kv
    rb = B_HEADS // B_KV
    main = _matmul(xf, w_in[:, :n_main].astype(bf16), bf16, tn=512).reshape(bn, s, n_main)
    w_tail = jnp.pad(w_in[:, n_main:], ((0, 0), (0, 128 - 3 * B_HEADS))).astype(bf16)
    gates = _matmul(xf, w_tail, f32)[:, :3 * B_HEADS]
    gates = gates.reshape(bn, s, 3, B_KV, rb).transpose(0, 3, 1, 2, 4).reshape(bn, B_KV, s, 3 * rb)
    cols = np.cumsum([0, a_q, a_kv, a_kv, b_q] + [b_kv] * 6)
    qa, ka, va, qb, kc_in, vc_in, ksl, vsl, kwn, vwn = [main[:, :, cols[t]:cols[t + 1]] for t in range(10)]

    near_idx = _bucket_of(np.arange(TQ)[:, None] + TQ - np.arange(2 * TQ)[None, :])
    near = _bias_expand(table_t, near_idx, TQ)
    pad_b = -(-(B_WINDOW - 1) // TQ) * TQ
    win_idx = _bucket_of(np.arange(TQ)[:, None] + pad_b - np.arange(pad_b + TQ)[None, :])
    bias_win = _bias_expand(table_t[A_HEADS:A_HEADS + B_HEADS], win_idx, TQ)
    nc = s // CMP_STRIDE
    nt = s // TQ
    cmp_idx = _bucket_of((np.arange(nt)[:, None, None] * TQ + np.arange(TQ)[None, None, :])
                         - (np.arange(nc)[None, :, None] * CMP_STRIDE + CMP_BLOCK - 1)).reshape(nt * nc, TQ)
    bias_c = _bias_expand(table_t[A_HEADS:A_HEADS + B_HEADS], cmp_idx, min(nt * nc, 512))
    bias_c = bias_c.reshape(B_HEADS, nt, nc, TQ)
    far = table_t[:, NUM_BUCKETS - 1]

    qt_a = _queries_on_lanes(qa, A_KV)
    qt_b = _queries_on_lanes(qb, B_KV)
    out_a = _banded_gqa(qt_a, _heads_first(ka, A_KV), _heads_first(va, A_KV), near[:A_HEADS], sinks, A_WINDOW, bf16)
    o_win = _banded_gqa(qt_b, _heads_first(kwn, B_KV), _heads_first(vwn, B_KV), bias_win, None, B_WINDOW, f32)
    kc = _nsa_compress(_heads_first(kc_in, B_KV), cmpk_pos, cmpk_w1, cmpk_w2)
    vc = _nsa_compress(_heads_first(vc_in, B_KV), cmpv_pos, cmpv_w1, cmpv_w2)
    out_b = _nsa_mix(qt_b, kc, vc, bias_c, _heads_first(ksl, B_KV), _heads_first(vsl, B_KV),
                     near[A_HEADS:A_HEADS + B_HEADS], far[A_HEADS:A_HEADS + B_HEADS], gates, o_win)
    mix = jnp.concatenate([out_a, out_b], axis=-1).reshape(bn * s, a_q + b_q)
    x1 = _proj_res_ln(mix, w_out.astype(bf16), xf, ln1_g, ln1_b)
    x2 = _ffn_res_ln(x1, ffn_gate.astype(bf16), ffn_up.astype(bf16), ffn_down.astype(bf16), ln2_g, ln2_b)
    return x2.reshape(bn, s, d), near, far


def _odd_layer(x, near, far, w_in, w_out, ln1_g, ln1_b, router, exp_gate, exp_up, exp_down, ln2_g, ln2_b):
    bn, s, d = x.shape
    xf = x.reshape(bn * s, d)
    c_q = C_HEADS * HEAD_DIM
    n_main = c_q + 2 * HEAD_DIM + IDX_HEADS * IDX_DIM
    main = _matmul(xf, w_in[:, :n_main].astype(bf16), bf16, tn=n_main // 2 if n_main % 256 == 0 else n_main)
    main = main.reshape(bn, s, n_main)
    w_tail = jnp.pad(w_in[:, n_main:], ((0, 0), (0, 128 - IDX_DIM - IDX_HEADS))).astype(bf16)
    tail = _matmul(xf, w_tail, f32).reshape(bn, s, 128)
    q = main[:, :, :c_q]
    k = main[:, :, c_q:c_q + HEAD_DIM]
    v = main[:, :, c_q + HEAD_DIM:c_q + 2 * HEAD_DIM]
    qi = main[:, :, c_q + 2 * HEAD_DIM:]
    ki = tail[:, :, :IDX_DIM].astype(bf16)
    wi = tail[:, :, IDX_DIM:IDX_DIM + IDX_HEADS]
    mix = _dsa_attention(q, k, v, qi, ki, wi, near[:C_HEADS], far[:C_HEADS]).reshape(bn * s, c_q)
    x1 = _proj_res_ln(mix, w_out.astype(bf16), xf, ln1_g, ln1_b)
    x2 = _moe_res_ln(x1, router, exp_gate.astype(bf16), exp_up.astype(bf16), exp_down.astype(bf16), ln2_g, ln2_b)
    return x2.reshape(bn, s, d)


def kernel(x, rel_bias, l0_w_in, l0_sinks, l0_cmpk_pos, l0_cmpk_w1, l0_cmpk_w2, l0_cmpv_pos, l0_cmpv_w1,
           l0_cmpv_w2, l0_w_out, l0_ln1_g, l0_ln1_b, l0_ffn_gate, l0_ffn_up, l0_ffn_down, l0_ln2_g, l0_ln2_b,
           l1_w_in, l1_w_out, l1_ln1_g, l1_ln1_b, l1_router, l1_exp_gate, l1_exp_up, l1_exp_down, l1_ln2_g,
           l1_ln2_b):
    table_t = rel_bias.T.astype(f32)
    x, near, far = _even_layer(x, table_t, l0_w_in, l0_sinks, l0_cmpk_pos, l0_cmpk_w1, l0_cmpk_w2, l0_cmpv_pos,
                               l0_cmpv_w1, l0_cmpv_w2, l0_w_out, l0_ln1_g, l0_ln1_b, l0_ffn_gate, l0_ffn_up,
                               l0_ffn_down, l0_ln2_g, l0_ln2_b)
    return _odd_layer(x, near, far, l1_w_in, l1_w_out, l1_ln1_g, l1_ln1_b, l1_router, l1_exp_gate, l1_exp_up,
                      l1_exp_down, l1_ln2_g, l1_ln2_b)
```

```python
import functools
import math

import numpy as np
import jax
import jax.numpy as jnp
from jax import lax
from jax.experimental import pallas as pl
from jax.experimental.pallas import tpu as pltpu

f32 = jnp.float32
bf16 = jnp.bfloat16
i32 = jnp.int32

D_MODEL = 1024
HEAD_DIM = 64
NUM_BUCKETS = 32
MAX_DISTANCE = 128
A_HEADS, A_KV, A_WINDOW = 8, 2, 128
B_HEADS, B_KV, B_WINDOW = 8, 2, 512
CMP_BLOCK, CMP_STRIDE = 32, 16
SLC_BLOCK, SLC_TOPN = 64, 16
C_HEADS, IDX_HEADS, IDX_DIM, C_TOPK_MAX = 16, 8, 32, 256
D_FF, N_EXPERTS = 2816, 8
DEPTH = 2
ALPHA = (2.0 * DEPTH) ** 0.25
LN_EPS = 1e-5
NEG = -1e30
POS_BIG = 1e30
INT_MIN = -(2 ** 31)

TQ = 128
FF_CHUNK = 256
VMEM_LIMIT = 56 * 1024 * 1024


def _cparams(sem):
    return pltpu.CompilerParams(dimension_semantics=sem, vmem_limit_bytes=VMEM_LIMIT)


def _bucket_table(max_dist):
    n = np.arange(max_dist + 1)
    max_exact = NUM_BUCKETS // 2
    nf = np.maximum(n, 1).astype(np.float64)
    large = max_exact + (np.log(nf / max_exact) / math.log(MAX_DISTANCE / max_exact)
                         * (NUM_BUCKETS - max_exact)).astype(np.int64)
    large = np.minimum(large, NUM_BUCKETS - 1)
    return np.where(n < max_exact, n, large).astype(np.int32)


def _bucket_of(dist):
    dist = np.maximum(dist, 0)
    return _bucket_table(int(dist.max()))[dist]


def _mm_kernel(x_ref, w_ref, o_ref):
    o_ref[...] = jnp.dot(x_ref[...].astype(bf16), w_ref[...], preferred_element_type=f32).astype(o_ref.dtype)


def _matmul(x, w, out_dtype, tm=512, tn=None):
    m, k = x.shape
    n = w.shape[1]
    tn = n if tn is None else tn
    return pl.pallas_call(
        _mm_kernel,
        grid=(m // tm, n // tn),
        in_specs=[pl.BlockSpec((tm, k), lambda i, j: (i, 0)),
                  pl.BlockSpec((k, tn), lambda i, j: (0, j))],
        out_specs=pl.BlockSpec((tm, tn), lambda i, j: (i, j)),
        out_shape=jax.ShapeDtypeStruct((m, n), out_dtype),
        compiler_params=_cparams(("parallel", "arbitrary")),
        name="matmul",
    )(x, w)


def _bias_expand_kernel(tab_ref, idx_ref, o_ref):
    h = pl.program_id(0)
    idx = idx_ref[...]
    acc = jnp.zeros(idx.shape, f32)
    for b in range(NUM_BUCKETS):
        acc = jnp.where(idx == b, tab_ref[h, b], acc)
    o_ref[0] = acc


def _bias_expand(table_t, idx, tr):
    nh = table_t.shape[0]
    r, c = idx.shape
    return pl.pallas_call(
        _bias_expand_kernel,
        grid=(nh, r // tr),
        in_specs=[pl.BlockSpec(memory_space=pltpu.SMEM),
                  pl.BlockSpec((tr, c), lambda h, i: (i, 0))],
        out_specs=pl.BlockSpec((1, tr, c), lambda h, i: (h, i, 0)),
        out_shape=jax.ShapeDtypeStruct((nh, r, c), f32),
        compiler_params=_cparams(("parallel", "arbitrary")),
        name="bias_expand",
    )(table_t, jnp.asarray(idx, dtype=i32))


def _layer_norm(z, g, b):
    mu = jnp.mean(z, axis=-1, keepdims=True)
    zc = z - mu
    var = jnp.mean(zc * zc, axis=-1, keepdims=True)
    return zc * lax.rsqrt(var + LN_EPS) * g + b


def _proj_ln_kernel(mix_ref, w_ref, x_ref, g_ref, b_ref, o_ref):
    y = jnp.dot(mix_ref[...], w_ref[...], preferred_element_type=f32)
    o_ref[...] = _layer_norm(ALPHA * x_ref[...] + y, g_ref[...], b_ref[...])


def _proj_res_ln(mix, w, x, g, b, tm=512):
    m, k = mix.shape
    d = w.shape[1]
    return pl.pallas_call(
        _proj_ln_kernel,
        grid=(m // tm,),
        in_specs=[pl.BlockSpec((tm, k), lambda i: (i, 0)),
                  pl.BlockSpec((k, d), lambda i: (0, 0)),
                  pl.BlockSpec((tm, d), lambda i: (i, 0)),
                  pl.BlockSpec((1, d), lambda i: (0, 0)),
                  pl.BlockSpec((1, d), lambda i: (0, 0))],
        out_specs=pl.BlockSpec((tm, d), lambda i: (i, 0)),
        out_shape=jax.ShapeDtypeStruct((m, d), f32),
        compiler_params=_cparams(("parallel",)),
        name="proj_res_ln",
    )(mix, w, x, g.reshape(1, d), b.reshape(1, d))


def _swiglu_accumulate(xb, wg_ref, wu_ref, wd_ref, acc_ref, widx):
    d_ff = wg_ref.shape[-1]
    for c in range(d_ff // FF_CHUNK):
        cols = slice(c * FF_CHUNK, (c + 1) * FF_CHUNK)
        gate = jnp.dot(xb, wg_ref[widx + (slice(None), cols)], preferred_element_type=f32)
        up = jnp.dot(xb, wu_ref[widx + (slice(None), cols)], preferred_element_type=f32)
        h = jax.nn.silu(gate) * up
        acc_ref[...] += jnp.dot(h.astype(bf16), wd_ref[widx + (cols, slice(None))], preferred_element_type=f32)


def _ffn_kernel(x_ref, wg_ref, wu_ref, wd_ref, g_ref, b_ref, o_ref, acc_ref):
    x = x_ref[...]
    acc_ref[...] = jnp.zeros_like(acc_ref)
    _swiglu_accumulate(x.astype(bf16), wg_ref, wu_ref, wd_ref, acc_ref, ())
    o_ref[...] = _layer_norm(ALPHA * x + acc_ref[...], g_ref[...], b_ref[...])


def _ffn_res_ln(x, wg, wu, wd, g, b, tm=512):
    m, d = x.shape
    d_ff = wg.shape[1]
    return pl.pallas_call(
        _ffn_kernel,
        grid=(m // tm,),
        in_specs=[pl.BlockSpec((tm, d), lambda i: (i, 0)),
                  pl.BlockSpec((d, d_ff), lambda i: (0, 0)),
                  pl.BlockSpec((d, d_ff), lambda i: (0, 0)),
                  pl.BlockSpec((d_ff, d), lambda i: (0, 0)),
                  pl.BlockSpec((1, d), lambda i: (0, 0)),
                  pl.BlockSpec((1, d), lambda i: (0, 0))],
        out_specs=pl.BlockSpec((tm, d), lambda i: (i, 0)),
        out_shape=jax.ShapeDtypeStruct((m, d), f32),
        scratch_shapes=[pltpu.VMEM((tm, d), f32)],
        compiler_params=_cparams(("parallel",)),
        name="ffn_res_ln",
    )(x, wg, wu, wd, g.reshape(1, d), b.reshape(1, d))


MOE_CAP = 384


def _router_gates(x, r_ref, lane):
    logits = jnp.dot(x, r_ref[...], preferred_element_type=f32)
    logits = jnp.where(lane < N_EXPERTS, logits, -jnp.inf)
    m1 = jnp.max(logits, axis=-1, keepdims=True)
    i1 = jnp.min(jnp.where(logits == m1, lane, 128), axis=-1, keepdims=True)
    rest = jnp.where(lane == i1, -jnp.inf, logits)
    m2 = jnp.max(rest, axis=-1, keepdims=True)
    i2 = jnp.min(jnp.where(rest == m2, lane, 128), axis=-1, keepdims=True)
    e2 = jnp.exp(m2 - m1)
    g1 = 1.0 / (1.0 + e2)
    return jnp.where(lane == i1, g1, 0.0) + jnp.where(lane == i2, e2 * g1, 0.0)


def _moe_kernel(x_ref, r_ref, tri_ref, wg_ref, wu_ref, wd_ref, g_ref, b_ref, o_ref,
                comb_ref, pos_ref, combt_ref, post_ref, yc_ref):
    e = pl.program_id(1)
    tm = x_ref.shape[0]
    lane = lax.broadcasted_iota(i32, (tm, 128), 1)

    @pl.when(e == 0)
    def _():
        comb = _router_gates(x_ref[...], r_ref, lane)
        comb_ref[...] = comb
        chosen = jnp.where(comb > 0.0, 1.0, 0.0)
        pos = jnp.dot(tri_ref[...], chosen.astype(bf16), preferred_element_type=f32)
        pos = jnp.where(comb > 0.0, pos, -1.0)
        pos_ref[...] = pos
        for t in range(tm // 128):
            rows = slice(t * 128, (t + 1) * 128)
            combt_ref[:, rows] = comb[rows].T[0:N_EXPERTS]
            post_ref[:, rows] = pos[rows].T[0:N_EXPERTS]
        o_ref[...] = jnp.zeros_like(o_ref)

    gate_col = jnp.sum(jnp.where(lane == e, comb_ref[...], 0.0), axis=-1, keepdims=True)
    pos_col = jnp.sum(jnp.where(lane == e, pos_ref[...], 0.0), axis=-1, keepdims=True).astype(i32)
    pos_row = post_ref[pl.ds(e, 1), :].astype(i32)
    n_tok = jnp.sum(jnp.where(combt_ref[pl.ds(e, 1), :] > 0.0, 1.0, 0.0)).astype(i32)

    def pass_body(p, carry):
        base = p * MOE_CAP
        gather = jnp.where(pos_row == base + lax.broadcasted_iota(i32, (MOE_CAP, tm), 0), 1.0, 0.0).astype(bf16)
        xc = jnp.dot(gather, x_ref[...].astype(bf16), preferred_element_type=f32).astype(bf16)
        yc_ref[...] = jnp.zeros_like(yc_ref)
        _swiglu_accumulate(xc, wg_ref, wu_ref, wd_ref, yc_ref, (0,))
        scatter = jnp.where(pos_col == base + lax.broadcasted_iota(i32, (tm, MOE_CAP), 1), 1.0, 0.0).astype(bf16)
        o_ref[...] += gate_col * jnp.dot(scatter, yc_ref[...].astype(bf16), preferred_element_type=f32)
        return carry

    lax.fori_loop(0, (n_tok + MOE_CAP - 1) // MOE_CAP, pass_body, 0)

    @pl.when(e == N_EXPERTS - 1)
    def _():
        o_ref[...] = _layer_norm(ALPHA * x_ref[...] + o_ref[...], g_ref[...], b_ref[...])


def _moe_res_ln(x, router, wg, wu, wd, g, b, tm=1024):
    m, d = x.shape
    ne, _, d_ff = wg.shape
    tm = min(tm, m)
    router_p = jnp.pad(router, ((0, 0), (0, 128 - ne)))
    tri = jnp.asarray(np.tril(np.ones((tm, tm), np.float32), -1), bf16)
    single = pl.Buffered(1)
    return pl.pallas_call(
        _moe_kernel,
        grid=(m // tm, ne),
        in_specs=[pl.BlockSpec((tm, d), lambda i, e: (i, 0), pipeline_mode=single),
                  pl.BlockSpec((d, 128), lambda i, e: (0, 0), pipeline_mode=single),
                  pl.BlockSpec((tm, tm), lambda i, e: (0, 0), pipeline_mode=single),
                  pl.BlockSpec((1, d, d_ff), lambda i, e: (e, 0, 0)),
                  pl.BlockSpec((1, d, d_ff), lambda i, e: (e, 0, 0)),
                  pl.BlockSpec((1, d_ff, d), lambda i, e: (e, 0, 0)),
                  pl.BlockSpec((1, d), lambda i, e: (0, 0)),
                  pl.BlockSpec((1, d), lambda i, e: (0, 0))],
        out_specs=pl.BlockSpec((tm, d), lambda i, e: (i, 0)),
        out_shape=jax.ShapeDtypeStruct((m, d), f32),
        scratch_shapes=[pltpu.VMEM((tm, 128), f32), pltpu.VMEM((tm, 128), f32),
                        pltpu.VMEM((N_EXPERTS, tm), f32), pltpu.VMEM((N_EXPERTS, tm), f32),
                        pltpu.VMEM((MOE_CAP, d), f32)],
        compiler_params=_cparams(("parallel", "arbitrary")),
        name="moe_res_ln",
    )(x, router_p, tri, wg, wu, wd, g.reshape(1, d), b.reshape(1, d))


def _compress_kernel(blk_ref, pos_ref, w1_ref, w2_ref, o_ref):
    xb = (blk_ref[0].astype(f32) + pos_ref[...]).astype(bf16)
    h = jax.nn.gelu(jnp.dot(xb, w1_ref[...], preferred_element_type=f32))
    o_ref[0] = jnp.dot(h.astype(bf16), w2_ref[...], preferred_element_type=f32).astype(o_ref.dtype)


def _nsa_compress(kv, pos, w1, w2):
    bn, g, s, dh = kv.shape
    nc = s // CMP_STRIDE
    chunks = kv.reshape(bn, g, nc, CMP_STRIDE * dh)
    nxt = jnp.pad(chunks[:, :, 1:], ((0, 0), (0, 0), (0, 1), (0, 0)))
    blocks = jnp.concatenate([chunks, nxt], axis=-1).reshape(bn * g, nc, CMP_BLOCK * dh)
    cin = CMP_BLOCK * dh
    hid = w1.shape[1]
    out = pl.pallas_call(
        _compress_kernel,
        grid=(bn * g,),
        in_specs=[pl.BlockSpec((1, nc, cin), lambda i: (i, 0, 0)),
                  pl.BlockSpec((1, cin), lambda i: (0, 0)),
                  pl.BlockSpec((cin, hid), lambda i: (0, 0)),
                  pl.BlockSpec((hid, dh), lambda i: (0, 0))],
        out_specs=pl.BlockSpec((1, nc, dh), lambda i: (i, 0, 0)),
        out_shape=jax.ShapeDtypeStruct((bn * g, nc, dh), bf16),
        compiler_params=_cparams(("parallel",)),
        name="nsa_compress",
    )(blocks, pos.reshape(1, cin), w1.astype(bf16), w2.astype(bf16))
    return out.reshape(bn, g, nc, dh)


V_ROWS = HEAD_DIM + 16


def _masked_flash(i, qs_ref, key_tile, value_tile, mask_bias, bias_ref, m_ref, acc_ref, s_ref, mt_ref, al_ref, chunk,
                  shared_keys):
    width = qs_ref.shape[1]
    n_chunks = width // chunk
    m_ref[...] = jnp.full(m_ref.shape, NEG, f32)
    acc_ref[...] = jnp.zeros(acc_ref.shape, f32)

    def scores(j):
        slot = j % 2
        kind = jnp.clip(j - (i - 2), 0, 2)
        for c in range(n_chunks):
            if c == 0 or not shared_keys:
                kt = key_tile(j, c)
                maskb = jnp.concatenate([mask_bias(j, c)] * (chunk // TQ), axis=1)
            cols = slice(c * chunk, (c + 1) * chunk)
            s = jnp.dot(kt, qs_ref[:, cols], preferred_element_type=f32) + bias_ref[kind, :, cols] + maskb
            s_ref[slot, :, cols] = s
            m_old = m_ref[:, cols]
            m_new = jnp.maximum(m_old, jnp.max(s, axis=0, keepdims=True))
            m_ref[:, cols] = m_new
            mt_ref[slot, :, cols] = m_new
            al_ref[slot, :, cols] = jnp.exp(m_old - m_new)

    def values(j):
        slot = j % 2
        for c in range(n_chunks):
            if c == 0 or not shared_keys:
                vt = value_tile(j, c)
            cols = slice(c * chunk, (c + 1) * chunk)
            p = jnp.exp(s_ref[slot, :, cols] - mt_ref[slot, :, cols])
            acc_ref[:, cols] = (al_ref[slot, :, cols] * acc_ref[:, cols]
                                + jnp.dot(vt, p.astype(bf16), preferred_element_type=f32))

    scores(0)

    def pipe_body(j, carry):
        values(j - 1)
        scores(j)
        return carry

    lax.fori_loop(1, i + 1, pipe_body, 0)
    values(i)


def _flash_bias_tiles(near, far):
    nh = near.shape[0]
    width = nh * TQ
    neart = near.reshape(nh, TQ, 2, TQ).transpose(2, 3, 0, 1).reshape(2, TQ, width)
    causal = np.where(np.arange(TQ)[:, None] <= np.arange(TQ)[None, :], 0.0, NEG).astype(np.float32)
    far_tile = jnp.broadcast_to(jnp.repeat(far, TQ).reshape(1, width), (TQ, width))
    return jnp.stack([far_tile, neart[0], neart[1] + jnp.asarray(np.tile(causal, (1, nh)))], axis=0)


def _value_tiles(v):
    lead = v.shape[:-2]
    nt = v.shape[-2] // TQ
    vt = jnp.swapaxes(v.reshape(lead + (nt, TQ, HEAD_DIM)), -1, -2)
    return jnp.concatenate([vt, jnp.ones(lead + (nt, 1, TQ), vt.dtype),
                            jnp.zeros(lead + (nt, V_ROWS - HEAD_DIM - 1, TQ), vt.dtype)], axis=-2)


def _queries_on_lanes(q, g):
    bn, s, hq = q.shape
    rep = hq // (g * HEAD_DIM)
    nt = s // TQ
    return q.reshape(bn, nt, TQ, g, rep, HEAD_DIM).transpose(0, 3, 1, 5, 4, 2).reshape(bn, g, nt, HEAD_DIM, rep * TQ)


def _banded_kernel(qt_ref, k_ref, vt_ref, bias_ref, neg_ref, sink_ref, o_ref, *, rep, kt_tiles, use_sinks):
    i = pl.program_id(1)
    kw = kt_tiles * TQ
    outs = []
    for g in range(qt_ref.shape[1]):
        qs = jnp.concatenate([(qt_ref[0, g, 0].astype(f32) * (HEAD_DIM ** -0.5)).astype(bf16), neg_ref[...]], axis=0)
        k = k_ref[0, g, pl.ds(pl.multiple_of(i * TQ, TQ), kw), :]
        s = jnp.dot(k, qs, preferred_element_type=f32) + bias_ref[g]
        m = jnp.max(s, axis=0, keepdims=True)
        if use_sinks:
            m = jnp.maximum(m, sink_ref[g])
        p = jnp.exp(s - m).astype(bf16)
        vt = jnp.concatenate([vt_ref[0, g, i + t] for t in range(kt_tiles)], axis=1)
        acc = jnp.dot(vt, p, preferred_element_type=f32)
        l = acc[HEAD_DIM:HEAD_DIM + 1]
        if use_sinks:
            l = l + jnp.exp(sink_ref[g] - m)
        ot = acc[0:HEAD_DIM] / l
        outs += [ot[:, r * TQ:(r + 1) * TQ].T for r in range(rep)]
    o_ref[0] = jnp.concatenate(outs, axis=1).astype(o_ref.dtype)


def _banded_gqa(qt, k, v, bias, sinks, window, out_dtype):
    bn, g, nt, _, width = qt.shape
    rep = width // TQ
    s = nt * TQ
    pad = -(-(window - 1) // TQ) * TQ
    kw = pad + TQ
    kd = 2 * HEAD_DIM
    flag = jnp.broadcast_to((jnp.arange(pad + s) < pad).astype(k.dtype)[:, None], (bn, g, pad + s, 1))
    kp = jnp.concatenate([jnp.pad(k, ((0, 0), (0, 0), (pad, 0), (0, 0))), flag,
                          jnp.zeros((bn, g, pad + s, kd - HEAD_DIM - 1), k.dtype)], axis=-1)
    vt = _value_tiles(jnp.pad(v, ((0, 0), (0, 0), (pad, 0), (0, 0))))
    neg = np.zeros((kd - HEAD_DIM, width), np.float32)
    neg[0] = NEG
    dist = np.arange(TQ)[:, None] + pad - np.arange(kw)[None, :]
    band = jnp.asarray((dist >= 0) & (dist < window))
    bias_t = jnp.where(band, bias, NEG).reshape(g, rep, TQ, kw).transpose(0, 3, 1, 2).reshape(g, kw, width)
    use_sinks = sinks is not None
    sink_rows = (jnp.repeat(sinks.astype(f32), TQ) if use_sinks else jnp.zeros((g * width,), f32)).reshape(g, 1, width)
    kern = functools.partial(_banded_kernel, rep=rep, kt_tiles=kw // TQ, use_sinks=use_sinks)
    rw = rep * HEAD_DIM
    return pl.pallas_call(
        kern,
        grid=(bn, nt),
        in_specs=[pl.BlockSpec((1, g, 1, HEAD_DIM, width), lambda b, i: (b, 0, i, 0, 0)),
                  pl.BlockSpec((1, g, pad + s, kd), lambda b, i: (b, 0, 0, 0)),
                  pl.BlockSpec((1, g, (pad + s) // TQ, V_ROWS, TQ), lambda b, i: (b, 0, 0, 0, 0)),
                  pl.BlockSpec((g, kw, width), lambda b, i: (0, 0, 0)),
                  pl.BlockSpec((kd - HEAD_DIM, width), lambda b, i: (0, 0)),
                  pl.BlockSpec((g, 1, width), lambda b, i: (0, 0, 0))],
        out_specs=pl.BlockSpec((1, TQ, g * rw), lambda b, i: (b, i, 0)),
        out_shape=jax.ShapeDtypeStruct((bn, s, g * rw), out_dtype),
        compiler_params=_cparams(("parallel", "arbitrary")),
        name="banded_gqa_w%d" % window,
    )(qt, kp, vt, bias_t, jnp.asarray(neg, bf16), sink_rows)


def _nsa_kernel(qt_ref, kc_ref, vct_ref, biasc_ref, ovt_ref, ks_ref, vst_ref, bias_ref, gt_ref, gn_ref, owin_ref,
                o_ref, qs_ref, selb_ref, m_ref, acc_ref, s_ref, mt_ref, al_ref, *, rep, n_sel):
    i = pl.program_id(1)
    t0 = i * TQ
    groups = qt_ref.shape[1]
    nc = kc_ref.shape[2]
    nblk = ovt_ref.shape[0]
    width = rep * TQ
    cmp_end = lax.broadcasted_iota(i32, (nc, TQ), 0) * CMP_STRIDE + (CMP_BLOCK - 1)
    valid_c = t0 + lax.broadcasted_iota(i32, (nc, TQ), 1) >= cmp_end
    blk = lax.broadcasted_iota(i32, (nblk, TQ), 0)
    tq = t0 + lax.broadcasted_iota(i32, (nblk, TQ), 1)
    cur = tq // SLC_BLOCK
    forced = (blk == 0) | (blk == cur) | (blk == cur - 1)
    admitted = blk * SLC_BLOCK <= tq

    o_cmps = []
    for g in range(groups):
        qs = (qt_ref[0, g, 0].astype(f32) * (HEAD_DIM ** -0.5)).astype(bf16)
        qs_ref[:, g * width:(g + 1) * width] = qs

        s_c = jnp.dot(kc_ref[0, g], qs, preferred_element_type=f32)
        p_cols = []
        p_sum = jnp.zeros((nc, TQ), f32)
        for r in range(rep):
            sr = jnp.where(valid_c, s_c[:, r * TQ:(r + 1) * TQ] + biasc_ref[g * rep + r, 0], NEG)
            m = jnp.max(sr, axis=0, keepdims=True)
            e = jnp.where(valid_c, jnp.exp(sr - m), 0.0)
            l = jnp.sum(e, axis=0, keepdims=True)
            p = e * (1.0 / jnp.maximum(l, 1e-30))
            p_cols.append(p.astype(bf16))
            p_sum = p_sum + p
        o_cmps.append(jnp.dot(vct_ref[0, g], jnp.concatenate(p_cols, axis=1), preferred_element_type=f32))

        imp = jnp.dot(ovt_ref[...], p_sum, preferred_element_type=f32)
        score = jnp.where(forced, POS_BIG, jnp.where(admitted, imp, NEG))
        rank = jnp.zeros((nblk, TQ), i32)
        for kk in range(nblk):
            ck = score[kk:kk + 1, :]
            rank = rank + jnp.where(ck > score, 1, jnp.where(ck == score, jnp.where(blk > kk, 1, 0), 0))
        selb_ref[g] = jnp.where(rank < n_sel, 0.0, NEG)

    per_tile = TQ // SLC_BLOCK

    def mask_bias(j, c):
        rows = [jnp.broadcast_to(selb_ref[c, pl.ds(per_tile * j + t, 1), :], (SLC_BLOCK, TQ))
                for t in range(per_tile)]
        return jnp.concatenate(rows, axis=0)

    _masked_flash(i, qs_ref,
                  lambda j, c: ks_ref[0, c, pl.ds(pl.multiple_of(j * TQ, TQ), TQ), :],
                  lambda j, c: vst_ref[0, c, j],
                  mask_bias, bias_ref, m_ref, acc_ref, s_ref, mt_ref, al_ref, width, False)
    acc = acc_ref[...]
    o_slc = acc[0:HEAD_DIM] / acc[HEAD_DIM:HEAD_DIM + 1]

    outs = []
    for g in range(groups):
        sig_t = jax.nn.sigmoid(gt_ref[0, g, 0])
        sig_n = jax.nn.sigmoid(gn_ref[0, g])
        for r in range(rep):
            h = g * rep + r
            mixed = (sig_t[r:r + 1] * o_cmps[g][:, r * TQ:(r + 1) * TQ]
                     + sig_t[rep + r:rep + r + 1] * o_slc[:, h * TQ:(h + 1) * TQ])
            o_win = owin_ref[0, :, h * HEAD_DIM:(h + 1) * HEAD_DIM]
            outs.append(mixed.T + sig_n[:, 2 * rep + r:2 * rep + r + 1] * o_win)
    o_ref[0] = jnp.concatenate(outs, axis=1).astype(o_ref.dtype)


def _nsa_mix(qt, kc, vc, bias_c, ks, vs, near, far, gates, o_win):
    bn, g, nt, _, width = qt.shape
    s = nt * TQ
    nc = kc.shape[2]
    rep = width // TQ
    hq = g * rep * HEAD_DIM
    nblk = s // SLC_BLOCK
    n_sel = min(SLC_TOPN, nblk)
    ci = np.arange(nc)[:, None] * CMP_STRIDE
    sj = np.arange(nblk)[None, :] * SLC_BLOCK
    overlap = np.clip(np.minimum(ci + CMP_BLOCK, sj + SLC_BLOCK) - np.maximum(ci, sj), 0, None) / CMP_BLOCK
    overlap[nc - 1] = 0.0
    gates_t = gates.reshape(bn, g, nt, TQ, 3 * rep).transpose(0, 1, 2, 4, 3)
    kern = functools.partial(_nsa_kernel, rep=rep, n_sel=n_sel)
    return pl.pallas_call(
        kern,
        grid=(bn, nt),
        in_specs=[pl.BlockSpec((1, g, 1, HEAD_DIM, width), lambda b, i: (b, 0, i, 0, 0)),
                  pl.BlockSpec((1, g, nc, HEAD_DIM), lambda b, i: (b, 0, 0, 0)),
                  pl.BlockSpec((1, g, HEAD_DIM, nc), lambda b, i: (b, 0, 0, 0)),
                  pl.BlockSpec((g * rep, 1, nc, TQ), lambda b, i: (0, i, 0, 0)),
                  pl.BlockSpec((nblk, nc), lambda b, i: (0, 0)),
                  pl.BlockSpec((1, g, s, HEAD_DIM), lambda b, i: (b, 0, 0, 0)),
                  pl.BlockSpec((1, g, nt, V_ROWS, TQ), lambda b, i: (b, 0, 0, 0, 0)),
                  pl.BlockSpec((3, TQ, g * width), lambda b, i: (0, 0, 0)),
                  pl.BlockSpec((1, g, 1, 3 * rep, TQ), lambda b, i: (b, 0, i, 0, 0)),
                  pl.BlockSpec((1, g, TQ, 3 * rep), lambda b, i: (b, 0, i, 0)),
                  pl.BlockSpec((1, TQ, hq), lambda b, i: (b, i, 0))],
        out_specs=pl.BlockSpec((1, TQ, hq), lambda b, i: (b, i, 0)),
        out_shape=jax.ShapeDtypeStruct((bn, s, hq), bf16),
        scratch_shapes=[pltpu.VMEM((HEAD_DIM, g * width), bf16), pltpu.VMEM((g, nblk, TQ), f32),
                        pltpu.VMEM((1, g * width), f32), pltpu.VMEM((V_ROWS, g * width), f32),
                        pltpu.VMEM((2, TQ, g * width), f32), pltpu.VMEM((2, 1, g * width), f32),
                        pltpu.VMEM((2, 1, g * width), f32)],
        compiler_params=_cparams(("parallel", "arbitrary")),
        name="nsa_mix",
    )(qt, kc, jnp.swapaxes(vc, 2, 3), bias_c, jnp.asarray(overlap.T, f32), ks, _value_tiles(vs),
      _flash_bias_tiles(near, far), gates_t, gates, o_win)


def _dsa_kernel(qt_ref, k_ref, vt_ref, qit_ref, ki_ref, wit_ref, bias_ref, tri_ref, o_ref,
                keys_ref, qs_ref, m_ref, acc_ref, s_ref, mt_ref, al_ref, *, topk, chunk):
    i = pl.program_id(1)
    t0 = i * TQ
    width = qt_ref.shape[3]
    nh = width // TQ
    nih = qit_ref.shape[3] // TQ
    krow = lax.broadcasted_iota(i32, (TQ, TQ), 0)
    qcol = lax.broadcasted_iota(i32, (TQ, TQ), 1)
    n_pairs = jnp.maximum(i // 2 + 1, -(-topk // (2 * TQ)))

    qit = qit_ref[0, 0]
    wit = wit_ref[0, 0]

    def score_tile(j):
        kt = ki_ref[0, pl.ds(pl.multiple_of(j * TQ, TQ), TQ), :]
        rel = jnp.maximum(jnp.dot(kt, qit, preferred_element_type=f32), 0.0)
        sc = jnp.zeros((TQ, TQ), f32)
        for h in range(nih):
            sc = sc + wit[h:h + 1, :] * rel[:, h * TQ:(h + 1) * TQ]
        sc = jnp.where(j * TQ + krow <= t0 + qcol, sc, NEG)
        sc = jnp.where(sc == 0.0, 0.0, sc)
        bits = pltpu.bitcast(sc, i32)
        keys_ref[j] = jnp.where(bits < 0, bits ^ 0x7FFFFFFF, bits)

    def score_body(jj, carry):
        score_tile(2 * jj)
        score_tile(2 * jj + 1)
        return carry

    lax.fori_loop(0, n_pairs, score_body, 0)

    def count(pred_fn):
        def body(jj, cnt):
            return (cnt + jnp.where(pred_fn(keys_ref[2 * jj]), 1.0, 0.0)
                    + jnp.where(pred_fn(keys_ref[2 * jj + 1]), 1.0, 0.0))
        cnt = lax.fori_loop(0, n_pairs, body, jnp.zeros((TQ, TQ), f32))
        return jnp.sum(cnt, axis=0, keepdims=True)

    def bit_body(t, prefix):
        cand = prefix ^ jnp.left_shift(jnp.int32(1), 31 - t)
        return jnp.where(count(lambda kj: kj >= cand) >= topk, cand, prefix)

    thr = lax.fori_loop(0, 32, bit_body, jnp.full((1, TQ), INT_MIN, i32))
    n_ge = count(lambda kj: kj >= thr)

    @pl.when(jnp.max(n_ge) > topk)
    def _():
        need = topk - count(lambda kj: kj > thr)

        def fix_body(j, running):
            kj = keys_ref[j]
            tie = jnp.where(kj == thr, 1.0, 0.0)
            before = jnp.dot(tri_ref[...], tie.astype(bf16), preferred_element_type=f32) + running
            keys_ref[j] = jnp.where(before >= need, jnp.where(kj == thr, kj - 1, kj), kj)
            return running + jnp.sum(tie, axis=0, keepdims=True)

        lax.fori_loop(0, 2 * n_pairs, fix_body, jnp.zeros((1, TQ), f32))

    qs_ref[...] = (qt_ref[0, 0].astype(f32) * (HEAD_DIM ** -0.5)).astype(bf16)
    _masked_flash(i, qs_ref,
                  lambda j, c: k_ref[0, pl.ds(pl.multiple_of(j * TQ, TQ), TQ), :],
                  lambda j, c: vt_ref[0, j],
                  lambda j, c: jnp.where(keys_ref[j] >= thr, 0.0, NEG),
                  bias_ref, m_ref, acc_ref, s_ref, mt_ref, al_ref, chunk, True)
    acc = acc_ref[...]
    ot = acc[0:HEAD_DIM] / acc[HEAD_DIM:HEAD_DIM + 1]
    o_ref[0] = jnp.concatenate([ot[:, h * TQ:(h + 1) * TQ].T for h in range(nh)], axis=1).astype(o_ref.dtype)


def _dsa_attention(q, k, v, qi, ki, wi, near, far):
    bn, s, hq = q.shape
    nh = hq // HEAD_DIM
    nih = qi.shape[2] // IDX_DIM
    nt = s // TQ
    topk = min(C_TOPK_MAX, s // 4)
    width = nh * TQ
    qt = q.reshape(bn, nt, TQ, nh, HEAD_DIM).transpose(0, 1, 4, 3, 2).reshape(bn, nt, HEAD_DIM, width)
    qit = qi.reshape(bn, nt, TQ, nih, IDX_DIM).transpose(0, 1, 4, 3, 2).reshape(bn, nt, IDX_DIM, nih * TQ)
    wit = wi.reshape(bn, nt, TQ, nih).transpose(0, 1, 3, 2)
    tri = np.tril(np.ones((TQ, TQ), np.float32), -1)
    kern = functools.partial(_dsa_kernel, topk=topk, chunk=512)
    nt_scr = max(nt, -(-topk // TQ))
    return pl.pallas_call(
        kern,
        grid=(bn, nt),
        in_specs=[pl.BlockSpec((1, 1, HEAD_DIM, width), lambda b, i: (b, i, 0, 0)),
                  pl.BlockSpec((1, s, HEAD_DIM), lambda b, i: (b, 0, 0)),
                  pl.BlockSpec((1, nt, V_ROWS, TQ), lambda b, i: (b, 0, 0, 0)),
                  pl.BlockSpec((1, 1, IDX_DIM, nih * TQ), lambda b, i: (b, i, 0, 0)),
                  pl.BlockSpec((1, s, IDX_DIM), lambda b, i: (b, 0, 0)),
                  pl.BlockSpec((1, 1, nih, TQ), lambda b, i: (b, i, 0, 0)),
                  pl.BlockSpec((3, TQ, width), lambda b, i: (0, 0, 0)),
                  pl.BlockSpec((TQ, TQ), lambda b, i: (0, 0))],
        out_specs=pl.BlockSpec((1, TQ, hq), lambda b, i: (b, i, 0)),
        out_shape=jax.ShapeDtypeStruct((bn, s, hq), bf16),
        scratch_shapes=[pltpu.VMEM((nt_scr, TQ, TQ), i32), pltpu.VMEM((HEAD_DIM, width), bf16),
                        pltpu.VMEM((1, width), f32), pltpu.VMEM((V_ROWS, width), f32),
                        pltpu.VMEM((2, TQ, width), f32), pltpu.VMEM((2, 1, width), f32),
                        pltpu.VMEM((2, 1, width), f32)],
        compiler_params=_cparams(("parallel", "arbitrary")),
        name="dsa_attention",
    )(qt, k, _value_tiles(v), qit, ki, wit, _flash_bias_tiles(near, far), jnp.asarray(tri, bf16))


def _heads_first(a, g):
    bn, s, _ = a.shape
    return a.reshape(bn, s, g, HEAD_DIM).transpose(0, 2, 1, 3)


def _even_layer(x, table_t, w_in, sinks, cmpk_pos, cmpk_w1, cmpk_w2, cmpv_pos, cmpv_w1, cmpv_w2,
                w_out, ln1_g, ln1_b, ffn_gate, ffn_up, ffn_down, ln2_g, ln2_b):
    bn, s, d = x.shape
    xf = x.reshape(bn * s, d)
    a_q, a_kv, b_q, b_kv = A_HEADS * HEAD_DIM, A_KV * HEAD_DIM, B_HEADS * HEAD_DIM, B_KV * HEAD_DIM
    n_main = a_q + 2 * a_kv + b_q + 6 * b_kv
    rb = B_HEADS // B_KV
    main = _matmul(xf, w_in[:, :n_main].astype(bf16), bf16, tn=512).reshape(bn, s, n_main)
    w_tail = jnp.pad(w_in[:, n_main:], ((0, 0), (0, 128 - 3 * B_HEADS))).astype(bf16)
    gates = _matmul(xf, w_tail, f32)[:, :3 * B_HEADS]
    gates = gates.reshape(bn, s, 3, B_KV, rb).transpose(0, 3, 1, 2, 4).reshape(bn, B_KV, s, 3 * rb)
    cols = np.cumsum([0, a_q, a_kv, a_kv, b_q] + [b_kv] * 6)
    qa, ka, va, qb, kc_in, vc_in, ksl, vsl, kwn, vwn = [main[:, :, cols[t]:cols[t + 1]] for t in range(10)]

    near_idx = _bucket_of(np.arange(TQ)[:, None] + TQ - np.arange(2 * TQ)[None, :])
    near = _bias_expand(table_t, near_idx, TQ)
    pad_b = -(-(B_WINDOW - 1) // TQ) * TQ
    win_idx = _bucket_of(np.arange(TQ)[:, None] + pad_b - np.arange(pad_b + TQ)[None, :])
    bias_win = _bias_expand(table_t[A_HEADS:A_HEADS + B_HEADS], win_idx, TQ)
    nc = s // CMP_STRIDE
    nt = s // TQ
    cmp_idx = _bucket_of((np.arange(nt)[:, None, None] * TQ + np.arange(TQ)[None, None, :])
                         - (np.arange(nc)[None, :, None] * CMP_STRIDE + CMP_BLOCK - 1)).reshape(nt * nc, TQ)
    bias_c = _bias_expand(table_t[A_HEADS:A_HEADS + B_HEADS], cmp_idx, min(nt * nc, 512))
    bias_c = bias_c.reshape(B_HEADS, nt, nc, TQ)
    far = table_t[:, NUM_BUCKETS - 1]

    qt_a = _queries_on_lanes(qa, A_KV)
    qt_b = _queries_on_lanes(qb, B_KV)
    out_a = _banded_gqa(qt_a, _heads_first(ka, A_KV), _heads_first(va, A_KV), near[:A_HEADS], sinks, A_WINDOW, bf16)
    o_win = _banded_gqa(qt_b, _heads_first(kwn, B_KV), _heads_first(vwn, B_KV), bias_win, None, B_WINDOW, f32)
    kc = _nsa_compress(_heads_first(kc_in, B_KV), cmpk_pos, cmpk_w1, cmpk_w2)
    vc = _nsa_compress(_heads_first(vc_in, B_KV), cmpv_pos, cmpv_w1, cmpv_w2)
    out_b = _nsa_mix(qt_b, kc, vc, bias_c, _heads_first(ksl, B_KV), _heads_first(vsl, B_KV),
                     near[A_HEADS:A_HEADS + B_HEADS], far[A_HEADS:A_HEADS + B_HEADS], gates, o_win)
    mix = jnp.concatenate([out_a, out_b], axis=-1).reshape(bn * s, a_q + b_q)
    x1 = _proj_res_ln(mix, w_out.astype(bf16), xf, ln1_g, ln1_b)
    x2 = _ffn_res_ln(x1, ffn_gate.astype(bf16), ffn_up.astype(bf16), ffn_down.astype(bf16), ln2_g, ln2_b)
    return x2.reshape(bn, s, d), near, far


def _odd_layer(x, near, far, w_in, w_out, ln1_g, ln1_b, router, exp_gate, exp_up, exp_down, ln2_g, ln2_b):
    bn, s, d = x.shape
    xf = x.reshape(bn * s, d)
    c_q = C_HEADS * HEAD_DIM
    n_main = c_q + 2 * HEAD_DIM + IDX_HEADS * IDX_DIM
    main = _matmul(xf, w_in[:, :n_main].astype(bf16), bf16).reshape(bn, s, n_main)
    w_tail = jnp.pad(w_in[:, n_main:], ((0, 0), (0, 128 - IDX_DIM - IDX_HEADS))).astype(bf16)
    tail = _matmul(xf, w_tail, f32).reshape(bn, s, 128)
    q = main[:, :, :c_q]
    k = main[:, :, c_q:c_q + HEAD_DIM]
    v = main[:, :, c_q + HEAD_DIM:c_q + 2 * HEAD_DIM]
    qi = main[:, :, c_q + 2 * HEAD_DIM:]
    ki = tail[:, :, :IDX_DIM].astype(bf16)
    wi = tail[:, :, IDX_DIM:IDX_DIM + IDX_HEADS]
    mix = _dsa_attention(q, k, v, qi, ki, wi, near[:C_HEADS], far[:C_HEADS]).reshape(bn * s, c_q)
    x1 = _proj_res_ln(mix, w_out.astype(bf16), xf, ln1_g, ln1_b)
    x2 = _moe_res_ln(x1, router, exp_gate.astype(bf16), exp_up.astype(bf16), exp_down.astype(bf16), ln2_g, ln2_b)
    return x2.reshape(bn, s, d)


def kernel(x, rel_bias, l0_w_in, l0_sinks, l0_cmpk_pos, l0_cmpk_w1, l0_cmpk_w2, l0_cmpv_pos, l0_cmpv_w1,
           l0_cmpv_w2, l0_w_out, l0_ln1_g, l0_ln1_b, l0_ffn_gate, l0_ffn_up, l0_ffn_down, l0_ln2_g, l0_ln2_b,
           l1_w_in, l1_w_out, l1_ln1_g, l1_ln1_b, l1_router, l1_exp_gate, l1_exp_up, l1_exp_down, l1_ln2_g,
           l1_ln2_b):
    table_t = rel_bias.T.astype(f32)
    x, near, far = _even_layer(x, table_t, l0_w_in, l0_sinks, l0_cmpk_pos, l0_cmpk_w1, l0_cmpk_w2, l0_cmpv_pos,
                               l0_cmpv_w1, l0_cmpv_w2, l0_w_out, l0_ln1_g, l0_ln1_b, l0_ffn_gate, l0_ffn_up,
                               l0_ffn_down, l0_ln2_g, l0_ln2_b)
    return _odd_layer(x, near, far, l1_w_in, l1_w_out, l1_ln1_g, l1_ln1_b, l1_router, l1_exp_gate, l1_exp_up,
                      l1_exp_down, l1_ln2_g, l1_ln2_b)
```

```python
import functools
import math

import numpy as np
import jax
import jax.numpy as jnp
from jax import lax
from jax.experimental import pallas as pl
from jax.experimental.pallas import tpu as pltpu

f32 = jnp.float32
bf16 = jnp.bfloat16
i32 = jnp.int32

D_MODEL = 1024
HEAD_DIM = 64
NUM_BUCKETS = 32
MAX_DISTANCE = 128
A_HEADS, A_KV, A_WINDOW = 8, 2, 128
B_HEADS, B_KV, B_WINDOW = 8, 2, 512
CMP_BLOCK, CMP_STRIDE = 32, 16
SLC_BLOCK, SLC_TOPN = 64, 16
C_HEADS, IDX_HEADS, IDX_DIM, C_TOPK_MAX = 16, 8, 32, 256
D_FF, N_EXPERTS = 2816, 8
DEPTH = 2
ALPHA = (2.0 * DEPTH) ** 0.25
LN_EPS = 1e-5
NEG = -1e30
POS_BIG = 1e30
INT_MIN = -(2 ** 31)

TQ = 128
FF_CHUNK = 256
VMEM_LIMIT = 56 * 1024 * 1024


def _cparams(sem):
    return pltpu.CompilerParams(dimension_semantics=sem, vmem_limit_bytes=VMEM_LIMIT)


def _bucket_table(max_dist):
    n = np.arange(max_dist + 1)
    max_exact = NUM_BUCKETS // 2
    nf = np.maximum(n, 1).astype(np.float64)
    large = max_exact + (np.log(nf / max_exact) / math.log(MAX_DISTANCE / max_exact)
                         * (NUM_BUCKETS - max_exact)).astype(np.int64)
    large = np.minimum(large, NUM_BUCKETS - 1)
    return np.where(n < max_exact, n, large).astype(np.int32)


def _bucket_of(dist):
    dist = np.maximum(dist, 0)
    return _bucket_table(int(dist.max()))[dist]


def _proj_in_kernel(x_ref, wm_ref, wt_ref, om_ref, ot_ref):
    xb = x_ref[...].astype(bf16)
    om_ref[...] = jnp.dot(xb, wm_ref[...], preferred_element_type=f32).astype(om_ref.dtype)
    ot_ref[...] = jnp.dot(xb, wt_ref[...], preferred_element_type=f32)


def _proj_in(x, w_main, w_tail, tm=512):
    m, k = x.shape
    nm, ntl = w_main.shape[1], w_tail.shape[1]
    return pl.pallas_call(
        _proj_in_kernel,
        grid=(m // tm,),
        in_specs=[pl.BlockSpec((tm, k), lambda i: (i, 0)),
                  pl.BlockSpec((k, nm), lambda i: (0, 0)),
                  pl.BlockSpec((k, ntl), lambda i: (0, 0))],
        out_specs=[pl.BlockSpec((tm, nm), lambda i: (i, 0)), pl.BlockSpec((tm, ntl), lambda i: (i, 0))],
        out_shape=[jax.ShapeDtypeStruct((m, nm), bf16), jax.ShapeDtypeStruct((m, ntl), f32)],
        compiler_params=_cparams(("parallel",)),
        name="proj_in",
    )(x, w_main, w_tail)


def _bias_expand_kernel(tab_ref, idx_ref, o_ref):
    h = pl.program_id(0)
    idx = idx_ref[...]
    acc = jnp.zeros(idx.shape, f32)
    for b in range(NUM_BUCKETS):
        acc = jnp.where(idx == b, tab_ref[h, b], acc)
    o_ref[0] = acc


def _bias_expand(table_t, idx, tr):
    nh = table_t.shape[0]
    r, c = idx.shape
    return pl.pallas_call(
        _bias_expand_kernel,
        grid=(nh, r // tr),
        in_specs=[pl.BlockSpec(memory_space=pltpu.SMEM),
                  pl.BlockSpec((tr, c), lambda h, i: (i, 0))],
        out_specs=pl.BlockSpec((1, tr, c), lambda h, i: (h, i, 0)),
        out_shape=jax.ShapeDtypeStruct((nh, r, c), f32),
        compiler_params=_cparams(("parallel", "arbitrary")),
        name="bias_expand",
    )(table_t, jnp.asarray(idx, dtype=i32))


def _layer_norm(z, g, b):
    mu = jnp.mean(z, axis=-1, keepdims=True)
    zc = z - mu
    var = jnp.mean(zc * zc, axis=-1, keepdims=True)
    return zc * lax.rsqrt(var + LN_EPS) * g + b


def _proj_ln_kernel(mix_ref, w_ref, x_ref, g_ref, b_ref, o_ref):
    y = jnp.dot(mix_ref[...], w_ref[...], preferred_element_type=f32)
    o_ref[...] = _layer_norm(ALPHA * x_ref[...] + y, g_ref[...], b_ref[...])


def _proj_res_ln(mix, w, x, g, b, tm=512):
    m, k = mix.shape
    d = w.shape[1]
    return pl.pallas_call(
        _proj_ln_kernel,
        grid=(m // tm,),
        in_specs=[pl.BlockSpec((tm, k), lambda i: (i, 0)),
                  pl.BlockSpec((k, d), lambda i: (0, 0)),
                  pl.BlockSpec((tm, d), lambda i: (i, 0)),
                  pl.BlockSpec((1, d), lambda i: (0, 0)),
                  pl.BlockSpec((1, d), lambda i: (0, 0))],
        out_specs=pl.BlockSpec((tm, d), lambda i: (i, 0)),
        out_shape=jax.ShapeDtypeStruct((m, d), f32),
        compiler_params=_cparams(("parallel",)),
        name="proj_res_ln",
    )(mix, w, x, g.reshape(1, d), b.reshape(1, d))


def _swiglu_accumulate(xb, wg_ref, wu_ref, wd_ref, acc_ref, widx):
    d_ff = wg_ref.shape[-1]
    for c in range(d_ff // FF_CHUNK):
        cols = slice(c * FF_CHUNK, (c + 1) * FF_CHUNK)
        gate = jnp.dot(xb, wg_ref[widx + (slice(None), cols)], preferred_element_type=f32)
        up = jnp.dot(xb, wu_ref[widx + (slice(None), cols)], preferred_element_type=f32)
        h = jax.nn.silu(gate) * up
        acc_ref[...] += jnp.dot(h.astype(bf16), wd_ref[widx + (cols, slice(None))], preferred_element_type=f32)


def _ffn_kernel(x_ref, wg_ref, wu_ref, wd_ref, g_ref, b_ref, o_ref, acc_ref):
    x = x_ref[...]
    acc_ref[...] = jnp.zeros_like(acc_ref)
    _swiglu_accumulate(x.astype(bf16), wg_ref, wu_ref, wd_ref, acc_ref, ())
    o_ref[...] = _layer_norm(ALPHA * x + acc_ref[...], g_ref[...], b_ref[...])


def _ffn_res_ln(x, wg, wu, wd, g, b, tm=512):
    m, d = x.shape
    d_ff = wg.shape[1]
    return pl.pallas_call(
        _ffn_kernel,
        grid=(m // tm,),
        in_specs=[pl.BlockSpec((tm, d), lambda i: (i, 0)),
                  pl.BlockSpec((d, d_ff), lambda i: (0, 0)),
                  pl.BlockSpec((d, d_ff), lambda i: (0, 0)),
                  pl.BlockSpec((d_ff, d), lambda i: (0, 0)),
                  pl.BlockSpec((1, d), lambda i: (0, 0)),
                  pl.BlockSpec((1, d), lambda i: (0, 0))],
        out_specs=pl.BlockSpec((tm, d), lambda i: (i, 0)),
        out_shape=jax.ShapeDtypeStruct((m, d), f32),
        scratch_shapes=[pltpu.VMEM((tm, d), f32)],
        compiler_params=_cparams(("parallel",)),
        name="ffn_res_ln",
    )(x, wg, wu, wd, g.reshape(1, d), b.reshape(1, d))


MOE_CAP = 320


def _router_gates(x, r_ref, lane):
    logits = jnp.dot(x, r_ref[...], preferred_element_type=f32)
    logits = jnp.where(lane < N_EXPERTS, logits, -jnp.inf)
    m1 = jnp.max(logits, axis=-1, keepdims=True)
    i1 = jnp.min(jnp.where(logits == m1, lane, 128), axis=-1, keepdims=True)
    rest = jnp.where(lane == i1, -jnp.inf, logits)
    m2 = jnp.max(rest, axis=-1, keepdims=True)
    i2 = jnp.min(jnp.where(rest == m2, lane, 128), axis=-1, keepdims=True)
    e2 = jnp.exp(m2 - m1)
    g1 = 1.0 / (1.0 + e2)
    return jnp.where(lane == i1, g1, 0.0) + jnp.where(lane == i2, e2 * g1, 0.0)


def _moe_kernel(x_ref, r_ref, tri_ref, wg_ref, wu_ref, wd_ref, g_ref, b_ref, o_ref,
                comb_ref, pos_ref, combt_ref, post_ref, yc_ref):
    e = pl.program_id(1)
    tm = x_ref.shape[0]
    lane = lax.broadcasted_iota(i32, (tm, 128), 1)

    @pl.when(e == 0)
    def _():
        comb = _router_gates(x_ref[...], r_ref, lane)
        comb_ref[...] = comb
        chosen = jnp.where(comb > 0.0, 1.0, 0.0)
        pos = jnp.dot(tri_ref[...], chosen.astype(bf16), preferred_element_type=f32)
        pos = jnp.where(comb > 0.0, pos, -1.0)
        pos_ref[...] = pos
        for t in range(tm // 128):
            rows = slice(t * 128, (t + 1) * 128)
            combt_ref[:, rows] = comb[rows].T[0:N_EXPERTS]
            post_ref[:, rows] = pos[rows].T[0:N_EXPERTS]
        o_ref[...] = jnp.zeros_like(o_ref)

    gate_col = jnp.sum(jnp.where(lane == e, comb_ref[...], 0.0), axis=-1, keepdims=True)
    pos_col = jnp.sum(jnp.where(lane == e, pos_ref[...], 0.0), axis=-1, keepdims=True).astype(i32)
    pos_row = post_ref[pl.ds(e, 1), :].astype(i32)
    n_tok = jnp.sum(jnp.where(combt_ref[pl.ds(e, 1), :] > 0.0, 1.0, 0.0)).astype(i32)

    def pass_body(p, carry):
        base = p * MOE_CAP
        gather = jnp.where(pos_row == base + lax.broadcasted_iota(i32, (MOE_CAP, tm), 0), 1.0, 0.0).astype(bf16)
        xc = jnp.dot(gather, x_ref[...].astype(bf16), preferred_element_type=f32).astype(bf16)
        yc_ref[...] = jnp.zeros_like(yc_ref)
        _swiglu_accumulate(xc, wg_ref, wu_ref, wd_ref, yc_ref, (0,))
        scatter = jnp.where(pos_col == base + lax.broadcasted_iota(i32, (tm, MOE_CAP), 1), 1.0, 0.0).astype(bf16)
        o_ref[...] += gate_col * jnp.dot(scatter, yc_ref[...].astype(bf16), preferred_element_type=f32)
        return carry

    lax.fori_loop(0, (n_tok + MOE_CAP - 1) // MOE_CAP, pass_body, 0)

    @pl.when(e == N_EXPERTS - 1)
    def _():
        o_ref[...] = _layer_norm(ALPHA * x_ref[...] + o_ref[...], g_ref[...], b_ref[...])


def _moe_res_ln(x, router, wg, wu, wd, g, b, tm=1024):
    m, d = x.shape
    ne, _, d_ff = wg.shape
    tm = min(tm, m)
    router_p = jnp.pad(router, ((0, 0), (0, 128 - ne)))
    tri = jnp.asarray(np.tril(np.ones((tm, tm), np.float32), -1), bf16)
    single = pl.Buffered(1)
    return pl.pallas_call(
        _moe_kernel,
        grid=(m // tm, ne),
        in_specs=[pl.BlockSpec((tm, d), lambda i, e: (i, 0), pipeline_mode=single),
                  pl.BlockSpec((d, 128), lambda i, e: (0, 0), pipeline_mode=single),
                  pl.BlockSpec((tm, tm), lambda i, e: (0, 0), pipeline_mode=single),
                  pl.BlockSpec((1, d, d_ff), lambda i, e: (e, 0, 0)),
                  pl.BlockSpec((1, d, d_ff), lambda i, e: (e, 0, 0)),
                  pl.BlockSpec((1, d_ff, d), lambda i, e: (e, 0, 0)),
                  pl.BlockSpec((1, d), lambda i, e: (0, 0)),
                  pl.BlockSpec((1, d), lambda i, e: (0, 0))],
        out_specs=pl.BlockSpec((tm, d), lambda i, e: (i, 0)),
        out_shape=jax.ShapeDtypeStruct((m, d), f32),
        scratch_shapes=[pltpu.VMEM((tm, 128), f32), pltpu.VMEM((tm, 128), f32),
                        pltpu.VMEM((N_EXPERTS, tm), f32), pltpu.VMEM((N_EXPERTS, tm), f32),
                        pltpu.VMEM((MOE_CAP, d), f32)],
        compiler_params=_cparams(("parallel", "arbitrary")),
        name="moe_res_ln",
    )(x, router_p, tri, wg, wu, wd, g.reshape(1, d), b.reshape(1, d))


def _compress_kernel(blk_ref, pos_ref, w1_ref, w2_ref, o_ref):
    xb = (blk_ref[0].astype(f32) + pos_ref[...]).astype(bf16)
    h = jax.nn.gelu(jnp.dot(xb, w1_ref[...], preferred_element_type=f32))
    o_ref[0] = jnp.dot(h.astype(bf16), w2_ref[...], preferred_element_type=f32).astype(o_ref.dtype)


def _nsa_compress(kv, pos, w1, w2):
    bn, g, s, dh = kv.shape
    nc = s // CMP_STRIDE
    chunks = kv.reshape(bn, g, nc, CMP_STRIDE * dh)
    nxt = jnp.pad(chunks[:, :, 1:], ((0, 0), (0, 0), (0, 1), (0, 0)))
    blocks = jnp.concatenate([chunks, nxt], axis=-1).reshape(bn * g, nc, CMP_BLOCK * dh)
    cin = CMP_BLOCK * dh
    hid = w1.shape[1]
    out = pl.pallas_call(
        _compress_kernel,
        grid=(bn * g,),
        in_specs=[pl.BlockSpec((1, nc, cin), lambda i: (i, 0, 0)),
                  pl.BlockSpec((1, cin), lambda i: (0, 0)),
                  pl.BlockSpec((cin, hid), lambda i: (0, 0)),
                  pl.BlockSpec((hid, dh), lambda i: (0, 0))],
        out_specs=pl.BlockSpec((1, nc, dh), lambda i: (i, 0, 0)),
        out_shape=jax.ShapeDtypeStruct((bn * g, nc, dh), bf16),
        compiler_params=_cparams(("parallel",)),
        name="nsa_compress",
    )(blocks, pos.reshape(1, cin), w1.astype(bf16), w2.astype(bf16))
    return out.reshape(bn, g, nc, dh)


V_ROWS = HEAD_DIM + 16


def _masked_flash(i, qs_ref, key_tile, value_tile, mask_bias, bias_ref, m_ref, acc_ref, s_ref, mt_ref, al_ref, chunk,
                  shared_keys):
    width = qs_ref.shape[1]
    n_chunks = width // chunk
    m_ref[...] = jnp.full(m_ref.shape, NEG, f32)
    acc_ref[...] = jnp.zeros(acc_ref.shape, f32)

    def scores(j):
        slot = j % 2
        kind = jnp.clip(j - (i - 2), 0, 2)
        for c in range(n_chunks):
            if c == 0 or not shared_keys:
                kt = key_tile(j, c)
                maskb = jnp.concatenate([mask_bias(j, c)] * (chunk // TQ), axis=1)
            cols = slice(c * chunk, (c + 1) * chunk)
            s = jnp.dot(kt, qs_ref[:, cols], preferred_element_type=f32) + bias_ref[kind, :, cols] + maskb
            s_ref[slot, :, cols] = s
            m_old = m_ref[:, cols]
            m_new = jnp.maximum(m_old, jnp.max(s, axis=0, keepdims=True))
            m_ref[:, cols] = m_new
            mt_ref[slot, :, cols] = m_new
            al_ref[slot, :, cols] = jnp.exp(m_old - m_new)

    def values(j):
        slot = j % 2
        for c in range(n_chunks):
            if c == 0 or not shared_keys:
                vt = value_tile(j, c)
            cols = slice(c * chunk, (c + 1) * chunk)
            p = jnp.exp(s_ref[slot, :, cols] - mt_ref[slot, :, cols])
            acc_ref[:, cols] = (al_ref[slot, :, cols] * acc_ref[:, cols]
                                + jnp.dot(vt, p.astype(bf16), preferred_element_type=f32))

    scores(0)

    def pipe_body(j, carry):
        values(j - 1)
        scores(j)
        return carry

    lax.fori_loop(1, i + 1, pipe_body, 0)
    values(i)


def _flash_bias_tiles(near, far):
    nh = near.shape[0]
    width = nh * TQ
    neart = near.reshape(nh, TQ, 2, TQ).transpose(2, 3, 0, 1).reshape(2, TQ, width)
    causal = np.where(np.arange(TQ)[:, None] <= np.arange(TQ)[None, :], 0.0, NEG).astype(np.float32)
    far_tile = jnp.broadcast_to(jnp.repeat(far, TQ).reshape(1, width), (TQ, width))
    return jnp.stack([far_tile, neart[0], neart[1] + jnp.asarray(np.tile(causal, (1, nh)))], axis=0)


def _value_tiles(v):
    lead = v.shape[:-2]
    nt = v.shape[-2] // TQ
    vt = jnp.swapaxes(v.reshape(lead + (nt, TQ, HEAD_DIM)), -1, -2)
    return jnp.concatenate([vt, jnp.ones(lead + (nt, 1, TQ), vt.dtype),
                            jnp.zeros(lead + (nt, V_ROWS - HEAD_DIM - 1, TQ), vt.dtype)], axis=-2)


def _queries_on_lanes(q, g):
    bn, s, hq = q.shape
    rep = hq // (g * HEAD_DIM)
    nt = s // TQ
    return q.reshape(bn, nt, TQ, g, rep, HEAD_DIM).transpose(0, 3, 1, 5, 4, 2).reshape(bn, g, nt, HEAD_DIM, rep * TQ)


def _banded_kernel(qt_ref, k_ref, vt_ref, bias_ref, neg_ref, sink_ref, o_ref, *, rep, kt_tiles, use_sinks):
    i = pl.program_id(1)
    kw = kt_tiles * TQ
    outs = []
    for g in range(qt_ref.shape[1]):
        qs = jnp.concatenate([(qt_ref[0, g, 0].astype(f32) * (HEAD_DIM ** -0.5)).astype(bf16), neg_ref[...]], axis=0)
        k = k_ref[0, g, pl.ds(pl.multiple_of(i * TQ, TQ), kw), :]
        s = jnp.dot(k, qs, preferred_element_type=f32) + bias_ref[g]
        m = jnp.max(s, axis=0, keepdims=True)
        if use_sinks:
            m = jnp.maximum(m, sink_ref[g])
        p = jnp.exp(s - m).astype(bf16)
        vt = jnp.concatenate([vt_ref[0, g, i + t] for t in range(kt_tiles)], axis=1)
        acc = jnp.dot(vt, p, preferred_element_type=f32)
        l = acc[HEAD_DIM:HEAD_DIM + 1]
        if use_sinks:
            l = l + jnp.exp(sink_ref[g] - m)
        ot = acc[0:HEAD_DIM] / l
        outs += [ot[:, r * TQ:(r + 1) * TQ].T for r in range(rep)]
    o_ref[0] = jnp.concatenate(outs, axis=1).astype(o_ref.dtype)


def _banded_gqa(qt, k, v, bias, sinks, window, out_dtype):
    bn, g, nt, _, width = qt.shape
    rep = width // TQ
    s = nt * TQ
    pad = -(-(window - 1) // TQ) * TQ
    kw = pad + TQ
    kd = 2 * HEAD_DIM
    flag = jnp.broadcast_to((jnp.arange(pad + s) < pad).astype(k.dtype)[:, None], (bn, g, pad + s, 1))
    kp = jnp.concatenate([jnp.pad(k, ((0, 0), (0, 0), (pad, 0), (0, 0))), flag,
                          jnp.zeros((bn, g, pad + s, kd - HEAD_DIM - 1), k.dtype)], axis=-1)
    vt = _value_tiles(jnp.pad(v, ((0, 0), (0, 0), (pad, 0), (0, 0))))
    neg = np.zeros((kd - HEAD_DIM, width), np.float32)
    neg[0] = NEG
    dist = np.arange(TQ)[:, None] + pad - np.arange(kw)[None, :]
    band = jnp.asarray((dist >= 0) & (dist < window))
    bias_t = jnp.where(band, bias, NEG).reshape(g, rep, TQ, kw).transpose(0, 3, 1, 2).reshape(g, kw, width)
    use_sinks = sinks is not None
    sink_rows = (jnp.repeat(sinks.astype(f32), TQ) if use_sinks else jnp.zeros((g * width,), f32)).reshape(g, 1, width)
    kern = functools.partial(_banded_kernel, rep=rep, kt_tiles=kw // TQ, use_sinks=use_sinks)
    rw = rep * HEAD_DIM
    return pl.pallas_call(
        kern,
        grid=(bn, nt),
        in_specs=[pl.BlockSpec((1, g, 1, HEAD_DIM, width), lambda b, i: (b, 0, i, 0, 0)),
                  pl.BlockSpec((1, g, pad + s, kd), lambda b, i: (b, 0, 0, 0)),
                  pl.BlockSpec((1, g, (pad + s) // TQ, V_ROWS, TQ), lambda b, i: (b, 0, 0, 0, 0)),
                  pl.BlockSpec((g, kw, width), lambda b, i: (0, 0, 0)),
                  pl.BlockSpec((kd - HEAD_DIM, width), lambda b, i: (0, 0)),
                  pl.BlockSpec((g, 1, width), lambda b, i: (0, 0, 0))],
        out_specs=pl.BlockSpec((1, TQ, g * rw), lambda b, i: (b, i, 0)),
        out_shape=jax.ShapeDtypeStruct((bn, s, g * rw), out_dtype),
        compiler_params=_cparams(("parallel", "arbitrary")),
        name="banded_gqa_w%d" % window,
    )(qt, kp, vt, bias_t, jnp.asarray(neg, bf16), sink_rows)


def _nsa_kernel(qt_ref, kc_ref, vct_ref, biasc_ref, ovt_ref, ks_ref, vst_ref, bias_ref, gt_ref, gn_ref, owin_ref,
                o_ref, qs_ref, selb_ref, m_ref, acc_ref, s_ref, mt_ref, al_ref, *, rep, n_sel):
    i = pl.program_id(1)
    t0 = i * TQ
    groups = qt_ref.shape[1]
    nc = kc_ref.shape[2]
    nblk = ovt_ref.shape[0]
    width = rep * TQ
    cmp_end = lax.broadcasted_iota(i32, (nc, TQ), 0) * CMP_STRIDE + (CMP_BLOCK - 1)
    valid_c = t0 + lax.broadcasted_iota(i32, (nc, TQ), 1) >= cmp_end
    blk = lax.broadcasted_iota(i32, (nblk, TQ), 0)
    tq = t0 + lax.broadcasted_iota(i32, (nblk, TQ), 1)
    cur = tq // SLC_BLOCK
    forced = (blk == 0) | (blk == cur) | (blk == cur - 1)
    admitted = blk * SLC_BLOCK <= tq

    o_cmps = []
    for g in range(groups):
        qs = (qt_ref[0, g, 0].astype(f32) * (HEAD_DIM ** -0.5)).astype(bf16)
        qs_ref[:, g * width:(g + 1) * width] = qs

        s_c = jnp.dot(kc_ref[0, g], qs, preferred_element_type=f32)
        p_cols = []
        p_sum = jnp.zeros((nc, TQ), f32)
        for r in range(rep):
            sr = jnp.where(valid_c, s_c[:, r * TQ:(r + 1) * TQ] + biasc_ref[g * rep + r, 0], NEG)
            m = jnp.max(sr, axis=0, keepdims=True)
            e = jnp.where(valid_c, jnp.exp(sr - m), 0.0)
            l = jnp.sum(e, axis=0, keepdims=True)
            p = e * (1.0 / jnp.maximum(l, 1e-30))
            p_cols.append(p.astype(bf16))
            p_sum = p_sum + p
        o_cmps.append(jnp.dot(vct_ref[0, g], jnp.concatenate(p_cols, axis=1), preferred_element_type=f32))

        imp = jnp.dot(ovt_ref[...], p_sum, preferred_element_type=f32)
        score = jnp.where(forced, POS_BIG, jnp.where(admitted, imp, NEG))
        rank = jnp.zeros((nblk, TQ), i32)
        for kk in range(nblk):
            ck = score[kk:kk + 1, :]
            rank = rank + jnp.where(ck > score, 1, jnp.where(ck == score, jnp.where(blk > kk, 1, 0), 0))
        selb_ref[g] = jnp.where(rank < n_sel, 0.0, NEG)

    per_tile = TQ // SLC_BLOCK

    def mask_bias(j, c):
        rows = [jnp.broadcast_to(selb_ref[c, pl.ds(per_tile * j + t, 1), :], (SLC_BLOCK, TQ))
                for t in range(per_tile)]
        return jnp.concatenate(rows, axis=0)

    _masked_flash(i, qs_ref,
                  lambda j, c: ks_ref[0, c, pl.ds(pl.multiple_of(j * TQ, TQ), TQ), :],
                  lambda j, c: vst_ref[0, c, j],
                  mask_bias, bias_ref, m_ref, acc_ref, s_ref, mt_ref, al_ref, width, False)
    acc = acc_ref[...]
    o_slc = acc[0:HEAD_DIM] / acc[HEAD_DIM:HEAD_DIM + 1]

    outs = []
    for g in range(groups):
        sig_t = jax.nn.sigmoid(gt_ref[0, g, 0])
        sig_n = jax.nn.sigmoid(gn_ref[0, g])
        for r in range(rep):
            h = g * rep + r
            mixed = (sig_t[r:r + 1] * o_cmps[g][:, r * TQ:(r + 1) * TQ]
                     + sig_t[rep + r:rep + r + 1] * o_slc[:, h * TQ:(h + 1) * TQ])
            o_win = owin_ref[0, :, h * HEAD_DIM:(h + 1) * HEAD_DIM]
            outs.append(mixed.T + sig_n[:, 2 * rep + r:2 * rep + r + 1] * o_win)
    o_ref[0] = jnp.concatenate(outs, axis=1).astype(o_ref.dtype)


def _nsa_mix(qt, kc, vc, bias_c, ks, vs, near, far, gates, o_win):
    bn, g, nt, _, width = qt.shape
    s = nt * TQ
    nc = kc.shape[2]
    rep = width // TQ
    hq = g * rep * HEAD_DIM
    nblk = s // SLC_BLOCK
    n_sel = min(SLC_TOPN, nblk)
    ci = np.arange(nc)[:, None] * CMP_STRIDE
    sj = np.arange(nblk)[None, :] * SLC_BLOCK
    overlap = np.clip(np.minimum(ci + CMP_BLOCK, sj + SLC_BLOCK) - np.maximum(ci, sj), 0, None) / CMP_BLOCK
    overlap[nc - 1] = 0.0
    gates_t = gates.reshape(bn, g, nt, TQ, 3 * rep).transpose(0, 1, 2, 4, 3)
    kern = functools.partial(_nsa_kernel, rep=rep, n_sel=n_sel)
    return pl.pallas_call(
        kern,
        grid=(bn, nt),
        in_specs=[pl.BlockSpec((1, g, 1, HEAD_DIM, width), lambda b, i: (b, 0, i, 0, 0)),
                  pl.BlockSpec((1, g, nc, HEAD_DIM), lambda b, i: (b, 0, 0, 0)),
                  pl.BlockSpec((1, g, HEAD_DIM, nc), lambda b, i: (b, 0, 0, 0)),
                  pl.BlockSpec((g * rep, 1, nc, TQ), lambda b, i: (0, i, 0, 0)),
                  pl.BlockSpec((nblk, nc), lambda b, i: (0, 0)),
                  pl.BlockSpec((1, g, s, HEAD_DIM), lambda b, i: (b, 0, 0, 0)),
                  pl.BlockSpec((1, g, nt, V_ROWS, TQ), lambda b, i: (b, 0, 0, 0, 0)),
                  pl.BlockSpec((3, TQ, g * width), lambda b, i: (0, 0, 0)),
                  pl.BlockSpec((1, g, 1, 3 * rep, TQ), lambda b, i: (b, 0, i, 0, 0)),
                  pl.BlockSpec((1, g, TQ, 3 * rep), lambda b, i: (b, 0, i, 0)),
                  pl.BlockSpec((1, TQ, hq), lambda b, i: (b, i, 0))],
        out_specs=pl.BlockSpec((1, TQ, hq), lambda b, i: (b, i, 0)),
        out_shape=jax.ShapeDtypeStruct((bn, s, hq), bf16),
        scratch_shapes=[pltpu.VMEM((HEAD_DIM, g * width), bf16), pltpu.VMEM((g, nblk, TQ), f32),
                        pltpu.VMEM((1, g * width), f32), pltpu.VMEM((V_ROWS, g * width), f32),
                        pltpu.VMEM((2, TQ, g * width), f32), pltpu.VMEM((2, 1, g * width), f32),
                        pltpu.VMEM((2, 1, g * width), f32)],
        compiler_params=_cparams(("parallel", "arbitrary")),
        name="nsa_mix",
    )(qt, kc, jnp.swapaxes(vc, 2, 3), bias_c, jnp.asarray(overlap.T, f32), ks, _value_tiles(vs),
      _flash_bias_tiles(near, far), gates_t, gates, o_win)


def _dsa_kernel(qt_ref, k_ref, vt_ref, qit_ref, ki_ref, wit_ref, bias_ref, tri_ref, o_ref,
                keys_ref, qs_ref, m_ref, acc_ref, s_ref, mt_ref, al_ref, *, topk, chunk):
    i = pl.program_id(1)
    t0 = i * TQ
    width = qt_ref.shape[3]
    nh = width // TQ
    nih = qit_ref.shape[3] // TQ
    krow = lax.broadcasted_iota(i32, (TQ, TQ), 0)
    qcol = lax.broadcasted_iota(i32, (TQ, TQ), 1)
    n_pairs = jnp.maximum(i // 2 + 1, -(-topk // (2 * TQ)))

    qit = qit_ref[0, 0]
    wit = wit_ref[0, 0]

    def score_tile(j):
        kt = ki_ref[0, pl.ds(pl.multiple_of(j * TQ, TQ), TQ), :]
        rel = jnp.maximum(jnp.dot(kt, qit, preferred_element_type=f32), 0.0)
        sc = jnp.zeros((TQ, TQ), f32)
        for h in range(nih):
            sc = sc + wit[h:h + 1, :] * rel[:, h * TQ:(h + 1) * TQ]
        sc = jnp.where(j * TQ + krow <= t0 + qcol, sc, NEG)
        sc = jnp.where(sc == 0.0, 0.0, sc)
        bits = pltpu.bitcast(sc, i32)
        keys_ref[j] = jnp.where(bits < 0, bits ^ 0x7FFFFFFF, bits)

    def score_body(jj, carry):
        score_tile(2 * jj)
        score_tile(2 * jj + 1)
        return carry

    lax.fori_loop(0, n_pairs, score_body, 0)

    def count(pred_fn):
        def body(jj, cnt):
            return (cnt + jnp.where(pred_fn(keys_ref[2 * jj]), 1.0, 0.0)
                    + jnp.where(pred_fn(keys_ref[2 * jj + 1]), 1.0, 0.0))
        cnt = lax.fori_loop(0, n_pairs, body, jnp.zeros((TQ, TQ), f32))
        return jnp.sum(cnt, axis=0, keepdims=True)

    def bit_body(t, prefix):
        cand = prefix ^ jnp.left_shift(jnp.int32(1), 31 - t)
        return jnp.where(count(lambda kj: kj >= cand) >= topk, cand, prefix)

    thr = lax.fori_loop(0, 32, bit_body, jnp.full((1, TQ), INT_MIN, i32))
    n_ge = count(lambda kj: kj >= thr)

    @pl.when(jnp.max(n_ge) > topk)
    def _():
        need = topk - count(lambda kj: kj > thr)

        def fix_body(j, running):
            kj = keys_ref[j]
            tie = jnp.where(kj == thr, 1.0, 0.0)
            before = jnp.dot(tri_ref[...], tie.astype(bf16), preferred_element_type=f32) + running
            keys_ref[j] = jnp.where(before >= need, jnp.where(kj == thr, kj - 1, kj), kj)
            return running + jnp.sum(tie, axis=0, keepdims=True)

        lax.fori_loop(0, 2 * n_pairs, fix_body, jnp.zeros((1, TQ), f32))

    qs_ref[...] = (qt_ref[0, 0].astype(f32) * (HEAD_DIM ** -0.5)).astype(bf16)
    _masked_flash(i, qs_ref,
                  lambda j, c: k_ref[0, pl.ds(pl.multiple_of(j * TQ, TQ), TQ), :],
                  lambda j, c: vt_ref[0, j],
                  lambda j, c: jnp.where(keys_ref[j] >= thr, 0.0, NEG),
                  bias_ref, m_ref, acc_ref, s_ref, mt_ref, al_ref, chunk, True)
    acc = acc_ref[...]
    ot = acc[0:HEAD_DIM] / acc[HEAD_DIM:HEAD_DIM + 1]
    o_ref[0] = jnp.concatenate([ot[:, h * TQ:(h + 1) * TQ].T for h in range(nh)], axis=1).astype(o_ref.dtype)


def _dsa_attention(q, k, v, qi, ki, wi, near, far):
    bn, s, hq = q.shape
    nh = hq // HEAD_DIM
    nih = qi.shape[2] // IDX_DIM
    nt = s // TQ
    topk = min(C_TOPK_MAX, s // 4)
    width = nh * TQ
    qt = q.reshape(bn, nt, TQ, nh, HEAD_DIM).transpose(0, 1, 4, 3, 2).reshape(bn, nt, HEAD_DIM, width)
    qit = qi.reshape(bn, nt, TQ, nih, IDX_DIM).transpose(0, 1, 4, 3, 2).reshape(bn, nt, IDX_DIM, nih * TQ)
    wit = wi.reshape(bn, nt, TQ, nih).transpose(0, 1, 3, 2)
    tri = np.tril(np.ones((TQ, TQ), np.float32), -1)
    kern = functools.partial(_dsa_kernel, topk=topk, chunk=512)
    nt_scr = max(nt, -(-topk // TQ))
    return pl.pallas_call(
        kern,
        grid=(bn, nt),
        in_specs=[pl.BlockSpec((1, 1, HEAD_DIM, width), lambda b, i: (b, i, 0, 0)),
                  pl.BlockSpec((1, s, HEAD_DIM), lambda b, i: (b, 0, 0)),
                  pl.BlockSpec((1, nt, V_ROWS, TQ), lambda b, i: (b, 0, 0, 0)),
                  pl.BlockSpec((1, 1, IDX_DIM, nih * TQ), lambda b, i: (b, i, 0, 0)),
                  pl.BlockSpec((1, s, IDX_DIM), lambda b, i: (b, 0, 0)),
                  pl.BlockSpec((1, 1, nih, TQ), lambda b, i: (b, i, 0, 0)),
                  pl.BlockSpec((3, TQ, width), lambda b, i: (0, 0, 0)),
                  pl.BlockSpec((TQ, TQ), lambda b, i: (0, 0))],
        out_specs=pl.BlockSpec((1, TQ, hq), lambda b, i: (b, i, 0)),
        out_shape=jax.ShapeDtypeStruct((bn, s, hq), bf16),
        scratch_shapes=[pltpu.VMEM((nt_scr, TQ, TQ), i32), pltpu.VMEM((HEAD_DIM, width), bf16),
                        pltpu.VMEM((1, width), f32), pltpu.VMEM((V_ROWS, width), f32),
                        pltpu.VMEM((2, TQ, width), f32), pltpu.VMEM((2, 1, width), f32),
                        pltpu.VMEM((2, 1, width), f32)],
        compiler_params=_cparams(("parallel", "arbitrary")),
        name="dsa_attention",
    )(qt, k, _value_tiles(v), qit, ki, wit, _flash_bias_tiles(near, far), jnp.asarray(tri, bf16))


def _heads_first(a, g):
    bn, s, _ = a.shape
    return a.reshape(bn, s, g, HEAD_DIM).transpose(0, 2, 1, 3)


def _even_layer(x, table_t, w_in, sinks, cmpk_pos, cmpk_w1, cmpk_w2, cmpv_pos, cmpv_w1, cmpv_w2,
                w_out, ln1_g, ln1_b, ffn_gate, ffn_up, ffn_down, ln2_g, ln2_b):
    bn, s, d = x.shape
    xf = x.reshape(bn * s, d)
    a_q, a_kv, b_q, b_kv = A_HEADS * HEAD_DIM, A_KV * HEAD_DIM, B_HEADS * HEAD_DIM, B_KV * HEAD_DIM
    n_main = a_q + 2 * a_kv + b_q + 6 * b_kv
    rb = B_HEADS // B_KV
    w_tail = jnp.pad(w_in[:, n_main:], ((0, 0), (0, 128 - 3 * B_HEADS))).astype(bf16)
    main, tail = _proj_in(xf, w_in[:, :n_main].astype(bf16), w_tail)
    main = main.reshape(bn, s, n_main)
    gates = tail[:, :3 * B_HEADS]
    gates = gates.reshape(bn, s, 3, B_KV, rb).transpose(0, 3, 1, 2, 4).reshape(bn, B_KV, s, 3 * rb)
    cols = np.cumsum([0, a_q, a_kv, a_kv, b_q] + [b_kv] * 6)
    qa, ka, va, qb, kc_in, vc_in, ksl, vsl, kwn, vwn = [main[:, :, cols[t]:cols[t + 1]] for t in range(10)]

    near_idx = _bucket_of(np.arange(TQ)[:, None] + TQ - np.arange(2 * TQ)[None, :])
    near = _bias_expand(table_t, near_idx, TQ)
    pad_b = -(-(B_WINDOW - 1) // TQ) * TQ
    win_idx = _bucket_of(np.arange(TQ)[:, None] + pad_b - np.arange(pad_b + TQ)[None, :])
    bias_win = _bias_expand(table_t[A_HEADS:A_HEADS + B_HEADS], win_idx, TQ)
    nc = s // CMP_STRIDE
    nt = s // TQ
    cmp_idx = _bucket_of((np.arange(nt)[:, None, None] * TQ + np.arange(TQ)[None, None, :])
                         - (np.arange(nc)[None, :, None] * CMP_STRIDE + CMP_BLOCK - 1)).reshape(nt * nc, TQ)
    bias_c = _bias_expand(table_t[A_HEADS:A_HEADS + B_HEADS], cmp_idx, min(nt * nc, 512))
    bias_c = bias_c.reshape(B_HEADS, nt, nc, TQ)
    far = table_t[:, NUM_BUCKETS - 1]

    qt_a = _queries_on_lanes(qa, A_KV)
    qt_b = _queries_on_lanes(qb, B_KV)
    out_a = _banded_gqa(qt_a, _heads_first(ka, A_KV), _heads_first(va, A_KV), near[:A_HEADS], sinks, A_WINDOW, bf16)
    o_win = _banded_gqa(qt_b, _heads_first(kwn, B_KV), _heads_first(vwn, B_KV), bias_win, None, B_WINDOW, f32)
    kc = _nsa_compress(_heads_first(kc_in, B_KV), cmpk_pos, cmpk_w1, cmpk_w2)
    vc = _nsa_compress(_heads_first(vc_in, B_KV), cmpv_pos, cmpv_w1, cmpv_w2)
    out_b = _nsa_mix(qt_b, kc, vc, bias_c, _heads_first(ksl, B_KV), _heads_first(vsl, B_KV),
                     near[A_HEADS:A_HEADS + B_HEADS], far[A_HEADS:A_HEADS + B_HEADS], gates, o_win)
    mix = jnp.concatenate([out_a, out_b], axis=-1).reshape(bn * s, a_q + b_q)
    x1 = _proj_res_ln(mix, w_out.astype(bf16), xf, ln1_g, ln1_b)
    x2 = _ffn_res_ln(x1, ffn_gate.astype(bf16), ffn_up.astype(bf16), ffn_down.astype(bf16), ln2_g, ln2_b)
    return x2.reshape(bn, s, d), near, far


def _odd_layer(x, near, far, w_in, w_out, ln1_g, ln1_b, router, exp_gate, exp_up, exp_down, ln2_g, ln2_b):
    bn, s, d = x.shape
    xf = x.reshape(bn * s, d)
    c_q = C_HEADS * HEAD_DIM
    n_main = c_q + 2 * HEAD_DIM + IDX_HEADS * IDX_DIM
    w_tail = jnp.pad(w_in[:, n_main:], ((0, 0), (0, 128 - IDX_DIM - IDX_HEADS))).astype(bf16)
    main, tail = _proj_in(xf, w_in[:, :n_main].astype(bf16), w_tail)
    main = main.reshape(bn, s, n_main)
    tail = tail.reshape(bn, s, 128)
    q = main[:, :, :c_q]
    k = main[:, :, c_q:c_q + HEAD_DIM]
    v = main[:, :, c_q + HEAD_DIM:c_q + 2 * HEAD_DIM]
    qi = main[:, :, c_q + 2 * HEAD_DIM:]
    ki = tail[:, :, :IDX_DIM].astype(bf16)
    wi = tail[:, :, IDX_DIM:IDX_DIM + IDX_HEADS]
    mix = _dsa_attention(q, k, v, qi, ki, wi, near[:C_HEADS], far[:C_HEADS]).reshape(bn * s, c_q)
    x1 = _proj_res_ln(mix, w_out.astype(bf16), xf, ln1_g, ln1_b)
    x2 = _moe_res_ln(x1, router, exp_gate.astype(bf16), exp_up.astype(bf16), exp_down.astype(bf16), ln2_g, ln2_b)
    return x2.reshape(bn, s, d)


def kernel(x, rel_bias, l0_w_in, l0_sinks, l0_cmpk_pos, l0_cmpk_w1, l0_cmpk_w2, l0_cmpv_pos, l0_cmpv_w1,
           l0_cmpv_w2, l0_w_out, l0_ln1_g, l0_ln1_b, l0_ffn_gate, l0_ffn_up, l0_ffn_down, l0_ln2_g, l0_ln2_b,
           l1_w_in, l1_w_out, l1_ln1_g, l1_ln1_b, l1_router, l1_exp_gate, l1_exp_up, l1_exp_down, l1_ln2_g,
           l1_ln2_b):
    table_t = rel_bias.T.astype(f32)
    x, near, far = _even_layer(x, table_t, l0_w_in, l0_sinks, l0_cmpk_pos, l0_cmpk_w1, l0_cmpk_w2, l0_cmpv_pos,
                               l0_cmpv_w1, l0_cmpv_w2, l0_w_out, l0_ln1_g, l0_ln1_b, l0_ffn_gate, l0_ffn_up,
                               l0_ffn_down, l0_ln2_g, l0_ln2_b)
    return _odd_layer(x, near, far, l1_w_in, l1_w_out, l1_ln1_g, l1_ln1_b, l1_router, l1_exp_gate, l1_exp_up,
                      l1_exp_down, l1_ln2_g, l1_ln2_b)
```

```python
import functools
import math

import numpy as np
import jax
import jax.numpy as jnp
from jax import lax
from jax.experimental import pallas as pl
from jax.experimental.pallas import tpu as pltpu

f32 = jnp.float32
bf16 = jnp.bfloat16
i32 = jnp.int32

D_MODEL = 1024
HEAD_DIM = 64
NUM_BUCKETS = 32
MAX_DISTANCE = 128
A_HEADS, A_KV, A_WINDOW = 8, 2, 128
B_HEADS, B_KV, B_WINDOW = 8, 2, 512
CMP_BLOCK, CMP_STRIDE = 32, 16
SLC_BLOCK, SLC_TOPN = 64, 16
C_HEADS, IDX_HEADS, IDX_DIM, C_TOPK_MAX = 16, 8, 32, 256
D_FF, N_EXPERTS = 2816, 8
DEPTH = 2
ALPHA = (2.0 * DEPTH) ** 0.25
LN_EPS = 1e-5
NEG = -1e30
POS_BIG = 1e30
INT_MIN = -(2 ** 31)

TQ = 128
FF_CHUNK = 256
VMEM_LIMIT = 56 * 1024 * 1024


def _cparams(sem):
    return pltpu.CompilerParams(dimension_semantics=sem, vmem_limit_bytes=VMEM_LIMIT)


def _bucket_table(max_dist):
    n = np.arange(max_dist + 1)
    max_exact = NUM_BUCKETS // 2
    nf = np.maximum(n, 1).astype(np.float64)
    large = max_exact + (np.log(nf / max_exact) / math.log(MAX_DISTANCE / max_exact)
                         * (NUM_BUCKETS - max_exact)).astype(np.int64)
    large = np.minimum(large, NUM_BUCKETS - 1)
    return np.where(n < max_exact, n, large).astype(np.int32)


def _bucket_of(dist):
    dist = np.maximum(dist, 0)
    return _bucket_table(int(dist.max()))[dist]


def _proj_in_kernel(x_ref, wm_ref, wt_ref, om_ref, ot_ref):
    xb = x_ref[...].astype(bf16)
    om_ref[...] = jnp.dot(xb, wm_ref[...], preferred_element_type=f32).astype(om_ref.dtype)
    ot_ref[...] = jnp.dot(xb, wt_ref[...], preferred_element_type=f32)


def _proj_in(x, w_main, w_tail, tm=512):
    m, k = x.shape
    nm, ntl = w_main.shape[1], w_tail.shape[1]
    return pl.pallas_call(
        _proj_in_kernel,
        grid=(m // tm,),
        in_specs=[pl.BlockSpec((tm, k), lambda i: (i, 0)),
                  pl.BlockSpec((k, nm), lambda i: (0, 0)),
                  pl.BlockSpec((k, ntl), lambda i: (0, 0))],
        out_specs=[pl.BlockSpec((tm, nm), lambda i: (i, 0)), pl.BlockSpec((tm, ntl), lambda i: (i, 0))],
        out_shape=[jax.ShapeDtypeStruct((m, nm), bf16), jax.ShapeDtypeStruct((m, ntl), f32)],
        compiler_params=_cparams(("parallel",)),
        name="proj_in",
    )(x, w_main, w_tail)


def _bias_expand_kernel(tab_ref, idx_ref, o_ref):
    h = pl.program_id(0)
    idx = idx_ref[...]
    acc = jnp.zeros(idx.shape, f32)
    for b in range(NUM_BUCKETS):
        acc = jnp.where(idx == b, tab_ref[h, b], acc)
    o_ref[0] = acc


def _bias_expand(table_t, idx, tr):
    nh = table_t.shape[0]
    r, c = idx.shape
    return pl.pallas_call(
        _bias_expand_kernel,
        grid=(nh, r // tr),
        in_specs=[pl.BlockSpec(memory_space=pltpu.SMEM),
                  pl.BlockSpec((tr, c), lambda h, i: (i, 0))],
        out_specs=pl.BlockSpec((1, tr, c), lambda h, i: (h, i, 0)),
        out_shape=jax.ShapeDtypeStruct((nh, r, c), f32),
        compiler_params=_cparams(("parallel", "arbitrary")),
        name="bias_expand",
    )(table_t, jnp.asarray(idx, dtype=i32))


def _layer_norm(z, g, b):
    mu = jnp.mean(z, axis=-1, keepdims=True)
    zc = z - mu
    var = jnp.mean(zc * zc, axis=-1, keepdims=True)
    return zc * lax.rsqrt(var + LN_EPS) * g + b


def _proj_ln_kernel(mix_ref, w_ref, x_ref, g_ref, b_ref, o_ref):
    y = jnp.dot(mix_ref[...], w_ref[...], preferred_element_type=f32)
    o_ref[...] = _layer_norm(ALPHA * x_ref[...] + y, g_ref[...], b_ref[...])


def _proj_res_ln(mix, w, x, g, b, tm=512):
    m, k = mix.shape
    d = w.shape[1]
    return pl.pallas_call(
        _proj_ln_kernel,
        grid=(m // tm,),
        in_specs=[pl.BlockSpec((tm, k), lambda i: (i, 0)),
                  pl.BlockSpec((k, d), lambda i: (0, 0)),
                  pl.BlockSpec((tm, d), lambda i: (i, 0)),
                  pl.BlockSpec((1, d), lambda i: (0, 0)),
                  pl.BlockSpec((1, d), lambda i: (0, 0))],
        out_specs=pl.BlockSpec((tm, d), lambda i: (i, 0)),
        out_shape=jax.ShapeDtypeStruct((m, d), f32),
        compiler_params=_cparams(("parallel",)),
        name="proj_res_ln",
    )(mix, w, x, g.reshape(1, d), b.reshape(1, d))


def _swiglu_accumulate(xb, wg_ref, wu_ref, wd_ref, acc_ref, widx):
    d_ff = wg_ref.shape[-1]
    for c in range(d_ff // FF_CHUNK):
        cols = slice(c * FF_CHUNK, (c + 1) * FF_CHUNK)
        gate = jnp.dot(xb, wg_ref[widx + (slice(None), cols)], preferred_element_type=f32)
        up = jnp.dot(xb, wu_ref[widx + (slice(None), cols)], preferred_element_type=f32)
        h = jax.nn.silu(gate) * up
        acc_ref[...] += jnp.dot(h.astype(bf16), wd_ref[widx + (cols, slice(None))], preferred_element_type=f32)


def _ffn_kernel(x_ref, wg_ref, wu_ref, wd_ref, g_ref, b_ref, o_ref, acc_ref):
    x = x_ref[...]
    acc_ref[...] = jnp.zeros_like(acc_ref)
    _swiglu_accumulate(x.astype(bf16), wg_ref, wu_ref, wd_ref, acc_ref, ())
    o_ref[...] = _layer_norm(ALPHA * x + acc_ref[...], g_ref[...], b_ref[...])


def _ffn_res_ln(x, wg, wu, wd, g, b, tm=512):
    m, d = x.shape
    d_ff = wg.shape[1]
    return pl.pallas_call(
        _ffn_kernel,
        grid=(m // tm,),
        in_specs=[pl.BlockSpec((tm, d), lambda i: (i, 0)),
                  pl.BlockSpec((d, d_ff), lambda i: (0, 0)),
                  pl.BlockSpec((d, d_ff), lambda i: (0, 0)),
                  pl.BlockSpec((d_ff, d), lambda i: (0, 0)),
                  pl.BlockSpec((1, d), lambda i: (0, 0)),
                  pl.BlockSpec((1, d), lambda i: (0, 0))],
        out_specs=pl.BlockSpec((tm, d), lambda i: (i, 0)),
        out_shape=jax.ShapeDtypeStruct((m, d), f32),
        scratch_shapes=[pltpu.VMEM((tm, d), f32)],
        compiler_params=_cparams(("parallel",)),
        name="ffn_res_ln",
    )(x, wg, wu, wd, g.reshape(1, d), b.reshape(1, d))


MOE_CAP = 320


def _router_gates(x, r_ref, lane):
    logits = jnp.dot(x, r_ref[...], preferred_element_type=f32)
    logits = jnp.where(lane < N_EXPERTS, logits, -jnp.inf)
    m1 = jnp.max(logits, axis=-1, keepdims=True)
    i1 = jnp.min(jnp.where(logits == m1, lane, 128), axis=-1, keepdims=True)
    rest = jnp.where(lane == i1, -jnp.inf, logits)
    m2 = jnp.max(rest, axis=-1, keepdims=True)
    i2 = jnp.min(jnp.where(rest == m2, lane, 128), axis=-1, keepdims=True)
    e2 = jnp.exp(m2 - m1)
    g1 = 1.0 / (1.0 + e2)
    return jnp.where(lane == i1, g1, 0.0) + jnp.where(lane == i2, e2 * g1, 0.0)


def _moe_kernel(x_ref, r_ref, tri_ref, wg_ref, wu_ref, wd_ref, g_ref, b_ref, o_ref,
                comb_ref, pos_ref, combt_ref, post_ref, yc_ref):
    e = pl.program_id(1)
    tm = x_ref.shape[0]
    lane = lax.broadcasted_iota(i32, (tm, 128), 1)

    @pl.when(e == 0)
    def _():
        comb = _router_gates(x_ref[...], r_ref, lane)
        comb_ref[...] = comb
        chosen = jnp.where(comb > 0.0, 1.0, 0.0)
        pos = jnp.dot(tri_ref[...], chosen.astype(bf16), preferred_element_type=f32)
        pos = jnp.where(comb > 0.0, pos, -1.0)
        pos_ref[...] = pos
        for t in range(tm // 128):
            rows = slice(t * 128, (t + 1) * 128)
            combt_ref[:, rows] = comb[rows].T[0:N_EXPERTS]
            post_ref[:, rows] = pos[rows].T[0:N_EXPERTS]
        o_ref[...] = jnp.zeros_like(o_ref)

    gate_col = jnp.sum(jnp.where(lane == e, comb_ref[...], 0.0), axis=-1, keepdims=True)
    pos_col = jnp.sum(jnp.where(lane == e, pos_ref[...], 0.0), axis=-1, keepdims=True).astype(i32)
    pos_row = post_ref[pl.ds(e, 1), :].astype(i32)
    n_tok = jnp.sum(jnp.where(combt_ref[pl.ds(e, 1), :] > 0.0, 1.0, 0.0)).astype(i32)

    def pass_body(p, carry):
        base = p * MOE_CAP
        gather = jnp.where(pos_row == base + lax.broadcasted_iota(i32, (MOE_CAP, tm), 0), 1.0, 0.0).astype(bf16)
        xc = jnp.dot(gather, x_ref[...].astype(bf16), preferred_element_type=f32).astype(bf16)
        yc_ref[...] = jnp.zeros_like(yc_ref)
        _swiglu_accumulate(xc, wg_ref, wu_ref, wd_ref, yc_ref, (0,))
        scatter = jnp.where(pos_col == base + lax.broadcasted_iota(i32, (tm, MOE_CAP), 1), 1.0, 0.0).astype(bf16)
        o_ref[...] += gate_col * jnp.dot(scatter, yc_ref[...].astype(bf16), preferred_element_type=f32)
        return carry

    lax.fori_loop(0, (n_tok + MOE_CAP - 1) // MOE_CAP, pass_body, 0)

    @pl.when(e == N_EXPERTS - 1)
    def _():
        o_ref[...] = _layer_norm(ALPHA * x_ref[...] + o_ref[...], g_ref[...], b_ref[...])


def _moe_res_ln(x, router, wg, wu, wd, g, b, tm=1024):
    m, d = x.shape
    ne, _, d_ff = wg.shape
    tm = min(tm, m)
    router_p = jnp.pad(router, ((0, 0), (0, 128 - ne)))
    tri = jnp.asarray(np.tril(np.ones((tm, tm), np.float32), -1), bf16)
    single = pl.Buffered(1)
    return pl.pallas_call(
        _moe_kernel,
        grid=(m // tm, ne),
        in_specs=[pl.BlockSpec((tm, d), lambda i, e: (i, 0), pipeline_mode=single),
                  pl.BlockSpec((d, 128), lambda i, e: (0, 0), pipeline_mode=single),
                  pl.BlockSpec((tm, tm), lambda i, e: (0, 0), pipeline_mode=single),
                  pl.BlockSpec((1, d, d_ff), lambda i, e: (e, 0, 0)),
                  pl.BlockSpec((1, d, d_ff), lambda i, e: (e, 0, 0)),
                  pl.BlockSpec((1, d_ff, d), lambda i, e: (e, 0, 0)),
                  pl.BlockSpec((1, d), lambda i, e: (0, 0)),
                  pl.BlockSpec((1, d), lambda i, e: (0, 0))],
        out_specs=pl.BlockSpec((tm, d), lambda i, e: (i, 0)),
        out_shape=jax.ShapeDtypeStruct((m, d), f32),
        scratch_shapes=[pltpu.VMEM((tm, 128), f32), pltpu.VMEM((tm, 128), f32),
                        pltpu.VMEM((N_EXPERTS, tm), f32), pltpu.VMEM((N_EXPERTS, tm), f32),
                        pltpu.VMEM((MOE_CAP, d), f32)],
        compiler_params=_cparams(("parallel", "arbitrary")),
        name="moe_res_ln",
    )(x, router_p, tri, wg, wu, wd, g.reshape(1, d), b.reshape(1, d))


def _compress_kernel(blk_ref, pos_ref, w1_ref, w2_ref, o_ref):
    xb = (blk_ref[0].astype(f32) + pos_ref[...]).astype(bf16)
    h = jax.nn.gelu(jnp.dot(xb, w1_ref[...], preferred_element_type=f32))
    o_ref[0] = jnp.dot(h.astype(bf16), w2_ref[...], preferred_element_type=f32).astype(o_ref.dtype)


def _nsa_compress(kv, pos, w1, w2):
    bn, g, s, dh = kv.shape
    nc = s // CMP_STRIDE
    chunks = kv.reshape(bn, g, nc, CMP_STRIDE * dh)
    nxt = jnp.pad(chunks[:, :, 1:], ((0, 0), (0, 0), (0, 1), (0, 0)))
    blocks = jnp.concatenate([chunks, nxt], axis=-1).reshape(bn * g, nc, CMP_BLOCK * dh)
    cin = CMP_BLOCK * dh
    hid = w1.shape[1]
    out = pl.pallas_call(
        _compress_kernel,
        grid=(bn * g,),
        in_specs=[pl.BlockSpec((1, nc, cin), lambda i: (i, 0, 0)),
                  pl.BlockSpec((1, cin), lambda i: (0, 0)),
                  pl.BlockSpec((cin, hid), lambda i: (0, 0)),
                  pl.BlockSpec((hid, dh), lambda i: (0, 0))],
        out_specs=pl.BlockSpec((1, nc, dh), lambda i: (i, 0, 0)),
        out_shape=jax.ShapeDtypeStruct((bn * g, nc, dh), bf16),
        compiler_params=_cparams(("parallel",)),
        name="nsa_compress",
    )(blocks, pos.reshape(1, cin), w1.astype(bf16), w2.astype(bf16))
    return out.reshape(bn, g, nc, dh)


V_ROWS = HEAD_DIM + 16


LOG2E = 1.4426950408889634
K_COLS = 2 * HEAD_DIM


def _masked_flash(i, qs_ref, key_tile, value_tile, mask_bias, delta_ref, m_ref, acc_ref, s_ref, mt_ref, al_ref, chunk,
                  shared_keys):
    width = qs_ref.shape[1]
    n_chunks = width // chunk
    m_ref[...] = jnp.full(m_ref.shape, NEG, f32)
    acc_ref[...] = jnp.zeros(acc_ref.shape, f32)

    def scores_chunk(j, delta, c, shared):
        slot = j % 2
        if c == 0 or not shared_keys:
            shared["kt"] = key_tile(j, c)
            shared["maskb"] = jnp.concatenate([mask_bias(j, c)] * (chunk // TQ), axis=1)
        cols = slice(c * chunk, (c + 1) * chunk)
        s = jnp.dot(shared["kt"], qs_ref[:, cols], preferred_element_type=f32) + shared["maskb"]
        if delta is not None:
            s = s + delta_ref[delta, :, cols]
        s_ref[slot, :, cols] = s
        m_old = m_ref[:, cols]
        m_new = jnp.maximum(m_old, jnp.max(s, axis=0, keepdims=True))
        m_ref[:, cols] = m_new
        mt_ref[slot, :, cols] = m_new
        al_ref[slot, :, cols] = jnp.exp2(m_old - m_new)

    def values_chunk(j, c, shared):
        slot = j % 2
        if c == 0 or not shared_keys:
            shared["vt"] = value_tile(j, c)
        cols = slice(c * chunk, (c + 1) * chunk)
        p = jnp.exp2(s_ref[slot, :, cols] - mt_ref[slot, :, cols])
        acc_ref[:, cols] = (al_ref[slot, :, cols] * acc_ref[:, cols]
                            + jnp.dot(shared["vt"], p.astype(bf16), preferred_element_type=f32))

    def scores(j, delta):
        shared = {}
        for c in range(n_chunks):
            scores_chunk(j, delta, c, shared)

    def values(j):
        shared = {}
        for c in range(n_chunks):
            values_chunk(j, c, shared)

    def values_then_scores(jv, js, delta):
        sv, ss = {}, {}
        for c in range(n_chunks):
            values_chunk(jv, c, sv)
            scores_chunk(js, delta, c, ss)

    @pl.when(i >= 2)
    def _():
        scores(0, None)

        def pipe_body(j, carry):
            values_then_scores(j - 1, j, None)
            return carry

        lax.fori_loop(1, i - 1, pipe_body, 0)
        values_then_scores(i - 2, i - 1, 0)
        values_then_scores(i - 1, i, 1)

    @pl.when(i == 1)
    def _():
        scores(0, 0)
        values_then_scores(0, 1, 1)

    @pl.when(i == 0)
    def _():
        scores(0, 1)

    values(i)


def _flash_operands(near, far):
    nh = near.shape[0]
    width = nh * TQ
    far2 = jnp.repeat(far * LOG2E, TQ).reshape(1, width)
    hi = far2.astype(bf16)
    lo = (far2 - hi.astype(f32)).astype(bf16)
    extra = jnp.concatenate([hi, lo, jnp.zeros((K_COLS - HEAD_DIM - 2, width), bf16)], axis=0)
    neart = near.reshape(nh, TQ, 2, TQ).transpose(2, 3, 0, 1).reshape(2, TQ, width) * LOG2E
    neart = neart - (hi.astype(f32) + lo.astype(f32))
    causal = np.where(np.arange(TQ)[:, None] <= np.arange(TQ)[None, :], 0.0, NEG).astype(np.float32)
    return extra, jnp.stack([neart[0], neart[1] + jnp.asarray(np.tile(causal, (1, nh)))], axis=0)


def _keys_with_ones(k):
    lead = k.shape[:-1]
    return jnp.concatenate([k, jnp.ones(lead + (2,), k.dtype), jnp.zeros(lead + (K_COLS - HEAD_DIM - 2,), k.dtype)],
                           axis=-1)


def _value_tiles(v):
    lead = v.shape[:-2]
    nt = v.shape[-2] // TQ
    vt = jnp.swapaxes(v.reshape(lead + (nt, TQ, HEAD_DIM)), -1, -2)
    return jnp.concatenate([vt, jnp.ones(lead + (nt, 1, TQ), vt.dtype),
                            jnp.zeros(lead + (nt, V_ROWS - HEAD_DIM - 1, TQ), vt.dtype)], axis=-2)


def _queries_on_lanes(q, g):
    bn, s, hq = q.shape
    rep = hq // (g * HEAD_DIM)
    nt = s // TQ
    return q.reshape(bn, nt, TQ, g, rep, HEAD_DIM).transpose(0, 3, 1, 5, 4, 2).reshape(bn, g, nt, HEAD_DIM, rep * TQ)


def _banded_kernel(qt_ref, k_ref, vt_ref, bias_ref, neg_ref, sink_ref, o_ref, *, rep, kt_tiles, use_sinks):
    i = pl.program_id(1)
    kw = kt_tiles * TQ
    outs = []
    for g in range(qt_ref.shape[1]):
        qs = jnp.concatenate([(qt_ref[0, g, 0].astype(f32) * (HEAD_DIM ** -0.5)).astype(bf16), neg_ref[...]], axis=0)
        k = k_ref[0, g, pl.ds(pl.multiple_of(i * TQ, TQ), kw), :]
        s = jnp.dot(k, qs, preferred_element_type=f32) + bias_ref[g]
        m = jnp.max(s, axis=0, keepdims=True)
        if use_sinks:
            m = jnp.maximum(m, sink_ref[g])
        p = jnp.exp(s - m).astype(bf16)
        vt = jnp.concatenate([vt_ref[0, g, i + t] for t in range(kt_tiles)], axis=1)
        acc = jnp.dot(vt, p, preferred_element_type=f32)
        l = acc[HEAD_DIM:HEAD_DIM + 1]
        if use_sinks:
            l = l + jnp.exp(sink_ref[g] - m)
        ot = acc[0:HEAD_DIM] / l
        outs += [ot[:, r * TQ:(r + 1) * TQ].T for r in range(rep)]
    o_ref[0] = jnp.concatenate(outs, axis=1).astype(o_ref.dtype)


def _banded_gqa(qt, k, v, bias, sinks, window, out_dtype):
    bn, g, nt, _, width = qt.shape
    rep = width // TQ
    s = nt * TQ
    pad = -(-(window - 1) // TQ) * TQ
    kw = pad + TQ
    kd = 2 * HEAD_DIM
    flag = jnp.broadcast_to((jnp.arange(pad + s) < pad).astype(k.dtype)[:, None], (bn, g, pad + s, 1))
    kp = jnp.concatenate([jnp.pad(k, ((0, 0), (0, 0), (pad, 0), (0, 0))), flag,
                          jnp.zeros((bn, g, pad + s, kd - HEAD_DIM - 1), k.dtype)], axis=-1)
    vt = _value_tiles(jnp.pad(v, ((0, 0), (0, 0), (pad, 0), (0, 0))))
    neg = np.zeros((kd - HEAD_DIM, width), np.float32)
    neg[0] = NEG
    dist = np.arange(TQ)[:, None] + pad - np.arange(kw)[None, :]
    band = jnp.asarray((dist >= 0) & (dist < window))
    bias_t = jnp.where(band, bias, NEG).reshape(g, rep, TQ, kw).transpose(0, 3, 1, 2).reshape(g, kw, width)
    use_sinks = sinks is not None
    sink_rows = (jnp.repeat(sinks.astype(f32), TQ) if use_sinks else jnp.zeros((g * width,), f32)).reshape(g, 1, width)
    kern = functools.partial(_banded_kernel, rep=rep, kt_tiles=kw // TQ, use_sinks=use_sinks)
    rw = rep * HEAD_DIM
    return pl.pallas_call(
        kern,
        grid=(bn, nt),
        in_specs=[pl.BlockSpec((1, g, 1, HEAD_DIM, width), lambda b, i: (b, 0, i, 0, 0)),
                  pl.BlockSpec((1, g, pad + s, kd), lambda b, i: (b, 0, 0, 0)),
                  pl.BlockSpec((1, g, (pad + s) // TQ, V_ROWS, TQ), lambda b, i: (b, 0, 0, 0, 0)),
                  pl.BlockSpec((g, kw, width), lambda b, i: (0, 0, 0)),
                  pl.BlockSpec((kd - HEAD_DIM, width), lambda b, i: (0, 0)),
                  pl.BlockSpec((g, 1, width), lambda b, i: (0, 0, 0))],
        out_specs=pl.BlockSpec((1, TQ, g * rw), lambda b, i: (b, i, 0)),
        out_shape=jax.ShapeDtypeStruct((bn, s, g * rw), out_dtype),
        compiler_params=_cparams(("parallel", "arbitrary")),
        name="banded_gqa_w%d" % window,
    )(qt, kp, vt, bias_t, jnp.asarray(neg, bf16), sink_rows)


def _nsa_kernel(qt_ref, kc_ref, vct_ref, biasc_ref, ovt_ref, ks_ref, vst_ref, qx_ref, delta_ref, gt_ref, gn_ref,
                owin_ref, o_ref, qs_ref, selb_ref, m_ref, acc_ref, s_ref, mt_ref, al_ref, *, rep, n_sel):
    i = pl.program_id(1)
    t0 = i * TQ
    groups = qt_ref.shape[1]
    nc = kc_ref.shape[2]
    nblk = ovt_ref.shape[0]
    width = rep * TQ
    cmp_end = lax.broadcasted_iota(i32, (nc, TQ), 0) * CMP_STRIDE + (CMP_BLOCK - 1)
    valid_c = t0 + lax.broadcasted_iota(i32, (nc, TQ), 1) >= cmp_end
    blk = lax.broadcasted_iota(i32, (nblk, TQ), 0)
    tq = t0 + lax.broadcasted_iota(i32, (nblk, TQ), 1)
    cur = tq // SLC_BLOCK
    forced = (blk == 0) | (blk == cur) | (blk == cur - 1)
    admitted = blk * SLC_BLOCK <= tq

    o_cmps = []
    for g in range(groups):
        qf = qt_ref[0, g, 0].astype(f32)
        qs = (qf * (HEAD_DIM ** -0.5)).astype(bf16)
        qs_ref[0:HEAD_DIM, g * width:(g + 1) * width] = (qf * (HEAD_DIM ** -0.5 * LOG2E)).astype(bf16)

        s_c = jnp.dot(kc_ref[0, g], qs, preferred_element_type=f32)
        p_cols = []
        p_sum = jnp.zeros((nc, TQ), f32)
        for r in range(rep):
            sr = jnp.where(valid_c, s_c[:, r * TQ:(r + 1) * TQ] + biasc_ref[g * rep + r, 0], NEG)
            m = jnp.max(sr, axis=0, keepdims=True)
            e = jnp.where(valid_c, jnp.exp(sr - m), 0.0)
            l = jnp.sum(e, axis=0, keepdims=True)
            p = e * (1.0 / jnp.maximum(l, 1e-30))
            p_cols.append(p.astype(bf16))
            p_sum = p_sum + p
        o_cmps.append(jnp.dot(vct_ref[0, g], jnp.concatenate(p_cols, axis=1), preferred_element_type=f32))

        imp = jnp.dot(ovt_ref[...], p_sum, preferred_element_type=f32)
        score = jnp.where(forced, POS_BIG, jnp.where(admitted, imp, NEG))
        rank = jnp.zeros((nblk, TQ), i32)
        for kk in range(nblk):
            ck = score[kk:kk + 1, :]
            rank = rank + jnp.where(ck > score, 1, jnp.where(ck == score, jnp.where(blk > kk, 1, 0), 0))
        selb_ref[g] = jnp.where(rank < n_sel, 0.0, NEG)

    per_tile = TQ // SLC_BLOCK

    def mask_bias(j, c):
        rows = [jnp.broadcast_to(selb_ref[c, pl.ds(per_tile * j + t, 1), :], (SLC_BLOCK, TQ))
                for t in range(per_tile)]
        return jnp.concatenate(rows, axis=0)

    qs_ref[HEAD_DIM:K_COLS, :] = qx_ref[...]
    _masked_flash(i, qs_ref,
                  lambda j, c: ks_ref[0, c, pl.ds(pl.multiple_of(j * TQ, TQ), TQ), :],
                  lambda j, c: vst_ref[0, c, j],
                  mask_bias, delta_ref, m_ref, acc_ref, s_ref, mt_ref, al_ref, width, False)
    acc = acc_ref[...]
    o_slc = acc[0:HEAD_DIM] / acc[HEAD_DIM:HEAD_DIM + 1]

    outs = []
    for g in range(groups):
        sig_t = jax.nn.sigmoid(gt_ref[0, g, 0])
        sig_n = jax.nn.sigmoid(gn_ref[0, g])
        for r in range(rep):
            h = g * rep + r
            mixed = (sig_t[r:r + 1] * o_cmps[g][:, r * TQ:(r + 1) * TQ]
                     + sig_t[rep + r:rep + r + 1] * o_slc[:, h * TQ:(h + 1) * TQ])
            o_win = owin_ref[0, :, h * HEAD_DIM:(h + 1) * HEAD_DIM]
            outs.append(mixed.T + sig_n[:, 2 * rep + r:2 * rep + r + 1] * o_win)
    o_ref[0] = jnp.concatenate(outs, axis=1).astype(o_ref.dtype)


def _nsa_mix(qt, kc, vc, bias_c, ks, vs, near, far, gates, o_win):
    bn, g, nt, _, width = qt.shape
    s = nt * TQ
    nc = kc.shape[2]
    rep = width // TQ
    hq = g * rep * HEAD_DIM
    nblk = s // SLC_BLOCK
    n_sel = min(SLC_TOPN, nblk)
    ci = np.arange(nc)[:, None] * CMP_STRIDE
    sj = np.arange(nblk)[None, :] * SLC_BLOCK
    overlap = np.clip(np.minimum(ci + CMP_BLOCK, sj + SLC_BLOCK) - np.maximum(ci, sj), 0, None) / CMP_BLOCK
    overlap[nc - 1] = 0.0
    gates_t = gates.reshape(bn, g, nt, TQ, 3 * rep).transpose(0, 1, 2, 4, 3)
    kern = functools.partial(_nsa_kernel, rep=rep, n_sel=n_sel)
    return pl.pallas_call(
        kern,
        grid=(bn, nt),
        in_specs=[pl.BlockSpec((1, g, 1, HEAD_DIM, width), lambda b, i: (b, 0, i, 0, 0)),
                  pl.BlockSpec((1, g, nc, HEAD_DIM), lambda b, i: (b, 0, 0, 0)),
                  pl.BlockSpec((1, g, HEAD_DIM, nc), lambda b, i: (b, 0, 0, 0)),
                  pl.BlockSpec((g * rep, 1, nc, TQ), lambda b, i: (0, i, 0, 0)),
                  pl.BlockSpec((nblk, nc), lambda b, i: (0, 0)),
                  pl.BlockSpec((1, g, s, K_COLS), lambda b, i: (b, 0, 0, 0)),
                  pl.BlockSpec((1, g, nt, V_ROWS, TQ), lambda b, i: (b, 0, 0, 0, 0)),
                  pl.BlockSpec((K_COLS - HEAD_DIM, g * width), lambda b, i: (0, 0)),
                  pl.BlockSpec((2, TQ, g * width), lambda b, i: (0, 0, 0)),
                  pl.BlockSpec((1, g, 1, 3 * rep, TQ), lambda b, i: (b, 0, i, 0, 0)),
                  pl.BlockSpec((1, g, TQ, 3 * rep), lambda b, i: (b, 0, i, 0)),
                  pl.BlockSpec((1, TQ, hq), lambda b, i: (b, i, 0))],
        out_specs=pl.BlockSpec((1, TQ, hq), lambda b, i: (b, i, 0)),
        out_shape=jax.ShapeDtypeStruct((bn, s, hq), bf16),
        scratch_shapes=[pltpu.VMEM((K_COLS, g * width), bf16), pltpu.VMEM((g, nblk, TQ), f32),
                        pltpu.VMEM((1, g * width), f32), pltpu.VMEM((V_ROWS, g * width), f32),
                        pltpu.VMEM((2, TQ, g * width), f32), pltpu.VMEM((2, 1, g * width), f32),
                        pltpu.VMEM((2, 1, g * width), f32)],
        compiler_params=_cparams(("parallel", "arbitrary")),
        name="nsa_mix",
    )(qt, kc, jnp.swapaxes(vc, 2, 3), bias_c, jnp.asarray(overlap.T, f32), _keys_with_ones(ks), _value_tiles(vs),
      *_flash_operands(near, far), gates_t, gates, o_win)


def _dsa_kernel(qt_ref, k_ref, vt_ref, qit_ref, ki_ref, wit_ref, qx_ref, delta_ref, tri_ref, o_ref,
                keys_ref, qs_ref, m_ref, acc_ref, s_ref, mt_ref, al_ref, *, topk, chunk):
    i = pl.program_id(1)
    t0 = i * TQ
    width = qt_ref.shape[3]
    nh = width // TQ
    nih = qit_ref.shape[3] // TQ
    krow = lax.broadcasted_iota(i32, (TQ, TQ), 0)
    qcol = lax.broadcasted_iota(i32, (TQ, TQ), 1)
    n_pairs = jnp.maximum(i // 2 + 1, -(-topk // (2 * TQ)))

    qit = qit_ref[0, 0]
    wit = wit_ref[0, 0]

    def score_tile(j):
        kt = ki_ref[0, pl.ds(pl.multiple_of(j * TQ, TQ), TQ), :]
        rel = jnp.maximum(jnp.dot(kt, qit, preferred_element_type=f32), 0.0)
        sc = jnp.zeros((TQ, TQ), f32)
        for h in range(nih):
            sc = sc + wit[h:h + 1, :] * rel[:, h * TQ:(h + 1) * TQ]
        sc = jnp.where(j * TQ + krow <= t0 + qcol, sc, NEG)
        sc = jnp.where(sc == 0.0, 0.0, sc)
        bits = pltpu.bitcast(sc, i32)
        keys_ref[j] = jnp.where(bits < 0, bits ^ 0x7FFFFFFF, bits)

    def score_body(jj, carry):
        score_tile(2 * jj)
        score_tile(2 * jj + 1)
        return carry

    lax.fori_loop(0, n_pairs, score_body, 0)

    def count(pred_fn):
        def body(jj, cnt):
            return (cnt + jnp.where(pred_fn(keys_ref[2 * jj]), 1.0, 0.0)
                    + jnp.where(pred_fn(keys_ref[2 * jj + 1]), 1.0, 0.0))
        cnt = lax.fori_loop(0, n_pairs, body, jnp.zeros((TQ, TQ), f32))
        return jnp.sum(cnt, axis=0, keepdims=True)

    def bit_body(t, prefix):
        cand = prefix ^ jnp.left_shift(jnp.int32(1), 31 - t)
        return jnp.where(count(lambda kj: kj >= cand) >= topk, cand, prefix)

    thr = lax.fori_loop(0, 32, bit_body, jnp.full((1, TQ), INT_MIN, i32))
    n_ge = count(lambda kj: kj >= thr)

    @pl.when(jnp.max(n_ge) > topk)
    def _():
        need = topk - count(lambda kj: kj > thr)

        def fix_body(j, running):
            kj = keys_ref[j]
            tie = jnp.where(kj == thr, 1.0, 0.0)
            before = jnp.dot(tri_ref[...], tie.astype(bf16), preferred_element_type=f32) + running
            keys_ref[j] = jnp.where(before >= need, jnp.where(kj == thr, kj - 1, kj), kj)
            return running + jnp.sum(tie, axis=0, keepdims=True)

        lax.fori_loop(0, 2 * n_pairs, fix_body, jnp.zeros((1, TQ), f32))

    qs_ref[0:HEAD_DIM, :] = (qt_ref[0, 0].astype(f32) * (HEAD_DIM ** -0.5 * LOG2E)).astype(bf16)
    qs_ref[HEAD_DIM:K_COLS, :] = qx_ref[...]
    _masked_flash(i, qs_ref,
                  lambda j, c: k_ref[0, pl.ds(pl.multiple_of(j * TQ, TQ), TQ), :],
                  lambda j, c: vt_ref[0, j],
                  lambda j, c: jnp.where(keys_ref[j] >= thr, 0.0, NEG),
                  delta_ref, m_ref, acc_ref, s_ref, mt_ref, al_ref, chunk, True)
    acc = acc_ref[...]
    ot = acc[0:HEAD_DIM] / acc[HEAD_DIM:HEAD_DIM + 1]
    o_ref[0] = jnp.concatenate([ot[:, h * TQ:(h + 1) * TQ].T for h in range(nh)], axis=1).astype(o_ref.dtype)


def _dsa_attention(q, k, v, qi, ki, wi, near, far):
    bn, s, hq = q.shape
    nh = hq // HEAD_DIM
    nih = qi.shape[2] // IDX_DIM
    nt = s // TQ
    topk = min(C_TOPK_MAX, s // 4)
    width = nh * TQ
    qt = q.reshape(bn, nt, TQ, nh, HEAD_DIM).transpose(0, 1, 4, 3, 2).reshape(bn, nt, HEAD_DIM, width)
    qit = qi.reshape(bn, nt, TQ, nih, IDX_DIM).transpose(0, 1, 4, 3, 2).reshape(bn, nt, IDX_DIM, nih * TQ)
    wit = wi.reshape(bn, nt, TQ, nih).transpose(0, 1, 3, 2)
    tri = np.tril(np.ones((TQ, TQ), np.float32), -1)
    kern = functools.partial(_dsa_kernel, topk=topk, chunk=512)
    nt_scr = max(nt, -(-topk // TQ))
    return pl.pallas_call(
        kern,
        grid=(bn, nt),
        in_specs=[pl.BlockSpec((1, 1, HEAD_DIM, width), lambda b, i: (b, i, 0, 0)),
                  pl.BlockSpec((1, s, K_COLS), lambda b, i: (b, 0, 0)),
                  pl.BlockSpec((1, nt, V_ROWS, TQ), lambda b, i: (b, 0, 0, 0)),
                  pl.BlockSpec((1, 1, IDX_DIM, nih * TQ), lambda b, i: (b, i, 0, 0)),
                  pl.BlockSpec((1, s, IDX_DIM), lambda b, i: (b, 0, 0)),
                  pl.BlockSpec((1, 1, nih, TQ), lambda b, i: (b, i, 0, 0)),
                  pl.BlockSpec((K_COLS - HEAD_DIM, width), lambda b, i: (0, 0)),
                  pl.BlockSpec((2, TQ, width), lambda b, i: (0, 0, 0)),
                  pl.BlockSpec((TQ, TQ), lambda b, i: (0, 0))],
        out_specs=pl.BlockSpec((1, TQ, hq), lambda b, i: (b, i, 0)),
        out_shape=jax.ShapeDtypeStruct((bn, s, hq), bf16),
        scratch_shapes=[pltpu.VMEM((nt_scr, TQ, TQ), i32), pltpu.VMEM((K_COLS, width), bf16),
                        pltpu.VMEM((1, width), f32), pltpu.VMEM((V_ROWS, width), f32),
                        pltpu.VMEM((2, TQ, width), f32), pltpu.VMEM((2, 1, width), f32),
                        pltpu.VMEM((2, 1, width), f32)],
        compiler_params=_cparams(("parallel", "arbitrary")),
        name="dsa_attention",
    )(qt, _keys_with_ones(k), _value_tiles(v), qit, ki, wit, *_flash_operands(near, far), jnp.asarray(tri, bf16))


def _heads_first(a, g):
    bn, s, _ = a.shape
    return a.reshape(bn, s, g, HEAD_DIM).transpose(0, 2, 1, 3)


def _even_layer(x, table_t, w_in, sinks, cmpk_pos, cmpk_w1, cmpk_w2, cmpv_pos, cmpv_w1, cmpv_w2,
                w_out, ln1_g, ln1_b, ffn_gate, ffn_up, ffn_down, ln2_g, ln2_b):
    bn, s, d = x.shape
    xf = x.reshape(bn * s, d)
    a_q, a_kv, b_q, b_kv = A_HEADS * HEAD_DIM, A_KV * HEAD_DIM, B_HEADS * HEAD_DIM, B_KV * HEAD_DIM
    n_main = a_q + 2 * a_kv + b_q + 6 * b_kv
    rb = B_HEADS // B_KV
    w_tail = jnp.pad(w_in[:, n_main:], ((0, 0), (0, 128 - 3 * B_HEADS))).astype(bf16)
    main, tail = _proj_in(xf, w_in[:, :n_main].astype(bf16), w_tail)
    main = main.reshape(bn, s, n_main)
    gates = tail[:, :3 * B_HEADS]
    gates = gates.reshape(bn, s, 3, B_KV, rb).transpose(0, 3, 1, 2, 4).reshape(bn, B_KV, s, 3 * rb)
    cols = np.cumsum([0, a_q, a_kv, a_kv, b_q] + [b_kv] * 6)
    qa, ka, va, qb, kc_in, vc_in, ksl, vsl, kwn, vwn = [main[:, :, cols[t]:cols[t + 1]] for t in range(10)]

    near_idx = _bucket_of(np.arange(TQ)[:, None] + TQ - np.arange(2 * TQ)[None, :])
    near = _bias_expand(table_t, near_idx, TQ)
    pad_b = -(-(B_WINDOW - 1) // TQ) * TQ
    win_idx = _bucket_of(np.arange(TQ)[:, None] + pad_b - np.arange(pad_b + TQ)[None, :])
    bias_win = _bias_expand(table_t[A_HEADS:A_HEADS + B_HEADS], win_idx, TQ)
    nc = s // CMP_STRIDE
    nt = s // TQ
    cmp_idx = _bucket_of((np.arange(nt)[:, None, None] * TQ + np.arange(TQ)[None, None, :])
                         - (np.arange(nc)[None, :, None] * CMP_STRIDE + CMP_BLOCK - 1)).reshape(nt * nc, TQ)
    bias_c = _bias_expand(table_t[A_HEADS:A_HEADS + B_HEADS], cmp_idx, min(nt * nc, 512))
    bias_c = bias_c.reshape(B_HEADS, nt, nc, TQ)
    far = table_t[:, NUM_BUCKETS - 1]

    qt_a = _queries_on_lanes(qa, A_KV)
    qt_b = _queries_on_lanes(qb, B_KV)
    out_a = _banded_gqa(qt_a, _heads_first(ka, A_KV), _heads_first(va, A_KV), near[:A_HEADS], sinks, A_WINDOW, bf16)
    o_win = _banded_gqa(qt_b, _heads_first(kwn, B_KV), _heads_first(vwn, B_KV), bias_win, None, B_WINDOW, f32)
    kc = _nsa_compress(_heads_first(kc_in, B_KV), cmpk_pos, cmpk_w1, cmpk_w2)
    vc = _nsa_compress(_heads_first(vc_in, B_KV), cmpv_pos, cmpv_w1, cmpv_w2)
    out_b = _nsa_mix(qt_b, kc, vc, bias_c, _heads_first(ksl, B_KV), _heads_first(vsl, B_KV),
                     near[A_HEADS:A_HEADS + B_HEADS], far[A_HEADS:A_HEADS + B_HEADS], gates, o_win)
    mix = jnp.concatenate([out_a, out_b], axis=-1).reshape(bn * s, a_q + b_q)
    x1 = _proj_res_ln(mix, w_out.astype(bf16), xf, ln1_g, ln1_b)
    x2 = _ffn_res_ln(x1, ffn_gate.astype(bf16), ffn_up.astype(bf16), ffn_down.astype(bf16), ln2_g, ln2_b)
    return x2.reshape(bn, s, d), near, far


def _odd_layer(x, near, far, w_in, w_out, ln1_g, ln1_b, router, exp_gate, exp_up, exp_down, ln2_g, ln2_b):
    bn, s, d = x.shape
    xf = x.reshape(bn * s, d)
    c_q = C_HEADS * HEAD_DIM
    n_main = c_q + 2 * HEAD_DIM + IDX_HEADS * IDX_DIM
    w_tail = jnp.pad(w_in[:, n_main:], ((0, 0), (0, 128 - IDX_DIM - IDX_HEADS))).astype(bf16)
    main, tail = _proj_in(xf, w_in[:, :n_main].astype(bf16), w_tail)
    main = main.reshape(bn, s, n_main)
    tail = tail.reshape(bn, s, 128)
    q = main[:, :, :c_q]
    k = main[:, :, c_q:c_q + HEAD_DIM]
    v = main[:, :, c_q + HEAD_DIM:c_q + 2 * HEAD_DIM]
    qi = main[:, :, c_q + 2 * HEAD_DIM:]
    ki = tail[:, :, :IDX_DIM].astype(bf16)
    wi = tail[:, :, IDX_DIM:IDX_DIM + IDX_HEADS]
    mix = _dsa_attention(q, k, v, qi, ki, wi, near[:C_HEADS], far[:C_HEADS]).reshape(bn * s, c_q)
    x1 = _proj_res_ln(mix, w_out.astype(bf16), xf, ln1_g, ln1_b)
    x2 = _moe_res_ln(x1, router, exp_gate.astype(bf16), exp_up.astype(bf16), exp_down.astype(bf16), ln2_g, ln2_b)
    return x2.reshape(bn, s, d)


def kernel(x, rel_bias, l0_w_in, l0_sinks, l0_cmpk_pos, l0_cmpk_w1, l0_cmpk_w2, l0_cmpv_pos, l0_cmpv_w1,
           l0_cmpv_w2, l0_w_out, l0_ln1_g, l0_ln1_b, l0_ffn_gate, l0_ffn_up, l0_ffn_down, l0_ln2_g, l0_ln2_b,
           l1_w_in, l1_w_out, l1_ln1_g, l1_ln1_b, l1_router, l1_exp_gate, l1_exp_up, l1_exp_down, l1_ln2_g,
           l1_ln2_b):
    table_t = rel_bias.T.astype(f32)
    x, near, far = _even_layer(x, table_t, l0_w_in, l0_sinks, l0_cmpk_pos, l0_cmpk_w1, l0_cmpk_w2, l0_cmpv_pos,
                               l0_cmpv_w1, l0_cmpv_w2, l0_w_out, l0_ln1_g, l0_ln1_b, l0_ffn_gate, l0_ffn_up,
                               l0_ffn_down, l0_ln2_g, l0_ln2_b)
    return _odd_layer(x, near, far, l1_w_in, l1_w_out, l1_ln1_g, l1_ln1_b, l1_router, l1_exp_gate, l1_exp_up,
                      l1_exp_down, l1_ln2_g, l1_ln2_b)
```

```python
import functools
import math

import numpy as np
import jax
import jax.numpy as jnp
from jax import lax
from jax.experimental import pallas as pl
from jax.experimental.pallas import tpu as pltpu

f32 = jnp.float32
bf16 = jnp.bfloat16
i32 = jnp.int32

D_MODEL = 1024
HEAD_DIM = 64
NUM_BUCKETS = 32
MAX_DISTANCE = 128
A_HEADS, A_KV, A_WINDOW = 8, 2, 128
B_HEADS, B_KV, B_WINDOW = 8, 2, 512
CMP_BLOCK, CMP_STRIDE = 32, 16
SLC_BLOCK, SLC_TOPN = 64, 16
C_HEADS, IDX_HEADS, IDX_DIM, C_TOPK_MAX = 16, 8, 32, 256
D_FF, N_EXPERTS = 2816, 8
DEPTH = 2
ALPHA = (2.0 * DEPTH) ** 0.25
LN_EPS = 1e-5
NEG = -1e30
POS_BIG = 1e30
INT_MIN = -(2 ** 31)

TQ = 128
FF_CHUNK = 256
VMEM_LIMIT = 56 * 1024 * 1024


def _cparams(sem):
    return pltpu.CompilerParams(dimension_semantics=sem, vmem_limit_bytes=VMEM_LIMIT)


def _bucket_table(max_dist):
    n = np.arange(max_dist + 1)
    max_exact = NUM_BUCKETS // 2
    nf = np.maximum(n, 1).astype(np.float64)
    large = max_exact + (np.log(nf / max_exact) / math.log(MAX_DISTANCE / max_exact)
                         * (NUM_BUCKETS - max_exact)).astype(np.int64)
    large = np.minimum(large, NUM_BUCKETS - 1)
    return np.where(n < max_exact, n, large).astype(np.int32)


def _bucket_of(dist):
    dist = np.maximum(dist, 0)
    return _bucket_table(int(dist.max()))[dist]


def _proj_in_kernel(x_ref, wm_ref, wt_ref, om_ref, ot_ref):
    xb = x_ref[...].astype(bf16)
    om_ref[...] = jnp.dot(xb, wm_ref[...], preferred_element_type=f32).astype(om_ref.dtype)
    ot_ref[...] = jnp.dot(xb, wt_ref[...], preferred_element_type=f32)


def _proj_in(x, w_main, w_tail, tm=512):
    m, k = x.shape
    nm, ntl = w_main.shape[1], w_tail.shape[1]
    return pl.pallas_call(
        _proj_in_kernel,
        grid=(m // tm,),
        in_specs=[pl.BlockSpec((tm, k), lambda i: (i, 0)),
                  pl.BlockSpec((k, nm), lambda i: (0, 0)),
                  pl.BlockSpec((k, ntl), lambda i: (0, 0))],
        out_specs=[pl.BlockSpec((tm, nm), lambda i: (i, 0)), pl.BlockSpec((tm, ntl), lambda i: (i, 0))],
        out_shape=[jax.ShapeDtypeStruct((m, nm), bf16), jax.ShapeDtypeStruct((m, ntl), f32)],
        compiler_params=_cparams(("parallel",)),
        name="proj_in",
    )(x, w_main, w_tail)


def _bias_expand_kernel(tab_ref, idx_ref, o_ref):
    h = pl.program_id(0)
    idx = idx_ref[...]
    acc = jnp.zeros(idx.shape, f32)
    for b in range(NUM_BUCKETS):
        acc = jnp.where(idx == b, tab_ref[h, b], acc)
    o_ref[0] = acc


def _bias_expand(table_t, idx, tr):
    nh = table_t.shape[0]
    r, c = idx.shape
    return pl.pallas_call(
        _bias_expand_kernel,
        grid=(nh, r // tr),
        in_specs=[pl.BlockSpec(memory_space=pltpu.SMEM),
                  pl.BlockSpec((tr, c), lambda h, i: (i, 0))],
        out_specs=pl.BlockSpec((1, tr, c), lambda h, i: (h, i, 0)),
        out_shape=jax.ShapeDtypeStruct((nh, r, c), f32),
        compiler_params=_cparams(("parallel", "arbitrary")),
        name="bias_expand",
    )(table_t, jnp.asarray(idx, dtype=i32))


def _layer_norm(z, g, b):
    mu = jnp.mean(z, axis=-1, keepdims=True)
    zc = z - mu
    var = jnp.mean(zc * zc, axis=-1, keepdims=True)
    return zc * lax.rsqrt(var + LN_EPS) * g + b


def _proj_ln_kernel(mix_ref, w_ref, x_ref, g_ref, b_ref, o_ref):
    y = jnp.dot(mix_ref[...], w_ref[...], preferred_element_type=f32)
    o_ref[...] = _layer_norm(ALPHA * x_ref[...] + y, g_ref[...], b_ref[...])


def _proj_res_ln(mix, w, x, g, b, tm=512):
    m, k = mix.shape
    d = w.shape[1]
    return pl.pallas_call(
        _proj_ln_kernel,
        grid=(m // tm,),
        in_specs=[pl.BlockSpec((tm, k), lambda i: (i, 0)),
                  pl.BlockSpec((k, d), lambda i: (0, 0)),
                  pl.BlockSpec((tm, d), lambda i: (i, 0)),
                  pl.BlockSpec((1, d), lambda i: (0, 0)),
                  pl.BlockSpec((1, d), lambda i: (0, 0))],
        out_specs=pl.BlockSpec((tm, d), lambda i: (i, 0)),
        out_shape=jax.ShapeDtypeStruct((m, d), f32),
        compiler_params=_cparams(("parallel",)),
        name="proj_res_ln",
    )(mix, w, x, g.reshape(1, d), b.reshape(1, d))


def _swiglu_accumulate(xb, wg_ref, wu_ref, wd_ref, acc_ref, widx):
    d_ff = wg_ref.shape[-1]
    for c in range(d_ff // FF_CHUNK):
        cols = slice(c * FF_CHUNK, (c + 1) * FF_CHUNK)
        gate = jnp.dot(xb, wg_ref[widx + (slice(None), cols)], preferred_element_type=f32)
        up = jnp.dot(xb, wu_ref[widx + (slice(None), cols)], preferred_element_type=f32)
        h = jax.nn.silu(gate) * up
        acc_ref[...] += jnp.dot(h.astype(bf16), wd_ref[widx + (cols, slice(None))], preferred_element_type=f32)


def _ffn_kernel(x_ref, wg_ref, wu_ref, wd_ref, g_ref, b_ref, o_ref, acc_ref):
    x = x_ref[...]
    acc_ref[...] = jnp.zeros_like(acc_ref)
    _swiglu_accumulate(x.astype(bf16), wg_ref, wu_ref, wd_ref, acc_ref, ())
    o_ref[...] = _layer_norm(ALPHA * x + acc_ref[...], g_ref[...], b_ref[...])


def _ffn_res_ln(x, wg, wu, wd, g, b, tm=512):
    m, d = x.shape
    d_ff = wg.shape[1]
    return pl.pallas_call(
        _ffn_kernel,
        grid=(m // tm,),
        in_specs=[pl.BlockSpec((tm, d), lambda i: (i, 0)),
                  pl.BlockSpec((d, d_ff), lambda i: (0, 0)),
                  pl.BlockSpec((d, d_ff), lambda i: (0, 0)),
                  pl.BlockSpec((d_ff, d), lambda i: (0, 0)),
                  pl.BlockSpec((1, d), lambda i: (0, 0)),
                  pl.BlockSpec((1, d), lambda i: (0, 0))],
        out_specs=pl.BlockSpec((tm, d), lambda i: (i, 0)),
        out_shape=jax.ShapeDtypeStruct((m, d), f32),
        scratch_shapes=[pltpu.VMEM((tm, d), f32)],
        compiler_params=_cparams(("parallel",)),
        name="ffn_res_ln",
    )(x, wg, wu, wd, g.reshape(1, d), b.reshape(1, d))


MOE_CAP = 320


def _router_gates(x, r_ref, lane):
    logits = jnp.dot(x, r_ref[...], preferred_element_type=f32)
    logits = jnp.where(lane < N_EXPERTS, logits, -jnp.inf)
    m1 = jnp.max(logits, axis=-1, keepdims=True)
    i1 = jnp.min(jnp.where(logits == m1, lane, 128), axis=-1, keepdims=True)
    rest = jnp.where(lane == i1, -jnp.inf, logits)
    m2 = jnp.max(rest, axis=-1, keepdims=True)
    i2 = jnp.min(jnp.where(rest == m2, lane, 128), axis=-1, keepdims=True)
    e2 = jnp.exp(m2 - m1)
    g1 = 1.0 / (1.0 + e2)
    return jnp.where(lane == i1, g1, 0.0) + jnp.where(lane == i2, e2 * g1, 0.0)


def _moe_kernel(x_ref, r_ref, tri_ref, wg_ref, wu_ref, wd_ref, g_ref, b_ref, o_ref,
                split_ref, post_ref, yc_ref):
    e = pl.program_id(1)
    tm = x_ref.shape[0]

    @pl.when(e == 0)
    def _():
        comb = _router_gates(x_ref[...], r_ref, lax.broadcasted_iota(i32, (tm, 128), 1))
        hi = comb.astype(bf16)
        rest = comb - hi.astype(f32)
        mid = rest.astype(bf16)
        split_ref[0] = hi
        split_ref[1] = mid
        split_ref[2] = (rest - mid.astype(f32)).astype(bf16)
        chosen = jnp.where(comb > 0.0, 1.0, 0.0)
        pos = jnp.dot(tri_ref[...], chosen.astype(bf16), preferred_element_type=f32)
        pos = jnp.where(comb > 0.0, pos, -1.0)
        for t in range(tm // 128):
            rows = slice(t * 128, (t + 1) * 128)
            post_ref[:, rows] = pos[rows].T[0:N_EXPERTS]
        hi32 = jnp.floor((pos + 1.0) * (1.0 / 32.0))
        split_ref[3] = hi32.astype(bf16)
        split_ref[4] = (pos + 1.0 - 32.0 * hi32).astype(bf16)
        o_ref[...] = jnp.zeros_like(o_ref)

    pos_row = post_ref[pl.ds(e, 1), :].astype(i32)
    n_tok = jnp.sum(jnp.where(pos_row >= 0, 1.0, 0.0)).astype(i32)
    pick = jnp.where(lax.broadcasted_iota(i32, (128, 128), 0) == e, 1.0, 0.0).astype(bf16)

    def column(t):
        return jnp.dot(split_ref[t], pick, preferred_element_type=f32)

    gate_b = column(0) + column(1) + column(2)
    pos_b = (32.0 * column(3) + column(4) - 1.0).astype(i32)
    lane = lax.broadcasted_iota(i32, (tm, 128), 1)

    def pass_body(p, carry):
        base = p * MOE_CAP
        gather = jnp.where(pos_row == base + lax.broadcasted_iota(i32, (MOE_CAP, tm), 0), 1.0, 0.0).astype(bf16)
        xc = jnp.dot(gather, x_ref[...].astype(bf16), preferred_element_type=f32).astype(bf16)
        yc_ref[...] = jnp.zeros_like(yc_ref)
        _swiglu_accumulate(xc, wg_ref, wu_ref, wd_ref, yc_ref, (0,))
        scatter = jnp.concatenate([jnp.where(pos_b == base + t * 128 + lane, 1.0, 0.0).astype(bf16)
                                   for t in range(-(-MOE_CAP // 128))], axis=1)[:, :MOE_CAP]
        y = jnp.dot(scatter, yc_ref[...].astype(bf16), preferred_element_type=f32)
        o_ref[...] += jnp.concatenate([gate_b] * (y.shape[1] // 128), axis=1) * y
        return carry

    lax.fori_loop(0, (n_tok + MOE_CAP - 1) // MOE_CAP, pass_body, 0)

    @pl.when(e == N_EXPERTS - 1)
    def _():
        o_ref[...] = _layer_norm(ALPHA * x_ref[...] + o_ref[...], g_ref[...], b_ref[...])


def _moe_res_ln(x, router, wg, wu, wd, g, b, tm=1024):
    m, d = x.shape
    ne, _, d_ff = wg.shape
    tm = min(tm, m)
    router_p = jnp.pad(router, ((0, 0), (0, 128 - ne)))
    tri = jnp.asarray(np.tril(np.ones((tm, tm), np.float32), -1), bf16)
    single = pl.Buffered(1)
    return pl.pallas_call(
        _moe_kernel,
        grid=(m // tm, ne),
        in_specs=[pl.BlockSpec((tm, d), lambda i, e: (i, 0), pipeline_mode=single),
                  pl.BlockSpec((d, 128), lambda i, e: (0, 0), pipeline_mode=single),
                  pl.BlockSpec((tm, tm), lambda i, e: (0, 0), pipeline_mode=single),
                  pl.BlockSpec((1, d, d_ff), lambda i, e: (e, 0, 0)),
                  pl.BlockSpec((1, d, d_ff), lambda i, e: (e, 0, 0)),
                  pl.BlockSpec((1, d_ff, d), lambda i, e: (e, 0, 0)),
                  pl.BlockSpec((1, d), lambda i, e: (0, 0)),
                  pl.BlockSpec((1, d), lambda i, e: (0, 0))],
        out_specs=pl.BlockSpec((tm, d), lambda i, e: (i, 0)),
        out_shape=jax.ShapeDtypeStruct((m, d), f32),
        scratch_shapes=[pltpu.VMEM((5, tm, 128), bf16), pltpu.VMEM((N_EXPERTS, tm), f32),
                        pltpu.VMEM((MOE_CAP, d), f32)],
        compiler_params=_cparams(("parallel", "arbitrary")),
        name="moe_res_ln",
    )(x, router_p, tri, wg, wu, wd, g.reshape(1, d), b.reshape(1, d))


def _compress_kernel(blk_ref, pos_ref, w1_ref, w2_ref, o_ref):
    xb = (blk_ref[0].astype(f32) + pos_ref[...]).astype(bf16)
    h = jax.nn.gelu(jnp.dot(xb, w1_ref[...], preferred_element_type=f32))
    o_ref[0] = jnp.dot(h.astype(bf16), w2_ref[...], preferred_element_type=f32).astype(o_ref.dtype)


def _nsa_compress(kv, pos, w1, w2):
    bn, g, s, dh = kv.shape
    nc = s // CMP_STRIDE
    chunks = kv.reshape(bn, g, nc, CMP_STRIDE * dh)
    nxt = jnp.pad(chunks[:, :, 1:], ((0, 0), (0, 0), (0, 1), (0, 0)))
    blocks = jnp.concatenate([chunks, nxt], axis=-1).reshape(bn * g, nc, CMP_BLOCK * dh)
    cin = CMP_BLOCK * dh
    hid = w1.shape[1]
    out = pl.pallas_call(
        _compress_kernel,
        grid=(bn * g,),
        in_specs=[pl.BlockSpec((1, nc, cin), lambda i: (i, 0, 0)),
                  pl.BlockSpec((1, cin), lambda i: (0, 0)),
                  pl.BlockSpec((cin, hid), lambda i: (0, 0)),
                  pl.BlockSpec((hid, dh), lambda i: (0, 0))],
        out_specs=pl.BlockSpec((1, nc, dh), lambda i: (i, 0, 0)),
        out_shape=jax.ShapeDtypeStruct((bn * g, nc, dh), bf16),
        compiler_params=_cparams(("parallel",)),
        name="nsa_compress",
    )(blocks, pos.reshape(1, cin), w1.astype(bf16), w2.astype(bf16))
    return out.reshape(bn, g, nc, dh)


V_ROWS = HEAD_DIM + 16


LOG2E = 1.4426950408889634
K_COLS = 2 * HEAD_DIM


def _masked_flash(i, qs_ref, key_tile, value_tile, mask_bias, delta_ref, m_ref, acc_ref, s_ref, mt_ref, al_ref, chunk,
                  shared_keys):
    width = qs_ref.shape[1]
    n_chunks = width // chunk
    m_ref[...] = jnp.full(m_ref.shape, NEG, f32)
    acc_ref[...] = jnp.zeros(acc_ref.shape, f32)

    def scores_chunk(j, delta, c, shared):
        slot = j % 2
        if c == 0 or not shared_keys:
            shared["kt"] = key_tile(j, c)
            shared["maskb"] = jnp.concatenate([mask_bias(j, c)] * (chunk // TQ), axis=1)
        cols = slice(c * chunk, (c + 1) * chunk)
        s = jnp.dot(shared["kt"], qs_ref[:, cols], preferred_element_type=f32) + shared["maskb"]
        if delta is not None:
            s = s + delta_ref[delta, :, cols]
        s_ref[slot, :, cols] = s
        m_old = m_ref[:, cols]
        m_new = jnp.maximum(m_old, jnp.max(s, axis=0, keepdims=True))
        m_ref[:, cols] = m_new
        mt_ref[slot, :, cols] = m_new
        al_ref[slot, :, cols] = jnp.exp2(m_old - m_new)

    def values_chunk(j, c, shared):
        slot = j % 2
        if c == 0 or not shared_keys:
            shared["vt"] = value_tile(j, c)
        cols = slice(c * chunk, (c + 1) * chunk)
        p = jnp.exp2(s_ref[slot, :, cols] - mt_ref[slot, :, cols])
        acc_ref[:, cols] = (al_ref[slot, :, cols] * acc_ref[:, cols]
                            + jnp.dot(shared["vt"], p.astype(bf16), preferred_element_type=f32))

    def scores(j, delta):
        shared = {}
        for c in range(n_chunks):
            scores_chunk(j, delta, c, shared)

    def values(j):
        shared = {}
        for c in range(n_chunks):
            values_chunk(j, c, shared)

    def values_then_scores(jv, js, delta):
        sv, ss = {}, {}
        for c in range(n_chunks):
            values_chunk(jv, c, sv)
            scores_chunk(js, delta, c, ss)

    @pl.when(i >= 2)
    def _():
        scores(0, None)

        def pipe_body(j, carry):
            values_then_scores(j - 1, j, None)
            return carry

        lax.fori_loop(1, i - 1, pipe_body, 0)
        values_then_scores(i - 2, i - 1, 0)
        values_then_scores(i - 1, i, 1)

    @pl.when(i == 1)
    def _():
        scores(0, 0)
        values_then_scores(0, 1, 1)

    @pl.when(i == 0)
    def _():
        scores(0, 1)

    values(i)


def _flash_operands(near, far):
    nh = near.shape[0]
    width = nh * TQ
    far2 = jnp.repeat(far * LOG2E, TQ).reshape(1, width)
    hi = far2.astype(bf16)
    lo = (far2 - hi.astype(f32)).astype(bf16)
    extra = jnp.concatenate([hi, lo, jnp.zeros((K_COLS - HEAD_DIM - 2, width), bf16)], axis=0)
    neart = near.reshape(nh, TQ, 2, TQ).transpose(2, 3, 0, 1).reshape(2, TQ, width) * LOG2E
    neart = neart - (hi.astype(f32) + lo.astype(f32))
    causal = np.where(np.arange(TQ)[:, None] <= np.arange(TQ)[None, :], 0.0, NEG).astype(np.float32)
    return extra, jnp.stack([neart[0], neart[1] + jnp.asarray(np.tile(causal, (1, nh)))], axis=0)


def _keys_with_ones(k):
    lead = k.shape[:-1]
    return jnp.concatenate([k, jnp.ones(lead + (2,), k.dtype), jnp.zeros(lead + (K_COLS - HEAD_DIM - 2,), k.dtype)],
                           axis=-1)


def _value_tiles(v):
    lead = v.shape[:-2]
    nt = v.shape[-2] // TQ
    vt = jnp.swapaxes(v.reshape(lead + (nt, TQ, HEAD_DIM)), -1, -2)
    return jnp.concatenate([vt, jnp.ones(lead + (nt, 1, TQ), vt.dtype),
                            jnp.zeros(lead + (nt, V_ROWS - HEAD_DIM - 1, TQ), vt.dtype)], axis=-2)


def _queries_on_lanes(q, g):
    bn, s, hq = q.shape
    rep = hq // (g * HEAD_DIM)
    nt = s // TQ
    return q.reshape(bn, nt, TQ, g, rep, HEAD_DIM).transpose(0, 3, 1, 5, 4, 2).reshape(bn, g, nt, HEAD_DIM, rep * TQ)


def _banded_kernel(qt_ref, k_ref, vt_ref, bias_ref, neg_ref, sink_ref, o_ref, *, rep, kt_tiles, use_sinks):
    i = pl.program_id(1)
    kw = kt_tiles * TQ
    outs = []
    for g in range(qt_ref.shape[1]):
        qs = jnp.concatenate([(qt_ref[0, g, 0].astype(f32) * (HEAD_DIM ** -0.5)).astype(bf16), neg_ref[...]], axis=0)
        k = k_ref[0, g, pl.ds(pl.multiple_of(i * TQ, TQ), kw), :]
        s = jnp.dot(k, qs, preferred_element_type=f32) + bias_ref[g]
        m = jnp.max(s, axis=0, keepdims=True)
        if use_sinks:
            m = jnp.maximum(m, sink_ref[g])
        p = jnp.exp(s - m).astype(bf16)
        vt = jnp.concatenate([vt_ref[0, g, i + t] for t in range(kt_tiles)], axis=1)
        acc = jnp.dot(vt, p, preferred_element_type=f32)
        l = acc[HEAD_DIM:HEAD_DIM + 1]
        if use_sinks:
            l = l + jnp.exp(sink_ref[g] - m)
        ot = acc[0:HEAD_DIM] / l
        outs += [ot[:, r * TQ:(r + 1) * TQ].T for r in range(rep)]
    o_ref[0] = jnp.concatenate(outs, axis=1).astype(o_ref.dtype)


def _banded_gqa(qt, k, v, bias, sinks, window, out_dtype):
    bn, g, nt, _, width = qt.shape
    rep = width // TQ
    s = nt * TQ
    pad = -(-(window - 1) // TQ) * TQ
    kw = pad + TQ
    kd = 2 * HEAD_DIM
    flag = jnp.broadcast_to((jnp.arange(pad + s) < pad).astype(k.dtype)[:, None], (bn, g, pad + s, 1))
    kp = jnp.concatenate([jnp.pad(k, ((0, 0), (0, 0), (pad, 0), (0, 0))), flag,
                          jnp.zeros((bn, g, pad + s, kd - HEAD_DIM - 1), k.dtype)], axis=-1)
    vt = _value_tiles(jnp.pad(v, ((0, 0), (0, 0), (pad, 0), (0, 0))))
    neg = np.zeros((kd - HEAD_DIM, width), np.float32)
    neg[0] = NEG
    dist = np.arange(TQ)[:, None] + pad - np.arange(kw)[None, :]
    band = jnp.asarray((dist >= 0) & (dist < window))
    bias_t = jnp.where(band, bias, NEG).reshape(g, rep, TQ, kw).transpose(0, 3, 1, 2).reshape(g, kw, width)
    use_sinks = sinks is not None
    sink_rows = (jnp.repeat(sinks.astype(f32), TQ) if use_sinks else jnp.zeros((g * width,), f32)).reshape(g, 1, width)
    kern = functools.partial(_banded_kernel, rep=rep, kt_tiles=kw // TQ, use_sinks=use_sinks)
    rw = rep * HEAD_DIM
    return pl.pallas_call(
        kern,
        grid=(bn, nt),
        in_specs=[pl.BlockSpec((1, g, 1, HEAD_DIM, width), lambda b, i: (b, 0, i, 0, 0)),
                  pl.BlockSpec((1, g, pad + s, kd), lambda b, i: (b, 0, 0, 0)),
                  pl.BlockSpec((1, g, (pad + s) // TQ, V_ROWS, TQ), lambda b, i: (b, 0, 0, 0, 0)),
                  pl.BlockSpec((g, kw, width), lambda b, i: (0, 0, 0)),
                  pl.BlockSpec((kd - HEAD_DIM, width), lambda b, i: (0, 0)),
                  pl.BlockSpec((g, 1, width), lambda b, i: (0, 0, 0))],
        out_specs=pl.BlockSpec((1, TQ, g * rw), lambda b, i: (b, i, 0)),
        out_shape=jax.ShapeDtypeStruct((bn, s, g * rw), out_dtype),
        compiler_params=_cparams(("parallel", "arbitrary")),
        name="banded_gqa_w%d" % window,
    )(qt, kp, vt, bias_t, jnp.asarray(neg, bf16), sink_rows)


def _nsa_kernel(qt_ref, kc_ref, vct_ref, biasc_ref, ovt_ref, ks_ref, vst_ref, qx_ref, delta_ref, gt_ref, gn_ref,
                owin_ref, o_ref, qs_ref, selb_ref, m_ref, acc_ref, s_ref, mt_ref, al_ref, *, rep, n_sel):
    i = pl.program_id(1)
    t0 = i * TQ
    groups = qt_ref.shape[1]
    nc = kc_ref.shape[2]
    nblk = ovt_ref.shape[0]
    width = rep * TQ
    cmp_end = lax.broadcasted_iota(i32, (nc, TQ), 0) * CMP_STRIDE + (CMP_BLOCK - 1)
    valid_c = t0 + lax.broadcasted_iota(i32, (nc, TQ), 1) >= cmp_end
    blk = lax.broadcasted_iota(i32, (nblk, TQ), 0)
    tq = t0 + lax.broadcasted_iota(i32, (nblk, TQ), 1)
    cur = tq // SLC_BLOCK
    forced = (blk == 0) | (blk == cur) | (blk == cur - 1)
    admitted = blk * SLC_BLOCK <= tq

    o_cmps = []
    for g in range(groups):
        qf = qt_ref[0, g, 0].astype(f32)
        qs = (qf * (HEAD_DIM ** -0.5)).astype(bf16)
        qs_ref[0:HEAD_DIM, g * width:(g + 1) * width] = (qf * (HEAD_DIM ** -0.5 * LOG2E)).astype(bf16)

        s_c = jnp.dot(kc_ref[0, g], qs, preferred_element_type=f32)
        p_cols = []
        p_sum = jnp.zeros((nc, TQ), f32)
        for r in range(rep):
            sr = jnp.where(valid_c, s_c[:, r * TQ:(r + 1) * TQ] + biasc_ref[g * rep + r, 0], NEG)
            m = jnp.max(sr, axis=0, keepdims=True)
            e = jnp.where(valid_c, jnp.exp(sr - m), 0.0)
            l = jnp.sum(e, axis=0, keepdims=True)
            p = e * (1.0 / jnp.maximum(l, 1e-30))
            p_cols.append(p.astype(bf16))
            p_sum = p_sum + p
        o_cmps.append(jnp.dot(vct_ref[0, g], jnp.concatenate(p_cols, axis=1), preferred_element_type=f32))

        imp = jnp.dot(ovt_ref[...], p_sum, preferred_element_type=f32)
        score = jnp.where(forced, POS_BIG, jnp.where(admitted, imp, NEG))
        rank = jnp.zeros((nblk, TQ), i32)
        for kk in range(nblk):
            ck = score[kk:kk + 1, :]
            rank = rank + jnp.where(ck > score, 1, jnp.where(ck == score, jnp.where(blk > kk, 1, 0), 0))
        selb_ref[g] = jnp.where(rank < n_sel, 0.0, NEG)

    per_tile = TQ // SLC_BLOCK

    def mask_bias(j, c):
        rows = [jnp.broadcast_to(selb_ref[c, pl.ds(per_tile * j + t, 1), :], (SLC_BLOCK, TQ))
                for t in range(per_tile)]
        return jnp.concatenate(rows, axis=0)

    qs_ref[HEAD_DIM:K_COLS, :] = qx_ref[...]
    _masked_flash(i, qs_ref,
                  lambda j, c: ks_ref[0, c, pl.ds(pl.multiple_of(j * TQ, TQ), TQ), :],
                  lambda j, c: vst_ref[0, c, j],
                  mask_bias, delta_ref, m_ref, acc_ref, s_ref, mt_ref, al_ref, width, False)
    acc = acc_ref[...]
    o_slc = acc[0:HEAD_DIM] / acc[HEAD_DIM:HEAD_DIM + 1]

    outs = []
    for g in range(groups):
        sig_t = jax.nn.sigmoid(gt_ref[0, g, 0])
        sig_n = jax.nn.sigmoid(gn_ref[0, g])
        for r in range(rep):
            h = g * rep + r
            mixed = (sig_t[r:r + 1] * o_cmps[g][:, r * TQ:(r + 1) * TQ]
                     + sig_t[rep + r:rep + r + 1] * o_slc[:, h * TQ:(h + 1) * TQ])
            o_win = owin_ref[0, :, h * HEAD_DIM:(h + 1) * HEAD_DIM]
            outs.append(mixed.T + sig_n[:, 2 * rep + r:2 * rep + r + 1] * o_win)
    o_ref[0] = jnp.concatenate(outs, axis=1).astype(o_ref.dtype)


def _nsa_mix(qt, kc, vc, bias_c, ks, vs, near, far, gates, o_win):
    bn, g, nt, _, width = qt.shape
    s = nt * TQ
    nc = kc.shape[2]
    rep = width // TQ
    hq = g * rep * HEAD_DIM
    nblk = s // SLC_BLOCK
    n_sel = min(SLC_TOPN, nblk)
    ci = np.arange(nc)[:, None] * CMP_STRIDE
    sj = np.arange(nblk)[None, :] * SLC_BLOCK
    overlap = np.clip(np.minimum(ci + CMP_BLOCK, sj + SLC_BLOCK) - np.maximum(ci, sj), 0, None) / CMP_BLOCK
    overlap[nc - 1] = 0.0
    gates_t = gates.reshape(bn, g, nt, TQ, 3 * rep).transpose(0, 1, 2, 4, 3)
    kern = functools.partial(_nsa_kernel, rep=rep, n_sel=n_sel)
    return pl.pallas_call(
        kern,
        grid=(bn, nt),
        in_specs=[pl.BlockSpec((1, g, 1, HEAD_DIM, width), lambda b, i: (b, 0, i, 0, 0)),
                  pl.BlockSpec((1, g, nc, HEAD_DIM), lambda b, i: (b, 0, 0, 0)),
                  pl.BlockSpec((1, g, HEAD_DIM, nc), lambda b, i: (b, 0, 0, 0)),
                  pl.BlockSpec((g * rep, 1, nc, TQ), lambda b, i: (0, i, 0, 0)),
                  pl.BlockSpec((nblk, nc), lambda b, i: (0, 0)),
                  pl.BlockSpec((1, g, s, K_COLS), lambda b, i: (b, 0, 0, 0)),
                  pl.BlockSpec((1, g, nt, V_ROWS, TQ), lambda b, i: (b, 0, 0, 0, 0)),
                  pl.BlockSpec((K_COLS - HEAD_DIM, g * width), lambda b, i: (0, 0)),
                  pl.BlockSpec((2, TQ, g * width), lambda b, i: (0, 0, 0)),
                  pl.BlockSpec((1, g, 1, 3 * rep, TQ), lambda b, i: (b, 0, i, 0, 0)),
                  pl.BlockSpec((1, g, TQ, 3 * rep), lambda b, i: (b, 0, i, 0)),
                  pl.BlockSpec((1, TQ, hq), lambda b, i: (b, i, 0))],
        out_specs=pl.BlockSpec((1, TQ, hq), lambda b, i: (b, i, 0)),
        out_shape=jax.ShapeDtypeStruct((bn, s, hq), bf16),
        scratch_shapes=[pltpu.VMEM((K_COLS, g * width), bf16), pltpu.VMEM((g, nblk, TQ), f32),
                        pltpu.VMEM((1, g * width), f32), pltpu.VMEM((V_ROWS, g * width), f32),
                        pltpu.VMEM((2, TQ, g * width), f32), pltpu.VMEM((2, 1, g * width), f32),
                        pltpu.VMEM((2, 1, g * width), f32)],
        compiler_params=_cparams(("parallel", "arbitrary")),
        name="nsa_mix",
    )(qt, kc, jnp.swapaxes(vc, 2, 3), bias_c, jnp.asarray(overlap.T, f32), _keys_with_ones(ks), _value_tiles(vs),
      *_flash_operands(near, far), gates_t, gates, o_win)


RADIX_BITS_PER_CHECK = 8


def _dsa_kernel(qt_ref, k_ref, vt_ref, qit_ref, ki_ref, wit_ref, qx_ref, delta_ref, tri_ref, o_ref,
                keys_ref, qs_ref, m_ref, acc_ref, s_ref, mt_ref, al_ref, *, topk, chunk):
    i = pl.program_id(1)
    t0 = i * TQ
    width = qt_ref.shape[3]
    nh = width // TQ
    nih = qit_ref.shape[3] // TQ
    krow = lax.broadcasted_iota(i32, (TQ, TQ), 0)
    qcol = lax.broadcasted_iota(i32, (TQ, TQ), 1)
    n_pairs = jnp.maximum(i // 2 + 1, -(-topk // (2 * TQ)))

    qit = qit_ref[0, 0]
    wit = wit_ref[0, 0]

    def score_tile(j):
        kt = ki_ref[0, pl.ds(pl.multiple_of(j * TQ, TQ), TQ), :]
        rel = jnp.maximum(jnp.dot(kt, qit, preferred_element_type=f32), 0.0)
        sc = jnp.zeros((TQ, TQ), f32)
        for h in range(nih):
            sc = sc + wit[h:h + 1, :] * rel[:, h * TQ:(h + 1) * TQ]
        sc = jnp.where(j * TQ + krow <= t0 + qcol, sc, NEG)
        sc = jnp.where(sc == 0.0, 0.0, sc)
        bits = pltpu.bitcast(sc, i32)
        keys_ref[j] = jnp.where(bits < 0, bits ^ 0x7FFFFFFF, bits)

    def score_body(jj, carry):
        score_tile(2 * jj)
        score_tile(2 * jj + 1)
        return carry

    lax.fori_loop(0, n_pairs, score_body, 0)

    def count(pred_fn):
        def body(jj, cnt):
            return (cnt + jnp.where(pred_fn(keys_ref[2 * jj]), 1.0, 0.0)
                    + jnp.where(pred_fn(keys_ref[2 * jj + 1]), 1.0, 0.0))
        cnt = lax.fori_loop(0, n_pairs, body, jnp.zeros((TQ, TQ), f32))
        return jnp.sum(cnt, axis=0, keepdims=True)

    def bits_body(state):
        t, prefix, n_ge = state
        for u in range(RADIX_BITS_PER_CHECK):
            cand = prefix ^ jnp.left_shift(jnp.int32(1), 31 - (t + u))
            cnt = count(lambda kj: kj >= cand)
            prefix = jnp.where(cnt >= topk, cand, prefix)
            n_ge = jnp.where(cnt >= topk, cnt, n_ge)
        return t + RADIX_BITS_PER_CHECK, prefix, n_ge

    n_all = (2 * n_pairs * TQ).astype(f32)
    _, thr, n_ge = lax.while_loop(lambda st: (st[0] < 32) & (jnp.max(st[2]) > topk), bits_body,
                                  (jnp.int32(0), jnp.full((1, TQ), INT_MIN, i32), jnp.full((1, TQ), n_all, f32)))

    @pl.when(jnp.max(n_ge) > topk)
    def _():
        need = topk - count(lambda kj: kj > thr)

        def fix_body(j, running):
            kj = keys_ref[j]
            tie = jnp.where(kj == thr, 1.0, 0.0)
            before = jnp.dot(tri_ref[...], tie.astype(bf16), preferred_element_type=f32) + running
            keys_ref[j] = jnp.where(before >= need, jnp.where(kj == thr, kj - 1, kj), kj)
            return running + jnp.sum(tie, axis=0, keepdims=True)

        lax.fori_loop(0, 2 * n_pairs, fix_body, jnp.zeros((1, TQ), f32))

    qs_ref[0:HEAD_DIM, :] = (qt_ref[0, 0].astype(f32) * (HEAD_DIM ** -0.5 * LOG2E)).astype(bf16)
    qs_ref[HEAD_DIM:K_COLS, :] = qx_ref[...]
    _masked_flash(i, qs_ref,
                  lambda j, c: k_ref[0, pl.ds(pl.multiple_of(j * TQ, TQ), TQ), :],
                  lambda j, c: vt_ref[0, j],
                  lambda j, c: jnp.where(keys_ref[j] >= thr, 0.0, NEG),
                  delta_ref, m_ref, acc_ref, s_ref, mt_ref, al_ref, chunk, True)
    acc = acc_ref[...]
    ot = acc[0:HEAD_DIM] / acc[HEAD_DIM:HEAD_DIM + 1]
    o_ref[0] = jnp.concatenate([ot[:, h * TQ:(h + 1) * TQ].T for h in range(nh)], axis=1).astype(o_ref.dtype)


def _dsa_attention(q, k, v, qi, ki, wi, near, far):
    bn, s, hq = q.shape
    nh = hq // HEAD_DIM
    nih = qi.shape[2] // IDX_DIM
    nt = s // TQ
    topk = min(C_TOPK_MAX, s // 4)
    width = nh * TQ
    qt = q.reshape(bn, nt, TQ, nh, HEAD_DIM).transpose(0, 1, 4, 3, 2).reshape(bn, nt, HEAD_DIM, width)
    qit = qi.reshape(bn, nt, TQ, nih, IDX_DIM).transpose(0, 1, 4, 3, 2).reshape(bn, nt, IDX_DIM, nih * TQ)
    wit = wi.reshape(bn, nt, TQ, nih).transpose(0, 1, 3, 2)
    tri = np.tril(np.ones((TQ, TQ), np.float32), -1)
    kern = functools.partial(_dsa_kernel, topk=topk, chunk=512)
    nt_scr = max(nt, -(-topk // TQ))
    return pl.pallas_call(
        kern,
        grid=(bn, nt),
        in_specs=[pl.BlockSpec((1, 1, HEAD_DIM, width), lambda b, i: (b, i, 0, 0)),
                  pl.BlockSpec((1, s, K_COLS), lambda b, i: (b, 0, 0)),
                  pl.BlockSpec((1, nt, V_ROWS, TQ), lambda b, i: (b, 0, 0, 0)),
                  pl.BlockSpec((1, 1, IDX_DIM, nih * TQ), lambda b, i: (b, i, 0, 0)),
                  pl.BlockSpec((1, s, IDX_DIM), lambda b, i: (b, 0, 0)),
                  pl.BlockSpec((1, 1, nih, TQ), lambda b, i: (b, i, 0, 0)),
                  pl.BlockSpec((K_COLS - HEAD_DIM, width), lambda b, i: (0, 0)),
                  pl.BlockSpec((2, TQ, width), lambda b, i: (0, 0, 0)),
                  pl.BlockSpec((TQ, TQ), lambda b, i: (0, 0))],
        out_specs=pl.BlockSpec((1, TQ, hq), lambda b, i: (b, i, 0)),
        out_shape=jax.ShapeDtypeStruct((bn, s, hq), bf16),
        scratch_shapes=[pltpu.VMEM((nt_scr, TQ, TQ), i32), pltpu.VMEM((K_COLS, width), bf16),
                        pltpu.VMEM((1, width), f32), pltpu.VMEM((V_ROWS, width), f32),
                        pltpu.VMEM((2, TQ, width), f32), pltpu.VMEM((2, 1, width), f32),
                        pltpu.VMEM((2, 1, width), f32)],
        compiler_params=_cparams(("parallel", "arbitrary")),
        name="dsa_attention",
    )(qt, _keys_with_ones(k), _value_tiles(v), qit, ki, wit, *_flash_operands(near, far), jnp.asarray(tri, bf16))


def _heads_first(a, g):
    bn, s, _ = a.shape
    return a.reshape(bn, s, g, HEAD_DIM).transpose(0, 2, 1, 3)


def _even_layer(x, table_t, w_in, sinks, cmpk_pos, cmpk_w1, cmpk_w2, cmpv_pos, cmpv_w1, cmpv_w2,
                w_out, ln1_g, ln1_b, ffn_gate, ffn_up, ffn_down, ln2_g, ln2_b):
    bn, s, d = x.shape
    xf = x.reshape(bn * s, d)
    a_q, a_kv, b_q, b_kv = A_HEADS * HEAD_DIM, A_KV * HEAD_DIM, B_HEADS * HEAD_DIM, B_KV * HEAD_DIM
    n_main = a_q + 2 * a_kv + b_q + 6 * b_kv
    rb = B_HEADS // B_KV
    w_tail = jnp.pad(w_in[:, n_main:], ((0, 0), (0, 128 - 3 * B_HEADS))).astype(bf16)
    main, tail = _proj_in(xf, w_in[:, :n_main].astype(bf16), w_tail)
    main = main.reshape(bn, s, n_main)
    gates = tail[:, :3 * B_HEADS]
    gates = gates.reshape(bn, s, 3, B_KV, rb).transpose(0, 3, 1, 2, 4).reshape(bn, B_KV, s, 3 * rb)
    cols = np.cumsum([0, a_q, a_kv, a_kv, b_q] + [b_kv] * 6)
    qa, ka, va, qb, kc_in, vc_in, ksl, vsl, kwn, vwn = [main[:, :, cols[t]:cols[t + 1]] for t in range(10)]

    near_idx = _bucket_of(np.arange(TQ)[:, None] + TQ - np.arange(2 * TQ)[None, :])
    near = _bias_expand(table_t, near_idx, TQ)
    pad_b = -(-(B_WINDOW - 1) // TQ) * TQ
    win_idx = _bucket_of(np.arange(TQ)[:, None] + pad_b - np.arange(pad_b + TQ)[None, :])
    bias_win = _bias_expand(table_t[A_HEADS:A_HEADS + B_HEADS], win_idx, TQ)
    nc = s // CMP_STRIDE
    nt = s // TQ
    cmp_idx = _bucket_of((np.arange(nt)[:, None, None] * TQ + np.arange(TQ)[None, None, :])
                         - (np.arange(nc)[None, :, None] * CMP_STRIDE + CMP_BLOCK - 1)).reshape(nt * nc, TQ)
    bias_c = _bias_expand(table_t[A_HEADS:A_HEADS + B_HEADS], cmp_idx, min(nt * nc, 512))
    bias_c = bias_c.reshape(B_HEADS, nt, nc, TQ)
    far = table_t[:, NUM_BUCKETS - 1]

    qt_a = _queries_on_lanes(qa, A_KV)
    qt_b = _queries_on_lanes(qb, B_KV)
    out_a = _banded_gqa(qt_a, _heads_first(ka, A_KV), _heads_first(va, A_KV), near[:A_HEADS], sinks, A_WINDOW, bf16)
    o_win = _banded_gqa(qt_b, _heads_first(kwn, B_KV), _heads_first(vwn, B_KV), bias_win, None, B_WINDOW, f32)
    kc = _nsa_compress(_heads_first(kc_in, B_KV), cmpk_pos, cmpk_w1, cmpk_w2)
    vc = _nsa_compress(_heads_first(vc_in, B_KV), cmpv_pos, cmpv_w1, cmpv_w2)
    out_b = _nsa_mix(qt_b, kc, vc, bias_c, _heads_first(ksl, B_KV), _heads_first(vsl, B_KV),
                     near[A_HEADS:A_HEADS + B_HEADS], far[A_HEADS:A_HEADS + B_HEADS], gates, o_win)
    mix = jnp.concatenate([out_a, out_b], axis=-1).reshape(bn * s, a_q + b_q)
    x1 = _proj_res_ln(mix, w_out.astype(bf16), xf, ln1_g, ln1_b)
    x2 = _ffn_res_ln(x1, ffn_gate.astype(bf16), ffn_up.astype(bf16), ffn_down.astype(bf16), ln2_g, ln2_b)
    return x2.reshape(bn, s, d), near, far


def _odd_layer(x, near, far, w_in, w_out, ln1_g, ln1_b, router, exp_gate, exp_up, exp_down, ln2_g, ln2_b):
    bn, s, d = x.shape
    xf = x.reshape(bn * s, d)
    c_q = C_HEADS * HEAD_DIM
    n_main = c_q + 2 * HEAD_DIM + IDX_HEADS * IDX_DIM
    w_tail = jnp.pad(w_in[:, n_main:], ((0, 0), (0, 128 - IDX_DIM - IDX_HEADS))).astype(bf16)
    main, tail = _proj_in(xf, w_in[:, :n_main].astype(bf16), w_tail)
    main = main.reshape(bn, s, n_main)
    tail = tail.reshape(bn, s, 128)
    q = main[:, :, :c_q]
    k = main[:, :, c_q:c_q + HEAD_DIM]
    v = main[:, :, c_q + HEAD_DIM:c_q + 2 * HEAD_DIM]
    qi = main[:, :, c_q + 2 * HEAD_DIM:]
    ki = tail[:, :, :IDX_DIM].astype(bf16)
    wi = tail[:, :, IDX_DIM:IDX_DIM + IDX_HEADS]
    mix = _dsa_attention(q, k, v, qi, ki, wi, near[:C_HEADS], far[:C_HEADS]).reshape(bn * s, c_q)
    x1 = _proj_res_ln(mix, w_out.astype(bf16), xf, ln1_g, ln1_b)
    x2 = _moe_res_ln(x1, router, exp_gate.astype(bf16), exp_up.astype(bf16), exp_down.astype(bf16), ln2_g, ln2_b)
    return x2.reshape(bn, s, d)


def kernel(x, rel_bias, l0_w_in, l0_sinks, l0_cmpk_pos, l0_cmpk_w1, l0_cmpk_w2, l0_cmpv_pos, l0_cmpv_w1,
           l0_cmpv_w2, l0_w_out, l0_ln1_g, l0_ln1_b, l0_ffn_gate, l0_ffn_up, l0_ffn_down, l0_ln2_g, l0_ln2_b,
           l1_w_in, l1_w_out, l1_ln1_g, l1_ln1_b, l1_router, l1_exp_gate, l1_exp_up, l1_exp_down, l1_ln2_g,
           l1_ln2_b):
    table_t = rel_bias.T.astype(f32)
    x, near, far = _even_layer(x, table_t, l0_w_in, l0_sinks, l0_cmpk_pos, l0_cmpk_w1, l0_cmpk_w2, l0_cmpv_pos,
                               l0_cmpv_w1, l0_cmpv_w2, l0_w_out, l0_ln1_g, l0_ln1_b, l0_ffn_gate, l0_ffn_up,
                               l0_ffn_down, l0_ln2_g, l0_ln2_b)
    return _odd_layer(x, near, far, l1_w_in, l1_w_out, l1_ln1_g, l1_ln1_b, l1_router, l1_exp_gate, l1_exp_up,
                      l1_exp_down, l1_ln2_g, l1_ln2_b)
```

```python
import functools
import math

import numpy as np
import jax
import jax.numpy as jnp
from jax import lax
from jax.experimental import pallas as pl
from jax.experimental.pallas import tpu as pltpu

f32 = jnp.float32
bf16 = jnp.bfloat16
i32 = jnp.int32

D_MODEL = 1024
HEAD_DIM = 64
NUM_BUCKETS = 32
MAX_DISTANCE = 128
A_HEADS, A_KV, A_WINDOW = 8, 2, 128
B_HEADS, B_KV, B_WINDOW = 8, 2, 512
CMP_BLOCK, CMP_STRIDE = 32, 16
SLC_BLOCK, SLC_TOPN = 64, 16
C_HEADS, IDX_HEADS, IDX_DIM, C_TOPK_MAX = 16, 8, 32, 256
D_FF, N_EXPERTS = 2816, 8
DEPTH = 2
ALPHA = (2.0 * DEPTH) ** 0.25
LN_EPS = 1e-5
NEG = -1e30
POS_BIG = 1e30
INT_MIN = -(2 ** 31)

TQ = 128
FF_CHUNK = 256
VMEM_LIMIT = 56 * 1024 * 1024


def _cparams(sem):
    return pltpu.CompilerParams(dimension_semantics=sem, vmem_limit_bytes=VMEM_LIMIT)


def _bucket_table(max_dist):
    n = np.arange(max_dist + 1)
    max_exact = NUM_BUCKETS // 2
    nf = np.maximum(n, 1).astype(np.float64)
    large = max_exact + (np.log(nf / max_exact) / math.log(MAX_DISTANCE / max_exact)
                         * (NUM_BUCKETS - max_exact)).astype(np.int64)
    large = np.minimum(large, NUM_BUCKETS - 1)
    return np.where(n < max_exact, n, large).astype(np.int32)


def _bucket_of(dist):
    dist = np.maximum(dist, 0)
    return _bucket_table(int(dist.max()))[dist]


def _proj_in_kernel(*refs, groups):
    ng = len(groups)
    x_ref, wm_ref, wt_ref = refs[:3]
    wq_refs = refs[3:3 + ng]
    om_ref, ot_ref = refs[3 + ng:5 + ng]
    oq_refs = refs[5 + ng:]
    xb = x_ref[...].astype(bf16)
    om_ref[...] = jnp.dot(xb, wm_ref[...], preferred_element_type=f32).astype(om_ref.dtype)
    ot_ref[...] = jnp.dot(xb, wt_ref[...], preferred_element_type=f32)
    for wq_ref, oq_ref, (gb, hb, dh) in zip(wq_refs, oq_refs, groups):
        qt = lax.dot_general(wq_ref[...], xb, (((1,), (1,)), ((), ())), preferred_element_type=f32).astype(bf16)
        for g in range(gb):
            for h in range(hb):
                rows = slice((g * hb + h) * dh, (g * hb + h + 1) * dh)
                for t in range(xb.shape[0] // TQ):
                    oq_ref[0, g, t, :, h * TQ:(h + 1) * TQ] = qt[rows, t * TQ:(t + 1) * TQ]


def _proj_in(x3, w_main, w_tail, w_queries, groups, tm=512):
    bn, s, k = x3.shape
    m = bn * s
    tm = min(tm, s)
    nm, ntl = w_main.shape[1], w_tail.shape[1]
    per_b = s // tm
    tiles = tm // TQ
    q_specs = [pl.BlockSpec((1, gb, tiles, dh, hb * TQ), lambda i: (i // per_b, 0, i % per_b, 0, 0))
               for gb, hb, dh in groups]
    q_shapes = [jax.ShapeDtypeStruct((bn, gb, s // TQ, dh, hb * TQ), bf16) for gb, hb, dh in groups]
    return pl.pallas_call(
        functools.partial(_proj_in_kernel, groups=tuple(groups)),
        grid=(m // tm,),
        in_specs=[pl.BlockSpec((tm, k), lambda i: (i, 0)),
                  pl.BlockSpec((k, nm), lambda i: (0, 0)),
                  pl.BlockSpec((k, ntl), lambda i: (0, 0))]
                 + [pl.BlockSpec(w.shape, lambda i: (0, 0)) for w in w_queries],
        out_specs=[pl.BlockSpec((tm, nm), lambda i: (i, 0)), pl.BlockSpec((tm, ntl), lambda i: (i, 0))] + q_specs,
        out_shape=[jax.ShapeDtypeStruct((m, nm), bf16), jax.ShapeDtypeStruct((m, ntl), f32)] + q_shapes,
        compiler_params=_cparams(("parallel",)),
        name="proj_in",
    )(x3.reshape(m, k), w_main, w_tail, *w_queries)


def _bias_expand_kernel(tab_ref, idx_ref, o_ref):
    h = pl.program_id(0)
    idx = idx_ref[...]
    acc = jnp.zeros(idx.shape, f32)
    for b in range(NUM_BUCKETS):
        acc = jnp.where(idx == b, tab_ref[h, b], acc)
    o_ref[0] = acc


def _bias_expand(table_t, idx, tr):
    nh = table_t.shape[0]
    r, c = idx.shape
    return pl.pallas_call(
        _bias_expand_kernel,
        grid=(nh, r // tr),
        in_specs=[pl.BlockSpec(memory_space=pltpu.SMEM),
                  pl.BlockSpec((tr, c), lambda h, i: (i, 0))],
        out_specs=pl.BlockSpec((1, tr, c), lambda h, i: (h, i, 0)),
        out_shape=jax.ShapeDtypeStruct((nh, r, c), f32),
        compiler_params=_cparams(("parallel", "arbitrary")),
        name="bias_expand",
    )(table_t, jnp.asarray(idx, dtype=i32))


def _layer_norm(z, g, b):
    mu = jnp.mean(z, axis=-1, keepdims=True)
    zc = z - mu
    var = jnp.mean(zc * zc, axis=-1, keepdims=True)
    return zc * lax.rsqrt(var + LN_EPS) * g + b


def _proj_ln_kernel(mix_ref, w_ref, x_ref, g_ref, b_ref, o_ref):
    y = jnp.dot(mix_ref[...], w_ref[...], preferred_element_type=f32)
    o_ref[...] = _layer_norm(ALPHA * x_ref[...] + y, g_ref[...], b_ref[...])


def _proj_res_ln(mix, w, x, g, b, tm=512):
    m, k = mix.shape
    d = w.shape[1]
    return pl.pallas_call(
        _proj_ln_kernel,
        grid=(m // tm,),
        in_specs=[pl.BlockSpec((tm, k), lambda i: (i, 0)),
                  pl.BlockSpec((k, d), lambda i: (0, 0)),
                  pl.BlockSpec((tm, d), lambda i: (i, 0)),
                  pl.BlockSpec((1, d), lambda i: (0, 0)),
                  pl.BlockSpec((1, d), lambda i: (0, 0))],
        out_specs=pl.BlockSpec((tm, d), lambda i: (i, 0)),
        out_shape=jax.ShapeDtypeStruct((m, d), f32),
        compiler_params=_cparams(("parallel",)),
        name="proj_res_ln",
    )(mix, w, x, g.reshape(1, d), b.reshape(1, d))


def _swiglu_accumulate(xb, wg_ref, wu_ref, wd_ref, acc_ref, widx):
    d_ff = wg_ref.shape[-1]
    for c in range(d_ff // FF_CHUNK):
        cols = slice(c * FF_CHUNK, (c + 1) * FF_CHUNK)
        gate = jnp.dot(xb, wg_ref[widx + (slice(None), cols)], preferred_element_type=f32)
        up = jnp.dot(xb, wu_ref[widx + (slice(None), cols)], preferred_element_type=f32)
        h = jax.nn.silu(gate) * up
        acc_ref[...] += jnp.dot(h.astype(bf16), wd_ref[widx + (cols, slice(None))], preferred_element_type=f32)


def _ffn_kernel(x_ref, wg_ref, wu_ref, wd_ref, g_ref, b_ref, o_ref, acc_ref):
    x = x_ref[...]
    acc_ref[...] = jnp.zeros_like(acc_ref)
    _swiglu_accumulate(x.astype(bf16), wg_ref, wu_ref, wd_ref, acc_ref, ())
    o_ref[...] = _layer_norm(ALPHA * x + acc_ref[...], g_ref[...], b_ref[...])


def _ffn_res_ln(x, wg, wu, wd, g, b, tm=512):
    m, d = x.shape
    d_ff = wg.shape[1]
    return pl.pallas_call(
        _ffn_kernel,
        grid=(m // tm,),
        in_specs=[pl.BlockSpec((tm, d), lambda i: (i, 0)),
                  pl.BlockSpec((d, d_ff), lambda i: (0, 0)),
                  pl.BlockSpec((d, d_ff), lambda i: (0, 0)),
                  pl.BlockSpec((d_ff, d), lambda i: (0, 0)),
                  pl.BlockSpec((1, d), lambda i: (0, 0)),
                  pl.BlockSpec((1, d), lambda i: (0, 0))],
        out_specs=pl.BlockSpec((tm, d), lambda i: (i, 0)),
        out_shape=jax.ShapeDtypeStruct((m, d), f32),
        scratch_shapes=[pltpu.VMEM((tm, d), f32)],
        compiler_params=_cparams(("parallel",)),
        name="ffn_res_ln",
    )(x, wg, wu, wd, g.reshape(1, d), b.reshape(1, d))


MOE_CAP = 288


def _router_gates(x, r_ref, lane):
    logits = jnp.dot(x, r_ref[...], preferred_element_type=f32)
    logits = jnp.where(lane < N_EXPERTS, logits, -jnp.inf)
    m1 = jnp.max(logits, axis=-1, keepdims=True)
    i1 = jnp.min(jnp.where(logits == m1, lane, 128), axis=-1, keepdims=True)
    rest = jnp.where(lane == i1, -jnp.inf, logits)
    m2 = jnp.max(rest, axis=-1, keepdims=True)
    i2 = jnp.min(jnp.where(rest == m2, lane, 128), axis=-1, keepdims=True)
    e2 = jnp.exp(m2 - m1)
    g1 = 1.0 / (1.0 + e2)
    return jnp.where(lane == i1, g1, 0.0) + jnp.where(lane == i2, e2 * g1, 0.0)


def _moe_kernel(x_ref, r_ref, tri_ref, wg_ref, wu_ref, wd_ref, g_ref, b_ref, o_ref,
                split_ref, post_ref, yc_ref):
    e = pl.program_id(1)
    tm = x_ref.shape[0]

    @pl.when(e == 0)
    def _():
        comb = _router_gates(x_ref[...], r_ref, lax.broadcasted_iota(i32, (tm, 128), 1))
        hi = comb.astype(bf16)
        rest = comb - hi.astype(f32)
        mid = rest.astype(bf16)
        split_ref[0] = hi
        split_ref[1] = mid
        split_ref[2] = (rest - mid.astype(f32)).astype(bf16)
        chosen = jnp.where(comb > 0.0, 1.0, 0.0)
        pos = jnp.dot(tri_ref[...], chosen.astype(bf16), preferred_element_type=f32)
        pos = jnp.where(comb > 0.0, pos, -1.0)
        for t in range(tm // 128):
            rows = slice(t * 128, (t + 1) * 128)
            post_ref[:, rows] = pos[rows].T[0:N_EXPERTS]
        hi32 = jnp.floor((pos + 1.0) * (1.0 / 32.0))
        split_ref[3] = hi32.astype(bf16)
        split_ref[4] = (pos + 1.0 - 32.0 * hi32).astype(bf16)
        o_ref[...] = jnp.zeros_like(o_ref)

    pos_row = post_ref[pl.ds(e, 1), :].astype(i32)
    n_tok = jnp.sum(jnp.where(pos_row >= 0, 1.0, 0.0)).astype(i32)
    pick = jnp.where(lax.broadcasted_iota(i32, (128, 128), 0) == e, 1.0, 0.0).astype(bf16)

    def column(t):
        return jnp.dot(split_ref[t], pick, preferred_element_type=f32)

    gate_b = column(0) + column(1) + column(2)
    pos_b = (32.0 * column(3) + column(4) - 1.0).astype(i32)
    lane = lax.broadcasted_iota(i32, (tm, 128), 1)

    def pass_body(p, carry):
        base = p * MOE_CAP
        gather = jnp.where(pos_row == base + lax.broadcasted_iota(i32, (MOE_CAP, tm), 0), 1.0, 0.0).astype(bf16)
        xc = jnp.dot(gather, x_ref[...].astype(bf16), preferred_element_type=f32).astype(bf16)
        yc_ref[...] = jnp.zeros_like(yc_ref)
        _swiglu_accumulate(xc, wg_ref, wu_ref, wd_ref, yc_ref, (0,))
        scatter = jnp.concatenate([jnp.where(pos_b == base + t * 128 + lane, 1.0, 0.0).astype(bf16)
                                   for t in range(-(-MOE_CAP // 128))], axis=1)[:, :MOE_CAP]
        y = jnp.dot(scatter, yc_ref[...].astype(bf16), preferred_element_type=f32)
        o_ref[...] += jnp.concatenate([gate_b] * (y.shape[1] // 128), axis=1) * y
        return carry

    lax.fori_loop(0, (n_tok + MOE_CAP - 1) // MOE_CAP, pass_body, 0)

    @pl.when(e == N_EXPERTS - 1)
    def _():
        o_ref[...] = _layer_norm(ALPHA * x_ref[...] + o_ref[...], g_ref[...], b_ref[...])


def _moe_res_ln(x, router, wg, wu, wd, g, b, tm=1024):
    m, d = x.shape
    ne, _, d_ff = wg.shape
    tm = min(tm, m)
    router_p = jnp.pad(router, ((0, 0), (0, 128 - ne)))
    tri = jnp.asarray(np.tril(np.ones((tm, tm), np.float32), -1), bf16)
    single = pl.Buffered(1)
    return pl.pallas_call(
        _moe_kernel,
        grid=(m // tm, ne),
        in_specs=[pl.BlockSpec((tm, d), lambda i, e: (i, 0), pipeline_mode=single),
                  pl.BlockSpec((d, 128), lambda i, e: (0, 0), pipeline_mode=single),
                  pl.BlockSpec((tm, tm), lambda i, e: (0, 0), pipeline_mode=single),
                  pl.BlockSpec((1, d, d_ff), lambda i, e: (e, 0, 0)),
                  pl.BlockSpec((1, d, d_ff), lambda i, e: (e, 0, 0)),
                  pl.BlockSpec((1, d_ff, d), lambda i, e: (e, 0, 0)),
                  pl.BlockSpec((1, d), lambda i, e: (0, 0)),
                  pl.BlockSpec((1, d), lambda i, e: (0, 0))],
        out_specs=pl.BlockSpec((tm, d), lambda i, e: (i, 0)),
        out_shape=jax.ShapeDtypeStruct((m, d), f32),
        scratch_shapes=[pltpu.VMEM((5, tm, 128), bf16), pltpu.VMEM((N_EXPERTS, tm), f32),
                        pltpu.VMEM((MOE_CAP, d), f32)],
        compiler_params=_cparams(("parallel", "arbitrary")),
        name="moe_res_ln",
    )(x, router_p, tri, wg, wu, wd, g.reshape(1, d), b.reshape(1, d))


def _compress_kernel(blk_ref, pos_ref, w1_ref, w2_ref, o_ref):
    xb = (blk_ref[0].astype(f32) + pos_ref[...]).astype(bf16)
    h = jax.nn.gelu(jnp.dot(xb, w1_ref[...], preferred_element_type=f32))
    o_ref[0] = jnp.dot(h.astype(bf16), w2_ref[...], preferred_element_type=f32).astype(o_ref.dtype)


def _nsa_compress(kv, pos, w1, w2):
    bn, g, s, dh = kv.shape
    nc = s // CMP_STRIDE
    chunks = kv.reshape(bn, g, nc, CMP_STRIDE * dh)
    nxt = jnp.pad(chunks[:, :, 1:], ((0, 0), (0, 0), (0, 1), (0, 0)))
    blocks = jnp.concatenate([chunks, nxt], axis=-1).reshape(bn * g, nc, CMP_BLOCK * dh)
    cin = CMP_BLOCK * dh
    hid = w1.shape[1]
    out = pl.pallas_call(
        _compress_kernel,
        grid=(bn * g,),
        in_specs=[pl.BlockSpec((1, nc, cin), lambda i: (i, 0, 0)),
                  pl.BlockSpec((1, cin), lambda i: (0, 0)),
                  pl.BlockSpec((cin, hid), lambda i: (0, 0)),
                  pl.BlockSpec((hid, dh), lambda i: (0, 0))],
        out_specs=pl.BlockSpec((1, nc, dh), lambda i: (i, 0, 0)),
        out_shape=jax.ShapeDtypeStruct((bn * g, nc, dh), bf16),
        compiler_params=_cparams(("parallel",)),
        name="nsa_compress",
    )(blocks, pos.reshape(1, cin), w1.astype(bf16), w2.astype(bf16))
    return out.reshape(bn, g, nc, dh)


V_ROWS = HEAD_DIM + 16


LOG2E = 1.4426950408889634
K_COLS = 2 * HEAD_DIM


def _masked_flash(i, qs_ref, key_tile, value_tile, mask_bias, delta_ref, m_ref, acc_ref, s_ref, mt_ref, al_ref, chunk,
                  shared_keys):
    width = qs_ref.shape[1]
    n_chunks = width // chunk
    m_ref[...] = jnp.full(m_ref.shape, NEG, f32)
    acc_ref[...] = jnp.zeros(acc_ref.shape, f32)

    def scores_chunk(j, delta, c, shared):
        slot = j % 2
        if c == 0 or not shared_keys:
            shared["kt"] = key_tile(j, c)
            shared["maskb"] = jnp.concatenate([mask_bias(j, c)] * (chunk // TQ), axis=1)
        cols = slice(c * chunk, (c + 1) * chunk)
        s = jnp.dot(shared["kt"], qs_ref[:, cols], preferred_element_type=f32) + shared["maskb"]
        if delta is not None:
            s = s + delta_ref[delta, :, cols]
        s_ref[slot, :, cols] = s
        m_old = m_ref[:, cols]
        m_new = jnp.maximum(m_old, jnp.max(s, axis=0, keepdims=True))
        m_ref[:, cols] = m_new
        mt_ref[slot, :, cols] = m_new
        al_ref[slot, :, cols] = jnp.exp2(m_old - m_new)

    def values_chunk(j, c, shared):
        slot = j % 2
        if c == 0 or not shared_keys:
            shared["vt"] = value_tile(j, c)
        cols = slice(c * chunk, (c + 1) * chunk)
        p = jnp.exp2(s_ref[slot, :, cols] - mt_ref[slot, :, cols])
        acc_ref[:, cols] = (al_ref[slot, :, cols] * acc_ref[:, cols]
                            + jnp.dot(shared["vt"], p.astype(bf16), preferred_element_type=f32))

    def scores(j, delta):
        shared = {}
        for c in range(n_chunks):
            scores_chunk(j, delta, c, shared)

    def values(j):
        shared = {}
        for c in range(n_chunks):
            values_chunk(j, c, shared)

    def values_then_scores(jv, js, delta):
        sv, ss = {}, {}
        for c in range(n_chunks):
            values_chunk(jv, c, sv)
            scores_chunk(js, delta, c, ss)

    @pl.when(i >= 2)
    def _():
        scores(0, None)

        def pipe_body(j, carry):
            values_then_scores(j - 1, j, None)
            return carry

        lax.fori_loop(1, i - 1, pipe_body, 0)
        values_then_scores(i - 2, i - 1, 0)
        values_then_scores(i - 1, i, 1)

    @pl.when(i == 1)
    def _():
        scores(0, 0)
        values_then_scores(0, 1, 1)

    @pl.when(i == 0)
    def _():
        scores(0, 1)

    values(i)


def _flash_operands(near, far):
    nh = near.shape[0]
    width = nh * TQ
    far2 = jnp.repeat(far * LOG2E, TQ).reshape(1, width)
    hi = far2.astype(bf16)
    lo = (far2 - hi.astype(f32)).astype(bf16)
    extra = jnp.concatenate([hi, lo, jnp.zeros((K_COLS - HEAD_DIM - 2, width), bf16)], axis=0)
    neart = near.reshape(nh, TQ, 2, TQ).transpose(2, 3, 0, 1).reshape(2, TQ, width) * LOG2E
    neart = neart - (hi.astype(f32) + lo.astype(f32))
    causal = np.where(np.arange(TQ)[:, None] <= np.arange(TQ)[None, :], 0.0, NEG).astype(np.float32)
    return extra, jnp.stack([neart[0], neart[1] + jnp.asarray(np.tile(causal, (1, nh)))], axis=0)


def _keys_with_ones(k):
    lead = k.shape[:-1]
    return jnp.concatenate([k, jnp.ones(lead + (2,), k.dtype), jnp.zeros(lead + (K_COLS - HEAD_DIM - 2,), k.dtype)],
                           axis=-1)


def _value_tiles(v):
    lead = v.shape[:-2]
    nt = v.shape[-2] // TQ
    vt = jnp.swapaxes(v.reshape(lead + (nt, TQ, HEAD_DIM)), -1, -2)
    return jnp.concatenate([vt, jnp.ones(lead + (nt, 1, TQ), vt.dtype),
                            jnp.zeros(lead + (nt, V_ROWS - HEAD_DIM - 1, TQ), vt.dtype)], axis=-2)


def _banded_kernel(qt_ref, k_ref, vt_ref, bias_ref, neg_ref, sink_ref, o_ref, *, rep, kt_tiles, use_sinks):
    i = pl.program_id(1)
    kw = kt_tiles * TQ
    outs = []
    for g in range(qt_ref.shape[1]):
        qs = jnp.concatenate([(qt_ref[0, g, 0].astype(f32) * (HEAD_DIM ** -0.5)).astype(bf16), neg_ref[...]], axis=0)
        k = k_ref[0, g, pl.ds(pl.multiple_of(i * TQ, TQ), kw), :]
        s = jnp.dot(k, qs, preferred_element_type=f32) + bias_ref[g]
        m = jnp.max(s, axis=0, keepdims=True)
        if use_sinks:
            m = jnp.maximum(m, sink_ref[g])
        p = jnp.exp(s - m).astype(bf16)
        vt = jnp.concatenate([vt_ref[0, g, i + t] for t in range(kt_tiles)], axis=1)
        acc = jnp.dot(vt, p, preferred_element_type=f32)
        l = acc[HEAD_DIM:HEAD_DIM + 1]
        if use_sinks:
            l = l + jnp.exp(sink_ref[g] - m)
        ot = acc[0:HEAD_DIM] / l
        outs += [ot[:, r * TQ:(r + 1) * TQ].T for r in range(rep)]
    o_ref[0] = jnp.concatenate(outs, axis=1).astype(o_ref.dtype)


def _banded_gqa(qt, k, v, bias, sinks, window, out_dtype):
    bn, g, nt, _, width = qt.shape
    rep = width // TQ
    s = nt * TQ
    pad = -(-(window - 1) // TQ) * TQ
    kw = pad + TQ
    kd = 2 * HEAD_DIM
    flag = jnp.broadcast_to((jnp.arange(pad + s) < pad).astype(k.dtype)[:, None], (bn, g, pad + s, 1))
    kp = jnp.concatenate([jnp.pad(k, ((0, 0), (0, 0), (pad, 0), (0, 0))), flag,
                          jnp.zeros((bn, g, pad + s, kd - HEAD_DIM - 1), k.dtype)], axis=-1)
    vt = _value_tiles(jnp.pad(v, ((0, 0), (0, 0), (pad, 0), (0, 0))))
    neg = np.zeros((kd - HEAD_DIM, width), np.float32)
    neg[0] = NEG
    dist = np.arange(TQ)[:, None] + pad - np.arange(kw)[None, :]
    band = jnp.asarray((dist >= 0) & (dist < window))
    bias_t = jnp.where(band, bias, NEG).reshape(g, rep, TQ, kw).transpose(0, 3, 1, 2).reshape(g, kw, width)
    use_sinks = sinks is not None
    sink_rows = (jnp.repeat(sinks.astype(f32), TQ) if use_sinks else jnp.zeros((g * width,), f32)).reshape(g, 1, width)
    kern = functools.partial(_banded_kernel, rep=rep, kt_tiles=kw // TQ, use_sinks=use_sinks)
    rw = rep * HEAD_DIM
    return pl.pallas_call(
        kern,
        grid=(bn, nt),
        in_specs=[pl.BlockSpec((1, g, 1, HEAD_DIM, width), lambda b, i: (b, 0, i, 0, 0)),
                  pl.BlockSpec((1, g, pad + s, kd), lambda b, i: (b, 0, 0, 0)),
                  pl.BlockSpec((1, g, (pad + s) // TQ, V_ROWS, TQ), lambda b, i: (b, 0, 0, 0, 0)),
                  pl.BlockSpec((g, kw, width), lambda b, i: (0, 0, 0)),
                  pl.BlockSpec((kd - HEAD_DIM, width), lambda b, i: (0, 0)),
                  pl.BlockSpec((g, 1, width), lambda b, i: (0, 0, 0))],
        out_specs=pl.BlockSpec((1, TQ, g * rw), lambda b, i: (b, i, 0)),
        out_shape=jax.ShapeDtypeStruct((bn, s, g * rw), out_dtype),
        compiler_params=_cparams(("parallel", "arbitrary")),
        name="banded_gqa_w%d" % window,
    )(qt, kp, vt, bias_t, jnp.asarray(neg, bf16), sink_rows)


def _nsa_kernel(qt_ref, kc_ref, vct_ref, biasc_ref, ovt_ref, ks_ref, vst_ref, qx_ref, delta_ref, gt_ref, gn_ref,
                owin_ref, o_ref, qs_ref, selb_ref, m_ref, acc_ref, s_ref, mt_ref, al_ref, *, rep, n_sel):
    i = pl.program_id(1)
    t0 = i * TQ
    groups = qt_ref.shape[1]
    nc = kc_ref.shape[2]
    nblk = ovt_ref.shape[0]
    width = rep * TQ
    cmp_end = lax.broadcasted_iota(i32, (nc, TQ), 0) * CMP_STRIDE + (CMP_BLOCK - 1)
    valid_c = t0 + lax.broadcasted_iota(i32, (nc, TQ), 1) >= cmp_end
    blk = lax.broadcasted_iota(i32, (nblk, TQ), 0)
    tq = t0 + lax.broadcasted_iota(i32, (nblk, TQ), 1)
    cur = tq // SLC_BLOCK
    forced = (blk == 0) | (blk == cur) | (blk == cur - 1)
    admitted = blk * SLC_BLOCK <= tq

    o_cmps = []
    for g in range(groups):
        qf = qt_ref[0, g, 0].astype(f32)
        qs = (qf * (HEAD_DIM ** -0.5)).astype(bf16)
        qs_ref[0:HEAD_DIM, g * width:(g + 1) * width] = (qf * (HEAD_DIM ** -0.5 * LOG2E)).astype(bf16)

        s_c = jnp.dot(kc_ref[0, g], qs, preferred_element_type=f32)
        p_cols = []
        p_sum = jnp.zeros((nc, TQ), f32)
        for r in range(rep):
            sr = jnp.where(valid_c, s_c[:, r * TQ:(r + 1) * TQ] + biasc_ref[g * rep + r, 0], NEG)
            m = jnp.max(sr, axis=0, keepdims=True)
            e = jnp.where(valid_c, jnp.exp(sr - m), 0.0)
            l = jnp.sum(e, axis=0, keepdims=True)
            p = e * (1.0 / jnp.maximum(l, 1e-30))
            p_cols.append(p.astype(bf16))
            p_sum = p_sum + p
        o_cmps.append(jnp.dot(vct_ref[0, g], jnp.concatenate(p_cols, axis=1), preferred_element_type=f32))

        imp = jnp.dot(ovt_ref[...], p_sum, preferred_element_type=f32)
        score = jnp.where(forced, POS_BIG, jnp.where(admitted, imp, NEG))
        rank = jnp.zeros((nblk, TQ), i32)
        for kk in range(nblk):
            ck = score[kk:kk + 1, :]
            rank = rank + jnp.where(ck > score, 1, jnp.where(ck == score, jnp.where(blk > kk, 1, 0), 0))
        selb_ref[g] = jnp.where(rank < n_sel, 0.0, NEG)

    per_tile = TQ // SLC_BLOCK

    def mask_bias(j, c):
        rows = [jnp.broadcast_to(selb_ref[c, pl.ds(per_tile * j + t, 1), :], (SLC_BLOCK, TQ))
                for t in range(per_tile)]
        return jnp.concatenate(rows, axis=0)

    qs_ref[HEAD_DIM:K_COLS, :] = qx_ref[...]
    _masked_flash(i, qs_ref,
                  lambda j, c: ks_ref[0, c, pl.ds(pl.multiple_of(j * TQ, TQ), TQ), :],
                  lambda j, c: vst_ref[0, c, j],
                  mask_bias, delta_ref, m_ref, acc_ref, s_ref, mt_ref, al_ref, width, False)
    acc = acc_ref[...]
    o_slc = acc[0:HEAD_DIM] / acc[HEAD_DIM:HEAD_DIM + 1]

    outs = []
    for g in range(groups):
        sig_t = jax.nn.sigmoid(gt_ref[0, g, 0])
        sig_n = jax.nn.sigmoid(gn_ref[0, g])
        for r in range(rep):
            h = g * rep + r
            mixed = (sig_t[r:r + 1] * o_cmps[g][:, r * TQ:(r + 1) * TQ]
                     + sig_t[rep + r:rep + r + 1] * o_slc[:, h * TQ:(h + 1) * TQ])
            o_win = owin_ref[0, :, h * HEAD_DIM:(h + 1) * HEAD_DIM]
            outs.append(mixed.T + sig_n[:, 2 * rep + r:2 * rep + r + 1] * o_win)
    o_ref[0] = jnp.concatenate(outs, axis=1).astype(o_ref.dtype)


def _nsa_mix(qt, kc, vc, bias_c, ks, vs, near, far, gates, o_win):
    bn, g, nt, _, width = qt.shape
    s = nt * TQ
    nc = kc.shape[2]
    rep = width // TQ
    hq = g * rep * HEAD_DIM
    nblk = s // SLC_BLOCK
    n_sel = min(SLC_TOPN, nblk)
    ci = np.arange(nc)[:, None] * CMP_STRIDE
    sj = np.arange(nblk)[None, :] * SLC_BLOCK
    overlap = np.clip(np.minimum(ci + CMP_BLOCK, sj + SLC_BLOCK) - np.maximum(ci, sj), 0, None) / CMP_BLOCK
    overlap[nc - 1] = 0.0
    gates_t = gates.reshape(bn, g, nt, TQ, 3 * rep).transpose(0, 1, 2, 4, 3)
    kern = functools.partial(_nsa_kernel, rep=rep, n_sel=n_sel)
    return pl.pallas_call(
        kern,
        grid=(bn, nt),
        in_specs=[pl.BlockSpec((1, g, 1, HEAD_DIM, width), lambda b, i: (b, 0, i, 0, 0)),
                  pl.BlockSpec((1, g, nc, HEAD_DIM), lambda b, i: (b, 0, 0, 0)),
                  pl.BlockSpec((1, g, HEAD_DIM, nc), lambda b, i: (b, 0, 0, 0)),
                  pl.BlockSpec((g * rep, 1, nc, TQ), lambda b, i: (0, i, 0, 0)),
                  pl.BlockSpec((nblk, nc), lambda b, i: (0, 0)),
                  pl.BlockSpec((1, g, s, K_COLS), lambda b, i: (b, 0, 0, 0)),
                  pl.BlockSpec((1, g, nt, V_ROWS, TQ), lambda b, i: (b, 0, 0, 0, 0)),
                  pl.BlockSpec((K_COLS - HEAD_DIM, g * width), lambda b, i: (0, 0)),
                  pl.BlockSpec((2, TQ, g * width), lambda b, i: (0, 0, 0)),
                  pl.BlockSpec((1, g, 1, 3 * rep, TQ), lambda b, i: (b, 0, i, 0, 0)),
                  pl.BlockSpec((1, g, TQ, 3 * rep), lambda b, i: (b, 0, i, 0)),
                  pl.BlockSpec((1, TQ, hq), lambda b, i: (b, i, 0))],
        out_specs=pl.BlockSpec((1, TQ, hq), lambda b, i: (b, i, 0)),
        out_shape=jax.ShapeDtypeStruct((bn, s, hq), bf16),
        scratch_shapes=[pltpu.VMEM((K_COLS, g * width), bf16), pltpu.VMEM((g, nblk, TQ), f32),
                        pltpu.VMEM((1, g * width), f32), pltpu.VMEM((V_ROWS, g * width), f32),
                        pltpu.VMEM((2, TQ, g * width), f32), pltpu.VMEM((2, 1, g * width), f32),
                        pltpu.VMEM((2, 1, g * width), f32)],
        compiler_params=_cparams(("parallel", "arbitrary")),
        name="nsa_mix",
    )(qt, kc, jnp.swapaxes(vc, 2, 3), bias_c, jnp.asarray(overlap.T, f32), _keys_with_ones(ks), _value_tiles(vs),
      *_flash_operands(near, far), gates_t, gates, o_win)


RADIX_BITS_PER_CHECK = 8


def _dsa_kernel(qt_ref, k_ref, vt_ref, qit_ref, ki_ref, wit_ref, qx_ref, delta_ref, tri_ref, o_ref,
                keys_ref, qs_ref, m_ref, acc_ref, s_ref, mt_ref, al_ref, *, topk, chunk):
    i = pl.program_id(1)
    t0 = i * TQ
    width = qt_ref.shape[3]
    nh = width // TQ
    nih = qit_ref.shape[3] // TQ
    krow = lax.broadcasted_iota(i32, (TQ, TQ), 0)
    qcol = lax.broadcasted_iota(i32, (TQ, TQ), 1)
    n_pairs = jnp.maximum(i // 2 + 1, -(-topk // (2 * TQ)))

    qit = qit_ref[0, 0]
    wit = wit_ref[0, 0]

    def score_tile(j):
        kt = ki_ref[0, pl.ds(pl.multiple_of(j * TQ, TQ), TQ), :]
        rel = jnp.maximum(jnp.dot(kt, qit, preferred_element_type=f32), 0.0)
        sc = jnp.zeros((TQ, TQ), f32)
        for h in range(nih):
            sc = sc + wit[h:h + 1, :] * rel[:, h * TQ:(h + 1) * TQ]
        sc = jnp.where(j * TQ + krow <= t0 + qcol, sc, NEG)
        sc = jnp.where(sc == 0.0, 0.0, sc)
        bits = pltpu.bitcast(sc, i32)
        keys_ref[j] = jnp.where(bits < 0, bits ^ 0x7FFFFFFF, bits)

    def score_body(jj, carry):
        score_tile(2 * jj)
        score_tile(2 * jj + 1)
        return carry

    lax.fori_loop(0, n_pairs, score_body, 0)

    def count(pred_fn):
        def body(jj, cnt):
            return (cnt + jnp.where(pred_fn(keys_ref[2 * jj]), 1.0, 0.0)
                    + jnp.where(pred_fn(keys_ref[2 * jj + 1]), 1.0, 0.0))
        cnt = lax.fori_loop(0, n_pairs, body, jnp.zeros((TQ, TQ), f32))
        return jnp.sum(cnt, axis=0, keepdims=True)

    def bits_body(state):
        t, prefix, n_ge = state
        for u in range(RADIX_BITS_PER_CHECK):
            cand = prefix ^ jnp.left_shift(jnp.int32(1), 31 - (t + u))
            cnt = count(lambda kj: kj >= cand)
            prefix = jnp.where(cnt >= topk, cand, prefix)
            n_ge = jnp.where(cnt >= topk, cnt, n_ge)
        return t + RADIX_BITS_PER_CHECK, prefix, n_ge

    n_all = (2 * n_pairs * TQ).astype(f32)
    _, thr, n_ge = lax.while_loop(lambda st: (st[0] < 32) & (jnp.max(st[2]) > topk), bits_body,
                                  (jnp.int32(0), jnp.full((1, TQ), INT_MIN, i32), jnp.full((1, TQ), n_all, f32)))

    @pl.when(jnp.max(n_ge) > topk)
    def _():
        need = topk - count(lambda kj: kj > thr)

        def fix_body(j, running):
            kj = keys_ref[j]
            tie = jnp.where(kj == thr, 1.0, 0.0)
            before = jnp.dot(tri_ref[...], tie.astype(bf16), preferred_element_type=f32) + running
            keys_ref[j] = jnp.where(before >= need, jnp.where(kj == thr, kj - 1, kj), kj)
            return running + jnp.sum(tie, axis=0, keepdims=True)

        lax.fori_loop(0, 2 * n_pairs, fix_body, jnp.zeros((1, TQ), f32))

    qs_ref[0:HEAD_DIM, :] = (qt_ref[0, 0].astype(f32) * (HEAD_DIM ** -0.5 * LOG2E)).astype(bf16)
    qs_ref[HEAD_DIM:K_COLS, :] = qx_ref[...]
    _masked_flash(i, qs_ref,
                  lambda j, c: k_ref[0, pl.ds(pl.multiple_of(j * TQ, TQ), TQ), :],
                  lambda j, c: vt_ref[0, j],
                  lambda j, c: jnp.where(keys_ref[j] >= thr, 0.0, NEG),
                  delta_ref, m_ref, acc_ref, s_ref, mt_ref, al_ref, chunk, True)
    acc = acc_ref[...]
    ot = acc[0:HEAD_DIM] / acc[HEAD_DIM:HEAD_DIM + 1]
    o_ref[0] = jnp.concatenate([ot[:, h * TQ:(h + 1) * TQ].T for h in range(nh)], axis=1).astype(o_ref.dtype)


def _dsa_attention(qt, k, v, qit, ki, wi, near, far):
    bn, nt, _, width = qt.shape
    s = nt * TQ
    nh = width // TQ
    hq = nh * HEAD_DIM
    nih = qit.shape[3] // TQ
    topk = min(C_TOPK_MAX, s // 4)
    wit = wi.reshape(bn, nt, TQ, nih).transpose(0, 1, 3, 2)
    tri = np.tril(np.ones((TQ, TQ), np.float32), -1)
    kern = functools.partial(_dsa_kernel, topk=topk, chunk=512)
    nt_scr = max(nt, -(-topk // TQ))
    return pl.pallas_call(
        kern,
        grid=(bn, nt),
        in_specs=[pl.BlockSpec((1, 1, HEAD_DIM, width), lambda b, i: (b, i, 0, 0)),
                  pl.BlockSpec((1, s, K_COLS), lambda b, i: (b, 0, 0)),
                  pl.BlockSpec((1, nt, V_ROWS, TQ), lambda b, i: (b, 0, 0, 0)),
                  pl.BlockSpec((1, 1, IDX_DIM, nih * TQ), lambda b, i: (b, i, 0, 0)),
                  pl.BlockSpec((1, s, IDX_DIM), lambda b, i: (b, 0, 0)),
                  pl.BlockSpec((1, 1, nih, TQ), lambda b, i: (b, i, 0, 0)),
                  pl.BlockSpec((K_COLS - HEAD_DIM, width), lambda b, i: (0, 0)),
                  pl.BlockSpec((2, TQ, width), lambda b, i: (0, 0, 0)),
                  pl.BlockSpec((TQ, TQ), lambda b, i: (0, 0))],
        out_specs=pl.BlockSpec((1, TQ, hq), lambda b, i: (b, i, 0)),
        out_shape=jax.ShapeDtypeStruct((bn, s, hq), bf16),
        scratch_shapes=[pltpu.VMEM((nt_scr, TQ, TQ), i32), pltpu.VMEM((K_COLS, width), bf16),
                        pltpu.VMEM((1, width), f32), pltpu.VMEM((V_ROWS, width), f32),
                        pltpu.VMEM((2, TQ, width), f32), pltpu.VMEM((2, 1, width), f32),
                        pltpu.VMEM((2, 1, width), f32)],
        compiler_params=_cparams(("parallel", "arbitrary")),
        name="dsa_attention",
    )(qt, _keys_with_ones(k), _value_tiles(v), qit, ki, wit, *_flash_operands(near, far), jnp.asarray(tri, bf16))


def _heads_first(a, g):
    bn, s, _ = a.shape
    return a.reshape(bn, s, g, HEAD_DIM).transpose(0, 2, 1, 3)


def _even_layer(x, table_t, w_in, sinks, cmpk_pos, cmpk_w1, cmpk_w2, cmpv_pos, cmpv_w1, cmpv_w2,
                w_out, ln1_g, ln1_b, ffn_gate, ffn_up, ffn_down, ln2_g, ln2_b):
    bn, s, d = x.shape
    xf = x.reshape(bn * s, d)
    a_q, a_kv, b_q, b_kv = A_HEADS * HEAD_DIM, A_KV * HEAD_DIM, B_HEADS * HEAD_DIM, B_KV * HEAD_DIM
    n_main = a_q + 2 * a_kv + b_q + 6 * b_kv
    rb = B_HEADS // B_KV
    w_tail = jnp.pad(w_in[:, n_main:], ((0, 0), (0, 128 - 3 * B_HEADS))).astype(bf16)
    cols = np.cumsum([0, a_q, a_kv, a_kv, b_q] + [b_kv] * 6)
    w_kv = jnp.concatenate([w_in[:, cols[1]:cols[3]], w_in[:, cols[4]:cols[10]]], axis=1).astype(bf16)
    w_qa = w_in[:, cols[0]:cols[1]].T.astype(bf16)
    w_qb = w_in[:, cols[3]:cols[4]].T.astype(bf16)
    main, tail, qt_a, qt_b = _proj_in(x, w_kv, w_tail, [w_qa, w_qb],
                                      [(A_KV, A_HEADS // A_KV, HEAD_DIM), (B_KV, rb, HEAD_DIM)])
    main = main.reshape(bn, s, 2 * a_kv + 6 * b_kv)
    gates = tail[:, :3 * B_HEADS]
    gates = gates.reshape(bn, s, 3, B_KV, rb).transpose(0, 3, 1, 2, 4).reshape(bn, B_KV, s, 3 * rb)
    kcols = np.cumsum([0, a_kv, a_kv] + [b_kv] * 6)
    ka, va, kc_in, vc_in, ksl, vsl, kwn, vwn = [main[:, :, kcols[t]:kcols[t + 1]] for t in range(8)]

    near_idx = _bucket_of(np.arange(TQ)[:, None] + TQ - np.arange(2 * TQ)[None, :])
    near = _bias_expand(table_t, near_idx, TQ)
    pad_b = -(-(B_WINDOW - 1) // TQ) * TQ
    win_idx = _bucket_of(np.arange(TQ)[:, None] + pad_b - np.arange(pad_b + TQ)[None, :])
    bias_win = _bias_expand(table_t[A_HEADS:A_HEADS + B_HEADS], win_idx, TQ)
    nc = s // CMP_STRIDE
    nt = s // TQ
    cmp_idx = _bucket_of((np.arange(nt)[:, None, None] * TQ + np.arange(TQ)[None, None, :])
                         - (np.arange(nc)[None, :, None] * CMP_STRIDE + CMP_BLOCK - 1)).reshape(nt * nc, TQ)
    bias_c = _bias_expand(table_t[A_HEADS:A_HEADS + B_HEADS], cmp_idx, min(nt * nc, 512))
    bias_c = bias_c.reshape(B_HEADS, nt, nc, TQ)
    far = table_t[:, NUM_BUCKETS - 1]

    out_a =_banded_gqa(qt_a, _heads_first(ka, A_KV), _heads_first(va, A_KV), near[:A_HEADS], sinks, A_WINDOW, bf16)
    o_win = _banded_gqa(qt_b, _heads_first(kwn, B_KV), _heads_first(vwn, B_KV), bias_win, None, B_WINDOW, f32)
    kc = _nsa_compress(_heads_first(kc_in, B_KV), cmpk_pos, cmpk_w1, cmpk_w2)
    vc = _nsa_compress(_heads_first(vc_in, B_KV), cmpv_pos, cmpv_w1, cmpv_w2)
    out_b = _nsa_mix(qt_b, kc, vc, bias_c, _heads_first(ksl, B_KV), _heads_first(vsl, B_KV),
                     near[A_HEADS:A_HEADS + B_HEADS], far[A_HEADS:A_HEADS + B_HEADS], gates, o_win)
    mix = jnp.concatenate([out_a, out_b], axis=-1).reshape(bn * s, a_q + b_q)
    x1 = _proj_res_ln(mix, w_out.astype(bf16), xf, ln1_g, ln1_b)
    x2 = _ffn_res_ln(x1, ffn_gate.astype(bf16), ffn_up.astype(bf16), ffn_down.astype(bf16), ln2_g, ln2_b)
    return x2.reshape(bn, s, d), near, far


def _odd_layer(x, near, far, w_in, w_out, ln1_g, ln1_b, router, exp_gate, exp_up, exp_down, ln2_g, ln2_b):
    bn, s, d = x.shape
    xf = x.reshape(bn * s, d)
    c_q = C_HEADS * HEAD_DIM
    n_main = c_q + 2 * HEAD_DIM + IDX_HEADS * IDX_DIM
    w_tail = jnp.pad(w_in[:, n_main:], ((0, 0), (0, 128 - IDX_DIM - IDX_HEADS))).astype(bf16)
    w_kv = w_in[:, c_q:c_q + 2 * HEAD_DIM].astype(bf16)
    w_q = w_in[:, :c_q].T.astype(bf16)
    w_qi = w_in[:, c_q + 2 * HEAD_DIM:n_main].T.astype(bf16)
    main, tail, qt, qit = _proj_in(x, w_kv, w_tail, [w_q, w_qi],
                                   [(1, C_HEADS, HEAD_DIM), (1, IDX_HEADS, IDX_DIM)])
    main = main.reshape(bn, s, 2 * HEAD_DIM)
    tail = tail.reshape(bn, s, 128)
    k = main[:, :, :HEAD_DIM]
    v = main[:, :, HEAD_DIM:]
    ki = tail[:, :, :IDX_DIM].astype(bf16)
    wi = tail[:, :, IDX_DIM:IDX_DIM + IDX_HEADS]
    mix = _dsa_attention(qt[:, 0], k, v, qit[:, 0], ki, wi, near[:C_HEADS], far[:C_HEADS]).reshape(bn * s, c_q)
    x1 = _proj_res_ln(mix, w_out.astype(bf16), xf, ln1_g, ln1_b)
    x2 = _moe_res_ln(x1, router, exp_gate.astype(bf16), exp_up.astype(bf16), exp_down.astype(bf16), ln2_g, ln2_b)
    return x2.reshape(bn, s, d)


def kernel(x, rel_bias, l0_w_in, l0_sinks, l0_cmpk_pos, l0_cmpk_w1, l0_cmpk_w2, l0_cmpv_pos, l0_cmpv_w1,
           l0_cmpv_w2, l0_w_out, l0_ln1_g, l0_ln1_b, l0_ffn_gate, l0_ffn_up, l0_ffn_down, l0_ln2_g, l0_ln2_b,
           l1_w_in, l1_w_out, l1_ln1_g, l1_ln1_b, l1_router, l1_exp_gate, l1_exp_up, l1_exp_down, l1_ln2_g,
           l1_ln2_b):
    table_t = rel_bias.T.astype(f32)
    x, near, far = _even_layer(x, table_t, l0_w_in, l0_sinks, l0_cmpk_pos, l0_cmpk_w1, l0_cmpk_w2, l0_cmpv_pos,
                               l0_cmpv_w1, l0_cmpv_w2, l0_w_out, l0_ln1_g, l0_ln1_b, l0_ffn_gate, l0_ffn_up,
                               l0_ffn_down, l0_ln2_g, l0_ln2_b)
    return _odd_layer(x, near, far, l1_w_in, l1_w_out, l1_ln1_g, l1_ln1_b, l1_router, l1_exp_gate, l1_exp_up,
                      l1_exp_down, l1_ln2_g, l1_ln2_b)
```

```python
import functools
import math

import numpy as np
import jax
import jax.numpy as jnp
from jax import lax
from jax.experimental import pallas as pl
from jax.experimental.pallas import tpu as pltpu

f32 = jnp.float32
bf16 = jnp.bfloat16
i32 = jnp.int32

D_MODEL = 1024
HEAD_DIM = 64
NUM_BUCKETS = 32
MAX_DISTANCE = 128
A_HEADS, A_KV, A_WINDOW = 8, 2, 128
B_HEADS, B_KV, B_WINDOW = 8, 2, 512
CMP_BLOCK, CMP_STRIDE = 32, 16
SLC_BLOCK, SLC_TOPN = 64, 16
C_HEADS, IDX_HEADS, IDX_DIM, C_TOPK_MAX = 16, 8, 32, 256
D_FF, N_EXPERTS = 2816, 8
DEPTH = 2
ALPHA = (2.0 * DEPTH) ** 0.25
LN_EPS = 1e-5
NEG = -1e30
POS_BIG = 1e30
INT_MIN = -(2 ** 31)

TQ = 128
FF_CHUNK = 256
VMEM_LIMIT = 56 * 1024 * 1024


def _cparams(sem):
    return pltpu.CompilerParams(dimension_semantics=sem, vmem_limit_bytes=VMEM_LIMIT)


def _bucket_table(max_dist):
    n = np.arange(max_dist + 1)
    max_exact = NUM_BUCKETS // 2
    nf = np.maximum(n, 1).astype(np.float64)
    large = max_exact + (np.log(nf / max_exact) / math.log(MAX_DISTANCE / max_exact)
                         * (NUM_BUCKETS - max_exact)).astype(np.int64)
    large = np.minimum(large, NUM_BUCKETS - 1)
    return np.where(n < max_exact, n, large).astype(np.int32)


def _bucket_of(dist):
    dist = np.maximum(dist, 0)
    return _bucket_table(int(dist.max()))[dist]


def _proj_in_kernel(*refs, groups):
    ng = len(groups)
    x_ref = refs[0]
    wq_refs = refs[1:1 + ng]
    oq_refs = refs[1 + ng:]
    xb = x_ref[...].astype(bf16)
    tm = xb.shape[0]
    for wq_ref, oq_ref, (kind, gb, hb, dh) in zip(wq_refs, oq_refs, groups):
        if kind == "plain":
            oq_ref[...] = jnp.dot(xb, wq_ref[...], preferred_element_type=f32).astype(oq_ref.dtype)
            continue
        if kind == "keys":
            kk = jnp.dot(xb, wq_ref[...], preferred_element_type=f32).astype(bf16)
            ones = jnp.where(lax.broadcasted_iota(i32, (tm, K_COLS - dh), 1) < 2, 1.0, 0.0).astype(bf16)
            for g in range(gb):
                oq_ref[0, g, :, 0:dh] = kk[:, g * dh:(g + 1) * dh]
                oq_ref[0, g, :, dh:K_COLS] = ones
            continue
        qt = lax.dot_general(wq_ref[...], xb, (((1,), (1,)), ((), ())), preferred_element_type=f32).astype(bf16)
        ones = jnp.where(lax.broadcasted_iota(i32, (V_ROWS - HEAD_DIM, TQ), 0) < 1, 1.0, 0.0).astype(bf16)
        for g in range(gb):
            for h in range(hb):
                rows = slice((g * hb + h) * dh, (g * hb + h + 1) * dh)
                for t in range(tm // TQ):
                    oq_ref[0, g, t, 0:dh, h * TQ:(h + 1) * TQ] = qt[rows, t * TQ:(t + 1) * TQ]
                    if kind == "values":
                        oq_ref[0, g, t, dh:V_ROWS, :] = ones


def _proj_in(x3, w_groups, groups, tm=512):
    bn, s, k = x3.shape
    m = bn * s
    tm = min(tm, s)
    per_b = s // tm
    tiles = tm // TQ
    q_specs, q_shapes = [], []
    for kind, gb, hb, dh in groups:
        if kind == "plain":
            q_specs.append(pl.BlockSpec((tm, gb), lambda i: (i, 0)))
            q_shapes.append(jax.ShapeDtypeStruct((m, gb), hb))
        elif kind == "keys":
            q_specs.append(pl.BlockSpec((1, gb, tm, K_COLS), lambda i: (i // per_b, 0, i % per_b, 0)))
            q_shapes.append(jax.ShapeDtypeStruct((bn, gb, s, K_COLS), bf16))
        else:
            rows = V_ROWS if kind == "values" else dh
            q_specs.append(pl.BlockSpec((1, gb, tiles, rows, hb * TQ), lambda i: (i // per_b, 0, i % per_b, 0, 0)))
            q_shapes.append(jax.ShapeDtypeStruct((bn, gb, s // TQ, rows, hb * TQ), bf16))
    return pl.pallas_call(
        functools.partial(_proj_in_kernel, groups=tuple(groups)),
        grid=(m // tm,),
        in_specs=[pl.BlockSpec((tm, k), lambda i: (i, 0))]
                 + [pl.BlockSpec(w.shape, lambda i: (0, 0)) for w in w_groups],
        out_specs=q_specs,
        out_shape=q_shapes,
        compiler_params=_cparams(("parallel",)),
        name="proj_in",
    )(x3.reshape(m, k), *w_groups)


def _bias_expand_kernel(tab_ref, idx_ref, o_ref):
    h = pl.program_id(0)
    idx = idx_ref[...]
    acc = jnp.zeros(idx.shape, f32)
    for b in range(NUM_BUCKETS):
        acc = jnp.where(idx == b, tab_ref[h, b], acc)
    o_ref[0] = acc


def _bias_expand(table_t, idx, tr):
    nh = table_t.shape[0]
    r, c = idx.shape
    return pl.pallas_call(
        _bias_expand_kernel,
        grid=(nh, r // tr),
        in_specs=[pl.BlockSpec(memory_space=pltpu.SMEM),
                  pl.BlockSpec((tr, c), lambda h, i: (i, 0))],
        out_specs=pl.BlockSpec((1, tr, c), lambda h, i: (h, i, 0)),
        out_shape=jax.ShapeDtypeStruct((nh, r, c), f32),
        compiler_params=_cparams(("parallel", "arbitrary")),
        name="bias_expand",
    )(table_t, jnp.asarray(idx, dtype=i32))


def _layer_norm(z, g, b):
    mu = jnp.mean(z, axis=-1, keepdims=True)
    zc = z - mu
    var = jnp.mean(zc * zc, axis=-1, keepdims=True)
    return zc * lax.rsqrt(var + LN_EPS) * g + b


def _proj_ln_kernel(mix_ref, w_ref, x_ref, g_ref, b_ref, o_ref):
    y = jnp.dot(mix_ref[...], w_ref[...], preferred_element_type=f32)
    o_ref[...] = _layer_norm(ALPHA * x_ref[...] + y, g_ref[...], b_ref[...])


def _proj_res_ln(mix, w, x, g, b, tm=512):
    m, k = mix.shape
    d = w.shape[1]
    return pl.pallas_call(
        _proj_ln_kernel,
        grid=(m // tm,),
        in_specs=[pl.BlockSpec((tm, k), lambda i: (i, 0)),
                  pl.BlockSpec((k, d), lambda i: (0, 0)),
                  pl.BlockSpec((tm, d), lambda i: (i, 0)),
                  pl.BlockSpec((1, d), lambda i: (0, 0)),
                  pl.BlockSpec((1, d), lambda i: (0, 0))],
        out_specs=pl.BlockSpec((tm, d), lambda i: (i, 0)),
        out_shape=jax.ShapeDtypeStruct((m, d), f32),
        compiler_params=_cparams(("parallel",)),
        name="proj_res_ln",
    )(mix, w, x, g.reshape(1, d), b.reshape(1, d))


def _swiglu_accumulate(xb, wg_ref, wu_ref, wd_ref, acc_ref, widx):
    d_ff = wg_ref.shape[-1]
    for c in range(d_ff // FF_CHUNK):
        cols = slice(c * FF_CHUNK, (c + 1) * FF_CHUNK)
        gate = jnp.dot(xb, wg_ref[widx + (slice(None), cols)], preferred_element_type=f32)
        up = jnp.dot(xb, wu_ref[widx + (slice(None), cols)], preferred_element_type=f32)
        h = jax.nn.silu(gate) * up
        acc_ref[...] += jnp.dot(h.astype(bf16), wd_ref[widx + (cols, slice(None))], preferred_element_type=f32)


def _ffn_kernel(x_ref, wg_ref, wu_ref, wd_ref, g_ref, b_ref, o_ref, acc_ref):
    x = x_ref[...]
    acc_ref[...] = jnp.zeros_like(acc_ref)
    _swiglu_accumulate(x.astype(bf16), wg_ref, wu_ref, wd_ref, acc_ref, ())
    o_ref[...] = _layer_norm(ALPHA * x + acc_ref[...], g_ref[...], b_ref[...])


def _ffn_res_ln(x, wg, wu, wd, g, b, tm=512):
    m, d = x.shape
    d_ff = wg.shape[1]
    return pl.pallas_call(
        _ffn_kernel,
        grid=(m // tm,),
        in_specs=[pl.BlockSpec((tm, d), lambda i: (i, 0)),
                  pl.BlockSpec((d, d_ff), lambda i: (0, 0)),
                  pl.BlockSpec((d, d_ff), lambda i: (0, 0)),
                  pl.BlockSpec((d_ff, d), lambda i: (0, 0)),
                  pl.BlockSpec((1, d), lambda i: (0, 0)),
                  pl.BlockSpec((1, d), lambda i: (0, 0))],
        out_specs=pl.BlockSpec((tm, d), lambda i: (i, 0)),
        out_shape=jax.ShapeDtypeStruct((m, d), f32),
        scratch_shapes=[pltpu.VMEM((tm, d), f32)],
        compiler_params=_cparams(("parallel",)),
        name="ffn_res_ln",
    )(x, wg, wu, wd, g.reshape(1, d), b.reshape(1, d))


MOE_CAP = 288


def _router_gates(x, r_ref, lane):
    logits = jnp.dot(x, r_ref[...], preferred_element_type=f32)
    logits = jnp.where(lane < N_EXPERTS, logits, -jnp.inf)
    m1 = jnp.max(logits, axis=-1, keepdims=True)
    i1 = jnp.min(jnp.where(logits == m1, lane, 128), axis=-1, keepdims=True)
    rest = jnp.where(lane == i1, -jnp.inf, logits)
    m2 = jnp.max(rest, axis=-1, keepdims=True)
    i2 = jnp.min(jnp.where(rest == m2, lane, 128), axis=-1, keepdims=True)
    e2 = jnp.exp(m2 - m1)
    g1 = 1.0 / (1.0 + e2)
    return jnp.where(lane == i1, g1, 0.0) + jnp.where(lane == i2, e2 * g1, 0.0)


def _moe_kernel(x_ref, r_ref, tri_ref, wg_ref, wu_ref, wd_ref, g_ref, b_ref, o_ref,
                split_ref, post_ref, yc_ref):
    e = pl.program_id(1)
    tm = x_ref.shape[0]

    @pl.when(e == 0)
    def _():
        comb = _router_gates(x_ref[...], r_ref, lax.broadcasted_iota(i32, (tm, 128), 1))
        hi = comb.astype(bf16)
        rest = comb - hi.astype(f32)
        mid = rest.astype(bf16)
        split_ref[0] = hi
        split_ref[1] = mid
        split_ref[2] = (rest - mid.astype(f32)).astype(bf16)
        chosen = jnp.where(comb > 0.0, 1.0, 0.0)
        pos = jnp.dot(tri_ref[...], chosen.astype(bf16), preferred_element_type=f32)
        pos = jnp.where(comb > 0.0, pos, -1.0)
        for t in range(tm // 128):
            rows = slice(t * 128, (t + 1) * 128)
            post_ref[:, rows] = pos[rows].T[0:N_EXPERTS]
        hi32 = jnp.floor((pos + 1.0) * (1.0 / 32.0))
        split_ref[3] = hi32.astype(bf16)
        split_ref[4] = (pos + 1.0 - 32.0 * hi32).astype(bf16)
        o_ref[...] = jnp.zeros_like(o_ref)

    pos_row = post_ref[pl.ds(e, 1), :].astype(i32)
    n_tok = jnp.sum(jnp.where(pos_row >= 0, 1.0, 0.0)).astype(i32)
    pick = jnp.where(lax.broadcasted_iota(i32, (128, 128), 0) == e, 1.0, 0.0).astype(bf16)

    def column(t):
        return jnp.dot(split_ref[t], pick, preferred_element_type=f32)

    gate_b = column(0) + column(1) + column(2)
    pos_b = (32.0 * column(3) + column(4) - 1.0).astype(i32)
    lane = lax.broadcasted_iota(i32, (tm, 128), 1)

    def pass_body(p, carry):
        base = p * MOE_CAP
        gather = jnp.where(pos_row == base + lax.broadcasted_iota(i32, (MOE_CAP, tm), 0), 1.0, 0.0).astype(bf16)
        xc = jnp.dot(gather, x_ref[...].astype(bf16), preferred_element_type=f32).astype(bf16)
        yc_ref[...] = jnp.zeros_like(yc_ref)
        _swiglu_accumulate(xc, wg_ref, wu_ref, wd_ref, yc_ref, (0,))
        scatter = jnp.concatenate([jnp.where(pos_b == base + t * 128 + lane, 1.0, 0.0).astype(bf16)
                                   for t in range(-(-MOE_CAP // 128))], axis=1)[:, :MOE_CAP]
        y = jnp.dot(scatter, yc_ref[...].astype(bf16), preferred_element_type=f32)
        o_ref[...] += jnp.concatenate([gate_b] * (y.shape[1] // 128), axis=1) * y
        return carry

    lax.fori_loop(0, (n_tok + MOE_CAP - 1) // MOE_CAP, pass_body, 0)

    @pl.when(e == N_EXPERTS - 1)
    def _():
        o_ref[...] = _layer_norm(ALPHA * x_ref[...] + o_ref[...], g_ref[...], b_ref[...])


def _moe_res_ln(x, router, wg, wu, wd, g, b, tm=1024):
    m, d = x.shape
    ne, _, d_ff = wg.shape
    tm = min(tm, m)
    router_p = jnp.pad(router, ((0, 0), (0, 128 - ne)))
    tri = jnp.asarray(np.tril(np.ones((tm, tm), np.float32), -1), bf16)
    single = pl.Buffered(1)
    return pl.pallas_call(
        _moe_kernel,
        grid=(m // tm, ne),
        in_specs=[pl.BlockSpec((tm, d), lambda i, e: (i, 0), pipeline_mode=single),
                  pl.BlockSpec((d, 128), lambda i, e: (0, 0), pipeline_mode=single),
                  pl.BlockSpec((tm, tm), lambda i, e: (0, 0), pipeline_mode=single),
                  pl.BlockSpec((1, d, d_ff), lambda i, e: (e, 0, 0)),
                  pl.BlockSpec((1, d, d_ff), lambda i, e: (e, 0, 0)),
                  pl.BlockSpec((1, d_ff, d), lambda i, e: (e, 0, 0)),
                  pl.BlockSpec((1, d), lambda i, e: (0, 0)),
                  pl.BlockSpec((1, d), lambda i, e: (0, 0))],
        out_specs=pl.BlockSpec((tm, d), lambda i, e: (i, 0)),
        out_shape=jax.ShapeDtypeStruct((m, d), f32),
        scratch_shapes=[pltpu.VMEM((5, tm, 128), bf16), pltpu.VMEM((N_EXPERTS, tm), f32),
                        pltpu.VMEM((MOE_CAP, d), f32)],
        compiler_params=_cparams(("parallel", "arbitrary")),
        name="moe_res_ln",
    )(x, router_p, tri, wg, wu, wd, g.reshape(1, d), b.reshape(1, d))


def _compress_kernel(blk_ref, pos_ref, w1_ref, w2_ref, o_ref):
    xb = (blk_ref[0].astype(f32) + pos_ref[...]).astype(bf16)
    h = jax.nn.gelu(jnp.dot(xb, w1_ref[...], preferred_element_type=f32))
    o_ref[0] = jnp.dot(h.astype(bf16), w2_ref[...], preferred_element_type=f32).astype(o_ref.dtype)


def _nsa_compress(kv, pos, w1, w2):
    bn, g, s, dh = kv.shape
    nc = s // CMP_STRIDE
    chunks = kv.reshape(bn, g, nc, CMP_STRIDE * dh)
    nxt = jnp.pad(chunks[:, :, 1:], ((0, 0), (0, 0), (0, 1), (0, 0)))
    blocks = jnp.concatenate([chunks, nxt], axis=-1).reshape(bn * g, nc, CMP_BLOCK * dh)
    cin = CMP_BLOCK * dh
    hid = w1.shape[1]
    out = pl.pallas_call(
        _compress_kernel,
        grid=(bn * g,),
        in_specs=[pl.BlockSpec((1, nc, cin), lambda i: (i, 0, 0)),
                  pl.BlockSpec((1, cin), lambda i: (0, 0)),
                  pl.BlockSpec((cin, hid), lambda i: (0, 0)),
                  pl.BlockSpec((hid, dh), lambda i: (0, 0))],
        out_specs=pl.BlockSpec((1, nc, dh), lambda i: (i, 0, 0)),
        out_shape=jax.ShapeDtypeStruct((bn * g, nc, dh), bf16),
        compiler_params=_cparams(("parallel",)),
        name="nsa_compress",
    )(blocks, pos.reshape(1, cin), w1.astype(bf16), w2.astype(bf16))
    return out.reshape(bn, g, nc, dh)


V_ROWS = HEAD_DIM + 16


LOG2E = 1.4426950408889634
K_COLS = 2 * HEAD_DIM


def _masked_flash(i, qs_ref, key_tile, value_tile, mask_bias, delta_ref, m_ref, acc_ref, s_ref, mt_ref, al_ref, chunk,
                  shared_keys):
    width = qs_ref.shape[1]
    n_chunks = width // chunk
    m_ref[...] = jnp.full(m_ref.shape, NEG, f32)
    acc_ref[...] = jnp.zeros(acc_ref.shape, f32)

    def scores_chunk(j, delta, c, shared):
        slot = j % 2
        if c == 0 or not shared_keys:
            shared["kt"] = key_tile(j, c)
            shared["maskb"] = jnp.concatenate([mask_bias(j, c)] * (chunk // TQ), axis=1)
        cols = slice(c * chunk, (c + 1) * chunk)
        s = jnp.dot(shared["kt"], qs_ref[:, cols], preferred_element_type=f32) + shared["maskb"]
        if delta is not None:
            s = s + delta_ref[delta, :, cols]
        s_ref[slot, :, cols] = s
        m_old = m_ref[:, cols]
        m_new = jnp.maximum(m_old, jnp.max(s, axis=0, keepdims=True))
        m_ref[:, cols] = m_new
        mt_ref[slot, :, cols] = m_new
        al_ref[slot, :, cols] = jnp.exp2(m_old - m_new)

    def values_chunk(j, c, shared):
        slot = j % 2
        if c == 0 or not shared_keys:
            shared["vt"] = value_tile(j, c)
        cols = slice(c * chunk, (c + 1) * chunk)
        p = jnp.exp2(s_ref[slot, :, cols] - mt_ref[slot, :, cols])
        acc_ref[:, cols] = (al_ref[slot, :, cols] * acc_ref[:, cols]
                            + jnp.dot(shared["vt"], p.astype(bf16), preferred_element_type=f32))

    def scores(j, delta):
        shared = {}
        for c in range(n_chunks):
            scores_chunk(j, delta, c, shared)

    def values(j):
        shared = {}
        for c in range(n_chunks):
            values_chunk(j, c, shared)

    def values_then_scores(jv, js, delta):
        sv, ss = {}, {}
        for c in range(n_chunks):
            values_chunk(jv, c, sv)
            scores_chunk(js, delta, c, ss)

    @pl.when(i >= 2)
    def _():
        scores(0, None)

        def pipe_body(j, carry):
            values_then_scores(j - 1, j, None)
            return carry

        lax.fori_loop(1, i - 1, pipe_body, 0)
        values_then_scores(i - 2, i - 1, 0)
        values_then_scores(i - 1, i, 1)

    @pl.when(i == 1)
    def _():
        scores(0, 0)
        values_then_scores(0, 1, 1)

    @pl.when(i == 0)
    def _():
        scores(0, 1)

    values(i)


def _flash_operands(near, far):
    nh = near.shape[0]
    width = nh * TQ
    far2 = jnp.repeat(far * LOG2E, TQ).reshape(1, width)
    hi = far2.astype(bf16)
    lo = (far2 - hi.astype(f32)).astype(bf16)
    extra = jnp.concatenate([hi, lo, jnp.zeros((K_COLS - HEAD_DIM - 2, width), bf16)], axis=0)
    neart = near.reshape(nh, TQ, 2, TQ).transpose(2, 3, 0, 1).reshape(2, TQ, width) * LOG2E
    neart = neart - (hi.astype(f32) + lo.astype(f32))
    causal = np.where(np.arange(TQ)[:, None] <= np.arange(TQ)[None, :], 0.0, NEG).astype(np.float32)
    return extra, jnp.stack([neart[0], neart[1] + jnp.asarray(np.tile(causal, (1, nh)))], axis=0)


def _value_tiles(v):
    lead = v.shape[:-2]
    nt = v.shape[-2] // TQ
    vt = jnp.swapaxes(v.reshape(lead + (nt, TQ, HEAD_DIM)), -1, -2)
    return jnp.concatenate([vt, jnp.ones(lead + (nt, 1, TQ), vt.dtype),
                            jnp.zeros(lead + (nt, V_ROWS - HEAD_DIM - 1, TQ), vt.dtype)], axis=-2)


def _banded_kernel(qt_ref, k_ref, vt_ref, bias_ref, neg_ref, sink_ref, o_ref, *, rep, kt_tiles, use_sinks):
    i = pl.program_id(1)
    kw = kt_tiles * TQ
    outs = []
    for g in range(qt_ref.shape[1]):
        qs = jnp.concatenate([(qt_ref[0, g, 0].astype(f32) * (HEAD_DIM ** -0.5)).astype(bf16), neg_ref[...]], axis=0)
        k = k_ref[0, g, pl.ds(pl.multiple_of(i * TQ, TQ), kw), :]
        s = jnp.dot(k, qs, preferred_element_type=f32) + bias_ref[g]
        m = jnp.max(s, axis=0, keepdims=True)
        if use_sinks:
            m = jnp.maximum(m, sink_ref[g])
        p = jnp.exp(s - m).astype(bf16)
        vt = jnp.concatenate([vt_ref[0, g, i + t] for t in range(kt_tiles)], axis=1)
        acc = jnp.dot(vt, p, preferred_element_type=f32)
        l = acc[HEAD_DIM:HEAD_DIM + 1]
        if use_sinks:
            l = l + jnp.exp(sink_ref[g] - m)
        ot = acc[0:HEAD_DIM] / l
        outs += [ot[:, r * TQ:(r + 1) * TQ].T for r in range(rep)]
    o_ref[0] = jnp.concatenate(outs, axis=1).astype(o_ref.dtype)


def _banded_gqa(qt, k, v, bias, sinks, window, out_dtype):
    bn, g, nt, _, width = qt.shape
    rep = width // TQ
    s = nt * TQ
    pad = -(-(window - 1) // TQ) * TQ
    kw = pad + TQ
    kd = 2 * HEAD_DIM
    flag = jnp.broadcast_to((jnp.arange(pad + s) < pad).astype(k.dtype)[:, None], (bn, g, pad + s, 1))
    kp = jnp.concatenate([jnp.pad(k, ((0, 0), (0, 0), (pad, 0), (0, 0))), flag,
                          jnp.zeros((bn, g, pad + s, kd - HEAD_DIM - 1), k.dtype)], axis=-1)
    vt = _value_tiles(jnp.pad(v, ((0, 0), (0, 0), (pad, 0), (0, 0))))
    neg = np.zeros((kd - HEAD_DIM, width), np.float32)
    neg[0] = NEG
    dist = np.arange(TQ)[:, None] + pad - np.arange(kw)[None, :]
    band = jnp.asarray((dist >= 0) & (dist < window))
    bias_t = jnp.where(band, bias, NEG).reshape(g, rep, TQ, kw).transpose(0, 3, 1, 2).reshape(g, kw, width)
    use_sinks = sinks is not None
    sink_rows = (jnp.repeat(sinks.astype(f32), TQ) if use_sinks else jnp.zeros((g * width,), f32)).reshape(g, 1, width)
    kern = functools.partial(_banded_kernel, rep=rep, kt_tiles=kw // TQ, use_sinks=use_sinks)
    rw = rep * HEAD_DIM
    return pl.pallas_call(
        kern,
        grid=(bn, nt),
        in_specs=[pl.BlockSpec((1, g, 1, HEAD_DIM, width), lambda b, i: (b, 0, i, 0, 0)),
                  pl.BlockSpec((1, g, pad + s, kd), lambda b, i: (b, 0, 0, 0)),
                  pl.BlockSpec((1, g, (pad + s) // TQ, V_ROWS, TQ), lambda b, i: (b, 0, 0, 0, 0)),
                  pl.BlockSpec((g, kw, width), lambda b, i: (0, 0, 0)),
                  pl.BlockSpec((kd - HEAD_DIM, width), lambda b, i: (0, 0)),
                  pl.BlockSpec((g, 1, width), lambda b, i: (0, 0, 0))],
        out_specs=pl.BlockSpec((1, TQ, g * rw), lambda b, i: (b, i, 0)),
        out_shape=jax.ShapeDtypeStruct((bn, s, g * rw), out_dtype),
        compiler_params=_cparams(("parallel", "arbitrary")),
        name="banded_gqa_w%d" % window,
    )(qt, kp, vt, bias_t, jnp.asarray(neg, bf16), sink_rows)


def _nsa_kernel(qt_ref, kc_ref, vct_ref, biasc_ref, ovt_ref, ks_ref, vst_ref, qx_ref, delta_ref, gt_ref, gn_ref,
                owin_ref, o_ref, qs_ref, selb_ref, m_ref, acc_ref, s_ref, mt_ref, al_ref, *, rep, n_sel):
    i = pl.program_id(1)
    t0 = i * TQ
    groups = qt_ref.shape[1]
    nc = kc_ref.shape[2]
    nblk = ovt_ref.shape[0]
    width = rep * TQ
    cmp_end = lax.broadcasted_iota(i32, (nc, TQ), 0) * CMP_STRIDE + (CMP_BLOCK - 1)
    valid_c = t0 + lax.broadcasted_iota(i32, (nc, TQ), 1) >= cmp_end
    blk = lax.broadcasted_iota(i32, (nblk, TQ), 0)
    tq = t0 + lax.broadcasted_iota(i32, (nblk, TQ), 1)
    cur = tq // SLC_BLOCK
    forced = (blk == 0) | (blk == cur) | (blk == cur - 1)
    admitted = blk * SLC_BLOCK <= tq

    o_cmps = []
    for g in range(groups):
        qf = qt_ref[0, g, 0].astype(f32)
        qs = (qf * (HEAD_DIM ** -0.5)).astype(bf16)
        qs_ref[0:HEAD_DIM, g * width:(g + 1) * width] = (qf * (HEAD_DIM ** -0.5 * LOG2E)).astype(bf16)

        s_c = jnp.dot(kc_ref[0, g], qs, preferred_element_type=f32)
        p_cols = []
        p_sum = jnp.zeros((nc, TQ), f32)
        for r in range(rep):
            sr = jnp.where(valid_c, s_c[:, r * TQ:(r + 1) * TQ] + biasc_ref[g * rep + r, 0], NEG)
            m = jnp.max(sr, axis=0, keepdims=True)
            e = jnp.where(valid_c, jnp.exp(sr - m), 0.0)
            l = jnp.sum(e, axis=0, keepdims=True)
            p = e * (1.0 / jnp.maximum(l, 1e-30))
            p_cols.append(p.astype(bf16))
            p_sum = p_sum + p
        o_cmps.append(jnp.dot(vct_ref[0, g], jnp.concatenate(p_cols, axis=1), preferred_element_type=f32))

        imp = jnp.dot(ovt_ref[...], p_sum, preferred_element_type=f32)
        score = jnp.where(forced, POS_BIG, jnp.where(admitted, imp, NEG))
        rank = jnp.zeros((nblk, TQ), i32)
        for kk in range(nblk):
            ck = score[kk:kk + 1, :]
            rank = rank + jnp.where(ck > score, 1, jnp.where(ck == score, jnp.where(blk > kk, 1, 0), 0))
        selb_ref[g] = jnp.where(rank < n_sel, 0.0, NEG)

    per_tile = TQ // SLC_BLOCK

    def mask_bias(j, c):
        rows = [jnp.broadcast_to(selb_ref[c, pl.ds(per_tile * j + t, 1), :], (SLC_BLOCK, TQ))
                for t in range(per_tile)]
        return jnp.concatenate(rows, axis=0)

    qs_ref[HEAD_DIM:K_COLS, :] = qx_ref[...]
    _masked_flash(i, qs_ref,
                  lambda j, c: ks_ref[0, c, pl.ds(pl.multiple_of(j * TQ, TQ), TQ), :],
                  lambda j, c: vst_ref[0, c, j],
                  mask_bias, delta_ref, m_ref, acc_ref, s_ref, mt_ref, al_ref, width, False)
    acc = acc_ref[...]
    o_slc = acc[0:HEAD_DIM] / acc[HEAD_DIM:HEAD_DIM + 1]

    outs = []
    for g in range(groups):
        sig_t = jax.nn.sigmoid(gt_ref[0, g, 0])
        sig_n = jax.nn.sigmoid(gn_ref[0, g])
        for r in range(rep):
            h = g * rep + r
            mixed = (sig_t[r:r + 1] * o_cmps[g][:, r * TQ:(r + 1) * TQ]
                     + sig_t[rep + r:rep + r + 1] * o_slc[:, h * TQ:(h + 1) * TQ])
            o_win = owin_ref[0, :, h * HEAD_DIM:(h + 1) * HEAD_DIM]
            outs.append(mixed.T + sig_n[:, 2 * rep + r:2 * rep + r + 1] * o_win)
    o_ref[0] = jnp.concatenate(outs, axis=1).astype(o_ref.dtype)


def _nsa_mix(qt, kc, vc, bias_c, ks, vs, near, far, gates, o_win):
    bn, g, nt, _, width = qt.shape
    s = nt * TQ
    nc = kc.shape[2]
    rep = width // TQ
    hq = g * rep * HEAD_DIM
    nblk = s // SLC_BLOCK
    n_sel = min(SLC_TOPN, nblk)
    ci = np.arange(nc)[:, None] * CMP_STRIDE
    sj = np.arange(nblk)[None, :] * SLC_BLOCK
    overlap = np.clip(np.minimum(ci + CMP_BLOCK, sj + SLC_BLOCK) - np.maximum(ci, sj), 0, None) / CMP_BLOCK
    overlap[nc - 1] = 0.0
    gates_t = gates.reshape(bn, g, nt, TQ, 3 * rep).transpose(0, 1, 2, 4, 3)
    kern = functools.partial(_nsa_kernel, rep=rep, n_sel=n_sel)
    return pl.pallas_call(
        kern,
        grid=(bn, nt),
        in_specs=[pl.BlockSpec((1, g, 1, HEAD_DIM, width), lambda b, i: (b, 0, i, 0, 0)),
                  pl.BlockSpec((1, g, nc, HEAD_DIM), lambda b, i: (b, 0, 0, 0)),
                  pl.BlockSpec((1, g, HEAD_DIM, nc), lambda b, i: (b, 0, 0, 0)),
                  pl.BlockSpec((g * rep, 1, nc, TQ), lambda b, i: (0, i, 0, 0)),
                  pl.BlockSpec((nblk, nc), lambda b, i: (0, 0)),
                  pl.BlockSpec((1, g, s, K_COLS), lambda b, i: (b, 0, 0, 0)),
                  pl.BlockSpec((1, g, nt, V_ROWS, TQ), lambda b, i: (b, 0, 0, 0, 0)),
                  pl.BlockSpec((K_COLS - HEAD_DIM, g * width), lambda b, i: (0, 0)),
                  pl.BlockSpec((2, TQ, g * width), lambda b, i: (0, 0, 0)),
                  pl.BlockSpec((1, g, 1, 3 * rep, TQ), lambda b, i: (b, 0, i, 0, 0)),
                  pl.BlockSpec((1, g, TQ, 3 * rep), lambda b, i: (b, 0, i, 0)),
                  pl.BlockSpec((1, TQ, hq), lambda b, i: (b, i, 0))],
        out_specs=pl.BlockSpec((1, TQ, hq), lambda b, i: (b, i, 0)),
        out_shape=jax.ShapeDtypeStruct((bn, s, hq), bf16),
        scratch_shapes=[pltpu.VMEM((K_COLS, g * width), bf16), pltpu.VMEM((g, nblk, TQ), f32),
                        pltpu.VMEM((1, g * width), f32), pltpu.VMEM((V_ROWS, g * width), f32),
                        pltpu.VMEM((2, TQ, g * width), f32), pltpu.VMEM((2, 1, g * width), f32),
                        pltpu.VMEM((2, 1, g * width), f32)],
        compiler_params=_cparams(("parallel", "arbitrary")),
        name="nsa_mix",
    )(qt, kc, jnp.swapaxes(vc, 2, 3), bias_c, jnp.asarray(overlap.T, f32), ks, vs,
      *_flash_operands(near, far), gates_t, gates, o_win)


RADIX_BITS_PER_CHECK = 8


def _dsa_kernel(qt_ref, k_ref, vt_ref, qit_ref, ki_ref, wit_ref, qx_ref, delta_ref, tri_ref, o_ref,
                keys_ref, qs_ref, m_ref, acc_ref, s_ref, mt_ref, al_ref, *, topk, chunk):
    i = pl.program_id(1)
    t0 = i * TQ
    width = qt_ref.shape[3]
    nh = width // TQ
    nih = qit_ref.shape[3] // TQ
    krow = lax.broadcasted_iota(i32, (TQ, TQ), 0)
    qcol = lax.broadcasted_iota(i32, (TQ, TQ), 1)
    n_pairs = jnp.maximum(i // 2 + 1, -(-topk // (2 * TQ)))

    qit = qit_ref[0, 0]
    wit = wit_ref[0, 0]

    def score_tile(j):
        kt = ki_ref[0, pl.ds(pl.multiple_of(j * TQ, TQ), TQ), :]
        rel = jnp.maximum(jnp.dot(kt, qit, preferred_element_type=f32), 0.0)
        sc = jnp.zeros((TQ, TQ), f32)
        for h in range(nih):
            sc = sc + wit[h:h + 1, :] * rel[:, h * TQ:(h + 1) * TQ]
        sc = jnp.where(j * TQ + krow <= t0 + qcol, sc, NEG)
        sc = jnp.where(sc == 0.0, 0.0, sc)
        bits = pltpu.bitcast(sc, i32)
        keys_ref[j] = jnp.where(bits < 0, bits ^ 0x7FFFFFFF, bits)

    def score_body(jj, carry):
        for u in range(4):
            score_tile(4 * jj + u)
        return carry

    lax.fori_loop(0, (n_pairs + 1) // 2, score_body, 0)

    def count(pred_fn):
        def body(jj, cnt):
            return (cnt + jnp.where(pred_fn(keys_ref[2 * jj]), 1.0, 0.0)
                    + jnp.where(pred_fn(keys_ref[2 * jj + 1]), 1.0, 0.0))
        cnt = lax.fori_loop(0, n_pairs, body, jnp.zeros((TQ, TQ), f32))
        return jnp.sum(cnt, axis=0, keepdims=True)

    def bits_body(state):
        t, prefix, n_ge = state
        for u in range(RADIX_BITS_PER_CHECK):
            cand = prefix ^ jnp.left_shift(jnp.int32(1), 31 - (t + u))
            cnt = count(lambda kj: kj >= cand)
            prefix = jnp.where(cnt >= topk, cand, prefix)
            n_ge = jnp.where(cnt >= topk, cnt, n_ge)
        return t + RADIX_BITS_PER_CHECK, prefix, n_ge

    n_all = (2 * n_pairs * TQ).astype(f32)
    _, thr, n_ge = lax.while_loop(lambda st: (st[0] < 32) & (jnp.max(st[2]) > topk), bits_body,
                                  (jnp.int32(0), jnp.full((1, TQ), INT_MIN, i32), jnp.full((1, TQ), n_all, f32)))

    @pl.when(jnp.max(n_ge) > topk)
    def _():
        need = topk - count(lambda kj: kj > thr)

        def fix_body(j, running):
            kj = keys_ref[j]
            tie = jnp.where(kj == thr, 1.0, 0.0)
            before = jnp.dot(tri_ref[...], tie.astype(bf16), preferred_element_type=f32) + running
            keys_ref[j] = jnp.where(before >= need, jnp.where(kj == thr, kj - 1, kj), kj)
            return running + jnp.sum(tie, axis=0, keepdims=True)

        lax.fori_loop(0, 2 * n_pairs, fix_body, jnp.zeros((1, TQ), f32))

    qs_ref[0:HEAD_DIM, :] = (qt_ref[0, 0].astype(f32) * (HEAD_DIM ** -0.5 * LOG2E)).astype(bf16)
    qs_ref[HEAD_DIM:K_COLS, :] = qx_ref[...]
    _masked_flash(i, qs_ref,
                  lambda j, c: k_ref[0, pl.ds(pl.multiple_of(j * TQ, TQ), TQ), :],
                  lambda j, c: vt_ref[0, j],
                  lambda j, c: jnp.where(keys_ref[j] >= thr, 0.0, NEG),
                  delta_ref, m_ref, acc_ref, s_ref, mt_ref, al_ref, chunk, True)
    acc = acc_ref[...]
    ot = acc[0:HEAD_DIM] / acc[HEAD_DIM:HEAD_DIM + 1]
    o_ref[0] = jnp.concatenate([ot[:, h * TQ:(h + 1) * TQ].T for h in range(nh)], axis=1).astype(o_ref.dtype)


def _dsa_attention(qt, k, v, qit, ki, wi, near, far):
    bn, nt, _, width = qt.shape
    s = nt * TQ
    nh = width // TQ
    hq = nh * HEAD_DIM
    nih = qit.shape[3] // TQ
    topk = min(C_TOPK_MAX, s // 4)
    wit = wi.reshape(bn, nt, TQ, nih).transpose(0, 1, 3, 2)
    tri = np.tril(np.ones((TQ, TQ), np.float32), -1)
    kern = functools.partial(_dsa_kernel, topk=topk, chunk=512)
    nt_scr = max(nt, -(-topk // TQ))
    return pl.pallas_call(
        kern,
        grid=(bn, nt),
        in_specs=[pl.BlockSpec((1, 1, HEAD_DIM, width), lambda b, i: (b, i, 0, 0)),
                  pl.BlockSpec((1, s, K_COLS), lambda b, i: (b, 0, 0)),
                  pl.BlockSpec((1, nt, V_ROWS, TQ), lambda b, i: (b, 0, 0, 0)),
                  pl.BlockSpec((1, 1, IDX_DIM, nih * TQ), lambda b, i: (b, i, 0, 0)),
                  pl.BlockSpec((1, s, IDX_DIM), lambda b, i: (b, 0, 0)),
                  pl.BlockSpec((1, 1, nih, TQ), lambda b, i: (b, i, 0, 0)),
                  pl.BlockSpec((K_COLS - HEAD_DIM, width), lambda b, i: (0, 0)),
                  pl.BlockSpec((2, TQ, width), lambda b, i: (0, 0, 0)),
                  pl.BlockSpec((TQ, TQ), lambda b, i: (0, 0))],
        out_specs=pl.BlockSpec((1, TQ, hq), lambda b, i: (b, i, 0)),
        out_shape=jax.ShapeDtypeStruct((bn, s, hq), bf16),
        scratch_shapes=[pltpu.VMEM((nt_scr, TQ, TQ), i32), pltpu.VMEM((K_COLS, width), bf16),
                        pltpu.VMEM((1, width), f32), pltpu.VMEM((V_ROWS, width), f32),
                        pltpu.VMEM((2, TQ, width), f32), pltpu.VMEM((2, 1, width), f32),
                        pltpu.VMEM((2, 1, width), f32)],
        compiler_params=_cparams(("parallel", "arbitrary")),
        name="dsa_attention",
    )(qt, k, v, qit, ki, wit, *_flash_operands(near, far), jnp.asarray(tri, bf16))


def _heads_first(a, g):
    bn, s, _ = a.shape
    return a.reshape(bn, s, g, HEAD_DIM).transpose(0, 2, 1, 3)


def _even_layer(x, table_t, w_in, sinks, cmpk_pos, cmpk_w1, cmpk_w2, cmpv_pos, cmpv_w1, cmpv_w2,
                w_out, ln1_g, ln1_b, ffn_gate, ffn_up, ffn_down, ln2_g, ln2_b):
    bn, s, d = x.shape
    xf = x.reshape(bn * s, d)
    a_q, a_kv, b_q, b_kv = A_HEADS * HEAD_DIM, A_KV * HEAD_DIM, B_HEADS * HEAD_DIM, B_KV * HEAD_DIM
    n_main = a_q + 2 * a_kv + b_q + 6 * b_kv
    rb = B_HEADS // B_KV
    w_tail = jnp.pad(w_in[:, n_main:], ((0, 0), (0, 128 - 3 * B_HEADS))).astype(bf16)
    cols = np.cumsum([0, a_q, a_kv, a_kv, b_q] + [b_kv] * 6)
    w_kv = jnp.concatenate([w_in[:, cols[1]:cols[3]], w_in[:, cols[4]:cols[6]], w_in[:, cols[8]:cols[10]]],
                           axis=1).astype(bf16)
    main, tail, qt_a, qt_b, ks1, vst = _proj_in(
        x, [w_kv, w_tail, w_in[:, cols[0]:cols[1]].T.astype(bf16), w_in[:, cols[3]:cols[4]].T.astype(bf16),
            w_in[:, cols[6]:cols[7]].astype(bf16), w_in[:, cols[7]:cols[8]].T.astype(bf16)],
        [("plain", w_kv.shape[1], bf16, 0), ("plain", 128, f32, 0),
         ("queries", A_KV, A_HEADS // A_KV, HEAD_DIM), ("queries", B_KV, rb, HEAD_DIM),
         ("keys", B_KV, 1, HEAD_DIM), ("values", B_KV, 1, HEAD_DIM)])
    main = main.reshape(bn, s, w_kv.shape[1])
    gates = tail[:, :3 * B_HEADS]
    gates = gates.reshape(bn, s, 3, B_KV, rb).transpose(0, 3, 1, 2, 4).reshape(bn, B_KV, s, 3 * rb)
    kcols = np.cumsum([0, a_kv, a_kv] + [b_kv] * 4)
    ka, va, kc_in, vc_in, kwn, vwn = [main[:, :, kcols[t]:kcols[t + 1]] for t in range(6)]

    near_idx = _bucket_of(np.arange(TQ)[:, None] + TQ - np.arange(2 * TQ)[None, :])
    near = _bias_expand(table_t, near_idx, TQ)
    pad_b = -(-(B_WINDOW - 1) // TQ) * TQ
    win_idx = _bucket_of(np.arange(TQ)[:, None] + pad_b - np.arange(pad_b + TQ)[None, :])
    bias_win = _bias_expand(table_t[A_HEADS:A_HEADS + B_HEADS], win_idx, TQ)
    nc = s // CMP_STRIDE
    nt = s // TQ
    cmp_idx = _bucket_of((np.arange(nt)[:, None, None] * TQ + np.arange(TQ)[None, None, :])
                         - (np.arange(nc)[None, :, None] * CMP_STRIDE + CMP_BLOCK - 1)).reshape(nt * nc, TQ)
    bias_c = _bias_expand(table_t[A_HEADS:A_HEADS + B_HEADS], cmp_idx, min(nt * nc, 512))
    bias_c = bias_c.reshape(B_HEADS, nt, nc, TQ)
    far = table_t[:, NUM_BUCKETS - 1]

    out_a =_banded_gqa(qt_a, _heads_first(ka, A_KV), _heads_first(va, A_KV), near[:A_HEADS], sinks, A_WINDOW, bf16)
    o_win = _banded_gqa(qt_b, _heads_first(kwn, B_KV), _heads_first(vwn, B_KV), bias_win, None, B_WINDOW, f32)
    kc = _nsa_compress(_heads_first(kc_in, B_KV), cmpk_pos, cmpk_w1, cmpk_w2)
    vc = _nsa_compress(_heads_first(vc_in, B_KV), cmpv_pos, cmpv_w1, cmpv_w2)
    out_b = _nsa_mix(qt_b, kc, vc, bias_c, ks1, vst,
                     near[A_HEADS:A_HEADS + B_HEADS], far[A_HEADS:A_HEADS + B_HEADS], gates, o_win)
    mix = jnp.concatenate([out_a, out_b], axis=-1).reshape(bn * s, a_q + b_q)
    x1 = _proj_res_ln(mix, w_out.astype(bf16), xf, ln1_g, ln1_b)
    x2 = _ffn_res_ln(x1, ffn_gate.astype(bf16), ffn_up.astype(bf16), ffn_down.astype(bf16), ln2_g, ln2_b)
    return x2.reshape(bn, s, d), near, far


def _odd_layer(x, near, far, w_in, w_out, ln1_g, ln1_b, router, exp_gate, exp_up, exp_down, ln2_g, ln2_b):
    bn, s, d = x.shape
    xf = x.reshape(bn * s, d)
    c_q = C_HEADS * HEAD_DIM
    n_main = c_q + 2 * HEAD_DIM + IDX_HEADS * IDX_DIM
    w_tail = jnp.pad(w_in[:, n_main:], ((0, 0), (0, 128 - IDX_DIM - IDX_HEADS))).astype(bf16)
    tail, qt, qit, k1, vt = _proj_in(
        x, [w_tail, w_in[:, :c_q].T.astype(bf16), w_in[:, c_q + 2 * HEAD_DIM:n_main].T.astype(bf16),
            w_in[:, c_q:c_q + HEAD_DIM].astype(bf16), w_in[:, c_q + HEAD_DIM:c_q + 2 * HEAD_DIM].T.astype(bf16)],
        [("plain", 128, f32, 0), ("queries", 1, C_HEADS, HEAD_DIM), ("queries", 1, IDX_HEADS, IDX_DIM),
         ("keys", 1, 1, HEAD_DIM), ("values", 1, 1, HEAD_DIM)])
    tail = tail.reshape(bn, s, 128)
    ki = tail[:, :, :IDX_DIM].astype(bf16)
    wi = tail[:, :, IDX_DIM:IDX_DIM + IDX_HEADS]
    mix = _dsa_attention(qt[:, 0], k1[:, 0], vt[:, 0], qit[:, 0], ki, wi, near[:C_HEADS], far[:C_HEADS])
    mix = mix.reshape(bn * s, c_q)
    x1 = _proj_res_ln(mix, w_out.astype(bf16), xf, ln1_g, ln1_b)
    x2 = _moe_res_ln(x1, router, exp_gate.astype(bf16), exp_up.astype(bf16), exp_down.astype(bf16), ln2_g, ln2_b)
    return x2.reshape(bn, s, d)


def kernel(x, rel_bias, l0_w_in, l0_sinks, l0_cmpk_pos, l0_cmpk_w1, l0_cmpk_w2, l0_cmpv_pos, l0_cmpv_w1,
           l0_cmpv_w2, l0_w_out, l0_ln1_g, l0_ln1_b, l0_ffn_gate, l0_ffn_up, l0_ffn_down, l0_ln2_g, l0_ln2_b,
           l1_w_in, l1_w_out, l1_ln1_g, l1_ln1_b, l1_router, l1_exp_gate, l1_exp_up, l1_exp_down, l1_ln2_g,
           l1_ln2_b):
    table_t = rel_bias.T.astype(f32)
    x, near, far = _even_layer(x, table_t, l0_w_in, l0_sinks, l0_cmpk_pos, l0_cmpk_w1, l0_cmpk_w2, l0_cmpv_pos,
                               l0_cmpv_w1, l0_cmpv_w2, l0_w_out, l0_ln1_g, l0_ln1_b, l0_ffn_gate, l0_ffn_up,
                               l0_ffn_down, l0_ln2_g, l0_ln2_b)
    return _odd_layer(x, near, far, l1_w_in, l1_w_out, l1_ln1_g, l1_ln1_b, l1_router, l1_exp_gate, l1_exp_up,
                      l1_exp_down, l1_ln2_g, l1_ln2_b)
```

```python
import functools
import math

import numpy as np
import jax
import jax.numpy as jnp
from jax import lax
from jax.experimental import pallas as pl
from jax.experimental.pallas import tpu as pltpu

f32 = jnp.float32
bf16 = jnp.bfloat16
i32 = jnp.int32

D_MODEL = 1024
HEAD_DIM = 64
NUM_BUCKETS = 32
MAX_DISTANCE = 128
A_HEADS, A_KV, A_WINDOW = 8, 2, 128
B_HEADS, B_KV, B_WINDOW = 8, 2, 512
CMP_BLOCK, CMP_STRIDE = 32, 16
SLC_BLOCK, SLC_TOPN = 64, 16
C_HEADS, IDX_HEADS, IDX_DIM, C_TOPK_MAX = 16, 8, 32, 256
D_FF, N_EXPERTS = 2816, 8
DEPTH = 2
ALPHA = (2.0 * DEPTH) ** 0.25
LN_EPS = 1e-5
NEG = -1e30
POS_BIG = 1e30
INT_MIN = -(2 ** 31)

TQ = 128
FF_CHUNK = 256
VMEM_LIMIT = 56 * 1024 * 1024


def _cparams(sem):
    return pltpu.CompilerParams(dimension_semantics=sem, vmem_limit_bytes=VMEM_LIMIT)


def _bucket_table(max_dist):
    n = np.arange(max_dist + 1)
    max_exact = NUM_BUCKETS // 2
    nf = np.maximum(n, 1).astype(np.float64)
    large = max_exact + (np.log(nf / max_exact) / math.log(MAX_DISTANCE / max_exact)
                         * (NUM_BUCKETS - max_exact)).astype(np.int64)
    large = np.minimum(large, NUM_BUCKETS - 1)
    return np.where(n < max_exact, n, large).astype(np.int32)


def _bucket_of(dist):
    dist = np.maximum(dist, 0)
    return _bucket_table(int(dist.max()))[dist]


def _proj_in_kernel(*refs, groups):
    ng = len(groups)
    x_ref = refs[0]
    wq_refs = refs[1:1 + ng]
    oq_refs = refs[1 + ng:]
    xb = x_ref[...].astype(bf16)
    tm = xb.shape[0]
    for wq_ref, oq_ref, (kind, gb, hb, dh) in zip(wq_refs, oq_refs, groups):
        if kind == "plain":
            oq_ref[...] = jnp.dot(xb, wq_ref[...], preferred_element_type=f32).astype(oq_ref.dtype)
            continue
        if kind == "keys":
            kk = jnp.dot(xb, wq_ref[...], preferred_element_type=f32).astype(bf16)
            ones = jnp.where(lax.broadcasted_iota(i32, (tm, K_COLS - dh), 1) < 2, 1.0, 0.0).astype(bf16)
            for g in range(gb):
                oq_ref[0, g, :, 0:dh] = kk[:, g * dh:(g + 1) * dh]
                oq_ref[0, g, :, dh:K_COLS] = ones
            continue
        qt = lax.dot_general(wq_ref[...], xb, (((1,), (1,)), ((), ())), preferred_element_type=f32).astype(bf16)
        ones = jnp.where(lax.broadcasted_iota(i32, (V_ROWS - HEAD_DIM, TQ), 0) < 1, 1.0, 0.0).astype(bf16)
        for g in range(gb):
            for h in range(hb):
                rows = slice((g * hb + h) * dh, (g * hb + h + 1) * dh)
                for t in range(tm // TQ):
                    oq_ref[0, g, t, 0:dh, h * TQ:(h + 1) * TQ] = qt[rows, t * TQ:(t + 1) * TQ]
                    if kind == "values":
                        oq_ref[0, g, t, dh:V_ROWS, :] = ones


def _proj_in(x3, w_groups, groups, tm=512):
    bn, s, k = x3.shape
    m = bn * s
    tm = min(tm, s)
    per_b = s // tm
    tiles = tm // TQ
    q_specs, q_shapes = [], []
    for kind, gb, hb, dh in groups:
        if kind == "plain":
            q_specs.append(pl.BlockSpec((tm, gb), lambda i: (i, 0)))
            q_shapes.append(jax.ShapeDtypeStruct((m, gb), hb))
        elif kind == "keys":
            q_specs.append(pl.BlockSpec((1, gb, tm, K_COLS), lambda i: (i // per_b, 0, i % per_b, 0)))
            q_shapes.append(jax.ShapeDtypeStruct((bn, gb, s, K_COLS), bf16))
        else:
            rows = V_ROWS if kind == "values" else dh
            q_specs.append(pl.BlockSpec((1, gb, tiles, rows, hb * TQ), lambda i: (i // per_b, 0, i % per_b, 0, 0)))
            q_shapes.append(jax.ShapeDtypeStruct((bn, gb, s // TQ, rows, hb * TQ), bf16))
    return pl.pallas_call(
        functools.partial(_proj_in_kernel, groups=tuple(groups)),
        grid=(m // tm,),
        in_specs=[pl.BlockSpec((tm, k), lambda i: (i, 0))]
                 + [pl.BlockSpec(w.shape, lambda i: (0, 0)) for w in w_groups],
        out_specs=q_specs,
        out_shape=q_shapes,
        compiler_params=_cparams(("parallel",)),
        name="proj_in",
    )(x3.reshape(m, k), *w_groups)


def _bias_expand_kernel(tab_ref, idx_ref, o_ref):
    h = pl.program_id(0)
    idx = idx_ref[...]
    acc = jnp.zeros(idx.shape, f32)
    for b in range(NUM_BUCKETS):
        acc = jnp.where(idx == b, tab_ref[h, b], acc)
    o_ref[0] = acc


def _bias_expand(table_t, idx, tr):
    nh = table_t.shape[0]
    r, c = idx.shape
    return pl.pallas_call(
        _bias_expand_kernel,
        grid=(nh, r // tr),
        in_specs=[pl.BlockSpec(memory_space=pltpu.SMEM),
                  pl.BlockSpec((tr, c), lambda h, i: (i, 0))],
        out_specs=pl.BlockSpec((1, tr, c), lambda h, i: (h, i, 0)),
        out_shape=jax.ShapeDtypeStruct((nh, r, c), f32),
        compiler_params=_cparams(("parallel", "arbitrary")),
        name="bias_expand",
    )(table_t, jnp.asarray(idx, dtype=i32))


def _layer_norm(z, g, b):
    mu = jnp.mean(z, axis=-1, keepdims=True)
    zc = z - mu
    var = jnp.mean(zc * zc, axis=-1, keepdims=True)
    return zc * lax.rsqrt(var + LN_EPS) * g + b


def _proj_ln_kernel(mix_ref, w_ref, x_ref, g_ref, b_ref, o_ref):
    y = jnp.dot(mix_ref[...], w_ref[...], preferred_element_type=f32)
    o_ref[...] = _layer_norm(ALPHA * x_ref[...] + y, g_ref[...], b_ref[...])


def _proj_res_ln(mix, w, x, g, b, tm=512):
    m, k = mix.shape
    d = w.shape[1]
    return pl.pallas_call(
        _proj_ln_kernel,
        grid=(m // tm,),
        in_specs=[pl.BlockSpec((tm, k), lambda i: (i, 0)),
                  pl.BlockSpec((k, d), lambda i: (0, 0)),
                  pl.BlockSpec((tm, d), lambda i: (i, 0)),
                  pl.BlockSpec((1, d), lambda i: (0, 0)),
                  pl.BlockSpec((1, d), lambda i: (0, 0))],
        out_specs=pl.BlockSpec((tm, d), lambda i: (i, 0)),
        out_shape=jax.ShapeDtypeStruct((m, d), f32),
        compiler_params=_cparams(("parallel",)),
        name="proj_res_ln",
    )(mix, w, x, g.reshape(1, d), b.reshape(1, d))


def _swiglu_accumulate(xb, wg_ref, wu_ref, wd_ref, acc_ref, widx):
    d_ff = wg_ref.shape[-1]
    for c in range(d_ff // FF_CHUNK):
        cols = slice(c * FF_CHUNK, (c + 1) * FF_CHUNK)
        gate = jnp.dot(xb, wg_ref[widx + (slice(None), cols)], preferred_element_type=f32)
        up = jnp.dot(xb, wu_ref[widx + (slice(None), cols)], preferred_element_type=f32)
        h = jax.nn.silu(gate) * up
        acc_ref[...] += jnp.dot(h.astype(bf16), wd_ref[widx + (cols, slice(None))], preferred_element_type=f32)


def _ffn_kernel(x_ref, wg_ref, wu_ref, wd_ref, g_ref, b_ref, o_ref, acc_ref):
    x = x_ref[...]
    acc_ref[...] = jnp.zeros_like(acc_ref)
    _swiglu_accumulate(x.astype(bf16), wg_ref, wu_ref, wd_ref, acc_ref, ())
    o_ref[...] = _layer_norm(ALPHA * x + acc_ref[...], g_ref[...], b_ref[...])


def _ffn_res_ln(x, wg, wu, wd, g, b, tm=512):
    m, d = x.shape
    d_ff = wg.shape[1]
    return pl.pallas_call(
        _ffn_kernel,
        grid=(m // tm,),
        in_specs=[pl.BlockSpec((tm, d), lambda i: (i, 0)),
                  pl.BlockSpec((d, d_ff), lambda i: (0, 0)),
                  pl.BlockSpec((d, d_ff), lambda i: (0, 0)),
                  pl.BlockSpec((d_ff, d), lambda i: (0, 0)),
                  pl.BlockSpec((1, d), lambda i: (0, 0)),
                  pl.BlockSpec((1, d), lambda i: (0, 0))],
        out_specs=pl.BlockSpec((tm, d), lambda i: (i, 0)),
        out_shape=jax.ShapeDtypeStruct((m, d), f32),
        scratch_shapes=[pltpu.VMEM((tm, d), f32)],
        compiler_params=_cparams(("parallel",)),
        name="ffn_res_ln",
    )(x, wg, wu, wd, g.reshape(1, d), b.reshape(1, d))


MOE_CAP = 288


def _router_gates(x, r_ref, lane):
    logits = jnp.dot(x, r_ref[...], preferred_element_type=f32)
    logits = jnp.where(lane < N_EXPERTS, logits, -jnp.inf)
    m1 = jnp.max(logits, axis=-1, keepdims=True)
    i1 = jnp.min(jnp.where(logits == m1, lane, 128), axis=-1, keepdims=True)
    rest = jnp.where(lane == i1, -jnp.inf, logits)
    m2 = jnp.max(rest, axis=-1, keepdims=True)
    i2 = jnp.min(jnp.where(rest == m2, lane, 128), axis=-1, keepdims=True)
    e2 = jnp.exp(m2 - m1)
    g1 = 1.0 / (1.0 + e2)
    return jnp.where(lane == i1, g1, 0.0) + jnp.where(lane == i2, e2 * g1, 0.0)


def _moe_kernel(x_ref, r_ref, tri_ref, wg_ref, wu_ref, wd_ref, g_ref, b_ref, o_ref,
                split_ref, post_ref, yc_ref):
    e = pl.program_id(1)
    tm = x_ref.shape[0]

    @pl.when(e == 0)
    def _():
        comb = _router_gates(x_ref[...], r_ref, lax.broadcasted_iota(i32, (tm, 128), 1))
        hi = comb.astype(bf16)
        rest = comb - hi.astype(f32)
        mid = rest.astype(bf16)
        split_ref[0] = hi
        split_ref[1] = mid
        split_ref[2] = (rest - mid.astype(f32)).astype(bf16)
        chosen = jnp.where(comb > 0.0, 1.0, 0.0)
        pos = jnp.dot(tri_ref[...], chosen.astype(bf16), preferred_element_type=f32)
        pos = jnp.where(comb > 0.0, pos, -1.0)
        for t in range(tm // 128):
            rows = slice(t * 128, (t + 1) * 128)
            post_ref[:, rows] = pos[rows].T[0:N_EXPERTS]
        hi32 = jnp.floor((pos + 1.0) * (1.0 / 32.0))
        split_ref[3] = hi32.astype(bf16)
        split_ref[4] = (pos + 1.0 - 32.0 * hi32).astype(bf16)
        o_ref[...] = jnp.zeros_like(o_ref)

    pos_row = post_ref[pl.ds(e, 1), :].astype(i32)
    n_tok = jnp.sum(jnp.where(pos_row >= 0, 1.0, 0.0)).astype(i32)
    pick = jnp.where(lax.broadcasted_iota(i32, (128, 128), 0) == e, 1.0, 0.0).astype(bf16)

    def column(t):
        return jnp.dot(split_ref[t], pick, preferred_element_type=f32)

    gate_b = column(0) + column(1) + column(2)
    pos_b = (32.0 * column(3) + column(4) - 1.0).astype(i32)
    lane = lax.broadcasted_iota(i32, (tm, 128), 1)

    def pass_body(p, carry):
        base = p * MOE_CAP
        gather = jnp.where(pos_row == base + lax.broadcasted_iota(i32, (MOE_CAP, tm), 0), 1.0, 0.0).astype(bf16)
        xc = jnp.dot(gather, x_ref[...].astype(bf16), preferred_element_type=f32).astype(bf16)
        yc_ref[...] = jnp.zeros_like(yc_ref)
        _swiglu_accumulate(xc, wg_ref, wu_ref, wd_ref, yc_ref, (0,))
        scatter = jnp.concatenate([jnp.where(pos_b == base + t * 128 + lane, 1.0, 0.0).astype(bf16)
                                   for t in range(-(-MOE_CAP // 128))], axis=1)[:, :MOE_CAP]
        y = jnp.dot(scatter, yc_ref[...].astype(bf16), preferred_element_type=f32)
        o_ref[...] += jnp.concatenate([gate_b] * (y.shape[1] // 128), axis=1) * y
        return carry

    lax.fori_loop(0, (n_tok + MOE_CAP - 1) // MOE_CAP, pass_body, 0)

    @pl.when(e == N_EXPERTS - 1)
    def _():
        o_ref[...] = _layer_norm(ALPHA * x_ref[...] + o_ref[...], g_ref[...], b_ref[...])


def _moe_res_ln(x, router, wg, wu, wd, g, b, tm=1024):
    m, d = x.shape
    ne, _, d_ff = wg.shape
    tm = min(tm, m)
    router_p = jnp.pad(router, ((0, 0), (0, 128 - ne)))
    tri = jnp.asarray(np.tril(np.ones((tm, tm), np.float32), -1), bf16)
    single = pl.Buffered(1)
    return pl.pallas_call(
        _moe_kernel,
        grid=(m // tm, ne),
        in_specs=[pl.BlockSpec((tm, d), lambda i, e: (i, 0), pipeline_mode=single),
                  pl.BlockSpec((d, 128), lambda i, e: (0, 0), pipeline_mode=single),
                  pl.BlockSpec((tm, tm), lambda i, e: (0, 0), pipeline_mode=single),
                  pl.BlockSpec((1, d, d_ff), lambda i, e: (e, 0, 0)),
                  pl.BlockSpec((1, d, d_ff), lambda i, e: (e, 0, 0)),
                  pl.BlockSpec((1, d_ff, d), lambda i, e: (e, 0, 0)),
                  pl.BlockSpec((1, d), lambda i, e: (0, 0)),
                  pl.BlockSpec((1, d), lambda i, e: (0, 0))],
        out_specs=pl.BlockSpec((tm, d), lambda i, e: (i, 0)),
        out_shape=jax.ShapeDtypeStruct((m, d), f32),
        scratch_shapes=[pltpu.VMEM((5, tm, 128), bf16), pltpu.VMEM((N_EXPERTS, tm), f32),
                        pltpu.VMEM((MOE_CAP, d), f32)],
        compiler_params=_cparams(("parallel", "arbitrary")),
        name="moe_res_ln",
    )(x, router_p, tri, wg, wu, wd, g.reshape(1, d), b.reshape(1, d))


def _compress_kernel(blk_ref, pos_ref, w1_ref, w2_ref, o_ref):
    xb = (blk_ref[0].astype(f32) + pos_ref[...]).astype(bf16)
    h = jax.nn.gelu(jnp.dot(xb, w1_ref[...], preferred_element_type=f32))
    o_ref[0] = jnp.dot(h.astype(bf16), w2_ref[...], preferred_element_type=f32).astype(o_ref.dtype)


def _nsa_compress(kv, pos, w1, w2):
    bn, g, s, dh = kv.shape
    nc = s // CMP_STRIDE
    chunks = kv.reshape(bn, g, nc, CMP_STRIDE * dh)
    nxt = jnp.pad(chunks[:, :, 1:], ((0, 0), (0, 0), (0, 1), (0, 0)))
    blocks = jnp.concatenate([chunks, nxt], axis=-1).reshape(bn * g, nc, CMP_BLOCK * dh)
    cin = CMP_BLOCK * dh
    hid = w1.shape[1]
    out = pl.pallas_call(
        _compress_kernel,
        grid=(bn * g,),
        in_specs=[pl.BlockSpec((1, nc, cin), lambda i: (i, 0, 0)),
                  pl.BlockSpec((1, cin), lambda i: (0, 0)),
                  pl.BlockSpec((cin, hid), lambda i: (0, 0)),
                  pl.BlockSpec((hid, dh), lambda i: (0, 0))],
        out_specs=pl.BlockSpec((1, nc, dh), lambda i: (i, 0, 0)),
        out_shape=jax.ShapeDtypeStruct((bn * g, nc, dh), bf16),
        compiler_params=_cparams(("parallel",)),
        name="nsa_compress",
    )(blocks, pos.reshape(1, cin), w1.astype(bf16), w2.astype(bf16))
    return out.reshape(bn, g, nc, dh)


V_ROWS = HEAD_DIM + 16


LOG2E = 1.4426950408889634
K_COLS = 2 * HEAD_DIM


def _masked_flash(i, qs_ref, key_tile, value_tile, mask_bias, delta_ref, m_ref, acc_ref, s_ref, mt_ref, al_ref, chunk,
                  shared_keys):
    width = qs_ref.shape[1]
    n_chunks = width // chunk
    m_ref[...] = jnp.full(m_ref.shape, NEG, f32)
    acc_ref[...] = jnp.zeros(acc_ref.shape, f32)

    def scores_chunk(j, delta, c, shared):
        slot = j % 2
        if c == 0 or not shared_keys:
            shared["kt"] = key_tile(j, c)
            shared["maskb"] = jnp.concatenate([mask_bias(j, c)] * (chunk // TQ), axis=1)
        cols = slice(c * chunk, (c + 1) * chunk)
        s = jnp.dot(shared["kt"], qs_ref[:, cols], preferred_element_type=f32) + shared["maskb"]
        if delta is not None:
            s = s + delta_ref[delta, :, cols]
        s_ref[slot, :, cols] = s
        m_old = m_ref[:, cols]
        m_new = jnp.maximum(m_old, jnp.max(s, axis=0, keepdims=True))
        m_ref[:, cols] = m_new
        mt_ref[slot, :, cols] = m_new
        al_ref[slot, :, cols] = jnp.exp2(m_old - m_new)

    def values_chunk(j, c, shared):
        slot = j % 2
        if c == 0 or not shared_keys:
            shared["vt"] = value_tile(j, c)
        cols = slice(c * chunk, (c + 1) * chunk)
        p = jnp.exp2(s_ref[slot, :, cols] - mt_ref[slot, :, cols])
        acc_ref[:, cols] = (al_ref[slot, :, cols] * acc_ref[:, cols]
                            + jnp.dot(shared["vt"], p.astype(bf16), preferred_element_type=f32))

    def scores(j, delta):
        shared = {}
        for c in range(n_chunks):
            scores_chunk(j, delta, c, shared)

    def values(j):
        shared = {}
        for c in range(n_chunks):
            values_chunk(j, c, shared)

    def values_then_scores(jv, js, delta):
        sv, ss = {}, {}
        for c in range(n_chunks):
            values_chunk(jv, c, sv)
            scores_chunk(js, delta, c, ss)

    @pl.when(i >= 2)
    def _():
        scores(0, None)

        def pipe_body(j, carry):
            values_then_scores(j - 1, j, None)
            return carry

        lax.fori_loop(1, i - 1, pipe_body, 0)
        values_then_scores(i - 2, i - 1, 0)
        values_then_scores(i - 1, i, 1)

    @pl.when(i == 1)
    def _():
        scores(0, 0)
        values_then_scores(0, 1, 1)

    @pl.when(i == 0)
    def _():
        scores(0, 1)

    values(i)


def _flash_operands(near, far):
    nh = near.shape[0]
    width = nh * TQ
    far2 = jnp.repeat(far * LOG2E, TQ).reshape(1, width)
    hi = far2.astype(bf16)
    lo = (far2 - hi.astype(f32)).astype(bf16)
    extra = jnp.concatenate([hi, lo, jnp.zeros((K_COLS - HEAD_DIM - 2, width), bf16)], axis=0)
    neart = near.reshape(nh, TQ, 2, TQ).transpose(2, 3, 0, 1).reshape(2, TQ, width) * LOG2E
    neart = neart - (hi.astype(f32) + lo.astype(f32))
    causal = np.where(np.arange(TQ)[:, None] <= np.arange(TQ)[None, :], 0.0, NEG).astype(np.float32)
    return extra, jnp.stack([neart[0], neart[1] + jnp.asarray(np.tile(causal, (1, nh)))], axis=0)


def _value_tiles(v):
    lead = v.shape[:-2]
    nt = v.shape[-2] // TQ
    vt = jnp.swapaxes(v.reshape(lead + (nt, TQ, HEAD_DIM)), -1, -2)
    return jnp.concatenate([vt, jnp.ones(lead + (nt, 1, TQ), vt.dtype),
                            jnp.zeros(lead + (nt, V_ROWS - HEAD_DIM - 1, TQ), vt.dtype)], axis=-2)


def _banded_kernel(qt_ref, k_ref, vt_ref, bias_ref, neg_ref, sink_ref, o_ref, *, rep, kt_tiles, use_sinks):
    i = pl.program_id(1)
    kw = kt_tiles * TQ
    outs = []
    for g in range(qt_ref.shape[1]):
        qs = jnp.concatenate([(qt_ref[0, g, 0].astype(f32) * (HEAD_DIM ** -0.5)).astype(bf16), neg_ref[...]], axis=0)
        k = k_ref[0, g, pl.ds(pl.multiple_of(i * TQ, TQ), kw), :]
        s = jnp.dot(k, qs, preferred_element_type=f32) + bias_ref[g]
        m = jnp.max(s, axis=0, keepdims=True)
        if use_sinks:
            m = jnp.maximum(m, sink_ref[g])
        p = jnp.exp(s - m).astype(bf16)
        vt = jnp.concatenate([vt_ref[0, g, i + t] for t in range(kt_tiles)], axis=1)
        acc = jnp.dot(vt, p, preferred_element_type=f32)
        l = acc[HEAD_DIM:HEAD_DIM + 1]
        if use_sinks:
            l = l + jnp.exp(sink_ref[g] - m)
        ot = acc[0:HEAD_DIM] / l
        outs += [ot[:, r * TQ:(r + 1) * TQ].T for r in range(rep)]
    o_ref[0] = jnp.concatenate(outs, axis=1).astype(o_ref.dtype)


def _banded_gqa(qt, k, v, bias, sinks, window, out_dtype):
    bn, g, nt, _, width = qt.shape
    rep = width // TQ
    s = nt * TQ
    pad = -(-(window - 1) // TQ) * TQ
    kw = pad + TQ
    kd = 2 * HEAD_DIM
    flag = jnp.broadcast_to((jnp.arange(pad + s) < pad).astype(k.dtype)[:, None], (bn, g, pad + s, 1))
    kp = jnp.concatenate([jnp.pad(k, ((0, 0), (0, 0), (pad, 0), (0, 0))), flag,
                          jnp.zeros((bn, g, pad + s, kd - HEAD_DIM - 1), k.dtype)], axis=-1)
    vt = _value_tiles(jnp.pad(v, ((0, 0), (0, 0), (pad, 0), (0, 0))))
    neg = np.zeros((kd - HEAD_DIM, width), np.float32)
    neg[0] = NEG
    dist = np.arange(TQ)[:, None] + pad - np.arange(kw)[None, :]
    band = jnp.asarray((dist >= 0) & (dist < window))
    bias_t = jnp.where(band, bias, NEG).reshape(g, rep, TQ, kw).transpose(0, 3, 1, 2).reshape(g, kw, width)
    use_sinks = sinks is not None
    sink_rows = (jnp.repeat(sinks.astype(f32), TQ) if use_sinks else jnp.zeros((g * width,), f32)).reshape(g, 1, width)
    kern = functools.partial(_banded_kernel, rep=rep, kt_tiles=kw // TQ, use_sinks=use_sinks)
    rw = rep * HEAD_DIM
    return pl.pallas_call(
        kern,
        grid=(bn, nt),
        in_specs=[pl.BlockSpec((1, g, 1, HEAD_DIM, width), lambda b, i: (b, 0, i, 0, 0)),
                  pl.BlockSpec((1, g, pad + s, kd), lambda b, i: (b, 0, 0, 0)),
                  pl.BlockSpec((1, g, (pad + s) // TQ, V_ROWS, TQ), lambda b, i: (b, 0, 0, 0, 0)),
                  pl.BlockSpec((g, kw, width), lambda b, i: (0, 0, 0)),
                  pl.BlockSpec((kd - HEAD_DIM, width), lambda b, i: (0, 0)),
                  pl.BlockSpec((g, 1, width), lambda b, i: (0, 0, 0))],
        out_specs=pl.BlockSpec((1, TQ, g * rw), lambda b, i: (b, i, 0)),
        out_shape=jax.ShapeDtypeStruct((bn, s, g * rw), out_dtype),
        compiler_params=_cparams(("parallel", "arbitrary")),
        name="banded_gqa_w%d" % window,
    )(qt, kp, vt, bias_t, jnp.asarray(neg, bf16), sink_rows)


def _nsa_kernel(qt_ref, kc_ref, vct_ref, biasc_ref, ovt_ref, ks_ref, vst_ref, qx_ref, delta_ref, gt_ref, gn_ref,
                owin_ref, o_ref, qs_ref, selb_ref, m_ref, acc_ref, s_ref, mt_ref, al_ref, *, rep, n_sel):
    i = pl.program_id(1)
    t0 = i * TQ
    groups = qt_ref.shape[1]
    nc = kc_ref.shape[2]
    nblk = ovt_ref.shape[0]
    width = rep * TQ
    cmp_end = lax.broadcasted_iota(i32, (nc, TQ), 0) * CMP_STRIDE + (CMP_BLOCK - 1)
    valid_c = t0 + lax.broadcasted_iota(i32, (nc, TQ), 1) >= cmp_end
    blk = lax.broadcasted_iota(i32, (nblk, TQ), 0)
    tq = t0 + lax.broadcasted_iota(i32, (nblk, TQ), 1)
    cur = tq // SLC_BLOCK
    forced = (blk == 0) | (blk == cur) | (blk == cur - 1)
    admitted = blk * SLC_BLOCK <= tq

    o_cmps = []
    for g in range(groups):
        qf = qt_ref[0, g, 0].astype(f32)
        qs = (qf * (HEAD_DIM ** -0.5)).astype(bf16)
        qs_ref[0:HEAD_DIM, g * width:(g + 1) * width] = (qf * (HEAD_DIM ** -0.5 * LOG2E)).astype(bf16)

        s_c = jnp.dot(kc_ref[0, g], qs, preferred_element_type=f32)
        p_cols = []
        p_sum = jnp.zeros((nc, TQ), f32)
        for r in range(rep):
            sr = jnp.where(valid_c, s_c[:, r * TQ:(r + 1) * TQ] + biasc_ref[g * rep + r, 0], NEG)
            m = jnp.max(sr, axis=0, keepdims=True)
            e = jnp.where(valid_c, jnp.exp(sr - m), 0.0)
            l = jnp.sum(e, axis=0, keepdims=True)
            p = e * (1.0 / jnp.maximum(l, 1e-30))
            p_cols.append(p.astype(bf16))
            p_sum = p_sum + p
        o_cmps.append(jnp.dot(vct_ref[0, g], jnp.concatenate(p_cols, axis=1), preferred_element_type=f32))

        imp = jnp.dot(ovt_ref[...], p_sum, preferred_element_type=f32)
        score = jnp.where(forced, POS_BIG, jnp.where(admitted, imp, NEG))
        rank = jnp.zeros((nblk, TQ), i32)
        for kk in range(nblk):
            ck = score[kk:kk + 1, :]
            rank = rank + jnp.where(ck > score, 1, jnp.where(ck == score, jnp.where(blk > kk, 1, 0), 0))
        selb_ref[g] = jnp.where(rank < n_sel, 0.0, NEG)

    per_tile = TQ // SLC_BLOCK

    def mask_bias(j, c):
        rows = [jnp.broadcast_to(selb_ref[c, pl.ds(per_tile * j + t, 1), :], (SLC_BLOCK, TQ))
                for t in range(per_tile)]
        return jnp.concatenate(rows, axis=0)

    qs_ref[HEAD_DIM:K_COLS, :] = qx_ref[...]
    _masked_flash(i, qs_ref,
                  lambda j, c: ks_ref[0, c, pl.ds(pl.multiple_of(j * TQ, TQ), TQ), :],
                  lambda j, c: vst_ref[0, c, j],
                  mask_bias, delta_ref, m_ref, acc_ref, s_ref, mt_ref, al_ref, width, False)
    acc = acc_ref[...]
    o_slc = acc[0:HEAD_DIM] / acc[HEAD_DIM:HEAD_DIM + 1]

    outs = []
    for g in range(groups):
        sig_t = jax.nn.sigmoid(gt_ref[0, g, 0])
        sig_n = jax.nn.sigmoid(gn_ref[0, g])
        for r in range(rep):
            h = g * rep + r
            mixed = (sig_t[r:r + 1] * o_cmps[g][:, r * TQ:(r + 1) * TQ]
                     + sig_t[rep + r:rep + r + 1] * o_slc[:, h * TQ:(h + 1) * TQ])
            o_win = owin_ref[0, :, h * HEAD_DIM:(h + 1) * HEAD_DIM]
            outs.append(mixed.T + sig_n[:, 2 * rep + r:2 * rep + r + 1] * o_win)
    o_ref[0] = jnp.concatenate(outs, axis=1).astype(o_ref.dtype)


def _nsa_mix(qt, kc, vc, bias_c, ks, vs, near, far, gates, o_win):
    bn, g, nt, _, width = qt.shape
    s = nt * TQ
    nc = kc.shape[2]
    rep = width // TQ
    hq = g * rep * HEAD_DIM
    nblk = s // SLC_BLOCK
    n_sel = min(SLC_TOPN, nblk)
    ci = np.arange(nc)[:, None] * CMP_STRIDE
    sj = np.arange(nblk)[None, :] * SLC_BLOCK
    overlap = np.clip(np.minimum(ci + CMP_BLOCK, sj + SLC_BLOCK) - np.maximum(ci, sj), 0, None) / CMP_BLOCK
    overlap[nc - 1] = 0.0
    gates_t = gates.reshape(bn, g, nt, TQ, 3 * rep).transpose(0, 1, 2, 4, 3)
    kern = functools.partial(_nsa_kernel, rep=rep, n_sel=n_sel)
    return pl.pallas_call(
        kern,
        grid=(bn, nt),
        in_specs=[pl.BlockSpec((1, g, 1, HEAD_DIM, width), lambda b, i: (b, 0, i, 0, 0)),
                  pl.BlockSpec((1, g, nc, HEAD_DIM), lambda b, i: (b, 0, 0, 0)),
                  pl.BlockSpec((1, g, HEAD_DIM, nc), lambda b, i: (b, 0, 0, 0)),
                  pl.BlockSpec((g * rep, 1, nc, TQ), lambda b, i: (0, i, 0, 0)),
                  pl.BlockSpec((nblk, nc), lambda b, i: (0, 0)),
                  pl.BlockSpec((1, g, s, K_COLS), lambda b, i: (b, 0, 0, 0)),
                  pl.BlockSpec((1, g, nt, V_ROWS, TQ), lambda b, i: (b, 0, 0, 0, 0)),
                  pl.BlockSpec((K_COLS - HEAD_DIM, g * width), lambda b, i: (0, 0)),
                  pl.BlockSpec((2, TQ, g * width), lambda b, i: (0, 0, 0)),
                  pl.BlockSpec((1, g, 1, 3 * rep, TQ), lambda b, i: (b, 0, i, 0, 0)),
                  pl.BlockSpec((1, g, TQ, 3 * rep), lambda b, i: (b, 0, i, 0)),
                  pl.BlockSpec((1, TQ, hq), lambda b, i: (b, i, 0))],
        out_specs=pl.BlockSpec((1, TQ, hq), lambda b, i: (b, i, 0)),
        out_shape=jax.ShapeDtypeStruct((bn, s, hq), bf16),
        scratch_shapes=[pltpu.VMEM((K_COLS, g * width), bf16), pltpu.VMEM((g, nblk, TQ), f32),
                        pltpu.VMEM((1, g * width), f32), pltpu.VMEM((V_ROWS, g * width), f32),
                        pltpu.VMEM((2, TQ, g * width), f32), pltpu.VMEM((2, 1, g * width), f32),
                        pltpu.VMEM((2, 1, g * width), f32)],
        compiler_params=_cparams(("parallel", "arbitrary")),
        name="nsa_mix",
    )(qt, kc, jnp.swapaxes(vc, 2, 3), bias_c, jnp.asarray(overlap.T, f32), ks, vs,
      *_flash_operands(near, far), gates_t, gates, o_win)


RADIX_BITS_PER_CHECK = 8


def _dsa_kernel(qt_ref, k_ref, vt_ref, qit_ref, ki_ref, wit_ref, qx_ref, delta_ref, tri_ref, o_ref,
                keys_ref, top_ref, qs_ref, m_ref, acc_ref, s_ref, mt_ref, al_ref, *, topk, chunk):
    i = pl.program_id(1)
    t0 = i * TQ
    width = qt_ref.shape[3]
    nh = width // TQ
    nih = qit_ref.shape[3] // TQ
    krow = lax.broadcasted_iota(i32, (TQ, TQ), 0)
    qcol = lax.broadcasted_iota(i32, (TQ, TQ), 1)
    n_pairs = jnp.maximum(i // 2 + 1, -(-topk // (2 * TQ)))

    qit = qit_ref[0, 0]
    wit = wit_ref[0, 0]

    def score_tile(j):
        kt = ki_ref[0, pl.ds(pl.multiple_of(j * TQ, TQ), TQ), :]
        rel = jnp.maximum(jnp.dot(kt, qit, preferred_element_type=f32), 0.0)
        sc = jnp.zeros((TQ, TQ), f32)
        for h in range(nih):
            sc = sc + wit[h:h + 1, :] * rel[:, h * TQ:(h + 1) * TQ]
        sc = jnp.where(j * TQ + krow <= t0 + qcol, sc, NEG)
        sc = jnp.where(sc == 0.0, 0.0, sc)
        bits = pltpu.bitcast(sc, i32)
        keys_ref[j] = jnp.where(bits < 0, bits ^ 0x7FFFFFFF, bits)
        top_ref[j] = pltpu.bitcast(bits & jnp.int32(-65536), f32).astype(bf16)

    def score_body(jj, carry):
        for u in range(4):
            score_tile(4 * jj + u)
        return carry

    lax.fori_loop(0, (n_pairs + 1) // 2, score_body, 0)

    def count(pred_fn):
        def body(jj, cnt):
            return (cnt + jnp.where(pred_fn(keys_ref[2 * jj]), 1.0, 0.0)
                    + jnp.where(pred_fn(keys_ref[2 * jj + 1]), 1.0, 0.0))
        cnt = lax.fori_loop(0, n_pairs, body, jnp.zeros((TQ, TQ), f32))
        return jnp.sum(cnt, axis=0, keepdims=True)

    def bits_body(state):
        t, prefix, n_ge = state
        for u in range(RADIX_BITS_PER_CHECK):
            cand = prefix ^ jnp.left_shift(jnp.int32(1), 31 - (t + u))
            cnt = count(lambda kj: kj >= cand)
            prefix = jnp.where(cnt >= topk, cand, prefix)
            n_ge = jnp.where(cnt >= topk, cnt, n_ge)
        return t + RADIX_BITS_PER_CHECK, prefix, n_ge

    def count_top(cand):
        one, zero = jnp.ones((), bf16), jnp.zeros((), bf16)

        def body(jj, cnt):
            return (cnt + jnp.where(top_ref[2 * jj] >= cand, one, zero)
                    + jnp.where(top_ref[2 * jj + 1] >= cand, one, zero))
        cnt = lax.fori_loop(0, n_pairs, body, jnp.zeros((TQ, TQ), bf16))
        return jnp.sum(cnt.astype(f32), axis=0, keepdims=True)

    def top_body(state):
        t, hi, n_ge = state
        for u in range(RADIX_BITS_PER_CHECK):
            cand = hi + jnp.left_shift(jnp.int32(1), 15 - (t + u))
            pattern = jnp.where(cand < 0, cand ^ 0x7FFF, jnp.where(cand > 0, jnp.maximum(cand, 128), cand))
            cand_f = pltpu.bitcast(jnp.left_shift(pattern, 16), f32).astype(bf16)
            cnt = count_top(cand_f)
            hi = jnp.where(cnt >= topk, cand, hi)
            n_ge = jnp.where(cnt >= topk, cnt, n_ge)
        return t + RADIX_BITS_PER_CHECK, hi, n_ge

    n_all = (2 * n_pairs * TQ).astype(f32)
    _, hi, n_ge = lax.while_loop(lambda st: (st[0] < 16) & (jnp.max(st[2]) > topk), top_body,
                                 (jnp.int32(0), jnp.full((1, TQ), -32768, i32), jnp.full((1, TQ), n_all, f32)))
    _, thr, n_ge = lax.while_loop(lambda st: (st[0] < 32) & (jnp.max(st[2]) > topk), bits_body,
                                  (jnp.int32(16), jnp.left_shift(hi, 16), n_ge))

    @pl.when(jnp.max(n_ge) > topk)
    def _():
        need = topk - count(lambda kj: kj > thr)

        def fix_body(j, running):
            kj = keys_ref[j]
            tie = jnp.where(kj == thr, 1.0, 0.0)
            before = jnp.dot(tri_ref[...], tie.astype(bf16), preferred_element_type=f32) + running
            keys_ref[j] = jnp.where(before >= need, jnp.where(kj == thr, kj - 1, kj), kj)
            return running + jnp.sum(tie, axis=0, keepdims=True)

        lax.fori_loop(0, 2 * n_pairs, fix_body, jnp.zeros((1, TQ), f32))

    qs_ref[0:HEAD_DIM, :] = (qt_ref[0, 0].astype(f32) * (HEAD_DIM ** -0.5 * LOG2E)).astype(bf16)
    qs_ref[HEAD_DIM:K_COLS, :] = qx_ref[...]
    _masked_flash(i, qs_ref,
                  lambda j, c: k_ref[0, pl.ds(pl.multiple_of(j * TQ, TQ), TQ), :],
                  lambda j, c: vt_ref[0, j],
                  lambda j, c: jnp.where(keys_ref[j] >= thr, 0.0, NEG),
                  delta_ref, m_ref, acc_ref, s_ref, mt_ref, al_ref, chunk, True)
    acc = acc_ref[...]
    ot = acc[0:HEAD_DIM] / acc[HEAD_DIM:HEAD_DIM + 1]
    o_ref[0] = jnp.concatenate([ot[:, h * TQ:(h + 1) * TQ].T for h in range(nh)], axis=1).astype(o_ref.dtype)


def _dsa_attention(qt, k, v, qit, ki, wi, near, far):
    bn, nt, _, width = qt.shape
    s = nt * TQ
    nh = width // TQ
    hq = nh * HEAD_DIM
    nih = qit.shape[3] // TQ
    topk = min(C_TOPK_MAX, s // 4)
    wit = wi.reshape(bn, nt, TQ, nih).transpose(0, 1, 3, 2)
    tri = np.tril(np.ones((TQ, TQ), np.float32), -1)
    kern = functools.partial(_dsa_kernel, topk=topk, chunk=512)
    nt_scr = max(nt, -(-topk // TQ))
    return pl.pallas_call(
        kern,
        grid=(bn, nt),
        in_specs=[pl.BlockSpec((1, 1, HEAD_DIM, width), lambda b, i: (b, i, 0, 0)),
                  pl.BlockSpec((1, s, K_COLS), lambda b, i: (b, 0, 0)),
                  pl.BlockSpec((1, nt, V_ROWS, TQ), lambda b, i: (b, 0, 0, 0)),
                  pl.BlockSpec((1, 1, IDX_DIM, nih * TQ), lambda b, i: (b, i, 0, 0)),
                  pl.BlockSpec((1, s, IDX_DIM), lambda b, i: (b, 0, 0)),
                  pl.BlockSpec((1, 1, nih, TQ), lambda b, i: (b, i, 0, 0)),
                  pl.BlockSpec((K_COLS - HEAD_DIM, width), lambda b, i: (0, 0)),
                  pl.BlockSpec((2, TQ, width), lambda b, i: (0, 0, 0)),
                  pl.BlockSpec((TQ, TQ), lambda b, i: (0, 0))],
        out_specs=pl.BlockSpec((1, TQ, hq), lambda b, i: (b, i, 0)),
        out_shape=jax.ShapeDtypeStruct((bn, s, hq), bf16),
        scratch_shapes=[pltpu.VMEM((nt_scr, TQ, TQ), i32), pltpu.VMEM((nt_scr, TQ, TQ), bf16),
                        pltpu.VMEM((K_COLS, width), bf16),
                        pltpu.VMEM((1, width), f32), pltpu.VMEM((V_ROWS, width), f32),
                        pltpu.VMEM((2, TQ, width), f32), pltpu.VMEM((2, 1, width), f32),
                        pltpu.VMEM((2, 1, width), f32)],
        compiler_params=_cparams(("parallel", "arbitrary")),
        name="dsa_attention",
    )(qt, k, v, qit, ki, wit, *_flash_operands(near, far), jnp.asarray(tri, bf16))


def _heads_first(a, g):
    bn, s, _ = a.shape
    return a.reshape(bn, s, g, HEAD_DIM).transpose(0, 2, 1, 3)


def _even_layer(x, table_t, w_in, sinks, cmpk_pos, cmpk_w1, cmpk_w2, cmpv_pos, cmpv_w1, cmpv_w2,
                w_out, ln1_g, ln1_b, ffn_gate, ffn_up, ffn_down, ln2_g, ln2_b):
    bn, s, d = x.shape
    xf = x.reshape(bn * s, d)
    a_q, a_kv, b_q, b_kv = A_HEADS * HEAD_DIM, A_KV * HEAD_DIM, B_HEADS * HEAD_DIM, B_KV * HEAD_DIM
    n_main = a_q + 2 * a_kv + b_q + 6 * b_kv
    rb = B_HEADS // B_KV
    w_tail = jnp.pad(w_in[:, n_main:], ((0, 0), (0, 128 - 3 * B_HEADS))).astype(bf16)
    cols = np.cumsum([0, a_q, a_kv, a_kv, b_q] + [b_kv] * 6)
    w_kv = jnp.concatenate([w_in[:, cols[1]:cols[3]], w_in[:, cols[4]:cols[6]], w_in[:, cols[8]:cols[10]]],
                           axis=1).astype(bf16)
    main, tail, qt_a, qt_b, ks1, vst = _proj_in(
        x, [w_kv, w_tail, w_in[:, cols[0]:cols[1]].T.astype(bf16), w_in[:, cols[3]:cols[4]].T.astype(bf16),
            w_in[:, cols[6]:cols[7]].astype(bf16), w_in[:, cols[7]:cols[8]].T.astype(bf16)],
        [("plain", w_kv.shape[1], bf16, 0), ("plain", 128, f32, 0),
         ("queries", A_KV, A_HEADS // A_KV, HEAD_DIM), ("queries", B_KV, rb, HEAD_DIM),
         ("keys", B_KV, 1, HEAD_DIM), ("values", B_KV, 1, HEAD_DIM)])
    main = main.reshape(bn, s, w_kv.shape[1])
    gates = tail[:, :3 * B_HEADS]
    gates = gates.reshape(bn, s, 3, B_KV, rb).transpose(0, 3, 1, 2, 4).reshape(bn, B_KV, s, 3 * rb)
    kcols = np.cumsum([0, a_kv, a_kv] + [b_kv] * 4)
    ka, va, kc_in, vc_in, kwn, vwn = [main[:, :, kcols[t]:kcols[t + 1]] for t in range(6)]

    near_idx = _bucket_of(np.arange(TQ)[:, None] + TQ - np.arange(2 * TQ)[None, :])
    near = _bias_expand(table_t, near_idx, TQ)
    pad_b = -(-(B_WINDOW - 1) // TQ) * TQ
    win_idx = _bucket_of(np.arange(TQ)[:, None] + pad_b - np.arange(pad_b + TQ)[None, :])
    bias_win = _bias_expand(table_t[A_HEADS:A_HEADS + B_HEADS], win_idx, TQ)
    nc = s // CMP_STRIDE
    nt = s // TQ
    cmp_idx = _bucket_of((np.arange(nt)[:, None, None] * TQ + np.arange(TQ)[None, None, :])
                         - (np.arange(nc)[None, :, None] * CMP_STRIDE + CMP_BLOCK - 1)).reshape(nt * nc, TQ)
    bias_c = _bias_expand(table_t[A_HEADS:A_HEADS + B_HEADS], cmp_idx, min(nt * nc, 512))
    bias_c = bias_c.reshape(B_HEADS, nt, nc, TQ)
    far = table_t[:, NUM_BUCKETS - 1]

    out_a =_banded_gqa(qt_a, _heads_first(ka, A_KV), _heads_first(va, A_KV), near[:A_HEADS], sinks, A_WINDOW, bf16)
    o_win = _banded_gqa(qt_b, _heads_first(kwn, B_KV), _heads_first(vwn, B_KV), bias_win, None, B_WINDOW, f32)
    kc = _nsa_compress(_heads_first(kc_in, B_KV), cmpk_pos, cmpk_w1, cmpk_w2)
    vc = _nsa_compress(_heads_first(vc_in, B_KV), cmpv_pos, cmpv_w1, cmpv_w2)
    out_b = _nsa_mix(qt_b, kc, vc, bias_c, ks1, vst,
                     near[A_HEADS:A_HEADS + B_HEADS], far[A_HEADS:A_HEADS + B_HEADS], gates, o_win)
    mix = jnp.concatenate([out_a, out_b], axis=-1).reshape(bn * s, a_q + b_q)
    x1 = _proj_res_ln(mix, w_out.astype(bf16), xf, ln1_g, ln1_b)
    x2 = _ffn_res_ln(x1, ffn_gate.astype(bf16), ffn_up.astype(bf16), ffn_down.astype(bf16), ln2_g, ln2_b)
    return x2.reshape(bn, s, d), near, far


def _odd_layer(x, near, far, w_in, w_out, ln1_g, ln1_b, router, exp_gate, exp_up, exp_down, ln2_g, ln2_b):
    bn, s, d = x.shape
    xf = x.reshape(bn * s, d)
    c_q = C_HEADS * HEAD_DIM
    n_main = c_q + 2 * HEAD_DIM + IDX_HEADS * IDX_DIM
    w_tail = jnp.pad(w_in[:, n_main:], ((0, 0), (0, 128 - IDX_DIM - IDX_HEADS))).astype(bf16)
    tail, qt, qit, k1, vt = _proj_in(
        x, [w_tail, w_in[:, :c_q].T.astype(bf16), w_in[:, c_q + 2 * HEAD_DIM:n_main].T.astype(bf16),
            w_in[:, c_q:c_q + HEAD_DIM].astype(bf16), w_in[:, c_q + HEAD_DIM:c_q + 2 * HEAD_DIM].T.astype(bf16)],
        [("plain", 128, f32, 0), ("queries", 1, C_HEADS, HEAD_DIM), ("queries", 1, IDX_HEADS, IDX_DIM),
         ("keys", 1, 1, HEAD_DIM), ("values", 1, 1, HEAD_DIM)])
    tail = tail.reshape(bn, s, 128)
    ki = tail[:, :, :IDX_DIM].astype(bf16)
    wi = tail[:, :, IDX_DIM:IDX_DIM + IDX_HEADS]
    mix = _dsa_attention(qt[:, 0], k1[:, 0], vt[:, 0], qit[:, 0], ki, wi, near[:C_HEADS], far[:C_HEADS])
    mix = mix.reshape(bn * s, c_q)
    x1 = _proj_res_ln(mix, w_out.astype(bf16), xf, ln1_g, ln1_b)
    x2 = _moe_res_ln(x1, router, exp_gate.astype(bf16), exp_up.astype(bf16), exp_down.astype(bf16), ln2_g, ln2_b)
    return x2.reshape(bn, s, d)


def kernel(x, rel_bias, l0_w_in, l0_sinks, l0_cmpk_pos, l0_cmpk_w1, l0_cmpk_w2, l0_cmpv_pos, l0_cmpv_w1,
           l0_cmpv_w2, l0_w_out, l0_ln1_g, l0_ln1_b, l0_ffn_gate, l0_ffn_up, l0_ffn_down, l0_ln2_g, l0_ln2_b,
           l1_w_in, l1_w_out, l1_ln1_g, l1_ln1_b, l1_router, l1_exp_gate, l1_exp_up, l1_exp_down, l1_ln2_g,
           l1_ln2_b):
    table_t = rel_bias.T.astype(f32)
    x, near, far = _even_layer(x, table_t, l0_w_in, l0_sinks, l0_cmpk_pos, l0_cmpk_w1, l0_cmpk_w2, l0_cmpv_pos,
                               l0_cmpv_w1, l0_cmpv_w2, l0_w_out, l0_ln1_g, l0_ln1_b, l0_ffn_gate, l0_ffn_up,
                               l0_ffn_down, l0_ln2_g, l0_ln2_b)
    return _odd_layer(x, near, far, l1_w_in, l1_w_out, l1_ln1_g, l1_ln1_b, l1_router, l1_exp_gate, l1_exp_up,
                      l1_exp_down, l1_ln2_g, l1_ln2_b)
```

```python
import functools
import math

import numpy as np
import jax
import jax.numpy as jnp
from jax import lax
from jax.experimental import pallas as pl
from jax.experimental.pallas import tpu as pltpu

f32 = jnp.float32
bf16 = jnp.bfloat16
i32 = jnp.int32

D_MODEL = 1024
HEAD_DIM = 64
NUM_BUCKETS = 32
MAX_DISTANCE = 128
A_HEADS, A_KV, A_WINDOW = 8, 2, 128
B_HEADS, B_KV, B_WINDOW = 8, 2, 512
CMP_BLOCK, CMP_STRIDE = 32, 16
SLC_BLOCK, SLC_TOPN = 64, 16
C_HEADS, IDX_HEADS, IDX_DIM, C_TOPK_MAX = 16, 8, 32, 256
D_FF, N_EXPERTS = 2816, 8
DEPTH = 2
ALPHA = (2.0 * DEPTH) ** 0.25
LN_EPS = 1e-5
NEG = -1e30
POS_BIG = 1e30
INT_MIN = -(2 ** 31)

TQ = 128
FF_CHUNK = 256
VMEM_LIMIT = 56 * 1024 * 1024


def _cparams(sem):
    return pltpu.CompilerParams(dimension_semantics=sem, vmem_limit_bytes=VMEM_LIMIT)


def _bucket_table(max_dist):
    n = np.arange(max_dist + 1)
    max_exact = NUM_BUCKETS // 2
    nf = np.maximum(n, 1).astype(np.float64)
    large = max_exact + (np.log(nf / max_exact) / math.log(MAX_DISTANCE / max_exact)
                         * (NUM_BUCKETS - max_exact)).astype(np.int64)
    large = np.minimum(large, NUM_BUCKETS - 1)
    return np.where(n < max_exact, n, large).astype(np.int32)


def _bucket_of(dist):
    dist = np.maximum(dist, 0)
    return _bucket_table(int(dist.max()))[dist]


def _proj_in_kernel(*refs, groups):
    ng = len(groups)
    x_ref = refs[0]
    wq_refs = refs[1:1 + ng]
    oq_refs = refs[1 + ng:]
    xb = x_ref[...].astype(bf16)
    tm = xb.shape[0]
    for wq_ref, oq_ref, (kind, gb, hb, dh) in zip(wq_refs, oq_refs, groups):
        if kind == "plain":
            oq_ref[...] = jnp.dot(xb, wq_ref[...], preferred_element_type=f32).astype(oq_ref.dtype)
            continue
        if kind == "keys":
            kk = jnp.dot(xb, wq_ref[...], preferred_element_type=f32).astype(bf16)
            ones = jnp.where(lax.broadcasted_iota(i32, (tm, K_COLS - dh), 1) < 2, 1.0, 0.0).astype(bf16)
            for g in range(gb):
                oq_ref[0, g, :, 0:dh] = kk[:, g * dh:(g + 1) * dh]
                oq_ref[0, g, :, dh:K_COLS] = ones
            continue
        qt = lax.dot_general(wq_ref[...], xb, (((1,), (1,)), ((), ())), preferred_element_type=f32).astype(bf16)
        ones = jnp.where(lax.broadcasted_iota(i32, (V_ROWS - HEAD_DIM, TQ), 0) < 1, 1.0, 0.0).astype(bf16)
        for g in range(gb):
            for h in range(hb):
                rows = slice((g * hb + h) * dh, (g * hb + h + 1) * dh)
                for t in range(tm // TQ):
                    oq_ref[0, g, t, 0:dh, h * TQ:(h + 1) * TQ] = qt[rows, t * TQ:(t + 1) * TQ]
                    if kind == "values":
                        oq_ref[0, g, t, dh:V_ROWS, :] = ones


def _proj_in(x3, w_groups, groups, tm=512):
    bn, s, k = x3.shape
    m = bn * s
    tm = min(tm, s)
    per_b = s // tm
    tiles = tm // TQ
    q_specs, q_shapes = [], []
    for kind, gb, hb, dh in groups:
        if kind == "plain":
            q_specs.append(pl.BlockSpec((tm, gb), lambda i: (i, 0)))
            q_shapes.append(jax.ShapeDtypeStruct((m, gb), hb))
        elif kind == "keys":
            q_specs.append(pl.BlockSpec((1, gb, tm, K_COLS), lambda i: (i // per_b, 0, i % per_b, 0)))
            q_shapes.append(jax.ShapeDtypeStruct((bn, gb, s, K_COLS), bf16))
        else:
            rows = V_ROWS if kind == "values" else dh
            q_specs.append(pl.BlockSpec((1, gb, tiles, rows, hb * TQ), lambda i: (i // per_b, 0, i % per_b, 0, 0)))
            q_shapes.append(jax.ShapeDtypeStruct((bn, gb, s // TQ, rows, hb * TQ), bf16))
    return pl.pallas_call(
        functools.partial(_proj_in_kernel, groups=tuple(groups)),
        grid=(m // tm,),
        in_specs=[pl.BlockSpec((tm, k), lambda i: (i, 0))]
                 + [pl.BlockSpec(w.shape, lambda i: (0, 0)) for w in w_groups],
        out_specs=q_specs,
        out_shape=q_shapes,
        compiler_params=_cparams(("parallel",)),
        name="proj_in",
    )(x3.reshape(m, k), *w_groups)


def _bias_expand_kernel(tab_ref, idx_ref, o_ref):
    h = pl.program_id(0)
    idx = idx_ref[...]
    acc = jnp.zeros(idx.shape, f32)
    for b in range(NUM_BUCKETS):
        acc = jnp.where(idx == b, tab_ref[h, b], acc)
    o_ref[0] = acc


def _bias_expand(table_t, idx, tr):
    nh = table_t.shape[0]
    r, c = idx.shape
    return pl.pallas_call(
        _bias_expand_kernel,
        grid=(nh, r // tr),
        in_specs=[pl.BlockSpec(memory_space=pltpu.SMEM),
                  pl.BlockSpec((tr, c), lambda h, i: (i, 0))],
        out_specs=pl.BlockSpec((1, tr, c), lambda h, i: (h, i, 0)),
        out_shape=jax.ShapeDtypeStruct((nh, r, c), f32),
        compiler_params=_cparams(("parallel", "arbitrary")),
        name="bias_expand",
    )(table_t, jnp.asarray(idx, dtype=i32))


def _layer_norm(z, g, b):
    mu = jnp.mean(z, axis=-1, keepdims=True)
    zc = z - mu
    var = jnp.mean(zc * zc, axis=-1, keepdims=True)
    return zc * lax.rsqrt(var + LN_EPS) * g + b


def _proj_ln_kernel(mix_ref, w_ref, x_ref, g_ref, b_ref, o_ref):
    y = jnp.dot(mix_ref[...], w_ref[...], preferred_element_type=f32)
    o_ref[...] = _layer_norm(ALPHA * x_ref[...] + y, g_ref[...], b_ref[...])


def _proj_res_ln(mix, w, x, g, b, tm=512):
    m, k = mix.shape
    d = w.shape[1]
    return pl.pallas_call(
        _proj_ln_kernel,
        grid=(m // tm,),
        in_specs=[pl.BlockSpec((tm, k), lambda i: (i, 0)),
                  pl.BlockSpec((k, d), lambda i: (0, 0)),
                  pl.BlockSpec((tm, d), lambda i: (i, 0)),
                  pl.BlockSpec((1, d), lambda i: (0, 0)),
                  pl.BlockSpec((1, d), lambda i: (0, 0))],
        out_specs=pl.BlockSpec((tm, d), lambda i: (i, 0)),
        out_shape=jax.ShapeDtypeStruct((m, d), f32),
        compiler_params=_cparams(("parallel",)),
        name="proj_res_ln",
    )(mix, w, x, g.reshape(1, d), b.reshape(1, d))


def _swiglu_accumulate(xb, wg_ref, wu_ref, wd_ref, acc_ref, widx):
    d_ff = wg_ref.shape[-1]
    for c in range(d_ff // FF_CHUNK):
        cols = slice(c * FF_CHUNK, (c + 1) * FF_CHUNK)
        gate = jnp.dot(xb, wg_ref[widx + (slice(None), cols)], preferred_element_type=f32)
        up = jnp.dot(xb, wu_ref[widx + (slice(None), cols)], preferred_element_type=f32)
        h = jax.nn.silu(gate) * up
        acc_ref[...] += jnp.dot(h.astype(bf16), wd_ref[widx + (cols, slice(None))], preferred_element_type=f32)


def _ffn_kernel(x_ref, wg_ref, wu_ref, wd_ref, g_ref, b_ref, o_ref, acc_ref):
    x = x_ref[...]
    acc_ref[...] = jnp.zeros_like(acc_ref)
    _swiglu_accumulate(x.astype(bf16), wg_ref, wu_ref, wd_ref, acc_ref, ())
    o_ref[...] = _layer_norm(ALPHA * x + acc_ref[...], g_ref[...], b_ref[...])


def _ffn_res_ln(x, wg, wu, wd, g, b, tm=512):
    m, d = x.shape
    d_ff = wg.shape[1]
    return pl.pallas_call(
        _ffn_kernel,
        grid=(m // tm,),
        in_specs=[pl.BlockSpec((tm, d), lambda i: (i, 0)),
                  pl.BlockSpec((d, d_ff), lambda i: (0, 0)),
                  pl.BlockSpec((d, d_ff), lambda i: (0, 0)),
                  pl.BlockSpec((d_ff, d), lambda i: (0, 0)),
                  pl.BlockSpec((1, d), lambda i: (0, 0)),
                  pl.BlockSpec((1, d), lambda i: (0, 0))],
        out_specs=pl.BlockSpec((tm, d), lambda i: (i, 0)),
        out_shape=jax.ShapeDtypeStruct((m, d), f32),
        scratch_shapes=[pltpu.VMEM((tm, d), f32)],
        compiler_params=_cparams(("parallel",)),
        name="ffn_res_ln",
    )(x, wg, wu, wd, g.reshape(1, d), b.reshape(1, d))


MOE_CAP = 288


def _router_gates(x, r_ref, lane):
    logits = jnp.dot(x, r_ref[...], preferred_element_type=f32)
    logits = jnp.where(lane < N_EXPERTS, logits, -jnp.inf)
    m1 = jnp.max(logits, axis=-1, keepdims=True)
    i1 = jnp.min(jnp.where(logits == m1, lane, 128), axis=-1, keepdims=True)
    rest = jnp.where(lane == i1, -jnp.inf, logits)
    m2 = jnp.max(rest, axis=-1, keepdims=True)
    i2 = jnp.min(jnp.where(rest == m2, lane, 128), axis=-1, keepdims=True)
    e2 = jnp.exp(m2 - m1)
    g1 = 1.0 / (1.0 + e2)
    return jnp.where(lane == i1, g1, 0.0) + jnp.where(lane == i2, e2 * g1, 0.0)


def _moe_kernel(x_ref, r_ref, tri_ref, wg_ref, wu_ref, wd_ref, g_ref, b_ref, o_ref,
                split_ref, post_ref, yc_ref):
    e = pl.program_id(1)
    tm = x_ref.shape[0]

    @pl.when(e == 0)
    def _():
        comb = _router_gates(x_ref[...], r_ref, lax.broadcasted_iota(i32, (tm, 128), 1))
        hi = comb.astype(bf16)
        rest = comb - hi.astype(f32)
        mid = rest.astype(bf16)
        split_ref[0] = hi
        split_ref[1] = mid
        split_ref[2] = (rest - mid.astype(f32)).astype(bf16)
        chosen = jnp.where(comb > 0.0, 1.0, 0.0)
        pos = jnp.dot(tri_ref[...], chosen.astype(bf16), preferred_element_type=f32)
        pos = jnp.where(comb > 0.0, pos, -1.0)
        for t in range(tm // 128):
            rows = slice(t * 128, (t + 1) * 128)
            post_ref[:, rows] = pos[rows].T[0:N_EXPERTS]
        hi32 = jnp.floor((pos + 1.0) * (1.0 / 32.0))
        split_ref[3] = hi32.astype(bf16)
        split_ref[4] = (pos + 1.0 - 32.0 * hi32).astype(bf16)
        o_ref[...] = jnp.zeros_like(o_ref)

    pos_row = post_ref[pl.ds(e, 1), :].astype(i32)
    n_tok = jnp.sum(jnp.where(pos_row >= 0, 1.0, 0.0)).astype(i32)
    pick = jnp.where(lax.broadcasted_iota(i32, (128, 128), 0) == e, 1.0, 0.0).astype(bf16)

    def column(t):
        return jnp.dot(split_ref[t], pick, preferred_element_type=f32)

    gate_b = column(0) + column(1) + column(2)
    pos_b = (32.0 * column(3) + column(4) - 1.0).astype(i32)
    lane = lax.broadcasted_iota(i32, (tm, 128), 1)

    def pass_body(p, carry):
        base = p * MOE_CAP
        gather = jnp.where(pos_row == base + lax.broadcasted_iota(i32, (MOE_CAP, tm), 0), 1.0, 0.0).astype(bf16)
        xc = jnp.dot(gather, x_ref[...].astype(bf16), preferred_element_type=f32).astype(bf16)
        yc_ref[...] = jnp.zeros_like(yc_ref)
        _swiglu_accumulate(xc, wg_ref, wu_ref, wd_ref, yc_ref, (0,))
        scatter = jnp.concatenate([jnp.where(pos_b == base + t * 128 + lane, 1.0, 0.0).astype(bf16)
                                   for t in range(-(-MOE_CAP // 128))], axis=1)[:, :MOE_CAP]
        y = jnp.dot(scatter, yc_ref[...].astype(bf16), preferred_element_type=f32)
        o_ref[...] += jnp.concatenate([gate_b] * (y.shape[1] // 128), axis=1) * y
        return carry

    lax.fori_loop(0, (n_tok + MOE_CAP - 1) // MOE_CAP, pass_body, 0)

    @pl.when(e == N_EXPERTS - 1)
    def _():
        o_ref[...] = _layer_norm(ALPHA * x_ref[...] + o_ref[...], g_ref[...], b_ref[...])


def _moe_res_ln(x, router, wg, wu, wd, g, b, tm=1024):
    m, d = x.shape
    ne, _, d_ff = wg.shape
    tm = min(tm, m)
    router_p = jnp.pad(router, ((0, 0), (0, 128 - ne)))
    tri = jnp.asarray(np.tril(np.ones((tm, tm), np.float32), -1), bf16)
    single = pl.Buffered(1)
    return pl.pallas_call(
        _moe_kernel,
        grid=(m // tm, ne),
        in_specs=[pl.BlockSpec((tm, d), lambda i, e: (i, 0), pipeline_mode=single),
                  pl.BlockSpec((d, 128), lambda i, e: (0, 0), pipeline_mode=single),
                  pl.BlockSpec((tm, tm), lambda i, e: (0, 0), pipeline_mode=single),
                  pl.BlockSpec((1, d, d_ff), lambda i, e: (e, 0, 0)),
                  pl.BlockSpec((1, d, d_ff), lambda i, e: (e, 0, 0)),
                  pl.BlockSpec((1, d_ff, d), lambda i, e: (e, 0, 0)),
                  pl.BlockSpec((1, d), lambda i, e: (0, 0)),
                  pl.BlockSpec((1, d), lambda i, e: (0, 0))],
        out_specs=pl.BlockSpec((tm, d), lambda i, e: (i, 0)),
        out_shape=jax.ShapeDtypeStruct((m, d), f32),
        scratch_shapes=[pltpu.VMEM((5, tm, 128), bf16), pltpu.VMEM((N_EXPERTS, tm), f32),
                        pltpu.VMEM((MOE_CAP, d), f32)],
        compiler_params=_cparams(("parallel", "arbitrary")),
        name="moe_res_ln",
    )(x, router_p, tri, wg, wu, wd, g.reshape(1, d), b.reshape(1, d))


def _compress_kernel(blk_ref, pos_ref, w1_ref, w2_ref, o_ref):
    xb = (blk_ref[0].astype(f32) + pos_ref[...]).astype(bf16)
    h = jax.nn.gelu(jnp.dot(xb, w1_ref[...], preferred_element_type=f32))
    o_ref[0] = jnp.dot(h.astype(bf16), w2_ref[...], preferred_element_type=f32).astype(o_ref.dtype)


def _nsa_compress(kv, pos, w1, w2):
    bn, g, s, dh = kv.shape
    nc = s // CMP_STRIDE
    chunks = kv.reshape(bn, g, nc, CMP_STRIDE * dh)
    nxt = jnp.pad(chunks[:, :, 1:], ((0, 0), (0, 0), (0, 1), (0, 0)))
    blocks = jnp.concatenate([chunks, nxt], axis=-1).reshape(bn * g, nc, CMP_BLOCK * dh)
    cin = CMP_BLOCK * dh
    hid = w1.shape[1]
    out = pl.pallas_call(
        _compress_kernel,
        grid=(bn * g,),
        in_specs=[pl.BlockSpec((1, nc, cin), lambda i: (i, 0, 0)),
                  pl.BlockSpec((1, cin), lambda i: (0, 0)),
                  pl.BlockSpec((cin, hid), lambda i: (0, 0)),
                  pl.BlockSpec((hid, dh), lambda i: (0, 0))],
        out_specs=pl.BlockSpec((1, nc, dh), lambda i: (i, 0, 0)),
        out_shape=jax.ShapeDtypeStruct((bn * g, nc, dh), bf16),
        compiler_params=_cparams(("parallel",)),
        name="nsa_compress",
    )(blocks, pos.reshape(1, cin), w1.astype(bf16), w2.astype(bf16))
    return out.reshape(bn, g, nc, dh)


V_ROWS = HEAD_DIM + 16


LOG2E = 1.4426950408889634
K_COLS = 2 * HEAD_DIM


def _masked_flash(i, qs_ref, key_tile, value_tile, mask_bias, delta_ref, m_ref, acc_ref, s_ref, mt_ref, al_ref, chunk,
                  shared_keys):
    width = qs_ref.shape[1]
    n_chunks = width // chunk
    m_ref[...] = jnp.full(m_ref.shape, NEG, f32)
    acc_ref[...] = jnp.zeros(acc_ref.shape, f32)

    def scores_chunk(j, delta, c, shared):
        slot = j % 2
        if c == 0 or not shared_keys:
            shared["kt"] = key_tile(j, c)
            shared["maskb"] = jnp.concatenate([mask_bias(j, c)] * (chunk // TQ), axis=1)
        cols = slice(c * chunk, (c + 1) * chunk)
        s = jnp.dot(shared["kt"], qs_ref[:, cols], preferred_element_type=f32) + shared["maskb"]
        if delta is not None:
            s = s + delta_ref[delta, :, cols]
        s_ref[slot, :, cols] = s
        m_old = m_ref[:, cols]
        m_new = jnp.maximum(m_old, jnp.max(s, axis=0, keepdims=True))
        m_ref[:, cols] = m_new
        mt_ref[slot, :, cols] = m_new
        al_ref[slot, :, cols] = jnp.exp2(m_old - m_new)

    def values_chunk(j, c, shared):
        slot = j % 2
        if c == 0 or not shared_keys:
            shared["vt"] = value_tile(j, c)
        cols = slice(c * chunk, (c + 1) * chunk)
        p = jnp.exp2(s_ref[slot, :, cols] - mt_ref[slot, :, cols])
        acc_ref[:, cols] = (al_ref[slot, :, cols] * acc_ref[:, cols]
                            + jnp.dot(shared["vt"], p.astype(bf16), preferred_element_type=f32))

    def scores(j, delta):
        shared = {}
        for c in range(n_chunks):
            scores_chunk(j, delta, c, shared)

    def values(j):
        shared = {}
        for c in range(n_chunks):
            values_chunk(j, c, shared)

    def values_then_scores(jv, js, delta):
        sv, ss = {}, {}
        for c in range(n_chunks):
            values_chunk(jv, c, sv)
            scores_chunk(js, delta, c, ss)

    @pl.when(i >= 2)
    def _():
        scores(0, None)

        def pipe_body(j, carry):
            values_then_scores(j - 1, j, None)
            return carry

        lax.fori_loop(1, i - 1, pipe_body, 0)
        values_then_scores(i - 2, i - 1, 0)
        values_then_scores(i - 1, i, 1)

    @pl.when(i == 1)
    def _():
        scores(0, 0)
        values_then_scores(0, 1, 1)

    @pl.when(i == 0)
    def _():
        scores(0, 1)

    values(i)


def _flash_operands(near, far):
    nh = near.shape[0]
    width = nh * TQ
    far2 = jnp.repeat(far * LOG2E, TQ).reshape(1, width)
    hi = far2.astype(bf16)
    lo = (far2 - hi.astype(f32)).astype(bf16)
    extra = jnp.concatenate([hi, lo, jnp.zeros((K_COLS - HEAD_DIM - 2, width), bf16)], axis=0)
    neart = near.reshape(nh, TQ, 2, TQ).transpose(2, 3, 0, 1).reshape(2, TQ, width) * LOG2E
    neart = neart - (hi.astype(f32) + lo.astype(f32))
    causal = np.where(np.arange(TQ)[:, None] <= np.arange(TQ)[None, :], 0.0, NEG).astype(np.float32)
    return extra, jnp.stack([neart[0], neart[1] + jnp.asarray(np.tile(causal, (1, nh)))], axis=0)


def _value_tiles(v):
    lead = v.shape[:-2]
    nt = v.shape[-2] // TQ
    vt = jnp.swapaxes(v.reshape(lead + (nt, TQ, HEAD_DIM)), -1, -2)
    return jnp.concatenate([vt, jnp.ones(lead + (nt, 1, TQ), vt.dtype),
                            jnp.zeros(lead + (nt, V_ROWS - HEAD_DIM - 1, TQ), vt.dtype)], axis=-2)


BANDED_TILES = 2


def _banded_kernel(qt_ref, k_ref, vt_ref, bias_ref, neg_ref, sink_ref, o_ref, *, rep, kt_tiles, use_sinks):
    kw = kt_tiles * TQ
    for u in range(qt_ref.shape[2]):
        i = pl.program_id(1) * qt_ref.shape[2] + u
        outs = []
        for g in range(qt_ref.shape[1]):
            qs = jnp.concatenate([(qt_ref[0, g, u].astype(f32) * (HEAD_DIM ** -0.5)).astype(bf16), neg_ref[...]],
                                 axis=0)
            k = k_ref[0, g, pl.ds(pl.multiple_of(i * TQ, TQ), kw), :]
            s = jnp.dot(k, qs, preferred_element_type=f32) + bias_ref[g]
            m = jnp.max(s, axis=0, keepdims=True)
            if use_sinks:
                m = jnp.maximum(m, sink_ref[g])
            p = jnp.exp(s - m).astype(bf16)
            vt = jnp.concatenate([vt_ref[0, g, i + t] for t in range(kt_tiles)], axis=1)
            acc = jnp.dot(vt, p, preferred_element_type=f32)
            l = acc[HEAD_DIM:HEAD_DIM + 1]
            if use_sinks:
                l = l + jnp.exp(sink_ref[g] - m)
            ot = acc[0:HEAD_DIM] / l
            outs += [ot[:, r * TQ:(r + 1) * TQ].T for r in range(rep)]
        o_ref[0, u * TQ:(u + 1) * TQ, :] = jnp.concatenate(outs, axis=1).astype(o_ref.dtype)


def _banded_gqa(qt, k, v, bias, sinks, window, out_dtype):
    bn, g, nt, _, width = qt.shape
    rep = width // TQ
    s = nt * TQ
    pad = -(-(window - 1) // TQ) * TQ
    kw = pad + TQ
    kd = 2 * HEAD_DIM
    flag = jnp.broadcast_to((jnp.arange(pad + s) < pad).astype(k.dtype)[:, None], (bn, g, pad + s, 1))
    kp = jnp.concatenate([jnp.pad(k, ((0, 0), (0, 0), (pad, 0), (0, 0))), flag,
                          jnp.zeros((bn, g, pad + s, kd - HEAD_DIM - 1), k.dtype)], axis=-1)
    vt = _value_tiles(jnp.pad(v, ((0, 0), (0, 0), (pad, 0), (0, 0))))
    neg = np.zeros((kd - HEAD_DIM, width), np.float32)
    neg[0] = NEG
    dist = np.arange(TQ)[:, None] + pad - np.arange(kw)[None, :]
    band = jnp.asarray((dist >= 0) & (dist < window))
    bias_t = jnp.where(band, bias, NEG).reshape(g, rep, TQ, kw).transpose(0, 3, 1, 2).reshape(g, kw, width)
    use_sinks = sinks is not None
    sink_rows = (jnp.repeat(sinks.astype(f32), TQ) if use_sinks else jnp.zeros((g * width,), f32)).reshape(g, 1, width)
    kern = functools.partial(_banded_kernel, rep=rep, kt_tiles=kw // TQ, use_sinks=use_sinks)
    rw = rep * HEAD_DIM
    return pl.pallas_call(
        kern,
        grid=(bn, nt // BANDED_TILES),
        in_specs=[pl.BlockSpec((1, g, BANDED_TILES, HEAD_DIM, width), lambda b, i: (b, 0, i, 0, 0)),
                  pl.BlockSpec((1, g, pad + s, kd), lambda b, i: (b, 0, 0, 0)),
                  pl.BlockSpec((1, g, (pad + s) // TQ, V_ROWS, TQ), lambda b, i: (b, 0, 0, 0, 0)),
                  pl.BlockSpec((g, kw, width), lambda b, i: (0, 0, 0)),
                  pl.BlockSpec((kd - HEAD_DIM, width), lambda b, i: (0, 0)),
                  pl.BlockSpec((g, 1, width), lambda b, i: (0, 0, 0))],
        out_specs=pl.BlockSpec((1, BANDED_TILES * TQ, g * rw), lambda b, i: (b, i, 0)),
        out_shape=jax.ShapeDtypeStruct((bn, s, g * rw), out_dtype),
        compiler_params=_cparams(("parallel", "arbitrary")),
        name="banded_gqa_w%d" % window,
    )(qt, kp, vt, bias_t, jnp.asarray(neg, bf16), sink_rows)


def _nsa_kernel(qt_ref, kc_ref, vct_ref, biasc_ref, ovt_ref, ks_ref, vst_ref, qx_ref, delta_ref, gt_ref, gn_ref,
                owin_ref, o_ref, qs_ref, selb_ref, m_ref, acc_ref, s_ref, mt_ref, al_ref, *, rep, n_sel):
    i = pl.program_id(1)
    t0 = i * TQ
    groups = qt_ref.shape[1]
    nc = kc_ref.shape[2]
    nblk = ovt_ref.shape[0]
    width = rep * TQ
    cmp_end = lax.broadcasted_iota(i32, (nc, TQ), 0) * CMP_STRIDE + (CMP_BLOCK - 1)
    valid_c = t0 + lax.broadcasted_iota(i32, (nc, TQ), 1) >= cmp_end
    blk = lax.broadcasted_iota(i32, (nblk, TQ), 0)
    tq = t0 + lax.broadcasted_iota(i32, (nblk, TQ), 1)
    cur = tq // SLC_BLOCK
    forced = (blk == 0) | (blk == cur) | (blk == cur - 1)
    admitted = blk * SLC_BLOCK <= tq

    o_cmps = []
    for g in range(groups):
        qf = qt_ref[0, g, 0].astype(f32)
        qs = (qf * (HEAD_DIM ** -0.5)).astype(bf16)
        qs_ref[0:HEAD_DIM, g * width:(g + 1) * width] = (qf * (HEAD_DIM ** -0.5 * LOG2E)).astype(bf16)

        s_c = jnp.dot(kc_ref[0, g], qs, preferred_element_type=f32)
        p_cols = []
        p_sum = jnp.zeros((nc, TQ), f32)
        for r in range(rep):
            sr = jnp.where(valid_c, s_c[:, r * TQ:(r + 1) * TQ] + biasc_ref[g * rep + r, 0], NEG)
            m = jnp.max(sr, axis=0, keepdims=True)
            e = jnp.where(valid_c, jnp.exp(sr - m), 0.0)
            l = jnp.sum(e, axis=0, keepdims=True)
            p = e * (1.0 / jnp.maximum(l, 1e-30))
            p_cols.append(p.astype(bf16))
            p_sum = p_sum + p
        o_cmps.append(jnp.dot(vct_ref[0, g], jnp.concatenate(p_cols, axis=1), preferred_element_type=f32))

        imp = jnp.dot(ovt_ref[...], p_sum, preferred_element_type=f32)
        score = jnp.where(forced, POS_BIG, jnp.where(admitted, imp, NEG))
        rank = jnp.zeros((nblk, TQ), i32)
        for kk in range(nblk):
            ck = score[kk:kk + 1, :]
            rank = rank + jnp.where(ck > score, 1, jnp.where(ck == score, jnp.where(blk > kk, 1, 0), 0))
        selb_ref[g] = jnp.where(rank < n_sel, 0.0, NEG)

    per_tile = TQ // SLC_BLOCK

    def mask_bias(j, c):
        rows = [jnp.broadcast_to(selb_ref[c, pl.ds(per_tile * j + t, 1), :], (SLC_BLOCK, TQ))
                for t in range(per_tile)]
        return jnp.concatenate(rows, axis=0)

    qs_ref[HEAD_DIM:K_COLS, :] = qx_ref[...]
    _masked_flash(i, qs_ref,
                  lambda j, c: ks_ref[0, c, pl.ds(pl.multiple_of(j * TQ, TQ), TQ), :],
                  lambda j, c: vst_ref[0, c, j],
                  mask_bias, delta_ref, m_ref, acc_ref, s_ref, mt_ref, al_ref, width, False)
    acc = acc_ref[...]
    o_slc = acc[0:HEAD_DIM] / acc[HEAD_DIM:HEAD_DIM + 1]

    outs = []
    for g in range(groups):
        sig_t = jax.nn.sigmoid(gt_ref[0, g, 0])
        sig_n = jax.nn.sigmoid(gn_ref[0, g])
        for r in range(rep):
            h = g * rep + r
            mixed = (sig_t[r:r + 1] * o_cmps[g][:, r * TQ:(r + 1) * TQ]
                     + sig_t[rep + r:rep + r + 1] * o_slc[:, h * TQ:(h + 1) * TQ])
            o_win = owin_ref[0, :, h * HEAD_DIM:(h + 1) * HEAD_DIM]
            outs.append(mixed.T + sig_n[:, 2 * rep + r:2 * rep + r + 1] * o_win)
    o_ref[0] = jnp.concatenate(outs, axis=1).astype(o_ref.dtype)


def _nsa_mix(qt, kc, vc, bias_c, ks, vs, near, far, gates, o_win):
    bn, g, nt, _, width = qt.shape
    s = nt * TQ
    nc = kc.shape[2]
    rep = width // TQ
    hq = g * rep * HEAD_DIM
    nblk = s // SLC_BLOCK
    n_sel = min(SLC_TOPN, nblk)
    ci = np.arange(nc)[:, None] * CMP_STRIDE
    sj = np.arange(nblk)[None, :] * SLC_BLOCK
    overlap = np.clip(np.minimum(ci + CMP_BLOCK, sj + SLC_BLOCK) - np.maximum(ci, sj), 0, None) / CMP_BLOCK
    overlap[nc - 1] = 0.0
    gates_t = gates.reshape(bn, g, nt, TQ, 3 * rep).transpose(0, 1, 2, 4, 3)
    kern = functools.partial(_nsa_kernel, rep=rep, n_sel=n_sel)
    return pl.pallas_call(
        kern,
        grid=(bn, nt),
        in_specs=[pl.BlockSpec((1, g, 1, HEAD_DIM, width), lambda b, i: (b, 0, i, 0, 0)),
                  pl.BlockSpec((1, g, nc, HEAD_DIM), lambda b, i: (b, 0, 0, 0)),
                  pl.BlockSpec((1, g, HEAD_DIM, nc), lambda b, i: (b, 0, 0, 0)),
                  pl.BlockSpec((g * rep, 1, nc, TQ), lambda b, i: (0, i, 0, 0)),
                  pl.BlockSpec((nblk, nc), lambda b, i: (0, 0)),
                  pl.BlockSpec((1, g, s, K_COLS), lambda b, i: (b, 0, 0, 0)),
                  pl.BlockSpec((1, g, nt, V_ROWS, TQ), lambda b, i: (b, 0, 0, 0, 0)),
                  pl.BlockSpec((K_COLS - HEAD_DIM, g * width), lambda b, i: (0, 0)),
                  pl.BlockSpec((2, TQ, g * width), lambda b, i: (0, 0, 0)),
                  pl.BlockSpec((1, g, 1, 3 * rep, TQ), lambda b, i: (b, 0, i, 0, 0)),
                  pl.BlockSpec((1, g, TQ, 3 * rep), lambda b, i: (b, 0, i, 0)),
                  pl.BlockSpec((1, TQ, hq), lambda b, i: (b, i, 0))],
        out_specs=pl.BlockSpec((1, TQ, hq), lambda b, i: (b, i, 0)),
        out_shape=jax.ShapeDtypeStruct((bn, s, hq), bf16),
        scratch_shapes=[pltpu.VMEM((K_COLS, g * width), bf16), pltpu.VMEM((g, nblk, TQ), f32),
                        pltpu.VMEM((1, g * width), f32), pltpu.VMEM((V_ROWS, g * width), f32),
                        pltpu.VMEM((2, TQ, g * width), f32), pltpu.VMEM((2, 1, g * width), f32),
                        pltpu.VMEM((2, 1, g * width), f32)],
        compiler_params=_cparams(("parallel", "arbitrary")),
        name="nsa_mix",
    )(qt, kc, jnp.swapaxes(vc, 2, 3), bias_c, jnp.asarray(overlap.T, f32), ks, vs,
      *_flash_operands(near, far), gates_t, gates, o_win)


RADIX_BITS_PER_CHECK = 8


def _dsa_kernel(qt_ref, k_ref, vt_ref, qit_ref, ki_ref, wit_ref, qx_ref, delta_ref, tri_ref, o_ref,
                keys_ref, qs_ref, m_ref, acc_ref, s_ref, mt_ref, al_ref, *, topk, chunk):
    i = pl.program_id(1)
    t0 = i * TQ
    width = qt_ref.shape[3]
    nh = width // TQ
    nih = qit_ref.shape[3] // TQ
    krow = lax.broadcasted_iota(i32, (TQ, TQ), 0)
    qcol = lax.broadcasted_iota(i32, (TQ, TQ), 1)
    n_pairs = jnp.maximum(i // 2 + 1, -(-topk // (2 * TQ)))

    qit = qit_ref[0, 0]
    wit = wit_ref[0, 0]

    def score_tile(j):
        kt = ki_ref[0, pl.ds(pl.multiple_of(j * TQ, TQ), TQ), :]
        rel = jnp.maximum(jnp.dot(kt, qit, preferred_element_type=f32), 0.0)
        sc = jnp.zeros((TQ, TQ), f32)
        for h in range(nih):
            sc = sc + wit[h:h + 1, :] * rel[:, h * TQ:(h + 1) * TQ]
        sc = jnp.where(j * TQ + krow <= t0 + qcol, sc, NEG)
        keys_ref[j] = jnp.where(sc == 0.0, 0.0, sc)

    def score_body(jj, carry):
        for u in range(4):
            score_tile(4 * jj + u)
        return carry

    lax.fori_loop(0, (n_pairs + 1) // 2, score_body, 0)

    def code_to_float(code):
        code = jnp.clip(code, -0x7F800001, 0x7F800000)
        bits = jnp.where(code < 0, code ^ 0x7FFFFFFF, code)
        bits = jnp.where((bits > 0) & (bits < 0x00800000), 0x00800000, bits)
        return pltpu.bitcast(bits, f32)

    def count(pred_fn):
        def body(jj, cnt):
            return (cnt + jnp.where(pred_fn(keys_ref[2 * jj]), 1.0, 0.0)
                    + jnp.where(pred_fn(keys_ref[2 * jj + 1]), 1.0, 0.0))
        cnt = lax.fori_loop(0, n_pairs, body, jnp.zeros((TQ, TQ), f32))
        return jnp.sum(cnt, axis=0, keepdims=True)

    def bits_body(state):
        t, prefix, n_ge = state
        for u in range(RADIX_BITS_PER_CHECK):
            cand = prefix ^ jnp.left_shift(jnp.int32(1), 31 - (t + u))
            cand_f = code_to_float(cand)
            cnt = count(lambda sj: sj >= cand_f)
            prefix = jnp.where(cnt >= topk, cand, prefix)
            n_ge = jnp.where(cnt >= topk, cnt, n_ge)
        return t + RADIX_BITS_PER_CHECK, prefix, n_ge

    n_all = (2 * n_pairs * TQ).astype(f32)
    _, code, n_ge = lax.while_loop(lambda st: (st[0] < 32) & (jnp.max(st[2]) > topk), bits_body,
                                   (jnp.int32(0), jnp.full((1, TQ), INT_MIN, i32), jnp.full((1, TQ), n_all, f32)))
    thr = code_to_float(code)

    @pl.when(jnp.max(n_ge) > topk)
    def _():
        need = topk - count(lambda sj: sj > thr)

        def fix_body(j, running):
            sj = keys_ref[j]
            tie = jnp.where(sj == thr, 1.0, 0.0)
            before = jnp.dot(tri_ref[...], tie.astype(bf16), preferred_element_type=f32) + running
            keys_ref[j] = jnp.where(before >= need, jnp.where(sj == thr, -jnp.inf, sj), sj)
            return running + jnp.sum(tie, axis=0, keepdims=True)

        lax.fori_loop(0, 2 * n_pairs, fix_body, jnp.zeros((1, TQ), f32))

    qs_ref[0:HEAD_DIM, :] = (qt_ref[0, 0].astype(f32) * (HEAD_DIM ** -0.5 * LOG2E)).astype(bf16)
    qs_ref[HEAD_DIM:K_COLS, :] = qx_ref[...]
    _masked_flash(i, qs_ref,
                  lambda j, c: k_ref[0, pl.ds(pl.multiple_of(j * TQ, TQ), TQ), :],
                  lambda j, c: vt_ref[0, j],
                  lambda j, c: jnp.where(keys_ref[j] >= thr, 0.0, NEG),
                  delta_ref, m_ref, acc_ref, s_ref, mt_ref, al_ref, chunk, True)
    acc = acc_ref[...]
    ot = acc[0:HEAD_DIM] / acc[HEAD_DIM:HEAD_DIM + 1]
    o_ref[0] = jnp.concatenate([ot[:, h * TQ:(h + 1) * TQ].T for h in range(nh)], axis=1).astype(o_ref.dtype)


def _dsa_attention(qt, k, v, qit, ki, wi, near, far):
    bn, nt, _, width = qt.shape
    s = nt * TQ
    nh = width // TQ
    hq = nh * HEAD_DIM
    nih = qit.shape[3] // TQ
    topk = min(C_TOPK_MAX, s // 4)
    wit = wi.reshape(bn, nt, TQ, nih).transpose(0, 1, 3, 2)
    tri = np.tril(np.ones((TQ, TQ), np.float32), -1)
    kern = functools.partial(_dsa_kernel, topk=topk, chunk=512)
    nt_scr = max(nt, -(-topk // TQ))
    return pl.pallas_call(
        kern,
        grid=(bn, nt),
        in_specs=[pl.BlockSpec((1, 1, HEAD_DIM, width), lambda b, i: (b, i, 0, 0)),
                  pl.BlockSpec((1, s, K_COLS), lambda b, i: (b, 0, 0)),
                  pl.BlockSpec((1, nt, V_ROWS, TQ), lambda b, i: (b, 0, 0, 0)),
                  pl.BlockSpec((1, 1, IDX_DIM, nih * TQ), lambda b, i: (b, i, 0, 0)),
                  pl.BlockSpec((1, s, IDX_DIM), lambda b, i: (b, 0, 0)),
                  pl.BlockSpec((1, 1, nih, TQ), lambda b, i: (b, i, 0, 0)),
                  pl.BlockSpec((K_COLS - HEAD_DIM, width), lambda b, i: (0, 0)),
                  pl.BlockSpec((2, TQ, width), lambda b, i: (0, 0, 0)),
                  pl.BlockSpec((TQ, TQ), lambda b, i: (0, 0))],
        out_specs=pl.BlockSpec((1, TQ, hq), lambda b, i: (b, i, 0)),
        out_shape=jax.ShapeDtypeStruct((bn, s, hq), bf16),
        scratch_shapes=[pltpu.VMEM((nt_scr, TQ, TQ), f32), pltpu.VMEM((K_COLS, width), bf16),
                        pltpu.VMEM((1, width), f32), pltpu.VMEM((V_ROWS, width), f32),
                        pltpu.VMEM((2, TQ, width), f32), pltpu.VMEM((2, 1, width), f32),
                        pltpu.VMEM((2, 1, width), f32)],
        compiler_params=_cparams(("parallel", "arbitrary")),
        name="dsa_attention",
    )(qt, k, v, qit, ki, wit, *_flash_operands(near, far), jnp.asarray(tri, bf16))


def _heads_first(a, g):
    bn, s, _ = a.shape
    return a.reshape(bn, s, g, HEAD_DIM).transpose(0, 2, 1, 3)


def _even_layer(x, table_t, w_in, sinks, cmpk_pos, cmpk_w1, cmpk_w2, cmpv_pos, cmpv_w1, cmpv_w2,
                w_out, ln1_g, ln1_b, ffn_gate, ffn_up, ffn_down, ln2_g, ln2_b):
    bn, s, d = x.shape
    xf = x.reshape(bn * s, d)
    a_q, a_kv, b_q, b_kv = A_HEADS * HEAD_DIM, A_KV * HEAD_DIM, B_HEADS * HEAD_DIM, B_KV * HEAD_DIM
    n_main = a_q + 2 * a_kv + b_q + 6 * b_kv
    rb = B_HEADS // B_KV
    w_tail = jnp.pad(w_in[:, n_main:], ((0, 0), (0, 128 - 3 * B_HEADS))).astype(bf16)
    cols = np.cumsum([0, a_q, a_kv, a_kv, b_q] + [b_kv] * 6)
    w_kv = jnp.concatenate([w_in[:, cols[1]:cols[3]], w_in[:, cols[4]:cols[6]], w_in[:, cols[8]:cols[10]]],
                           axis=1).astype(bf16)
    main, tail, qt_a, qt_b, ks1, vst = _proj_in(
        x, [w_kv, w_tail, w_in[:, cols[0]:cols[1]].T.astype(bf16), w_in[:, cols[3]:cols[4]].T.astype(bf16),
            w_in[:, cols[6]:cols[7]].astype(bf16), w_in[:, cols[7]:cols[8]].T.astype(bf16)],
        [("plain", w_kv.shape[1], bf16, 0), ("plain", 128, f32, 0),
         ("queries", A_KV, A_HEADS // A_KV, HEAD_DIM), ("queries", B_KV, rb, HEAD_DIM),
         ("keys", B_KV, 1, HEAD_DIM), ("values", B_KV, 1, HEAD_DIM)])
    main = main.reshape(bn, s, w_kv.shape[1])
    gates = tail[:, :3 * B_HEADS]
    gates = gates.reshape(bn, s, 3, B_KV, rb).transpose(0, 3, 1, 2, 4).reshape(bn, B_KV, s, 3 * rb)
    kcols = np.cumsum([0, a_kv, a_kv] + [b_kv] * 4)
    ka, va, kc_in, vc_in, kwn, vwn = [main[:, :, kcols[t]:kcols[t + 1]] for t in range(6)]

    near_idx = _bucket_of(np.arange(TQ)[:, None] + TQ - np.arange(2 * TQ)[None, :])
    near = _bias_expand(table_t, near_idx, TQ)
    pad_b = -(-(B_WINDOW - 1) // TQ) * TQ
    win_idx = _bucket_of(np.arange(TQ)[:, None] + pad_b - np.arange(pad_b + TQ)[None, :])
    bias_win = _bias_expand(table_t[A_HEADS:A_HEADS + B_HEADS], win_idx, TQ)
    nc = s // CMP_STRIDE
    nt = s // TQ
    cmp_idx = _bucket_of((np.arange(nt)[:, None, None] * TQ + np.arange(TQ)[None, None, :])
                         - (np.arange(nc)[None, :, None] * CMP_STRIDE + CMP_BLOCK - 1)).reshape(nt * nc, TQ)
    bias_c = _bias_expand(table_t[A_HEADS:A_HEADS + B_HEADS], cmp_idx, min(nt * nc, 512))
    bias_c = bias_c.reshape(B_HEADS, nt, nc, TQ)
    far = table_t[:, NUM_BUCKETS - 1]

    out_a =_banded_gqa(qt_a, _heads_first(ka, A_KV), _heads_first(va, A_KV), near[:A_HEADS], sinks, A_WINDOW, bf16)
    o_win = _banded_gqa(qt_b, _heads_first(kwn, B_KV), _heads_first(vwn, B_KV), bias_win, None, B_WINDOW, f32)
    kc = _nsa_compress(_heads_first(kc_in, B_KV), cmpk_pos, cmpk_w1, cmpk_w2)
    vc = _nsa_compress(_heads_first(vc_in, B_KV), cmpv_pos, cmpv_w1, cmpv_w2)
    out_b = _nsa_mix(qt_b, kc, vc, bias_c, ks1, vst,
                     near[A_HEADS:A_HEADS + B_HEADS], far[A_HEADS:A_HEADS + B_HEADS], gates, o_win)
    mix = jnp.concatenate([out_a, out_b], axis=-1).reshape(bn * s, a_q + b_q)
    x1 = _proj_res_ln(mix, w_out.astype(bf16), xf, ln1_g, ln1_b)
    x2 = _ffn_res_ln(x1, ffn_gate.astype(bf16), ffn_up.astype(bf16), ffn_down.astype(bf16), ln2_g, ln2_b)
    return x2.reshape(bn, s, d), near, far


def _odd_layer(x, near, far, w_in, w_out, ln1_g, ln1_b, router, exp_gate, exp_up, exp_down, ln2_g, ln2_b):
    bn, s, d = x.shape
    xf = x.reshape(bn * s, d)
    c_q = C_HEADS * HEAD_DIM
    n_main = c_q + 2 * HEAD_DIM + IDX_HEADS * IDX_DIM
    w_tail = jnp.pad(w_in[:, n_main:], ((0, 0), (0, 128 - IDX_DIM - IDX_HEADS))).astype(bf16)
    tail, qt, qit, k1, vt = _proj_in(
        x, [w_tail, w_in[:, :c_q].T.astype(bf16), w_in[:, c_q + 2 * HEAD_DIM:n_main].T.astype(bf16),
            w_in[:, c_q:c_q + HEAD_DIM].astype(bf16), w_in[:, c_q + HEAD_DIM:c_q + 2 * HEAD_DIM].T.astype(bf16)],
        [("plain", 128, f32, 0), ("queries", 1, C_HEADS, HEAD_DIM), ("queries", 1, IDX_HEADS, IDX_DIM),
         ("keys", 1, 1, HEAD_DIM), ("values", 1, 1, HEAD_DIM)])
    tail = tail.reshape(bn, s, 128)
    ki = tail[:, :, :IDX_DIM].astype(bf16)
    wi = tail[:, :, IDX_DIM:IDX_DIM + IDX_HEADS]
    mix = _dsa_attention(qt[:, 0], k1[:, 0], vt[:, 0], qit[:, 0], ki, wi, near[:C_HEADS], far[:C_HEADS])
    mix = mix.reshape(bn * s, c_q)
    x1 = _proj_res_ln(mix, w_out.astype(bf16), xf, ln1_g, ln1_b)
    x2 = _moe_res_ln(x1, router, exp_gate.astype(bf16), exp_up.astype(bf16), exp_down.astype(bf16), ln2_g, ln2_b)
    return x2.reshape(bn, s, d)


def kernel(x, rel_bias, l0_w_in, l0_sinks, l0_cmpk_pos, l0_cmpk_w1, l0_cmpk_w2, l0_cmpv_pos, l0_cmpv_w1,
           l0_cmpv_w2, l0_w_out, l0_ln1_g, l0_ln1_b, l0_ffn_gate, l0_ffn_up, l0_ffn_down, l0_ln2_g, l0_ln2_b,
           l1_w_in, l1_w_out, l1_ln1_g, l1_ln1_b, l1_router, l1_exp_gate, l1_exp_up, l1_exp_down, l1_ln2_g,
           l1_ln2_b):
    table_t = rel_bias.T.astype(f32)
    x, near, far = _even_layer(x, table_t, l0_w_in, l0_sinks, l0_cmpk_pos, l0_cmpk_w1, l0_cmpk_w2, l0_cmpv_pos,
                               l0_cmpv_w1, l0_cmpv_w2, l0_w_out, l0_ln1_g, l0_ln1_b, l0_ffn_gate, l0_ffn_up,
                               l0_ffn_down, l0_ln2_g, l0_ln2_b)
    return _odd_layer(x, near, far, l1_w_in, l1_w_out, l1_ln1_g, l1_ln1_b, l1_router, l1_exp_gate, l1_exp_up,
                      l1_exp_down, l1_ln2_g, l1_ln2_b)
```

```python
import functools
import math

import numpy as np
import jax
import jax.numpy as jnp
from jax import lax
from jax.experimental import pallas as pl
from jax.experimental.pallas import tpu as pltpu

f32 = jnp.float32
bf16 = jnp.bfloat16
i32 = jnp.int32

D_MODEL = 1024
HEAD_DIM = 64
NUM_BUCKETS = 32
MAX_DISTANCE = 128
A_HEADS, A_KV, A_WINDOW = 8, 2, 128
B_HEADS, B_KV, B_WINDOW = 8, 2, 512
CMP_BLOCK, CMP_STRIDE = 32, 16
SLC_BLOCK, SLC_TOPN = 64, 16
C_HEADS, IDX_HEADS, IDX_DIM, C_TOPK_MAX = 16, 8, 32, 256
D_FF, N_EXPERTS = 2816, 8
DEPTH = 2
ALPHA = (2.0 * DEPTH) ** 0.25
LN_EPS = 1e-5
NEG = -1e30
POS_BIG = 1e30
INT_MIN = -(2 ** 31)

TQ = 128
FF_CHUNK = 256
VMEM_LIMIT = 56 * 1024 * 1024


def _cparams(sem):
    return pltpu.CompilerParams(dimension_semantics=sem, vmem_limit_bytes=VMEM_LIMIT)


def _bucket_table(max_dist):
    n = np.arange(max_dist + 1)
    max_exact = NUM_BUCKETS // 2
    nf = np.maximum(n, 1).astype(np.float64)
    large = max_exact + (np.log(nf / max_exact) / math.log(MAX_DISTANCE / max_exact)
                         * (NUM_BUCKETS - max_exact)).astype(np.int64)
    large = np.minimum(large, NUM_BUCKETS - 1)
    return np.where(n < max_exact, n, large).astype(np.int32)


def _bucket_of(dist):
    dist = np.maximum(dist, 0)
    return _bucket_table(int(dist.max()))[dist]


def _proj_in_kernel(*refs, groups):
    ng = len(groups)
    x_ref = refs[0]
    wq_refs = refs[1:1 + ng]
    oq_refs = refs[1 + ng:]
    xb = x_ref[...].astype(bf16)
    tm = xb.shape[0]
    for wq_ref, oq_ref, (kind, gb, hb, dh) in zip(wq_refs, oq_refs, groups):
        if kind == "plain":
            oq_ref[...] = jnp.dot(xb, wq_ref[...], preferred_element_type=f32).astype(oq_ref.dtype)
            continue
        if kind == "keys":
            kk = jnp.dot(xb, wq_ref[...], preferred_element_type=f32).astype(bf16)
            ones = jnp.where(lax.broadcasted_iota(i32, (tm, K_COLS - dh), 1) < 2, 1.0, 0.0).astype(bf16)
            for g in range(gb):
                oq_ref[0, g, :, 0:dh] = kk[:, g * dh:(g + 1) * dh]
                oq_ref[0, g, :, dh:K_COLS] = ones
            continue
        qt = lax.dot_general(wq_ref[...], xb, (((1,), (1,)), ((), ())), preferred_element_type=f32).astype(bf16)
        ones = jnp.where(lax.broadcasted_iota(i32, (V_ROWS - HEAD_DIM, TQ), 0) < 1, 1.0, 0.0).astype(bf16)
        for g in range(gb):
            for h in range(hb):
                rows = slice((g * hb + h) * dh, (g * hb + h + 1) * dh)
                for t in range(tm // TQ):
                    oq_ref[0, g, t, 0:dh, h * TQ:(h + 1) * TQ] = qt[rows, t * TQ:(t + 1) * TQ]
                    if kind == "values":
                        oq_ref[0, g, t, dh:V_ROWS, :] = ones


def _proj_in(x3, w_groups, groups, tm=512):
    bn, s, k = x3.shape
    m = bn * s
    tm = min(tm, s)
    per_b = s // tm
    tiles = tm // TQ
    q_specs, q_shapes = [], []
    for kind, gb, hb, dh in groups:
        if kind == "plain":
            q_specs.append(pl.BlockSpec((tm, gb), lambda i: (i, 0)))
            q_shapes.append(jax.ShapeDtypeStruct((m, gb), hb))
        elif kind == "keys":
            q_specs.append(pl.BlockSpec((1, gb, tm, K_COLS), lambda i: (i // per_b, 0, i % per_b, 0)))
            q_shapes.append(jax.ShapeDtypeStruct((bn, gb, s, K_COLS), bf16))
        else:
            rows = V_ROWS if kind == "values" else dh
            q_specs.append(pl.BlockSpec((1, gb, tiles, rows, hb * TQ), lambda i: (i // per_b, 0, i % per_b, 0, 0)))
            q_shapes.append(jax.ShapeDtypeStruct((bn, gb, s // TQ, rows, hb * TQ), bf16))
    return pl.pallas_call(
        functools.partial(_proj_in_kernel, groups=tuple(groups)),
        grid=(m // tm,),
        in_specs=[pl.BlockSpec((tm, k), lambda i: (i, 0))]
                 + [pl.BlockSpec(w.shape, lambda i: (0, 0)) for w in w_groups],
        out_specs=q_specs,
        out_shape=q_shapes,
        compiler_params=_cparams(("parallel",)),
        name="proj_in",
    )(x3.reshape(m, k), *w_groups)


def _bias_expand_kernel(tab_ref, idx_ref, o_ref):
    h = pl.program_id(0)
    idx = idx_ref[...]
    acc = jnp.zeros(idx.shape, f32)
    for b in range(NUM_BUCKETS):
        acc = jnp.where(idx == b, tab_ref[h, b], acc)
    o_ref[0] = acc


def _bias_expand(table_t, idx, tr):
    nh = table_t.shape[0]
    r, c = idx.shape
    return pl.pallas_call(
        _bias_expand_kernel,
        grid=(nh, r // tr),
        in_specs=[pl.BlockSpec(memory_space=pltpu.SMEM),
                  pl.BlockSpec((tr, c), lambda h, i: (i, 0))],
        out_specs=pl.BlockSpec((1, tr, c), lambda h, i: (h, i, 0)),
        out_shape=jax.ShapeDtypeStruct((nh, r, c), f32),
        compiler_params=_cparams(("parallel", "arbitrary")),
        name="bias_expand",
    )(table_t, jnp.asarray(idx, dtype=i32))


def _layer_norm(z, g, b):
    mu = jnp.mean(z, axis=-1, keepdims=True)
    zc = z - mu
    var = jnp.mean(zc * zc, axis=-1, keepdims=True)
    return zc * lax.rsqrt(var + LN_EPS) * g + b


def _proj_ln_kernel(mix_ref, w_ref, x_ref, g_ref, b_ref, o_ref):
    y = jnp.dot(mix_ref[...], w_ref[...], preferred_element_type=f32)
    o_ref[...] = _layer_norm(ALPHA * x_ref[...] + y, g_ref[...], b_ref[...])


def _proj_res_ln(mix, w, x, g, b, tm=512):
    m, k = mix.shape
    d = w.shape[1]
    return pl.pallas_call(
        _proj_ln_kernel,
        grid=(m // tm,),
        in_specs=[pl.BlockSpec((tm, k), lambda i: (i, 0)),
                  pl.BlockSpec((k, d), lambda i: (0, 0)),
                  pl.BlockSpec((tm, d), lambda i: (i, 0)),
                  pl.BlockSpec((1, d), lambda i: (0, 0)),
                  pl.BlockSpec((1, d), lambda i: (0, 0))],
        out_specs=pl.BlockSpec((tm, d), lambda i: (i, 0)),
        out_shape=jax.ShapeDtypeStruct((m, d), f32),
        compiler_params=_cparams(("parallel",)),
        name="proj_res_ln",
    )(mix, w, x, g.reshape(1, d), b.reshape(1, d))


def _swiglu_accumulate(xb, wg_ref, wu_ref, wd_ref, acc_ref, widx):
    d_ff = wg_ref.shape[-1]
    for c in range(d_ff // FF_CHUNK):
        cols = slice(c * FF_CHUNK, (c + 1) * FF_CHUNK)
        gate = jnp.dot(xb, wg_ref[widx + (slice(None), cols)], preferred_element_type=f32)
        up = jnp.dot(xb, wu_ref[widx + (slice(None), cols)], preferred_element_type=f32)
        h = jax.nn.silu(gate) * up
        acc_ref[...] += jnp.dot(h.astype(bf16), wd_ref[widx + (cols, slice(None))], preferred_element_type=f32)


def _proj_ffn_kernel(mix_ref, wo_ref, x_ref, g1_ref, b1_ref, wg_ref, wu_ref, wd_ref, g2_ref, b2_ref, o_ref, acc_ref):
    y = jnp.dot(mix_ref[...], wo_ref[...], preferred_element_type=f32)
    o_ref[...] = _layer_norm(ALPHA * x_ref[...] + y, g1_ref[...], b1_ref[...])
    acc_ref[...] = jnp.zeros_like(acc_ref)
    _swiglu_accumulate(o_ref[...].astype(bf16), wg_ref, wu_ref, wd_ref, acc_ref, ())
    o_ref[...] = _layer_norm(ALPHA * o_ref[...] + acc_ref[...], g2_ref[...], b2_ref[...])


def _proj_ffn_res_ln(mix, w_out, x, g1, b1, wg, wu, wd, g2, b2, tm=512):
    m, k = mix.shape
    d = x.shape[1]
    d_ff = wg.shape[1]
    single = pl.Buffered(1)
    row = lambda i: (i, 0)
    fixed = lambda i: (0, 0)
    return pl.pallas_call(
        _proj_ffn_kernel,
        grid=(m // tm,),
        in_specs=[pl.BlockSpec((tm, k), row),
                  pl.BlockSpec((k, d), fixed, pipeline_mode=single),
                  pl.BlockSpec((tm, d), row),
                  pl.BlockSpec((1, d), fixed),
                  pl.BlockSpec((1, d), fixed),
                  pl.BlockSpec((d, d_ff), fixed, pipeline_mode=single),
                  pl.BlockSpec((d, d_ff), fixed, pipeline_mode=single),
                  pl.BlockSpec((d_ff, d), fixed, pipeline_mode=single),
                  pl.BlockSpec((1, d), fixed),
                  pl.BlockSpec((1, d), fixed)],
        out_specs=pl.BlockSpec((tm, d), row),
        out_shape=jax.ShapeDtypeStruct((m, d), f32),
        scratch_shapes=[pltpu.VMEM((tm, d), f32)],
        compiler_params=_cparams(("parallel",)),
        name="proj_ffn_res_ln",
    )(mix, w_out, x, g1.reshape(1, d), b1.reshape(1, d), wg, wu, wd, g2.reshape(1, d), b2.reshape(1, d))


MOE_CAP = 288


def _router_gates(x, r_ref, lane):
    logits = jnp.dot(x, r_ref[...], preferred_element_type=f32)
    logits = jnp.where(lane < N_EXPERTS, logits, -jnp.inf)
    m1 = jnp.max(logits, axis=-1, keepdims=True)
    i1 = jnp.min(jnp.where(logits == m1, lane, 128), axis=-1, keepdims=True)
    rest = jnp.where(lane == i1, -jnp.inf, logits)
    m2 = jnp.max(rest, axis=-1, keepdims=True)
    i2 = jnp.min(jnp.where(rest == m2, lane, 128), axis=-1, keepdims=True)
    e2 = jnp.exp(m2 - m1)
    g1 = 1.0 / (1.0 + e2)
    return jnp.where(lane == i1, g1, 0.0) + jnp.where(lane == i2, e2 * g1, 0.0)


def _moe_kernel(x_ref, r_ref, tri_ref, wg_ref, wu_ref, wd_ref, g_ref, b_ref, o_ref,
                split_ref, post_ref, yc_ref):
    e = pl.program_id(1)
    tm = x_ref.shape[0]

    @pl.when(e == 0)
    def _():
        comb = _router_gates(x_ref[...], r_ref, lax.broadcasted_iota(i32, (tm, 128), 1))
        hi = comb.astype(bf16)
        rest = comb - hi.astype(f32)
        mid = rest.astype(bf16)
        split_ref[0] = hi
        split_ref[1] = mid
        split_ref[2] = (rest - mid.astype(f32)).astype(bf16)
        chosen = jnp.where(comb > 0.0, 1.0, 0.0)
        pos = jnp.dot(tri_ref[...], chosen.astype(bf16), preferred_element_type=f32)
        pos = jnp.where(comb > 0.0, pos, -1.0)
        for t in range(tm // 128):
            rows = slice(t * 128, (t + 1) * 128)
            post_ref[:, rows] = pos[rows].T[0:N_EXPERTS]
        hi32 = jnp.floor((pos + 1.0) * (1.0 / 32.0))
        split_ref[3] = hi32.astype(bf16)
        split_ref[4] = (pos + 1.0 - 32.0 * hi32).astype(bf16)
        o_ref[...] = jnp.zeros_like(o_ref)

    pos_row = post_ref[pl.ds(e, 1), :].astype(i32)
    n_tok = jnp.sum(jnp.where(pos_row >= 0, 1.0, 0.0)).astype(i32)
    pick = jnp.where(lax.broadcasted_iota(i32, (128, 128), 0) == e, 1.0, 0.0).astype(bf16)

    def column(t):
        return jnp.dot(split_ref[t], pick, preferred_element_type=f32)

    gate_b = column(0) + column(1) + column(2)
    pos_b = (32.0 * column(3) + column(4) - 1.0).astype(i32)
    lane = lax.broadcasted_iota(i32, (tm, 128), 1)

    def pass_body(p, carry):
        base = p * MOE_CAP
        gather = jnp.where(pos_row == base + lax.broadcasted_iota(i32, (MOE_CAP, tm), 0), 1.0, 0.0).astype(bf16)
        xc = jnp.dot(gather, x_ref[...].astype(bf16), preferred_element_type=f32).astype(bf16)
        yc_ref[...] = jnp.zeros_like(yc_ref)
        _swiglu_accumulate(xc, wg_ref, wu_ref, wd_ref, yc_ref, (0,))
        scatter = jnp.concatenate([jnp.where(pos_b == base + t * 128 + lane, 1.0, 0.0).astype(bf16)
                                   for t in range(-(-MOE_CAP // 128))], axis=1)[:, :MOE_CAP]
        y = jnp.dot(scatter, yc_ref[...].astype(bf16), preferred_element_type=f32)
        o_ref[...] += jnp.concatenate([gate_b] * (y.shape[1] // 128), axis=1) * y
        return carry

    lax.fori_loop(0, (n_tok + MOE_CAP - 1) // MOE_CAP, pass_body, 0)

    @pl.when(e == N_EXPERTS - 1)
    def _():
        o_ref[...] = _layer_norm(ALPHA * x_ref[...] + o_ref[...], g_ref[...], b_ref[...])


def _moe_res_ln(x, router, wg, wu, wd, g, b, tm=1024):
    m, d = x.shape
    ne, _, d_ff = wg.shape
    tm = min(tm, m)
    router_p = jnp.pad(router, ((0, 0), (0, 128 - ne)))
    tri = jnp.asarray(np.tril(np.ones((tm, tm), np.float32), -1), bf16)
    single = pl.Buffered(1)
    return pl.pallas_call(
        _moe_kernel,
        grid=(m // tm, ne),
        in_specs=[pl.BlockSpec((tm, d), lambda i, e: (i, 0), pipeline_mode=single),
                  pl.BlockSpec((d, 128), lambda i, e: (0, 0), pipeline_mode=single),
                  pl.BlockSpec((tm, tm), lambda i, e: (0, 0), pipeline_mode=single),
                  pl.BlockSpec((1, d, d_ff), lambda i, e: (e, 0, 0)),
                  pl.BlockSpec((1, d, d_ff), lambda i, e: (e, 0, 0)),
                  pl.BlockSpec((1, d_ff, d), lambda i, e: (e, 0, 0)),
                  pl.BlockSpec((1, d), lambda i, e: (0, 0)),
                  pl.BlockSpec((1, d), lambda i, e: (0, 0))],
        out_specs=pl.BlockSpec((tm, d), lambda i, e: (i, 0)),
        out_shape=jax.ShapeDtypeStruct((m, d), f32),
        scratch_shapes=[pltpu.VMEM((5, tm, 128), bf16), pltpu.VMEM((N_EXPERTS, tm), f32),
                        pltpu.VMEM((MOE_CAP, d), f32)],
        compiler_params=_cparams(("parallel", "arbitrary")),
        name="moe_res_ln",
    )(x, router_p, tri, wg, wu, wd, g.reshape(1, d), b.reshape(1, d))


def _compress_kernel(blk_ref, pos_ref, w1_ref, w2_ref, o_ref):
    xb = (blk_ref[0].astype(f32) + pos_ref[...]).astype(bf16)
    h = jax.nn.gelu(jnp.dot(xb, w1_ref[...], preferred_element_type=f32))
    o_ref[0] = jnp.dot(h.astype(bf16), w2_ref[...], preferred_element_type=f32).astype(o_ref.dtype)


def _nsa_compress(kv, pos, w1, w2):
    bn, g, s, dh = kv.shape
    nc = s // CMP_STRIDE
    chunks = kv.reshape(bn, g, nc, CMP_STRIDE * dh)
    nxt = jnp.pad(chunks[:, :, 1:], ((0, 0), (0, 0), (0, 1), (0, 0)))
    blocks = jnp.concatenate([chunks, nxt], axis=-1).reshape(bn * g, nc, CMP_BLOCK * dh)
    cin = CMP_BLOCK * dh
    hid = w1.shape[1]
    out = pl.pallas_call(
        _compress_kernel,
        grid=(bn * g,),
        in_specs=[pl.BlockSpec((1, nc, cin), lambda i: (i, 0, 0)),
                  pl.BlockSpec((1, cin), lambda i: (0, 0)),
                  pl.BlockSpec((cin, hid), lambda i: (0, 0)),
                  pl.BlockSpec((hid, dh), lambda i: (0, 0))],
        out_specs=pl.BlockSpec((1, nc, dh), lambda i: (i, 0, 0)),
        out_shape=jax.ShapeDtypeStruct((bn * g, nc, dh), bf16),
        compiler_params=_cparams(("parallel",)),
        name="nsa_compress",
    )(blocks, pos.reshape(1, cin), w1.astype(bf16), w2.astype(bf16))
    return out.reshape(bn, g, nc, dh)


V_ROWS = HEAD_DIM + 16


LOG2E = 1.4426950408889634
K_COLS = 2 * HEAD_DIM


def _masked_flash(i, qs_ref, key_tile, value_tile, mask_bias, delta_ref, m_ref, acc_ref, s_ref, mt_ref, al_ref, chunk,
                  shared_keys):
    width = qs_ref.shape[1]
    n_chunks = width // chunk
    m_ref[...] = jnp.full(m_ref.shape, NEG, f32)
    acc_ref[...] = jnp.zeros(acc_ref.shape, f32)

    def scores_chunk(j, delta, c, shared):
        slot = j % 2
        if c == 0 or not shared_keys:
            shared["kt"] = key_tile(j, c)
            shared["maskb"] = jnp.concatenate([mask_bias(j, c)] * (chunk // TQ), axis=1)
        cols = slice(c * chunk, (c + 1) * chunk)
        s = jnp.dot(shared["kt"], qs_ref[:, cols], preferred_element_type=f32) + shared["maskb"]
        if delta is not None:
            s = s + delta_ref[delta, :, cols]
        s_ref[slot, :, cols] = s
        m_old = m_ref[:, cols]
        m_new = jnp.maximum(m_old, jnp.max(s, axis=0, keepdims=True))
        m_ref[:, cols] = m_new
        mt_ref[slot, :, cols] = m_new
        al_ref[slot, :, cols] = jnp.exp2(m_old - m_new)

    def values_chunk(j, c, shared):
        slot = j % 2
        if c == 0 or not shared_keys:
            shared["vt"] = value_tile(j, c)
        cols = slice(c * chunk, (c + 1) * chunk)
        p = jnp.exp2(s_ref[slot, :, cols] - mt_ref[slot, :, cols])
        acc_ref[:, cols] = (al_ref[slot, :, cols] * acc_ref[:, cols]
                            + jnp.dot(shared["vt"], p.astype(bf16), preferred_element_type=f32))

    def scores(j, delta):
        shared = {}
        for c in range(n_chunks):
            scores_chunk(j, delta, c, shared)

    def values(j):
        shared = {}
        for c in range(n_chunks):
            values_chunk(j, c, shared)

    def values_then_scores(jv, js, delta):
        sv, ss = {}, {}
        for c in range(n_chunks):
            values_chunk(jv, c, sv)
            scores_chunk(js, delta, c, ss)

    @pl.when(i >= 2)
    def _():
        scores(0, None)

        def pipe_body(j, carry):
            values_then_scores(j - 1, j, None)
            return carry

        lax.fori_loop(1, i - 1, pipe_body, 0)
        values_then_scores(i - 2, i - 1, 0)
        values_then_scores(i - 1, i, 1)

    @pl.when(i == 1)
    def _():
        scores(0, 0)
        values_then_scores(0, 1, 1)

    @pl.when(i == 0)
    def _():
        scores(0, 1)

    values(i)


def _flash_operands(near, far):
    nh = near.shape[0]
    width = nh * TQ
    far2 = jnp.repeat(far * LOG2E, TQ).reshape(1, width)
    hi = far2.astype(bf16)
    lo = (far2 - hi.astype(f32)).astype(bf16)
    extra = jnp.concatenate([hi, lo, jnp.zeros((K_COLS - HEAD_DIM - 2, width), bf16)], axis=0)
    neart = near.reshape(nh, TQ, 2, TQ).transpose(2, 3, 0, 1).reshape(2, TQ, width) * LOG2E
    neart = neart - (hi.astype(f32) + lo.astype(f32))
    causal = np.where(np.arange(TQ)[:, None] <= np.arange(TQ)[None, :], 0.0, NEG).astype(np.float32)
    return extra, jnp.stack([neart[0], neart[1] + jnp.asarray(np.tile(causal, (1, nh)))], axis=0)


def _value_tiles(v):
    lead = v.shape[:-2]
    nt = v.shape[-2] // TQ
    vt = jnp.swapaxes(v.reshape(lead + (nt, TQ, HEAD_DIM)), -1, -2)
    return jnp.concatenate([vt, jnp.ones(lead + (nt, 1, TQ), vt.dtype),
                            jnp.zeros(lead + (nt, V_ROWS - HEAD_DIM - 1, TQ), vt.dtype)], axis=-2)


BANDED_TILES = 4


def _banded_kernel(qt_ref, k_ref, vt_ref, bias_ref, neg_ref, sink_ref, o_ref, *, rep, kt_tiles, use_sinks):
    kw = kt_tiles * TQ
    for u in range(qt_ref.shape[2]):
        i = pl.program_id(1) * qt_ref.shape[2] + u
        outs = []
        for g in range(qt_ref.shape[1]):
            qs = jnp.concatenate([(qt_ref[0, g, u].astype(f32) * (HEAD_DIM ** -0.5)).astype(bf16), neg_ref[...]],
                                 axis=0)
            k = k_ref[0, g, pl.ds(pl.multiple_of(i * TQ, TQ), kw), :]
            s = jnp.dot(k, qs, preferred_element_type=f32) + bias_ref[g]
            m = jnp.max(s, axis=0, keepdims=True)
            if use_sinks:
                m = jnp.maximum(m, sink_ref[g])
            p = jnp.exp(s - m).astype(bf16)
            vt = jnp.concatenate([vt_ref[0, g, i + t] for t in range(kt_tiles)], axis=1)
            acc = jnp.dot(vt, p, preferred_element_type=f32)
            l = acc[HEAD_DIM:HEAD_DIM + 1]
            if use_sinks:
                l = l + jnp.exp(sink_ref[g] - m)
            ot = acc[0:HEAD_DIM] / l
            outs += [ot[:, r * TQ:(r + 1) * TQ].T for r in range(rep)]
        o_ref[0, u * TQ:(u + 1) * TQ, :] = jnp.concatenate(outs, axis=1).astype(o_ref.dtype)


def _banded_gqa(qt, k, v, bias, sinks, window, out_dtype):
    bn, g, nt, _, width = qt.shape
    rep = width // TQ
    s = nt * TQ
    pad = -(-(window - 1) // TQ) * TQ
    kw = pad + TQ
    kd = 2 * HEAD_DIM
    flag = jnp.broadcast_to((jnp.arange(pad + s) < pad).astype(k.dtype)[:, None], (bn, g, pad + s, 1))
    kp = jnp.concatenate([jnp.pad(k, ((0, 0), (0, 0), (pad, 0), (0, 0))), flag,
                          jnp.zeros((bn, g, pad + s, kd - HEAD_DIM - 1), k.dtype)], axis=-1)
    vt = _value_tiles(jnp.pad(v, ((0, 0), (0, 0), (pad, 0), (0, 0))))
    neg = np.zeros((kd - HEAD_DIM, width), np.float32)
    neg[0] = NEG
    dist = np.arange(TQ)[:, None] + pad - np.arange(kw)[None, :]
    band = jnp.asarray((dist >= 0) & (dist < window))
    bias_t = jnp.where(band, bias, NEG).reshape(g, rep, TQ, kw).transpose(0, 3, 1, 2).reshape(g, kw, width)
    use_sinks = sinks is not None
    sink_rows = (jnp.repeat(sinks.astype(f32), TQ) if use_sinks else jnp.zeros((g * width,), f32)).reshape(g, 1, width)
    kern = functools.partial(_banded_kernel, rep=rep, kt_tiles=kw // TQ, use_sinks=use_sinks)
    rw = rep * HEAD_DIM
    return pl.pallas_call(
        kern,
        grid=(bn, nt // BANDED_TILES),
        in_specs=[pl.BlockSpec((1, g, BANDED_TILES, HEAD_DIM, width), lambda b, i: (b, 0, i, 0, 0)),
                  pl.BlockSpec((1, g, pad + s, kd), lambda b, i: (b, 0, 0, 0)),
                  pl.BlockSpec((1, g, (pad + s) // TQ, V_ROWS, TQ), lambda b, i: (b, 0, 0, 0, 0)),
                  pl.BlockSpec((g, kw, width), lambda b, i: (0, 0, 0)),
                  pl.BlockSpec((kd - HEAD_DIM, width), lambda b, i: (0, 0)),
                  pl.BlockSpec((g, 1, width), lambda b, i: (0, 0, 0))],
        out_specs=pl.BlockSpec((1, BANDED_TILES * TQ, g * rw), lambda b, i: (b, i, 0)),
        out_shape=jax.ShapeDtypeStruct((bn, s, g * rw), out_dtype),
        compiler_params=_cparams(("parallel", "arbitrary")),
        name="banded_gqa_w%d" % window,
    )(qt, kp, vt, bias_t, jnp.asarray(neg, bf16), sink_rows)


def _nsa_kernel(qt_ref, kc_ref, vct_ref, biasc_ref, ovt_ref, ks_ref, vst_ref, qx_ref, delta_ref, gt_ref, gn_ref,
                owin_ref, o_ref, qs_ref, selb_ref, m_ref, acc_ref, s_ref, mt_ref, al_ref, *, rep, n_sel):
    i = pl.program_id(1)
    t0 = i * TQ
    groups = qt_ref.shape[1]
    nc = kc_ref.shape[2]
    nblk = ovt_ref.shape[0]
    width = rep * TQ
    cmp_end = lax.broadcasted_iota(i32, (nc, TQ), 0) * CMP_STRIDE + (CMP_BLOCK - 1)
    valid_c = t0 + lax.broadcasted_iota(i32, (nc, TQ), 1) >= cmp_end
    blk = lax.broadcasted_iota(i32, (nblk, TQ), 0)
    tq = t0 + lax.broadcasted_iota(i32, (nblk, TQ), 1)
    cur = tq // SLC_BLOCK
    forced = (blk == 0) | (blk == cur) | (blk == cur - 1)
    admitted = blk * SLC_BLOCK <= tq

    o_cmps = []
    for g in range(groups):
        qf = qt_ref[0, g, 0].astype(f32)
        qs = (qf * (HEAD_DIM ** -0.5)).astype(bf16)
        qs_ref[0:HEAD_DIM, g * width:(g + 1) * width] = (qf * (HEAD_DIM ** -0.5 * LOG2E)).astype(bf16)

        s_c = jnp.dot(kc_ref[0, g], qs, preferred_element_type=f32)
        p_cols = []
        p_sum = jnp.zeros((nc, TQ), f32)
        for r in range(rep):
            sr = jnp.where(valid_c, s_c[:, r * TQ:(r + 1) * TQ] + biasc_ref[g * rep + r, 0], NEG)
            m = jnp.max(sr, axis=0, keepdims=True)
            e = jnp.where(valid_c, jnp.exp(sr - m), 0.0)
            l = jnp.sum(e, axis=0, keepdims=True)
            p = e * (1.0 / jnp.maximum(l, 1e-30))
            p_cols.append(p.astype(bf16))
            p_sum = p_sum + p
        o_cmps.append(jnp.dot(vct_ref[0, g], jnp.concatenate(p_cols, axis=1), preferred_element_type=f32))

        imp = jnp.dot(ovt_ref[...], p_sum, preferred_element_type=f32)
        score = jnp.where(forced, POS_BIG, jnp.where(admitted, imp, NEG))
        rank = jnp.zeros((nblk, TQ), i32)
        for kk in range(nblk):
            ck = score[kk:kk + 1, :]
            rank = rank + jnp.where(ck > score, 1, jnp.where(ck == score, jnp.where(blk > kk, 1, 0), 0))
        selb_ref[g] = jnp.where(rank < n_sel, 0.0, NEG)

    per_tile = TQ // SLC_BLOCK

    def mask_bias(j, c):
        rows = [jnp.broadcast_to(selb_ref[c, pl.ds(per_tile * j + t, 1), :], (SLC_BLOCK, TQ))
                for t in range(per_tile)]
        return jnp.concatenate(rows, axis=0)

    qs_ref[HEAD_DIM:K_COLS, :] = qx_ref[...]
    _masked_flash(i, qs_ref,
                  lambda j, c: ks_ref[0, c, pl.ds(pl.multiple_of(j * TQ, TQ), TQ), :],
                  lambda j, c: vst_ref[0, c, j],
                  mask_bias, delta_ref, m_ref, acc_ref, s_ref, mt_ref, al_ref, width, False)
    acc = acc_ref[...]
    o_slc = acc[0:HEAD_DIM] / acc[HEAD_DIM:HEAD_DIM + 1]

    outs = []
    for g in range(groups):
        sig_t = jax.nn.sigmoid(gt_ref[0, g, 0])
        sig_n = jax.nn.sigmoid(gn_ref[0, g])
        for r in range(rep):
            h = g * rep + r
            mixed = (sig_t[r:r + 1] * o_cmps[g][:, r * TQ:(r + 1) * TQ]
                     + sig_t[rep + r:rep + r + 1] * o_slc[:, h * TQ:(h + 1) * TQ])
            o_win = owin_ref[0, :, h * HEAD_DIM:(h + 1) * HEAD_DIM]
            outs.append(mixed.T + sig_n[:, 2 * rep + r:2 * rep + r + 1] * o_win)
    o_ref[0] = jnp.concatenate(outs, axis=1).astype(o_ref.dtype)


def _nsa_mix(qt, kc, vc, bias_c, ks, vs, near, far, gates, o_win):
    bn, g, nt, _, width = qt.shape
    s = nt * TQ
    nc = kc.shape[2]
    rep = width // TQ
    hq = g * rep * HEAD_DIM
    nblk = s // SLC_BLOCK
    n_sel = min(SLC_TOPN, nblk)
    ci = np.arange(nc)[:, None] * CMP_STRIDE
    sj = np.arange(nblk)[None, :] * SLC_BLOCK
    overlap = np.clip(np.minimum(ci + CMP_BLOCK, sj + SLC_BLOCK) - np.maximum(ci, sj), 0, None) / CMP_BLOCK
    overlap[nc - 1] = 0.0
    gates_t = gates.reshape(bn, g, nt, TQ, 3 * rep).transpose(0, 1, 2, 4, 3)
    kern = functools.partial(_nsa_kernel, rep=rep, n_sel=n_sel)
    return pl.pallas_call(
        kern,
        grid=(bn, nt),
        in_specs=[pl.BlockSpec((1, g, 1, HEAD_DIM, width), lambda b, i: (b, 0, i, 0, 0)),
                  pl.BlockSpec((1, g, nc, HEAD_DIM), lambda b, i: (b, 0, 0, 0)),
                  pl.BlockSpec((1, g, HEAD_DIM, nc), lambda b, i: (b, 0, 0, 0)),
                  pl.BlockSpec((g * rep, 1, nc, TQ), lambda b, i: (0, i, 0, 0)),
                  pl.BlockSpec((nblk, nc), lambda b, i: (0, 0)),
                  pl.BlockSpec((1, g, s, K_COLS), lambda b, i: (b, 0, 0, 0)),
                  pl.BlockSpec((1, g, nt, V_ROWS, TQ), lambda b, i: (b, 0, 0, 0, 0)),
                  pl.BlockSpec((K_COLS - HEAD_DIM, g * width), lambda b, i: (0, 0)),
                  pl.BlockSpec((2, TQ, g * width), lambda b, i: (0, 0, 0)),
                  pl.BlockSpec((1, g, 1, 3 * rep, TQ), lambda b, i: (b, 0, i, 0, 0)),
                  pl.BlockSpec((1, g, TQ, 3 * rep), lambda b, i: (b, 0, i, 0)),
                  pl.BlockSpec((1, TQ, hq), lambda b, i: (b, i, 0))],
        out_specs=pl.BlockSpec((1, TQ, hq), lambda b, i: (b, i, 0)),
        out_shape=jax.ShapeDtypeStruct((bn, s, hq), bf16),
        scratch_shapes=[pltpu.VMEM((K_COLS, g * width), bf16), pltpu.VMEM((g, nblk, TQ), f32),
                        pltpu.VMEM((1, g * width), f32), pltpu.VMEM((V_ROWS, g * width), f32),
                        pltpu.VMEM((2, TQ, g * width), f32), pltpu.VMEM((2, 1, g * width), f32),
                        pltpu.VMEM((2, 1, g * width), f32)],
        compiler_params=_cparams(("parallel", "arbitrary")),
        name="nsa_mix",
    )(qt, kc, jnp.swapaxes(vc, 2, 3), bias_c, jnp.asarray(overlap.T, f32), ks, vs,
      *_flash_operands(near, far), gates_t, gates, o_win)


RADIX_BITS_PER_CHECK = 8


def _dsa_kernel(qt_ref, k_ref, vt_ref, qit_ref, ki_ref, wit_ref, qx_ref, delta_ref, tri_ref, o_ref,
                keys_ref, qs_ref, m_ref, acc_ref, s_ref, mt_ref, al_ref, *, topk, chunk):
    i = pl.program_id(1)
    t0 = i * TQ
    width = qt_ref.shape[3]
    nh = width // TQ
    nih = qit_ref.shape[3] // TQ
    krow = lax.broadcasted_iota(i32, (TQ, TQ), 0)
    qcol = lax.broadcasted_iota(i32, (TQ, TQ), 1)
    n_pairs = jnp.maximum(i // 2 + 1, -(-topk // (2 * TQ)))

    qit = qit_ref[0, 0]
    wit = wit_ref[0, 0]

    def score_tile(j):
        kt = ki_ref[0, pl.ds(pl.multiple_of(j * TQ, TQ), TQ), :]
        rel = jnp.maximum(jnp.dot(kt, qit, preferred_element_type=f32), 0.0)
        sc = jnp.zeros((TQ, TQ), f32)
        for h in range(nih):
            sc = sc + wit[h:h + 1, :] * rel[:, h * TQ:(h + 1) * TQ]
        sc = jnp.where(j * TQ + krow <= t0 + qcol, sc, NEG)
        keys_ref[j] = jnp.where(sc == 0.0, 0.0, sc)

    def score_body(jj, carry):
        for u in range(4):
            score_tile(4 * jj + u)
        return carry

    lax.fori_loop(0, (n_pairs + 1) // 2, score_body, 0)

    def code_to_float(code):
        code = jnp.clip(code, -0x7F800001, 0x7F800000)
        bits = jnp.where(code < 0, code ^ 0x7FFFFFFF, code)
        bits = jnp.where((bits > 0) & (bits < 0x00800000), 0x00800000, bits)
        return pltpu.bitcast(bits, f32)

    def count(pred_fn):
        def body(jj, cnt):
            return (cnt + jnp.where(pred_fn(keys_ref[2 * jj]), 1.0, 0.0)
                    + jnp.where(pred_fn(keys_ref[2 * jj + 1]), 1.0, 0.0))
        cnt = lax.fori_loop(0, n_pairs, body, jnp.zeros((TQ, TQ), f32))
        return jnp.sum(cnt, axis=0, keepdims=True)

    def bits_body(state):
        t, prefix, n_ge = state
        for u in range(RADIX_BITS_PER_CHECK):
            cand = prefix ^ jnp.left_shift(jnp.int32(1), 31 - (t + u))
            cand_f = code_to_float(cand)
            cnt = count(lambda sj: sj >= cand_f)
            prefix = jnp.where(cnt >= topk, cand, prefix)
            n_ge = jnp.where(cnt >= topk, cnt, n_ge)
        return t + RADIX_BITS_PER_CHECK, prefix, n_ge

    n_all = (2 * n_pairs * TQ).astype(f32)
    _, code, n_ge = lax.while_loop(lambda st: (st[0] < 32) & (jnp.max(st[2]) > topk), bits_body,
                                   (jnp.int32(0), jnp.full((1, TQ), INT_MIN, i32), jnp.full((1, TQ), n_all, f32)))
    thr = code_to_float(code)

    @pl.when(jnp.max(n_ge) > topk)
    def _():
        need = topk - count(lambda sj: sj > thr)

        def fix_body(j, running):
            sj = keys_ref[j]
            tie = jnp.where(sj == thr, 1.0, 0.0)
            before = jnp.dot(tri_ref[...], tie.astype(bf16), preferred_element_type=f32) + running
            keys_ref[j] = jnp.where(before >= need, jnp.where(sj == thr, -jnp.inf, sj), sj)
            return running + jnp.sum(tie, axis=0, keepdims=True)

        lax.fori_loop(0, 2 * n_pairs, fix_body, jnp.zeros((1, TQ), f32))

    qs_ref[0:HEAD_DIM, :] = (qt_ref[0, 0].astype(f32) * (HEAD_DIM ** -0.5 * LOG2E)).astype(bf16)
    qs_ref[HEAD_DIM:K_COLS, :] = qx_ref[...]
    _masked_flash(i, qs_ref,
                  lambda j, c: k_ref[0, pl.ds(pl.multiple_of(j * TQ, TQ), TQ), :],
                  lambda j, c: vt_ref[0, j],
                  lambda j, c: jnp.where(keys_ref[j] >= thr, 0.0, NEG),
                  delta_ref, m_ref, acc_ref, s_ref, mt_ref, al_ref, chunk, True)
    acc = acc_ref[...]
    ot = acc[0:HEAD_DIM] / acc[HEAD_DIM:HEAD_DIM + 1]
    o_ref[0] = jnp.concatenate([ot[:, h * TQ:(h + 1) * TQ].T for h in range(nh)], axis=1).astype(o_ref.dtype)


def _dsa_attention(qt, k, v, qit, ki, wi, near, far):
    bn, nt, _, width = qt.shape
    s = nt * TQ
    nh = width // TQ
    hq = nh * HEAD_DIM
    nih = qit.shape[3] // TQ
    topk = min(C_TOPK_MAX, s // 4)
    wit = wi.reshape(bn, nt, TQ, nih).transpose(0, 1, 3, 2)
    tri = np.tril(np.ones((TQ, TQ), np.float32), -1)
    kern = functools.partial(_dsa_kernel, topk=topk, chunk=512)
    nt_scr = max(nt, -(-topk // TQ))
    return pl.pallas_call(
        kern,
        grid=(bn, nt),
        in_specs=[pl.BlockSpec((1, 1, HEAD_DIM, width), lambda b, i: (b, i, 0, 0)),
                  pl.BlockSpec((1, s, K_COLS), lambda b, i: (b, 0, 0)),
                  pl.BlockSpec((1, nt, V_ROWS, TQ), lambda b, i: (b, 0, 0, 0)),
                  pl.BlockSpec((1, 1, IDX_DIM, nih * TQ), lambda b, i: (b, i, 0, 0)),
                  pl.BlockSpec((1, s, IDX_DIM), lambda b, i: (b, 0, 0)),
                  pl.BlockSpec((1, 1, nih, TQ), lambda b, i: (b, i, 0, 0)),
                  pl.BlockSpec((K_COLS - HEAD_DIM, width), lambda b, i: (0, 0)),
                  pl.BlockSpec((2, TQ, width), lambda b, i: (0, 0, 0)),
                  pl.BlockSpec((TQ, TQ), lambda b, i: (0, 0))],
        out_specs=pl.BlockSpec((1, TQ, hq), lambda b, i: (b, i, 0)),
        out_shape=jax.ShapeDtypeStruct((bn, s, hq), bf16),
        scratch_shapes=[pltpu.VMEM((nt_scr, TQ, TQ), f32), pltpu.VMEM((K_COLS, width), bf16),
                        pltpu.VMEM((1, width), f32), pltpu.VMEM((V_ROWS, width), f32),
                        pltpu.VMEM((2, TQ, width), f32), pltpu.VMEM((2, 1, width), f32),
                        pltpu.VMEM((2, 1, width), f32)],
        compiler_params=_cparams(("parallel", "arbitrary")),
        name="dsa_attention",
    )(qt, k, v, qit, ki, wit, *_flash_operands(near, far), jnp.asarray(tri, bf16))


def _heads_first(a, g):
    bn, s, _ = a.shape
    return a.reshape(bn, s, g, HEAD_DIM).transpose(0, 2, 1, 3)


def _even_layer(x, table_t, w_in, sinks, cmpk_pos, cmpk_w1, cmpk_w2, cmpv_pos, cmpv_w1, cmpv_w2,
                w_out, ln1_g, ln1_b, ffn_gate, ffn_up, ffn_down, ln2_g, ln2_b):
    bn, s, d = x.shape
    xf = x.reshape(bn * s, d)
    a_q, a_kv, b_q, b_kv = A_HEADS * HEAD_DIM, A_KV * HEAD_DIM, B_HEADS * HEAD_DIM, B_KV * HEAD_DIM
    n_main = a_q + 2 * a_kv + b_q + 6 * b_kv
    rb = B_HEADS // B_KV
    w_tail = jnp.pad(w_in[:, n_main:], ((0, 0), (0, 128 - 3 * B_HEADS))).astype(bf16)
    cols = np.cumsum([0, a_q, a_kv, a_kv, b_q] + [b_kv] * 6)
    w_kv = jnp.concatenate([w_in[:, cols[1]:cols[3]], w_in[:, cols[4]:cols[6]], w_in[:, cols[8]:cols[10]]],
                           axis=1).astype(bf16)
    main, tail, qt_a, qt_b, ks1, vst = _proj_in(
        x, [w_kv, w_tail, w_in[:, cols[0]:cols[1]].T.astype(bf16), w_in[:, cols[3]:cols[4]].T.astype(bf16),
            w_in[:, cols[6]:cols[7]].astype(bf16), w_in[:, cols[7]:cols[8]].T.astype(bf16)],
        [("plain", w_kv.shape[1], bf16, 0), ("plain", 128, f32, 0),
         ("queries", A_KV, A_HEADS // A_KV, HEAD_DIM), ("queries", B_KV, rb, HEAD_DIM),
         ("keys", B_KV, 1, HEAD_DIM), ("values", B_KV, 1, HEAD_DIM)])
    main = main.reshape(bn, s, w_kv.shape[1])
    gates = tail[:, :3 * B_HEADS]
    gates = gates.reshape(bn, s, 3, B_KV, rb).transpose(0, 3, 1, 2, 4).reshape(bn, B_KV, s, 3 * rb)
    kcols = np.cumsum([0, a_kv, a_kv] + [b_kv] * 4)
    ka, va, kc_in, vc_in, kwn, vwn = [main[:, :, kcols[t]:kcols[t + 1]] for t in range(6)]

    near_idx = _bucket_of(np.arange(TQ)[:, None] + TQ - np.arange(2 * TQ)[None, :])
    near = _bias_expand(table_t, near_idx, TQ)
    pad_b = -(-(B_WINDOW - 1) // TQ) * TQ
    win_idx = _bucket_of(np.arange(TQ)[:, None] + pad_b - np.arange(pad_b + TQ)[None, :])
    bias_win = _bias_expand(table_t[A_HEADS:A_HEADS + B_HEADS], win_idx, TQ)
    nc = s // CMP_STRIDE
    nt = s // TQ
    cmp_idx = _bucket_of((np.arange(nt)[:, None, None] * TQ + np.arange(TQ)[None, None, :])
                         - (np.arange(nc)[None, :, None] * CMP_STRIDE + CMP_BLOCK - 1)).reshape(nt * nc, TQ)
    bias_c = _bias_expand(table_t[A_HEADS:A_HEADS + B_HEADS], cmp_idx, min(nt * nc, 512))
    bias_c = bias_c.reshape(B_HEADS, nt, nc, TQ)
    far = table_t[:, NUM_BUCKETS - 1]

    out_a =_banded_gqa(qt_a, _heads_first(ka, A_KV), _heads_first(va, A_KV), near[:A_HEADS], sinks, A_WINDOW, bf16)
    o_win = _banded_gqa(qt_b, _heads_first(kwn, B_KV), _heads_first(vwn, B_KV), bias_win, None, B_WINDOW, f32)
    kc = _nsa_compress(_heads_first(kc_in, B_KV), cmpk_pos, cmpk_w1, cmpk_w2)
    vc = _nsa_compress(_heads_first(vc_in, B_KV), cmpv_pos, cmpv_w1, cmpv_w2)
    out_b = _nsa_mix(qt_b, kc, vc, bias_c, ks1, vst,
                     near[A_HEADS:A_HEADS + B_HEADS], far[A_HEADS:A_HEADS + B_HEADS], gates, o_win)
    mix = jnp.concatenate([out_a, out_b], axis=-1).reshape(bn * s, a_q + b_q)
    x2 = _proj_ffn_res_ln(mix, w_out.astype(bf16), xf, ln1_g, ln1_b, ffn_gate.astype(bf16), ffn_up.astype(bf16),
                          ffn_down.astype(bf16), ln2_g, ln2_b)
    return x2.reshape(bn, s, d), near, far


def _odd_layer(x, near, far, w_in, w_out, ln1_g, ln1_b, router, exp_gate, exp_up, exp_down, ln2_g, ln2_b):
    bn, s, d = x.shape
    xf = x.reshape(bn * s, d)
    c_q = C_HEADS * HEAD_DIM
    n_main = c_q + 2 * HEAD_DIM + IDX_HEADS * IDX_DIM
    w_tail = jnp.pad(w_in[:, n_main:], ((0, 0), (0, 128 - IDX_DIM - IDX_HEADS))).astype(bf16)
    tail, qt, qit, k1, vt = _proj_in(
        x, [w_tail, w_in[:, :c_q].T.astype(bf16), w_in[:, c_q + 2 * HEAD_DIM:n_main].T.astype(bf16),
            w_in[:, c_q:c_q + HEAD_DIM].astype(bf16), w_in[:, c_q + HEAD_DIM:c_q + 2 * HEAD_DIM].T.astype(bf16)],
        [("plain", 128, f32, 0), ("queries", 1, C_HEADS, HEAD_DIM), ("queries", 1, IDX_HEADS, IDX_DIM),
         ("keys", 1, 1, HEAD_DIM), ("values", 1, 1, HEAD_DIM)])
    tail = tail.reshape(bn, s, 128)
    ki = tail[:, :, :IDX_DIM].astype(bf16)
    wi = tail[:, :, IDX_DIM:IDX_DIM + IDX_HEADS]
    mix = _dsa_attention(qt[:, 0], k1[:, 0], vt[:, 0], qit[:, 0], ki, wi, near[:C_HEADS], far[:C_HEADS])
    mix = mix.reshape(bn * s, c_q)
    x1 = _proj_res_ln(mix, w_out.astype(bf16), xf, ln1_g, ln1_b)
    x2 = _moe_res_ln(x1, router, exp_gate.astype(bf16), exp_up.astype(bf16), exp_down.astype(bf16), ln2_g, ln2_b)
    return x2.reshape(bn, s, d)


def kernel(x, rel_bias, l0_w_in, l0_sinks, l0_cmpk_pos, l0_cmpk_w1, l0_cmpk_w2, l0_cmpv_pos, l0_cmpv_w1,
           l0_cmpv_w2, l0_w_out, l0_ln1_g, l0_ln1_b, l0_ffn_gate, l0_ffn_up, l0_ffn_down, l0_ln2_g, l0_ln2_b,
           l1_w_in, l1_w_out, l1_ln1_g, l1_ln1_b, l1_router, l1_exp_gate, l1_exp_up, l1_exp_down, l1_ln2_g,
           l1_ln2_b):
    table_t = rel_bias.T.astype(f32)
    x, near, far = _even_layer(x, table_t, l0_w_in, l0_sinks, l0_cmpk_pos, l0_cmpk_w1, l0_cmpk_w2, l0_cmpv_pos,
                               l0_cmpv_w1, l0_cmpv_w2, l0_w_out, l0_ln1_g, l0_ln1_b, l0_ffn_gate, l0_ffn_up,
                               l0_ffn_down, l0_ln2_g, l0_ln2_b)
    return _odd_layer(x, near, far, l1_w_in, l1_w_out, l1_ln1_g, l1_ln1_b, l1_router, l1_exp_gate, l1_exp_up,
                      l1_exp_down, l1_ln2_g, l1_ln2_b)
```

```python
import functools
import math

import numpy as np
import jax
import jax.numpy as jnp
from jax import lax
from jax.experimental import pallas as pl
from jax.experimental.pallas import tpu as pltpu

f32 = jnp.float32
bf16 = jnp.bfloat16
i32 = jnp.int32

D_MODEL = 1024
HEAD_DIM = 64
NUM_BUCKETS = 32
MAX_DISTANCE = 128
A_HEADS, A_KV, A_WINDOW = 8, 2, 128
B_HEADS, B_KV, B_WINDOW = 8, 2, 512
CMP_BLOCK, CMP_STRIDE = 32, 16
SLC_BLOCK, SLC_TOPN = 64, 16
C_HEADS, IDX_HEADS, IDX_DIM, C_TOPK_MAX = 16, 8, 32, 256
D_FF, N_EXPERTS = 2816, 8
DEPTH = 2
ALPHA = (2.0 * DEPTH) ** 0.25
LN_EPS = 1e-5
NEG = -1e30
POS_BIG = 1e30
INT_MIN = -(2 ** 31)

TQ = 128
FF_CHUNK = 256
VMEM_LIMIT = 56 * 1024 * 1024


def _cparams(sem):
    return pltpu.CompilerParams(dimension_semantics=sem, vmem_limit_bytes=VMEM_LIMIT)


def _bucket_table(max_dist):
    n = np.arange(max_dist + 1)
    max_exact = NUM_BUCKETS // 2
    nf = np.maximum(n, 1).astype(np.float64)
    large = max_exact + (np.log(nf / max_exact) / math.log(MAX_DISTANCE / max_exact)
                         * (NUM_BUCKETS - max_exact)).astype(np.int64)
    large = np.minimum(large, NUM_BUCKETS - 1)
    return np.where(n < max_exact, n, large).astype(np.int32)


def _bucket_of(dist):
    dist = np.maximum(dist, 0)
    return _bucket_table(int(dist.max()))[dist]


def _proj_in_kernel(*refs, groups):
    ng = len(groups)
    x_ref = refs[0]
    wq_refs = refs[1:1 + ng]
    oq_refs = refs[1 + ng:]
    xb = x_ref[...].astype(bf16)
    tm = xb.shape[0]
    for wq_ref, oq_ref, (kind, gb, hb, dh) in zip(wq_refs, oq_refs, groups):
        if kind == "plain":
            oq_ref[...] = jnp.dot(xb, wq_ref[...], preferred_element_type=f32).astype(oq_ref.dtype)
            continue
        if kind == "keys":
            kk = jnp.dot(xb, wq_ref[...], preferred_element_type=f32).astype(bf16)
            ones = jnp.where(lax.broadcasted_iota(i32, (tm, K_COLS - dh), 1) < 2, 1.0, 0.0).astype(bf16)
            for g in range(gb):
                oq_ref[0, g, :, 0:dh] = kk[:, g * dh:(g + 1) * dh]
                oq_ref[0, g, :, dh:K_COLS] = ones
            continue
        qt = lax.dot_general(wq_ref[...], xb, (((1,), (1,)), ((), ())), preferred_element_type=f32).astype(bf16)
        ones = jnp.where(lax.broadcasted_iota(i32, (V_ROWS - HEAD_DIM, TQ), 0) < 1, 1.0, 0.0).astype(bf16)
        for g in range(gb):
            for h in range(hb):
                rows = slice((g * hb + h) * dh, (g * hb + h + 1) * dh)
                for t in range(tm // TQ):
                    oq_ref[0, g, t, 0:dh, h * TQ:(h + 1) * TQ] = qt[rows, t * TQ:(t + 1) * TQ]
                    if kind == "values":
                        oq_ref[0, g, t, dh:V_ROWS, :] = ones


def _proj_in(x3, w_groups, groups, tm=512):
    bn, s, k = x3.shape
    m = bn * s
    tm = min(tm, s)
    per_b = s // tm
    tiles = tm // TQ
    q_specs, q_shapes = [], []
    for kind, gb, hb, dh in groups:
        if kind == "plain":
            q_specs.append(pl.BlockSpec((tm, gb), lambda i: (i, 0)))
            q_shapes.append(jax.ShapeDtypeStruct((m, gb), hb))
        elif kind == "keys":
            q_specs.append(pl.BlockSpec((1, gb, tm, K_COLS), lambda i: (i // per_b, 0, i % per_b, 0)))
            q_shapes.append(jax.ShapeDtypeStruct((bn, gb, s, K_COLS), bf16))
        else:
            rows = V_ROWS if kind == "values" else dh
            q_specs.append(pl.BlockSpec((1, gb, tiles, rows, hb * TQ), lambda i: (i // per_b, 0, i % per_b, 0, 0)))
            q_shapes.append(jax.ShapeDtypeStruct((bn, gb, s // TQ, rows, hb * TQ), bf16))
    return pl.pallas_call(
        functools.partial(_proj_in_kernel, groups=tuple(groups)),
        grid=(m // tm,),
        in_specs=[pl.BlockSpec((tm, k), lambda i: (i, 0))]
                 + [pl.BlockSpec(w.shape, lambda i: (0, 0)) for w in w_groups],
        out_specs=q_specs,
        out_shape=q_shapes,
        compiler_params=_cparams(("parallel",)),
        name="proj_in",
    )(x3.reshape(m, k), *w_groups)


def _bias_expand_kernel(tab_ref, idx_ref, o_ref):
    h = pl.program_id(0)
    idx = idx_ref[...]
    acc = jnp.zeros(idx.shape, f32)
    for b in range(NUM_BUCKETS):
        acc = jnp.where(idx == b, tab_ref[h, b], acc)
    o_ref[0] = acc


def _bias_expand(table_t, idx, tr):
    nh = table_t.shape[0]
    r, c = idx.shape
    return pl.pallas_call(
        _bias_expand_kernel,
        grid=(nh, r // tr),
        in_specs=[pl.BlockSpec(memory_space=pltpu.SMEM),
                  pl.BlockSpec((tr, c), lambda h, i: (i, 0))],
        out_specs=pl.BlockSpec((1, tr, c), lambda h, i: (h, i, 0)),
        out_shape=jax.ShapeDtypeStruct((nh, r, c), f32),
        compiler_params=_cparams(("parallel", "arbitrary")),
        name="bias_expand",
    )(table_t, jnp.asarray(idx, dtype=i32))


def _layer_norm(z, g, b):
    mu = jnp.mean(z, axis=-1, keepdims=True)
    zc = z - mu
    var = jnp.mean(zc * zc, axis=-1, keepdims=True)
    return zc * lax.rsqrt(var + LN_EPS) * g + b


def _proj_ln_kernel(mix_ref, w_ref, x_ref, g_ref, b_ref, o_ref):
    y = jnp.dot(mix_ref[...], w_ref[...], preferred_element_type=f32)
    o_ref[...] = _layer_norm(ALPHA * x_ref[...] + y, g_ref[...], b_ref[...])


def _proj_res_ln(mix, w, x, g, b, tm=512):
    m, k = mix.shape
    d = w.shape[1]
    return pl.pallas_call(
        _proj_ln_kernel,
        grid=(m // tm,),
        in_specs=[pl.BlockSpec((tm, k), lambda i: (i, 0)),
                  pl.BlockSpec((k, d), lambda i: (0, 0)),
                  pl.BlockSpec((tm, d), lambda i: (i, 0)),
                  pl.BlockSpec((1, d), lambda i: (0, 0)),
                  pl.BlockSpec((1, d), lambda i: (0, 0))],
        out_specs=pl.BlockSpec((tm, d), lambda i: (i, 0)),
        out_shape=jax.ShapeDtypeStruct((m, d), f32),
        compiler_params=_cparams(("parallel",)),
        name="proj_res_ln",
    )(mix, w, x, g.reshape(1, d), b.reshape(1, d))


def _swiglu_accumulate(xb, wg_ref, wu_ref, wd_ref, acc_ref, widx):
    d_ff = wg_ref.shape[-1]
    for c in range(d_ff // FF_CHUNK):
        cols = slice(c * FF_CHUNK, (c + 1) * FF_CHUNK)
        gate = jnp.dot(xb, wg_ref[widx + (slice(None), cols)], preferred_element_type=f32)
        up = jnp.dot(xb, wu_ref[widx + (slice(None), cols)], preferred_element_type=f32)
        h = jax.nn.silu(gate) * up
        acc_ref[...] += jnp.dot(h.astype(bf16), wd_ref[widx + (cols, slice(None))], preferred_element_type=f32)


def _proj_ffn_kernel(mix_ref, wo_ref, x_ref, g1_ref, b1_ref, wg_ref, wu_ref, wd_ref, g2_ref, b2_ref, o_ref, acc_ref):
    y = jnp.dot(mix_ref[...], wo_ref[...], preferred_element_type=f32)
    o_ref[...] = _layer_norm(ALPHA * x_ref[...] + y, g1_ref[...], b1_ref[...])
    acc_ref[...] = jnp.zeros_like(acc_ref)
    _swiglu_accumulate(o_ref[...].astype(bf16), wg_ref, wu_ref, wd_ref, acc_ref, ())
    o_ref[...] = _layer_norm(ALPHA * o_ref[...] + acc_ref[...], g2_ref[...], b2_ref[...])


def _proj_ffn_res_ln(mix, w_out, x, g1, b1, wg, wu, wd, g2, b2, tm=512):
    m, k = mix.shape
    d = x.shape[1]
    d_ff = wg.shape[1]
    single = pl.Buffered(1)
    row = lambda i: (i, 0)
    fixed = lambda i: (0, 0)
    return pl.pallas_call(
        _proj_ffn_kernel,
        grid=(m // tm,),
        in_specs=[pl.BlockSpec((tm, k), row),
                  pl.BlockSpec((k, d), fixed, pipeline_mode=single),
                  pl.BlockSpec((tm, d), row),
                  pl.BlockSpec((1, d), fixed),
                  pl.BlockSpec((1, d), fixed),
                  pl.BlockSpec((d, d_ff), fixed, pipeline_mode=single),
                  pl.BlockSpec((d, d_ff), fixed, pipeline_mode=single),
                  pl.BlockSpec((d_ff, d), fixed, pipeline_mode=single),
                  pl.BlockSpec((1, d), fixed),
                  pl.BlockSpec((1, d), fixed)],
        out_specs=pl.BlockSpec((tm, d), row),
        out_shape=jax.ShapeDtypeStruct((m, d), f32),
        scratch_shapes=[pltpu.VMEM((tm, d), f32)],
        compiler_params=_cparams(("parallel",)),
        name="proj_ffn_res_ln",
    )(mix, w_out, x, g1.reshape(1, d), b1.reshape(1, d), wg, wu, wd, g2.reshape(1, d), b2.reshape(1, d))


MOE_CAP = 288


def _router_gates(x, r_ref, lane):
    logits = jnp.dot(x, r_ref[...], preferred_element_type=f32)
    logits = jnp.where(lane < N_EXPERTS, logits, -jnp.inf)
    m1 = jnp.max(logits, axis=-1, keepdims=True)
    i1 = jnp.min(jnp.where(logits == m1, lane, 128), axis=-1, keepdims=True)
    rest = jnp.where(lane == i1, -jnp.inf, logits)
    m2 = jnp.max(rest, axis=-1, keepdims=True)
    i2 = jnp.min(jnp.where(rest == m2, lane, 128), axis=-1, keepdims=True)
    e2 = jnp.exp(m2 - m1)
    g1 = 1.0 / (1.0 + e2)
    return jnp.where(lane == i1, g1, 0.0) + jnp.where(lane == i2, e2 * g1, 0.0)


def _moe_kernel(x_ref, r_ref, tri_ref, wg_ref, wu_ref, wd_ref, g_ref, b_ref, o_ref,
                split_ref, post_ref, yc_ref):
    e = pl.program_id(1)
    tm = x_ref.shape[0]

    @pl.when(e == 0)
    def _():
        comb = _router_gates(x_ref[...], r_ref, lax.broadcasted_iota(i32, (tm, 128), 1))
        hi = comb.astype(bf16)
        rest = comb - hi.astype(f32)
        mid = rest.astype(bf16)
        split_ref[0] = hi
        split_ref[1] = mid
        split_ref[2] = (rest - mid.astype(f32)).astype(bf16)
        chosen = jnp.where(comb > 0.0, 1.0, 0.0)
        pos = jnp.dot(tri_ref[...], chosen.astype(bf16), preferred_element_type=f32)
        pos = jnp.where(comb > 0.0, pos, -1.0)
        for t in range(tm // 128):
            rows = slice(t * 128, (t + 1) * 128)
            post_ref[:, rows] = pos[rows].T[0:N_EXPERTS]
        hi32 = jnp.floor((pos + 1.0) * (1.0 / 32.0))
        split_ref[3] = hi32.astype(bf16)
        split_ref[4] = (pos + 1.0 - 32.0 * hi32).astype(bf16)
        o_ref[...] = jnp.zeros_like(o_ref)

    pos_row = post_ref[pl.ds(e, 1), :].astype(i32)
    n_tok = jnp.sum(jnp.where(pos_row >= 0, 1.0, 0.0)).astype(i32)
    pick = jnp.where(lax.broadcasted_iota(i32, (128, 128), 0) == e, 1.0, 0.0).astype(bf16)

    def column(t):
        return jnp.dot(split_ref[t], pick, preferred_element_type=f32)

    gate_b = column(0) + column(1) + column(2)
    pos_b = (32.0 * column(3) + column(4) - 1.0).astype(i32)
    lane = lax.broadcasted_iota(i32, (tm, 128), 1)

    def pass_body(p, carry):
        base = p * MOE_CAP
        gather = jnp.where(pos_row == base + lax.broadcasted_iota(i32, (MOE_CAP, tm), 0), 1.0, 0.0).astype(bf16)
        xc = jnp.dot(gather, x_ref[...].astype(bf16), preferred_element_type=f32).astype(bf16)
        yc_ref[...] = jnp.zeros_like(yc_ref)
        _swiglu_accumulate(xc, wg_ref, wu_ref, wd_ref, yc_ref, (0,))
        scatter = jnp.concatenate([jnp.where(pos_b == base + t * 128 + lane, 1.0, 0.0).astype(bf16)
                                   for t in range(-(-MOE_CAP // 128))], axis=1)[:, :MOE_CAP]
        y = jnp.dot(scatter, yc_ref[...].astype(bf16), preferred_element_type=f32)
        o_ref[...] += jnp.concatenate([gate_b] * (y.shape[1] // 128), axis=1) * y
        return carry

    lax.fori_loop(0, (n_tok + MOE_CAP - 1) // MOE_CAP, pass_body, 0)

    @pl.when(e == N_EXPERTS - 1)
    def _():
        o_ref[...] = _layer_norm(ALPHA * x_ref[...] + o_ref[...], g_ref[...], b_ref[...])


def _moe_res_ln(x, router, wg, wu, wd, g, b, tm=1024):
    m, d = x.shape
    ne, _, d_ff = wg.shape
    tm = min(tm, m)
    router_p = jnp.pad(router, ((0, 0), (0, 128 - ne)))
    tri = jnp.asarray(np.tril(np.ones((tm, tm), np.float32), -1), bf16)
    single = pl.Buffered(1)
    return pl.pallas_call(
        _moe_kernel,
        grid=(m // tm, ne),
        in_specs=[pl.BlockSpec((tm, d), lambda i, e: (i, 0), pipeline_mode=single),
                  pl.BlockSpec((d, 128), lambda i, e: (0, 0), pipeline_mode=single),
                  pl.BlockSpec((tm, tm), lambda i, e: (0, 0), pipeline_mode=single),
                  pl.BlockSpec((1, d, d_ff), lambda i, e: (e, 0, 0)),
                  pl.BlockSpec((1, d, d_ff), lambda i, e: (e, 0, 0)),
                  pl.BlockSpec((1, d_ff, d), lambda i, e: (e, 0, 0)),
                  pl.BlockSpec((1, d), lambda i, e: (0, 0)),
                  pl.BlockSpec((1, d), lambda i, e: (0, 0))],
        out_specs=pl.BlockSpec((tm, d), lambda i, e: (i, 0)),
        out_shape=jax.ShapeDtypeStruct((m, d), f32),
        scratch_shapes=[pltpu.VMEM((5, tm, 128), bf16), pltpu.VMEM((N_EXPERTS, tm), f32),
                        pltpu.VMEM((MOE_CAP, d), f32)],
        compiler_params=_cparams(("parallel", "arbitrary")),
        name="moe_res_ln",
    )(x, router_p, tri, wg, wu, wd, g.reshape(1, d), b.reshape(1, d))


def _compress_kernel(blk_ref, pos_ref, w1_ref, w2_ref, o_ref):
    xb = (blk_ref[0].astype(f32) + pos_ref[...]).astype(bf16)
    h = jax.nn.gelu(jnp.dot(xb, w1_ref[...], preferred_element_type=f32))
    o_ref[0] = jnp.dot(h.astype(bf16), w2_ref[...], preferred_element_type=f32).astype(o_ref.dtype)


def _nsa_compress(kv, pos, w1, w2):
    bn, g, s, dh = kv.shape
    nc = s // CMP_STRIDE
    chunks = kv.reshape(bn, g, nc, CMP_STRIDE * dh)
    nxt = jnp.pad(chunks[:, :, 1:], ((0, 0), (0, 0), (0, 1), (0, 0)))
    blocks = jnp.concatenate([chunks, nxt], axis=-1).reshape(bn * g, nc, CMP_BLOCK * dh)
    cin = CMP_BLOCK * dh
    hid = w1.shape[1]
    out = pl.pallas_call(
        _compress_kernel,
        grid=(bn * g,),
        in_specs=[pl.BlockSpec((1, nc, cin), lambda i: (i, 0, 0)),
                  pl.BlockSpec((1, cin), lambda i: (0, 0)),
                  pl.BlockSpec((cin, hid), lambda i: (0, 0)),
                  pl.BlockSpec((hid, dh), lambda i: (0, 0))],
        out_specs=pl.BlockSpec((1, nc, dh), lambda i: (i, 0, 0)),
        out_shape=jax.ShapeDtypeStruct((bn * g, nc, dh), bf16),
        compiler_params=_cparams(("parallel",)),
        name="nsa_compress",
    )(blocks, pos.reshape(1, cin), w1.astype(bf16), w2.astype(bf16))
    return out.reshape(bn, g, nc, dh)


V_ROWS = HEAD_DIM + 16


LOG2E = 1.4426950408889634
K_COLS = 2 * HEAD_DIM


def _masked_flash(i, qs_ref, key_tile, value_tile, mask_bias, delta_ref, m_ref, acc_ref, s_ref, mt_ref, al_ref, chunk,
                  shared_keys):
    width = qs_ref.shape[1]
    n_chunks = width // chunk
    m_ref[...] = jnp.full(m_ref.shape, NEG, f32)
    acc_ref[...] = jnp.zeros(acc_ref.shape, f32)

    def scores_chunk(j, delta, c, shared, slot=None):
        slot = j % 2 if slot is None else slot
        if c == 0 or not shared_keys:
            shared["kt"] = key_tile(j, c)
            shared["maskb"] = jnp.concatenate([mask_bias(j, c)] * (chunk // TQ), axis=1)
        cols = slice(c * chunk, (c + 1) * chunk)
        s = jnp.dot(shared["kt"], qs_ref[:, cols], preferred_element_type=f32) + shared["maskb"]
        if delta is not None:
            s = s + delta_ref[delta, :, cols]
        s_ref[slot, :, cols] = s
        m_old = m_ref[:, cols]
        m_new = jnp.maximum(m_old, jnp.max(s, axis=0, keepdims=True))
        m_ref[:, cols] = m_new
        mt_ref[slot, :, cols] = m_new
        al_ref[slot, :, cols] = jnp.exp2(m_old - m_new)

    def values_chunk(j, c, shared, slot=None):
        slot = j % 2 if slot is None else slot
        if c == 0 or not shared_keys:
            shared["vt"] = value_tile(j, c)
        cols = slice(c * chunk, (c + 1) * chunk)
        p = jnp.exp2(s_ref[slot, :, cols] - mt_ref[slot, :, cols])
        acc_ref[:, cols] = (al_ref[slot, :, cols] * acc_ref[:, cols]
                            + jnp.dot(shared["vt"], p.astype(bf16), preferred_element_type=f32))

    def scores(j, delta):
        shared = {}
        for c in range(n_chunks):
            scores_chunk(j, delta, c, shared)

    def values(j):
        shared = {}
        for c in range(n_chunks):
            values_chunk(j, c, shared)

    def values_then_scores(jv, js, delta, slot_v=None, slot_s=None):
        sv, ss = {}, {}
        for c in range(n_chunks):
            values_chunk(jv, c, sv, slot_v)
            scores_chunk(js, delta, c, ss, slot_s)

    @pl.when(i >= 2)
    def _():
        scores(0, None)

        def pair_body(k, carry):
            j = 2 * k + 1
            values_then_scores(j - 1, j, None, 0, 1)
            values_then_scores(j, j + 1, None, 1, 0)
            return carry

        lax.fori_loop(0, (i - 2) // 2, pair_body, 0)

        @pl.when((i - 2) % 2 == 1)
        def _():
            values_then_scores(i - 3, i - 2, None)

        values_then_scores(i - 2, i - 1, 0)
        values_then_scores(i - 1, i, 1)

    @pl.when(i == 1)
    def _():
        scores(0, 0)
        values_then_scores(0, 1, 1)

    @pl.when(i == 0)
    def _():
        scores(0, 1)

    values(i)


def _flash_operands(near, far):
    nh = near.shape[0]
    width = nh * TQ
    far2 = jnp.repeat(far * LOG2E, TQ).reshape(1, width)
    hi = far2.astype(bf16)
    lo = (far2 - hi.astype(f32)).astype(bf16)
    extra = jnp.concatenate([hi, lo, jnp.zeros((K_COLS - HEAD_DIM - 2, width), bf16)], axis=0)
    neart = near.reshape(nh, TQ, 2, TQ).transpose(2, 3, 0, 1).reshape(2, TQ, width) * LOG2E
    neart = neart - (hi.astype(f32) + lo.astype(f32))
    causal = np.where(np.arange(TQ)[:, None] <= np.arange(TQ)[None, :], 0.0, NEG).astype(np.float32)
    return extra, jnp.stack([neart[0], neart[1] + jnp.asarray(np.tile(causal, (1, nh)))], axis=0)


def _value_tiles(v):
    lead = v.shape[:-2]
    nt = v.shape[-2] // TQ
    vt = jnp.swapaxes(v.reshape(lead + (nt, TQ, HEAD_DIM)), -1, -2)
    return jnp.concatenate([vt, jnp.ones(lead + (nt, 1, TQ), vt.dtype),
                            jnp.zeros(lead + (nt, V_ROWS - HEAD_DIM - 1, TQ), vt.dtype)], axis=-2)


BANDED_TILES = 4


def _banded_kernel(qt_ref, k_ref, vt_ref, bias_ref, neg_ref, sink_ref, o_ref, *, rep, kt_tiles, use_sinks):
    kw = kt_tiles * TQ
    for u in range(qt_ref.shape[2]):
        i = pl.program_id(1) * qt_ref.shape[2] + u
        outs = []
        for g in range(qt_ref.shape[1]):
            qs = jnp.concatenate([(qt_ref[0, g, u].astype(f32) * (HEAD_DIM ** -0.5)).astype(bf16), neg_ref[...]],
                                 axis=0)
            k = k_ref[0, g, pl.ds(pl.multiple_of(i * TQ, TQ), kw), :]
            s = jnp.dot(k, qs, preferred_element_type=f32) + bias_ref[g]
            m = jnp.max(s, axis=0, keepdims=True)
            if use_sinks:
                m = jnp.maximum(m, sink_ref[g])
            p = jnp.exp(s - m).astype(bf16)
            vt = jnp.concatenate([vt_ref[0, g, i + t] for t in range(kt_tiles)], axis=1)
            acc = jnp.dot(vt, p, preferred_element_type=f32)
            l = acc[HEAD_DIM:HEAD_DIM + 1]
            if use_sinks:
                l = l + jnp.exp(sink_ref[g] - m)
            ot = acc[0:HEAD_DIM] / l
            outs += [ot[:, r * TQ:(r + 1) * TQ].T for r in range(rep)]
        o_ref[0, u * TQ:(u + 1) * TQ, :] = jnp.concatenate(outs, axis=1).astype(o_ref.dtype)


def _banded_gqa(qt, k, v, bias, sinks, window, out_dtype):
    bn, g, nt, _, width = qt.shape
    rep = width // TQ
    s = nt * TQ
    pad = -(-(window - 1) // TQ) * TQ
    kw = pad + TQ
    kd = 2 * HEAD_DIM
    flag = jnp.broadcast_to((jnp.arange(pad + s) < pad).astype(k.dtype)[:, None], (bn, g, pad + s, 1))
    kp = jnp.concatenate([jnp.pad(k, ((0, 0), (0, 0), (pad, 0), (0, 0))), flag,
                          jnp.zeros((bn, g, pad + s, kd - HEAD_DIM - 1), k.dtype)], axis=-1)
    vt = _value_tiles(jnp.pad(v, ((0, 0), (0, 0), (pad, 0), (0, 0))))
    neg = np.zeros((kd - HEAD_DIM, width), np.float32)
    neg[0] = NEG
    dist = np.arange(TQ)[:, None] + pad - np.arange(kw)[None, :]
    band = jnp.asarray((dist >= 0) & (dist < window))
    bias_t = jnp.where(band, bias, NEG).reshape(g, rep, TQ, kw).transpose(0, 3, 1, 2).reshape(g, kw, width)
    use_sinks = sinks is not None
    sink_rows = (jnp.repeat(sinks.astype(f32), TQ) if use_sinks else jnp.zeros((g * width,), f32)).reshape(g, 1, width)
    kern = functools.partial(_banded_kernel, rep=rep, kt_tiles=kw // TQ, use_sinks=use_sinks)
    rw = rep * HEAD_DIM
    return pl.pallas_call(
        kern,
        grid=(bn, nt // BANDED_TILES),
        in_specs=[pl.BlockSpec((1, g, BANDED_TILES, HEAD_DIM, width), lambda b, i: (b, 0, i, 0, 0)),
                  pl.BlockSpec((1, g, pad + s, kd), lambda b, i: (b, 0, 0, 0)),
                  pl.BlockSpec((1, g, (pad + s) // TQ, V_ROWS, TQ), lambda b, i: (b, 0, 0, 0, 0)),
                  pl.BlockSpec((g, kw, width), lambda b, i: (0, 0, 0)),
                  pl.BlockSpec((kd - HEAD_DIM, width), lambda b, i: (0, 0)),
                  pl.BlockSpec((g, 1, width), lambda b, i: (0, 0, 0))],
        out_specs=pl.BlockSpec((1, BANDED_TILES * TQ, g * rw), lambda b, i: (b, i, 0)),
        out_shape=jax.ShapeDtypeStruct((bn, s, g * rw), out_dtype),
        compiler_params=_cparams(("parallel", "arbitrary")),
        name="banded_gqa_w%d" % window,
    )(qt, kp, vt, bias_t, jnp.asarray(neg, bf16), sink_rows)


def _nsa_kernel(qt_ref, kc_ref, vct_ref, biasc_ref, ovt_ref, ks_ref, vst_ref, qx_ref, delta_ref, gt_ref, gn_ref,
                owin_ref, o_ref, qs_ref, selb_ref, m_ref, acc_ref, s_ref, mt_ref, al_ref, *, rep, n_sel):
    i = pl.program_id(1)
    t0 = i * TQ
    groups = qt_ref.shape[1]
    nc = kc_ref.shape[2]
    nblk = ovt_ref.shape[0]
    width = rep * TQ
    cmp_end = lax.broadcasted_iota(i32, (nc, TQ), 0) * CMP_STRIDE + (CMP_BLOCK - 1)
    valid_c = t0 + lax.broadcasted_iota(i32, (nc, TQ), 1) >= cmp_end
    blk = lax.broadcasted_iota(i32, (nblk, TQ), 0)
    tq = t0 + lax.broadcasted_iota(i32, (nblk, TQ), 1)
    cur = tq // SLC_BLOCK
    forced = (blk == 0) | (blk == cur) | (blk == cur - 1)
    admitted = blk * SLC_BLOCK <= tq

    o_cmps = []
    for g in range(groups):
        qf = qt_ref[0, g, 0].astype(f32)
        qs = (qf * (HEAD_DIM ** -0.5)).astype(bf16)
        qs_ref[0:HEAD_DIM, g * width:(g + 1) * width] = (qf * (HEAD_DIM ** -0.5 * LOG2E)).astype(bf16)

        s_c = jnp.dot(kc_ref[0, g], qs, preferred_element_type=f32)
        p_cols = []
        p_sum = jnp.zeros((nc, TQ), f32)
        for r in range(rep):
            sr = jnp.where(valid_c, s_c[:, r * TQ:(r + 1) * TQ] + biasc_ref[g * rep + r, 0], NEG)
            m = jnp.max(sr, axis=0, keepdims=True)
            e = jnp.where(valid_c, jnp.exp(sr - m), 0.0)
            l = jnp.sum(e, axis=0, keepdims=True)
            p = e * (1.0 / jnp.maximum(l, 1e-30))
            p_cols.append(p.astype(bf16))
            p_sum = p_sum + p
        o_cmps.append(jnp.dot(vct_ref[0, g], jnp.concatenate(p_cols, axis=1), preferred_element_type=f32))

        imp = jnp.dot(ovt_ref[...], p_sum, preferred_element_type=f32)
        score = jnp.where(forced, POS_BIG, jnp.where(admitted, imp, NEG))
        rank = jnp.zeros((nblk, TQ), i32)
        for kk in range(nblk):
            ck = score[kk:kk + 1, :]
            rank = rank + jnp.where(ck > score, 1, jnp.where(ck == score, jnp.where(blk > kk, 1, 0), 0))
        selb_ref[g] = jnp.where(rank < n_sel, 0.0, NEG)

    per_tile = TQ // SLC_BLOCK

    def mask_bias(j, c):
        rows = [jnp.broadcast_to(selb_ref[c, pl.ds(per_tile * j + t, 1), :], (SLC_BLOCK, TQ))
                for t in range(per_tile)]
        return jnp.concatenate(rows, axis=0)

    qs_ref[HEAD_DIM:K_COLS, :] = qx_ref[...]
    _masked_flash(i, qs_ref,
                  lambda j, c: ks_ref[0, c, pl.ds(pl.multiple_of(j * TQ, TQ), TQ), :],
                  lambda j, c: vst_ref[0, c, j],
                  mask_bias, delta_ref, m_ref, acc_ref, s_ref, mt_ref, al_ref, width, False)
    acc = acc_ref[...]
    o_slc = acc[0:HEAD_DIM] / acc[HEAD_DIM:HEAD_DIM + 1]

    outs = []
    for g in range(groups):
        sig_t = jax.nn.sigmoid(gt_ref[0, g, 0])
        sig_n = jax.nn.sigmoid(gn_ref[0, g])
        for r in range(rep):
            h = g * rep + r
            mixed = (sig_t[r:r + 1] * o_cmps[g][:, r * TQ:(r + 1) * TQ]
                     + sig_t[rep + r:rep + r + 1] * o_slc[:, h * TQ:(h + 1) * TQ])
            o_win = owin_ref[0, :, h * HEAD_DIM:(h + 1) * HEAD_DIM]
            outs.append(mixed.T + sig_n[:, 2 * rep + r:2 * rep + r + 1] * o_win)
    o_ref[0] = jnp.concatenate(outs, axis=1).astype(o_ref.dtype)


def _nsa_mix(qt, kc, vc, bias_c, ks, vs, near, far, gates, o_win):
    bn, g, nt, _, width = qt.shape
    s = nt * TQ
    nc = kc.shape[2]
    rep = width // TQ
    hq = g * rep * HEAD_DIM
    nblk = s // SLC_BLOCK
    n_sel = min(SLC_TOPN, nblk)
    ci = np.arange(nc)[:, None] * CMP_STRIDE
    sj = np.arange(nblk)[None, :] * SLC_BLOCK
    overlap = np.clip(np.minimum(ci + CMP_BLOCK, sj + SLC_BLOCK) - np.maximum(ci, sj), 0, None) / CMP_BLOCK
    overlap[nc - 1] = 0.0
    gates_t = gates.reshape(bn, g, nt, TQ, 3 * rep).transpose(0, 1, 2, 4, 3)
    kern = functools.partial(_nsa_kernel, rep=rep, n_sel=n_sel)
    return pl.pallas_call(
        kern,
        grid=(bn, nt),
        in_specs=[pl.BlockSpec((1, g, 1, HEAD_DIM, width), lambda b, i: (b, 0, i, 0, 0)),
                  pl.BlockSpec((1, g, nc, HEAD_DIM), lambda b, i: (b, 0, 0, 0)),
                  pl.BlockSpec((1, g, HEAD_DIM, nc), lambda b, i: (b, 0, 0, 0)),
                  pl.BlockSpec((g * rep, 1, nc, TQ), lambda b, i: (0, i, 0, 0)),
                  pl.BlockSpec((nblk, nc), lambda b, i: (0, 0)),
                  pl.BlockSpec((1, g, s, K_COLS), lambda b, i: (b, 0, 0, 0)),
                  pl.BlockSpec((1, g, nt, V_ROWS, TQ), lambda b, i: (b, 0, 0, 0, 0)),
                  pl.BlockSpec((K_COLS - HEAD_DIM, g * width), lambda b, i: (0, 0)),
                  pl.BlockSpec((2, TQ, g * width), lambda b, i: (0, 0, 0)),
                  pl.BlockSpec((1, g, 1, 3 * rep, TQ), lambda b, i: (b, 0, i, 0, 0)),
                  pl.BlockSpec((1, g, TQ, 3 * rep), lambda b, i: (b, 0, i, 0)),
                  pl.BlockSpec((1, TQ, hq), lambda b, i: (b, i, 0))],
        out_specs=pl.BlockSpec((1, TQ, hq), lambda b, i: (b, i, 0)),
        out_shape=jax.ShapeDtypeStruct((bn, s, hq), bf16),
        scratch_shapes=[pltpu.VMEM((K_COLS, g * width), bf16), pltpu.VMEM((g, nblk, TQ), f32),
                        pltpu.VMEM((1, g * width), f32), pltpu.VMEM((V_ROWS, g * width), f32),
                        pltpu.VMEM((2, TQ, g * width), f32), pltpu.VMEM((2, 1, g * width), f32),
                        pltpu.VMEM((2, 1, g * width), f32)],
        compiler_params=_cparams(("parallel", "arbitrary")),
        name="nsa_mix",
    )(qt, kc, jnp.swapaxes(vc, 2, 3), bias_c, jnp.asarray(overlap.T, f32), ks, vs,
      *_flash_operands(near, far), gates_t, gates, o_win)


RADIX_BITS_PER_CHECK = 8


def _dsa_kernel(qt_ref, k_ref, vt_ref, qit_ref, ki_ref, wit_ref, qx_ref, delta_ref, tri_ref, o_ref,
                keys_ref, qs_ref, m_ref, acc_ref, s_ref, mt_ref, al_ref, *, topk, chunk):
    i = pl.program_id(1)
    t0 = i * TQ
    width = qt_ref.shape[3]
    nh = width // TQ
    nih = qit_ref.shape[3] // TQ
    krow = lax.broadcasted_iota(i32, (TQ, TQ), 0)
    qcol = lax.broadcasted_iota(i32, (TQ, TQ), 1)
    n_pairs = jnp.maximum(i // 2 + 1, -(-topk // (2 * TQ)))

    qit = qit_ref[0, 0]
    wit = wit_ref[0, 0]

    def score_tile(j):
        kt = ki_ref[0, pl.ds(pl.multiple_of(j * TQ, TQ), TQ), :]
        rel = jnp.maximum(jnp.dot(kt, qit, preferred_element_type=f32), 0.0)
        sc = jnp.zeros((TQ, TQ), f32)
        for h in range(nih):
            sc = sc + wit[h:h + 1, :] * rel[:, h * TQ:(h + 1) * TQ]
        sc = jnp.where(j * TQ + krow <= t0 + qcol, sc, NEG)
        keys_ref[j] = jnp.where(sc == 0.0, 0.0, sc)

    def score_body(jj, carry):
        for u in range(4):
            score_tile(4 * jj + u)
        return carry

    lax.fori_loop(0, (n_pairs + 1) // 2, score_body, 0)

    def code_to_float(code):
        code = jnp.clip(code, -0x7F800001, 0x7F800000)
        bits = jnp.where(code < 0, code ^ 0x7FFFFFFF, code)
        bits = jnp.where((bits > 0) & (bits < 0x00800000), 0x00800000, bits)
        return pltpu.bitcast(bits, f32)

    def count(pred_fn):
        def body(jj, cnt):
            return (cnt + jnp.where(pred_fn(keys_ref[2 * jj]), 1.0, 0.0)
                    + jnp.where(pred_fn(keys_ref[2 * jj + 1]), 1.0, 0.0))
        cnt = lax.fori_loop(0, n_pairs, body, jnp.zeros((TQ, TQ), f32))
        return jnp.sum(cnt, axis=0, keepdims=True)

    def bits_body(state):
        t, prefix, n_ge = state
        for u in range(RADIX_BITS_PER_CHECK):
            cand = prefix ^ jnp.left_shift(jnp.int32(1), 31 - (t + u))
            cand_f = code_to_float(cand)
            cnt = count(lambda sj: sj >= cand_f)
            prefix = jnp.where(cnt >= topk, cand, prefix)
            n_ge = jnp.where(cnt >= topk, cnt, n_ge)
        return t + RADIX_BITS_PER_CHECK, prefix, n_ge

    n_all = (2 * n_pairs * TQ).astype(f32)
    _, code, n_ge = lax.while_loop(lambda st: (st[0] < 32) & (jnp.max(st[2]) > topk), bits_body,
                                   (jnp.int32(0), jnp.full((1, TQ), INT_MIN, i32), jnp.full((1, TQ), n_all, f32)))
    thr = code_to_float(code)

    @pl.when(jnp.max(n_ge) > topk)
    def _():
        need = topk - count(lambda sj: sj > thr)

        def fix_body(j, running):
            sj = keys_ref[j]
            tie = jnp.where(sj == thr, 1.0, 0.0)
            before = jnp.dot(tri_ref[...], tie.astype(bf16), preferred_element_type=f32) + running
            keys_ref[j] = jnp.where(before >= need, jnp.where(sj == thr, -jnp.inf, sj), sj)
            return running + jnp.sum(tie, axis=0, keepdims=True)

        lax.fori_loop(0, 2 * n_pairs, fix_body, jnp.zeros((1, TQ), f32))

    qs_ref[0:HEAD_DIM, :] = (qt_ref[0, 0].astype(f32) * (HEAD_DIM ** -0.5 * LOG2E)).astype(bf16)
    qs_ref[HEAD_DIM:K_COLS, :] = qx_ref[...]
    _masked_flash(i, qs_ref,
                  lambda j, c: k_ref[0, pl.ds(pl.multiple_of(j * TQ, TQ), TQ), :],
                  lambda j, c: vt_ref[0, j],
                  lambda j, c: jnp.where(keys_ref[j] >= thr, 0.0, NEG),
                  delta_ref, m_ref, acc_ref, s_ref, mt_ref, al_ref, chunk, True)
    acc = acc_ref[...]
    ot = acc[0:HEAD_DIM] / acc[HEAD_DIM:HEAD_DIM + 1]
    o_ref[0] = jnp.concatenate([ot[:, h * TQ:(h + 1) * TQ].T for h in range(nh)], axis=1).astype(o_ref.dtype)


def _dsa_attention(qt, k, v, qit, ki, wi, near, far):
    bn, nt, _, width = qt.shape
    s = nt * TQ
    nh = width // TQ
    hq = nh * HEAD_DIM
    nih = qit.shape[3] // TQ
    topk = min(C_TOPK_MAX, s // 4)
    wit = wi.reshape(bn, nt, TQ, nih).transpose(0, 1, 3, 2)
    tri = np.tril(np.ones((TQ, TQ), np.float32), -1)
    kern = functools.partial(_dsa_kernel, topk=topk, chunk=512)
    nt_scr = max(nt, -(-topk // TQ))
    return pl.pallas_call(
        kern,
        grid=(bn, nt),
        in_specs=[pl.BlockSpec((1, 1, HEAD_DIM, width), lambda b, i: (b, i, 0, 0)),
                  pl.BlockSpec((1, s, K_COLS), lambda b, i: (b, 0, 0)),
                  pl.BlockSpec((1, nt, V_ROWS, TQ), lambda b, i: (b, 0, 0, 0)),
                  pl.BlockSpec((1, 1, IDX_DIM, nih * TQ), lambda b, i: (b, i, 0, 0)),
                  pl.BlockSpec((1, s, IDX_DIM), lambda b, i: (b, 0, 0)),
                  pl.BlockSpec((1, 1, nih, TQ), lambda b, i: (b, i, 0, 0)),
                  pl.BlockSpec((K_COLS - HEAD_DIM, width), lambda b, i: (0, 0)),
                  pl.BlockSpec((2, TQ, width), lambda b, i: (0, 0, 0)),
                  pl.BlockSpec((TQ, TQ), lambda b, i: (0, 0))],
        out_specs=pl.BlockSpec((1, TQ, hq), lambda b, i: (b, i, 0)),
        out_shape=jax.ShapeDtypeStruct((bn, s, hq), bf16),
        scratch_shapes=[pltpu.VMEM((nt_scr, TQ, TQ), f32), pltpu.VMEM((K_COLS, width), bf16),
                        pltpu.VMEM((1, width), f32), pltpu.VMEM((V_ROWS, width), f32),
                        pltpu.VMEM((2, TQ, width), f32), pltpu.VMEM((2, 1, width), f32),
                        pltpu.VMEM((2, 1, width), f32)],
        compiler_params=_cparams(("parallel", "arbitrary")),
        name="dsa_attention",
    )(qt, k, v, qit, ki, wit, *_flash_operands(near, far), jnp.asarray(tri, bf16))


def _heads_first(a, g):
    bn, s, _ = a.shape
    return a.reshape(bn, s, g, HEAD_DIM).transpose(0, 2, 1, 3)


def _even_layer(x, table_t, w_in, sinks, cmpk_pos, cmpk_w1, cmpk_w2, cmpv_pos, cmpv_w1, cmpv_w2,
                w_out, ln1_g, ln1_b, ffn_gate, ffn_up, ffn_down, ln2_g, ln2_b):
    bn, s, d = x.shape
    xf = x.reshape(bn * s, d)
    a_q, a_kv, b_q, b_kv = A_HEADS * HEAD_DIM, A_KV * HEAD_DIM, B_HEADS * HEAD_DIM, B_KV * HEAD_DIM
    n_main = a_q + 2 * a_kv + b_q + 6 * b_kv
    rb = B_HEADS // B_KV
    w_tail = jnp.pad(w_in[:, n_main:], ((0, 0), (0, 128 - 3 * B_HEADS))).astype(bf16)
    cols = np.cumsum([0, a_q, a_kv, a_kv, b_q] + [b_kv] * 6)
    w_kv = jnp.concatenate([w_in[:, cols[1]:cols[3]], w_in[:, cols[4]:cols[6]], w_in[:, cols[8]:cols[10]]],
                           axis=1).astype(bf16)
    main, tail, qt_a, qt_b, ks1, vst = _proj_in(
        x, [w_kv, w_tail, w_in[:, cols[0]:cols[1]].T.astype(bf16), w_in[:, cols[3]:cols[4]].T.astype(bf16),
            w_in[:, cols[6]:cols[7]].astype(bf16), w_in[:, cols[7]:cols[8]].T.astype(bf16)],
        [("plain", w_kv.shape[1], bf16, 0), ("plain", 128, f32, 0),
         ("queries", A_KV, A_HEADS // A_KV, HEAD_DIM), ("queries", B_KV, rb, HEAD_DIM),
         ("keys", B_KV, 1, HEAD_DIM), ("values", B_KV, 1, HEAD_DIM)])
    main = main.reshape(bn, s, w_kv.shape[1])
    gates = tail[:, :3 * B_HEADS]
    gates = gates.reshape(bn, s, 3, B_KV, rb).transpose(0, 3, 1, 2, 4).reshape(bn, B_KV, s, 3 * rb)
    kcols = np.cumsum([0, a_kv, a_kv] + [b_kv] * 4)
    ka, va, kc_in, vc_in, kwn, vwn = [main[:, :, kcols[t]:kcols[t + 1]] for t in range(6)]

    near_idx = _bucket_of(np.arange(TQ)[:, None] + TQ - np.arange(2 * TQ)[None, :])
    near = _bias_expand(table_t, near_idx, TQ)
    pad_b = -(-(B_WINDOW - 1) // TQ) * TQ
    win_idx = _bucket_of(np.arange(TQ)[:, None] + pad_b - np.arange(pad_b + TQ)[None, :])
    bias_win = _bias_expand(table_t[A_HEADS:A_HEADS + B_HEADS], win_idx, TQ)
    nc = s // CMP_STRIDE
    nt = s // TQ
    cmp_idx = _bucket_of((np.arange(nt)[:, None, None] * TQ + np.arange(TQ)[None, None, :])
                         - (np.arange(nc)[None, :, None] * CMP_STRIDE + CMP_BLOCK - 1)).reshape(nt * nc, TQ)
    bias_c = _bias_expand(table_t[A_HEADS:A_HEADS + B_HEADS], cmp_idx, min(nt * nc, 512))
    bias_c = bias_c.reshape(B_HEADS, nt, nc, TQ)
    far = table_t[:, NUM_BUCKETS - 1]

    out_a =_banded_gqa(qt_a, _heads_first(ka, A_KV), _heads_first(va, A_KV), near[:A_HEADS], sinks, A_WINDOW, bf16)
    o_win = _banded_gqa(qt_b, _heads_first(kwn, B_KV), _heads_first(vwn, B_KV), bias_win, None, B_WINDOW, f32)
    kc = _nsa_compress(_heads_first(kc_in, B_KV), cmpk_pos, cmpk_w1, cmpk_w2)
    vc = _nsa_compress(_heads_first(vc_in, B_KV), cmpv_pos, cmpv_w1, cmpv_w2)
    out_b = _nsa_mix(qt_b, kc, vc, bias_c, ks1, vst,
                     near[A_HEADS:A_HEADS + B_HEADS], far[A_HEADS:A_HEADS + B_HEADS], gates, o_win)
    mix = jnp.concatenate([out_a, out_b], axis=-1).reshape(bn * s, a_q + b_q)
    x2 = _proj_ffn_res_ln(mix, w_out.astype(bf16), xf, ln1_g, ln1_b, ffn_gate.astype(bf16), ffn_up.astype(bf16),
                          ffn_down.astype(bf16), ln2_g, ln2_b)
    return x2.reshape(bn, s, d), near, far


def _odd_layer(x, near, far, w_in, w_out, ln1_g, ln1_b, router, exp_gate, exp_up, exp_down, ln2_g, ln2_b):
    bn, s, d = x.shape
    xf = x.reshape(bn * s, d)
    c_q = C_HEADS * HEAD_DIM
    n_main = c_q + 2 * HEAD_DIM + IDX_HEADS * IDX_DIM
    w_tail = jnp.pad(w_in[:, n_main:], ((0, 0), (0, 128 - IDX_DIM - IDX_HEADS))).astype(bf16)
    tail, qt, qit, k1, vt = _proj_in(
        x, [w_tail, w_in[:, :c_q].T.astype(bf16), w_in[:, c_q + 2 * HEAD_DIM:n_main].T.astype(bf16),
            w_in[:, c_q:c_q + HEAD_DIM].astype(bf16), w_in[:, c_q + HEAD_DIM:c_q + 2 * HEAD_DIM].T.astype(bf16)],
        [("plain", 128, f32, 0), ("queries", 1, C_HEADS, HEAD_DIM), ("queries", 1, IDX_HEADS, IDX_DIM),
         ("keys", 1, 1, HEAD_DIM), ("values", 1, 1, HEAD_DIM)])
    tail = tail.reshape(bn, s, 128)
    ki = tail[:, :, :IDX_DIM].astype(bf16)
    wi = tail[:, :, IDX_DIM:IDX_DIM + IDX_HEADS]
    mix = _dsa_attention(qt[:, 0], k1[:, 0], vt[:, 0], qit[:, 0], ki, wi, near[:C_HEADS], far[:C_HEADS])
    mix = mix.reshape(bn * s, c_q)
    x1 = _proj_res_ln(mix, w_out.astype(bf16), xf, ln1_g, ln1_b)
    x2 = _moe_res_ln(x1, router, exp_gate.astype(bf16), exp_up.astype(bf16), exp_down.astype(bf16), ln2_g, ln2_b)
    return x2.reshape(bn, s, d)


def kernel(x, rel_bias, l0_w_in, l0_sinks, l0_cmpk_pos, l0_cmpk_w1, l0_cmpk_w2, l0_cmpv_pos, l0_cmpv_w1,
           l0_cmpv_w2, l0_w_out, l0_ln1_g, l0_ln1_b, l0_ffn_gate, l0_ffn_up, l0_ffn_down, l0_ln2_g, l0_ln2_b,
           l1_w_in, l1_w_out, l1_ln1_g, l1_ln1_b, l1_router, l1_exp_gate, l1_exp_up, l1_exp_down, l1_ln2_g,
           l1_ln2_b):
    table_t = rel_bias.T.astype(f32)
    x, near, far = _even_layer(x, table_t, l0_w_in, l0_sinks, l0_cmpk_pos, l0_cmpk_w1, l0_cmpk_w2, l0_cmpv_pos,
                               l0_cmpv_w1, l0_cmpv_w2, l0_w_out, l0_ln1_g, l0_ln1_b, l0_ffn_gate, l0_ffn_up,
                               l0_ffn_down, l0_ln2_g, l0_ln2_b)
    return _odd_layer(x, near, far, l1_w_in, l1_w_out, l1_ln1_g, l1_ln1_b, l1_router, l1_exp_gate, l1_exp_up,
                      l1_exp_down, l1_ln2_g, l1_ln2_b)
```

```python
import functools
import math

import numpy as np
import jax
import jax.numpy as jnp
from jax import lax
from jax.experimental import pallas as pl
from jax.experimental.pallas import tpu as pltpu

f32 = jnp.float32
bf16 = jnp.bfloat16
i32 = jnp.int32

D_MODEL = 1024
HEAD_DIM = 64
NUM_BUCKETS = 32
MAX_DISTANCE = 128
A_HEADS, A_KV, A_WINDOW = 8, 2, 128
B_HEADS, B_KV, B_WINDOW = 8, 2, 512
CMP_BLOCK, CMP_STRIDE = 32, 16
SLC_BLOCK, SLC_TOPN = 64, 16
C_HEADS, IDX_HEADS, IDX_DIM, C_TOPK_MAX = 16, 8, 32, 256
D_FF, N_EXPERTS = 2816, 8
DEPTH = 2
ALPHA = (2.0 * DEPTH) ** 0.25
LN_EPS = 1e-5
NEG = -1e30
POS_BIG = 1e30
INT_MIN = -(2 ** 31)

TQ = 128
FF_CHUNK = 256
VMEM_LIMIT = 56 * 1024 * 1024


def _cparams(sem):
    return pltpu.CompilerParams(dimension_semantics=sem, vmem_limit_bytes=VMEM_LIMIT)


def _bucket_table(max_dist):
    n = np.arange(max_dist + 1)
    max_exact = NUM_BUCKETS // 2
    nf = np.maximum(n, 1).astype(np.float64)
    large = max_exact + (np.log(nf / max_exact) / math.log(MAX_DISTANCE / max_exact)
                         * (NUM_BUCKETS - max_exact)).astype(np.int64)
    large = np.minimum(large, NUM_BUCKETS - 1)
    return np.where(n < max_exact, n, large).astype(np.int32)


def _bucket_of(dist):
    dist = np.maximum(dist, 0)
    return _bucket_table(int(dist.max()))[dist]


def _proj_in_kernel(*refs, groups):
    ng = len(groups)
    x_ref = refs[0]
    wq_refs = refs[1:1 + ng]
    oq_refs = refs[1 + ng:]
    xb = x_ref[...].astype(bf16)
    tm = xb.shape[0]
    for wq_ref, oq_ref, (kind, gb, hb, dh) in zip(wq_refs, oq_refs, groups):
        if kind == "plain":
            oq_ref[...] = jnp.dot(xb, wq_ref[...], preferred_element_type=f32).astype(oq_ref.dtype)
            continue
        if kind == "keys":
            kk = jnp.dot(xb, wq_ref[...], preferred_element_type=f32).astype(bf16)
            ones = jnp.where(lax.broadcasted_iota(i32, (tm, K_COLS - dh), 1) < 2, 1.0, 0.0).astype(bf16)
            for g in range(gb):
                oq_ref[0, g, :, 0:dh] = kk[:, g * dh:(g + 1) * dh]
                oq_ref[0, g, :, dh:K_COLS] = ones
            continue
        qt = lax.dot_general(wq_ref[...], xb, (((1,), (1,)), ((), ())), preferred_element_type=f32).astype(bf16)
        ones = jnp.where(lax.broadcasted_iota(i32, (V_ROWS - HEAD_DIM, TQ), 0) < 1, 1.0, 0.0).astype(bf16)
        for g in range(gb):
            for h in range(hb):
                rows = slice((g * hb + h) * dh, (g * hb + h + 1) * dh)
                for t in range(tm // TQ):
                    oq_ref[0, g, t, 0:dh, h * TQ:(h + 1) * TQ] = qt[rows, t * TQ:(t + 1) * TQ]
                    if kind == "values":
                        oq_ref[0, g, t, dh:V_ROWS, :] = ones


def _proj_in(x3, w_groups, groups, tm=512):
    bn, s, k = x3.shape
    m = bn * s
    tm = min(tm, s)
    per_b = s // tm
    tiles = tm // TQ
    q_specs, q_shapes = [], []
    for kind, gb, hb, dh in groups:
        if kind == "plain":
            q_specs.append(pl.BlockSpec((tm, gb), lambda i: (i, 0)))
            q_shapes.append(jax.ShapeDtypeStruct((m, gb), hb))
        elif kind == "keys":
            q_specs.append(pl.BlockSpec((1, gb, tm, K_COLS), lambda i: (i // per_b, 0, i % per_b, 0)))
            q_shapes.append(jax.ShapeDtypeStruct((bn, gb, s, K_COLS), bf16))
        else:
            rows = V_ROWS if kind == "values" else dh
            q_specs.append(pl.BlockSpec((1, gb, tiles, rows, hb * TQ), lambda i: (i // per_b, 0, i % per_b, 0, 0)))
            q_shapes.append(jax.ShapeDtypeStruct((bn, gb, s // TQ, rows, hb * TQ), bf16))
    return pl.pallas_call(
        functools.partial(_proj_in_kernel, groups=tuple(groups)),
        grid=(m // tm,),
        in_specs=[pl.BlockSpec((tm, k), lambda i: (i, 0))]
                 + [pl.BlockSpec(w.shape, lambda i: (0, 0)) for w in w_groups],
        out_specs=q_specs,
        out_shape=q_shapes,
        compiler_params=_cparams(("parallel",)),
        name="proj_in",
    )(x3.reshape(m, k), *w_groups)


def _bias_expand_kernel(tab_ref, idx_ref, o_ref):
    h = pl.program_id(0)
    idx = idx_ref[...]
    acc = jnp.zeros(idx.shape, f32)
    for b in range(NUM_BUCKETS):
        acc = jnp.where(idx == b, tab_ref[h, b], acc)
    o_ref[0] = acc


def _bias_expand(table_t, idx, tr):
    nh = table_t.shape[0]
    r, c = idx.shape
    return pl.pallas_call(
        _bias_expand_kernel,
        grid=(nh, r // tr),
        in_specs=[pl.BlockSpec(memory_space=pltpu.SMEM),
                  pl.BlockSpec((tr, c), lambda h, i: (i, 0))],
        out_specs=pl.BlockSpec((1, tr, c), lambda h, i: (h, i, 0)),
        out_shape=jax.ShapeDtypeStruct((nh, r, c), f32),
        compiler_params=_cparams(("parallel", "arbitrary")),
        name="bias_expand",
    )(table_t, jnp.asarray(idx, dtype=i32))


def _layer_norm(z, g, b):
    mu = jnp.mean(z, axis=-1, keepdims=True)
    zc = z - mu
    var = jnp.mean(zc * zc, axis=-1, keepdims=True)
    return zc * lax.rsqrt(var + LN_EPS) * g + b


def _proj_ln_kernel(mix_ref, w_ref, x_ref, g_ref, b_ref, o_ref):
    y = jnp.dot(mix_ref[...], w_ref[...], preferred_element_type=f32)
    o_ref[...] = _layer_norm(ALPHA * x_ref[...] + y, g_ref[...], b_ref[...])


def _proj_res_ln(mix, w, x, g, b, tm=512):
    m, k = mix.shape
    d = w.shape[1]
    return pl.pallas_call(
        _proj_ln_kernel,
        grid=(m // tm,),
        in_specs=[pl.BlockSpec((tm, k), lambda i: (i, 0)),
                  pl.BlockSpec((k, d), lambda i: (0, 0)),
                  pl.BlockSpec((tm, d), lambda i: (i, 0)),
                  pl.BlockSpec((1, d), lambda i: (0, 0)),
                  pl.BlockSpec((1, d), lambda i: (0, 0))],
        out_specs=pl.BlockSpec((tm, d), lambda i: (i, 0)),
        out_shape=jax.ShapeDtypeStruct((m, d), f32),
        compiler_params=_cparams(("parallel",)),
        name="proj_res_ln",
    )(mix, w, x, g.reshape(1, d), b.reshape(1, d))


def _swiglu_accumulate(xb, wg_ref, wu_ref, wd_ref, acc_ref, widx):
    d_ff = wg_ref.shape[-1]
    for c in range(d_ff // FF_CHUNK):
        cols = slice(c * FF_CHUNK, (c + 1) * FF_CHUNK)
        gate = jnp.dot(xb, wg_ref[widx + (slice(None), cols)], preferred_element_type=f32)
        up = jnp.dot(xb, wu_ref[widx + (slice(None), cols)], preferred_element_type=f32)
        h = jax.nn.silu(gate) * up
        acc_ref[...] += jnp.dot(h.astype(bf16), wd_ref[widx + (cols, slice(None))], preferred_element_type=f32)


def _proj_ffn_kernel(mix_ref, wo_ref, x_ref, g1_ref, b1_ref, wg_ref, wu_ref, wd_ref, g2_ref, b2_ref, o_ref, acc_ref):
    y = jnp.dot(mix_ref[...], wo_ref[...], preferred_element_type=f32)
    o_ref[...] = _layer_norm(ALPHA * x_ref[...] + y, g1_ref[...], b1_ref[...])
    acc_ref[...] = jnp.zeros_like(acc_ref)
    _swiglu_accumulate(o_ref[...].astype(bf16), wg_ref, wu_ref, wd_ref, acc_ref, ())
    o_ref[...] = _layer_norm(ALPHA * o_ref[...] + acc_ref[...], g2_ref[...], b2_ref[...])


def _proj_ffn_res_ln(mix, w_out, x, g1, b1, wg, wu, wd, g2, b2, tm=512):
    m, k = mix.shape
    d = x.shape[1]
    d_ff = wg.shape[1]
    single = pl.Buffered(1)
    row = lambda i: (i, 0)
    fixed = lambda i: (0, 0)
    return pl.pallas_call(
        _proj_ffn_kernel,
        grid=(m // tm,),
        in_specs=[pl.BlockSpec((tm, k), row),
                  pl.BlockSpec((k, d), fixed, pipeline_mode=single),
                  pl.BlockSpec((tm, d), row),
                  pl.BlockSpec((1, d), fixed),
                  pl.BlockSpec((1, d), fixed),
                  pl.BlockSpec((d, d_ff), fixed, pipeline_mode=single),
                  pl.BlockSpec((d, d_ff), fixed, pipeline_mode=single),
                  pl.BlockSpec((d_ff, d), fixed, pipeline_mode=single),
                  pl.BlockSpec((1, d), fixed),
                  pl.BlockSpec((1, d), fixed)],
        out_specs=pl.BlockSpec((tm, d), row),
        out_shape=jax.ShapeDtypeStruct((m, d), f32),
        scratch_shapes=[pltpu.VMEM((tm, d), f32)],
        compiler_params=_cparams(("parallel",)),
        name="proj_ffn_res_ln",
    )(mix, w_out, x, g1.reshape(1, d), b1.reshape(1, d), wg, wu, wd, g2.reshape(1, d), b2.reshape(1, d))


MOE_CAP = 288


def _router_gates(x, r_ref, lane):
    logits = jnp.dot(x, r_ref[...], preferred_element_type=f32)
    logits = jnp.where(lane < N_EXPERTS, logits, -jnp.inf)
    m1 = jnp.max(logits, axis=-1, keepdims=True)
    i1 = jnp.min(jnp.where(logits == m1, lane, 128), axis=-1, keepdims=True)
    rest = jnp.where(lane == i1, -jnp.inf, logits)
    m2 = jnp.max(rest, axis=-1, keepdims=True)
    i2 = jnp.min(jnp.where(rest == m2, lane, 128), axis=-1, keepdims=True)
    e2 = jnp.exp(m2 - m1)
    g1 = 1.0 / (1.0 + e2)
    return jnp.where(lane == i1, g1, 0.0) + jnp.where(lane == i2, e2 * g1, 0.0)


def _moe_kernel(x_ref, r_ref, tri_ref, wg_ref, wu_ref, wd_ref, g_ref, b_ref, o_ref,
                split_ref, post_ref, yc_ref):
    e = pl.program_id(1)
    tm = x_ref.shape[0]

    @pl.when(e == 0)
    def _():
        comb = _router_gates(x_ref[...], r_ref, lax.broadcasted_iota(i32, (tm, 128), 1))
        hi = comb.astype(bf16)
        rest = comb - hi.astype(f32)
        mid = rest.astype(bf16)
        split_ref[0] = hi
        split_ref[1] = mid
        split_ref[2] = (rest - mid.astype(f32)).astype(bf16)
        chosen = jnp.where(comb > 0.0, 1.0, 0.0)
        pos = jnp.dot(tri_ref[...], chosen.astype(bf16), preferred_element_type=f32)
        pos = jnp.where(comb > 0.0, pos, -1.0)
        for t in range(tm // 128):
            rows = slice(t * 128, (t + 1) * 128)
            post_ref[:, rows] = pos[rows].T[0:N_EXPERTS]
        hi32 = jnp.floor((pos + 1.0) * (1.0 / 32.0))
        split_ref[3] = hi32.astype(bf16)
        split_ref[4] = (pos + 1.0 - 32.0 * hi32).astype(bf16)
        o_ref[...] = jnp.zeros_like(o_ref)

    pos_row = post_ref[pl.ds(e, 1), :].astype(i32)
    n_tok = jnp.sum(jnp.where(pos_row >= 0, 1.0, 0.0)).astype(i32)
    pick = jnp.where(lax.broadcasted_iota(i32, (128, 128), 0) == e, 1.0, 0.0).astype(bf16)

    def column(t):
        return jnp.dot(split_ref[t], pick, preferred_element_type=f32)

    gate_b = column(0) + column(1) + column(2)
    pos_b = (32.0 * column(3) + column(4) - 1.0).astype(i32)
    lane = lax.broadcasted_iota(i32, (tm, 128), 1)

    def pass_body(p, carry):
        base = p * MOE_CAP
        gather = jnp.where(pos_row == base + lax.broadcasted_iota(i32, (MOE_CAP, tm), 0), 1.0, 0.0).astype(bf16)
        xc = jnp.dot(gather, x_ref[...].astype(bf16), preferred_element_type=f32).astype(bf16)
        yc_ref[...] = jnp.zeros_like(yc_ref)
        _swiglu_accumulate(xc, wg_ref, wu_ref, wd_ref, yc_ref, (0,))
        scatter = jnp.concatenate([jnp.where(pos_b == base + t * 128 + lane, 1.0, 0.0).astype(bf16)
                                   for t in range(-(-MOE_CAP // 128))], axis=1)[:, :MOE_CAP]
        y = jnp.dot(scatter, yc_ref[...].astype(bf16), preferred_element_type=f32)
        o_ref[...] += jnp.concatenate([gate_b] * (y.shape[1] // 128), axis=1) * y
        return carry

    lax.fori_loop(0, (n_tok + MOE_CAP - 1) // MOE_CAP, pass_body, 0)

    @pl.when(e == N_EXPERTS - 1)
    def _():
        o_ref[...] = _layer_norm(ALPHA * x_ref[...] + o_ref[...], g_ref[...], b_ref[...])


def _moe_res_ln(x, router, wg, wu, wd, g, b, tm=1024):
    m, d = x.shape
    ne, _, d_ff = wg.shape
    tm = min(tm, m)
    router_p = jnp.pad(router, ((0, 0), (0, 128 - ne)))
    tri = jnp.asarray(np.tril(np.ones((tm, tm), np.float32), -1), bf16)
    single = pl.Buffered(1)
    return pl.pallas_call(
        _moe_kernel,
        grid=(m // tm, ne),
        in_specs=[pl.BlockSpec((tm, d), lambda i, e: (i, 0), pipeline_mode=single),
                  pl.BlockSpec((d, 128), lambda i, e: (0, 0), pipeline_mode=single),
                  pl.BlockSpec((tm, tm), lambda i, e: (0, 0), pipeline_mode=single),
                  pl.BlockSpec((1, d, d_ff), lambda i, e: (e, 0, 0)),
                  pl.BlockSpec((1, d, d_ff), lambda i, e: (e, 0, 0)),
                  pl.BlockSpec((1, d_ff, d), lambda i, e: (e, 0, 0)),
                  pl.BlockSpec((1, d), lambda i, e: (0, 0)),
                  pl.BlockSpec((1, d), lambda i, e: (0, 0))],
        out_specs=pl.BlockSpec((tm, d), lambda i, e: (i, 0)),
        out_shape=jax.ShapeDtypeStruct((m, d), f32),
        scratch_shapes=[pltpu.VMEM((5, tm, 128), bf16), pltpu.VMEM((N_EXPERTS, tm), f32),
                        pltpu.VMEM((MOE_CAP, d), f32)],
        compiler_params=_cparams(("parallel", "arbitrary")),
        name="moe_res_ln",
    )(x, router_p, tri, wg, wu, wd, g.reshape(1, d), b.reshape(1, d))


def _compress_kernel(blk_ref, pos_ref, w1_ref, w2_ref, o_ref):
    xb = (blk_ref[0].astype(f32) + pos_ref[...]).astype(bf16)
    h = jax.nn.gelu(jnp.dot(xb, w1_ref[...], preferred_element_type=f32))
    o_ref[0] = jnp.dot(h.astype(bf16), w2_ref[...], preferred_element_type=f32).astype(o_ref.dtype)


def _nsa_compress(kv, pos, w1, w2):
    bn, g, s, dh = kv.shape
    nc = s // CMP_STRIDE
    chunks = kv.reshape(bn, g, nc, CMP_STRIDE * dh)
    nxt = jnp.pad(chunks[:, :, 1:], ((0, 0), (0, 0), (0, 1), (0, 0)))
    blocks = jnp.concatenate([chunks, nxt], axis=-1).reshape(bn * g, nc, CMP_BLOCK * dh)
    cin = CMP_BLOCK * dh
    hid = w1.shape[1]
    out = pl.pallas_call(
        _compress_kernel,
        grid=(bn * g,),
        in_specs=[pl.BlockSpec((1, nc, cin), lambda i: (i, 0, 0)),
                  pl.BlockSpec((1, cin), lambda i: (0, 0)),
                  pl.BlockSpec((cin, hid), lambda i: (0, 0)),
                  pl.BlockSpec((hid, dh), lambda i: (0, 0))],
        out_specs=pl.BlockSpec((1, nc, dh), lambda i: (i, 0, 0)),
        out_shape=jax.ShapeDtypeStruct((bn * g, nc, dh), bf16),
        compiler_params=_cparams(("parallel",)),
        name="nsa_compress",
    )(blocks, pos.reshape(1, cin), w1.astype(bf16), w2.astype(bf16))
    return out.reshape(bn, g, nc, dh)


V_ROWS = HEAD_DIM + 16


LOG2E = 1.4426950408889634
K_COLS = 2 * HEAD_DIM


def _masked_flash(i, qs_ref, key_tile, value_tile, mask_bias, delta_ref, m_ref, acc_ref, s_ref, mt_ref, al_ref, chunk,
                  shared_keys):
    width = qs_ref.shape[1]
    n_chunks = width // chunk
    m_ref[...] = jnp.full(m_ref.shape, NEG, f32)
    acc_ref[...] = jnp.zeros(acc_ref.shape, f32)

    def scores_chunk(j, delta, c, shared, slot=None):
        slot = j % 2 if slot is None else slot
        if c == 0 or not shared_keys:
            shared["kt"] = key_tile(j, c)
            shared["maskb"] = jnp.concatenate([mask_bias(j, c)] * (chunk // TQ), axis=1)
        cols = slice(c * chunk, (c + 1) * chunk)
        s = jnp.dot(shared["kt"], qs_ref[:, cols], preferred_element_type=f32) + shared["maskb"]
        if delta is not None:
            s = s + delta_ref[delta, :, cols]
        s_ref[slot, :, cols] = s
        m_old = m_ref[:, cols]
        m_new = jnp.maximum(m_old, jnp.max(s, axis=0, keepdims=True))
        m_ref[:, cols] = m_new
        mt_ref[slot, :, cols] = m_new
        al_ref[slot, :, cols] = jnp.exp2(m_old - m_new)

    def values_chunk(j, c, shared, slot=None):
        slot = j % 2 if slot is None else slot
        if c == 0 or not shared_keys:
            shared["vt"] = value_tile(j, c)
        cols = slice(c * chunk, (c + 1) * chunk)
        p = jnp.exp2(s_ref[slot, :, cols] - mt_ref[slot, :, cols])
        acc_ref[:, cols] = (al_ref[slot, :, cols] * acc_ref[:, cols]
                            + jnp.dot(shared["vt"], p.astype(bf16), preferred_element_type=f32))

    def scores(j, delta):
        shared = {}
        for c in range(n_chunks):
            scores_chunk(j, delta, c, shared)

    def values(j):
        shared = {}
        for c in range(n_chunks):
            values_chunk(j, c, shared)

    def values_then_scores(jv, js, delta, slot_v=None, slot_s=None):
        sv, ss = {}, {}
        for c in range(n_chunks):
            values_chunk(jv, c, sv, slot_v)
            scores_chunk(js, delta, c, ss, slot_s)

    @pl.when(i >= 2)
    def _():
        scores(0, None)

        def quad_body(k, carry):
            j = 4 * k + 1
            values_then_scores(j - 1, j, None, 0, 1)
            values_then_scores(j, j + 1, None, 1, 0)
            values_then_scores(j + 1, j + 2, None, 0, 1)
            values_then_scores(j + 2, j + 3, None, 1, 0)
            return carry

        n_quads = (i - 2) // 4
        lax.fori_loop(0, n_quads, quad_body, 0)

        def single_body(j, carry):
            values_then_scores(j - 1, j, None)
            return carry

        lax.fori_loop(4 * n_quads + 1, i - 1, single_body, 0)

        values_then_scores(i - 2, i - 1, 0)
        values_then_scores(i - 1, i, 1)

    @pl.when(i == 1)
    def _():
        scores(0, 0)
        values_then_scores(0, 1, 1)

    @pl.when(i == 0)
    def _():
        scores(0, 1)

    values(i)


def _flash_operands(near, far):
    nh = near.shape[0]
    width = nh * TQ
    far2 = jnp.repeat(far * LOG2E, TQ).reshape(1, width)
    hi = far2.astype(bf16)
    lo = (far2 - hi.astype(f32)).astype(bf16)
    extra = jnp.concatenate([hi, lo, jnp.zeros((K_COLS - HEAD_DIM - 2, width), bf16)], axis=0)
    neart = near.reshape(nh, TQ, 2, TQ).transpose(2, 3, 0, 1).reshape(2, TQ, width) * LOG2E
    neart = neart - (hi.astype(f32) + lo.astype(f32))
    causal = np.where(np.arange(TQ)[:, None] <= np.arange(TQ)[None, :], 0.0, NEG).astype(np.float32)
    return extra, jnp.stack([neart[0], neart[1] + jnp.asarray(np.tile(causal, (1, nh)))], axis=0)


def _value_tiles(v):
    lead = v.shape[:-2]
    nt = v.shape[-2] // TQ
    vt = jnp.swapaxes(v.reshape(lead + (nt, TQ, HEAD_DIM)), -1, -2)
    return jnp.concatenate([vt, jnp.ones(lead + (nt, 1, TQ), vt.dtype),
                            jnp.zeros(lead + (nt, V_ROWS - HEAD_DIM - 1, TQ), vt.dtype)], axis=-2)


BANDED_TILES = 4


def _banded_kernel(qt_ref, k_ref, vt_ref, bias_ref, neg_ref, sink_ref, o_ref, *, rep, kt_tiles, use_sinks):
    kw = kt_tiles * TQ
    for u in range(qt_ref.shape[2]):
        i = pl.program_id(1) * qt_ref.shape[2] + u
        outs = []
        for g in range(qt_ref.shape[1]):
            qs = jnp.concatenate([(qt_ref[0, g, u].astype(f32) * (HEAD_DIM ** -0.5)).astype(bf16), neg_ref[...]],
                                 axis=0)
            k = k_ref[0, g, pl.ds(pl.multiple_of(i * TQ, TQ), kw), :]
            s = jnp.dot(k, qs, preferred_element_type=f32) + bias_ref[g]
            m = jnp.max(s, axis=0, keepdims=True)
            if use_sinks:
                m = jnp.maximum(m, sink_ref[g])
            p = jnp.exp(s - m).astype(bf16)
            vt = jnp.concatenate([vt_ref[0, g, i + t] for t in range(kt_tiles)], axis=1)
            acc = jnp.dot(vt, p, preferred_element_type=f32)
            l = acc[HEAD_DIM:HEAD_DIM + 1]
            if use_sinks:
                l = l + jnp.exp(sink_ref[g] - m)
            ot = acc[0:HEAD_DIM] / l
            outs += [ot[:, r * TQ:(r + 1) * TQ].T for r in range(rep)]
        o_ref[0, u * TQ:(u + 1) * TQ, :] = jnp.concatenate(outs, axis=1).astype(o_ref.dtype)


def _banded_gqa(qt, k, v, bias, sinks, window, out_dtype):
    bn, g, nt, _, width = qt.shape
    rep = width // TQ
    s = nt * TQ
    pad = -(-(window - 1) // TQ) * TQ
    kw = pad + TQ
    kd = 2 * HEAD_DIM
    flag = jnp.broadcast_to((jnp.arange(pad + s) < pad).astype(k.dtype)[:, None], (bn, g, pad + s, 1))
    kp = jnp.concatenate([jnp.pad(k, ((0, 0), (0, 0), (pad, 0), (0, 0))), flag,
                          jnp.zeros((bn, g, pad + s, kd - HEAD_DIM - 1), k.dtype)], axis=-1)
    vt = _value_tiles(jnp.pad(v, ((0, 0), (0, 0), (pad, 0), (0, 0))))
    neg = np.zeros((kd - HEAD_DIM, width), np.float32)
    neg[0] = NEG
    dist = np.arange(TQ)[:, None] + pad - np.arange(kw)[None, :]
    band = jnp.asarray((dist >= 0) & (dist < window))
    bias_t = jnp.where(band, bias, NEG).reshape(g, rep, TQ, kw).transpose(0, 3, 1, 2).reshape(g, kw, width)
    use_sinks = sinks is not None
    sink_rows = (jnp.repeat(sinks.astype(f32), TQ) if use_sinks else jnp.zeros((g * width,), f32)).reshape(g, 1, width)
    kern = functools.partial(_banded_kernel, rep=rep, kt_tiles=kw // TQ, use_sinks=use_sinks)
    rw = rep * HEAD_DIM
    return pl.pallas_call(
        kern,
        grid=(bn, nt // BANDED_TILES),
        in_specs=[pl.BlockSpec((1, g, BANDED_TILES, HEAD_DIM, width), lambda b, i: (b, 0, i, 0, 0)),
                  pl.BlockSpec((1, g, pad + s, kd), lambda b, i: (b, 0, 0, 0)),
                  pl.BlockSpec((1, g, (pad + s) // TQ, V_ROWS, TQ), lambda b, i: (b, 0, 0, 0, 0)),
                  pl.BlockSpec((g, kw, width), lambda b, i: (0, 0, 0)),
                  pl.BlockSpec((kd - HEAD_DIM, width), lambda b, i: (0, 0)),
                  pl.BlockSpec((g, 1, width), lambda b, i: (0, 0, 0))],
        out_specs=pl.BlockSpec((1, BANDED_TILES * TQ, g * rw), lambda b, i: (b, i, 0)),
        out_shape=jax.ShapeDtypeStruct((bn, s, g * rw), out_dtype),
        compiler_params=_cparams(("parallel", "arbitrary")),
        name="banded_gqa_w%d" % window,
    )(qt, kp, vt, bias_t, jnp.asarray(neg, bf16), sink_rows)


def _nsa_kernel(qt_ref, kc_ref, vct_ref, biasc_ref, ovt_ref, ks_ref, vst_ref, qx_ref, delta_ref, gt_ref, gn_ref,
                owin_ref, o_ref, qs_ref, selb_ref, m_ref, acc_ref, s_ref, mt_ref, al_ref, *, rep, n_sel):
    i = pl.program_id(1)
    t0 = i * TQ
    groups = qt_ref.shape[1]
    nc = kc_ref.shape[2]
    nblk = ovt_ref.shape[0]
    width = rep * TQ
    cmp_end = lax.broadcasted_iota(i32, (nc, TQ), 0) * CMP_STRIDE + (CMP_BLOCK - 1)
    valid_c = t0 + lax.broadcasted_iota(i32, (nc, TQ), 1) >= cmp_end
    blk = lax.broadcasted_iota(i32, (nblk, TQ), 0)
    tq = t0 + lax.broadcasted_iota(i32, (nblk, TQ), 1)
    cur = tq // SLC_BLOCK
    forced = (blk == 0) | (blk == cur) | (blk == cur - 1)
    admitted = blk * SLC_BLOCK <= tq

    o_cmps = []
    for g in range(groups):
        qf = qt_ref[0, g, 0].astype(f32)
        qs = (qf * (HEAD_DIM ** -0.5)).astype(bf16)
        qs_ref[0:HEAD_DIM, g * width:(g + 1) * width] = (qf * (HEAD_DIM ** -0.5 * LOG2E)).astype(bf16)

        s_c = jnp.dot(kc_ref[0, g], qs, preferred_element_type=f32)
        p_cols = []
        p_sum = jnp.zeros((nc, TQ), f32)
        for r in range(rep):
            sr = jnp.where(valid_c, s_c[:, r * TQ:(r + 1) * TQ] + biasc_ref[g * rep + r, 0], NEG)
            m = jnp.max(sr, axis=0, keepdims=True)
            e = jnp.where(valid_c, jnp.exp(sr - m), 0.0)
            l = jnp.sum(e, axis=0, keepdims=True)
            p = e * (1.0 / jnp.maximum(l, 1e-30))
            p_cols.append(p.astype(bf16))
            p_sum = p_sum + p
        o_cmps.append(jnp.dot(vct_ref[0, g], jnp.concatenate(p_cols, axis=1), preferred_element_type=f32))

        imp = jnp.dot(ovt_ref[...], p_sum, preferred_element_type=f32)
        score = jnp.where(forced, POS_BIG, jnp.where(admitted, imp, NEG))
        rank = jnp.zeros((nblk, TQ), i32)
        for kk in range(nblk):
            ck = score[kk:kk + 1, :]
            rank = rank + jnp.where(ck > score, 1, jnp.where(ck == score, jnp.where(blk > kk, 1, 0), 0))
        selb_ref[g] = jnp.where(rank < n_sel, 0.0, NEG)

    per_tile = TQ // SLC_BLOCK

    def mask_bias(j, c):
        rows = [jnp.broadcast_to(selb_ref[c, pl.ds(per_tile * j + t, 1), :], (SLC_BLOCK, TQ))
                for t in range(per_tile)]
        return jnp.concatenate(rows, axis=0)

    qs_ref[HEAD_DIM:K_COLS, :] = qx_ref[...]
    _masked_flash(i, qs_ref,
                  lambda j, c: ks_ref[0, c, pl.ds(pl.multiple_of(j * TQ, TQ), TQ), :],
                  lambda j, c: vst_ref[0, c, j],
                  mask_bias, delta_ref, m_ref, acc_ref, s_ref, mt_ref, al_ref, width, False)
    acc = acc_ref[...]
    o_slc = acc[0:HEAD_DIM] / acc[HEAD_DIM:HEAD_DIM + 1]

    outs = []
    for g in range(groups):
        sig_t = jax.nn.sigmoid(gt_ref[0, g, 0])
        sig_n = jax.nn.sigmoid(gn_ref[0, g])
        for r in range(rep):
            h = g * rep + r
            mixed = (sig_t[r:r + 1] * o_cmps[g][:, r * TQ:(r + 1) * TQ]
                     + sig_t[rep + r:rep + r + 1] * o_slc[:, h * TQ:(h + 1) * TQ])
            o_win = owin_ref[0, :, h * HEAD_DIM:(h + 1) * HEAD_DIM]
            outs.append(mixed.T + sig_n[:, 2 * rep + r:2 * rep + r + 1] * o_win)
    o_ref[0] = jnp.concatenate(outs, axis=1).astype(o_ref.dtype)


def _nsa_mix(qt, kc, vc, bias_c, ks, vs, near, far, gates, o_win):
    bn, g, nt, _, width = qt.shape
    s = nt * TQ
    nc = kc.shape[2]
    rep = width // TQ
    hq = g * rep * HEAD_DIM
    nblk = s // SLC_BLOCK
    n_sel = min(SLC_TOPN, nblk)
    ci = np.arange(nc)[:, None] * CMP_STRIDE
    sj = np.arange(nblk)[None, :] * SLC_BLOCK
    overlap = np.clip(np.minimum(ci + CMP_BLOCK, sj + SLC_BLOCK) - np.maximum(ci, sj), 0, None) / CMP_BLOCK
    overlap[nc - 1] = 0.0
    gates_t = gates.reshape(bn, g, nt, TQ, 3 * rep).transpose(0, 1, 2, 4, 3)
    kern = functools.partial(_nsa_kernel, rep=rep, n_sel=n_sel)
    return pl.pallas_call(
        kern,
        grid=(bn, nt),
        in_specs=[pl.BlockSpec((1, g, 1, HEAD_DIM, width), lambda b, i: (b, 0, i, 0, 0)),
                  pl.BlockSpec((1, g, nc, HEAD_DIM), lambda b, i: (b, 0, 0, 0)),
                  pl.BlockSpec((1, g, HEAD_DIM, nc), lambda b, i: (b, 0, 0, 0)),
                  pl.BlockSpec((g * rep, 1, nc, TQ), lambda b, i: (0, i, 0, 0)),
                  pl.BlockSpec((nblk, nc), lambda b, i: (0, 0)),
                  pl.BlockSpec((1, g, s, K_COLS), lambda b, i: (b, 0, 0, 0)),
                  pl.BlockSpec((1, g, nt, V_ROWS, TQ), lambda b, i: (b, 0, 0, 0, 0)),
                  pl.BlockSpec((K_COLS - HEAD_DIM, g * width), lambda b, i: (0, 0)),
                  pl.BlockSpec((2, TQ, g * width), lambda b, i: (0, 0, 0)),
                  pl.BlockSpec((1, g, 1, 3 * rep, TQ), lambda b, i: (b, 0, i, 0, 0)),
                  pl.BlockSpec((1, g, TQ, 3 * rep), lambda b, i: (b, 0, i, 0)),
                  pl.BlockSpec((1, TQ, hq), lambda b, i: (b, i, 0))],
        out_specs=pl.BlockSpec((1, TQ, hq), lambda b, i: (b, i, 0)),
        out_shape=jax.ShapeDtypeStruct((bn, s, hq), bf16),
        scratch_shapes=[pltpu.VMEM((K_COLS, g * width), bf16), pltpu.VMEM((g, nblk, TQ), f32),
                        pltpu.VMEM((1, g * width), f32), pltpu.VMEM((V_ROWS, g * width), f32),
                        pltpu.VMEM((2, TQ, g * width), f32), pltpu.VMEM((2, 1, g * width), f32),
                        pltpu.VMEM((2, 1, g * width), f32)],
        compiler_params=_cparams(("parallel", "arbitrary")),
        name="nsa_mix",
    )(qt, kc, jnp.swapaxes(vc, 2, 3), bias_c, jnp.asarray(overlap.T, f32), ks, vs,
      *_flash_operands(near, far), gates_t, gates, o_win)


RADIX_BITS_PER_CHECK = 8


def _dsa_kernel(qt_ref, k_ref, vt_ref, qit_ref, ki_ref, wit_ref, qx_ref, delta_ref, tri_ref, o_ref,
                keys_ref, qs_ref, m_ref, acc_ref, s_ref, mt_ref, al_ref, *, topk, chunk):
    i = pl.program_id(1)
    t0 = i * TQ
    width = qt_ref.shape[3]
    nh = width // TQ
    nih = qit_ref.shape[3] // TQ
    krow = lax.broadcasted_iota(i32, (TQ, TQ), 0)
    qcol = lax.broadcasted_iota(i32, (TQ, TQ), 1)
    n_pairs = jnp.maximum(i // 2 + 1, -(-topk // (2 * TQ)))

    qit = qit_ref[0, 0]
    wit = wit_ref[0, 0]

    def score_tile(j):
        kt = ki_ref[0, pl.ds(pl.multiple_of(j * TQ, TQ), TQ), :]
        rel = jnp.maximum(jnp.dot(kt, qit, preferred_element_type=f32), 0.0)
        sc = jnp.zeros((TQ, TQ), f32)
        for h in range(nih):
            sc = sc + wit[h:h + 1, :] * rel[:, h * TQ:(h + 1) * TQ]
        sc = jnp.where(j * TQ + krow <= t0 + qcol, sc, NEG)
        keys_ref[j] = jnp.where(sc == 0.0, 0.0, sc)

    def score_body(jj, carry):
        for u in range(4):
            score_tile(4 * jj + u)
        return carry

    lax.fori_loop(0, (n_pairs + 1) // 2, score_body, 0)

    def code_to_float(code):
        code = jnp.clip(code, -0x7F800001, 0x7F800000)
        bits = jnp.where(code < 0, code ^ 0x7FFFFFFF, code)
        bits = jnp.where((bits > 0) & (bits < 0x00800000), 0x00800000, bits)
        return pltpu.bitcast(bits, f32)

    def count(pred_fn):
        def body(jj, cnt):
            return (cnt + jnp.where(pred_fn(keys_ref[2 * jj]), 1.0, 0.0)
                    + jnp.where(pred_fn(keys_ref[2 * jj + 1]), 1.0, 0.0))
        cnt = lax.fori_loop(0, n_pairs, body, jnp.zeros((TQ, TQ), f32))
        return jnp.sum(cnt, axis=0, keepdims=True)

    def bits_body(state):
        t, prefix, n_ge = state
        for u in range(RADIX_BITS_PER_CHECK):
            cand = prefix ^ jnp.left_shift(jnp.int32(1), 31 - (t + u))
            cand_f = code_to_float(cand)
            cnt = count(lambda sj: sj >= cand_f)
            prefix = jnp.where(cnt >= topk, cand, prefix)
            n_ge = jnp.where(cnt >= topk, cnt, n_ge)
        return t + RADIX_BITS_PER_CHECK, prefix, n_ge

    n_all = (2 * n_pairs * TQ).astype(f32)
    _, code, n_ge = lax.while_loop(lambda st: (st[0] < 32) & (jnp.max(st[2]) > topk), bits_body,
                                   (jnp.int32(0), jnp.full((1, TQ), INT_MIN, i32), jnp.full((1, TQ), n_all, f32)))
    thr = code_to_float(code)

    @pl.when(jnp.max(n_ge) > topk)
    def _():
        need = topk - count(lambda sj: sj > thr)

        def fix_body(j, running):
            sj = keys_ref[j]
            tie = jnp.where(sj == thr, 1.0, 0.0)
            before = jnp.dot(tri_ref[...], tie.astype(bf16), preferred_element_type=f32) + running
            keys_ref[j] = jnp.where(before >= need, jnp.where(sj == thr, -jnp.inf, sj), sj)
            return running + jnp.sum(tie, axis=0, keepdims=True)

        lax.fori_loop(0, 2 * n_pairs, fix_body, jnp.zeros((1, TQ), f32))

    qs_ref[0:HEAD_DIM, :] = (qt_ref[0, 0].astype(f32) * (HEAD_DIM ** -0.5 * LOG2E)).astype(bf16)
    qs_ref[HEAD_DIM:K_COLS, :] = qx_ref[...]
    _masked_flash(i, qs_ref,
                  lambda j, c: k_ref[0, pl.ds(pl.multiple_of(j * TQ, TQ), TQ), :],
                  lambda j, c: vt_ref[0, j],
                  lambda j, c: jnp.where(keys_ref[j] >= thr, 0.0, NEG),
                  delta_ref, m_ref, acc_ref, s_ref, mt_ref, al_ref, chunk, True)
    acc = acc_ref[...]
    ot = acc[0:HEAD_DIM] / acc[HEAD_DIM:HEAD_DIM + 1]
    o_ref[0] = jnp.concatenate([ot[:, h * TQ:(h + 1) * TQ].T for h in range(nh)], axis=1).astype(o_ref.dtype)


def _dsa_attention(qt, k, v, qit, ki, wi, near, far):
    bn, nt, _, width = qt.shape
    s = nt * TQ
    nh = width // TQ
    hq = nh * HEAD_DIM
    nih = qit.shape[3] // TQ
    topk = min(C_TOPK_MAX, s // 4)
    wit = wi.reshape(bn, nt, TQ, nih).transpose(0, 1, 3, 2)
    tri = np.tril(np.ones((TQ, TQ), np.float32), -1)
    kern = functools.partial(_dsa_kernel, topk=topk, chunk=512)
    nt_scr = max(nt, -(-topk // TQ))
    return pl.pallas_call(
        kern,
        grid=(bn, nt),
        in_specs=[pl.BlockSpec((1, 1, HEAD_DIM, width), lambda b, i: (b, i, 0, 0)),
                  pl.BlockSpec((1, s, K_COLS), lambda b, i: (b, 0, 0)),
                  pl.BlockSpec((1, nt, V_ROWS, TQ), lambda b, i: (b, 0, 0, 0)),
                  pl.BlockSpec((1, 1, IDX_DIM, nih * TQ), lambda b, i: (b, i, 0, 0)),
                  pl.BlockSpec((1, s, IDX_DIM), lambda b, i: (b, 0, 0)),
                  pl.BlockSpec((1, 1, nih, TQ), lambda b, i: (b, i, 0, 0)),
                  pl.BlockSpec((K_COLS - HEAD_DIM, width), lambda b, i: (0, 0)),
                  pl.BlockSpec((2, TQ, width), lambda b, i: (0, 0, 0)),
                  pl.BlockSpec((TQ, TQ), lambda b, i: (0, 0))],
        out_specs=pl.BlockSpec((1, TQ, hq), lambda b, i: (b, i, 0)),
        out_shape=jax.ShapeDtypeStruct((bn, s, hq), bf16),
        scratch_shapes=[pltpu.VMEM((nt_scr, TQ, TQ), f32), pltpu.VMEM((K_COLS, width), bf16),
                        pltpu.VMEM((1, width), f32), pltpu.VMEM((V_ROWS, width), f32),
                        pltpu.VMEM((2, TQ, width), f32), pltpu.VMEM((2, 1, width), f32),
                        pltpu.VMEM((2, 1, width), f32)],
        compiler_params=_cparams(("parallel", "arbitrary")),
        name="dsa_attention",
    )(qt, k, v, qit, ki, wit, *_flash_operands(near, far), jnp.asarray(tri, bf16))


def _heads_first(a, g):
    bn, s, _ = a.shape
    return a.reshape(bn, s, g, HEAD_DIM).transpose(0, 2, 1, 3)


def _even_layer(x, table_t, w_in, sinks, cmpk_pos, cmpk_w1, cmpk_w2, cmpv_pos, cmpv_w1, cmpv_w2,
                w_out, ln1_g, ln1_b, ffn_gate, ffn_up, ffn_down, ln2_g, ln2_b):
    bn, s, d = x.shape
    xf = x.reshape(bn * s, d)
    a_q, a_kv, b_q, b_kv = A_HEADS * HEAD_DIM, A_KV * HEAD_DIM, B_HEADS * HEAD_DIM, B_KV * HEAD_DIM
    n_main = a_q + 2 * a_kv + b_q + 6 * b_kv
    rb = B_HEADS // B_KV
    w_tail = jnp.pad(w_in[:, n_main:], ((0, 0), (0, 128 - 3 * B_HEADS))).astype(bf16)
    cols = np.cumsum([0, a_q, a_kv, a_kv, b_q] + [b_kv] * 6)
    w_kv = jnp.concatenate([w_in[:, cols[1]:cols[3]], w_in[:, cols[4]:cols[6]], w_in[:, cols[8]:cols[10]]],
                           axis=1).astype(bf16)
    main, tail, qt_a, qt_b, ks1, vst = _proj_in(
        x, [w_kv, w_tail, w_in[:, cols[0]:cols[1]].T.astype(bf16), w_in[:, cols[3]:cols[4]].T.astype(bf16),
            w_in[:, cols[6]:cols[7]].astype(bf16), w_in[:, cols[7]:cols[8]].T.astype(bf16)],
        [("plain", w_kv.shape[1], bf16, 0), ("plain", 128, f32, 0),
         ("queries", A_KV, A_HEADS // A_KV, HEAD_DIM), ("queries", B_KV, rb, HEAD_DIM),
         ("keys", B_KV, 1, HEAD_DIM), ("values", B_KV, 1, HEAD_DIM)])
    main = main.reshape(bn, s, w_kv.shape[1])
    gates = tail[:, :3 * B_HEADS]
    gates = gates.reshape(bn, s, 3, B_KV, rb).transpose(0, 3, 1, 2, 4).reshape(bn, B_KV, s, 3 * rb)
    kcols = np.cumsum([0, a_kv, a_kv] + [b_kv] * 4)
    ka, va, kc_in, vc_in, kwn, vwn = [main[:, :, kcols[t]:kcols[t + 1]] for t in range(6)]

    near_idx = _bucket_of(np.arange(TQ)[:, None] + TQ - np.arange(2 * TQ)[None, :])
    near = _bias_expand(table_t, near_idx, TQ)
    pad_b = -(-(B_WINDOW - 1) // TQ) * TQ
    win_idx = _bucket_of(np.arange(TQ)[:, None] + pad_b - np.arange(pad_b + TQ)[None, :])
    bias_win = _bias_expand(table_t[A_HEADS:A_HEADS + B_HEADS], win_idx, TQ)
    nc = s // CMP_STRIDE
    nt = s // TQ
    cmp_idx = _bucket_of((np.arange(nt)[:, None, None] * TQ + np.arange(TQ)[None, None, :])
                         - (np.arange(nc)[None, :, None] * CMP_STRIDE + CMP_BLOCK - 1)).reshape(nt * nc, TQ)
    bias_c = _bias_expand(table_t[A_HEADS:A_HEADS + B_HEADS], cmp_idx, min(nt * nc, 512))
    bias_c = bias_c.reshape(B_HEADS, nt, nc, TQ)
    far = table_t[:, NUM_BUCKETS - 1]

    out_a =_banded_gqa(qt_a, _heads_first(ka, A_KV), _heads_first(va, A_KV), near[:A_HEADS], sinks, A_WINDOW, bf16)
    o_win = _banded_gqa(qt_b, _heads_first(kwn, B_KV), _heads_first(vwn, B_KV), bias_win, None, B_WINDOW, f32)
    kc = _nsa_compress(_heads_first(kc_in, B_KV), cmpk_pos, cmpk_w1, cmpk_w2)
    vc = _nsa_compress(_heads_first(vc_in, B_KV), cmpv_pos, cmpv_w1, cmpv_w2)
    out_b = _nsa_mix(qt_b, kc, vc, bias_c, ks1, vst,
                     near[A_HEADS:A_HEADS + B_HEADS], far[A_HEADS:A_HEADS + B_HEADS], gates, o_win)
    mix = jnp.concatenate([out_a, out_b], axis=-1).reshape(bn * s, a_q + b_q)
    x2 = _proj_ffn_res_ln(mix, w_out.astype(bf16), xf, ln1_g, ln1_b, ffn_gate.astype(bf16), ffn_up.astype(bf16),
                          ffn_down.astype(bf16), ln2_g, ln2_b)
    return x2.reshape(bn, s, d), near, far


def _odd_layer(x, near, far, w_in, w_out, ln1_g, ln1_b, router, exp_gate, exp_up, exp_down, ln2_g, ln2_b):
    bn, s, d = x.shape
    xf = x.reshape(bn * s, d)
    c_q = C_HEADS * HEAD_DIM
    n_main = c_q + 2 * HEAD_DIM + IDX_HEADS * IDX_DIM
    w_tail = jnp.pad(w_in[:, n_main:], ((0, 0), (0, 128 - IDX_DIM - IDX_HEADS))).astype(bf16)
    tail, qt, qit, k1, vt = _proj_in(
        x, [w_tail, w_in[:, :c_q].T.astype(bf16), w_in[:, c_q + 2 * HEAD_DIM:n_main].T.astype(bf16),
            w_in[:, c_q:c_q + HEAD_DIM].astype(bf16), w_in[:, c_q + HEAD_DIM:c_q + 2 * HEAD_DIM].T.astype(bf16)],
        [("plain", 128, f32, 0), ("queries", 1, C_HEADS, HEAD_DIM), ("queries", 1, IDX_HEADS, IDX_DIM),
         ("keys", 1, 1, HEAD_DIM), ("values", 1, 1, HEAD_DIM)])
    tail = tail.reshape(bn, s, 128)
    ki = tail[:, :, :IDX_DIM].astype(bf16)
    wi = tail[:, :, IDX_DIM:IDX_DIM + IDX_HEADS]
    mix = _dsa_attention(qt[:, 0], k1[:, 0], vt[:, 0], qit[:, 0], ki, wi, near[:C_HEADS], far[:C_HEADS])
    mix = mix.reshape(bn * s, c_q)
    x1 = _proj_res_ln(mix, w_out.astype(bf16), xf, ln1_g, ln1_b)
    x2 = _moe_res_ln(x1, router, exp_gate.astype(bf16), exp_up.astype(bf16), exp_down.astype(bf16), ln2_g, ln2_b)
    return x2.reshape(bn, s, d)


def kernel(x, rel_bias, l0_w_in, l0_sinks, l0_cmpk_pos, l0_cmpk_w1, l0_cmpk_w2, l0_cmpv_pos, l0_cmpv_w1,
           l0_cmpv_w2, l0_w_out, l0_ln1_g, l0_ln1_b, l0_ffn_gate, l0_ffn_up, l0_ffn_down, l0_ln2_g, l0_ln2_b,
           l1_w_in, l1_w_out, l1_ln1_g, l1_ln1_b, l1_router, l1_exp_gate, l1_exp_up, l1_exp_down, l1_ln2_g,
           l1_ln2_b):
    table_t = rel_bias.T.astype(f32)
    x, near, far = _even_layer(x, table_t, l0_w_in, l0_sinks, l0_cmpk_pos, l0_cmpk_w1, l0_cmpk_w2, l0_cmpv_pos,
                               l0_cmpv_w1, l0_cmpv_w2, l0_w_out, l0_ln1_g, l0_ln1_b, l0_ffn_gate, l0_ffn_up,
                               l0_ffn_down, l0_ln2_g, l0_ln2_b)
    return _odd_layer(x, near, far, l1_w_in, l1_w_out, l1_ln1_g, l1_ln1_b, l1_router, l1_exp_gate, l1_exp_up,
                      l1_exp_down, l1_ln2_g, l1_ln2_b)
```

```python
import functools
import math

import numpy as np
import jax
import jax.numpy as jnp
from jax import lax
from jax.experimental import pallas as pl
from jax.experimental.pallas import tpu as pltpu

f32 = jnp.float32
bf16 = jnp.bfloat16
i32 = jnp.int32

D_MODEL = 1024
HEAD_DIM = 64
NUM_BUCKETS = 32
MAX_DISTANCE = 128
A_HEADS, A_KV, A_WINDOW = 8, 2, 128
B_HEADS, B_KV, B_WINDOW = 8, 2, 512
CMP_BLOCK, CMP_STRIDE = 32, 16
SLC_BLOCK, SLC_TOPN = 64, 16
C_HEADS, IDX_HEADS, IDX_DIM, C_TOPK_MAX = 16, 8, 32, 256
D_FF, N_EXPERTS = 2816, 8
DEPTH = 2
ALPHA = (2.0 * DEPTH) ** 0.25
LN_EPS = 1e-5
NEG = -1e30
POS_BIG = 1e30
INT_MIN = -(2 ** 31)

TQ = 128
FF_CHUNK = 256
VMEM_LIMIT = 56 * 1024 * 1024


def _cparams(sem):
    return pltpu.CompilerParams(dimension_semantics=sem, vmem_limit_bytes=VMEM_LIMIT)


def _bucket_table(max_dist):
    n = np.arange(max_dist + 1)
    max_exact = NUM_BUCKETS // 2
    nf = np.maximum(n, 1).astype(np.float64)
    large = max_exact + (np.log(nf / max_exact) / math.log(MAX_DISTANCE / max_exact)
                         * (NUM_BUCKETS - max_exact)).astype(np.int64)
    large = np.minimum(large, NUM_BUCKETS - 1)
    return np.where(n < max_exact, n, large).astype(np.int32)


def _bucket_of(dist):
    dist = np.maximum(dist, 0)
    return _bucket_table(int(dist.max()))[dist]


def _proj_in_kernel(*refs, groups):
    ng = len(groups)
    x_ref = refs[0]
    wq_refs = refs[1:1 + ng]
    oq_refs = refs[1 + ng:]
    xb = x_ref[...].astype(bf16)
    tm = xb.shape[0]
    for wq_ref, oq_ref, (kind, gb, hb, dh) in zip(wq_refs, oq_refs, groups):
        if kind == "plain":
            oq_ref[...] = jnp.dot(xb, wq_ref[...], preferred_element_type=f32).astype(oq_ref.dtype)
            continue
        if kind == "keys":
            kk = jnp.dot(xb, wq_ref[...], preferred_element_type=f32).astype(bf16)
            ones = jnp.where(lax.broadcasted_iota(i32, (tm, K_COLS - dh), 1) < 2, 1.0, 0.0).astype(bf16)
            for g in range(gb):
                oq_ref[0, g, :, 0:dh] = kk[:, g * dh:(g + 1) * dh]
                oq_ref[0, g, :, dh:K_COLS] = ones
            continue
        qt = lax.dot_general(wq_ref[...], xb, (((1,), (1,)), ((), ())), preferred_element_type=f32).astype(bf16)
        ones = jnp.where(lax.broadcasted_iota(i32, (V_ROWS - HEAD_DIM, TQ), 0) < 1, 1.0, 0.0).astype(bf16)
        for g in range(gb):
            for h in range(hb):
                rows = slice((g * hb + h) * dh, (g * hb + h + 1) * dh)
                for t in range(tm // TQ):
                    oq_ref[0, g, t, 0:dh, h * TQ:(h + 1) * TQ] = qt[rows, t * TQ:(t + 1) * TQ]
                    if kind == "values":
                        oq_ref[0, g, t, dh:V_ROWS, :] = ones


def _proj_in(x3, w_groups, groups, tm=512):
    bn, s, k = x3.shape
    m = bn * s
    tm = min(tm, s)
    per_b = s // tm
    tiles = tm // TQ
    q_specs, q_shapes = [], []
    for kind, gb, hb, dh in groups:
        if kind == "plain":
            q_specs.append(pl.BlockSpec((tm, gb), lambda i: (i, 0)))
            q_shapes.append(jax.ShapeDtypeStruct((m, gb), hb))
        elif kind == "keys":
            q_specs.append(pl.BlockSpec((1, gb, tm, K_COLS), lambda i: (i // per_b, 0, i % per_b, 0)))
            q_shapes.append(jax.ShapeDtypeStruct((bn, gb, s, K_COLS), bf16))
        else:
            rows = V_ROWS if kind == "values" else dh
            q_specs.append(pl.BlockSpec((1, gb, tiles, rows, hb * TQ), lambda i: (i // per_b, 0, i % per_b, 0, 0)))
            q_shapes.append(jax.ShapeDtypeStruct((bn, gb, s // TQ, rows, hb * TQ), bf16))
    return pl.pallas_call(
        functools.partial(_proj_in_kernel, groups=tuple(groups)),
        grid=(m // tm,),
        in_specs=[pl.BlockSpec((tm, k), lambda i: (i, 0))]
                 + [pl.BlockSpec(w.shape, lambda i: (0, 0)) for w in w_groups],
        out_specs=q_specs,
        out_shape=q_shapes,
        compiler_params=_cparams(("parallel",)),
        name="proj_in",
    )(x3.reshape(m, k), *w_groups)


def _bias_expand_kernel(tab_ref, idx_ref, o_ref):
    h = pl.program_id(0)
    idx = idx_ref[...]
    acc = jnp.zeros(idx.shape, f32)
    for b in range(NUM_BUCKETS):
        acc = jnp.where(idx == b, tab_ref[h, b], acc)
    o_ref[0] = acc


def _bias_expand(table_t, idx, tr):
    nh = table_t.shape[0]
    r, c = idx.shape
    return pl.pallas_call(
        _bias_expand_kernel,
        grid=(nh, r // tr),
        in_specs=[pl.BlockSpec(memory_space=pltpu.SMEM),
                  pl.BlockSpec((tr, c), lambda h, i: (i, 0))],
        out_specs=pl.BlockSpec((1, tr, c), lambda h, i: (h, i, 0)),
        out_shape=jax.ShapeDtypeStruct((nh, r, c), f32),
        compiler_params=_cparams(("parallel", "arbitrary")),
        name="bias_expand",
    )(table_t, jnp.asarray(idx, dtype=i32))


def _layer_norm(z, g, b):
    mu = jnp.mean(z, axis=-1, keepdims=True)
    zc = z - mu
    var = jnp.mean(zc * zc, axis=-1, keepdims=True)
    return zc * lax.rsqrt(var + LN_EPS) * g + b


def _proj_ln_kernel(mix_ref, w_ref, x_ref, g_ref, b_ref, o_ref):
    y = jnp.dot(mix_ref[...], w_ref[...], preferred_element_type=f32)
    o_ref[...] = _layer_norm(ALPHA * x_ref[...] + y, g_ref[...], b_ref[...])


def _proj_res_ln(mix, w, x, g, b, tm=512):
    m, k = mix.shape
    d = w.shape[1]
    return pl.pallas_call(
        _proj_ln_kernel,
        grid=(m // tm,),
        in_specs=[pl.BlockSpec((tm, k), lambda i: (i, 0)),
                  pl.BlockSpec((k, d), lambda i: (0, 0)),
                  pl.BlockSpec((tm, d), lambda i: (i, 0)),
                  pl.BlockSpec((1, d), lambda i: (0, 0)),
                  pl.BlockSpec((1, d), lambda i: (0, 0))],
        out_specs=pl.BlockSpec((tm, d), lambda i: (i, 0)),
        out_shape=jax.ShapeDtypeStruct((m, d), f32),
        compiler_params=_cparams(("parallel",)),
        name="proj_res_ln",
    )(mix, w, x, g.reshape(1, d), b.reshape(1, d))


def _swiglu_accumulate(xb, wg_ref, wu_ref, wd_ref, acc_ref, widx):
    d_ff = wg_ref.shape[-1]
    for c in range(d_ff // FF_CHUNK):
        cols = slice(c * FF_CHUNK, (c + 1) * FF_CHUNK)
        gate = jnp.dot(xb, wg_ref[widx + (slice(None), cols)], preferred_element_type=f32)
        up = jnp.dot(xb, wu_ref[widx + (slice(None), cols)], preferred_element_type=f32)
        h = jax.nn.silu(gate) * up
        acc_ref[...] += jnp.dot(h.astype(bf16), wd_ref[widx + (cols, slice(None))], preferred_element_type=f32)


def _proj_ffn_kernel(mix_ref, wo_ref, x_ref, g1_ref, b1_ref, wg_ref, wu_ref, wd_ref, g2_ref, b2_ref, o_ref, acc_ref):
    y = jnp.dot(mix_ref[...], wo_ref[...], preferred_element_type=f32)
    o_ref[...] = _layer_norm(ALPHA * x_ref[...] + y, g1_ref[...], b1_ref[...])
    acc_ref[...] = jnp.zeros_like(acc_ref)
    _swiglu_accumulate(o_ref[...].astype(bf16), wg_ref, wu_ref, wd_ref, acc_ref, ())
    o_ref[...] = _layer_norm(ALPHA * o_ref[...] + acc_ref[...], g2_ref[...], b2_ref[...])


def _proj_ffn_res_ln(mix, w_out, x, g1, b1, wg, wu, wd, g2, b2, tm=512):
    m, k = mix.shape
    d = x.shape[1]
    d_ff = wg.shape[1]
    single = pl.Buffered(1)
    row = lambda i: (i, 0)
    fixed = lambda i: (0, 0)
    return pl.pallas_call(
        _proj_ffn_kernel,
        grid=(m // tm,),
        in_specs=[pl.BlockSpec((tm, k), row),
                  pl.BlockSpec((k, d), fixed, pipeline_mode=single),
                  pl.BlockSpec((tm, d), row),
                  pl.BlockSpec((1, d), fixed),
                  pl.BlockSpec((1, d), fixed),
                  pl.BlockSpec((d, d_ff), fixed, pipeline_mode=single),
                  pl.BlockSpec((d, d_ff), fixed, pipeline_mode=single),
                  pl.BlockSpec((d_ff, d), fixed, pipeline_mode=single),
                  pl.BlockSpec((1, d), fixed),
                  pl.BlockSpec((1, d), fixed)],
        out_specs=pl.BlockSpec((tm, d), row),
        out_shape=jax.ShapeDtypeStruct((m, d), f32),
        scratch_shapes=[pltpu.VMEM((tm, d), f32)],
        compiler_params=_cparams(("parallel",)),
        name="proj_ffn_res_ln",
    )(mix, w_out, x, g1.reshape(1, d), b1.reshape(1, d), wg, wu, wd, g2.reshape(1, d), b2.reshape(1, d))


MOE_CAP = 288


def _router_gates(x, r_ref, lane):
    logits = jnp.dot(x, r_ref[...], preferred_element_type=f32)
    logits = jnp.where(lane < N_EXPERTS, logits, -jnp.inf)
    m1 = jnp.max(logits, axis=-1, keepdims=True)
    i1 = jnp.min(jnp.where(logits == m1, lane, 128), axis=-1, keepdims=True)
    rest = jnp.where(lane == i1, -jnp.inf, logits)
    m2 = jnp.max(rest, axis=-1, keepdims=True)
    i2 = jnp.min(jnp.where(rest == m2, lane, 128), axis=-1, keepdims=True)
    e2 = jnp.exp(m2 - m1)
    g1 = 1.0 / (1.0 + e2)
    return jnp.where(lane == i1, g1, 0.0) + jnp.where(lane == i2, e2 * g1, 0.0)


def _moe_kernel(x_ref, r_ref, tri_ref, wg_ref, wu_ref, wd_ref, g_ref, b_ref, o_ref,
                split_ref, post_ref, yc_ref):
    e = pl.program_id(1)
    tm = x_ref.shape[0]

    @pl.when(e == 0)
    def _():
        comb = _router_gates(x_ref[...], r_ref, lax.broadcasted_iota(i32, (tm, 128), 1))
        hi = comb.astype(bf16)
        rest = comb - hi.astype(f32)
        mid = rest.astype(bf16)
        split_ref[0] = hi
        split_ref[1] = mid
        split_ref[2] = (rest - mid.astype(f32)).astype(bf16)
        chosen = jnp.where(comb > 0.0, 1.0, 0.0)
        pos = jnp.dot(tri_ref[...], chosen.astype(bf16), preferred_element_type=f32)
        pos = jnp.where(comb > 0.0, pos, -1.0)
        for t in range(tm // 128):
            rows = slice(t * 128, (t + 1) * 128)
            post_ref[:, rows] = pos[rows].T[0:N_EXPERTS]
        hi32 = jnp.floor((pos + 1.0) * (1.0 / 32.0))
        split_ref[3] = hi32.astype(bf16)
        split_ref[4] = (pos + 1.0 - 32.0 * hi32).astype(bf16)
        o_ref[...] = jnp.zeros_like(o_ref)

    pos_row = post_ref[pl.ds(e, 1), :].astype(i32)
    n_tok = jnp.sum(jnp.where(pos_row >= 0, 1.0, 0.0)).astype(i32)
    pick = jnp.where(lax.broadcasted_iota(i32, (128, 128), 0) == e, 1.0, 0.0).astype(bf16)

    def column(t):
        return jnp.dot(split_ref[t], pick, preferred_element_type=f32)

    gate_b = column(0) + column(1) + column(2)
    pos_b = (32.0 * column(3) + column(4) - 1.0).astype(i32)
    lane = lax.broadcasted_iota(i32, (tm, 128), 1)

    def pass_body(p, carry):
        base = p * MOE_CAP
        gather = jnp.where(pos_row == base + lax.broadcasted_iota(i32, (MOE_CAP, tm), 0), 1.0, 0.0).astype(bf16)
        xc = jnp.dot(gather, x_ref[...].astype(bf16), preferred_element_type=f32).astype(bf16)
        yc_ref[...] = jnp.zeros_like(yc_ref)
        _swiglu_accumulate(xc, wg_ref, wu_ref, wd_ref, yc_ref, (0,))
        scatter = jnp.concatenate([jnp.where(pos_b == base + t * 128 + lane, 1.0, 0.0).astype(bf16)
                                   for t in range(-(-MOE_CAP // 128))], axis=1)[:, :MOE_CAP]
        y = jnp.dot(scatter, yc_ref[...].astype(bf16), preferred_element_type=f32)
        o_ref[...] += jnp.concatenate([gate_b] * (y.shape[1] // 128), axis=1) * y
        return carry

    lax.fori_loop(0, (n_tok + MOE_CAP - 1) // MOE_CAP, pass_body, 0)

    @pl.when(e == N_EXPERTS - 1)
    def _():
        o_ref[...] = _layer_norm(ALPHA * x_ref[...] + o_ref[...], g_ref[...], b_ref[...])


def _moe_res_ln(x, router, wg, wu, wd, g, b, tm=1024):
    m, d = x.shape
    ne, _, d_ff = wg.shape
    tm = min(tm, m)
    router_p = jnp.pad(router, ((0, 0), (0, 128 - ne)))
    tri = jnp.asarray(np.tril(np.ones((tm, tm), np.float32), -1), bf16)
    single = pl.Buffered(1)
    return pl.pallas_call(
        _moe_kernel,
        grid=(m // tm, ne),
        in_specs=[pl.BlockSpec((tm, d), lambda i, e: (i, 0), pipeline_mode=single),
                  pl.BlockSpec((d, 128), lambda i, e: (0, 0), pipeline_mode=single),
                  pl.BlockSpec((tm, tm), lambda i, e: (0, 0), pipeline_mode=single),
                  pl.BlockSpec((1, d, d_ff), lambda i, e: (e, 0, 0)),
                  pl.BlockSpec((1, d, d_ff), lambda i, e: (e, 0, 0)),
                  pl.BlockSpec((1, d_ff, d), lambda i, e: (e, 0, 0)),
                  pl.BlockSpec((1, d), lambda i, e: (0, 0)),
                  pl.BlockSpec((1, d), lambda i, e: (0, 0))],
        out_specs=pl.BlockSpec((tm, d), lambda i, e: (i, 0)),
        out_shape=jax.ShapeDtypeStruct((m, d), f32),
        scratch_shapes=[pltpu.VMEM((5, tm, 128), bf16), pltpu.VMEM((N_EXPERTS, tm), f32),
                        pltpu.VMEM((MOE_CAP, d), f32)],
        compiler_params=_cparams(("parallel", "arbitrary")),
        name="moe_res_ln",
    )(x, router_p, tri, wg, wu, wd, g.reshape(1, d), b.reshape(1, d))


def _compress_kernel(blk_ref, pos_ref, w1_ref, w2_ref, o_ref):
    xb = (blk_ref[0].astype(f32) + pos_ref[...]).astype(bf16)
    h = jax.nn.gelu(jnp.dot(xb, w1_ref[...], preferred_element_type=f32))
    o_ref[0] = jnp.dot(h.astype(bf16), w2_ref[...], preferred_element_type=f32).astype(o_ref.dtype)


def _nsa_compress(kv, pos, w1, w2):
    bn, g, s, dh = kv.shape
    nc = s // CMP_STRIDE
    chunks = kv.reshape(bn, g, nc, CMP_STRIDE * dh)
    nxt = jnp.pad(chunks[:, :, 1:], ((0, 0), (0, 0), (0, 1), (0, 0)))
    blocks = jnp.concatenate([chunks, nxt], axis=-1).reshape(bn * g, nc, CMP_BLOCK * dh)
    cin = CMP_BLOCK * dh
    hid = w1.shape[1]
    out = pl.pallas_call(
        _compress_kernel,
        grid=(bn * g,),
        in_specs=[pl.BlockSpec((1, nc, cin), lambda i: (i, 0, 0)),
                  pl.BlockSpec((1, cin), lambda i: (0, 0)),
                  pl.BlockSpec((cin, hid), lambda i: (0, 0)),
                  pl.BlockSpec((hid, dh), lambda i: (0, 0))],
        out_specs=pl.BlockSpec((1, nc, dh), lambda i: (i, 0, 0)),
        out_shape=jax.ShapeDtypeStruct((bn * g, nc, dh), bf16),
        compiler_params=_cparams(("parallel",)),
        name="nsa_compress",
    )(blocks, pos.reshape(1, cin), w1.astype(bf16), w2.astype(bf16))
    return out.reshape(bn, g, nc, dh)


V_ROWS = HEAD_DIM + 16


LOG2E = 1.4426950408889634
K_COLS = 2 * HEAD_DIM


def _masked_flash(i, qs_ref, key_tile, value_tile, mask_bias, delta_ref, m_ref, acc_ref, s_ref, mt_ref, al_ref, chunk,
                  shared_keys):
    width = qs_ref.shape[1]
    n_chunks = width // chunk
    m_ref[...] = jnp.full(m_ref.shape, NEG, f32)
    acc_ref[...] = jnp.zeros(acc_ref.shape, f32)

    def scores_chunk(j, delta, c, shared, slot=None):
        slot = j % 2 if slot is None else slot
        if c == 0 or not shared_keys:
            shared["kt"] = key_tile(j, c)
            shared["maskb"] = jnp.concatenate([mask_bias(j, c)] * (chunk // TQ), axis=1)
        cols = slice(c * chunk, (c + 1) * chunk)
        s = jnp.dot(shared["kt"], qs_ref[:, cols], preferred_element_type=f32) + shared["maskb"]
        if delta is not None:
            s = s + delta_ref[delta, :, cols]
        s_ref[slot, :, cols] = s
        m_old = m_ref[:, cols]
        m_new = jnp.maximum(m_old, jnp.max(s, axis=0, keepdims=True))
        m_ref[:, cols] = m_new
        mt_ref[slot, :, cols] = m_new
        al_ref[slot, :, cols] = jnp.exp2(m_old - m_new)

    def values_chunk(j, c, shared, slot=None):
        slot = j % 2 if slot is None else slot
        if c == 0 or not shared_keys:
            shared["vt"] = value_tile(j, c)
        cols = slice(c * chunk, (c + 1) * chunk)
        p = jnp.exp2(s_ref[slot, :, cols] - mt_ref[slot, :, cols])
        acc_ref[:, cols] = (al_ref[slot, :, cols] * acc_ref[:, cols]
                            + jnp.dot(shared["vt"], p.astype(bf16), preferred_element_type=f32))

    def scores(j, delta):
        shared = {}
        for c in range(n_chunks):
            scores_chunk(j, delta, c, shared)

    def values(j, slot=None):
        shared = {}
        for c in range(n_chunks):
            values_chunk(j, c, shared, slot)

    def values_then_scores(jv, js, delta, slot_v=None, slot_s=None):
        sv, ss = {}, {}
        for c in range(n_chunks):
            values_chunk(jv, c, sv, slot_v)
            scores_chunk(js, delta, c, ss, slot_s)

    @pl.when(i >= 2)
    def _():
        scores(0, None)

        def quad_body(k, carry):
            j = 4 * k + 1
            values_then_scores(j - 1, j, None, 0, 1)
            values_then_scores(j, j + 1, None, 1, 0)
            values_then_scores(j + 1, j + 2, None, 0, 1)
            values_then_scores(j + 2, j + 3, None, 1, 0)
            return carry

        n_quads = (i - 2) // 4
        lax.fori_loop(0, n_quads, quad_body, 0)

        def single_body(j, carry):
            values_then_scores(j - 1, j, None)
            return carry

        lax.fori_loop(4 * n_quads + 1, i - 1, single_body, 0)

    for par in range(2):
        @pl.when((i >= 2) & (i % 2 == par))
        def _():
            values_then_scores(i - 2, i - 1, 0, par, 1 - par)
            values_then_scores(i - 1, i, 1, 1 - par, par)
            values(i, par)

    @pl.when(i == 1)
    def _():
        scores(0, 0)
        values_then_scores(0, 1, 1, 0, 1)
        values(1, 1)

    @pl.when(i == 0)
    def _():
        scores(0, 1)
        values(0, 0)


def _flash_operands(near, far):
    nh = near.shape[0]
    width = nh * TQ
    far2 = jnp.repeat(far * LOG2E, TQ).reshape(1, width)
    hi = far2.astype(bf16)
    lo = (far2 - hi.astype(f32)).astype(bf16)
    extra = jnp.concatenate([hi, lo, jnp.zeros((K_COLS - HEAD_DIM - 2, width), bf16)], axis=0)
    neart = near.reshape(nh, TQ, 2, TQ).transpose(2, 3, 0, 1).reshape(2, TQ, width) * LOG2E
    neart = neart - (hi.astype(f32) + lo.astype(f32))
    causal = np.where(np.arange(TQ)[:, None] <= np.arange(TQ)[None, :], 0.0, NEG).astype(np.float32)
    return extra, jnp.stack([neart[0], neart[1] + jnp.asarray(np.tile(causal, (1, nh)))], axis=0)


def _value_tiles(v):
    lead = v.shape[:-2]
    nt = v.shape[-2] // TQ
    vt = jnp.swapaxes(v.reshape(lead + (nt, TQ, HEAD_DIM)), -1, -2)
    return jnp.concatenate([vt, jnp.ones(lead + (nt, 1, TQ), vt.dtype),
                            jnp.zeros(lead + (nt, V_ROWS - HEAD_DIM - 1, TQ), vt.dtype)], axis=-2)


BANDED_TILES = 4


def _banded_kernel(qt_ref, k_ref, vt_ref, bias_ref, neg_ref, sink_ref, o_ref, *, rep, kt_tiles, use_sinks):
    kw = kt_tiles * TQ
    for u in range(qt_ref.shape[2]):
        i = pl.program_id(1) * qt_ref.shape[2] + u
        outs = []
        for g in range(qt_ref.shape[1]):
            qs = jnp.concatenate([(qt_ref[0, g, u].astype(f32) * (HEAD_DIM ** -0.5)).astype(bf16), neg_ref[...]],
                                 axis=0)
            k = k_ref[0, g, pl.ds(pl.multiple_of(i * TQ, TQ), kw), :]
            s = jnp.dot(k, qs, preferred_element_type=f32) + bias_ref[g]
            m = jnp.max(s, axis=0, keepdims=True)
            if use_sinks:
                m = jnp.maximum(m, sink_ref[g])
            p = jnp.exp(s - m).astype(bf16)
            vt = jnp.concatenate([vt_ref[0, g, i + t] for t in range(kt_tiles)], axis=1)
            acc = jnp.dot(vt, p, preferred_element_type=f32)
            l = acc[HEAD_DIM:HEAD_DIM + 1]
            if use_sinks:
                l = l + jnp.exp(sink_ref[g] - m)
            ot = acc[0:HEAD_DIM] / l
            outs += [ot[:, r * TQ:(r + 1) * TQ].T for r in range(rep)]
        o_ref[0, u * TQ:(u + 1) * TQ, :] = jnp.concatenate(outs, axis=1).astype(o_ref.dtype)


def _banded_gqa(qt, k, v, bias, sinks, window, out_dtype):
    bn, g, nt, _, width = qt.shape
    rep = width // TQ
    s = nt * TQ
    pad = -(-(window - 1) // TQ) * TQ
    kw = pad + TQ
    kd = 2 * HEAD_DIM
    flag = jnp.broadcast_to((jnp.arange(pad + s) < pad).astype(k.dtype)[:, None], (bn, g, pad + s, 1))
    kp = jnp.concatenate([jnp.pad(k, ((0, 0), (0, 0), (pad, 0), (0, 0))), flag,
                          jnp.zeros((bn, g, pad + s, kd - HEAD_DIM - 1), k.dtype)], axis=-1)
    vt = _value_tiles(jnp.pad(v, ((0, 0), (0, 0), (pad, 0), (0, 0))))
    neg = np.zeros((kd - HEAD_DIM, width), np.float32)
    neg[0] = NEG
    dist = np.arange(TQ)[:, None] + pad - np.arange(kw)[None, :]
    band = jnp.asarray((dist >= 0) & (dist < window))
    bias_t = jnp.where(band, bias, NEG).reshape(g, rep, TQ, kw).transpose(0, 3, 1, 2).reshape(g, kw, width)
    use_sinks = sinks is not None
    sink_rows = (jnp.repeat(sinks.astype(f32), TQ) if use_sinks else jnp.zeros((g * width,), f32)).reshape(g, 1, width)
    kern = functools.partial(_banded_kernel, rep=rep, kt_tiles=kw // TQ, use_sinks=use_sinks)
    rw = rep * HEAD_DIM
    return pl.pallas_call(
        kern,
        grid=(bn, nt // BANDED_TILES),
        in_specs=[pl.BlockSpec((1, g, BANDED_TILES, HEAD_DIM, width), lambda b, i: (b, 0, i, 0, 0)),
                  pl.BlockSpec((1, g, pad + s, kd), lambda b, i: (b, 0, 0, 0)),
                  pl.BlockSpec((1, g, (pad + s) // TQ, V_ROWS, TQ), lambda b, i: (b, 0, 0, 0, 0)),
                  pl.BlockSpec((g, kw, width), lambda b, i: (0, 0, 0)),
                  pl.BlockSpec((kd - HEAD_DIM, width), lambda b, i: (0, 0)),
                  pl.BlockSpec((g, 1, width), lambda b, i: (0, 0, 0))],
        out_specs=pl.BlockSpec((1, BANDED_TILES * TQ, g * rw), lambda b, i: (b, i, 0)),
        out_shape=jax.ShapeDtypeStruct((bn, s, g * rw), out_dtype),
        compiler_params=_cparams(("parallel", "arbitrary")),
        name="banded_gqa_w%d" % window,
    )(qt, kp, vt, bias_t, jnp.asarray(neg, bf16), sink_rows)


def _nsa_kernel(qt_ref, kc_ref, vct_ref, biasc_ref, ovt_ref, ks_ref, vst_ref, qx_ref, delta_ref, gt_ref, gn_ref,
                owin_ref, o_ref, qs_ref, selb_ref, m_ref, acc_ref, s_ref, mt_ref, al_ref, *, rep, n_sel):
    i = pl.program_id(1)
    t0 = i * TQ
    groups = qt_ref.shape[1]
    nc = kc_ref.shape[2]
    nblk = ovt_ref.shape[0]
    width = rep * TQ
    cmp_end = lax.broadcasted_iota(i32, (nc, TQ), 0) * CMP_STRIDE + (CMP_BLOCK - 1)
    valid_c = t0 + lax.broadcasted_iota(i32, (nc, TQ), 1) >= cmp_end
    blk = lax.broadcasted_iota(i32, (nblk, TQ), 0)
    tq = t0 + lax.broadcasted_iota(i32, (nblk, TQ), 1)
    cur = tq // SLC_BLOCK
    forced = (blk == 0) | (blk == cur) | (blk == cur - 1)
    admitted = blk * SLC_BLOCK <= tq

    o_cmps = []
    for g in range(groups):
        qf = qt_ref[0, g, 0].astype(f32)
        qs = (qf * (HEAD_DIM ** -0.5)).astype(bf16)
        qs_ref[0:HEAD_DIM, g * width:(g + 1) * width] = (qf * (HEAD_DIM ** -0.5 * LOG2E)).astype(bf16)

        s_c = jnp.dot(kc_ref[0, g], qs, preferred_element_type=f32)
        p_cols = []
        p_sum = jnp.zeros((nc, TQ), f32)
        for r in range(rep):
            sr = jnp.where(valid_c, s_c[:, r * TQ:(r + 1) * TQ] + biasc_ref[g * rep + r, 0], NEG)
            m = jnp.max(sr, axis=0, keepdims=True)
            e = jnp.where(valid_c, jnp.exp(sr - m), 0.0)
            l = jnp.sum(e, axis=0, keepdims=True)
            p = e * (1.0 / jnp.maximum(l, 1e-30))
            p_cols.append(p.astype(bf16))
            p_sum = p_sum + p
        o_cmps.append(jnp.dot(vct_ref[0, g], jnp.concatenate(p_cols, axis=1), preferred_element_type=f32))

        imp = jnp.dot(ovt_ref[...], p_sum, preferred_element_type=f32)
        score = jnp.where(forced, POS_BIG, jnp.where(admitted, imp, NEG))
        rank = jnp.zeros((nblk, TQ), i32)
        for kk in range(nblk):
            ck = score[kk:kk + 1, :]
            rank = rank + jnp.where(ck > score, 1, jnp.where(ck == score, jnp.where(blk > kk, 1, 0), 0))
        selb_ref[g] = jnp.where(rank < n_sel, 0.0, NEG)

    per_tile = TQ // SLC_BLOCK

    def mask_bias(j, c):
        rows = [jnp.broadcast_to(selb_ref[c, pl.ds(per_tile * j + t, 1), :], (SLC_BLOCK, TQ))
                for t in range(per_tile)]
        return jnp.concatenate(rows, axis=0)

    qs_ref[HEAD_DIM:K_COLS, :] = qx_ref[...]
    _masked_flash(i, qs_ref,
                  lambda j, c: ks_ref[0, c, pl.ds(pl.multiple_of(j * TQ, TQ), TQ), :],
                  lambda j, c: vst_ref[0, c, j],
                  mask_bias, delta_ref, m_ref, acc_ref, s_ref, mt_ref, al_ref, width, False)
    acc = acc_ref[...]
    o_slc = acc[0:HEAD_DIM] / acc[HEAD_DIM:HEAD_DIM + 1]

    outs = []
    for g in range(groups):
        sig_t = jax.nn.sigmoid(gt_ref[0, g, 0])
        sig_n = jax.nn.sigmoid(gn_ref[0, g])
        for r in range(rep):
            h = g * rep + r
            mixed = (sig_t[r:r + 1] * o_cmps[g][:, r * TQ:(r + 1) * TQ]
                     + sig_t[rep + r:rep + r + 1] * o_slc[:, h * TQ:(h + 1) * TQ])
            o_win = owin_ref[0, :, h * HEAD_DIM:(h + 1) * HEAD_DIM]
            outs.append(mixed.T + sig_n[:, 2 * rep + r:2 * rep + r + 1] * o_win)
    o_ref[0] = jnp.concatenate(outs, axis=1).astype(o_ref.dtype)


def _nsa_mix(qt, kc, vc, bias_c, ks, vs, near, far, gates, o_win):
    bn, g, nt, _, width = qt.shape
    s = nt * TQ
    nc = kc.shape[2]
    rep = width // TQ
    hq = g * rep * HEAD_DIM
    nblk = s // SLC_BLOCK
    n_sel = min(SLC_TOPN, nblk)
    ci = np.arange(nc)[:, None] * CMP_STRIDE
    sj = np.arange(nblk)[None, :] * SLC_BLOCK
    overlap = np.clip(np.minimum(ci + CMP_BLOCK, sj + SLC_BLOCK) - np.maximum(ci, sj), 0, None) / CMP_BLOCK
    overlap[nc - 1] = 0.0
    gates_t = gates.reshape(bn, g, nt, TQ, 3 * rep).transpose(0, 1, 2, 4, 3)
    kern = functools.partial(_nsa_kernel, rep=rep, n_sel=n_sel)
    return pl.pallas_call(
        kern,
        grid=(bn, nt),
        in_specs=[pl.BlockSpec((1, g, 1, HEAD_DIM, width), lambda b, i: (b, 0, i, 0, 0)),
                  pl.BlockSpec((1, g, nc, HEAD_DIM), lambda b, i: (b, 0, 0, 0)),
                  pl.BlockSpec((1, g, HEAD_DIM, nc), lambda b, i: (b, 0, 0, 0)),
                  pl.BlockSpec((g * rep, 1, nc, TQ), lambda b, i: (0, i, 0, 0)),
                  pl.BlockSpec((nblk, nc), lambda b, i: (0, 0)),
                  pl.BlockSpec((1, g, s, K_COLS), lambda b, i: (b, 0, 0, 0)),
                  pl.BlockSpec((1, g, nt, V_ROWS, TQ), lambda b, i: (b, 0, 0, 0, 0)),
                  pl.BlockSpec((K_COLS - HEAD_DIM, g * width), lambda b, i: (0, 0)),
                  pl.BlockSpec((2, TQ, g * width), lambda b, i: (0, 0, 0)),
                  pl.BlockSpec((1, g, 1, 3 * rep, TQ), lambda b, i: (b, 0, i, 0, 0)),
                  pl.BlockSpec((1, g, TQ, 3 * rep), lambda b, i: (b, 0, i, 0)),
                  pl.BlockSpec((1, TQ, hq), lambda b, i: (b, i, 0))],
        out_specs=pl.BlockSpec((1, TQ, hq), lambda b, i: (b, i, 0)),
        out_shape=jax.ShapeDtypeStruct((bn, s, hq), bf16),
        scratch_shapes=[pltpu.VMEM((K_COLS, g * width), bf16), pltpu.VMEM((g, nblk, TQ), f32),
                        pltpu.VMEM((1, g * width), f32), pltpu.VMEM((V_ROWS, g * width), f32),
                        pltpu.VMEM((2, TQ, g * width), f32), pltpu.VMEM((2, 1, g * width), f32),
                        pltpu.VMEM((2, 1, g * width), f32)],
        compiler_params=_cparams(("parallel", "arbitrary")),
        name="nsa_mix",
    )(qt, kc, jnp.swapaxes(vc, 2, 3), bias_c, jnp.asarray(overlap.T, f32), ks, vs,
      *_flash_operands(near, far), gates_t, gates, o_win)


RADIX_BITS_PER_CHECK = 8


def _dsa_kernel(qt_ref, k_ref, vt_ref, qit_ref, ki_ref, wit_ref, qx_ref, delta_ref, tri_ref, o_ref,
                keys_ref, qs_ref, m_ref, acc_ref, s_ref, mt_ref, al_ref, *, topk, chunk):
    i = pl.program_id(1)
    t0 = i * TQ
    width = qt_ref.shape[3]
    nh = width // TQ
    nih = qit_ref.shape[3] // TQ
    krow = lax.broadcasted_iota(i32, (TQ, TQ), 0)
    qcol = lax.broadcasted_iota(i32, (TQ, TQ), 1)
    n_pairs = jnp.maximum(i // 2 + 1, -(-topk // (2 * TQ)))

    qit = qit_ref[0, 0]
    wit = wit_ref[0, 0]

    def score_tile(j):
        kt = ki_ref[0, pl.ds(pl.multiple_of(j * TQ, TQ), TQ), :]
        rel = jnp.maximum(jnp.dot(kt, qit, preferred_element_type=f32), 0.0)
        sc = jnp.zeros((TQ, TQ), f32)
        for h in range(nih):
            sc = sc + wit[h:h + 1, :] * rel[:, h * TQ:(h + 1) * TQ]
        sc = jnp.where(j * TQ + krow <= t0 + qcol, sc, NEG)
        keys_ref[j] = jnp.where(sc == 0.0, 0.0, sc)

    def score_body(jj, carry):
        for u in range(4):
            score_tile(4 * jj + u)
        return carry

    lax.fori_loop(0, (n_pairs + 1) // 2, score_body, 0)

    def code_to_float(code):
        code = jnp.clip(code, -0x7F800001, 0x7F800000)
        bits = jnp.where(code < 0, code ^ 0x7FFFFFFF, code)
        bits = jnp.where((bits > 0) & (bits < 0x00800000), 0x00800000, bits)
        return pltpu.bitcast(bits, f32)

    def count(pred_fn):
        def body(jj, cnt):
            return (cnt + jnp.where(pred_fn(keys_ref[2 * jj]), 1.0, 0.0)
                    + jnp.where(pred_fn(keys_ref[2 * jj + 1]), 1.0, 0.0))
        cnt = lax.fori_loop(0, n_pairs, body, jnp.zeros((TQ, TQ), f32))
        return jnp.sum(cnt, axis=0, keepdims=True)

    def bits_body(state):
        t, prefix, n_ge = state
        for u in range(RADIX_BITS_PER_CHECK):
            cand = prefix ^ jnp.left_shift(jnp.int32(1), 31 - (t + u))
            cand_f = code_to_float(cand)
            cnt = count(lambda sj: sj >= cand_f)
            prefix = jnp.where(cnt >= topk, cand, prefix)
            n_ge = jnp.where(cnt >= topk, cnt, n_ge)
        return t + RADIX_BITS_PER_CHECK, prefix, n_ge

    n_all = (2 * n_pairs * TQ).astype(f32)
    _, code, n_ge = lax.while_loop(lambda st: (st[0] < 32) & (jnp.max(st[2]) > topk), bits_body,
                                   (jnp.int32(0), jnp.full((1, TQ), INT_MIN, i32), jnp.full((1, TQ), n_all, f32)))
    thr = code_to_float(code)

    @pl.when(jnp.max(n_ge) > topk)
    def _():
        need = topk - count(lambda sj: sj > thr)

        def fix_body(j, running):
            sj = keys_ref[j]
            tie = jnp.where(sj == thr, 1.0, 0.0)
            before = jnp.dot(tri_ref[...], tie.astype(bf16), preferred_element_type=f32) + running
            keys_ref[j] = jnp.where(before >= need, jnp.where(sj == thr, -jnp.inf, sj), sj)
            return running + jnp.sum(tie, axis=0, keepdims=True)

        lax.fori_loop(0, 2 * n_pairs, fix_body, jnp.zeros((1, TQ), f32))

    qs_ref[0:HEAD_DIM, :] = (qt_ref[0, 0].astype(f32) * (HEAD_DIM ** -0.5 * LOG2E)).astype(bf16)
    qs_ref[HEAD_DIM:K_COLS, :] = qx_ref[...]
    _masked_flash(i, qs_ref,
                  lambda j, c: k_ref[0, pl.ds(pl.multiple_of(j * TQ, TQ), TQ), :],
                  lambda j, c: vt_ref[0, j],
                  lambda j, c: jnp.where(keys_ref[j] >= thr, 0.0, NEG),
                  delta_ref, m_ref, acc_ref, s_ref, mt_ref, al_ref, chunk, True)
    acc = acc_ref[...]
    ot = acc[0:HEAD_DIM] / acc[HEAD_DIM:HEAD_DIM + 1]
    o_ref[0] = jnp.concatenate([ot[:, h * TQ:(h + 1) * TQ].T for h in range(nh)], axis=1).astype(o_ref.dtype)


def _dsa_attention(qt, k, v, qit, ki, wi, near, far):
    bn, nt, _, width = qt.shape
    s = nt * TQ
    nh = width // TQ
    hq = nh * HEAD_DIM
    nih = qit.shape[3] // TQ
    topk = min(C_TOPK_MAX, s // 4)
    wit = wi.reshape(bn, nt, TQ, nih).transpose(0, 1, 3, 2)
    tri = np.tril(np.ones((TQ, TQ), np.float32), -1)
    kern = functools.partial(_dsa_kernel, topk=topk, chunk=512)
    nt_scr = max(nt, -(-topk // TQ))
    return pl.pallas_call(
        kern,
        grid=(bn, nt),
        in_specs=[pl.BlockSpec((1, 1, HEAD_DIM, width), lambda b, i: (b, i, 0, 0)),
                  pl.BlockSpec((1, s, K_COLS), lambda b, i: (b, 0, 0)),
                  pl.BlockSpec((1, nt, V_ROWS, TQ), lambda b, i: (b, 0, 0, 0)),
                  pl.BlockSpec((1, 1, IDX_DIM, nih * TQ), lambda b, i: (b, i, 0, 0)),
                  pl.BlockSpec((1, s, IDX_DIM), lambda b, i: (b, 0, 0)),
                  pl.BlockSpec((1, 1, nih, TQ), lambda b, i: (b, i, 0, 0)),
                  pl.BlockSpec((K_COLS - HEAD_DIM, width), lambda b, i: (0, 0)),
                  pl.BlockSpec((2, TQ, width), lambda b, i: (0, 0, 0)),
                  pl.BlockSpec((TQ, TQ), lambda b, i: (0, 0))],
        out_specs=pl.BlockSpec((1, TQ, hq), lambda b, i: (b, i, 0)),
        out_shape=jax.ShapeDtypeStruct((bn, s, hq), bf16),
        scratch_shapes=[pltpu.VMEM((nt_scr, TQ, TQ), f32), pltpu.VMEM((K_COLS, width), bf16),
                        pltpu.VMEM((1, width), f32), pltpu.VMEM((V_ROWS, width), f32),
                        pltpu.VMEM((2, TQ, width), f32), pltpu.VMEM((2, 1, width), f32),
                        pltpu.VMEM((2, 1, width), f32)],
        compiler_params=_cparams(("parallel", "arbitrary")),
        name="dsa_attention",
    )(qt, k, v, qit, ki, wit, *_flash_operands(near, far), jnp.asarray(tri, bf16))


def _heads_first(a, g):
    bn, s, _ = a.shape
    return a.reshape(bn, s, g, HEAD_DIM).transpose(0, 2, 1, 3)


def _even_layer(x, table_t, w_in, sinks, cmpk_pos, cmpk_w1, cmpk_w2, cmpv_pos, cmpv_w1, cmpv_w2,
                w_out, ln1_g, ln1_b, ffn_gate, ffn_up, ffn_down, ln2_g, ln2_b):
    bn, s, d = x.shape
    xf = x.reshape(bn * s, d)
    a_q, a_kv, b_q, b_kv = A_HEADS * HEAD_DIM, A_KV * HEAD_DIM, B_HEADS * HEAD_DIM, B_KV * HEAD_DIM
    n_main = a_q + 2 * a_kv + b_q + 6 * b_kv
    rb = B_HEADS // B_KV
    w_tail = jnp.pad(w_in[:, n_main:], ((0, 0), (0, 128 - 3 * B_HEADS))).astype(bf16)
    cols = np.cumsum([0, a_q, a_kv, a_kv, b_q] + [b_kv] * 6)
    w_kv = jnp.concatenate([w_in[:, cols[1]:cols[3]], w_in[:, cols[4]:cols[6]], w_in[:, cols[8]:cols[10]]],
                           axis=1).astype(bf16)
    main, tail, qt_a, qt_b, ks1, vst = _proj_in(
        x, [w_kv, w_tail, w_in[:, cols[0]:cols[1]].T.astype(bf16), w_in[:, cols[3]:cols[4]].T.astype(bf16),
            w_in[:, cols[6]:cols[7]].astype(bf16), w_in[:, cols[7]:cols[8]].T.astype(bf16)],
        [("plain", w_kv.shape[1], bf16, 0), ("plain", 128, f32, 0),
         ("queries", A_KV, A_HEADS // A_KV, HEAD_DIM), ("queries", B_KV, rb, HEAD_DIM),
         ("keys", B_KV, 1, HEAD_DIM), ("values", B_KV, 1, HEAD_DIM)])
    main = main.reshape(bn, s, w_kv.shape[1])
    gates = tail[:, :3 * B_HEADS]
    gates = gates.reshape(bn, s, 3, B_KV, rb).transpose(0, 3, 1, 2, 4).reshape(bn, B_KV, s, 3 * rb)
    kcols = np.cumsum([0, a_kv, a_kv] + [b_kv] * 4)
    ka, va, kc_in, vc_in, kwn, vwn = [main[:, :, kcols[t]:kcols[t + 1]] for t in range(6)]

    near_idx = _bucket_of(np.arange(TQ)[:, None] + TQ - np.arange(2 * TQ)[None, :])
    near = _bias_expand(table_t, near_idx, TQ)
    pad_b = -(-(B_WINDOW - 1) // TQ) * TQ
    win_idx = _bucket_of(np.arange(TQ)[:, None] + pad_b - np.arange(pad_b + TQ)[None, :])
    bias_win = _bias_expand(table_t[A_HEADS:A_HEADS + B_HEADS], win_idx, TQ)
    nc = s // CMP_STRIDE
    nt = s // TQ
    cmp_idx = _bucket_of((np.arange(nt)[:, None, None] * TQ + np.arange(TQ)[None, None, :])
                         - (np.arange(nc)[None, :, None] * CMP_STRIDE + CMP_BLOCK - 1)).reshape(nt * nc, TQ)
    bias_c = _bias_expand(table_t[A_HEADS:A_HEADS + B_HEADS], cmp_idx, min(nt * nc, 512))
    bias_c = bias_c.reshape(B_HEADS, nt, nc, TQ)
    far = table_t[:, NUM_BUCKETS - 1]

    out_a =_banded_gqa(qt_a, _heads_first(ka, A_KV), _heads_first(va, A_KV), near[:A_HEADS], sinks, A_WINDOW, bf16)
    o_win = _banded_gqa(qt_b, _heads_first(kwn, B_KV), _heads_first(vwn, B_KV), bias_win, None, B_WINDOW, f32)
    kc = _nsa_compress(_heads_first(kc_in, B_KV), cmpk_pos, cmpk_w1, cmpk_w2)
    vc = _nsa_compress(_heads_first(vc_in, B_KV), cmpv_pos, cmpv_w1, cmpv_w2)
    out_b = _nsa_mix(qt_b, kc, vc, bias_c, ks1, vst,
                     near[A_HEADS:A_HEADS + B_HEADS], far[A_HEADS:A_HEADS + B_HEADS], gates, o_win)
    mix = jnp.concatenate([out_a, out_b], axis=-1).reshape(bn * s, a_q + b_q)
    x2 = _proj_ffn_res_ln(mix, w_out.astype(bf16), xf, ln1_g, ln1_b, ffn_gate.astype(bf16), ffn_up.astype(bf16),
                          ffn_down.astype(bf16), ln2_g, ln2_b)
    return x2.reshape(bn, s, d), near, far


def _odd_layer(x, near, far, w_in, w_out, ln1_g, ln1_b, router, exp_gate, exp_up, exp_down, ln2_g, ln2_b):
    bn, s, d = x.shape
    xf = x.reshape(bn * s, d)
    c_q = C_HEADS * HEAD_DIM
    n_main = c_q + 2 * HEAD_DIM + IDX_HEADS * IDX_DIM
    w_tail = jnp.pad(w_in[:, n_main:], ((0, 0), (0, 128 - IDX_DIM - IDX_HEADS))).astype(bf16)
    tail, qt, qit, k1, vt = _proj_in(
        x, [w_tail, w_in[:, :c_q].T.astype(bf16), w_in[:, c_q + 2 * HEAD_DIM:n_main].T.astype(bf16),
            w_in[:, c_q:c_q + HEAD_DIM].astype(bf16), w_in[:, c_q + HEAD_DIM:c_q + 2 * HEAD_DIM].T.astype(bf16)],
        [("plain", 128, f32, 0), ("queries", 1, C_HEADS, HEAD_DIM), ("queries", 1, IDX_HEADS, IDX_DIM),
         ("keys", 1, 1, HEAD_DIM), ("values", 1, 1, HEAD_DIM)])
    tail = tail.reshape(bn, s, 128)
    ki = tail[:, :, :IDX_DIM].astype(bf16)
    wi = tail[:, :, IDX_DIM:IDX_DIM + IDX_HEADS]
    mix = _dsa_attention(qt[:, 0], k1[:, 0], vt[:, 0], qit[:, 0], ki, wi, near[:C_HEADS], far[:C_HEADS])
    mix = mix.reshape(bn * s, c_q)
    x1 = _proj_res_ln(mix, w_out.astype(bf16), xf, ln1_g, ln1_b)
    x2 = _moe_res_ln(x1, router, exp_gate.astype(bf16), exp_up.astype(bf16), exp_down.astype(bf16), ln2_g, ln2_b)
    return x2.reshape(bn, s, d)


def kernel(x, rel_bias, l0_w_in, l0_sinks, l0_cmpk_pos, l0_cmpk_w1, l0_cmpk_w2, l0_cmpv_pos, l0_cmpv_w1,
           l0_cmpv_w2, l0_w_out, l0_ln1_g, l0_ln1_b, l0_ffn_gate, l0_ffn_up, l0_ffn_down, l0_ln2_g, l0_ln2_b,
           l1_w_in, l1_w_out, l1_ln1_g, l1_ln1_b, l1_router, l1_exp_gate, l1_exp_up, l1_exp_down, l1_ln2_g,
           l1_ln2_b):
    table_t = rel_bias.T.astype(f32)
    x, near, far = _even_layer(x, table_t, l0_w_in, l0_sinks, l0_cmpk_pos, l0_cmpk_w1, l0_cmpk_w2, l0_cmpv_pos,
                               l0_cmpv_w1, l0_cmpv_w2, l0_w_out, l0_ln1_g, l0_ln1_b, l0_ffn_gate, l0_ffn_up,
                               l0_ffn_down, l0_ln2_g, l0_ln2_b)
    return _odd_layer(x, near, far, l1_w_in, l1_w_out, l1_ln1_g, l1_ln1_b, l1_router, l1_exp_gate, l1_exp_up,
                      l1_exp_down, l1_ln2_g, l1_ln2_b)
```

```python
import functools
import math

import numpy as np
import jax
import jax.numpy as jnp
from jax import lax
from jax.experimental import pallas as pl
from jax.experimental.pallas import tpu as pltpu

f32 = jnp.float32
bf16 = jnp.bfloat16
i32 = jnp.int32

D_MODEL = 1024
HEAD_DIM = 64
NUM_BUCKETS = 32
MAX_DISTANCE = 128
A_HEADS, A_KV, A_WINDOW = 8, 2, 128
B_HEADS, B_KV, B_WINDOW = 8, 2, 512
CMP_BLOCK, CMP_STRIDE = 32, 16
SLC_BLOCK, SLC_TOPN = 64, 16
C_HEADS, IDX_HEADS, IDX_DIM, C_TOPK_MAX = 16, 8, 32, 256
D_FF, N_EXPERTS = 2816, 8
DEPTH = 2
ALPHA = (2.0 * DEPTH) ** 0.25
LN_EPS = 1e-5
NEG = -1e30
POS_BIG = 1e30
INT_MIN = -(2 ** 31)

TQ = 128
FF_CHUNK = 256
VMEM_LIMIT = 56 * 1024 * 1024


def _cparams(sem):
    return pltpu.CompilerParams(dimension_semantics=sem, vmem_limit_bytes=VMEM_LIMIT)


def _bucket_table(max_dist):
    n = np.arange(max_dist + 1)
    max_exact = NUM_BUCKETS // 2
    nf = np.maximum(n, 1).astype(np.float64)
    large = max_exact + (np.log(nf / max_exact) / math.log(MAX_DISTANCE / max_exact)
                         * (NUM_BUCKETS - max_exact)).astype(np.int64)
    large = np.minimum(large, NUM_BUCKETS - 1)
    return np.where(n < max_exact, n, large).astype(np.int32)


def _bucket_of(dist):
    dist = np.maximum(dist, 0)
    return _bucket_table(int(dist.max()))[dist]


def _proj_in_kernel(*refs, groups):
    ng = len(groups)
    x_ref = refs[0]
    wq_refs = refs[1:1 + ng]
    oq_refs = refs[1 + ng:]
    xb = x_ref[...].astype(bf16)
    tm = xb.shape[0]
    for wq_ref, oq_ref, (kind, gb, hb, dh) in zip(wq_refs, oq_refs, groups):
        if kind == "plain":
            oq_ref[...] = jnp.dot(xb, wq_ref[...], preferred_element_type=f32).astype(oq_ref.dtype)
            continue
        if kind == "keys":
            kk = jnp.dot(xb, wq_ref[...], preferred_element_type=f32).astype(bf16)
            ones = jnp.where(lax.broadcasted_iota(i32, (tm, K_COLS - dh), 1) < 2, 1.0, 0.0).astype(bf16)
            for g in range(gb):
                oq_ref[0, g, :, 0:dh] = kk[:, g * dh:(g + 1) * dh]
                oq_ref[0, g, :, dh:K_COLS] = ones
            continue
        qt = lax.dot_general(wq_ref[...], xb, (((1,), (1,)), ((), ())), preferred_element_type=f32).astype(bf16)
        ones = jnp.where(lax.broadcasted_iota(i32, (V_ROWS - HEAD_DIM, TQ), 0) < 1, 1.0, 0.0).astype(bf16)
        for g in range(gb):
            for h in range(hb):
                rows = slice((g * hb + h) * dh, (g * hb + h + 1) * dh)
                for t in range(tm // TQ):
                    oq_ref[0, g, t, 0:dh, h * TQ:(h + 1) * TQ] = qt[rows, t * TQ:(t + 1) * TQ]
                    if kind == "values":
                        oq_ref[0, g, t, dh:V_ROWS, :] = ones


def _proj_in(x3, w_groups, groups, tm=512):
    bn, s, k = x3.shape
    m = bn * s
    tm = min(tm, s)
    per_b = s // tm
    tiles = tm // TQ
    q_specs, q_shapes = [], []
    for kind, gb, hb, dh in groups:
        if kind == "plain":
            q_specs.append(pl.BlockSpec((tm, gb), lambda i: (i, 0)))
            q_shapes.append(jax.ShapeDtypeStruct((m, gb), hb))
        elif kind == "keys":
            q_specs.append(pl.BlockSpec((1, gb, tm, K_COLS), lambda i: (i // per_b, 0, i % per_b, 0)))
            q_shapes.append(jax.ShapeDtypeStruct((bn, gb, s, K_COLS), bf16))
        else:
            rows = V_ROWS if kind == "values" else dh
            q_specs.append(pl.BlockSpec((1, gb, tiles, rows, hb * TQ), lambda i: (i // per_b, 0, i % per_b, 0, 0)))
            q_shapes.append(jax.ShapeDtypeStruct((bn, gb, s // TQ, rows, hb * TQ), bf16))
    return pl.pallas_call(
        functools.partial(_proj_in_kernel, groups=tuple(groups)),
        grid=(m // tm,),
        in_specs=[pl.BlockSpec((tm, k), lambda i: (i, 0))]
                 + [pl.BlockSpec(w.shape, lambda i: (0, 0)) for w in w_groups],
        out_specs=q_specs,
        out_shape=q_shapes,
        compiler_params=_cparams(("parallel",)),
        name="proj_in",
    )(x3.reshape(m, k), *w_groups)


def _bias_expand_kernel(tab_ref, idx_ref, o_ref):
    h = pl.program_id(0)
    idx = idx_ref[...]
    acc = jnp.zeros(idx.shape, f32)
    for b in range(NUM_BUCKETS):
        acc = jnp.where(idx == b, tab_ref[h, b], acc)
    o_ref[0] = acc


def _bias_expand(table_t, idx, tr):
    nh = table_t.shape[0]
    r, c = idx.shape
    return pl.pallas_call(
        _bias_expand_kernel,
        grid=(nh, r // tr),
        in_specs=[pl.BlockSpec(memory_space=pltpu.SMEM),
                  pl.BlockSpec((tr, c), lambda h, i: (i, 0))],
        out_specs=pl.BlockSpec((1, tr, c), lambda h, i: (h, i, 0)),
        out_shape=jax.ShapeDtypeStruct((nh, r, c), f32),
        compiler_params=_cparams(("parallel", "arbitrary")),
        name="bias_expand",
    )(table_t, jnp.asarray(idx, dtype=i32))


def _layer_norm(z, g, b):
    mu = jnp.mean(z, axis=-1, keepdims=True)
    zc = z - mu
    var = jnp.mean(zc * zc, axis=-1, keepdims=True)
    return zc * lax.rsqrt(var + LN_EPS) * g + b


def _proj_ln_kernel(mix_ref, w_ref, x_ref, g_ref, b_ref, o_ref):
    y = jnp.dot(mix_ref[...], w_ref[...], preferred_element_type=f32)
    o_ref[...] = _layer_norm(ALPHA * x_ref[...] + y, g_ref[...], b_ref[...])


def _proj_res_ln(mix, w, x, g, b, tm=512):
    m, k = mix.shape
    d = w.shape[1]
    return pl.pallas_call(
        _proj_ln_kernel,
        grid=(m // tm,),
        in_specs=[pl.BlockSpec((tm, k), lambda i: (i, 0)),
                  pl.BlockSpec((k, d), lambda i: (0, 0)),
                  pl.BlockSpec((tm, d), lambda i: (i, 0)),
                  pl.BlockSpec((1, d), lambda i: (0, 0)),
                  pl.BlockSpec((1, d), lambda i: (0, 0))],
        out_specs=pl.BlockSpec((tm, d), lambda i: (i, 0)),
        out_shape=jax.ShapeDtypeStruct((m, d), f32),
        compiler_params=_cparams(("parallel",)),
        name="proj_res_ln",
    )(mix, w, x, g.reshape(1, d), b.reshape(1, d))


def _swiglu_accumulate(xb, wg_ref, wu_ref, wd_ref, acc_ref, widx):
    d_ff = wg_ref.shape[-1]
    for c in range(d_ff // FF_CHUNK):
        cols = slice(c * FF_CHUNK, (c + 1) * FF_CHUNK)
        gate = jnp.dot(xb, wg_ref[widx + (slice(None), cols)], preferred_element_type=f32)
        up = jnp.dot(xb, wu_ref[widx + (slice(None), cols)], preferred_element_type=f32)
        h = jax.nn.silu(gate) * up
        acc_ref[...] += jnp.dot(h.astype(bf16), wd_ref[widx + (cols, slice(None))], preferred_element_type=f32)


def _proj_ffn_kernel(mix_ref, wo_ref, x_ref, g1_ref, b1_ref, wg_ref, wu_ref, wd_ref, g2_ref, b2_ref, o_ref, acc_ref):
    y = jnp.dot(mix_ref[...], wo_ref[...], preferred_element_type=f32)
    o_ref[...] = _layer_norm(ALPHA * x_ref[...] + y, g1_ref[...], b1_ref[...])
    acc_ref[...] = jnp.zeros_like(acc_ref)
    _swiglu_accumulate(o_ref[...].astype(bf16), wg_ref, wu_ref, wd_ref, acc_ref, ())
    o_ref[...] = _layer_norm(ALPHA * o_ref[...] + acc_ref[...], g2_ref[...], b2_ref[...])


def _proj_ffn_res_ln(mix, w_out, x, g1, b1, wg, wu, wd, g2, b2, tm=512):
    m, k = mix.shape
    d = x.shape[1]
    d_ff = wg.shape[1]
    single = pl.Buffered(1)
    row = lambda i: (i, 0)
    fixed = lambda i: (0, 0)
    return pl.pallas_call(
        _proj_ffn_kernel,
        grid=(m // tm,),
        in_specs=[pl.BlockSpec((tm, k), row),
                  pl.BlockSpec((k, d), fixed, pipeline_mode=single),
                  pl.BlockSpec((tm, d), row),
                  pl.BlockSpec((1, d), fixed),
                  pl.BlockSpec((1, d), fixed),
                  pl.BlockSpec((d, d_ff), fixed, pipeline_mode=single),
                  pl.BlockSpec((d, d_ff), fixed, pipeline_mode=single),
                  pl.BlockSpec((d_ff, d), fixed, pipeline_mode=single),
                  pl.BlockSpec((1, d), fixed),
                  pl.BlockSpec((1, d), fixed)],
        out_specs=pl.BlockSpec((tm, d), row),
        out_shape=jax.ShapeDtypeStruct((m, d), f32),
        scratch_shapes=[pltpu.VMEM((tm, d), f32)],
        compiler_params=_cparams(("parallel",)),
        name="proj_ffn_res_ln",
    )(mix, w_out, x, g1.reshape(1, d), b1.reshape(1, d), wg, wu, wd, g2.reshape(1, d), b2.reshape(1, d))


MOE_CAP = 288


def _router_gates(x, r_ref, lane):
    logits = jnp.dot(x, r_ref[...], preferred_element_type=f32)
    logits = jnp.where(lane < N_EXPERTS, logits, -jnp.inf)
    m1 = jnp.max(logits, axis=-1, keepdims=True)
    i1 = jnp.min(jnp.where(logits == m1, lane, 128), axis=-1, keepdims=True)
    rest = jnp.where(lane == i1, -jnp.inf, logits)
    m2 = jnp.max(rest, axis=-1, keepdims=True)
    i2 = jnp.min(jnp.where(rest == m2, lane, 128), axis=-1, keepdims=True)
    e2 = jnp.exp(m2 - m1)
    g1 = 1.0 / (1.0 + e2)
    return jnp.where(lane == i1, g1, 0.0) + jnp.where(lane == i2, e2 * g1, 0.0)


def _moe_kernel(x_ref, r_ref, tri_ref, wg_ref, wu_ref, wd_ref, g_ref, b_ref, o_ref,
                split_ref, post_ref, yc_ref):
    e = pl.program_id(1)
    tm = x_ref.shape[0]

    @pl.when(e == 0)
    def _():
        comb = _router_gates(x_ref[...], r_ref, lax.broadcasted_iota(i32, (tm, 128), 1))
        hi = comb.astype(bf16)
        rest = comb - hi.astype(f32)
        mid = rest.astype(bf16)
        split_ref[0] = hi
        split_ref[1] = mid
        split_ref[2] = (rest - mid.astype(f32)).astype(bf16)
        chosen = jnp.where(comb > 0.0, 1.0, 0.0)
        pos = jnp.dot(tri_ref[...], chosen.astype(bf16), preferred_element_type=f32)
        pos = jnp.where(comb > 0.0, pos, -1.0)
        for t in range(tm // 128):
            rows = slice(t * 128, (t + 1) * 128)
            post_ref[:, rows] = pos[rows].T[0:N_EXPERTS]
        hi32 = jnp.floor((pos + 1.0) * (1.0 / 32.0))
        split_ref[3] = hi32.astype(bf16)
        split_ref[4] = (pos + 1.0 - 32.0 * hi32).astype(bf16)
        o_ref[...] = jnp.zeros_like(o_ref)

    pos_row = post_ref[pl.ds(e, 1), :].astype(i32)
    n_tok = jnp.sum(jnp.where(pos_row >= 0, 1.0, 0.0)).astype(i32)
    pick = jnp.where(lax.broadcasted_iota(i32, (128, 128), 0) == e, 1.0, 0.0).astype(bf16)

    def column(t):
        return jnp.dot(split_ref[t], pick, preferred_element_type=f32)

    gate_b = column(0) + column(1) + column(2)
    pos_b = (32.0 * column(3) + column(4) - 1.0).astype(i32)
    lane = lax.broadcasted_iota(i32, (tm, 128), 1)

    def pass_body(p, carry):
        base = p * MOE_CAP
        gather = jnp.where(pos_row == base + lax.broadcasted_iota(i32, (MOE_CAP, tm), 0), 1.0, 0.0).astype(bf16)
        xc = jnp.dot(gather, x_ref[...].astype(bf16), preferred_element_type=f32).astype(bf16)
        yc_ref[...] = jnp.zeros_like(yc_ref)
        _swiglu_accumulate(xc, wg_ref, wu_ref, wd_ref, yc_ref, (0,))
        scatter = jnp.concatenate([jnp.where(pos_b == base + t * 128 + lane, 1.0, 0.0).astype(bf16)
                                   for t in range(-(-MOE_CAP // 128))], axis=1)[:, :MOE_CAP]
        y = jnp.dot(scatter, yc_ref[...].astype(bf16), preferred_element_type=f32)
        o_ref[...] += jnp.concatenate([gate_b] * (y.shape[1] // 128), axis=1) * y
        return carry

    lax.fori_loop(0, (n_tok + MOE_CAP - 1) // MOE_CAP, pass_body, 0)

    @pl.when(e == N_EXPERTS - 1)
    def _():
        o_ref[...] = _layer_norm(ALPHA * x_ref[...] + o_ref[...], g_ref[...], b_ref[...])


def _moe_res_ln(x, router, wg, wu, wd, g, b, tm=1024):
    m, d = x.shape
    ne, _, d_ff = wg.shape
    tm = min(tm, m)
    router_p = jnp.pad(router, ((0, 0), (0, 128 - ne)))
    tri = jnp.asarray(np.tril(np.ones((tm, tm), np.float32), -1), bf16)
    single = pl.Buffered(1)
    return pl.pallas_call(
        _moe_kernel,
        grid=(m // tm, ne),
        in_specs=[pl.BlockSpec((tm, d), lambda i, e: (i, 0), pipeline_mode=single),
                  pl.BlockSpec((d, 128), lambda i, e: (0, 0), pipeline_mode=single),
                  pl.BlockSpec((tm, tm), lambda i, e: (0, 0), pipeline_mode=single),
                  pl.BlockSpec((1, d, d_ff), lambda i, e: (e, 0, 0)),
                  pl.BlockSpec((1, d, d_ff), lambda i, e: (e, 0, 0)),
                  pl.BlockSpec((1, d_ff, d), lambda i, e: (e, 0, 0)),
                  pl.BlockSpec((1, d), lambda i, e: (0, 0)),
                  pl.BlockSpec((1, d), lambda i, e: (0, 0))],
        out_specs=pl.BlockSpec((tm, d), lambda i, e: (i, 0)),
        out_shape=jax.ShapeDtypeStruct((m, d), f32),
        scratch_shapes=[pltpu.VMEM((5, tm, 128), bf16), pltpu.VMEM((N_EXPERTS, tm), f32),
                        pltpu.VMEM((MOE_CAP, d), f32)],
        compiler_params=_cparams(("parallel", "arbitrary")),
        name="moe_res_ln",
    )(x, router_p, tri, wg, wu, wd, g.reshape(1, d), b.reshape(1, d))


def _compress_kernel(blk_ref, pos_ref, w1_ref, w2_ref, o_ref):
    xb = (blk_ref[0].astype(f32) + pos_ref[...]).astype(bf16)
    h = jax.nn.gelu(jnp.dot(xb, w1_ref[...], preferred_element_type=f32))
    o_ref[0] = jnp.dot(h.astype(bf16), w2_ref[...], preferred_element_type=f32).astype(o_ref.dtype)


def _nsa_compress(kv, pos, w1, w2):
    bn, g, s, dh = kv.shape
    nc = s // CMP_STRIDE
    chunks = kv.reshape(bn, g, nc, CMP_STRIDE * dh)
    nxt = jnp.pad(chunks[:, :, 1:], ((0, 0), (0, 0), (0, 1), (0, 0)))
    blocks = jnp.concatenate([chunks, nxt], axis=-1).reshape(bn * g, nc, CMP_BLOCK * dh)
    cin = CMP_BLOCK * dh
    hid = w1.shape[1]
    out = pl.pallas_call(
        _compress_kernel,
        grid=(bn * g,),
        in_specs=[pl.BlockSpec((1, nc, cin), lambda i: (i, 0, 0)),
                  pl.BlockSpec((1, cin), lambda i: (0, 0)),
                  pl.BlockSpec((cin, hid), lambda i: (0, 0)),
                  pl.BlockSpec((hid, dh), lambda i: (0, 0))],
        out_specs=pl.BlockSpec((1, nc, dh), lambda i: (i, 0, 0)),
        out_shape=jax.ShapeDtypeStruct((bn * g, nc, dh), bf16),
        compiler_params=_cparams(("parallel",)),
        name="nsa_compress",
    )(blocks, pos.reshape(1, cin), w1.astype(bf16), w2.astype(bf16))
    return out.reshape(bn, g, nc, dh)


V_ROWS = HEAD_DIM + 16


LOG2E = 1.4426950408889634
K_COLS = 2 * HEAD_DIM


def _masked_flash(i, qs_ref, key_tile, value_tile, mask_bias, delta_ref, m_ref, acc_ref, s_ref, mt_ref, al_ref, chunk,
                  shared_keys):
    width = qs_ref.shape[1]
    n_chunks = width // chunk
    m_ref[...] = jnp.full(m_ref.shape, NEG, f32)
    acc_ref[...] = jnp.zeros(acc_ref.shape, f32)

    def scores_chunk(j, delta, c, shared, slot=None):
        slot = j % 2 if slot is None else slot
        if c == 0 or not shared_keys:
            shared["kt"] = key_tile(j, c)
            shared["maskb"] = jnp.concatenate([mask_bias(j, c)] * (chunk // TQ), axis=1)
        cols = slice(c * chunk, (c + 1) * chunk)
        s = jnp.dot(shared["kt"], qs_ref[:, cols], preferred_element_type=f32) + shared["maskb"]
        if delta is not None:
            s = s + delta_ref[delta, :, cols]
        s_ref[slot, :, cols] = s
        m_old = m_ref[:, cols]
        m_new = jnp.maximum(m_old, jnp.max(s, axis=0, keepdims=True))
        m_ref[:, cols] = m_new
        mt_ref[slot, :, cols] = m_new
        al_ref[slot, :, cols] = jnp.exp2(m_old - m_new)

    def values_chunk(j, c, shared, slot=None):
        slot = j % 2 if slot is None else slot
        if c == 0 or not shared_keys:
            shared["vt"] = value_tile(j, c)
        cols = slice(c * chunk, (c + 1) * chunk)
        p = jnp.exp2(s_ref[slot, :, cols] - mt_ref[slot, :, cols])
        acc_ref[:, cols] = (al_ref[slot, :, cols] * acc_ref[:, cols]
                            + jnp.dot(shared["vt"], p.astype(bf16), preferred_element_type=f32))

    def scores(j, delta):
        shared = {}
        for c in range(n_chunks):
            scores_chunk(j, delta, c, shared)

    def values(j):
        shared = {}
        for c in range(n_chunks):
            values_chunk(j, c, shared)

    def values_then_scores(jv, js, delta, slot_v=None, slot_s=None):
        sv, ss = {}, {}
        for c in range(n_chunks):
            values_chunk(jv, c, sv, slot_v)
            scores_chunk(js, delta, c, ss, slot_s)

    @pl.when(i >= 2)
    def _():
        scores(0, None)

        def quad_body(k, carry):
            j = 4 * k + 1
            values_then_scores(j - 1, j, None, 0, 1)
            values_then_scores(j, j + 1, None, 1, 0)
            values_then_scores(j + 1, j + 2, None, 0, 1)
            values_then_scores(j + 2, j + 3, None, 1, 0)
            return carry

        n_quads = (i - 2) // 4
        lax.fori_loop(0, n_quads, quad_body, 0)

        def single_body(j, carry):
            values_then_scores(j - 1, j, None)
            return carry

        lax.fori_loop(4 * n_quads + 1, i - 1, single_body, 0)

        values_then_scores(i - 2, i - 1, 0)
        values_then_scores(i - 1, i, 1)

    @pl.when(i == 1)
    def _():
        scores(0, 0)
        values_then_scores(0, 1, 1)

    @pl.when(i == 0)
    def _():
        scores(0, 1)

    values(i)


def _flash_operands(near, far):
    nh = near.shape[0]
    width = nh * TQ
    far2 = jnp.repeat(far * LOG2E, TQ).reshape(1, width)
    hi = far2.astype(bf16)
    lo = (far2 - hi.astype(f32)).astype(bf16)
    extra = jnp.concatenate([hi, lo, jnp.zeros((K_COLS - HEAD_DIM - 2, width), bf16)], axis=0)
    neart = near.reshape(nh, TQ, 2, TQ).transpose(2, 3, 0, 1).reshape(2, TQ, width) * LOG2E
    neart = neart - (hi.astype(f32) + lo.astype(f32))
    causal = np.where(np.arange(TQ)[:, None] <= np.arange(TQ)[None, :], 0.0, NEG).astype(np.float32)
    return extra, jnp.stack([neart[0], neart[1] + jnp.asarray(np.tile(causal, (1, nh)))], axis=0)


def _value_tiles(v):
    lead = v.shape[:-2]
    nt = v.shape[-2] // TQ
    vt = jnp.swapaxes(v.reshape(lead + (nt, TQ, HEAD_DIM)), -1, -2)
    return jnp.concatenate([vt, jnp.ones(lead + (nt, 1, TQ), vt.dtype),
                            jnp.zeros(lead + (nt, V_ROWS - HEAD_DIM - 1, TQ), vt.dtype)], axis=-2)


BANDED_TILES = 4


def _banded_kernel(qt_ref, k_ref, vt_ref, bias_ref, neg_ref, sink_ref, o_ref, *, rep, kt_tiles, use_sinks):
    kw = kt_tiles * TQ
    for u in range(qt_ref.shape[2]):
        i = pl.program_id(1) * qt_ref.shape[2] + u
        outs = []
        for g in range(qt_ref.shape[1]):
            qs = jnp.concatenate([(qt_ref[0, g, u].astype(f32) * (HEAD_DIM ** -0.5)).astype(bf16), neg_ref[...]],
                                 axis=0)
            k = k_ref[0, g, pl.ds(pl.multiple_of(i * TQ, TQ), kw), :]
            s = jnp.dot(k, qs, preferred_element_type=f32) + bias_ref[g]
            m = jnp.max(s, axis=0, keepdims=True)
            if use_sinks:
                m = jnp.maximum(m, sink_ref[g])
            p = jnp.exp(s - m).astype(bf16)
            vt = jnp.concatenate([vt_ref[0, g, i + t] for t in range(kt_tiles)], axis=1)
            acc = jnp.dot(vt, p, preferred_element_type=f32)
            l = acc[HEAD_DIM:HEAD_DIM + 1]
            if use_sinks:
                l = l + jnp.exp(sink_ref[g] - m)
            ot = acc[0:HEAD_DIM] / l
            outs += [ot[:, r * TQ:(r + 1) * TQ].T for r in range(rep)]
        o_ref[0, u * TQ:(u + 1) * TQ, :] = jnp.concatenate(outs, axis=1).astype(o_ref.dtype)


def _banded_gqa(qt, k, v, bias, sinks, window, out_dtype):
    bn, g, nt, _, width = qt.shape
    rep = width // TQ
    s = nt * TQ
    pad = -(-(window - 1) // TQ) * TQ
    kw = pad + TQ
    kd = 2 * HEAD_DIM
    flag = jnp.broadcast_to((jnp.arange(pad + s) < pad).astype(k.dtype)[:, None], (bn, g, pad + s, 1))
    kp = jnp.concatenate([jnp.pad(k, ((0, 0), (0, 0), (pad, 0), (0, 0))), flag,
                          jnp.zeros((bn, g, pad + s, kd - HEAD_DIM - 1), k.dtype)], axis=-1)
    vt = _value_tiles(jnp.pad(v, ((0, 0), (0, 0), (pad, 0), (0, 0))))
    neg = np.zeros((kd - HEAD_DIM, width), np.float32)
    neg[0] = NEG
    dist = np.arange(TQ)[:, None] + pad - np.arange(kw)[None, :]
    band = jnp.asarray((dist >= 0) & (dist < window))
    bias_t = jnp.where(band, bias, NEG).reshape(g, rep, TQ, kw).transpose(0, 3, 1, 2).reshape(g, kw, width)
    use_sinks = sinks is not None
    sink_rows = (jnp.repeat(sinks.astype(f32), TQ) if use_sinks else jnp.zeros((g * width,), f32)).reshape(g, 1, width)
    kern = functools.partial(_banded_kernel, rep=rep, kt_tiles=kw // TQ, use_sinks=use_sinks)
    rw = rep * HEAD_DIM
    return pl.pallas_call(
        kern,
        grid=(bn, nt // BANDED_TILES),
        in_specs=[pl.BlockSpec((1, g, BANDED_TILES, HEAD_DIM, width), lambda b, i: (b, 0, i, 0, 0)),
                  pl.BlockSpec((1, g, pad + s, kd), lambda b, i: (b, 0, 0, 0)),
                  pl.BlockSpec((1, g, (pad + s) // TQ, V_ROWS, TQ), lambda b, i: (b, 0, 0, 0, 0)),
                  pl.BlockSpec((g, kw, width), lambda b, i: (0, 0, 0)),
                  pl.BlockSpec((kd - HEAD_DIM, width), lambda b, i: (0, 0)),
                  pl.BlockSpec((g, 1, width), lambda b, i: (0, 0, 0))],
        out_specs=pl.BlockSpec((1, BANDED_TILES * TQ, g * rw), lambda b, i: (b, i, 0)),
        out_shape=jax.ShapeDtypeStruct((bn, s, g * rw), out_dtype),
        compiler_params=_cparams(("parallel", "arbitrary")),
        name="banded_gqa_w%d" % window,
    )(qt, kp, vt, bias_t, jnp.asarray(neg, bf16), sink_rows)


def _nsa_kernel(qt_ref, kc_ref, vct_ref, biasc_ref, ovt_ref, ks_ref, vst_ref, qx_ref, delta_ref, gt_ref, gn_ref,
                owin_ref, o_ref, qs_ref, selb_ref, m_ref, acc_ref, s_ref, mt_ref, al_ref, *, rep, n_sel):
    i = pl.program_id(1)
    t0 = i * TQ
    groups = qt_ref.shape[1]
    nc = kc_ref.shape[2]
    nblk = ovt_ref.shape[0]
    width = rep * TQ
    cmp_end = lax.broadcasted_iota(i32, (nc, TQ), 0) * CMP_STRIDE + (CMP_BLOCK - 1)
    valid_c = t0 + lax.broadcasted_iota(i32, (nc, TQ), 1) >= cmp_end
    blk = lax.broadcasted_iota(i32, (nblk, TQ), 0)
    tq = t0 + lax.broadcasted_iota(i32, (nblk, TQ), 1)
    cur = tq // SLC_BLOCK
    forced = (blk == 0) | (blk == cur) | (blk == cur - 1)
    admitted = blk * SLC_BLOCK <= tq

    o_cmps = []
    for g in range(groups):
        qf = qt_ref[0, g, 0].astype(f32)
        qs = (qf * (HEAD_DIM ** -0.5)).astype(bf16)
        qs_ref[0:HEAD_DIM, g * width:(g + 1) * width] = (qf * (HEAD_DIM ** -0.5 * LOG2E)).astype(bf16)

        s_c = jnp.dot(kc_ref[0, g], qs, preferred_element_type=f32)
        p_cols = []
        p_sum = jnp.zeros((nc, TQ), f32)
        for r in range(rep):
            sr = jnp.where(valid_c, s_c[:, r * TQ:(r + 1) * TQ] + biasc_ref[g * rep + r, 0], NEG)
            m = jnp.max(sr, axis=0, keepdims=True)
            e = jnp.where(valid_c, jnp.exp(sr - m), 0.0)
            l = jnp.sum(e, axis=0, keepdims=True)
            p = e * (1.0 / jnp.maximum(l, 1e-30))
            p_cols.append(p.astype(bf16))
            p_sum = p_sum + p
        o_cmps.append(jnp.dot(vct_ref[0, g], jnp.concatenate(p_cols, axis=1), preferred_element_type=f32))

        imp = jnp.dot(ovt_ref[...], p_sum, preferred_element_type=f32)
        score = jnp.where(forced, POS_BIG, jnp.where(admitted, imp, NEG))
        groups8 = [score[v * 8:(v + 1) * 8] for v in range(nblk // 8)]
        ranks = [jnp.zeros((8, TQ), i32) for _ in groups8]
        for kk in range(nblk):
            ck = score[kk:kk + 1, :]
            for v, sv in enumerate(groups8):
                if v * 8 > kk:
                    inc = jnp.where(ck >= sv, 1, 0)
                elif (v + 1) * 8 <= kk:
                    inc = jnp.where(ck > sv, 1, 0)
                else:
                    inc = jnp.where(blk[v * 8:(v + 1) * 8] > kk, jnp.where(ck >= sv, 1, 0), jnp.where(ck > sv, 1, 0))
                ranks[v] = ranks[v] + inc
        selb_ref[g] = jnp.where(jnp.concatenate(ranks, axis=0) < n_sel, 0.0, NEG)

    per_tile = TQ // SLC_BLOCK

    def mask_bias(j, c):
        rows = [jnp.broadcast_to(selb_ref[c, pl.ds(per_tile * j + t, 1), :], (SLC_BLOCK, TQ))
                for t in range(per_tile)]
        return jnp.concatenate(rows, axis=0)

    qs_ref[HEAD_DIM:K_COLS, :] = qx_ref[...]
    _masked_flash(i, qs_ref,
                  lambda j, c: ks_ref[0, c, pl.ds(pl.multiple_of(j * TQ, TQ), TQ), :],
                  lambda j, c: vst_ref[0, c, j],
                  mask_bias, delta_ref, m_ref, acc_ref, s_ref, mt_ref, al_ref, width, False)
    acc = acc_ref[...]
    o_slc = acc[0:HEAD_DIM] / acc[HEAD_DIM:HEAD_DIM + 1]

    outs = []
    for g in range(groups):
        sig_t = jax.nn.sigmoid(gt_ref[0, g, 0])
        sig_n = jax.nn.sigmoid(gn_ref[0, g])
        for r in range(rep):
            h = g * rep + r
            mixed = (sig_t[r:r + 1] * o_cmps[g][:, r * TQ:(r + 1) * TQ]
                     + sig_t[rep + r:rep + r + 1] * o_slc[:, h * TQ:(h + 1) * TQ])
            o_win = owin_ref[0, :, h * HEAD_DIM:(h + 1) * HEAD_DIM]
            outs.append(mixed.T + sig_n[:, 2 * rep + r:2 * rep + r + 1] * o_win)
    o_ref[0] = jnp.concatenate(outs, axis=1).astype(o_ref.dtype)


def _nsa_mix(qt, kc, vc, bias_c, ks, vs, near, far, gates, o_win):
    bn, g, nt, _, width = qt.shape
    s = nt * TQ
    nc = kc.shape[2]
    rep = width // TQ
    hq = g * rep * HEAD_DIM
    nblk = s // SLC_BLOCK
    n_sel = min(SLC_TOPN, nblk)
    ci = np.arange(nc)[:, None] * CMP_STRIDE
    sj = np.arange(nblk)[None, :] * SLC_BLOCK
    overlap = np.clip(np.minimum(ci + CMP_BLOCK, sj + SLC_BLOCK) - np.maximum(ci, sj), 0, None) / CMP_BLOCK
    overlap[nc - 1] = 0.0
    gates_t = gates.reshape(bn, g, nt, TQ, 3 * rep).transpose(0, 1, 2, 4, 3)
    kern = functools.partial(_nsa_kernel, rep=rep, n_sel=n_sel)
    return pl.pallas_call(
        kern,
        grid=(bn, nt),
        in_specs=[pl.BlockSpec((1, g, 1, HEAD_DIM, width), lambda b, i: (b, 0, i, 0, 0)),
                  pl.BlockSpec((1, g, nc, HEAD_DIM), lambda b, i: (b, 0, 0, 0)),
                  pl.BlockSpec((1, g, HEAD_DIM, nc), lambda b, i: (b, 0, 0, 0)),
                  pl.BlockSpec((g * rep, 1, nc, TQ), lambda b, i: (0, i, 0, 0)),
                  pl.BlockSpec((nblk, nc), lambda b, i: (0, 0)),
                  pl.BlockSpec((1, g, s, K_COLS), lambda b, i: (b, 0, 0, 0)),
                  pl.BlockSpec((1, g, nt, V_ROWS, TQ), lambda b, i: (b, 0, 0, 0, 0)),
                  pl.BlockSpec((K_COLS - HEAD_DIM, g * width), lambda b, i: (0, 0)),
                  pl.BlockSpec((2, TQ, g * width), lambda b, i: (0, 0, 0)),
                  pl.BlockSpec((1, g, 1, 3 * rep, TQ), lambda b, i: (b, 0, i, 0, 0)),
                  pl.BlockSpec((1, g, TQ, 3 * rep), lambda b, i: (b, 0, i, 0)),
                  pl.BlockSpec((1, TQ, hq), lambda b, i: (b, i, 0))],
        out_specs=pl.BlockSpec((1, TQ, hq), lambda b, i: (b, i, 0)),
        out_shape=jax.ShapeDtypeStruct((bn, s, hq), bf16),
        scratch_shapes=[pltpu.VMEM((K_COLS, g * width), bf16), pltpu.VMEM((g, nblk, TQ), f32),
                        pltpu.VMEM((1, g * width), f32), pltpu.VMEM((V_ROWS, g * width), f32),
                        pltpu.VMEM((2, TQ, g * width), f32), pltpu.VMEM((2, 1, g * width), f32),
                        pltpu.VMEM((2, 1, g * width), f32)],
        compiler_params=_cparams(("parallel", "arbitrary")),
        name="nsa_mix",
    )(qt, kc, jnp.swapaxes(vc, 2, 3), bias_c, jnp.asarray(overlap.T, f32), ks, vs,
      *_flash_operands(near, far), gates_t, gates, o_win)


RADIX_BITS_PER_CHECK = 8


def _dsa_kernel(qt_ref, k_ref, vt_ref, qit_ref, ki_ref, wit_ref, qx_ref, delta_ref, tri_ref, o_ref,
                keys_ref, qs_ref, m_ref, acc_ref, s_ref, mt_ref, al_ref, *, topk, chunk):
    i = pl.program_id(1)
    t0 = i * TQ
    width = qt_ref.shape[3]
    nh = width // TQ
    nih = qit_ref.shape[3] // TQ
    krow = lax.broadcasted_iota(i32, (TQ, TQ), 0)
    qcol = lax.broadcasted_iota(i32, (TQ, TQ), 1)
    n_quads = jnp.maximum(i // 4 + 1, -(-topk // (4 * TQ)))

    qit = qit_ref[0, 0]
    wit = wit_ref[0, 0]

    def score_tile(j):
        kt = ki_ref[0, pl.ds(pl.multiple_of(j * TQ, TQ), TQ), :]
        rel = jnp.maximum(jnp.dot(kt, qit, preferred_element_type=f32), 0.0)
        sc = jnp.zeros((TQ, TQ), f32)
        for h in range(nih):
            sc = sc + wit[h:h + 1, :] * rel[:, h * TQ:(h + 1) * TQ]
        sc = jnp.where(j * TQ + krow <= t0 + qcol, sc, NEG)
        keys_ref[j] = jnp.where(sc == 0.0, 0.0, sc)

    def score_body(jj, carry):
        for u in range(4):
            score_tile(4 * jj + u)
        return carry

    lax.fori_loop(0, n_quads, score_body, 0)

    def code_to_float(code):
        code = jnp.clip(code, -0x7F800001, 0x7F800000)
        bits = jnp.where(code < 0, code ^ 0x7FFFFFFF, code)
        bits = jnp.where((bits > 0) & (bits < 0x00800000), 0x00800000, bits)
        return pltpu.bitcast(bits, f32)

    def count(pred_fn):
        def body(jj, cnt):
            for u in range(4):
                cnt = cnt + jnp.where(pred_fn(keys_ref[4 * jj + u]), 1.0, 0.0)
            return cnt
        cnt = lax.fori_loop(0, n_quads, body, jnp.zeros((TQ, TQ), f32))
        return jnp.sum(cnt, axis=0, keepdims=True)

    def bits_body(state):
        t, prefix, n_ge = state
        for u in range(RADIX_BITS_PER_CHECK):
            cand = prefix ^ jnp.left_shift(jnp.int32(1), 31 - (t + u))
            cand_f = code_to_float(cand)
            cnt = count(lambda sj: sj >= cand_f)
            prefix = jnp.where(cnt >= topk, cand, prefix)
            n_ge = jnp.where(cnt >= topk, cnt, n_ge)
        return t + RADIX_BITS_PER_CHECK, prefix, n_ge

    n_all = (4 * n_quads * TQ).astype(f32)
    _, code, n_ge = lax.while_loop(lambda st: (st[0] < 32) & (jnp.max(st[2]) > topk), bits_body,
                                   (jnp.int32(0), jnp.full((1, TQ), INT_MIN, i32), jnp.full((1, TQ), n_all, f32)))
    thr = code_to_float(code)

    @pl.when(jnp.max(n_ge) > topk)
    def _():
        need = topk - count(lambda sj: sj > thr)

        def fix_body(j, running):
            sj = keys_ref[j]
            tie = jnp.where(sj == thr, 1.0, 0.0)
            before = jnp.dot(tri_ref[...], tie.astype(bf16), preferred_element_type=f32) + running
            keys_ref[j] = jnp.where(before >= need, jnp.where(sj == thr, -jnp.inf, sj), sj)
            return running + jnp.sum(tie, axis=0, keepdims=True)

        lax.fori_loop(0, 4 * n_quads, fix_body, jnp.zeros((1, TQ), f32))

    qs_ref[0:HEAD_DIM, :] = (qt_ref[0, 0].astype(f32) * (HEAD_DIM ** -0.5 * LOG2E)).astype(bf16)
    qs_ref[HEAD_DIM:K_COLS, :] = qx_ref[...]
    _masked_flash(i, qs_ref,
                  lambda j, c: k_ref[0, pl.ds(pl.multiple_of(j * TQ, TQ), TQ), :],
                  lambda j, c: vt_ref[0, j],
                  lambda j, c: jnp.where(keys_ref[j] >= thr, 0.0, NEG),
                  delta_ref, m_ref, acc_ref, s_ref, mt_ref, al_ref, chunk, True)
    acc = acc_ref[...]
    ot = acc[0:HEAD_DIM] / acc[HEAD_DIM:HEAD_DIM + 1]
    o_ref[0] = jnp.concatenate([ot[:, h * TQ:(h + 1) * TQ].T for h in range(nh)], axis=1).astype(o_ref.dtype)


def _dsa_attention(qt, k, v, qit, ki, wi, near, far):
    bn, nt, _, width = qt.shape
    s = nt * TQ
    nh = width // TQ
    hq = nh * HEAD_DIM
    nih = qit.shape[3] // TQ
    topk = min(C_TOPK_MAX, s // 4)
    wit = wi.reshape(bn, nt, TQ, nih).transpose(0, 1, 3, 2)
    tri = np.tril(np.ones((TQ, TQ), np.float32), -1)
    kern = functools.partial(_dsa_kernel, topk=topk, chunk=512)
    nt_scr = max(nt, -(-topk // TQ))
    return pl.pallas_call(
        kern,
        grid=(bn, nt),
        in_specs=[pl.BlockSpec((1, 1, HEAD_DIM, width), lambda b, i: (b, i, 0, 0)),
                  pl.BlockSpec((1, s, K_COLS), lambda b, i: (b, 0, 0)),
                  pl.BlockSpec((1, nt, V_ROWS, TQ), lambda b, i: (b, 0, 0, 0)),
                  pl.BlockSpec((1, 1, IDX_DIM, nih * TQ), lambda b, i: (b, i, 0, 0)),
                  pl.BlockSpec((1, s, IDX_DIM), lambda b, i: (b, 0, 0)),
                  pl.BlockSpec((1, 1, nih, TQ), lambda b, i: (b, i, 0, 0)),
                  pl.BlockSpec((K_COLS - HEAD_DIM, width), lambda b, i: (0, 0)),
                  pl.BlockSpec((2, TQ, width), lambda b, i: (0, 0, 0)),
                  pl.BlockSpec((TQ, TQ), lambda b, i: (0, 0))],
        out_specs=pl.BlockSpec((1, TQ, hq), lambda b, i: (b, i, 0)),
        out_shape=jax.ShapeDtypeStruct((bn, s, hq), bf16),
        scratch_shapes=[pltpu.VMEM((nt_scr, TQ, TQ), f32), pltpu.VMEM((K_COLS, width), bf16),
                        pltpu.VMEM((1, width), f32), pltpu.VMEM((V_ROWS, width), f32),
                        pltpu.VMEM((2, TQ, width), f32), pltpu.VMEM((2, 1, width), f32),
                        pltpu.VMEM((2, 1, width), f32)],
        compiler_params=_cparams(("parallel", "arbitrary")),
        name="dsa_attention",
    )(qt, k, v, qit, ki, wit, *_flash_operands(near, far), jnp.asarray(tri, bf16))


def _heads_first(a, g):
    bn, s, _ = a.shape
    return a.reshape(bn, s, g, HEAD_DIM).transpose(0, 2, 1, 3)


def _even_layer(x, table_t, w_in, sinks, cmpk_pos, cmpk_w1, cmpk_w2, cmpv_pos, cmpv_w1, cmpv_w2,
                w_out, ln1_g, ln1_b, ffn_gate, ffn_up, ffn_down, ln2_g, ln2_b):
    bn, s, d = x.shape
    xf = x.reshape(bn * s, d)
    a_q, a_kv, b_q, b_kv = A_HEADS * HEAD_DIM, A_KV * HEAD_DIM, B_HEADS * HEAD_DIM, B_KV * HEAD_DIM
    n_main = a_q + 2 * a_kv + b_q + 6 * b_kv
    rb = B_HEADS // B_KV
    w_tail = jnp.pad(w_in[:, n_main:], ((0, 0), (0, 128 - 3 * B_HEADS))).astype(bf16)
    cols = np.cumsum([0, a_q, a_kv, a_kv, b_q] + [b_kv] * 6)
    w_kv = jnp.concatenate([w_in[:, cols[1]:cols[3]], w_in[:, cols[4]:cols[6]], w_in[:, cols[8]:cols[10]]],
                           axis=1).astype(bf16)
    main, tail, qt_a, qt_b, ks1, vst = _proj_in(
        x, [w_kv, w_tail, w_in[:, cols[0]:cols[1]].T.astype(bf16), w_in[:, cols[3]:cols[4]].T.astype(bf16),
            w_in[:, cols[6]:cols[7]].astype(bf16), w_in[:, cols[7]:cols[8]].T.astype(bf16)],
        [("plain", w_kv.shape[1], bf16, 0), ("plain", 128, f32, 0),
         ("queries", A_KV, A_HEADS // A_KV, HEAD_DIM), ("queries", B_KV, rb, HEAD_DIM),
         ("keys", B_KV, 1, HEAD_DIM), ("values", B_KV, 1, HEAD_DIM)])
    main = main.reshape(bn, s, w_kv.shape[1])
    gates = tail[:, :3 * B_HEADS]
    gates = gates.reshape(bn, s, 3, B_KV, rb).transpose(0, 3, 1, 2, 4).reshape(bn, B_KV, s, 3 * rb)
    kcols = np.cumsum([0, a_kv, a_kv] + [b_kv] * 4)
    ka, va, kc_in, vc_in, kwn, vwn = [main[:, :, kcols[t]:kcols[t + 1]] for t in range(6)]

    near_idx = _bucket_of(np.arange(TQ)[:, None] + TQ - np.arange(2 * TQ)[None, :])
    near = _bias_expand(table_t, near_idx, TQ)
    pad_b = -(-(B_WINDOW - 1) // TQ) * TQ
    win_idx = _bucket_of(np.arange(TQ)[:, None] + pad_b - np.arange(pad_b + TQ)[None, :])
    bias_win = _bias_expand(table_t[A_HEADS:A_HEADS + B_HEADS], win_idx, TQ)
    nc = s // CMP_STRIDE
    nt = s // TQ
    cmp_idx = _bucket_of((np.arange(nt)[:, None, None] * TQ + np.arange(TQ)[None, None, :])
                         - (np.arange(nc)[None, :, None] * CMP_STRIDE + CMP_BLOCK - 1)).reshape(nt * nc, TQ)
    bias_c = _bias_expand(table_t[A_HEADS:A_HEADS + B_HEADS], cmp_idx, min(nt * nc, 512))
    bias_c = bias_c.reshape(B_HEADS, nt, nc, TQ)
    far = table_t[:, NUM_BUCKETS - 1]

    out_a =_banded_gqa(qt_a, _heads_first(ka, A_KV), _heads_first(va, A_KV), near[:A_HEADS], sinks, A_WINDOW, bf16)
    o_win = _banded_gqa(qt_b, _heads_first(kwn, B_KV), _heads_first(vwn, B_KV), bias_win, None, B_WINDOW, f32)
    kc = _nsa_compress(_heads_first(kc_in, B_KV), cmpk_pos, cmpk_w1, cmpk_w2)
    vc = _nsa_compress(_heads_first(vc_in, B_KV), cmpv_pos, cmpv_w1, cmpv_w2)
    out_b = _nsa_mix(qt_b, kc, vc, bias_c, ks1, vst,
                     near[A_HEADS:A_HEADS + B_HEADS], far[A_HEADS:A_HEADS + B_HEADS], gates, o_win)
    mix = jnp.concatenate([out_a, out_b], axis=-1).reshape(bn * s, a_q + b_q)
    x2 = _proj_ffn_res_ln(mix, w_out.astype(bf16), xf, ln1_g, ln1_b, ffn_gate.astype(bf16), ffn_up.astype(bf16),
                          ffn_down.astype(bf16), ln2_g, ln2_b)
    return x2.reshape(bn, s, d), near, far


def _odd_layer(x, near, far, w_in, w_out, ln1_g, ln1_b, router, exp_gate, exp_up, exp_down, ln2_g, ln2_b):
    bn, s, d = x.shape
    xf = x.reshape(bn * s, d)
    c_q = C_HEADS * HEAD_DIM
    n_main = c_q + 2 * HEAD_DIM + IDX_HEADS * IDX_DIM
    w_tail = jnp.pad(w_in[:, n_main:], ((0, 0), (0, 128 - IDX_DIM - IDX_HEADS))).astype(bf16)
    tail, qt, qit, k1, vt = _proj_in(
        x, [w_tail, w_in[:, :c_q].T.astype(bf16), w_in[:, c_q + 2 * HEAD_DIM:n_main].T.astype(bf16),
            w_in[:, c_q:c_q + HEAD_DIM].astype(bf16), w_in[:, c_q + HEAD_DIM:c_q + 2 * HEAD_DIM].T.astype(bf16)],
        [("plain", 128, f32, 0), ("queries", 1, C_HEADS, HEAD_DIM), ("queries", 1, IDX_HEADS, IDX_DIM),
         ("keys", 1, 1, HEAD_DIM), ("values", 1, 1, HEAD_DIM)])
    tail = tail.reshape(bn, s, 128)
    ki = tail[:, :, :IDX_DIM].astype(bf16)
    wi = tail[:, :, IDX_DIM:IDX_DIM + IDX_HEADS]
    mix = _dsa_attention(qt[:, 0], k1[:, 0], vt[:, 0], qit[:, 0], ki, wi, near[:C_HEADS], far[:C_HEADS])
    mix = mix.reshape(bn * s, c_q)
    x1 = _proj_res_ln(mix, w_out.astype(bf16), xf, ln1_g, ln1_b)
    x2 = _moe_res_ln(x1, router, exp_gate.astype(bf16), exp_up.astype(bf16), exp_down.astype(bf16), ln2_g, ln2_b)
    return x2.reshape(bn, s, d)


def kernel(x, rel_bias, l0_w_in, l0_sinks, l0_cmpk_pos, l0_cmpk_w1, l0_cmpk_w2, l0_cmpv_pos, l0_cmpv_w1,
           l0_cmpv_w2, l0_w_out, l0_ln1_g, l0_ln1_b, l0_ffn_gate, l0_ffn_up, l0_ffn_down, l0_ln2_g, l0_ln2_b,
           l1_w_in, l1_w_out, l1_ln1_g, l1_ln1_b, l1_router, l1_exp_gate, l1_exp_up, l1_exp_down, l1_ln2_g,
           l1_ln2_b):
    table_t = rel_bias.T.astype(f32)
    x, near, far = _even_layer(x, table_t, l0_w_in, l0_sinks, l0_cmpk_pos, l0_cmpk_w1, l0_cmpk_w2, l0_cmpv_pos,
                               l0_cmpv_w1, l0_cmpv_w2, l0_w_out, l0_ln1_g, l0_ln1_b, l0_ffn_gate, l0_ffn_up,
                               l0_ffn_down, l0_ln2_g, l0_ln2_b)
    return _odd_layer(x, near, far, l1_w_in, l1_w_out, l1_ln1_g, l1_ln1_b, l1_router, l1_exp_gate, l1_exp_up,
                      l1_exp_down, l1_ln2_g, l1_ln2_b)
```

```python
import functools
import math

import numpy as np
import jax
import jax.numpy as jnp
from jax import lax
from jax.experimental import pallas as pl
from jax.experimental.pallas import tpu as pltpu

f32 = jnp.float32
bf16 = jnp.bfloat16
i32 = jnp.int32

D_MODEL = 1024
HEAD_DIM = 64
NUM_BUCKETS = 32
MAX_DISTANCE = 128
A_HEADS, A_KV, A_WINDOW = 8, 2, 128
B_HEADS, B_KV, B_WINDOW = 8, 2, 512
CMP_BLOCK, CMP_STRIDE = 32, 16
SLC_BLOCK, SLC_TOPN = 64, 16
C_HEADS, IDX_HEADS, IDX_DIM, C_TOPK_MAX = 16, 8, 32, 256
D_FF, N_EXPERTS = 2816, 8
DEPTH = 2
ALPHA = (2.0 * DEPTH) ** 0.25
LN_EPS = 1e-5
NEG = -1e30
POS_BIG = 1e30
INT_MIN = -(2 ** 31)

TQ = 128
FF_CHUNK = 256
VMEM_LIMIT = 56 * 1024 * 1024


def _cparams(sem):
    return pltpu.CompilerParams(dimension_semantics=sem, vmem_limit_bytes=VMEM_LIMIT)


def _bucket_table(max_dist):
    n = np.arange(max_dist + 1)
    max_exact = NUM_BUCKETS // 2
    nf = np.maximum(n, 1).astype(np.float64)
    large = max_exact + (np.log(nf / max_exact) / math.log(MAX_DISTANCE / max_exact)
                         * (NUM_BUCKETS - max_exact)).astype(np.int64)
    large = np.minimum(large, NUM_BUCKETS - 1)
    return np.where(n < max_exact, n, large).astype(np.int32)


def _bucket_of(dist):
    dist = np.maximum(dist, 0)
    return _bucket_table(int(dist.max()))[dist]


def _proj_in_kernel(*refs, groups):
    ng = len(groups)
    x_ref = refs[0]
    wq_refs = refs[1:1 + ng]
    oq_refs = refs[1 + ng:]
    xb = x_ref[...].astype(bf16)
    tm = xb.shape[0]
    for wq_ref, oq_ref, (kind, gb, hb, dh) in zip(wq_refs, oq_refs, groups):
        if kind == "plain":
            oq_ref[...] = jnp.dot(xb, wq_ref[...], preferred_element_type=f32).astype(oq_ref.dtype)
            continue
        if kind == "keys":
            kk = jnp.dot(xb, wq_ref[...], preferred_element_type=f32).astype(bf16)
            ones = jnp.where(lax.broadcasted_iota(i32, (tm, K_COLS - dh), 1) < 2, 1.0, 0.0).astype(bf16)
            for g in range(gb):
                oq_ref[0, g, :, 0:dh] = kk[:, g * dh:(g + 1) * dh]
                oq_ref[0, g, :, dh:K_COLS] = ones
            continue
        qt = lax.dot_general(wq_ref[...], xb, (((1,), (1,)), ((), ())), preferred_element_type=f32).astype(bf16)
        ones = jnp.where(lax.broadcasted_iota(i32, (V_ROWS - HEAD_DIM, TQ), 0) < 1, 1.0, 0.0).astype(bf16)
        for g in range(gb):
            for h in range(hb):
                rows = slice((g * hb + h) * dh, (g * hb + h + 1) * dh)
                for t in range(tm // TQ):
                    oq_ref[0, g, t, 0:dh, h * TQ:(h + 1) * TQ] = qt[rows, t * TQ:(t + 1) * TQ]
                    if kind == "values":
                        oq_ref[0, g, t, dh:V_ROWS, :] = ones


def _proj_in(x3, w_groups, groups, tm=512):
    bn, s, k = x3.shape
    m = bn * s
    tm = min(tm, s)
    per_b = s // tm
    tiles = tm // TQ
    q_specs, q_shapes = [], []
    for kind, gb, hb, dh in groups:
        if kind == "plain":
            q_specs.append(pl.BlockSpec((tm, gb), lambda i: (i, 0)))
            q_shapes.append(jax.ShapeDtypeStruct((m, gb), hb))
        elif kind == "keys":
            q_specs.append(pl.BlockSpec((1, gb, tm, K_COLS), lambda i: (i // per_b, 0, i % per_b, 0)))
            q_shapes.append(jax.ShapeDtypeStruct((bn, gb, s, K_COLS), bf16))
        else:
            rows = V_ROWS if kind == "values" else dh
            q_specs.append(pl.BlockSpec((1, gb, tiles, rows, hb * TQ), lambda i: (i // per_b, 0, i % per_b, 0, 0)))
            q_shapes.append(jax.ShapeDtypeStruct((bn, gb, s // TQ, rows, hb * TQ), bf16))
    return pl.pallas_call(
        functools.partial(_proj_in_kernel, groups=tuple(groups)),
        grid=(m // tm,),
        in_specs=[pl.BlockSpec((tm, k), lambda i: (i, 0))]
                 + [pl.BlockSpec(w.shape, lambda i: (0, 0)) for w in w_groups],
        out_specs=q_specs,
        out_shape=q_shapes,
        compiler_params=_cparams(("parallel",)),
        name="proj_in",
    )(x3.reshape(m, k), *w_groups)


def _bias_expand_kernel(tab_ref, idx_ref, o_ref):
    h = pl.program_id(0)
    idx = idx_ref[...]
    acc = jnp.zeros(idx.shape, f32)
    for b in range(NUM_BUCKETS):
        acc = jnp.where(idx == b, tab_ref[h, b], acc)
    o_ref[0] = acc


def _bias_expand(table_t, idx, tr):
    nh = table_t.shape[0]
    r, c = idx.shape
    return pl.pallas_call(
        _bias_expand_kernel,
        grid=(nh, r // tr),
        in_specs=[pl.BlockSpec(memory_space=pltpu.SMEM),
                  pl.BlockSpec((tr, c), lambda h, i: (i, 0))],
        out_specs=pl.BlockSpec((1, tr, c), lambda h, i: (h, i, 0)),
        out_shape=jax.ShapeDtypeStruct((nh, r, c), f32),
        compiler_params=_cparams(("parallel", "arbitrary")),
        name="bias_expand",
    )(table_t, jnp.asarray(idx, dtype=i32))


def _layer_norm(z, g, b):
    mu = jnp.mean(z, axis=-1, keepdims=True)
    zc = z - mu
    var = jnp.mean(zc * zc, axis=-1, keepdims=True)
    return zc * lax.rsqrt(var + LN_EPS) * g + b


def _proj_ln_kernel(mix_ref, w_ref, x_ref, g_ref, b_ref, o_ref):
    y = jnp.dot(mix_ref[...], w_ref[...], preferred_element_type=f32)
    o_ref[...] = _layer_norm(ALPHA * x_ref[...] + y, g_ref[...], b_ref[...])


def _proj_res_ln(mix, w, x, g, b, tm=512):
    m, k = mix.shape
    d = w.shape[1]
    return pl.pallas_call(
        _proj_ln_kernel,
        grid=(m // tm,),
        in_specs=[pl.BlockSpec((tm, k), lambda i: (i, 0)),
                  pl.BlockSpec((k, d), lambda i: (0, 0)),
                  pl.BlockSpec((tm, d), lambda i: (i, 0)),
                  pl.BlockSpec((1, d), lambda i: (0, 0)),
                  pl.BlockSpec((1, d), lambda i: (0, 0))],
        out_specs=pl.BlockSpec((tm, d), lambda i: (i, 0)),
        out_shape=jax.ShapeDtypeStruct((m, d), f32),
        compiler_params=_cparams(("parallel",)),
        name="proj_res_ln",
    )(mix, w, x, g.reshape(1, d), b.reshape(1, d))


def _swiglu_accumulate(xb, wg_ref, wu_ref, wd_ref, acc_ref, widx):
    d_ff = wg_ref.shape[-1]
    for c in range(d_ff // FF_CHUNK):
        cols = slice(c * FF_CHUNK, (c + 1) * FF_CHUNK)
        gate = jnp.dot(xb, wg_ref[widx + (slice(None), cols)], preferred_element_type=f32)
        up = jnp.dot(xb, wu_ref[widx + (slice(None), cols)], preferred_element_type=f32)
        h = jax.nn.silu(gate) * up
        acc_ref[...] += jnp.dot(h.astype(bf16), wd_ref[widx + (cols, slice(None))], preferred_element_type=f32)


def _proj_ffn_kernel(mix_ref, wo_ref, x_ref, g1_ref, b1_ref, wg_ref, wu_ref, wd_ref, g2_ref, b2_ref, o_ref, acc_ref):
    y = jnp.dot(mix_ref[...], wo_ref[...], preferred_element_type=f32)
    o_ref[...] = _layer_norm(ALPHA * x_ref[...] + y, g1_ref[...], b1_ref[...])
    acc_ref[...] = jnp.zeros_like(acc_ref)
    _swiglu_accumulate(o_ref[...].astype(bf16), wg_ref, wu_ref, wd_ref, acc_ref, ())
    o_ref[...] = _layer_norm(ALPHA * o_ref[...] + acc_ref[...], g2_ref[...], b2_ref[...])


def _proj_ffn_res_ln(mix, w_out, x, g1, b1, wg, wu, wd, g2, b2, tm=512):
    m, k = mix.shape
    d = x.shape[1]
    d_ff = wg.shape[1]
    single = pl.Buffered(1)
    row = lambda i: (i, 0)
    fixed = lambda i: (0, 0)
    return pl.pallas_call(
        _proj_ffn_kernel,
        grid=(m // tm,),
        in_specs=[pl.BlockSpec((tm, k), row),
                  pl.BlockSpec((k, d), fixed, pipeline_mode=single),
                  pl.BlockSpec((tm, d), row),
                  pl.BlockSpec((1, d), fixed),
                  pl.BlockSpec((1, d), fixed),
                  pl.BlockSpec((d, d_ff), fixed, pipeline_mode=single),
                  pl.BlockSpec((d, d_ff), fixed, pipeline_mode=single),
                  pl.BlockSpec((d_ff, d), fixed, pipeline_mode=single),
                  pl.BlockSpec((1, d), fixed),
                  pl.BlockSpec((1, d), fixed)],
        out_specs=pl.BlockSpec((tm, d), row),
        out_shape=jax.ShapeDtypeStruct((m, d), f32),
        scratch_shapes=[pltpu.VMEM((tm, d), f32)],
        compiler_params=_cparams(("parallel",)),
        name="proj_ffn_res_ln",
    )(mix, w_out, x, g1.reshape(1, d), b1.reshape(1, d), wg, wu, wd, g2.reshape(1, d), b2.reshape(1, d))


MOE_CAP = 288


def _router_gates(x, r_ref, lane):
    logits = jnp.dot(x, r_ref[...], preferred_element_type=f32)
    logits = jnp.where(lane < N_EXPERTS, logits, -jnp.inf)
    m1 = jnp.max(logits, axis=-1, keepdims=True)
    i1 = jnp.min(jnp.where(logits == m1, lane, 128), axis=-1, keepdims=True)
    rest = jnp.where(lane == i1, -jnp.inf, logits)
    m2 = jnp.max(rest, axis=-1, keepdims=True)
    i2 = jnp.min(jnp.where(rest == m2, lane, 128), axis=-1, keepdims=True)
    e2 = jnp.exp(m2 - m1)
    g1 = 1.0 / (1.0 + e2)
    return jnp.where(lane == i1, g1, 0.0) + jnp.where(lane == i2, e2 * g1, 0.0)


def _moe_kernel(x_ref, r_ref, tri_ref, wg_ref, wu_ref, wd_ref, g_ref, b_ref, o_ref,
                split_ref, post_ref, yc_ref):
    e = pl.program_id(1)
    tm = x_ref.shape[0]

    @pl.when(e == 0)
    def _():
        comb = _router_gates(x_ref[...], r_ref, lax.broadcasted_iota(i32, (tm, 128), 1))
        hi = comb.astype(bf16)
        rest = comb - hi.astype(f32)
        mid = rest.astype(bf16)
        split_ref[0] = hi
        split_ref[1] = mid
        split_ref[2] = (rest - mid.astype(f32)).astype(bf16)
        chosen = jnp.where(comb > 0.0, 1.0, 0.0)
        pos = jnp.dot(tri_ref[...], chosen.astype(bf16), preferred_element_type=f32)
        pos = jnp.where(comb > 0.0, pos, -1.0)
        for t in range(tm // 128):
            rows = slice(t * 128, (t + 1) * 128)
            post_ref[:, rows] = pos[rows].T[0:N_EXPERTS]
        hi32 = jnp.floor((pos + 1.0) * (1.0 / 32.0))
        split_ref[3] = hi32.astype(bf16)
        split_ref[4] = (pos + 1.0 - 32.0 * hi32).astype(bf16)
        o_ref[...] = jnp.zeros_like(o_ref)

    pos_row = post_ref[pl.ds(e, 1), :].astype(i32)
    n_tok = jnp.sum(jnp.where(pos_row >= 0, 1.0, 0.0)).astype(i32)
    pick = jnp.where(lax.broadcasted_iota(i32, (128, 128), 0) == e, 1.0, 0.0).astype(bf16)

    def column(t):
        return jnp.dot(split_ref[t], pick, preferred_element_type=f32)

    gate_b = column(0) + column(1) + column(2)
    pos_b = (32.0 * column(3) + column(4) - 1.0).astype(i32)
    lane = lax.broadcasted_iota(i32, (tm, 128), 1)

    def pass_body(p, carry):
        base = p * MOE_CAP
        gather = jnp.where(pos_row == base + lax.broadcasted_iota(i32, (MOE_CAP, tm), 0), 1.0, 0.0).astype(bf16)
        xc = jnp.dot(gather, x_ref[...].astype(bf16), preferred_element_type=f32).astype(bf16)
        yc_ref[...] = jnp.zeros_like(yc_ref)
        _swiglu_accumulate(xc, wg_ref, wu_ref, wd_ref, yc_ref, (0,))
        scatter = jnp.concatenate([jnp.where(pos_b == base + t * 128 + lane, 1.0, 0.0).astype(bf16)
                                   for t in range(-(-MOE_CAP // 128))], axis=1)[:, :MOE_CAP]
        y = jnp.dot(scatter, yc_ref[...].astype(bf16), preferred_element_type=f32)
        o_ref[...] += jnp.concatenate([gate_b] * (y.shape[1] // 128), axis=1) * y
        return carry

    lax.fori_loop(0, (n_tok + MOE_CAP - 1) // MOE_CAP, pass_body, 0)

    @pl.when(e == N_EXPERTS - 1)
    def _():
        o_ref[...] = _layer_norm(ALPHA * x_ref[...] + o_ref[...], g_ref[...], b_ref[...])


def _moe_res_ln(x, router, wg, wu, wd, g, b, tm=1024):
    m, d = x.shape
    ne, _, d_ff = wg.shape
    tm = min(tm, m)
    router_p = jnp.pad(router, ((0, 0), (0, 128 - ne)))
    tri = jnp.asarray(np.tril(np.ones((tm, tm), np.float32), -1), bf16)
    single = pl.Buffered(1)
    return pl.pallas_call(
        _moe_kernel,
        grid=(m // tm, ne),
        in_specs=[pl.BlockSpec((tm, d), lambda i, e: (i, 0), pipeline_mode=single),
                  pl.BlockSpec((d, 128), lambda i, e: (0, 0), pipeline_mode=single),
                  pl.BlockSpec((tm, tm), lambda i, e: (0, 0), pipeline_mode=single),
                  pl.BlockSpec((1, d, d_ff), lambda i, e: (e, 0, 0)),
                  pl.BlockSpec((1, d, d_ff), lambda i, e: (e, 0, 0)),
                  pl.BlockSpec((1, d_ff, d), lambda i, e: (e, 0, 0)),
                  pl.BlockSpec((1, d), lambda i, e: (0, 0)),
                  pl.BlockSpec((1, d), lambda i, e: (0, 0))],
        out_specs=pl.BlockSpec((tm, d), lambda i, e: (i, 0)),
        out_shape=jax.ShapeDtypeStruct((m, d), f32),
        scratch_shapes=[pltpu.VMEM((5, tm, 128), bf16), pltpu.VMEM((N_EXPERTS, tm), f32),
                        pltpu.VMEM((MOE_CAP, d), f32)],
        compiler_params=_cparams(("parallel", "arbitrary")),
        name="moe_res_ln",
    )(x, router_p, tri, wg, wu, wd, g.reshape(1, d), b.reshape(1, d))


def _compress_kernel(blk_ref, pos_ref, w1_ref, w2_ref, o_ref):
    xb = (blk_ref[0].astype(f32) + pos_ref[...]).astype(bf16)
    h = jax.nn.gelu(jnp.dot(xb, w1_ref[...], preferred_element_type=f32))
    o_ref[0] = jnp.dot(h.astype(bf16), w2_ref[...], preferred_element_type=f32).astype(o_ref.dtype)


def _nsa_compress(kv, pos, w1, w2):
    bn, g, s, dh = kv.shape
    nc = s // CMP_STRIDE
    chunks = kv.reshape(bn, g, nc, CMP_STRIDE * dh)
    nxt = jnp.pad(chunks[:, :, 1:], ((0, 0), (0, 0), (0, 1), (0, 0)))
    blocks = jnp.concatenate([chunks, nxt], axis=-1).reshape(bn * g, nc, CMP_BLOCK * dh)
    cin = CMP_BLOCK * dh
    hid = w1.shape[1]
    out = pl.pallas_call(
        _compress_kernel,
        grid=(bn * g,),
        in_specs=[pl.BlockSpec((1, nc, cin), lambda i: (i, 0, 0)),
                  pl.BlockSpec((1, cin), lambda i: (0, 0)),
                  pl.BlockSpec((cin, hid), lambda i: (0, 0)),
                  pl.BlockSpec((hid, dh), lambda i: (0, 0))],
        out_specs=pl.BlockSpec((1, nc, dh), lambda i: (i, 0, 0)),
        out_shape=jax.ShapeDtypeStruct((bn * g, nc, dh), bf16),
        compiler_params=_cparams(("parallel",)),
        name="nsa_compress",
    )(blocks, pos.reshape(1, cin), w1.astype(bf16), w2.astype(bf16))
    return out.reshape(bn, g, nc, dh)


V_ROWS = HEAD_DIM + 16


LOG2E = 1.4426950408889634
K_COLS = 2 * HEAD_DIM


def _masked_flash(i, qs_ref, key_tile, value_tile, mask_bias, delta_ref, m_ref, acc_ref, s_ref, mt_ref, al_ref, chunk,
                  shared_keys):
    width = qs_ref.shape[1]
    n_chunks = width // chunk
    m_ref[...] = jnp.full(m_ref.shape, NEG, f32)
    acc_ref[...] = jnp.zeros(acc_ref.shape, f32)

    def scores_chunk(j, delta, c, shared, slot=None):
        slot = j % 2 if slot is None else slot
        if c == 0 or not shared_keys:
            shared["kt"] = key_tile(j, c)
            shared["maskb"] = jnp.concatenate([mask_bias(j, c)] * (chunk // TQ), axis=1)
        cols = slice(c * chunk, (c + 1) * chunk)
        s = jnp.dot(shared["kt"], qs_ref[:, cols], preferred_element_type=f32) + shared["maskb"]
        if delta is not None:
            s = s + delta_ref[delta, :, cols]
        s_ref[slot, :, cols] = s
        m_old = m_ref[:, cols]
        m_new = jnp.maximum(m_old, jnp.max(s, axis=0, keepdims=True))
        m_ref[:, cols] = m_new
        mt_ref[slot, :, cols] = m_new
        al_ref[slot, :, cols] = jnp.exp2(m_old - m_new)

    def values_chunk(j, c, shared, slot=None):
        slot = j % 2 if slot is None else slot
        if c == 0 or not shared_keys:
            shared["vt"] = value_tile(j, c)
        cols = slice(c * chunk, (c + 1) * chunk)
        p = jnp.exp2(s_ref[slot, :, cols] - mt_ref[slot, :, cols])
        acc_ref[:, cols] = (al_ref[slot, :, cols] * acc_ref[:, cols]
                            + jnp.dot(shared["vt"], p.astype(bf16), preferred_element_type=f32))

    def scores(j, delta):
        shared = {}
        for c in range(n_chunks):
            scores_chunk(j, delta, c, shared)

    def values(j):
        shared = {}
        for c in range(n_chunks):
            values_chunk(j, c, shared)

    def values_then_scores(jv, js, delta, slot_v=None, slot_s=None):
        sv, ss = {}, {}
        for c in range(n_chunks):
            values_chunk(jv, c, sv, slot_v)
            scores_chunk(js, delta, c, ss, slot_s)

    @pl.when(i >= 2)
    def _():
        scores(0, None)

        def quad_body(k, carry):
            j = 4 * k + 1
            values_then_scores(j - 1, j, None, 0, 1)
            values_then_scores(j, j + 1, None, 1, 0)
            values_then_scores(j + 1, j + 2, None, 0, 1)
            values_then_scores(j + 2, j + 3, None, 1, 0)
            return carry

        n_quads = (i - 2) // 4
        lax.fori_loop(0, n_quads, quad_body, 0)

        def single_body(j, carry):
            values_then_scores(j - 1, j, None)
            return carry

        lax.fori_loop(4 * n_quads + 1, i - 1, single_body, 0)

        values_then_scores(i - 2, i - 1, 0)
        values_then_scores(i - 1, i, 1)

    @pl.when(i == 1)
    def _():
        scores(0, 0)
        values_then_scores(0, 1, 1)

    @pl.when(i == 0)
    def _():
        scores(0, 1)

    values(i)


def _flash_operands(near, far):
    nh = near.shape[0]
    width = nh * TQ
    far2 = jnp.repeat(far * LOG2E, TQ).reshape(1, width)
    hi = far2.astype(bf16)
    lo = (far2 - hi.astype(f32)).astype(bf16)
    extra = jnp.concatenate([hi, lo, jnp.zeros((K_COLS - HEAD_DIM - 2, width), bf16)], axis=0)
    neart = near.reshape(nh, TQ, 2, TQ).transpose(2, 3, 0, 1).reshape(2, TQ, width) * LOG2E
    neart = neart - (hi.astype(f32) + lo.astype(f32))
    causal = np.where(np.arange(TQ)[:, None] <= np.arange(TQ)[None, :], 0.0, NEG).astype(np.float32)
    return extra, jnp.stack([neart[0], neart[1] + jnp.asarray(np.tile(causal, (1, nh)))], axis=0)


BANDED_TILES = 4


def _banded_kernel(qt_ref, k_ref, vt_ref, bias_ref, sink_ref, o_ref, *, rep, kt_tiles, use_sinks):
    kw = kt_tiles * TQ
    zeros = jnp.zeros((K_COLS - HEAD_DIM, qt_ref.shape[4]), bf16)
    for u in range(qt_ref.shape[2]):
        i = pl.program_id(1) * qt_ref.shape[2] + u
        first = jnp.maximum(i - (kt_tiles - 1), 0)
        shift = pl.multiple_of((first - (i - (kt_tiles - 1))) * TQ, TQ)
        outs = []
        for g in range(qt_ref.shape[1]):
            qs = jnp.concatenate([(qt_ref[0, g, u].astype(f32) * (HEAD_DIM ** -0.5)).astype(bf16), zeros], axis=0)
            k = k_ref[0, g, pl.ds(pl.multiple_of(first * TQ, TQ), kw), :]
            s = jnp.dot(k, qs, preferred_element_type=f32) + bias_ref[g, pl.ds(shift, kw), :]
            m = jnp.max(s, axis=0, keepdims=True)
            if use_sinks:
                m = jnp.maximum(m, sink_ref[g])
            p = jnp.exp(s - m).astype(bf16)
            vt = jnp.concatenate([vt_ref[0, g, first + t] for t in range(kt_tiles)], axis=1)
            acc = jnp.dot(vt, p, preferred_element_type=f32)
            l = acc[HEAD_DIM:HEAD_DIM + 1]
            if use_sinks:
                l = l + jnp.exp(sink_ref[g] - m)
            ot = acc[0:HEAD_DIM] / l
            outs += [ot[:, r * TQ:(r + 1) * TQ].T for r in range(rep)]
        o_ref[0, u * TQ:(u + 1) * TQ, :] = jnp.concatenate(outs, axis=1).astype(o_ref.dtype)


def _banded_gqa(qt, k, vt, bias, sinks, window, out_dtype):
    bn, g, nt, _, width = qt.shape
    rep = width // TQ
    s = nt * TQ
    pad = -(-(window - 1) // TQ) * TQ
    kw = pad + TQ
    dist = np.arange(TQ)[:, None] + pad - np.arange(kw)[None, :]
    band = jnp.asarray((dist >= 0) & (dist < window))
    bias_t = jnp.where(band, bias, NEG).reshape(g, rep, TQ, kw).transpose(0, 3, 1, 2).reshape(g, kw, width)
    bias_t = jnp.concatenate([bias_t, jnp.full((g, pad, width), NEG, f32)], axis=1)
    use_sinks = sinks is not None
    sink_rows = (jnp.repeat(sinks.astype(f32), TQ) if use_sinks else jnp.zeros((g * width,), f32)).reshape(g, 1, width)
    kern = functools.partial(_banded_kernel, rep=rep, kt_tiles=kw // TQ, use_sinks=use_sinks)
    rw = rep * HEAD_DIM
    return pl.pallas_call(
        kern,
        grid=(bn, nt // BANDED_TILES),
        in_specs=[pl.BlockSpec((1, g, BANDED_TILES, HEAD_DIM, width), lambda b, i: (b, 0, i, 0, 0)),
                  pl.BlockSpec((1, g, s, K_COLS), lambda b, i: (b, 0, 0, 0)),
                  pl.BlockSpec((1, g, nt, V_ROWS, TQ), lambda b, i: (b, 0, 0, 0, 0)),
                  pl.BlockSpec((g, kw + pad, width), lambda b, i: (0, 0, 0)),
                  pl.BlockSpec((g, 1, width), lambda b, i: (0, 0, 0))],
        out_specs=pl.BlockSpec((1, BANDED_TILES * TQ, g * rw), lambda b, i: (b, i, 0)),
        out_shape=jax.ShapeDtypeStruct((bn, s, g * rw), out_dtype),
        compiler_params=_cparams(("parallel", "arbitrary")),
        name="banded_gqa_w%d" % window,
    )(qt, k, vt, bias_t, sink_rows)


def _nsa_kernel(qt_ref, kc_ref, vct_ref, biasc_ref, ovt_ref, ks_ref, vst_ref, qx_ref, delta_ref, gt_ref, gn_ref,
                owin_ref, o_ref, qs_ref, selb_ref, m_ref, acc_ref, s_ref, mt_ref, al_ref, *, rep, n_sel):
    i = pl.program_id(1)
    t0 = i * TQ
    groups = qt_ref.shape[1]
    nc = kc_ref.shape[2]
    nblk = ovt_ref.shape[0]
    width = rep * TQ
    cmp_end = lax.broadcasted_iota(i32, (nc, TQ), 0) * CMP_STRIDE + (CMP_BLOCK - 1)
    valid_c = t0 + lax.broadcasted_iota(i32, (nc, TQ), 1) >= cmp_end
    blk = lax.broadcasted_iota(i32, (nblk, TQ), 0)
    tq = t0 + lax.broadcasted_iota(i32, (nblk, TQ), 1)
    cur = tq // SLC_BLOCK
    forced = (blk == 0) | (blk == cur) | (blk == cur - 1)
    admitted = blk * SLC_BLOCK <= tq

    o_cmps = []
    for g in range(groups):
        qf = qt_ref[0, g, 0].astype(f32)
        qs = (qf * (HEAD_DIM ** -0.5)).astype(bf16)
        qs_ref[0:HEAD_DIM, g * width:(g + 1) * width] = (qf * (HEAD_DIM ** -0.5 * LOG2E)).astype(bf16)

        s_c = jnp.dot(kc_ref[0, g], qs, preferred_element_type=f32)
        p_cols = []
        p_sum = jnp.zeros((nc, TQ), f32)
        for r in range(rep):
            sr = jnp.where(valid_c, s_c[:, r * TQ:(r + 1) * TQ] + biasc_ref[g * rep + r, 0], NEG)
            m = jnp.max(sr, axis=0, keepdims=True)
            e = jnp.where(valid_c, jnp.exp(sr - m), 0.0)
            l = jnp.sum(e, axis=0, keepdims=True)
            p = e * (1.0 / jnp.maximum(l, 1e-30))
            p_cols.append(p.astype(bf16))
            p_sum = p_sum + p
        o_cmps.append(jnp.dot(vct_ref[0, g], jnp.concatenate(p_cols, axis=1), preferred_element_type=f32))

        imp = jnp.dot(ovt_ref[...], p_sum, preferred_element_type=f32)
        score = jnp.where(forced, POS_BIG, jnp.where(admitted, imp, NEG))
        groups8 = [score[v * 8:(v + 1) * 8] for v in range(nblk // 8)]
        ranks = [jnp.zeros((8, TQ), i32) for _ in groups8]
        for kk in range(nblk):
            ck = score[kk:kk + 1, :]
            for v, sv in enumerate(groups8):
                if v * 8 > kk:
                    inc = jnp.where(ck >= sv, 1, 0)
                elif (v + 1) * 8 <= kk:
                    inc = jnp.where(ck > sv, 1, 0)
                else:
                    inc = jnp.where(blk[v * 8:(v + 1) * 8] > kk, jnp.where(ck >= sv, 1, 0), jnp.where(ck > sv, 1, 0))
                ranks[v] = ranks[v] + inc
        selb_ref[g] = jnp.where(jnp.concatenate(ranks, axis=0) < n_sel, 0.0, NEG)

    per_tile = TQ // SLC_BLOCK

    def mask_bias(j, c):
        rows = [jnp.broadcast_to(selb_ref[c, pl.ds(per_tile * j + t, 1), :], (SLC_BLOCK, TQ))
                for t in range(per_tile)]
        return jnp.concatenate(rows, axis=0)

    qs_ref[HEAD_DIM:K_COLS, :] = qx_ref[...]
    _masked_flash(i, qs_ref,
                  lambda j, c: ks_ref[0, c, pl.ds(pl.multiple_of(j * TQ, TQ), TQ), :],
                  lambda j, c: vst_ref[0, c, j],
                  mask_bias, delta_ref, m_ref, acc_ref, s_ref, mt_ref, al_ref, width, False)
    acc = acc_ref[...]
    o_slc = acc[0:HEAD_DIM] / acc[HEAD_DIM:HEAD_DIM + 1]

    outs = []
    for g in range(groups):
        sig_t = jax.nn.sigmoid(gt_ref[0, g, 0])
        sig_n = jax.nn.sigmoid(gn_ref[0, g])
        for r in range(rep):
            h = g * rep + r
            mixed = (sig_t[r:r + 1] * o_cmps[g][:, r * TQ:(r + 1) * TQ]
                     + sig_t[rep + r:rep + r + 1] * o_slc[:, h * TQ:(h + 1) * TQ])
            o_win = owin_ref[0, :, h * HEAD_DIM:(h + 1) * HEAD_DIM]
            outs.append(mixed.T + sig_n[:, 2 * rep + r:2 * rep + r + 1] * o_win)
    o_ref[0] = jnp.concatenate(outs, axis=1).astype(o_ref.dtype)


def _nsa_mix(qt, kc, vc, bias_c, ks, vs, near, far, gates, o_win):
    bn, g, nt, _, width = qt.shape
    s = nt * TQ
    nc = kc.shape[2]
    rep = width // TQ
    hq = g * rep * HEAD_DIM
    nblk = s // SLC_BLOCK
    n_sel = min(SLC_TOPN, nblk)
    ci = np.arange(nc)[:, None] * CMP_STRIDE
    sj = np.arange(nblk)[None, :] * SLC_BLOCK
    overlap = np.clip(np.minimum(ci + CMP_BLOCK, sj + SLC_BLOCK) - np.maximum(ci, sj), 0, None) / CMP_BLOCK
    overlap[nc - 1] = 0.0
    gates_t = gates.reshape(bn, g, nt, TQ, 3 * rep).transpose(0, 1, 2, 4, 3)
    kern = functools.partial(_nsa_kernel, rep=rep, n_sel=n_sel)
    return pl.pallas_call(
        kern,
        grid=(bn, nt),
        in_specs=[pl.BlockSpec((1, g, 1, HEAD_DIM, width), lambda b, i: (b, 0, i, 0, 0)),
                  pl.BlockSpec((1, g, nc, HEAD_DIM), lambda b, i: (b, 0, 0, 0)),
                  pl.BlockSpec((1, g, HEAD_DIM, nc), lambda b, i: (b, 0, 0, 0)),
                  pl.BlockSpec((g * rep, 1, nc, TQ), lambda b, i: (0, i, 0, 0)),
                  pl.BlockSpec((nblk, nc), lambda b, i: (0, 0)),
                  pl.BlockSpec((1, g, s, K_COLS), lambda b, i: (b, 0, 0, 0)),
                  pl.BlockSpec((1, g, nt, V_ROWS, TQ), lambda b, i: (b, 0, 0, 0, 0)),
                  pl.BlockSpec((K_COLS - HEAD_DIM, g * width), lambda b, i: (0, 0)),
                  pl.BlockSpec((2, TQ, g * width), lambda b, i: (0, 0, 0)),
                  pl.BlockSpec((1, g, 1, 3 * rep, TQ), lambda b, i: (b, 0, i, 0, 0)),
                  pl.BlockSpec((1, g, TQ, 3 * rep), lambda b, i: (b, 0, i, 0)),
                  pl.BlockSpec((1, TQ, hq), lambda b, i: (b, i, 0))],
        out_specs=pl.BlockSpec((1, TQ, hq), lambda b, i: (b, i, 0)),
        out_shape=jax.ShapeDtypeStruct((bn, s, hq), bf16),
        scratch_shapes=[pltpu.VMEM((K_COLS, g * width), bf16), pltpu.VMEM((g, nblk, TQ), f32),
                        pltpu.VMEM((1, g * width), f32), pltpu.VMEM((V_ROWS, g * width), f32),
                        pltpu.VMEM((2, TQ, g * width), f32), pltpu.VMEM((2, 1, g * width), f32),
                        pltpu.VMEM((2, 1, g * width), f32)],
        compiler_params=_cparams(("parallel", "arbitrary")),
        name="nsa_mix",
    )(qt, kc, jnp.swapaxes(vc, 2, 3), bias_c, jnp.asarray(overlap.T, f32), ks, vs,
      *_flash_operands(near, far), gates_t, gates, o_win)


RADIX_BITS_PER_CHECK = 8


def _dsa_kernel(qt_ref, k_ref, vt_ref, qit_ref, ki_ref, wit_ref, qx_ref, delta_ref, tri_ref, o_ref,
                keys_ref, qs_ref, m_ref, acc_ref, s_ref, mt_ref, al_ref, *, topk, chunk):
    i = pl.program_id(1)
    t0 = i * TQ
    width = qt_ref.shape[3]
    nh = width // TQ
    nih = qit_ref.shape[3] // TQ
    krow = lax.broadcasted_iota(i32, (TQ, TQ), 0)
    qcol = lax.broadcasted_iota(i32, (TQ, TQ), 1)
    n_quads = jnp.maximum(i // 4 + 1, -(-topk // (4 * TQ)))

    qit = qit_ref[0, 0]
    wit = wit_ref[0, 0]

    def score_tile(j):
        kt = ki_ref[0, pl.ds(pl.multiple_of(j * TQ, TQ), TQ), :]
        rel = jnp.maximum(jnp.dot(kt, qit, preferred_element_type=f32), 0.0)
        sc = jnp.zeros((TQ, TQ), f32)
        for h in range(nih):
            sc = sc + wit[h:h + 1, :] * rel[:, h * TQ:(h + 1) * TQ]
        sc = jnp.where(j * TQ + krow <= t0 + qcol, sc, NEG)
        keys_ref[j] = jnp.where(sc == 0.0, 0.0, sc)

    def score_body(jj, carry):
        for u in range(4):
            score_tile(4 * jj + u)
        return carry

    lax.fori_loop(0, n_quads, score_body, 0)

    def code_to_float(code):
        code = jnp.clip(code, -0x7F800001, 0x7F800000)
        bits = jnp.where(code < 0, code ^ 0x7FFFFFFF, code)
        bits = jnp.where((bits > 0) & (bits < 0x00800000), 0x00800000, bits)
        return pltpu.bitcast(bits, f32)

    def count(pred_fn):
        def body(jj, cnt):
            for u in range(4):
                cnt = cnt + jnp.where(pred_fn(keys_ref[4 * jj + u]), 1.0, 0.0)
            return cnt
        cnt = lax.fori_loop(0, n_quads, body, jnp.zeros((TQ, TQ), f32))
        return jnp.sum(cnt, axis=0, keepdims=True)

    def bits_body(state):
        t, prefix, n_ge = state
        for u in range(RADIX_BITS_PER_CHECK):
            cand = prefix ^ jnp.left_shift(jnp.int32(1), 31 - (t + u))
            cand_f = code_to_float(cand)
            cnt = count(lambda sj: sj >= cand_f)
            prefix = jnp.where(cnt >= topk, cand, prefix)
            n_ge = jnp.where(cnt >= topk, cnt, n_ge)
        return t + RADIX_BITS_PER_CHECK, prefix, n_ge

    n_all = (4 * n_quads * TQ).astype(f32)
    _, code, n_ge = lax.while_loop(lambda st: (st[0] < 32) & (jnp.max(st[2]) > topk), bits_body,
                                   (jnp.int32(0), jnp.full((1, TQ), INT_MIN, i32), jnp.full((1, TQ), n_all, f32)))
    thr = code_to_float(code)

    @pl.when(jnp.max(n_ge) > topk)
    def _():
        need = topk - count(lambda sj: sj > thr)

        def fix_body(j, running):
            sj = keys_ref[j]
            tie = jnp.where(sj == thr, 1.0, 0.0)
            before = jnp.dot(tri_ref[...], tie.astype(bf16), preferred_element_type=f32) + running
            keys_ref[j] = jnp.where(before >= need, jnp.where(sj == thr, -jnp.inf, sj), sj)
            return running + jnp.sum(tie, axis=0, keepdims=True)

        lax.fori_loop(0, 4 * n_quads, fix_body, jnp.zeros((1, TQ), f32))

    qs_ref[0:HEAD_DIM, :] = (qt_ref[0, 0].astype(f32) * (HEAD_DIM ** -0.5 * LOG2E)).astype(bf16)
    qs_ref[HEAD_DIM:K_COLS, :] = qx_ref[...]
    _masked_flash(i, qs_ref,
                  lambda j, c: k_ref[0, pl.ds(pl.multiple_of(j * TQ, TQ), TQ), :],
                  lambda j, c: vt_ref[0, j],
                  lambda j, c: jnp.where(keys_ref[j] >= thr, 0.0, NEG),
                  delta_ref, m_ref, acc_ref, s_ref, mt_ref, al_ref, chunk, True)
    acc = acc_ref[...]
    ot = acc[0:HEAD_DIM] / acc[HEAD_DIM:HEAD_DIM + 1]
    o_ref[0] = jnp.concatenate([ot[:, h * TQ:(h + 1) * TQ].T for h in range(nh)], axis=1).astype(o_ref.dtype)


def _dsa_attention(qt, k, v, qit, ki, wi, near, far):
    bn, nt, _, width = qt.shape
    s = nt * TQ
    nh = width // TQ
    hq = nh * HEAD_DIM
    nih = qit.shape[3] // TQ
    topk = min(C_TOPK_MAX, s // 4)
    wit = wi.reshape(bn, nt, TQ, nih).transpose(0, 1, 3, 2)
    tri = np.tril(np.ones((TQ, TQ), np.float32), -1)
    kern = functools.partial(_dsa_kernel, topk=topk, chunk=512)
    nt_scr = max(nt, -(-topk // TQ))
    return pl.pallas_call(
        kern,
        grid=(bn, nt),
        in_specs=[pl.BlockSpec((1, 1, HEAD_DIM, width), lambda b, i: (b, i, 0, 0)),
                  pl.BlockSpec((1, s, K_COLS), lambda b, i: (b, 0, 0)),
                  pl.BlockSpec((1, nt, V_ROWS, TQ), lambda b, i: (b, 0, 0, 0)),
                  pl.BlockSpec((1, 1, IDX_DIM, nih * TQ), lambda b, i: (b, i, 0, 0)),
                  pl.BlockSpec((1, s, IDX_DIM), lambda b, i: (b, 0, 0)),
                  pl.BlockSpec((1, 1, nih, TQ), lambda b, i: (b, i, 0, 0)),
                  pl.BlockSpec((K_COLS - HEAD_DIM, width), lambda b, i: (0, 0)),
                  pl.BlockSpec((2, TQ, width), lambda b, i: (0, 0, 0)),
                  pl.BlockSpec((TQ, TQ), lambda b, i: (0, 0))],
        out_specs=pl.BlockSpec((1, TQ, hq), lambda b, i: (b, i, 0)),
        out_shape=jax.ShapeDtypeStruct((bn, s, hq), bf16),
        scratch_shapes=[pltpu.VMEM((nt_scr, TQ, TQ), f32), pltpu.VMEM((K_COLS, width), bf16),
                        pltpu.VMEM((1, width), f32), pltpu.VMEM((V_ROWS, width), f32),
                        pltpu.VMEM((2, TQ, width), f32), pltpu.VMEM((2, 1, width), f32),
                        pltpu.VMEM((2, 1, width), f32)],
        compiler_params=_cparams(("parallel", "arbitrary")),
        name="dsa_attention",
    )(qt, k, v, qit, ki, wit, *_flash_operands(near, far), jnp.asarray(tri, bf16))


def _heads_first(a, g):
    bn, s, _ = a.shape
    return a.reshape(bn, s, g, HEAD_DIM).transpose(0, 2, 1, 3)


def _even_layer(x, table_t, w_in, sinks, cmpk_pos, cmpk_w1, cmpk_w2, cmpv_pos, cmpv_w1, cmpv_w2,
                w_out, ln1_g, ln1_b, ffn_gate, ffn_up, ffn_down, ln2_g, ln2_b):
    bn, s, d = x.shape
    xf = x.reshape(bn * s, d)
    a_q, a_kv, b_q, b_kv = A_HEADS * HEAD_DIM, A_KV * HEAD_DIM, B_HEADS * HEAD_DIM, B_KV * HEAD_DIM
    n_main = a_q + 2 * a_kv + b_q + 6 * b_kv
    rb = B_HEADS // B_KV
    w_tail = jnp.pad(w_in[:, n_main:], ((0, 0), (0, 128 - 3 * B_HEADS))).astype(bf16)
    cols = np.cumsum([0, a_q, a_kv, a_kv, b_q] + [b_kv] * 6)
    w_kv = w_in[:, cols[4]:cols[6]].astype(bf16)

    def w_cols(t, transposed):
        w = w_in[:, cols[t]:cols[t + 1]]
        return (w.T if transposed else w).astype(bf16)

    main, tail, qt_a, qt_b, ka1, vat, ks1, vst, kw1, vwt = _proj_in(
        x, [w_kv, w_tail, w_cols(0, True), w_cols(3, True), w_cols(1, False), w_cols(2, True),
            w_cols(6, False), w_cols(7, True), w_cols(8, False), w_cols(9, True)],
        [("plain", w_kv.shape[1], bf16, 0), ("plain", 128, f32, 0),
         ("queries", A_KV, A_HEADS // A_KV, HEAD_DIM), ("queries", B_KV, rb, HEAD_DIM),
         ("keys", A_KV, 1, HEAD_DIM), ("values", A_KV, 1, HEAD_DIM),
         ("keys", B_KV, 1, HEAD_DIM), ("values", B_KV, 1, HEAD_DIM),
         ("keys", B_KV, 1, HEAD_DIM), ("values", B_KV, 1, HEAD_DIM)])
    main = main.reshape(bn, s, w_kv.shape[1])
    gates = tail[:, :3 * B_HEADS]
    gates = gates.reshape(bn, s, 3, B_KV, rb).transpose(0, 3, 1, 2, 4).reshape(bn, B_KV, s, 3 * rb)
    kc_in, vc_in = main[:, :, :b_kv], main[:, :, b_kv:]

    near_idx = _bucket_of(np.arange(TQ)[:, None] + TQ - np.arange(2 * TQ)[None, :])
    near = _bias_expand(table_t, near_idx, TQ)
    pad_b = -(-(B_WINDOW - 1) // TQ) * TQ
    win_idx = _bucket_of(np.arange(TQ)[:, None] + pad_b - np.arange(pad_b + TQ)[None, :])
    bias_win = _bias_expand(table_t[A_HEADS:A_HEADS + B_HEADS], win_idx, TQ)
    nc = s // CMP_STRIDE
    nt = s // TQ
    cmp_idx = _bucket_of((np.arange(nt)[:, None, None] * TQ + np.arange(TQ)[None, None, :])
                         - (np.arange(nc)[None, :, None] * CMP_STRIDE + CMP_BLOCK - 1)).reshape(nt * nc, TQ)
    bias_c = _bias_expand(table_t[A_HEADS:A_HEADS + B_HEADS], cmp_idx, min(nt * nc, 512))
    bias_c = bias_c.reshape(B_HEADS, nt, nc, TQ)
    far = table_t[:, NUM_BUCKETS - 1]

    out_a = _banded_gqa(qt_a, ka1, vat, near[:A_HEADS], sinks, A_WINDOW, bf16)
    o_win = _banded_gqa(qt_b, kw1, vwt, bias_win, None, B_WINDOW, f32)
    kc = _nsa_compress(_heads_first(kc_in, B_KV), cmpk_pos, cmpk_w1, cmpk_w2)
    vc = _nsa_compress(_heads_first(vc_in, B_KV), cmpv_pos, cmpv_w1, cmpv_w2)
    out_b = _nsa_mix(qt_b, kc, vc, bias_c, ks1, vst,
                     near[A_HEADS:A_HEADS + B_HEADS], far[A_HEADS:A_HEADS + B_HEADS], gates, o_win)
    mix = jnp.concatenate([out_a, out_b], axis=-1).reshape(bn * s, a_q + b_q)
    x2 = _proj_ffn_res_ln(mix, w_out.astype(bf16), xf, ln1_g, ln1_b, ffn_gate.astype(bf16), ffn_up.astype(bf16),
                          ffn_down.astype(bf16), ln2_g, ln2_b)
    return x2.reshape(bn, s, d), near, far


def _odd_layer(x, near, far, w_in, w_out, ln1_g, ln1_b, router, exp_gate, exp_up, exp_down, ln2_g, ln2_b):
    bn, s, d = x.shape
    xf = x.reshape(bn * s, d)
    c_q = C_HEADS * HEAD_DIM
    n_main = c_q + 2 * HEAD_DIM + IDX_HEADS * IDX_DIM
    w_tail = jnp.pad(w_in[:, n_main:], ((0, 0), (0, 128 - IDX_DIM - IDX_HEADS))).astype(bf16)
    tail, qt, qit, k1, vt = _proj_in(
        x, [w_tail, w_in[:, :c_q].T.astype(bf16), w_in[:, c_q + 2 * HEAD_DIM:n_main].T.astype(bf16),
            w_in[:, c_q:c_q + HEAD_DIM].astype(bf16), w_in[:, c_q + HEAD_DIM:c_q + 2 * HEAD_DIM].T.astype(bf16)],
        [("plain", 128, f32, 0), ("queries", 1, C_HEADS, HEAD_DIM), ("queries", 1, IDX_HEADS, IDX_DIM),
         ("keys", 1, 1, HEAD_DIM), ("values", 1, 1, HEAD_DIM)])
    tail = tail.reshape(bn, s, 128)
    ki = tail[:, :, :IDX_DIM].astype(bf16)
    wi = tail[:, :, IDX_DIM:IDX_DIM + IDX_HEADS]
    mix = _dsa_attention(qt[:, 0], k1[:, 0], vt[:, 0], qit[:, 0], ki, wi, near[:C_HEADS], far[:C_HEADS])
    mix = mix.reshape(bn * s, c_q)
    x1 = _proj_res_ln(mix, w_out.astype(bf16), xf, ln1_g, ln1_b)
    x2 = _moe_res_ln(x1, router, exp_gate.astype(bf16), exp_up.astype(bf16), exp_down.astype(bf16), ln2_g, ln2_b)
    return x2.reshape(bn, s, d)


def kernel(x, rel_bias, l0_w_in, l0_sinks, l0_cmpk_pos, l0_cmpk_w1, l0_cmpk_w2, l0_cmpv_pos, l0_cmpv_w1,
           l0_cmpv_w2, l0_w_out, l0_ln1_g, l0_ln1_b, l0_ffn_gate, l0_ffn_up, l0_ffn_down, l0_ln2_g, l0_ln2_b,
           l1_w_in, l1_w_out, l1_ln1_g, l1_ln1_b, l1_router, l1_exp_gate, l1_exp_up, l1_exp_down, l1_ln2_g,
           l1_ln2_b):
    table_t = rel_bias.T.astype(f32)
    x, near, far = _even_layer(x, table_t, l0_w_in, l0_sinks, l0_cmpk_pos, l0_cmpk_w1, l0_cmpk_w2, l0_cmpv_pos,
                               l0_cmpv_w1, l0_cmpv_w2, l0_w_out, l0_ln1_g, l0_ln1_b, l0_ffn_gate, l0_ffn_up,
                               l0_ffn_down, l0_ln2_g, l0_ln2_b)
    return _odd_layer(x, near, far, l1_w_in, l1_w_out, l1_ln1_g, l1_ln1_b, l1_router, l1_exp_gate, l1_exp_up,
                      l1_exp_down, l1_ln2_g, l1_ln2_b)
```
